```python
import jax, jax.numpy as jnp
from jax import lax
import numpy as np

D_MODEL = 1024
BATCH = 4
SEQ = 8192
DEPTH = 1

GRID_W = 64
CTX_LEN = 256
N_HEADS = 8
N_KV_HEADS = 2
HEAD_DIM = 64
KV_GROUP = N_HEADS // N_KV_HEADS
ATTN_WIDTH = N_HEADS * HEAD_DIM
KV_WIDTH = N_KV_HEADS * HEAD_DIM
WINDOW = 128
BLOCK = 128
ROPE_THETA = 10000.0
POOL_WINDOWS = (2, 4, 8, 16)
N_POOL_GROUPS = 4
POOL_WIDTH = D_MODEL - ATTN_WIDTH
POOL_GROUP_DIM = POOL_WIDTH // N_POOL_GROUPS
MIX_WIDTH = ATTN_WIDTH + POOL_WIDTH
IN_COLS = ATTN_WIDTH + 2 * KV_WIDTH + POOL_WIDTH
N_EXPERTS = 64
TOP_K = 8
N_EXPERT_GROUPS = 8
EXPERTS_PER_GROUP = N_EXPERTS // N_EXPERT_GROUPS
TOPK_GROUPS = 4
D_EXPERT = 256
D_SHARED = 256
ROUTED_SCALE = 2.5
EXPERT_BLOCK = 128
EPS = 1e-6

kernel_name = "hybrid_window_gqa_pool_moe_dit_layer"

F32 = jnp.float32


def rmsnorm(x, g):
    x32 = x.astype(F32)
    y = x32 * lax.rsqrt(jnp.mean(x32 * x32, axis=-1, keepdims=True) + EPS)
    return (y * g.astype(F32)).astype(x.dtype)


def modulate(h, shift, scale):
    return h * (1 + scale) + shift


def axial_rope_tables(n_tokens):
    rows = n_tokens // GRID_W
    row = jnp.repeat(jnp.arange(rows), GRID_W).astype(F32)
    col = jnp.tile(jnp.arange(GRID_W), rows).astype(F32)
    n_freq = HEAD_DIM // 4
    inv_freq = ROPE_THETA ** (-jnp.arange(n_freq, dtype=F32) / n_freq)
    ang_r = row[:, None] * inv_freq[None, :]
    ang_c = col[:, None] * inv_freq[None, :]
    return jnp.cos(ang_r), jnp.sin(ang_r), jnp.cos(ang_c), jnp.sin(ang_c)


def _rope_half(x, cos, sin):
    x1, x2 = jnp.split(x, 2, axis=-1)
    cos = cos[:, None, :]
    sin = sin[:, None, :]
    return jnp.concatenate([x1 * cos - x2 * sin, x1 * sin + x2 * cos], axis=-1)


def apply_axial_rope(x, tables):
    cos_r, sin_r, cos_c, sin_c = tables
    xr, xc = jnp.split(x.astype(F32), 2, axis=-1)
    return jnp.concatenate([_rope_half(xr, cos_r, sin_r), _rope_half(xc, cos_c, sin_c)], axis=-1).astype(x.dtype)


def split_projection(z):
    B, L = z.shape[:2]
    q = z[..., :ATTN_WIDTH].reshape(B, L, N_HEADS, HEAD_DIM)
    k = z[..., ATTN_WIDTH:ATTN_WIDTH + KV_WIDTH].reshape(B, L, N_KV_HEADS, HEAD_DIM)
    v = z[..., ATTN_WIDTH + KV_WIDTH:ATTN_WIDTH + 2 * KV_WIDTH].reshape(B, L, N_KV_HEADS, HEAD_DIM)
    p = z[..., ATTN_WIDTH + 2 * KV_WIDTH:]
    return q, k, v, p


def split_kv(z):
    B, L = z.shape[:2]
    k = z[..., :KV_WIDTH].reshape(B, L, N_KV_HEADS, HEAD_DIM)
    v = z[..., KV_WIDTH:].reshape(B, L, N_KV_HEADS, HEAD_DIM)
    return k, v


def latent_window_attention(q, k, v, k_ctx, v_ctx, sink):
    B, S = q.shape[:2]
    nb = S // BLOCK
    scale = HEAD_DIM ** -0.5
    qb = q.reshape(B, nb, BLOCK, N_KV_HEADS, KV_GROUP, HEAD_DIM)

    def band(t):
        tp = jnp.pad(t, ((0, 0), (BLOCK, BLOCK), (0, 0), (0, 0))).reshape(B, nb + 2, BLOCK, N_KV_HEADS, HEAD_DIM)
        return jnp.concatenate([tp[:, :-2], tp[:, 1:-1], tp[:, 2:]], axis=2)

    kb, vb = band(k), band(v)
    s_loc = jnp.einsum('bnqkgd,bnjkd->bnkgqj', qb, kb, preferred_element_type=F32) * scale
    s_ctx = jnp.einsum('bnqkgd,bckd->bnkgqc', qb, k_ctx, preferred_element_type=F32) * scale
    q_pos = jnp.arange(nb)[:, None, None] * BLOCK + jnp.arange(BLOCK)[None, :, None]
    k_pos = jnp.arange(nb)[:, None, None] * BLOCK - BLOCK + jnp.arange(3 * BLOCK)[None, None, :]
    allowed = (jnp.abs(q_pos - k_pos) <= WINDOW) & (k_pos >= 0) & (k_pos < S)
    s_loc = jnp.where(allowed[None, :, None, None], s_loc, -jnp.inf)
    sink_l = sink.astype(F32).reshape(1, 1, N_KV_HEADS, KV_GROUP, 1, 1)
    m = jnp.maximum(jnp.maximum(s_loc.max(-1, keepdims=True), s_ctx.max(-1, keepdims=True)), sink_l)
    p_loc = jnp.exp(s_loc - m)
    p_ctx = jnp.exp(s_ctx - m)
    denom = p_loc.sum(-1, keepdims=True) + p_ctx.sum(-1, keepdims=True) + jnp.exp(sink_l - m)
    o = (jnp.einsum('bnkgqj,bnjkd->bnkgqd', p_loc, vb.astype(F32))
         + jnp.einsum('bnkgqc,bckd->bnkgqd', p_ctx, v_ctx.astype(F32))) / denom
    o = o.transpose(0, 1, 4, 2, 3, 5).reshape(B, S, ATTN_WIDTH)
    return o.astype(q.dtype)


def context_attention(q, k, v, sink):
    B, C = q.shape[:2]
    scale = HEAD_DIM ** -0.5
    qc = q.reshape(B, C, N_KV_HEADS, KV_GROUP, HEAD_DIM)
    s = jnp.einsum('bqkgd,bckd->bkgqc', qc, k, preferred_element_type=F32) * scale
    sink_l = sink.astype(F32).reshape(1, N_KV_HEADS, KV_GROUP, 1, 1)
    m = jnp.maximum(s.max(-1, keepdims=True), sink_l)
    p = jnp.exp(s - m)
    denom = p.sum(-1, keepdims=True) + jnp.exp(sink_l - m)
    o = jnp.einsum('bkgqc,bckd->bkgqd', p, v.astype(F32)) / denom
    return o.transpose(0, 3, 1, 2, 4).reshape(B, C, ATTN_WIDTH).astype(q.dtype)


def multiscale_pool(p, pool_w, pool_scale):
    B, L, _ = p.shape
    p32 = p.astype(F32)
    cs = jnp.concatenate([jnp.zeros((B, 1, POOL_WIDTH), F32), jnp.cumsum(p32, axis=1)], axis=1)
    t = jnp.arange(L)
    outs = []
    for g, w in enumerate(POOL_WINDOWS):
        lo = jnp.clip(t - w // 2, 0, L)
        hi = jnp.clip(t - w // 2 + w, 0, L)
        sl = slice(g * POOL_GROUP_DIM, (g + 1) * POOL_GROUP_DIM)
        csg = cs[:, :, sl]
        mean = (csg[:, hi] - csg[:, lo]) / (hi - lo).astype(F32)[None, :, None]
        outs.append(mean - p32[:, :, sl])
    d = jnp.stack(outs, axis=2)
    y = jnp.einsum('blgc,gcd->blgd', d, pool_w.astype(F32)).reshape(B, L, POOL_WIDTH)
    return (y * pool_scale.astype(F32)).astype(p.dtype)


def routed_experts(h, idx, gate, w_gate, w_up, w_down):
    T, D = h.shape
    TK = T * TOP_K
    flat_e = idx.reshape(TK)
    flat_w = gate.reshape(TK)
    order = jnp.argsort(flat_e)
    e_sorted = flat_e[order]
    sizes = jnp.bincount(flat_e, length=N_EXPERTS)
    padded = (sizes + EXPERT_BLOCK - 1) // EXPERT_BLOCK * EXPERT_BLOCK
    pad_end = jnp.cumsum(padded)
    pad_start = pad_end - padded
    start = jnp.cumsum(sizes) - sizes
    dest = pad_start[e_sorted] + (jnp.arange(TK) - start[e_sorted])
    n_rows = -(-(TK + N_EXPERTS * (EXPERT_BLOCK - 1)) // EXPERT_BLOCK) * EXPERT_BLOCK
    n_blocks = n_rows // EXPERT_BLOCK
    row_tok = jnp.zeros((n_rows,), jnp.int32).at[dest].set((order // TOP_K).astype(jnp.int32))
    row_w = jnp.zeros((n_rows,), F32).at[dest].set(flat_w[order])
    block_e = jnp.minimum(jnp.searchsorted(pad_end, jnp.arange(n_blocks) * EXPERT_BLOCK, side='right'),
                          N_EXPERTS - 1)

    def expert_block(args):
        tok_b, w_b, e = args
        xb = h[tok_b]
        y = (jax.nn.silu(xb @ w_gate[e]) * (xb @ w_up[e])) @ w_down[e]
        return y.astype(F32) * w_b[:, None]

    y_rows = lax.map(expert_block, (row_tok.reshape(n_blocks, EXPERT_BLOCK),
                                    row_w.reshape(n_blocks, EXPERT_BLOCK), block_e))
    return jnp.zeros((T, D), F32).at[row_tok].add(y_rows.reshape(n_rows, D))


def moe_ffn(h, w_router, router_bias, w_gate, w_up, w_down, ws_gate, ws_up, ws_down):
    T = h.shape[0]
    scores = jax.nn.sigmoid(jnp.dot(h, w_router, preferred_element_type=F32))
    biased = scores + router_bias.astype(F32)
    grp_score = lax.top_k(biased.reshape(T, N_EXPERT_GROUPS, EXPERTS_PER_GROUP), 2)[0].sum(-1)
    _, top_grp = lax.top_k(grp_score, TOPK_GROUPS)
    grp_mask = jnp.any(top_grp[:, :, None] == jnp.arange(N_EXPERT_GROUPS)[None, None, :], axis=1)
    expert_mask = jnp.repeat(grp_mask, EXPERTS_PER_GROUP, axis=1)
    _, idx = lax.top_k(jnp.where(expert_mask, biased, -jnp.inf), TOP_K)
    gate = jnp.take_along_axis(scores, idx, axis=1)
    gate = gate / jnp.sum(gate, axis=-1, keepdims=True) * ROUTED_SCALE
    routed = routed_experts(h, idx, gate, w_gate, w_up, w_down)
    shared = (jax.nn.silu(h @ ws_gate) * (h @ ws_up)) @ ws_down
    return (routed + shared.astype(F32)).astype(h.dtype)


def setup_inputs(seed: int = 0) -> dict:
    key = jax.random.key(seed)
    ks = jax.random.split(key, 24)
    D, L = D_MODEL, DEPTH

    def nrm(k, shape, scale):
        return jax.random.normal(k, shape, F32) * scale

    return {
        "x": nrm(ks[0], (BATCH, SEQ, D), 1.0),
        "c": nrm(ks[1], (BATCH, D), 1.0),
        "ctx": nrm(ks[2], (BATCH, CTX_LEN, D), 1.0),
        "c_ctx": nrm(ks[3], (D,), 1.0),
        "w_ada": nrm(ks[4], (L, D, 6 * D), 0.5 * D ** -0.5),
        "b_ada": nrm(ks[5], (L, 6 * D), 0.02),
        "norm1_g": 1.0 + nrm(ks[6], (L, D), 0.05),
        "norm2_g": 1.0 + nrm(ks[7], (L, D), 0.05),
        "w_in": nrm(ks[8], (L, D, IN_COLS), D ** -0.5),
        "attn_sink": nrm(ks[9], (L, N_HEADS), 0.5),
        "pool_w": nrm(ks[10], (L, N_POOL_GROUPS, POOL_GROUP_DIM, POOL_GROUP_DIM), POOL_GROUP_DIM ** -0.5),
        "pool_scale": 1.0 + nrm(ks[11], (L, POOL_WIDTH), 0.1),
        "w_out": nrm(ks[12], (L, MIX_WIDTH, D), MIX_WIDTH ** -0.5),
        "w_router": nrm(ks[13], (L, D, N_EXPERTS), D ** -0.5),
        "router_bias": nrm(ks[14], (L, N_EXPERTS), 0.01),
        "w_gate": nrm(ks[15], (L, N_EXPERTS, D, D_EXPERT), D ** -0.5),
        "w_up": nrm(ks[16], (L, N_EXPERTS, D, D_EXPERT), D ** -0.5),
        "w_down": nrm(ks[17], (L, N_EXPERTS, D_EXPERT, D), D_EXPERT ** -0.5),
        "ws_gate": nrm(ks[18], (L, D, D_SHARED), D ** -0.5),
        "ws_up": nrm(ks[19], (L, D, D_SHARED), D ** -0.5),
        "ws_down": nrm(ks[20], (L, D_SHARED, D), D_SHARED ** -0.5),
        "final_g": 1.0 + nrm(ks[21], (D,), 0.05),
    }


def reference(x, c, ctx, c_ctx, w_ada, b_ada, norm1_g, norm2_g, w_in, attn_sink, pool_w, pool_scale,
              w_out, w_router, router_bias, w_gate, w_up, w_down, ws_gate, ws_up, ws_down, final_g):
    B, S, D = x.shape
    rope = axial_rope_tables(S)
    silu_c = jax.nn.silu(c)
    silu_cc = jax.nn.silu(c_ctx)
    for l in range(DEPTH):
        last = l == DEPTH - 1
        mod = silu_c @ w_ada[l] + b_ada[l]
        mod_c = silu_cc @ w_ada[l] + b_ada[l]
        sh1, sc1, g1, sh2, sc2, g2 = jnp.split(mod[:, None, :], 6, axis=-1)
        sh1c, sc1c, g1c, sh2c, sc2c, g2c = jnp.split(mod_c, 6)

        hx = modulate(rmsnorm(x, norm1_g[l]), sh1, sc1)
        hc = modulate(rmsnorm(ctx, norm1_g[l]), sh1c, sc1c)
        qx, kx, vx, px = split_projection(hx @ w_in[l])
        qx = apply_axial_rope(qx, rope)
        kx = apply_axial_rope(kx, rope)
        if last:
            kc, vc = split_kv(hc @ w_in[l][:, ATTN_WIDTH:ATTN_WIDTH + 2 * KV_WIDTH])
        else:
            qc, kc, vc, pc = split_projection(hc @ w_in[l])
        attn_x = latent_window_attention(qx, kx, vx, kc, vc, attn_sink[l])
        pool_x = multiscale_pool(px, pool_w[l], pool_scale[l])
        x = x + g1 * (jnp.concatenate([attn_x, pool_x], axis=-1) @ w_out[l])
        if not last:
            attn_c = context_attention(qc, kc, vc, attn_sink[l])
            pool_c = multiscale_pool(pc, pool_w[l], pool_scale[l])
            ctx = ctx + g1c * (jnp.concatenate([attn_c, pool_c], axis=-1) @ w_out[l])

        hx2 = modulate(rmsnorm(x, norm2_g[l]), sh2, sc2)
        ffn_x = moe_ffn(hx2.reshape(B * S, D), w_router[l], router_bias[l], w_gate[l], w_up[l], w_down[l],
                        ws_gate[l], ws_up[l], ws_down[l]).reshape(B, S, D)
        x = x + g2 * ffn_x
        if not last:
            Cn = ctx.shape[1]
            hc2 = modulate(rmsnorm(ctx, norm2_g[l]), sh2c, sc2c)
            ffn_c = moe_ffn(hc2.reshape(B * Cn, D), w_router[l], router_bias[l], w_gate[l], w_up[l], w_down[l],
                            ws_gate[l], ws_up[l], ws_down[l]).reshape(B, Cn, D)
            ctx = ctx + g2c * ffn_c
    return rmsnorm(x, final_g)
```

```python
import functools

import jax
import jax.numpy as jnp
from jax import lax
from jax.experimental import pallas as pl
from jax.experimental.pallas import tpu as pltpu

F32 = jnp.float32
BF16 = jnp.bfloat16

D_MODEL = 1024
GRID_W = 64
N_HEADS = 8
N_KV_HEADS = 2
HEAD_DIM = 64
ATTN_WIDTH = N_HEADS * HEAD_DIM
KV_WIDTH = N_KV_HEADS * HEAD_DIM
WINDOW = 128
ROPE_THETA = 10000.0
POOL_WINDOWS = (2, 4, 8, 16)
POOL_WIDTH = D_MODEL - ATTN_WIDTH
POOL_GROUP_DIM = POOL_WIDTH // len(POOL_WINDOWS)
IN_COLS = ATTN_WIDTH + 2 * KV_WIDTH + POOL_WIDTH
N_EXPERTS = 64
TOP_K = 8
N_EXPERT_GROUPS = 8
EXPERTS_PER_GROUP = N_EXPERTS // N_EXPERT_GROUPS
TOPK_GROUPS = 4
D_EXPERT = 256
D_SHARED = 256
ROUTED_SCALE = 2.5
EPS = 1e-6

LANES = 128
VMEM_LIMIT = 48 * 1024 * 1024

TM_PROJ = 512
TQ = 512
QB = 128
POOL_SLAB = 256
POOL_OFF = 64
TR = 512
TM_EXP = 256
TD = 256
TF = 256


def _silu(x):
    return x * (1.0 / (1.0 + jnp.exp(-x)))


def _split_bf16(x):
    hi = x.astype(BF16)
    lo = (x - hi.astype(F32)).astype(BF16)
    return hi, lo


def _dot(a, b):
    return jnp.dot(a, b, preferred_element_type=F32)


def _dot_nt(a, b):
    return lax.dot_general(a, b, (((1,), (1,)), ((), ())), preferred_element_type=F32)


def _ada_kernel(c_ref, w_ref, b_ref, o_ref):
    a_hi, a_lo = _split_bf16(_silu(c_ref[...]))
    w_hi, w_lo = _split_bf16(w_ref[...])
    o_ref[...] = _dot(a_hi, w_hi) + _dot(a_lo, w_hi) + _dot(a_hi, w_lo) + b_ref[...]


def _ada(c8, w_ada, b_ada):
    d = c8.shape[1]
    n = w_ada.shape[1]
    tn = 512
    return pl.pallas_call(
        _ada_kernel,
        out_shape=jax.ShapeDtypeStruct((8, n), F32),
        grid=(n // tn,),
        in_specs=[pl.BlockSpec((8, d), lambda j: (0, 0)),
                  pl.BlockSpec((d, tn), lambda j: (0, j)),
                  pl.BlockSpec((1, tn), lambda j: (0, j))],
        out_specs=pl.BlockSpec((8, tn), lambda j: (0, j)),
        compiler_params=pltpu.CompilerParams(vmem_limit_bytes=VMEM_LIMIT),
        name="ada",
    )(c8, w_ada, b_ada.reshape(1, n))


def _norm_mod(x, g, shift, scale):
    ms = jnp.mean(x * x, axis=-1, keepdims=True)
    return (x * lax.rsqrt(ms + EPS) * g) * (1.0 + scale) + shift


def _lane_variants(t):
    lane = lax.broadcasted_iota(jnp.int32, t.shape, 1)
    lo = lane < HEAD_DIM
    tr = pltpu.roll(t, HEAD_DIM, 1)
    zero = jnp.zeros_like(t)
    return (jnp.where(lo, t, zero), jnp.where(lo, zero, tr),
            jnp.where(lo, tr, zero), jnp.where(lo, zero, t))


def _store_variants(ref, t):
    for i, var in enumerate(_lane_variants(t)):
        ref[0, :, i * LANES:(i + 1) * LANES] = var.astype(BF16)


def _inproj_kernel(x_ref, mod_ref, g_ref, w_ref, cos_ref, sin_ref, q_ref, k_ref, v_ref, p_ref):
    h = _norm_mod(x_ref[0], g_ref[...], mod_ref[0, 0:1, :], mod_ref[0, 1:2, :])
    z = _dot(h.astype(BF16), w_ref[...])
    cos = cos_ref[...]
    sin = sin_ref[...]
    lane = lax.broadcasted_iota(jnp.int32, cos.shape, 1)
    first_half = (lane & 16) == 0

    def rope(zc):
        partner = jnp.where(first_half, pltpu.roll(zc, LANES - 16, 1), pltpu.roll(zc, 16, 1))
        return zc * cos + partner * sin

    scale = HEAD_DIM ** -0.5
    for c in range(ATTN_WIDTH // LANES):
        q_ref[0, :, c * LANES:(c + 1) * LANES] = (rope(z[:, c * LANES:(c + 1) * LANES]) * scale).astype(BF16)
    _store_variants(k_ref, rope(z[:, ATTN_WIDTH:ATTN_WIDTH + KV_WIDTH]))
    _store_variants(v_ref, z[:, ATTN_WIDTH + KV_WIDTH:ATTN_WIDTH + 2 * KV_WIDTH])
    p_ref[0] = z[:, ATTN_WIDTH + 2 * KV_WIDTH:]


def _inproj(x, mod, g1, w_in_bf, cos_t, sin_t):
    b, s, d = x.shape
    tm = TM_PROJ
    return pl.pallas_call(
        _inproj_kernel,
        out_shape=(jax.ShapeDtypeStruct((b, s, ATTN_WIDTH), BF16),
                   jax.ShapeDtypeStruct((b, s, 4 * LANES), BF16),
                   jax.ShapeDtypeStruct((b, s, 4 * LANES), BF16),
                   jax.ShapeDtypeStruct((b, s, POOL_WIDTH), F32)),
        grid=(s // tm, b),
        in_specs=[pl.BlockSpec((1, tm, d), lambda n, bi: (bi, n, 0)),
                  pl.BlockSpec((1, 6, d), lambda n, bi: (bi, 0, 0)),
                  pl.BlockSpec((1, d), lambda n, bi: (0, 0)),
                  pl.BlockSpec((d, IN_COLS), lambda n, bi: (0, 0)),
                  pl.BlockSpec((tm, LANES), lambda n, bi: (n, 0)),
                  pl.BlockSpec((tm, LANES), lambda n, bi: (n, 0))],
        out_specs=(pl.BlockSpec((1, tm, ATTN_WIDTH), lambda n, bi: (bi, n, 0)),
                   pl.BlockSpec((1, tm, 4 * LANES), lambda n, bi: (bi, n, 0)),
                   pl.BlockSpec((1, tm, 4 * LANES), lambda n, bi: (bi, n, 0)),
                   pl.BlockSpec((1, tm, POOL_WIDTH), lambda n, bi: (bi, n, 0))),
        compiler_params=pltpu.CompilerParams(vmem_limit_bytes=VMEM_LIMIT),
        name="inproj",
    )(x, mod, g1, w_in_bf, cos_t, sin_t)


def _ctxproj_kernel(x_ref, mod_ref, g_ref, w_ref, k_ref, v_ref):
    h = _norm_mod(x_ref[0], g_ref[...], mod_ref[0, 0:1, :], mod_ref[0, 1:2, :])
    z = _dot(h.astype(BF16), w_ref[...])
    _store_variants(k_ref, z[:, :KV_WIDTH])
    _store_variants(v_ref, z[:, KV_WIDTH:])


def _ctxproj(ctx, mod_c, g1, w_kv_bf):
    b, c, d = ctx.shape
    return pl.pallas_call(
        _ctxproj_kernel,
        out_shape=(jax.ShapeDtypeStruct((b, c, 4 * LANES), BF16),
                   jax.ShapeDtypeStruct((b, c, 4 * LANES), BF16)),
        grid=(b,),
        in_specs=[pl.BlockSpec((1, c, d), lambda bi: (bi, 0, 0)),
                  pl.BlockSpec((1, 6, d), lambda bi: (0, 0, 0)),
                  pl.BlockSpec((1, d), lambda bi: (0, 0)),
                  pl.BlockSpec((d, 2 * KV_WIDTH), lambda bi: (0, 0))],
        out_specs=(pl.BlockSpec((1, c, 4 * LANES), lambda bi: (bi, 0, 0)),
                   pl.BlockSpec((1, c, 4 * LANES), lambda bi: (bi, 0, 0))),
        compiler_params=pltpu.CompilerParams(vmem_limit_bytes=VMEM_LIMIT),
        name="ctxproj",
    )(ctx, mod_c, g1, w_kv_bf)


def _attn_kernel(seq_len, sink_ref, q_ref, k_ref, kp_ref, kn_ref, v_ref, vp_ref, vn_ref, kc_ref, vc_ref,
                 p_ref, pp_ref, pn_ref, x_ref, mod_ref, band_ref, poolw_ref, pscale_ref, wout_ref,
                 g2_ref, wrh_ref, wrl_ref, x1_ref, h2_ref, lg_ref, kwin, vwin, pext, mix):
    n = pl.program_id(1)
    n_last = pl.num_programs(1) - 1

    kwin[0:QB, :] = kp_ref[0]
    kwin[QB:QB + TQ, :] = k_ref[0]
    kwin[QB + TQ:, :] = kn_ref[0]
    vwin[0:QB, :] = vp_ref[0]
    vwin[QB:QB + TQ, :] = v_ref[0]
    vwin[QB + TQ:, :] = vn_ref[0]

    pext[0:QB - 8, :] = jnp.zeros((QB - 8, POOL_WIDTH), F32)
    pext[QB - 8:QB, :] = jnp.where(n > 0, pp_ref[0], 0.0)
    pext[QB:QB + TQ, :] = p_ref[0]
    pext[QB + TQ:QB + TQ + 8, :] = jnp.where(n < n_last, pn_ref[0], 0.0)
    pext[QB + TQ + 8:, :] = jnp.zeros((QB - 8, POOL_WIDTH), F32)

    row = lax.broadcasted_iota(jnp.int32, (QB, 3 * QB), 0)
    col = lax.broadcasted_iota(jnp.int32, (QB, 3 * QB), 1)
    in_band = (col >= row) & (col <= row + 2 * WINDOW)
    tok = lax.broadcasted_iota(jnp.int32, (QB, 1), 0)
    kc = kc_ref[0]
    vc = vc_ref[0]

    def sub_block(j, carry):
        r0 = pl.multiple_of(j * QB, QB)
        qj = q_ref[0, pl.ds(r0, QB), :]
        kw = kwin[pl.ds(r0, 3 * QB), :]
        vw = vwin[pl.ds(r0, 3 * QB), :]
        kpos = col + (n * TQ + j * QB - QB)
        ok = in_band & (kpos >= 0) & (kpos < seq_len)
        bias = jnp.where(ok, 0.0, -jnp.inf)
        for c in range(N_HEADS // 2):
            qc = qj[:, c * LANES:(c + 1) * LANES]
            pair = None
            for par in range(2):
                head = 2 * c + par
                var = 2 * (head // (N_HEADS // N_KV_HEADS)) + par
                sl = slice(var * LANES, (var + 1) * LANES)
                s_loc = _dot_nt(qc, kw[:, sl]) + bias
                s_ctx = _dot_nt(qc, kc[:, sl])
                sink = sink_ref[head]
                m = jnp.maximum(jnp.maximum(jnp.max(s_loc, axis=1, keepdims=True),
                                            jnp.max(s_ctx, axis=1, keepdims=True)), sink)
                p_loc = jnp.exp(s_loc - m)
                p_ctx = jnp.exp(s_ctx - m)
                denom = (jnp.sum(p_loc, axis=1, keepdims=True) + jnp.sum(p_ctx, axis=1, keepdims=True)
                         + jnp.exp(sink - m))
                o = _dot(p_loc.astype(BF16), vw[:, sl]) + _dot(p_ctx.astype(BF16), vc[:, sl])
                o = o * (1.0 / denom)
                pair = o if pair is None else pair + o
            mix[pl.ds(r0, QB), c * LANES:(c + 1) * LANES] = pair.astype(BF16)

        slab = pext[pl.ds(pl.multiple_of(r0 + POOL_OFF, 8), POOL_SLAB), :]
        tpos = tok + (n * TQ + j * QB)
        for g, w in enumerate(POOL_WINDOWS):
            sg = slab[:, g * LANES:(g + 1) * LANES]
            hi, lo = _split_bf16(sg)
            band = band_ref[g]
            wsum = _dot(band, hi) + _dot(band, lo)
            cnt = (jnp.minimum(tpos - w // 2 + w, seq_len) - jnp.maximum(tpos - w // 2, 0)).astype(F32)
            dlt = wsum / cnt - sg[POOL_OFF:POOL_OFF + QB, :]
            y = _dot(dlt.astype(BF16), poolw_ref[g]) * pscale_ref[:, g * LANES:(g + 1) * LANES]
            mix[pl.ds(r0, QB), ATTN_WIDTH + g * LANES:ATTN_WIDTH + (g + 1) * LANES] = y.astype(BF16)
        return carry

    lax.fori_loop(0, TQ // QB, sub_block, 0)

    proj = _dot(mix[...], wout_ref[...])
    x1 = x_ref[0] + mod_ref[0, 2:3, :] * proj
    x1_ref[0] = x1
    h2 = _norm_mod(x1, g2_ref[...], mod_ref[0, 3:4, :], mod_ref[0, 4:5, :])
    h2_ref[0] = h2
    h_hi, h_lo = _split_bf16(h2)
    wrh = wrh_ref[...]
    lg_ref[...] = _dot_nt(wrh, h_hi) + _dot_nt(wrh, h_lo) + _dot_nt(wrl_ref[...], h_hi)


def _attn(sink, q, k4, v4, kc4, vc4, p, x, mod, band, poolw_bf, pscale, wout_bf, g2, wr_hi, wr_lo):
    b, s, d = x.shape
    c = kc4.shape[1]
    nt = s // TQ
    hb = TQ // QB
    pb = TQ // 8
    kv_main = pl.BlockSpec((1, TQ, 4 * LANES), lambda bi, n: (bi, n, 0))
    kv_prev = pl.BlockSpec((1, QB, 4 * LANES), lambda bi, n: (bi, jnp.maximum(n * hb - 1, 0), 0))
    kv_next = pl.BlockSpec((1, QB, 4 * LANES), lambda bi, n: (bi, jnp.minimum((n + 1) * hb, s // QB - 1), 0))
    const2 = lambda bi, n: (0, 0)
    const3 = lambda bi, n: (0, 0, 0)
    return pl.pallas_call(
        functools.partial(_attn_kernel, s),
        out_shape=(jax.ShapeDtypeStruct((b, s, d), F32),
                   jax.ShapeDtypeStruct((b, s, d), F32),
                   jax.ShapeDtypeStruct((N_EXPERTS, b * s), F32)),
        grid=(b, nt),
        in_specs=[pl.BlockSpec(memory_space=pltpu.SMEM),
                  pl.BlockSpec((1, TQ, ATTN_WIDTH), lambda bi, n: (bi, n, 0)),
                  kv_main, kv_prev, kv_next, kv_main, kv_prev, kv_next,
                  pl.BlockSpec((1, c, 4 * LANES), lambda bi, n: (bi, 0, 0)),
                  pl.BlockSpec((1, c, 4 * LANES), lambda bi, n: (bi, 0, 0)),
                  pl.BlockSpec((1, TQ, POOL_WIDTH), lambda bi, n: (bi, n, 0)),
                  pl.BlockSpec((1, 8, POOL_WIDTH), lambda bi, n: (bi, jnp.maximum(n * pb - 1, 0), 0)),
                  pl.BlockSpec((1, 8, POOL_WIDTH), lambda bi, n: (bi, jnp.minimum((n + 1) * pb, s // 8 - 1), 0)),
                  pl.BlockSpec((1, TQ, d), lambda bi, n: (bi, n, 0)),
                  pl.BlockSpec((1, 6, d), lambda bi, n: (bi, 0, 0)),
                  pl.BlockSpec((len(POOL_WINDOWS), QB, POOL_SLAB), const3),
                  pl.BlockSpec((len(POOL_WINDOWS), POOL_GROUP_DIM, POOL_GROUP_DIM), const3),
                  pl.BlockSpec((1, POOL_WIDTH), const2),
                  pl.BlockSpec((d, d), const2),
                  pl.BlockSpec((1, d), const2),
                  pl.BlockSpec((N_EXPERTS, d), const2),
                  pl.BlockSpec((N_EXPERTS, d), const2)],
        out_specs=(pl.BlockSpec((1, TQ, d), lambda bi, n: (bi, n, 0)),
                   pl.BlockSpec((1, TQ, d), lambda bi, n: (bi, n, 0)),
                   pl.BlockSpec((N_EXPERTS, TQ), lambda bi, n: (0, bi * nt + n))),
        scratch_shapes=[pltpu.VMEM((TQ + 2 * QB, 4 * LANES), BF16),
                        pltpu.VMEM((TQ + 2 * QB, 4 * LANES), BF16),
                        pltpu.VMEM((TQ + 2 * QB, POOL_WIDTH), F32),
                        pltpu.VMEM((TQ, d), BF16)],
        compiler_params=pltpu.CompilerParams(vmem_limit_bytes=VMEM_LIMIT),
        name="attn",
    )(sink, q, k4, k4, k4, v4, v4, v4, kc4, vc4, p, p, p, x, mod, band, poolw_bf, pscale, wout_bf,
      g2, wr_hi, wr_lo)


def _first_argmax_rows(v, row_iota, n_rows):
    m = jnp.max(v, axis=0, keepdims=True)
    idx = jnp.min(jnp.where(v == m, row_iota, n_rows), axis=0, keepdims=True)
    return m, idx


def _route_kernel(lg_ref, bias_ref, tri_ref, idx_ref, gate_ref, rank_ref, cnt_ref, carry):
    i = pl.program_id(0)

    @pl.when(i == 0)
    def _():
        carry[...] = jnp.zeros_like(carry)

    scores = 1.0 / (1.0 + jnp.exp(-lg_ref[...]))
    biased = scores + bias_ref[...]
    e_iota = lax.broadcasted_iota(jnp.int32, scores.shape, 0).astype(F32)
    g_iota = lax.broadcasted_iota(jnp.int32, (EXPERTS_PER_GROUP, TR), 0).astype(F32)
    neg = -jnp.inf

    grp = []
    for g in range(N_EXPERT_GROUPS):
        blk = biased[g * EXPERTS_PER_GROUP:(g + 1) * EXPERTS_PER_GROUP, :]
        m1, i1 = _first_argmax_rows(blk, g_iota, float(EXPERTS_PER_GROUP))
        m2 = jnp.max(jnp.where(g_iota == i1, neg, blk), axis=0, keepdims=True)
        grp.append(m1 + m2)
    grp = jnp.concatenate(grp, axis=0)
    gg_iota = lax.broadcasted_iota(jnp.int32, grp.shape, 0).astype(F32)
    grp_sel = jnp.zeros(grp.shape, F32)
    for _ in range(TOPK_GROUPS):
        _, gi = _first_argmax_rows(grp, gg_iota, float(N_EXPERT_GROUPS))
        hit = gg_iota == gi
        grp_sel = jnp.where(hit, 1.0, grp_sel)
        grp = jnp.where(hit, neg, grp)
    allowed = jnp.concatenate(
        [jnp.broadcast_to(grp_sel[g:g + 1, :], (EXPERTS_PER_GROUP, TR)) for g in range(N_EXPERT_GROUPS)], axis=0)
    masked = jnp.where(allowed > 0.5, biased, neg)

    idxs, gates = [], []
    onehot = jnp.zeros(scores.shape, F32)
    for _ in range(TOP_K):
        _, ei = _first_argmax_rows(masked, e_iota, float(N_EXPERTS))
        hit = e_iota == ei
        idxs.append(ei)
        gates.append(jnp.sum(jnp.where(hit, scores, 0.0), axis=0, keepdims=True))
        onehot = jnp.where(hit, 1.0, onehot)
        masked = jnp.where(hit, neg, masked)
    idx = jnp.concatenate(idxs, axis=0)
    gate = jnp.concatenate(gates, axis=0)
    gate = gate / jnp.sum(gate, axis=0, keepdims=True) * ROUTED_SCALE

    before = _dot(onehot.astype(BF16), tri_ref[...]) + carry[:, 0:1]
    ranks = [jnp.sum(jnp.where(e_iota == idxs[k], before, 0.0), axis=0, keepdims=True) for k in range(TOP_K)]
    idx_ref[...] = idx.astype(jnp.int32)
    gate_ref[...] = gate
    rank_ref[...] = jnp.concatenate(ranks, axis=0).astype(jnp.int32)
    total = carry[...] + jnp.sum(onehot, axis=1, keepdims=True)
    carry[...] = total
    cnt_ref[...] = total


def _route(lg_t, bias, tri):
    e, t = lg_t.shape
    tok = pl.BlockSpec((TOP_K, TR), lambda i: (0, i))
    return pl.pallas_call(
        _route_kernel,
        out_shape=(jax.ShapeDtypeStruct((TOP_K, t), jnp.int32),
                   jax.ShapeDtypeStruct((TOP_K, t), F32),
                   jax.ShapeDtypeStruct((TOP_K, t), jnp.int32),
                   jax.ShapeDtypeStruct((e, LANES), F32)),
        grid=(t // TR,),
        in_specs=[pl.BlockSpec((e, TR), lambda i: (0, i)),
                  pl.BlockSpec((e, 1), lambda i: (0, 0)),
                  pl.BlockSpec((TR, TR), lambda i: (0, 0))],
        out_specs=(tok, tok, tok, pl.BlockSpec((e, LANES), lambda i: (0, 0))),
        scratch_shapes=[pltpu.VMEM((e, LANES), F32)],
        compiler_params=pltpu.CompilerParams(vmem_limit_bytes=VMEM_LIMIT),
        name="route",
    )(lg_t, bias, tri)


def _dispatch_kernel(dest_hbm, h_hbm, xs_hbm, idx_smem, sem_idx, sem_rows):
    i = pl.program_id(0)
    n_idx = TD * TOP_K
    cp = pltpu.make_async_copy(dest_hbm.at[pl.ds(i * n_idx, n_idx)], idx_smem, sem_idx)
    cp.start()
    cp.wait()

    def row_copy(r):
        t = i * TD + lax.shift_right_logical(r, 3)
        return pltpu.make_async_copy(h_hbm.at[pl.ds(t, 1)], xs_hbm.at[pl.ds(idx_smem[r], 1)], sem_rows)

    def issue(r, carry):
        row_copy(r).start()
        return carry

    def drain(r, carry):
        row_copy(r).wait()
        return carry

    lax.fori_loop(0, n_idx, issue, 0)
    lax.fori_loop(0, n_idx, drain, 0)


def _dispatch(dest_flat, h2, n_rows):
    t, d = h2.shape
    return pl.pallas_call(
        _dispatch_kernel,
        out_shape=jax.ShapeDtypeStruct((n_rows, d), F32),
        grid=(t // TD,),
        in_specs=[pl.BlockSpec(memory_space=pl.ANY), pl.BlockSpec(memory_space=pl.ANY)],
        out_specs=pl.BlockSpec(memory_space=pl.ANY),
        scratch_shapes=[pltpu.SMEM((TD * TOP_K,), jnp.int32),
                        pltpu.SemaphoreType.DMA, pltpu.SemaphoreType.DMA],
        compiler_params=pltpu.CompilerParams(has_side_effects=True),
        name="dispatch",
    )(dest_flat, h2)


def _experts_kernel(be_ref, nact_ref, xs_ref, wg_ref, wu_ref, wd_ref, ys_ref, wg_bf, wu_bf, wd_bf):
    i = pl.program_id(0)
    prev = be_ref[jnp.maximum(i - 1, 0)]

    @pl.when((i == 0) | (be_ref[i] != prev))
    def _():
        wg_bf[...] = wg_ref[0].astype(BF16)
        wu_bf[...] = wu_ref[0].astype(BF16)
        wd_bf[...] = wd_ref[0].astype(BF16)

    @pl.when(i < nact_ref[0])
    def _():
        xb = xs_ref[...].astype(BF16)
        hid = _silu(_dot(xb, wg_bf[...])) * _dot(xb, wu_bf[...])
        ys_ref[...] = _dot(hid.astype(BF16), wd_bf[...])


def _experts(block_e, n_active, xs, w_gate, w_up, w_down):
    n_rows, d = xs.shape
    nb = n_rows // TM_EXP
    grid_spec = pltpu.PrefetchScalarGridSpec(
        num_scalar_prefetch=2,
        grid=(nb,),
        in_specs=[pl.BlockSpec((TM_EXP, d), lambda i, be, na: (jnp.minimum(i, na[0] - 1), 0)),
                  pl.BlockSpec((1, d, D_EXPERT), lambda i, be, na: (be[i], 0, 0)),
                  pl.BlockSpec((1, d, D_EXPERT), lambda i, be, na: (be[i], 0, 0)),
                  pl.BlockSpec((1, D_EXPERT, d), lambda i, be, na: (be[i], 0, 0))],
        out_specs=pl.BlockSpec((TM_EXP, d), lambda i, be, na: (jnp.minimum(i, na[0] - 1), 0)),
        scratch_shapes=[pltpu.VMEM((d, D_EXPERT), BF16), pltpu.VMEM((d, D_EXPERT), BF16),
                        pltpu.VMEM((D_EXPERT, d), BF16)],
    )
    return pl.pallas_call(
        _experts_kernel,
        out_shape=jax.ShapeDtypeStruct((n_rows, d), F32),
        grid_spec=grid_spec,
        compiler_params=pltpu.CompilerParams(vmem_limit_bytes=VMEM_LIMIT),
        name="experts",
    )(block_e, n_active, xs, w_gate, w_up, w_down)


def _combine_kernel(dest_hbm, ys_hbm, x1_ref, h2_ref, gate_ref, mod_ref, wsg_ref, wsu_ref, wsd_ref, fg_ref,
                    out_ref, idx_smem, buf, sem_idx, sem_rows):
    i = pl.program_id(0)
    n_idx = TF * TOP_K
    cp = pltpu.make_async_copy(dest_hbm.at[pl.ds(i * n_idx, n_idx)], idx_smem, sem_idx)
    cp.start()
    cp.wait()

    def row_copy(r):
        t = lax.shift_right_logical(r, 3)
        k = r & (TOP_K - 1)
        return pltpu.make_async_copy(ys_hbm.at[pl.ds(idx_smem[r], 1)], buf.at[k, pl.ds(t, 1)], sem_rows)

    def issue(r, carry):
        row_copy(r).start()
        return carry

    def drain(r, carry):
        row_copy(r).wait()
        return carry

    lax.fori_loop(0, n_idx, issue, 0)

    hb = h2_ref[...].astype(BF16)
    hid = _silu(_dot(hb, wsg_ref[...])) * _dot(hb, wsu_ref[...])
    ffn = _dot(hid.astype(BF16), wsd_ref[...])

    lax.fori_loop(0, n_idx, drain, 0)
    gate = gate_ref[...]
    for k in range(TOP_K):
        ffn = ffn + gate[:, k:k + 1] * buf[k]
    x2 = x1_ref[...] + mod_ref[0, 5:6, :] * ffn
    ms = jnp.mean(x2 * x2, axis=-1, keepdims=True)
    out_ref[...] = x2 * lax.rsqrt(ms + EPS) * fg_ref[...]


def _combine(dest_flat, ys, x1, h2, gate_tk, mod, wsg_bf, wsu_bf, wsd_bf, final_g, seq_len):
    t, d = x1.shape
    tiles_per_seq = seq_len // TF
    tok = pl.BlockSpec((TF, d), lambda i: (i, 0))
    return pl.pallas_call(
        _combine_kernel,
        out_shape=jax.ShapeDtypeStruct((t, d), F32),
        grid=(t // TF,),
        in_specs=[pl.BlockSpec(memory_space=pl.ANY), pl.BlockSpec(memory_space=pl.ANY),
                  tok, tok,
                  pl.BlockSpec((TF, TOP_K), lambda i: (i, 0)),
                  pl.BlockSpec((1, 6, d), lambda i: (i // tiles_per_seq, 0, 0)),
                  pl.BlockSpec((d, D_SHARED), lambda i: (0, 0)),
                  pl.BlockSpec((d, D_SHARED), lambda i: (0, 0)),
                  pl.BlockSpec((D_SHARED, d), lambda i: (0, 0)),
                  pl.BlockSpec((1, d), lambda i: (0, 0))],
        out_specs=tok,
        scratch_shapes=[pltpu.SMEM((TF * TOP_K,), jnp.int32),
                        pltpu.VMEM((TOP_K, TF, d), F32),
                        pltpu.SemaphoreType.DMA, pltpu.SemaphoreType.DMA],
        compiler_params=pltpu.CompilerParams(vmem_limit_bytes=VMEM_LIMIT),
        name="combine",
    )(dest_flat, ys, x1, h2, gate_tk, mod, wsg_bf, wsu_bf, wsd_bf, final_g)


def _rope_tables(n_tokens):
    pos = jnp.arange(n_tokens)
    row = (pos // GRID_W).astype(F32)
    col = (pos % GRID_W).astype(F32)
    n_freq = HEAD_DIM // 4
    inv_freq = ROPE_THETA ** (-jnp.arange(n_freq, dtype=F32) / n_freq)
    ang_r = row[:, None] * inv_freq[None, :]
    ang_c = col[:, None] * inv_freq[None, :]
    cos = jnp.concatenate([jnp.cos(ang_r)] * 2 + [jnp.cos(ang_c)] * 2, axis=1)
    sin = jnp.concatenate([-jnp.sin(ang_r), jnp.sin(ang_r), -jnp.sin(ang_c), jnp.sin(ang_c)], axis=1)
    reps = LANES // HEAD_DIM
    return jnp.tile(cos, (1, reps)), jnp.tile(sin, (1, reps))


def _pool_bands():
    i = jnp.arange(QB)[:, None]
    r = jnp.arange(POOL_SLAB)[None, :]
    return jnp.stack([((r >= i + POOL_OFF - w // 2) & (r < i + POOL_OFF + w // 2)).astype(BF16)
                      for w in POOL_WINDOWS])


def kernel(x, c, ctx, c_ctx, w_ada, b_ada, norm1_g, norm2_g, w_in, attn_sink, pool_w, pool_scale, w_out,
           w_router, router_bias, w_gate, w_up, w_down, ws_gate, ws_up, ws_down, final_g):
    b, s, d = x.shape
    t = b * s
    assert w_ada.shape[0] == 1 and d == D_MODEL and s % TQ == 0 and b + 1 <= 8

    c8 = jnp.zeros((8, d), F32).at[:b].set(c).at[b].set(c_ctx)
    mod = _ada(c8, w_ada[0], b_ada[0]).reshape(8, 6, d)
    g1 = norm1_g[0].reshape(1, d)
    g2 = norm2_g[0].reshape(1, d)
    w_in_bf = w_in[0].astype(BF16)
    cos_t, sin_t = _rope_tables(s)

    q, k4, v4, p = _inproj(x, mod, g1, w_in_bf, cos_t, sin_t)
    kc4, vc4 = _ctxproj(ctx, mod[b:b + 1], g1, w_in_bf[:, ATTN_WIDTH:ATTN_WIDTH + 2 * KV_WIDTH])

    wr_t = w_router[0].T
    wr_hi = wr_t.astype(BF16)
    wr_lo = (wr_t - wr_hi.astype(F32)).astype(BF16)
    x1, h2, lg_t = _attn(attn_sink[0], q, k4, v4, kc4, vc4, p, x, mod, _pool_bands(),
                         pool_w[0].astype(BF16), pool_scale[0].reshape(1, POOL_WIDTH), w_out[0].astype(BF16),
                         g2, wr_hi, wr_lo)

    tri = jnp.triu(jnp.ones((TR, TR), BF16), k=1)
    idx_kt, gate_kt, rank_kt, counts = _route(lg_t, router_bias[0].reshape(N_EXPERTS, 1), tri)

    sizes = counts[:, 0].astype(jnp.int32)
    padded = (sizes + TM_EXP - 1) // TM_EXP * TM_EXP
    pad_end = jnp.cumsum(padded)
    pad_start = pad_end - padded
    n_rows = -(-(t * TOP_K + N_EXPERTS * (TM_EXP - 1)) // TM_EXP) * TM_EXP
    n_blocks = n_rows // TM_EXP
    dest_flat = (pad_start[idx_kt] + rank_kt).T.reshape(t * TOP_K)
    block_e = jnp.minimum(jnp.searchsorted(pad_end, jnp.arange(n_blocks) * TM_EXP, side='right'),
                          N_EXPERTS - 1).astype(jnp.int32)
    n_active = (pad_end[-1:] // TM_EXP).astype(jnp.int32)

    h2f = h2.reshape(t, d)
    xs = _dispatch(dest_flat, h2f, n_rows)
    ys = _experts(block_e, n_active, xs, w_gate[0], w_up[0], w_down[0])
    out = _combine(dest_flat, ys, x1.reshape(t, d), h2f, gate_kt.T, mod,
                   ws_gate[0].astype(BF16), ws_up[0].astype(BF16), ws_down[0].astype(BF16),
                   final_g.reshape(1, d), s)
    return out.reshape(b, s, d)
```

```python
import functools

import jax
import jax.numpy as jnp
from jax import lax
from jax.experimental import pallas as pl
from jax.experimental.pallas import tpu as pltpu

F32 = jnp.float32
BF16 = jnp.bfloat16

D_MODEL = 1024
GRID_W = 64
N_HEADS = 8
N_KV_HEADS = 2
HEAD_DIM = 64
ATTN_WIDTH = N_HEADS * HEAD_DIM
KV_WIDTH = N_KV_HEADS * HEAD_DIM
WINDOW = 128
ROPE_THETA = 10000.0
POOL_WINDOWS = (2, 4, 8, 16)
POOL_WIDTH = D_MODEL - ATTN_WIDTH
POOL_GROUP_DIM = POOL_WIDTH // len(POOL_WINDOWS)
IN_COLS = ATTN_WIDTH + 2 * KV_WIDTH + POOL_WIDTH
N_EXPERTS = 64
TOP_K = 8
N_EXPERT_GROUPS = 8
EXPERTS_PER_GROUP = N_EXPERTS // N_EXPERT_GROUPS
TOPK_GROUPS = 4
D_EXPERT = 256
D_SHARED = 256
ROUTED_SCALE = 2.5
EPS = 1e-6

LANES = 128
ROW_TILE = D_MODEL // LANES
VMEM_LIMIT = 48 * 1024 * 1024

TM_PROJ = 512
TQ = 512
QB = 128
POOL_SLAB = 256
POOL_OFF = 64
TR = 512
TM_EXP = 256
TD = 256
TF = 256


def _silu(x):
    return x * (1.0 / (1.0 + jnp.exp(-x)))


def _split_bf16(x):
    hi = x.astype(BF16)
    lo = (x - hi.astype(F32)).astype(BF16)
    return hi, lo


def _dot(a, b):
    return jnp.dot(a, b, preferred_element_type=F32)


def _store_row_tiles(ref, val):
    rows = val.shape[0]
    for c in range(ROW_TILE):
        ref[pl.ds(c, rows, stride=ROW_TILE), :] = val[:, c * LANES:(c + 1) * LANES]


def _load_row_tiles(ref, rows, base=0):
    return jnp.concatenate([ref[pl.ds(base + c, rows, stride=ROW_TILE), :] for c in range(ROW_TILE)], axis=1)


def _dot_nt(a, b):
    return lax.dot_general(a, b, (((1,), (1,)), ((), ())), preferred_element_type=F32)


def _ada_kernel(c_ref, w_ref, b_ref, o_ref):
    a_hi, a_lo = _split_bf16(_silu(c_ref[...]))
    w_hi, w_lo = _split_bf16(w_ref[...])
    o_ref[...] = _dot(a_hi, w_hi) + _dot(a_lo, w_hi) + _dot(a_hi, w_lo) + b_ref[...]


def _ada(c8, w_ada, b_ada):
    d = c8.shape[1]
    n = w_ada.shape[1]
    tn = 512
    return pl.pallas_call(
        _ada_kernel,
        out_shape=jax.ShapeDtypeStruct((8, n), F32),
        grid=(n // tn,),
        in_specs=[pl.BlockSpec((8, d), lambda j: (0, 0)),
                  pl.BlockSpec((d, tn), lambda j: (0, j)),
                  pl.BlockSpec((1, tn), lambda j: (0, j))],
        out_specs=pl.BlockSpec((8, tn), lambda j: (0, j)),
        compiler_params=pltpu.CompilerParams(vmem_limit_bytes=VMEM_LIMIT),
        name="ada",
    )(c8, w_ada, b_ada.reshape(1, n))


def _norm_mod(x, g, shift, scale):
    ms = jnp.mean(x * x, axis=-1, keepdims=True)
    return (x * lax.rsqrt(ms + EPS) * g) * (1.0 + scale) + shift


def _lane_variants(t):
    lane = lax.broadcasted_iota(jnp.int32, t.shape, 1)
    lo = lane < HEAD_DIM
    tr = pltpu.roll(t, HEAD_DIM, 1)
    zero = jnp.zeros_like(t)
    return (jnp.where(lo, t, zero), jnp.where(lo, zero, tr),
            jnp.where(lo, tr, zero), jnp.where(lo, zero, t))


def _store_variants(ref, t):
    for i, var in enumerate(_lane_variants(t)):
        ref[0, :, i * LANES:(i + 1) * LANES] = var.astype(BF16)


def _inproj_kernel(x_ref, mod_ref, g_ref, w_ref, cos_ref, sin_ref, q_ref, k_ref, v_ref, p_ref):
    h = _norm_mod(x_ref[0], g_ref[...], mod_ref[0, 0:1, :], mod_ref[0, 1:2, :])
    z = _dot(h.astype(BF16), w_ref[...])
    cos = cos_ref[...]
    sin = sin_ref[...]
    lane = lax.broadcasted_iota(jnp.int32, cos.shape, 1)
    first_half = (lane & 16) == 0

    def rope(zc):
        partner = jnp.where(first_half, pltpu.roll(zc, LANES - 16, 1), pltpu.roll(zc, 16, 1))
        return zc * cos + partner * sin

    scale = HEAD_DIM ** -0.5
    for c in range(ATTN_WIDTH // LANES):
        q_ref[0, :, c * LANES:(c + 1) * LANES] = (rope(z[:, c * LANES:(c + 1) * LANES]) * scale).astype(BF16)
    _store_variants(k_ref, rope(z[:, ATTN_WIDTH:ATTN_WIDTH + KV_WIDTH]))
    _store_variants(v_ref, z[:, ATTN_WIDTH + KV_WIDTH:ATTN_WIDTH + 2 * KV_WIDTH])
    p_ref[0] = z[:, ATTN_WIDTH + 2 * KV_WIDTH:]


def _inproj(x, mod, g1, w_in_bf, cos_t, sin_t):
    b, s, d = x.shape
    tm = TM_PROJ
    return pl.pallas_call(
        _inproj_kernel,
        out_shape=(jax.ShapeDtypeStruct((b, s, ATTN_WIDTH), BF16),
                   jax.ShapeDtypeStruct((b, s, 4 * LANES), BF16),
                   jax.ShapeDtypeStruct((b, s, 4 * LANES), BF16),
                   jax.ShapeDtypeStruct((b, s, POOL_WIDTH), F32)),
        grid=(s // tm, b),
        in_specs=[pl.BlockSpec((1, tm, d), lambda n, bi: (bi, n, 0)),
                  pl.BlockSpec((1, 6, d), lambda n, bi: (bi, 0, 0)),
                  pl.BlockSpec((1, d), lambda n, bi: (0, 0)),
                  pl.BlockSpec((d, IN_COLS), lambda n, bi: (0, 0)),
                  pl.BlockSpec((tm, LANES), lambda n, bi: (n, 0)),
                  pl.BlockSpec((tm, LANES), lambda n, bi: (n, 0))],
        out_specs=(pl.BlockSpec((1, tm, ATTN_WIDTH), lambda n, bi: (bi, n, 0)),
                   pl.BlockSpec((1, tm, 4 * LANES), lambda n, bi: (bi, n, 0)),
                   pl.BlockSpec((1, tm, 4 * LANES), lambda n, bi: (bi, n, 0)),
                   pl.BlockSpec((1, tm, POOL_WIDTH), lambda n, bi: (bi, n, 0))),
        compiler_params=pltpu.CompilerParams(vmem_limit_bytes=VMEM_LIMIT),
        name="inproj",
    )(x, mod, g1, w_in_bf, cos_t, sin_t)


def _ctxproj_kernel(x_ref, mod_ref, g_ref, w_ref, k_ref, v_ref):
    h = _norm_mod(x_ref[0], g_ref[...], mod_ref[0, 0:1, :], mod_ref[0, 1:2, :])
    z = _dot(h.astype(BF16), w_ref[...])
    _store_variants(k_ref, z[:, :KV_WIDTH])
    _store_variants(v_ref, z[:, KV_WIDTH:])


def _ctxproj(ctx, mod_c, g1, w_kv_bf):
    b, c, d = ctx.shape
    return pl.pallas_call(
        _ctxproj_kernel,
        out_shape=(jax.ShapeDtypeStruct((b, c, 4 * LANES), BF16),
                   jax.ShapeDtypeStruct((b, c, 4 * LANES), BF16)),
        grid=(b,),
        in_specs=[pl.BlockSpec((1, c, d), lambda bi: (bi, 0, 0)),
                  pl.BlockSpec((1, 6, d), lambda bi: (0, 0, 0)),
                  pl.BlockSpec((1, d), lambda bi: (0, 0)),
                  pl.BlockSpec((d, 2 * KV_WIDTH), lambda bi: (0, 0))],
        out_specs=(pl.BlockSpec((1, c, 4 * LANES), lambda bi: (bi, 0, 0)),
                   pl.BlockSpec((1, c, 4 * LANES), lambda bi: (bi, 0, 0))),
        compiler_params=pltpu.CompilerParams(vmem_limit_bytes=VMEM_LIMIT),
        name="ctxproj",
    )(ctx, mod_c, g1, w_kv_bf)


def _attn_kernel(seq_len, sink_ref, q_ref, k_ref, kp_ref, kn_ref, v_ref, vp_ref, vn_ref, kc_ref, vc_ref,
                 p_ref, pp_ref, pn_ref, x_ref, mod_ref, band_ref, poolw_ref, pscale_ref, wout_ref,
                 g2_ref, wrh_ref, wrl_ref, x1_ref, h2_ref, lg_ref, kwin, vwin, pext, mix):
    n = pl.program_id(1)
    n_last = pl.num_programs(1) - 1

    kwin[0:QB, :] = kp_ref[0]
    kwin[QB:QB + TQ, :] = k_ref[0]
    kwin[QB + TQ:, :] = kn_ref[0]
    vwin[0:QB, :] = vp_ref[0]
    vwin[QB:QB + TQ, :] = v_ref[0]
    vwin[QB + TQ:, :] = vn_ref[0]

    pext[0:QB - 8, :] = jnp.zeros((QB - 8, POOL_WIDTH), F32)
    pext[QB - 8:QB, :] = jnp.where(n > 0, pp_ref[0], 0.0)
    pext[QB:QB + TQ, :] = p_ref[0]
    pext[QB + TQ:QB + TQ + 8, :] = jnp.where(n < n_last, pn_ref[0], 0.0)
    pext[QB + TQ + 8:, :] = jnp.zeros((QB - 8, POOL_WIDTH), F32)

    row = lax.broadcasted_iota(jnp.int32, (QB, 3 * QB), 0)
    col = lax.broadcasted_iota(jnp.int32, (QB, 3 * QB), 1)
    in_band = (col >= row) & (col <= row + 2 * WINDOW)
    tok = lax.broadcasted_iota(jnp.int32, (QB, 1), 0)
    kc = kc_ref[0]
    vc = vc_ref[0]

    def sub_block(j, carry):
        r0 = pl.multiple_of(j * QB, QB)
        qj = q_ref[0, pl.ds(r0, QB), :]
        kw = kwin[pl.ds(r0, 3 * QB), :]
        vw = vwin[pl.ds(r0, 3 * QB), :]
        kpos = col + (n * TQ + j * QB - QB)
        ok = in_band & (kpos >= 0) & (kpos < seq_len)
        bias = jnp.where(ok, 0.0, -jnp.inf)
        for c in range(N_HEADS // 2):
            qc = qj[:, c * LANES:(c + 1) * LANES]
            pair = None
            for par in range(2):
                head = 2 * c + par
                var = 2 * (head // (N_HEADS // N_KV_HEADS)) + par
                sl = slice(var * LANES, (var + 1) * LANES)
                s_loc = _dot_nt(qc, kw[:, sl]) + bias
                s_ctx = _dot_nt(qc, kc[:, sl])
                sink = sink_ref[head]
                m = jnp.maximum(jnp.maximum(jnp.max(s_loc, axis=1, keepdims=True),
                                            jnp.max(s_ctx, axis=1, keepdims=True)), sink)
                p_loc = jnp.exp(s_loc - m)
                p_ctx = jnp.exp(s_ctx - m)
                denom = (jnp.sum(p_loc, axis=1, keepdims=True) + jnp.sum(p_ctx, axis=1, keepdims=True)
                         + jnp.exp(sink - m))
                o = _dot(p_loc.astype(BF16), vw[:, sl]) + _dot(p_ctx.astype(BF16), vc[:, sl])
                o = o * (1.0 / denom)
                pair = o if pair is None else pair + o
            mix[pl.ds(r0, QB), c * LANES:(c + 1) * LANES] = pair.astype(BF16)

        slab = pext[pl.ds(pl.multiple_of(r0 + POOL_OFF, 8), POOL_SLAB), :]
        tpos = tok + (n * TQ + j * QB)
        for g, w in enumerate(POOL_WINDOWS):
            sg = slab[:, g * LANES:(g + 1) * LANES]
            hi, lo = _split_bf16(sg)
            band = band_ref[g]
            wsum = _dot(band, hi) + _dot(band, lo)
            cnt = (jnp.minimum(tpos - w // 2 + w, seq_len) - jnp.maximum(tpos - w // 2, 0)).astype(F32)
            dlt = wsum / cnt - sg[POOL_OFF:POOL_OFF + QB, :]
            y = _dot(dlt.astype(BF16), poolw_ref[g]) * pscale_ref[:, g * LANES:(g + 1) * LANES]
            mix[pl.ds(r0, QB), ATTN_WIDTH + g * LANES:ATTN_WIDTH + (g + 1) * LANES] = y.astype(BF16)
        return carry

    lax.fori_loop(0, TQ // QB, sub_block, 0)

    proj = _dot(mix[...], wout_ref[...])
    x1 = x_ref[0] + mod_ref[0, 2:3, :] * proj
    x1_ref[0] = x1
    h2 = _norm_mod(x1, g2_ref[...], mod_ref[0, 3:4, :], mod_ref[0, 4:5, :])
    _store_row_tiles(h2_ref, h2)
    h_hi, h_lo = _split_bf16(h2)
    wrh = wrh_ref[...]
    lg_ref[...] = _dot_nt(wrh, h_hi) + _dot_nt(wrh, h_lo) + _dot_nt(wrl_ref[...], h_hi)


def _attn(sink, q, k4, v4, kc4, vc4, p, x, mod, band, poolw_bf, pscale, wout_bf, g2, wr_hi, wr_lo):
    b, s, d = x.shape
    c = kc4.shape[1]
    nt = s // TQ
    hb = TQ // QB
    pb = TQ // 8
    kv_main = pl.BlockSpec((1, TQ, 4 * LANES), lambda bi, n: (bi, n, 0))
    kv_prev = pl.BlockSpec((1, QB, 4 * LANES), lambda bi, n: (bi, jnp.maximum(n * hb - 1, 0), 0))
    kv_next = pl.BlockSpec((1, QB, 4 * LANES), lambda bi, n: (bi, jnp.minimum((n + 1) * hb, s // QB - 1), 0))
    const2 = lambda bi, n: (0, 0)
    const3 = lambda bi, n: (0, 0, 0)
    return pl.pallas_call(
        functools.partial(_attn_kernel, s),
        out_shape=(jax.ShapeDtypeStruct((b, s, d), F32),
                   jax.ShapeDtypeStruct((b * s * ROW_TILE, LANES), F32),
                   jax.ShapeDtypeStruct((N_EXPERTS, b * s), F32)),
        grid=(b, nt),
        in_specs=[pl.BlockSpec(memory_space=pltpu.SMEM),
                  pl.BlockSpec((1, TQ, ATTN_WIDTH), lambda bi, n: (bi, n, 0)),
                  kv_main, kv_prev, kv_next, kv_main, kv_prev, kv_next,
                  pl.BlockSpec((1, c, 4 * LANES), lambda bi, n: (bi, 0, 0)),
                  pl.BlockSpec((1, c, 4 * LANES), lambda bi, n: (bi, 0, 0)),
                  pl.BlockSpec((1, TQ, POOL_WIDTH), lambda bi, n: (bi, n, 0)),
                  pl.BlockSpec((1, 8, POOL_WIDTH), lambda bi, n: (bi, jnp.maximum(n * pb - 1, 0), 0)),
                  pl.BlockSpec((1, 8, POOL_WIDTH), lambda bi, n: (bi, jnp.minimum((n + 1) * pb, s // 8 - 1), 0)),
                  pl.BlockSpec((1, TQ, d), lambda bi, n: (bi, n, 0)),
                  pl.BlockSpec((1, 6, d), lambda bi, n: (bi, 0, 0)),
                  pl.BlockSpec((len(POOL_WINDOWS), QB, POOL_SLAB), const3),
                  pl.BlockSpec((len(POOL_WINDOWS), POOL_GROUP_DIM, POOL_GROUP_DIM), const3),
                  pl.BlockSpec((1, POOL_WIDTH), const2),
                  pl.BlockSpec((d, d), const2),
                  pl.BlockSpec((1, d), const2),
                  pl.BlockSpec((N_EXPERTS, d), const2),
                  pl.BlockSpec((N_EXPERTS, d), const2)],
        out_specs=(pl.BlockSpec((1, TQ, d), lambda bi, n: (bi, n, 0)),
                   pl.BlockSpec((TQ * ROW_TILE, LANES), lambda bi, n: (bi * nt + n, 0)),
                   pl.BlockSpec((N_EXPERTS, TQ), lambda bi, n: (0, bi * nt + n))),
        scratch_shapes=[pltpu.VMEM((TQ + 2 * QB, 4 * LANES), BF16),
                        pltpu.VMEM((TQ + 2 * QB, 4 * LANES), BF16),
                        pltpu.VMEM((TQ + 2 * QB, POOL_WIDTH), F32),
                        pltpu.VMEM((TQ, d), BF16)],
        compiler_params=pltpu.CompilerParams(vmem_limit_bytes=VMEM_LIMIT),
        name="attn",
    )(sink, q, k4, k4, k4, v4, v4, v4, kc4, vc4, p, p, p, x, mod, band, poolw_bf, pscale, wout_bf,
      g2, wr_hi, wr_lo)


def _first_argmax_rows(v, row_iota, n_rows):
    m = jnp.max(v, axis=0, keepdims=True)
    idx = jnp.min(jnp.where(v == m, row_iota, n_rows), axis=0, keepdims=True)
    return m, idx


def _route_kernel(lg_ref, bias_ref, tri_ref, idx_ref, gate_ref, rank_ref, cnt_ref, carry):
    i = pl.program_id(0)

    @pl.when(i == 0)
    def _():
        carry[...] = jnp.zeros_like(carry)

    scores = 1.0 / (1.0 + jnp.exp(-lg_ref[...]))
    biased = scores + bias_ref[...]
    e_iota = lax.broadcasted_iota(jnp.int32, scores.shape, 0).astype(F32)
    g_iota = lax.broadcasted_iota(jnp.int32, (EXPERTS_PER_GROUP, TR), 0).astype(F32)
    neg = -jnp.inf

    grp = []
    for g in range(N_EXPERT_GROUPS):
        blk = biased[g * EXPERTS_PER_GROUP:(g + 1) * EXPERTS_PER_GROUP, :]
        m1, i1 = _first_argmax_rows(blk, g_iota, float(EXPERTS_PER_GROUP))
        m2 = jnp.max(jnp.where(g_iota == i1, neg, blk), axis=0, keepdims=True)
        grp.append(m1 + m2)
    grp = jnp.concatenate(grp, axis=0)
    gg_iota = lax.broadcasted_iota(jnp.int32, grp.shape, 0).astype(F32)
    grp_sel = jnp.zeros(grp.shape, F32)
    for _ in range(TOPK_GROUPS):
        _, gi = _first_argmax_rows(grp, gg_iota, float(N_EXPERT_GROUPS))
        hit = gg_iota == gi
        grp_sel = jnp.where(hit, 1.0, grp_sel)
        grp = jnp.where(hit, neg, grp)
    allowed = jnp.concatenate(
        [jnp.broadcast_to(grp_sel[g:g + 1, :], (EXPERTS_PER_GROUP, TR)) for g in range(N_EXPERT_GROUPS)], axis=0)
    masked = jnp.where(allowed > 0.5, biased, neg)

    idxs, gates = [], []
    onehot = jnp.zeros(scores.shape, F32)
    for _ in range(TOP_K):
        _, ei = _first_argmax_rows(masked, e_iota, float(N_EXPERTS))
        hit = e_iota == ei
        idxs.append(ei)
        gates.append(jnp.sum(jnp.where(hit, scores, 0.0), axis=0, keepdims=True))
        onehot = jnp.where(hit, 1.0, onehot)
        masked = jnp.where(hit, neg, masked)
    idx = jnp.concatenate(idxs, axis=0)
    gate = jnp.concatenate(gates, axis=0)
    gate = gate / jnp.sum(gate, axis=0, keepdims=True) * ROUTED_SCALE

    before = _dot(onehot.astype(BF16), tri_ref[...]) + carry[:, 0:1]
    ranks = [jnp.sum(jnp.where(e_iota == idxs[k], before, 0.0), axis=0, keepdims=True) for k in range(TOP_K)]
    idx_ref[...] = idx.astype(jnp.int32)
    gate_ref[...] = gate
    rank_ref[...] = jnp.concatenate(ranks, axis=0).astype(jnp.int32)
    total = carry[...] + jnp.sum(onehot, axis=1, keepdims=True)
    carry[...] = total
    cnt_ref[...] = total


def _route(lg_t, bias, tri):
    e, t = lg_t.shape
    tok = pl.BlockSpec((TOP_K, TR), lambda i: (0, i))
    return pl.pallas_call(
        _route_kernel,
        out_shape=(jax.ShapeDtypeStruct((TOP_K, t), jnp.int32),
                   jax.ShapeDtypeStruct((TOP_K, t), F32),
                   jax.ShapeDtypeStruct((TOP_K, t), jnp.int32),
                   jax.ShapeDtypeStruct((e, LANES), F32)),
        grid=(t // TR,),
        in_specs=[pl.BlockSpec((e, TR), lambda i: (0, i)),
                  pl.BlockSpec((e, 1), lambda i: (0, 0)),
                  pl.BlockSpec((TR, TR), lambda i: (0, 0))],
        out_specs=(tok, tok, tok, pl.BlockSpec((e, LANES), lambda i: (0, 0))),
        scratch_shapes=[pltpu.VMEM((e, LANES), F32)],
        compiler_params=pltpu.CompilerParams(vmem_limit_bytes=VMEM_LIMIT),
        name="route",
    )(lg_t, bias, tri)


def _tile_rows(ref, row):
    return ref.at[pl.ds(pl.multiple_of(row * ROW_TILE, ROW_TILE), ROW_TILE), :]


def _dispatch_kernel(dest_hbm, h_ref, xs_hbm, idx_smem, sem_idx, sem_rows):
    i = pl.program_id(0)
    n_idx = TD * TOP_K
    cp = pltpu.make_async_copy(dest_hbm.at[pl.ds(i * n_idx, n_idx)], idx_smem, sem_idx)
    cp.start()
    cp.wait()

    def issue(t, carry):
        src = _tile_rows(h_ref, t)
        for k in range(TOP_K):
            pltpu.make_async_copy(src, _tile_rows(xs_hbm, idx_smem[t * TOP_K + k]), sem_rows).start()
        return carry

    lax.fori_loop(0, TD, issue, 0, unroll=2)
    for _ in range(TOP_K):
        pltpu.make_async_copy(h_ref, xs_hbm.at[pl.ds(0, TD * ROW_TILE), :], sem_rows).wait()


def _dispatch(dest_flat, h2t, n_rows):
    t = h2t.shape[0] // ROW_TILE
    return pl.pallas_call(
        _dispatch_kernel,
        out_shape=jax.ShapeDtypeStruct((n_rows * ROW_TILE, LANES), F32),
        grid=(t // TD,),
        in_specs=[pl.BlockSpec(memory_space=pl.ANY),
                  pl.BlockSpec((TD * ROW_TILE, LANES), lambda i: (i, 0))],
        out_specs=pl.BlockSpec(memory_space=pl.ANY),
        scratch_shapes=[pltpu.SMEM((TD * TOP_K,), jnp.int32),
                        pltpu.SemaphoreType.DMA, pltpu.SemaphoreType.DMA],
        compiler_params=pltpu.CompilerParams(has_side_effects=True, disable_bounds_checks=True),
        name="dispatch",
    )(dest_flat, h2t)


def _experts_kernel(be_ref, nact_ref, xs_ref, wg_ref, wu_ref, wd_ref, ys_ref, wg_bf, wu_bf, wd_bf):
    i = pl.program_id(0)
    prev = be_ref[jnp.maximum(i - 1, 0)]

    @pl.when((i == 0) | (be_ref[i] != prev))
    def _():
        wg_bf[...] = wg_ref[0].astype(BF16)
        wu_bf[...] = wu_ref[0].astype(BF16)
        wd_bf[...] = wd_ref[0].astype(BF16)

    @pl.when(i < nact_ref[0])
    def _():
        xb = _load_row_tiles(xs_ref, TM_EXP).astype(BF16)
        hid = _silu(_dot(xb, wg_bf[...])) * _dot(xb, wu_bf[...])
        _store_row_tiles(ys_ref, _dot(hid.astype(BF16), wd_bf[...]))


def _experts(block_e, n_active, xs, w_gate, w_up, w_down):
    d = w_gate.shape[1]
    nb = xs.shape[0] // (TM_EXP * ROW_TILE)
    rows = pl.BlockSpec((TM_EXP * ROW_TILE, LANES), lambda i, be, na: (jnp.minimum(i, na[0] - 1), 0))
    grid_spec = pltpu.PrefetchScalarGridSpec(
        num_scalar_prefetch=2,
        grid=(nb,),
        in_specs=[rows,
                  pl.BlockSpec((1, d, D_EXPERT), lambda i, be, na: (be[i], 0, 0)),
                  pl.BlockSpec((1, d, D_EXPERT), lambda i, be, na: (be[i], 0, 0)),
                  pl.BlockSpec((1, D_EXPERT, d), lambda i, be, na: (be[i], 0, 0))],
        out_specs=rows,
        scratch_shapes=[pltpu.VMEM((d, D_EXPERT), BF16), pltpu.VMEM((d, D_EXPERT), BF16),
                        pltpu.VMEM((D_EXPERT, d), BF16)],
    )
    return pl.pallas_call(
        _experts_kernel,
        out_shape=jax.ShapeDtypeStruct(xs.shape, F32),
        grid_spec=grid_spec,
        compiler_params=pltpu.CompilerParams(vmem_limit_bytes=VMEM_LIMIT),
        name="experts",
    )(block_e, n_active, xs, w_gate, w_up, w_down)


def _combine_kernel(dest_hbm, ys_hbm, x1_ref, h2_ref, gate_ref, mod_ref, wsg_ref, wsu_ref, wsd_ref, fg_ref,
                    out_ref, idx_smem, buf, sem_idx, sem_rows):
    i = pl.program_id(0)
    n_idx = TF * TOP_K
    cp = pltpu.make_async_copy(dest_hbm.at[pl.ds(i * n_idx, n_idx)], idx_smem, sem_idx)
    cp.start()
    cp.wait()

    def issue(t, carry):
        for k in range(TOP_K):
            pltpu.make_async_copy(_tile_rows(ys_hbm, idx_smem[t * TOP_K + k]), _tile_rows(buf, k * TF + t),
                                  sem_rows).start()
        return carry

    lax.fori_loop(0, TF, issue, 0, unroll=2)

    hb = _load_row_tiles(h2_ref, TF).astype(BF16)
    hid = _silu(_dot(hb, wsg_ref[...])) * _dot(hb, wsu_ref[...])
    ffn = _dot(hid.astype(BF16), wsd_ref[...])

    pltpu.make_async_copy(ys_hbm.at[pl.ds(0, TOP_K * TF * ROW_TILE), :], buf, sem_rows).wait()
    gate = gate_ref[...]
    for k in range(TOP_K):
        ffn = ffn + gate[:, k:k + 1] * _load_row_tiles(buf, TF, base=k * TF * ROW_TILE)
    x2 = x1_ref[...] + mod_ref[0, 5:6, :] * ffn
    ms = jnp.mean(x2 * x2, axis=-1, keepdims=True)
    out_ref[...] = x2 * lax.rsqrt(ms + EPS) * fg_ref[...]


def _combine(dest_flat, ys, x1, h2t, gate_tk, mod, wsg_bf, wsu_bf, wsd_bf, final_g, seq_len):
    t, d = x1.shape
    tiles_per_seq = seq_len // TF
    tok = pl.BlockSpec((TF, d), lambda i: (i, 0))
    return pl.pallas_call(
        _combine_kernel,
        out_shape=jax.ShapeDtypeStruct((t, d), F32),
        grid=(t // TF,),
        in_specs=[pl.BlockSpec(memory_space=pl.ANY), pl.BlockSpec(memory_space=pl.ANY),
                  tok, pl.BlockSpec((TF * ROW_TILE, LANES), lambda i: (i, 0)),
                  pl.BlockSpec((TF, TOP_K), lambda i: (i, 0)),
                  pl.BlockSpec((1, 6, d), lambda i: (i // tiles_per_seq, 0, 0)),
                  pl.BlockSpec((d, D_SHARED), lambda i: (0, 0)),
                  pl.BlockSpec((d, D_SHARED), lambda i: (0, 0)),
                  pl.BlockSpec((D_SHARED, d), lambda i: (0, 0)),
                  pl.BlockSpec((1, d), lambda i: (0, 0))],
        out_specs=tok,
        scratch_shapes=[pltpu.SMEM((TF * TOP_K,), jnp.int32),
                        pltpu.VMEM((TOP_K * TF * ROW_TILE, LANES), F32),
                        pltpu.SemaphoreType.DMA, pltpu.SemaphoreType.DMA],
        compiler_params=pltpu.CompilerParams(vmem_limit_bytes=VMEM_LIMIT, disable_bounds_checks=True),
        name="combine",
    )(dest_flat, ys, x1, h2t, gate_tk, mod, wsg_bf, wsu_bf, wsd_bf, final_g)


def _rope_tables(n_tokens):
    n_rows = n_tokens // GRID_W
    n_freq = HEAD_DIM // 4
    inv_freq = ROPE_THETA ** (-jnp.arange(n_freq, dtype=F32) / n_freq)
    ang_r = jnp.arange(n_rows).astype(F32)[:, None] * inv_freq[None, :]
    ang_c = jnp.arange(GRID_W).astype(F32)[:, None] * inv_freq[None, :]

    def per_token(row_part, col_part):
        rows = jnp.broadcast_to(row_part[:, None, :], (n_rows, GRID_W, n_freq))
        cols = jnp.broadcast_to(col_part[None, :, :], (n_rows, GRID_W, n_freq))
        return rows.reshape(n_tokens, n_freq), cols.reshape(n_tokens, n_freq)

    cos_r, cos_c = per_token(jnp.cos(ang_r), jnp.cos(ang_c))
    sin_r, sin_c = per_token(jnp.sin(ang_r), jnp.sin(ang_c))
    cos = jnp.concatenate([cos_r, cos_r, cos_c, cos_c], axis=1)
    sin = jnp.concatenate([-sin_r, sin_r, -sin_c, sin_c], axis=1)
    reps = LANES // HEAD_DIM
    return jnp.tile(cos, (1, reps)), jnp.tile(sin, (1, reps))


def _pool_bands():
    i = jnp.arange(QB)[:, None]
    r = jnp.arange(POOL_SLAB)[None, :]
    return jnp.stack([((r >= i + POOL_OFF - w // 2) & (r < i + POOL_OFF + w // 2)).astype(BF16)
                      for w in POOL_WINDOWS])


def kernel(x, c, ctx, c_ctx, w_ada, b_ada, norm1_g, norm2_g, w_in, attn_sink, pool_w, pool_scale, w_out,
           w_router, router_bias, w_gate, w_up, w_down, ws_gate, ws_up, ws_down, final_g):
    b, s, d = x.shape
    t = b * s
    assert w_ada.shape[0] == 1 and d == D_MODEL and s % TQ == 0 and b + 1 <= 8

    c8 = jnp.zeros((8, d), F32).at[:b].set(c).at[b].set(c_ctx)
    mod = _ada(c8, w_ada[0], b_ada[0]).reshape(8, 6, d)
    g1 = norm1_g[0].reshape(1, d)
    g2 = norm2_g[0].reshape(1, d)
    w_in_bf = w_in[0].astype(BF16)
    cos_t, sin_t = _rope_tables(s)

    q, k4, v4, p = _inproj(x, mod, g1, w_in_bf, cos_t, sin_t)
    kc4, vc4 = _ctxproj(ctx, mod[b:b + 1], g1, w_in_bf[:, ATTN_WIDTH:ATTN_WIDTH + 2 * KV_WIDTH])

    wr_t = w_router[0].T
    wr_hi = wr_t.astype(BF16)
    wr_lo = (wr_t - wr_hi.astype(F32)).astype(BF16)
    x1, h2t, lg_t = _attn(attn_sink[0], q, k4, v4, kc4, vc4, p, x, mod, _pool_bands(),
                         pool_w[0].astype(BF16), pool_scale[0].reshape(1, POOL_WIDTH), w_out[0].astype(BF16),
                         g2, wr_hi, wr_lo)

    tri = jnp.triu(jnp.ones((TR, TR), BF16), k=1)
    idx_kt, gate_kt, rank_kt, counts = _route(lg_t, router_bias[0].reshape(N_EXPERTS, 1), tri)

    sizes = counts[:, 0].astype(jnp.int32)
    padded = (sizes + TM_EXP - 1) // TM_EXP * TM_EXP
    pad_end = jnp.cumsum(padded)
    pad_start = pad_end - padded
    n_rows = -(-(t * TOP_K + N_EXPERTS * (TM_EXP - 1)) // TM_EXP) * TM_EXP
    n_blocks = n_rows // TM_EXP
    experts = jnp.arange(N_EXPERTS, dtype=jnp.int32)
    start_of = jnp.sum(jnp.where(idx_kt[None] == experts[:, None, None], pad_start[:, None, None], 0), axis=0)
    dest_flat = (start_of + rank_kt).T.reshape(t * TOP_K)
    block_row = jnp.arange(n_blocks, dtype=jnp.int32) * TM_EXP
    block_e = jnp.minimum(jnp.sum((pad_end[None, :] <= block_row[:, None]).astype(jnp.int32), axis=1),
                          N_EXPERTS - 1)
    n_active = (pad_end[-1:] // TM_EXP).astype(jnp.int32)

    xs = _dispatch(dest_flat, h2t, n_rows)
    ys = _experts(block_e, n_active, xs, w_gate[0], w_up[0], w_down[0])
    out = _combine(dest_flat, ys, x1.reshape(t, d), h2t, gate_kt.T, mod,
                   ws_gate[0].astype(BF16), ws_up[0].astype(BF16), ws_down[0].astype(BF16),
                   final_g.reshape(1, d), s)
    return out.reshape(b, s, d)
```

```python
import functools

import jax
import jax.numpy as jnp
from jax import lax
from jax.experimental import pallas as pl
from jax.experimental.pallas import tpu as pltpu

F32 = jnp.float32
BF16 = jnp.bfloat16

D_MODEL = 1024
GRID_W = 64
N_HEADS = 8
N_KV_HEADS = 2
HEAD_DIM = 64
ATTN_WIDTH = N_HEADS * HEAD_DIM
KV_WIDTH = N_KV_HEADS * HEAD_DIM
WINDOW = 128
ROPE_THETA = 10000.0
POOL_WINDOWS = (2, 4, 8, 16)
POOL_WIDTH = D_MODEL - ATTN_WIDTH
POOL_GROUP_DIM = POOL_WIDTH // len(POOL_WINDOWS)
IN_COLS = ATTN_WIDTH + 2 * KV_WIDTH + POOL_WIDTH
N_EXPERTS = 64
TOP_K = 8
N_EXPERT_GROUPS = 8
EXPERTS_PER_GROUP = N_EXPERTS // N_EXPERT_GROUPS
TOPK_GROUPS = 4
D_EXPERT = 256
D_SHARED = 256
ROUTED_SCALE = 2.5
EPS = 1e-6
LOG2E = 1.4426950408889634

LANES = 128
ROW_TILE = D_MODEL // LANES
VMEM_LIMIT = 48 * 1024 * 1024

TM_PROJ = 512
TQ = 512
QB = 128
POOL_SLAB = 256
POOL_OFF = 64
TR = 512
TM_EXP = 256
TD = 256
TF = 256


def _silu(x):
    return x * (1.0 / (1.0 + jnp.exp(-x)))


def _split_bf16(x):
    hi = x.astype(BF16)
    lo = (x - hi.astype(F32)).astype(BF16)
    return hi, lo


def _dot(a, b):
    return jnp.dot(a, b, preferred_element_type=F32)


def _store_row_tiles(ref, val):
    rows = val.shape[0]
    for c in range(ROW_TILE):
        ref[pl.ds(c, rows, stride=ROW_TILE), :] = val[:, c * LANES:(c + 1) * LANES]


def _load_row_tiles(ref, rows, base=0):
    return jnp.concatenate([ref[pl.ds(base + c, rows, stride=ROW_TILE), :] for c in range(ROW_TILE)], axis=1)


def _dot_nt(a, b):
    return lax.dot_general(a, b, (((1,), (1,)), ((), ())), preferred_element_type=F32)


def _ada_kernel(c_ref, w_ref, b_ref, o_ref):
    a_hi, a_lo = _split_bf16(_silu(c_ref[...]))
    w_hi, w_lo = _split_bf16(w_ref[...])
    o_ref[...] = _dot(a_hi, w_hi) + _dot(a_lo, w_hi) + _dot(a_hi, w_lo) + b_ref[...]


def _ada(c8, w_ada, b_ada):
    d = c8.shape[1]
    n = w_ada.shape[1]
    tn = 512
    return pl.pallas_call(
        _ada_kernel,
        out_shape=jax.ShapeDtypeStruct((8, n), F32),
        grid=(n // tn,),
        in_specs=[pl.BlockSpec((8, d), lambda j: (0, 0)),
                  pl.BlockSpec((d, tn), lambda j: (0, j)),
                  pl.BlockSpec((1, tn), lambda j: (0, j))],
        out_specs=pl.BlockSpec((8, tn), lambda j: (0, j)),
        compiler_params=pltpu.CompilerParams(vmem_limit_bytes=VMEM_LIMIT),
        name="ada",
    )(c8, w_ada, b_ada.reshape(1, n))


def _norm_mod(x, g, shift, scale):
    ms = jnp.mean(x * x, axis=-1, keepdims=True)
    return (x * lax.rsqrt(ms + EPS) * g) * (1.0 + scale) + shift


def _lane_variants(t):
    lane = lax.broadcasted_iota(jnp.int32, t.shape, 1)
    lo = lane < HEAD_DIM
    tr = pltpu.roll(t, HEAD_DIM, 1)
    zero = jnp.zeros_like(t)
    return (jnp.where(lo, t, zero), jnp.where(lo, zero, tr),
            jnp.where(lo, tr, zero), jnp.where(lo, zero, t))


def _store_variants(ref, t):
    for i, var in enumerate(_lane_variants(t)):
        ref[0, :, i * LANES:(i + 1) * LANES] = var.astype(BF16)


def _inproj_kernel(x_ref, mod_ref, g_ref, w_ref, cos_ref, sin_ref, q_ref, k_ref, v_ref, p_ref):
    h = _norm_mod(x_ref[0], g_ref[...], mod_ref[0, 0:1, :], mod_ref[0, 1:2, :])
    z = _dot(h.astype(BF16), w_ref[...])
    cos = cos_ref[...]
    sin = sin_ref[...]
    lane = lax.broadcasted_iota(jnp.int32, cos.shape, 1)
    first_half = (lane & 16) == 0

    def rope(zc):
        partner = jnp.where(first_half, pltpu.roll(zc, LANES - 16, 1), pltpu.roll(zc, 16, 1))
        return zc * cos + partner * sin

    scale = HEAD_DIM ** -0.5 * LOG2E
    for c in range(ATTN_WIDTH // LANES):
        q_ref[0, :, c * LANES:(c + 1) * LANES] = (rope(z[:, c * LANES:(c + 1) * LANES]) * scale).astype(BF16)
    _store_variants(k_ref, rope(z[:, ATTN_WIDTH:ATTN_WIDTH + KV_WIDTH]))
    _store_variants(v_ref, z[:, ATTN_WIDTH + KV_WIDTH:ATTN_WIDTH + 2 * KV_WIDTH])
    p_ref[0] = z[:, ATTN_WIDTH + 2 * KV_WIDTH:]


def _inproj(x, mod, g1, w_in_bf, cos_t, sin_t):
    b, s, d = x.shape
    tm = TM_PROJ
    return pl.pallas_call(
        _inproj_kernel,
        out_shape=(jax.ShapeDtypeStruct((b, s, ATTN_WIDTH), BF16),
                   jax.ShapeDtypeStruct((b, s, 4 * LANES), BF16),
                   jax.ShapeDtypeStruct((b, s, 4 * LANES), BF16),
                   jax.ShapeDtypeStruct((b, s, POOL_WIDTH), F32)),
        grid=(s // tm, b),
        in_specs=[pl.BlockSpec((1, tm, d), lambda n, bi: (bi, n, 0)),
                  pl.BlockSpec((1, 6, d), lambda n, bi: (bi, 0, 0)),
                  pl.BlockSpec((1, d), lambda n, bi: (0, 0)),
                  pl.BlockSpec((d, IN_COLS), lambda n, bi: (0, 0)),
                  pl.BlockSpec((tm, LANES), lambda n, bi: (n, 0)),
                  pl.BlockSpec((tm, LANES), lambda n, bi: (n, 0))],
        out_specs=(pl.BlockSpec((1, tm, ATTN_WIDTH), lambda n, bi: (bi, n, 0)),
                   pl.BlockSpec((1, tm, 4 * LANES), lambda n, bi: (bi, n, 0)),
                   pl.BlockSpec((1, tm, 4 * LANES), lambda n, bi: (bi, n, 0)),
                   pl.BlockSpec((1, tm, POOL_WIDTH), lambda n, bi: (bi, n, 0))),
        compiler_params=pltpu.CompilerParams(vmem_limit_bytes=VMEM_LIMIT),
        name="inproj",
    )(x, mod, g1, w_in_bf, cos_t, sin_t)


def _ctxproj_kernel(x_ref, mod_ref, g_ref, w_ref, k_ref, v_ref):
    h = _norm_mod(x_ref[0], g_ref[...], mod_ref[0, 0:1, :], mod_ref[0, 1:2, :])
    z = _dot(h.astype(BF16), w_ref[...])
    _store_variants(k_ref, z[:, :KV_WIDTH])
    _store_variants(v_ref, z[:, KV_WIDTH:])


def _ctxproj(ctx, mod_c, g1, w_kv_bf):
    b, c, d = ctx.shape
    return pl.pallas_call(
        _ctxproj_kernel,
        out_shape=(jax.ShapeDtypeStruct((b, c, 4 * LANES), BF16),
                   jax.ShapeDtypeStruct((b, c, 4 * LANES), BF16)),
        grid=(b,),
        in_specs=[pl.BlockSpec((1, c, d), lambda bi: (bi, 0, 0)),
                  pl.BlockSpec((1, 6, d), lambda bi: (0, 0, 0)),
                  pl.BlockSpec((1, d), lambda bi: (0, 0)),
                  pl.BlockSpec((d, 2 * KV_WIDTH), lambda bi: (0, 0))],
        out_specs=(pl.BlockSpec((1, c, 4 * LANES), lambda bi: (bi, 0, 0)),
                   pl.BlockSpec((1, c, 4 * LANES), lambda bi: (bi, 0, 0))),
        compiler_params=pltpu.CompilerParams(vmem_limit_bytes=VMEM_LIMIT),
        name="ctxproj",
    )(ctx, mod_c, g1, w_kv_bf)


def _fold_lane_tiles(op, *arrays):
    tiles = [a[:, i * LANES:(i + 1) * LANES] for a in arrays for i in range(a.shape[1] // LANES)]
    while len(tiles) > 1:
        tiles = [op(tiles[i], tiles[i + 1]) if i + 1 < len(tiles) else tiles[i] for i in range(0, len(tiles), 2)]
    return tiles[0]


def _attn_kernel(seq_len, sink_ref, q_ref, k_ref, kp_ref, kn_ref, v_ref, vp_ref, vn_ref, kc_ref, vc_ref,
                 p_ref, pp_ref, pn_ref, x_ref, mod_ref, band_ref, poolw_ref, pscale_ref, wout_ref,
                 g2_ref, wrh_ref, wrl_ref, x1_ref, h2_ref, lg_ref, kwin, vwin, pext, mix, s_scr, p_scr, m_scr):
    n = pl.program_id(1)
    n_last = pl.num_programs(1) - 1

    kwin[0:QB, :] = kp_ref[0]
    kwin[QB:QB + TQ, :] = k_ref[0]
    kwin[QB + TQ:, :] = kn_ref[0]
    vwin[0:QB, :] = vp_ref[0]
    vwin[QB:QB + TQ, :] = v_ref[0]
    vwin[QB + TQ:, :] = vn_ref[0]

    pext[0:QB - 8, :] = jnp.zeros((QB - 8, POOL_WIDTH), F32)
    pext[QB - 8:QB, :] = jnp.where(n > 0, pp_ref[0], 0.0)
    pext[QB:QB + TQ, :] = p_ref[0]
    pext[QB + TQ:QB + TQ + 8, :] = jnp.where(n < n_last, pn_ref[0], 0.0)
    pext[QB + TQ + 8:, :] = jnp.zeros((QB - 8, POOL_WIDTH), F32)

    row = lax.broadcasted_iota(jnp.int32, (QB, 3 * QB), 0)
    col = lax.broadcasted_iota(jnp.int32, (QB, 3 * QB), 1)
    in_band = (col >= row) & (col <= row + 2 * WINDOW)
    tok = lax.broadcasted_iota(jnp.int32, (QB, 1), 0)
    kc = kc_ref[0]
    vc = vc_ref[0]

    def sub_block(j, carry):
        r0 = pl.multiple_of(j * QB, QB)
        qj = q_ref[0, pl.ds(r0, QB), :]
        kw = kwin[pl.ds(r0, 3 * QB), :]
        vw = vwin[pl.ds(r0, 3 * QB), :]
        kpos = col + (n * TQ + j * QB - QB)
        ok = in_band & (kpos >= 0) & (kpos < seq_len)
        bias = jnp.where(ok, 0.0, -jnp.inf)
        n_loc = 3 * QB

        def kv_lanes(head):
            var = 2 * (head // (N_HEADS // N_KV_HEADS)) + head % 2
            return slice(var * LANES, (var + 1) * LANES)

        for head in range(N_HEADS):
            qc = qj[:, (head // 2) * LANES:(head // 2 + 1) * LANES]
            s_scr[head, :, 0:n_loc] = _dot_nt(qc, kw[:, kv_lanes(head)]) + bias
            s_scr[head, :, n_loc:] = _dot_nt(qc, kc[:, kv_lanes(head)])
        n_tiles = (n_loc + kc.shape[0]) // LANES
        for head in range(N_HEADS):
            row_max = jnp.max(_fold_lane_tiles(jnp.maximum, s_scr[head]), axis=1, keepdims=True)
            m_scr[head] = jnp.broadcast_to(jnp.maximum(row_max, sink_ref[head] * LOG2E), (QB, LANES))
        for head in range(N_HEADS):
            m = m_scr[head]
            acc = None
            for i in range(n_tiles):
                p = jnp.exp2(s_scr[head, :, i * LANES:(i + 1) * LANES] - m)
                p_scr[head, :, i * LANES:(i + 1) * LANES] = p.astype(BF16)
                acc = p if acc is None else acc + p
            denom = (jnp.broadcast_to(jnp.sum(acc, axis=1, keepdims=True), (QB, LANES))
                     + jnp.exp2(sink_ref[head] * LOG2E - m))
            m_scr[head] = 1.0 / denom
        for c in range(N_HEADS // 2):
            pair = None
            for head in (2 * c, 2 * c + 1):
                o = (_dot(p_scr[head, :, 0:n_loc], vw[:, kv_lanes(head)])
                     + _dot(p_scr[head, :, n_loc:], vc[:, kv_lanes(head)])) * m_scr[head]
                pair = o if pair is None else pair + o
            mix[pl.ds(r0, QB), c * LANES:(c + 1) * LANES] = pair.astype(BF16)

        slab = pext[pl.ds(pl.multiple_of(r0 + POOL_OFF, 8), POOL_SLAB), :]
        tpos = tok + (n * TQ + j * QB)
        for g, w in enumerate(POOL_WINDOWS):
            sg = slab[:, g * LANES:(g + 1) * LANES]
            hi, lo = _split_bf16(sg)
            band = band_ref[g]
            wsum = _dot(band, hi) + _dot(band, lo)
            cnt = (jnp.minimum(tpos - w // 2 + w, seq_len) - jnp.maximum(tpos - w // 2, 0)).astype(F32)
            dlt = wsum / cnt - sg[POOL_OFF:POOL_OFF + QB, :]
            y = _dot(dlt.astype(BF16), poolw_ref[g]) * pscale_ref[:, g * LANES:(g + 1) * LANES]
            mix[pl.ds(r0, QB), ATTN_WIDTH + g * LANES:ATTN_WIDTH + (g + 1) * LANES] = y.astype(BF16)
        return carry

    lax.fori_loop(0, TQ // QB, sub_block, 0)

    proj = _dot(mix[...], wout_ref[...])
    x1 = x_ref[0] + mod_ref[0, 2:3, :] * proj
    x1_ref[0] = x1
    h2 = _norm_mod(x1, g2_ref[...], mod_ref[0, 3:4, :], mod_ref[0, 4:5, :])
    _store_row_tiles(h2_ref, h2)
    h_hi, h_lo = _split_bf16(h2)
    wrh = wrh_ref[...]
    lg_ref[...] = _dot_nt(wrh, h_hi) + _dot_nt(wrh, h_lo) + _dot_nt(wrl_ref[...], h_hi)


def _attn(sink, q, k4, v4, kc4, vc4, p, x, mod, band, poolw_bf, pscale, wout_bf, g2, wr_hi, wr_lo):
    b, s, d = x.shape
    c = kc4.shape[1]
    nt = s // TQ
    hb = TQ // QB
    pb = TQ // 8
    kv_main = pl.BlockSpec((1, TQ, 4 * LANES), lambda bi, n: (bi, n, 0))
    kv_prev = pl.BlockSpec((1, QB, 4 * LANES), lambda bi, n: (bi, jnp.maximum(n * hb - 1, 0), 0))
    kv_next = pl.BlockSpec((1, QB, 4 * LANES), lambda bi, n: (bi, jnp.minimum((n + 1) * hb, s // QB - 1), 0))
    const2 = lambda bi, n: (0, 0)
    const3 = lambda bi, n: (0, 0, 0)
    return pl.pallas_call(
        functools.partial(_attn_kernel, s),
        out_shape=(jax.ShapeDtypeStruct((b, s, d), F32),
                   jax.ShapeDtypeStruct((b * s * ROW_TILE, LANES), F32),
                   jax.ShapeDtypeStruct((N_EXPERTS, b * s), F32)),
        grid=(b, nt),
        in_specs=[pl.BlockSpec(memory_space=pltpu.SMEM),
                  pl.BlockSpec((1, TQ, ATTN_WIDTH), lambda bi, n: (bi, n, 0)),
                  kv_main, kv_prev, kv_next, kv_main, kv_prev, kv_next,
                  pl.BlockSpec((1, c, 4 * LANES), lambda bi, n: (bi, 0, 0)),
                  pl.BlockSpec((1, c, 4 * LANES), lambda bi, n: (bi, 0, 0)),
                  pl.BlockSpec((1, TQ, POOL_WIDTH), lambda bi, n: (bi, n, 0)),
                  pl.BlockSpec((1, 8, POOL_WIDTH), lambda bi, n: (bi, jnp.maximum(n * pb - 1, 0), 0)),
                  pl.BlockSpec((1, 8, POOL_WIDTH), lambda bi, n: (bi, jnp.minimum((n + 1) * pb, s // 8 - 1), 0)),
                  pl.BlockSpec((1, TQ, d), lambda bi, n: (bi, n, 0)),
                  pl.BlockSpec((1, 6, d), lambda bi, n: (bi, 0, 0)),
                  pl.BlockSpec((len(POOL_WINDOWS), QB, POOL_SLAB), const3),
                  pl.BlockSpec((len(POOL_WINDOWS), POOL_GROUP_DIM, POOL_GROUP_DIM), const3),
                  pl.BlockSpec((1, POOL_WIDTH), const2),
                  pl.BlockSpec((d, d), const2),
                  pl.BlockSpec((1, d), const2),
                  pl.BlockSpec((N_EXPERTS, d), const2),
                  pl.BlockSpec((N_EXPERTS, d), const2)],
        out_specs=(pl.BlockSpec((1, TQ, d), lambda bi, n: (bi, n, 0)),
                   pl.BlockSpec((TQ * ROW_TILE, LANES), lambda bi, n: (bi * nt + n, 0)),
                   pl.BlockSpec((N_EXPERTS, TQ), lambda bi, n: (0, bi * nt + n))),
        scratch_shapes=[pltpu.VMEM((TQ + 2 * QB, 4 * LANES), BF16),
                        pltpu.VMEM((TQ + 2 * QB, 4 * LANES), BF16),
                        pltpu.VMEM((TQ + 2 * QB, POOL_WIDTH), F32),
                        pltpu.VMEM((TQ, d), BF16),
                        pltpu.VMEM((N_HEADS, QB, 3 * QB + c), F32),
                        pltpu.VMEM((N_HEADS, QB, 3 * QB + c), BF16),
                        pltpu.VMEM((N_HEADS, QB, LANES), F32)],
        compiler_params=pltpu.CompilerParams(vmem_limit_bytes=VMEM_LIMIT),
        name="attn",
    )(sink, q, k4, k4, k4, v4, v4, v4, kc4, vc4, p, p, p, x, mod, band, poolw_bf, pscale, wout_bf,
      g2, wr_hi, wr_lo)


def _first_argmax_rows(v, row_iota, n_rows):
    m = jnp.max(v, axis=0, keepdims=True)
    idx = jnp.min(jnp.where(v == m, row_iota, n_rows), axis=0, keepdims=True)
    return m, idx


def _route_kernel(lg_ref, bias_ref, tri_ref, idx_ref, gate_ref, rank_ref, cnt_ref, carry):
    i = pl.program_id(0)

    @pl.when(i == 0)
    def _():
        carry[...] = jnp.zeros_like(carry)

    scores = 1.0 / (1.0 + jnp.exp(-lg_ref[...]))
    biased = scores + bias_ref[...]
    e_iota = lax.broadcasted_iota(jnp.int32, scores.shape, 0).astype(F32)
    g_iota = lax.broadcasted_iota(jnp.int32, (EXPERTS_PER_GROUP, TR), 0).astype(F32)
    neg = -jnp.inf

    grp = []
    for g in range(N_EXPERT_GROUPS):
        blk = biased[g * EXPERTS_PER_GROUP:(g + 1) * EXPERTS_PER_GROUP, :]
        m1, i1 = _first_argmax_rows(blk, g_iota, float(EXPERTS_PER_GROUP))
        m2 = jnp.max(jnp.where(g_iota == i1, neg, blk), axis=0, keepdims=True)
        grp.append(m1 + m2)
    grp = jnp.concatenate(grp, axis=0)
    gg_iota = lax.broadcasted_iota(jnp.int32, grp.shape, 0).astype(F32)
    grp_sel = jnp.zeros(grp.shape, F32)
    for _ in range(TOPK_GROUPS):
        _, gi = _first_argmax_rows(grp, gg_iota, float(N_EXPERT_GROUPS))
        hit = gg_iota == gi
        grp_sel = jnp.where(hit, 1.0, grp_sel)
        grp = jnp.where(hit, neg, grp)
    allowed = jnp.concatenate(
        [jnp.broadcast_to(grp_sel[g:g + 1, :], (EXPERTS_PER_GROUP, TR)) for g in range(N_EXPERT_GROUPS)], axis=0)
    masked = jnp.where(allowed > 0.5, biased, neg)

    idxs, gates = [], []
    onehot = jnp.zeros(scores.shape, F32)
    for _ in range(TOP_K):
        _, ei = _first_argmax_rows(masked, e_iota, float(N_EXPERTS))
        hit = e_iota == ei
        idxs.append(ei)
        gates.append(jnp.sum(jnp.where(hit, scores, 0.0), axis=0, keepdims=True))
        onehot = jnp.where(hit, 1.0, onehot)
        masked = jnp.where(hit, neg, masked)
    idx = jnp.concatenate(idxs, axis=0)
    gate = jnp.concatenate(gates, axis=0)
    gate = gate / jnp.sum(gate, axis=0, keepdims=True) * ROUTED_SCALE

    before = _dot(onehot.astype(BF16), tri_ref[...]) + carry[:, 0:1]
    ranks = [jnp.sum(jnp.where(e_iota == idxs[k], before, 0.0), axis=0, keepdims=True) for k in range(TOP_K)]
    idx_ref[...] = idx.astype(jnp.int32)
    gate_ref[...] = gate
    rank_ref[...] = jnp.concatenate(ranks, axis=0).astype(jnp.int32)
    total = carry[...] + jnp.sum(onehot, axis=1, keepdims=True)
    carry[...] = total
    cnt_ref[...] = total


def _route(lg_t, bias, tri):
    e, t = lg_t.shape
    tok = pl.BlockSpec((TOP_K, TR), lambda i: (0, i))
    return pl.pallas_call(
        _route_kernel,
        out_shape=(jax.ShapeDtypeStruct((TOP_K, t), jnp.int32),
                   jax.ShapeDtypeStruct((TOP_K, t), F32),
                   jax.ShapeDtypeStruct((TOP_K, t), jnp.int32),
                   jax.ShapeDtypeStruct((e, LANES), F32)),
        grid=(t // TR,),
        in_specs=[pl.BlockSpec((e, TR), lambda i: (0, i)),
                  pl.BlockSpec((e, 1), lambda i: (0, 0)),
                  pl.BlockSpec((TR, TR), lambda i: (0, 0))],
        out_specs=(tok, tok, tok, pl.BlockSpec((e, LANES), lambda i: (0, 0))),
        scratch_shapes=[pltpu.VMEM((e, LANES), F32)],
        compiler_params=pltpu.CompilerParams(vmem_limit_bytes=VMEM_LIMIT),
        name="route",
    )(lg_t, bias, tri)


def _tile_rows(ref, row):
    return ref.at[pl.ds(pl.multiple_of(row * ROW_TILE, ROW_TILE), ROW_TILE), :]


def _dispatch_kernel(dest_hbm, h_ref, xs_hbm, idx_smem, sem_idx, sem_rows):
    i = pl.program_id(0)
    n_idx = TD * TOP_K
    cp = pltpu.make_async_copy(dest_hbm.at[pl.ds(i * n_idx, n_idx)], idx_smem, sem_idx)
    cp.start()
    cp.wait()

    def issue(t, carry):
        src = _tile_rows(h_ref, t)
        for k in range(TOP_K):
            pltpu.make_async_copy(src, _tile_rows(xs_hbm, idx_smem[t * TOP_K + k]), sem_rows).start(priority=k % 2)
        return carry

    lax.fori_loop(0, TD, issue, 0, unroll=2)
    for _ in range(TOP_K):
        pltpu.make_async_copy(h_ref, xs_hbm.at[pl.ds(0, TD * ROW_TILE), :], sem_rows).wait()


def _dispatch(dest_flat, h2t, n_rows):
    t = h2t.shape[0] // ROW_TILE
    return pl.pallas_call(
        _dispatch_kernel,
        out_shape=jax.ShapeDtypeStruct((n_rows * ROW_TILE, LANES), F32),
        grid=(t // TD,),
        in_specs=[pl.BlockSpec(memory_space=pl.ANY),
                  pl.BlockSpec((TD * ROW_TILE, LANES), lambda i: (i, 0))],
        out_specs=pl.BlockSpec(memory_space=pl.ANY),
        scratch_shapes=[pltpu.SMEM((TD * TOP_K,), jnp.int32),
                        pltpu.SemaphoreType.DMA, pltpu.SemaphoreType.DMA],
        compiler_params=pltpu.CompilerParams(has_side_effects=True, disable_bounds_checks=True),
        name="dispatch",
    )(dest_flat, h2t)


def _experts_kernel(be_ref, nact_ref, xs_ref, wg_ref, wu_ref, wd_ref, ys_ref, wg_bf, wu_bf, wd_bf):
    i = pl.program_id(0)
    prev = be_ref[jnp.maximum(i - 1, 0)]

    @pl.when((i == 0) | (be_ref[i] != prev))
    def _():
        wg_bf[...] = wg_ref[0].astype(BF16)
        wu_bf[...] = wu_ref[0].astype(BF16)
        wd_bf[...] = wd_ref[0].astype(BF16)

    @pl.when(i < nact_ref[0])
    def _():
        xb = _load_row_tiles(xs_ref, TM_EXP).astype(BF16)
        hid = _silu(_dot(xb, wg_bf[...])) * _dot(xb, wu_bf[...])
        _store_row_tiles(ys_ref, _dot(hid.astype(BF16), wd_bf[...]))


def _experts(block_e, n_active, xs, w_gate, w_up, w_down):
    d = w_gate.shape[1]
    nb = xs.shape[0] // (TM_EXP * ROW_TILE)
    rows = pl.BlockSpec((TM_EXP * ROW_TILE, LANES), lambda i, be, na: (jnp.minimum(i, na[0] - 1), 0))
    grid_spec = pltpu.PrefetchScalarGridSpec(
        num_scalar_prefetch=2,
        grid=(nb,),
        in_specs=[rows,
                  pl.BlockSpec((1, d, D_EXPERT), lambda i, be, na: (be[i], 0, 0)),
                  pl.BlockSpec((1, d, D_EXPERT), lambda i, be, na: (be[i], 0, 0)),
                  pl.BlockSpec((1, D_EXPERT, d), lambda i, be, na: (be[i], 0, 0))],
        out_specs=rows,
        scratch_shapes=[pltpu.VMEM((d, D_EXPERT), BF16), pltpu.VMEM((d, D_EXPERT), BF16),
                        pltpu.VMEM((D_EXPERT, d), BF16)],
    )
    return pl.pallas_call(
        _experts_kernel,
        out_shape=jax.ShapeDtypeStruct(xs.shape, F32),
        grid_spec=grid_spec,
        compiler_params=pltpu.CompilerParams(vmem_limit_bytes=VMEM_LIMIT),
        name="experts",
    )(block_e, n_active, xs, w_gate, w_up, w_down)


def _combine_kernel(dest_hbm, ys_hbm, x1_ref, h2_ref, gate_ref, mod_ref, wsg_ref, wsu_ref, wsd_ref, fg_ref,
                    out_ref, idx_smem, buf, sem_idx, sem_rows):
    i = pl.program_id(0)
    n_idx = TF * TOP_K
    cp = pltpu.make_async_copy(dest_hbm.at[pl.ds(i * n_idx, n_idx)], idx_smem, sem_idx)
    cp.start()
    cp.wait()

    def issue(t, carry):
        for k in range(TOP_K):
            pltpu.make_async_copy(_tile_rows(ys_hbm, idx_smem[t * TOP_K + k]), _tile_rows(buf, k * TF + t),
                                  sem_rows).start(priority=k % 2)
        return carry

    lax.fori_loop(0, TF, issue, 0, unroll=2)

    hb = _load_row_tiles(h2_ref, TF).astype(BF16)
    hid = _silu(_dot(hb, wsg_ref[...])) * _dot(hb, wsu_ref[...])
    ffn = _dot(hid.astype(BF16), wsd_ref[...])

    pltpu.make_async_copy(ys_hbm.at[pl.ds(0, TOP_K * TF * ROW_TILE), :], buf, sem_rows).wait()
    gate = gate_ref[...]
    for k in range(TOP_K):
        ffn = ffn + gate[:, k:k + 1] * _load_row_tiles(buf, TF, base=k * TF * ROW_TILE)
    x2 = x1_ref[...] + mod_ref[0, 5:6, :] * ffn
    ms = jnp.mean(x2 * x2, axis=-1, keepdims=True)
    out_ref[...] = x2 * lax.rsqrt(ms + EPS) * fg_ref[...]


def _combine(dest_flat, ys, x1, h2t, gate_tk, mod, wsg_bf, wsu_bf, wsd_bf, final_g, seq_len):
    t, d = x1.shape
    tiles_per_seq = seq_len // TF
    tok = pl.BlockSpec((TF, d), lambda i: (i, 0))
    return pl.pallas_call(
        _combine_kernel,
        out_shape=jax.ShapeDtypeStruct((t, d), F32),
        grid=(t // TF,),
        in_specs=[pl.BlockSpec(memory_space=pl.ANY), pl.BlockSpec(memory_space=pl.ANY),
                  tok, pl.BlockSpec((TF * ROW_TILE, LANES), lambda i: (i, 0)),
                  pl.BlockSpec((TF, TOP_K), lambda i: (i, 0)),
                  pl.BlockSpec((1, 6, d), lambda i: (i // tiles_per_seq, 0, 0)),
                  pl.BlockSpec((d, D_SHARED), lambda i: (0, 0)),
                  pl.BlockSpec((d, D_SHARED), lambda i: (0, 0)),
                  pl.BlockSpec((D_SHARED, d), lambda i: (0, 0)),
                  pl.BlockSpec((1, d), lambda i: (0, 0))],
        out_specs=tok,
        scratch_shapes=[pltpu.SMEM((TF * TOP_K,), jnp.int32),
                        pltpu.VMEM((TOP_K * TF * ROW_TILE, LANES), F32),
                        pltpu.SemaphoreType.DMA, pltpu.SemaphoreType.DMA],
        compiler_params=pltpu.CompilerParams(vmem_limit_bytes=VMEM_LIMIT, disable_bounds_checks=True),
        name="combine",
    )(dest_flat, ys, x1, h2t, gate_tk, mod, wsg_bf, wsu_bf, wsd_bf, final_g)


def _rope_tables(n_tokens):
    n_rows = n_tokens // GRID_W
    n_freq = HEAD_DIM // 4
    inv_freq = ROPE_THETA ** (-jnp.arange(n_freq, dtype=F32) / n_freq)
    ang_r = jnp.arange(n_rows).astype(F32)[:, None] * inv_freq[None, :]
    ang_c = jnp.arange(GRID_W).astype(F32)[:, None] * inv_freq[None, :]

    def per_token(row_part, col_part):
        rows = jnp.broadcast_to(row_part[:, None, :], (n_rows, GRID_W, n_freq))
        cols = jnp.broadcast_to(col_part[None, :, :], (n_rows, GRID_W, n_freq))
        return rows.reshape(n_tokens, n_freq), cols.reshape(n_tokens, n_freq)

    cos_r, cos_c = per_token(jnp.cos(ang_r), jnp.cos(ang_c))
    sin_r, sin_c = per_token(jnp.sin(ang_r), jnp.sin(ang_c))
    cos = jnp.concatenate([cos_r, cos_r, cos_c, cos_c], axis=1)
    sin = jnp.concatenate([-sin_r, sin_r, -sin_c, sin_c], axis=1)
    reps = LANES // HEAD_DIM
    return jnp.tile(cos, (1, reps)), jnp.tile(sin, (1, reps))


def _pool_bands():
    i = jnp.arange(QB)[:, None]
    r = jnp.arange(POOL_SLAB)[None, :]
    return jnp.stack([((r >= i + POOL_OFF - w // 2) & (r < i + POOL_OFF + w // 2)).astype(BF16)
                      for w in POOL_WINDOWS])


def kernel(x, c, ctx, c_ctx, w_ada, b_ada, norm1_g, norm2_g, w_in, attn_sink, pool_w, pool_scale, w_out,
           w_router, router_bias, w_gate, w_up, w_down, ws_gate, ws_up, ws_down, final_g):
    b, s, d = x.shape
    t = b * s
    assert w_ada.shape[0] == 1 and d == D_MODEL and s % TQ == 0 and b + 1 <= 8

    c8 = jnp.zeros((8, d), F32).at[:b].set(c).at[b].set(c_ctx)
    mod = _ada(c8, w_ada[0], b_ada[0]).reshape(8, 6, d)
    g1 = norm1_g[0].reshape(1, d)
    g2 = norm2_g[0].reshape(1, d)
    w_in_bf = w_in[0].astype(BF16)
    cos_t, sin_t = _rope_tables(s)

    q, k4, v4, p = _inproj(x, mod, g1, w_in_bf, cos_t, sin_t)
    kc4, vc4 = _ctxproj(ctx, mod[b:b + 1], g1, w_in_bf[:, ATTN_WIDTH:ATTN_WIDTH + 2 * KV_WIDTH])

    wr_t = w_router[0].T
    wr_hi = wr_t.astype(BF16)
    wr_lo = (wr_t - wr_hi.astype(F32)).astype(BF16)
    x1, h2t, lg_t = _attn(attn_sink[0], q, k4, v4, kc4, vc4, p, x, mod, _pool_bands(),
                         pool_w[0].astype(BF16), pool_scale[0].reshape(1, POOL_WIDTH), w_out[0].astype(BF16),
                         g2, wr_hi, wr_lo)

    tri = jnp.triu(jnp.ones((TR, TR), BF16), k=1)
    idx_kt, gate_kt, rank_kt, counts = _route(lg_t, router_bias[0].reshape(N_EXPERTS, 1), tri)

    sizes = counts[:, 0].astype(jnp.int32)
    padded = (sizes + TM_EXP - 1) // TM_EXP * TM_EXP
    pad_end = jnp.cumsum(padded)
    pad_start = pad_end - padded
    n_rows = -(-(t * TOP_K + N_EXPERTS * (TM_EXP - 1)) // TM_EXP) * TM_EXP
    n_blocks = n_rows // TM_EXP
    experts = jnp.arange(N_EXPERTS, dtype=jnp.int32)
    start_of = jnp.sum(jnp.where(idx_kt[None] == experts[:, None, None], pad_start[:, None, None], 0), axis=0)
    dest_flat = (start_of + rank_kt).T.reshape(t * TOP_K)
    block_row = jnp.arange(n_blocks, dtype=jnp.int32) * TM_EXP
    block_e = jnp.minimum(jnp.sum((pad_end[None, :] <= block_row[:, None]).astype(jnp.int32), axis=1),
                          N_EXPERTS - 1)
    n_active = (pad_end[-1:] // TM_EXP).astype(jnp.int32)

    xs = _dispatch(dest_flat, h2t, n_rows)
    ys = _experts(block_e, n_active, xs, w_gate[0], w_up[0], w_down[0])
    out = _combine(dest_flat, ys, x1.reshape(t, d), h2t, gate_kt.T, mod,
                   ws_gate[0].astype(BF16), ws_up[0].astype(BF16), ws_down[0].astype(BF16),
                   final_g.reshape(1, d), s)
    return out.reshape(b, s, d)
```

```python
import functools

import jax
import jax.numpy as jnp
from jax import lax
from jax.experimental import pallas as pl
from jax.experimental.pallas import tpu as pltpu

F32 = jnp.float32
BF16 = jnp.bfloat16

D_MODEL = 1024
GRID_W = 64
N_HEADS = 8
N_KV_HEADS = 2
HEAD_DIM = 64
ATTN_WIDTH = N_HEADS * HEAD_DIM
KV_WIDTH = N_KV_HEADS * HEAD_DIM
WINDOW = 128
ROPE_THETA = 10000.0
POOL_WINDOWS = (2, 4, 8, 16)
POOL_WIDTH = D_MODEL - ATTN_WIDTH
POOL_GROUP_DIM = POOL_WIDTH // len(POOL_WINDOWS)
IN_COLS = ATTN_WIDTH + 2 * KV_WIDTH + POOL_WIDTH
N_EXPERTS = 64
TOP_K = 8
N_EXPERT_GROUPS = 8
EXPERTS_PER_GROUP = N_EXPERTS // N_EXPERT_GROUPS
TOPK_GROUPS = 4
D_EXPERT = 256
D_SHARED = 256
ROUTED_SCALE = 2.5
EPS = 1e-6
LOG2E = 1.4426950408889634

LANES = 128
ROW_TILE = D_MODEL // (2 * LANES)
U32 = jnp.uint32
VMEM_LIMIT = 48 * 1024 * 1024

TM_PROJ = 512
TQ = 512
QB = 128
POOL_SLAB = 256
POOL_OFF = 64
TR = 512
TM_EXP = 256
TD = 256
TF = 256


def _silu(x):
    return x * (1.0 / (1.0 + jnp.exp(-x)))


def _split_bf16(x):
    hi = x.astype(BF16)
    lo = (x - hi.astype(F32)).astype(BF16)
    return hi, lo


def _dot(a, b):
    return jnp.dot(a, b, preferred_element_type=F32)


def _store_row_tiles(ref, val):
    rows, half = val.shape[0], val.shape[1] // 2
    lo = lax.bitcast_convert_type(val[:, :half].astype(BF16).astype(F32), U32)
    hi = lax.bitcast_convert_type(val[:, half:].astype(BF16).astype(F32), U32)
    words = lax.shift_right_logical(lo, jnp.uint32(16)) | hi
    for c in range(ROW_TILE):
        ref[pl.ds(c, rows, stride=ROW_TILE), :] = words[:, c * LANES:(c + 1) * LANES]


def _load_row_tiles(ref, rows, base=0):
    words = jnp.concatenate([ref[pl.ds(base + c, rows, stride=ROW_TILE), :] for c in range(ROW_TILE)], axis=1)
    lo = lax.bitcast_convert_type(lax.shift_left(words, jnp.uint32(16)), F32)
    hi = lax.bitcast_convert_type(words & jnp.uint32(0xFFFF0000), F32)
    return jnp.concatenate([lo, hi], axis=1)


def _dot_nt(a, b):
    return lax.dot_general(a, b, (((1,), (1,)), ((), ())), preferred_element_type=F32)


def _ada_kernel(c_ref, w_ref, b_ref, o_ref):
    a_hi, a_lo = _split_bf16(_silu(c_ref[...]))
    w_hi, w_lo = _split_bf16(w_ref[...])
    o_ref[...] = _dot(a_hi, w_hi) + _dot(a_lo, w_hi) + _dot(a_hi, w_lo) + b_ref[...]


def _ada(c8, w_ada, b_ada):
    d = c8.shape[1]
    n = w_ada.shape[1]
    tn = 512
    return pl.pallas_call(
        _ada_kernel,
        out_shape=jax.ShapeDtypeStruct((8, n), F32),
        grid=(n // tn,),
        in_specs=[pl.BlockSpec((8, d), lambda j: (0, 0)),
                  pl.BlockSpec((d, tn), lambda j: (0, j)),
                  pl.BlockSpec((1, tn), lambda j: (0, j))],
        out_specs=pl.BlockSpec((8, tn), lambda j: (0, j)),
        compiler_params=pltpu.CompilerParams(vmem_limit_bytes=VMEM_LIMIT),
        name="ada",
    )(c8, w_ada, b_ada.reshape(1, n))


def _norm_mod(x, g, shift, scale):
    ms = jnp.mean(x * x, axis=-1, keepdims=True)
    return (x * lax.rsqrt(ms + EPS) * g) * (1.0 + scale) + shift


def _lane_variants(t):
    lane = lax.broadcasted_iota(jnp.int32, t.shape, 1)
    lo = lane < HEAD_DIM
    tr = pltpu.roll(t, HEAD_DIM, 1)
    zero = jnp.zeros_like(t)
    return (jnp.where(lo, t, zero), jnp.where(lo, zero, tr),
            jnp.where(lo, tr, zero), jnp.where(lo, zero, t))


def _store_variants(ref, t):
    for i, var in enumerate(_lane_variants(t)):
        ref[0, :, i * LANES:(i + 1) * LANES] = var.astype(BF16)


def _inproj_kernel(x_ref, mod_ref, g_ref, w_ref, cos_ref, sin_ref, q_ref, k_ref, v_ref, p_ref):
    h = _norm_mod(x_ref[0], g_ref[...], mod_ref[0, 0:1, :], mod_ref[0, 1:2, :])
    z = _dot(h.astype(BF16), w_ref[...])
    cos = cos_ref[...]
    sin = sin_ref[...]
    lane = lax.broadcasted_iota(jnp.int32, cos.shape, 1)
    first_half = (lane & 16) == 0

    def rope(zc):
        partner = jnp.where(first_half, pltpu.roll(zc, LANES - 16, 1), pltpu.roll(zc, 16, 1))
        return zc * cos + partner * sin

    scale = HEAD_DIM ** -0.5 * LOG2E
    for c in range(ATTN_WIDTH // LANES):
        q_ref[0, :, c * LANES:(c + 1) * LANES] = (rope(z[:, c * LANES:(c + 1) * LANES]) * scale).astype(BF16)
    _store_variants(k_ref, rope(z[:, ATTN_WIDTH:ATTN_WIDTH + KV_WIDTH]))
    _store_variants(v_ref, z[:, ATTN_WIDTH + KV_WIDTH:ATTN_WIDTH + 2 * KV_WIDTH])
    p_ref[0] = z[:, ATTN_WIDTH + 2 * KV_WIDTH:]


def _inproj(x, mod, g1, w_in_bf, cos_t, sin_t):
    b, s, d = x.shape
    tm = TM_PROJ
    return pl.pallas_call(
        _inproj_kernel,
        out_shape=(jax.ShapeDtypeStruct((b, s, ATTN_WIDTH), BF16),
                   jax.ShapeDtypeStruct((b, s, 4 * LANES), BF16),
                   jax.ShapeDtypeStruct((b, s, 4 * LANES), BF16),
                   jax.ShapeDtypeStruct((b, s, POOL_WIDTH), F32)),
        grid=(s // tm, b),
        in_specs=[pl.BlockSpec((1, tm, d), lambda n, bi: (bi, n, 0)),
                  pl.BlockSpec((1, 6, d), lambda n, bi: (bi, 0, 0)),
                  pl.BlockSpec((1, d), lambda n, bi: (0, 0)),
                  pl.BlockSpec((d, IN_COLS), lambda n, bi: (0, 0)),
                  pl.BlockSpec((tm, LANES), lambda n, bi: (n, 0)),
                  pl.BlockSpec((tm, LANES), lambda n, bi: (n, 0))],
        out_specs=(pl.BlockSpec((1, tm, ATTN_WIDTH), lambda n, bi: (bi, n, 0)),
                   pl.BlockSpec((1, tm, 4 * LANES), lambda n, bi: (bi, n, 0)),
                   pl.BlockSpec((1, tm, 4 * LANES), lambda n, bi: (bi, n, 0)),
                   pl.BlockSpec((1, tm, POOL_WIDTH), lambda n, bi: (bi, n, 0))),
        compiler_params=pltpu.CompilerParams(vmem_limit_bytes=VMEM_LIMIT),
        name="inproj",
    )(x, mod, g1, w_in_bf, cos_t, sin_t)


def _ctxproj_kernel(x_ref, mod_ref, g_ref, w_ref, k_ref, v_ref):
    h = _norm_mod(x_ref[0], g_ref[...], mod_ref[0, 0:1, :], mod_ref[0, 1:2, :])
    z = _dot(h.astype(BF16), w_ref[...])
    _store_variants(k_ref, z[:, :KV_WIDTH])
    _store_variants(v_ref, z[:, KV_WIDTH:])


def _ctxproj(ctx, mod_c, g1, w_kv_bf):
    b, c, d = ctx.shape
    return pl.pallas_call(
        _ctxproj_kernel,
        out_shape=(jax.ShapeDtypeStruct((b, c, 4 * LANES), BF16),
                   jax.ShapeDtypeStruct((b, c, 4 * LANES), BF16)),
        grid=(b,),
        in_specs=[pl.BlockSpec((1, c, d), lambda bi: (bi, 0, 0)),
                  pl.BlockSpec((1, 6, d), lambda bi: (0, 0, 0)),
                  pl.BlockSpec((1, d), lambda bi: (0, 0)),
                  pl.BlockSpec((d, 2 * KV_WIDTH), lambda bi: (0, 0))],
        out_specs=(pl.BlockSpec((1, c, 4 * LANES), lambda bi: (bi, 0, 0)),
                   pl.BlockSpec((1, c, 4 * LANES), lambda bi: (bi, 0, 0))),
        compiler_params=pltpu.CompilerParams(vmem_limit_bytes=VMEM_LIMIT),
        name="ctxproj",
    )(ctx, mod_c, g1, w_kv_bf)


def _fold_lane_tiles(op, *arrays):
    tiles = [a[:, i * LANES:(i + 1) * LANES] for a in arrays for i in range(a.shape[1] // LANES)]
    while len(tiles) > 1:
        tiles = [op(tiles[i], tiles[i + 1]) if i + 1 < len(tiles) else tiles[i] for i in range(0, len(tiles), 2)]
    return tiles[0]


def _attn_kernel(seq_len, sink_ref, q_ref, k_ref, kp_ref, kn_ref, v_ref, vp_ref, vn_ref, kc_ref, vc_ref,
                 p_ref, pp_ref, pn_ref, x_ref, mod_ref, band_ref, poolw_ref, pscale_ref, wout_ref,
                 g2_ref, wrh_ref, wrl_ref, x1_ref, h2_ref, lg_ref, kwin, vwin, pext, mix, s_scr, p_scr, m_scr):
    n = pl.program_id(1)
    n_last = pl.num_programs(1) - 1

    kwin[0:QB, :] = kp_ref[0]
    kwin[QB:QB + TQ, :] = k_ref[0]
    kwin[QB + TQ:, :] = kn_ref[0]
    vwin[0:QB, :] = vp_ref[0]
    vwin[QB:QB + TQ, :] = v_ref[0]
    vwin[QB + TQ:, :] = vn_ref[0]

    pext[0:QB - 8, :] = jnp.zeros((QB - 8, POOL_WIDTH), F32)
    pext[QB - 8:QB, :] = jnp.where(n > 0, pp_ref[0], 0.0)
    pext[QB:QB + TQ, :] = p_ref[0]
    pext[QB + TQ:QB + TQ + 8, :] = jnp.where(n < n_last, pn_ref[0], 0.0)
    pext[QB + TQ + 8:, :] = jnp.zeros((QB - 8, POOL_WIDTH), F32)

    row = lax.broadcasted_iota(jnp.int32, (QB, 3 * QB), 0)
    col = lax.broadcasted_iota(jnp.int32, (QB, 3 * QB), 1)
    in_band = (col >= row) & (col <= row + 2 * WINDOW)
    tok = lax.broadcasted_iota(jnp.int32, (QB, 1), 0)
    kc = kc_ref[0]
    vc = vc_ref[0]

    def sub_block(j, carry):
        r0 = pl.multiple_of(j * QB, QB)
        qj = q_ref[0, pl.ds(r0, QB), :]
        kw = kwin[pl.ds(r0, 3 * QB), :]
        vw = vwin[pl.ds(r0, 3 * QB), :]
        kpos = col + (n * TQ + j * QB - QB)
        ok = in_band & (kpos >= 0) & (kpos < seq_len)
        bias = jnp.where(ok, 0.0, -jnp.inf)
        n_loc = 3 * QB

        def kv_lanes(head):
            var = 2 * (head // (N_HEADS // N_KV_HEADS)) + head % 2
            return slice(var * LANES, (var + 1) * LANES)

        for head in range(N_HEADS):
            qc = qj[:, (head // 2) * LANES:(head // 2 + 1) * LANES]
            s_scr[head, :, 0:n_loc] = _dot_nt(qc, kw[:, kv_lanes(head)]) + bias
            s_scr[head, :, n_loc:] = _dot_nt(qc, kc[:, kv_lanes(head)])
        n_tiles = (n_loc + kc.shape[0]) // LANES
        for head in range(N_HEADS):
            row_max = jnp.max(_fold_lane_tiles(jnp.maximum, s_scr[head]), axis=1, keepdims=True)
            m_scr[head] = jnp.broadcast_to(jnp.maximum(row_max, sink_ref[head] * LOG2E), (QB, LANES))
        for head in range(N_HEADS):
            m = m_scr[head]
            acc = None
            for i in range(n_tiles):
                p = jnp.exp2(s_scr[head, :, i * LANES:(i + 1) * LANES] - m)
                p_scr[head, :, i * LANES:(i + 1) * LANES] = p.astype(BF16)
                acc = p if acc is None else acc + p
            denom = (jnp.broadcast_to(jnp.sum(acc, axis=1, keepdims=True), (QB, LANES))
                     + jnp.exp2(sink_ref[head] * LOG2E - m))
            m_scr[head] = 1.0 / denom
        for c in range(N_HEADS // 2):
            pair = None
            for head in (2 * c, 2 * c + 1):
                o = (_dot(p_scr[head, :, 0:n_loc], vw[:, kv_lanes(head)])
                     + _dot(p_scr[head, :, n_loc:], vc[:, kv_lanes(head)])) * m_scr[head]
                pair = o if pair is None else pair + o
            mix[pl.ds(r0, QB), c * LANES:(c + 1) * LANES] = pair.astype(BF16)

        slab = pext[pl.ds(pl.multiple_of(r0 + POOL_OFF, 8), POOL_SLAB), :]
        tpos = tok + (n * TQ + j * QB)
        for g, w in enumerate(POOL_WINDOWS):
            sg = slab[:, g * LANES:(g + 1) * LANES]
            hi, lo = _split_bf16(sg)
            band = band_ref[g]
            wsum = _dot(band, hi) + _dot(band, lo)
            cnt = (jnp.minimum(tpos - w // 2 + w, seq_len) - jnp.maximum(tpos - w // 2, 0)).astype(F32)
            dlt = wsum / cnt - sg[POOL_OFF:POOL_OFF + QB, :]
            y = _dot(dlt.astype(BF16), poolw_ref[g]) * pscale_ref[:, g * LANES:(g + 1) * LANES]
            mix[pl.ds(r0, QB), ATTN_WIDTH + g * LANES:ATTN_WIDTH + (g + 1) * LANES] = y.astype(BF16)
        return carry

    lax.fori_loop(0, TQ // QB, sub_block, 0)

    proj = _dot(mix[...], wout_ref[...])
    x1 = x_ref[0] + mod_ref[0, 2:3, :] * proj
    x1_ref[0] = x1
    h2 = _norm_mod(x1, g2_ref[...], mod_ref[0, 3:4, :], mod_ref[0, 4:5, :])
    _store_row_tiles(h2_ref, h2)
    h_hi, h_lo = _split_bf16(h2)
    wrh = wrh_ref[...]
    lg_ref[...] = _dot_nt(wrh, h_hi) + _dot_nt(wrh, h_lo) + _dot_nt(wrl_ref[...], h_hi)


def _attn(sink, q, k4, v4, kc4, vc4, p, x, mod, band, poolw_bf, pscale, wout_bf, g2, wr_hi, wr_lo):
    b, s, d = x.shape
    c = kc4.shape[1]
    nt = s // TQ
    hb = TQ // QB
    pb = TQ // 8
    kv_main = pl.BlockSpec((1, TQ, 4 * LANES), lambda bi, n: (bi, n, 0))
    kv_prev = pl.BlockSpec((1, QB, 4 * LANES), lambda bi, n: (bi, jnp.maximum(n * hb - 1, 0), 0))
    kv_next = pl.BlockSpec((1, QB, 4 * LANES), lambda bi, n: (bi, jnp.minimum((n + 1) * hb, s // QB - 1), 0))
    const2 = lambda bi, n: (0, 0)
    const3 = lambda bi, n: (0, 0, 0)
    return pl.pallas_call(
        functools.partial(_attn_kernel, s),
        out_shape=(jax.ShapeDtypeStruct((b, s, d), F32),
                   jax.ShapeDtypeStruct((b * s * ROW_TILE, LANES), U32),
                   jax.ShapeDtypeStruct((N_EXPERTS, b * s), F32)),
        grid=(b, nt),
        in_specs=[pl.BlockSpec(memory_space=pltpu.SMEM),
                  pl.BlockSpec((1, TQ, ATTN_WIDTH), lambda bi, n: (bi, n, 0)),
                  kv_main, kv_prev, kv_next, kv_main, kv_prev, kv_next,
                  pl.BlockSpec((1, c, 4 * LANES), lambda bi, n: (bi, 0, 0)),
                  pl.BlockSpec((1, c, 4 * LANES), lambda bi, n: (bi, 0, 0)),
                  pl.BlockSpec((1, TQ, POOL_WIDTH), lambda bi, n: (bi, n, 0)),
                  pl.BlockSpec((1, 8, POOL_WIDTH), lambda bi, n: (bi, jnp.maximum(n * pb - 1, 0), 0)),
                  pl.BlockSpec((1, 8, POOL_WIDTH), lambda bi, n: (bi, jnp.minimum((n + 1) * pb, s // 8 - 1), 0)),
                  pl.BlockSpec((1, TQ, d), lambda bi, n: (bi, n, 0)),
                  pl.BlockSpec((1, 6, d), lambda bi, n: (bi, 0, 0)),
                  pl.BlockSpec((len(POOL_WINDOWS), QB, POOL_SLAB), const3),
                  pl.BlockSpec((len(POOL_WINDOWS), POOL_GROUP_DIM, POOL_GROUP_DIM), const3),
                  pl.BlockSpec((1, POOL_WIDTH), const2),
                  pl.BlockSpec((d, d), const2),
                  pl.BlockSpec((1, d), const2),
                  pl.BlockSpec((N_EXPERTS, d), const2),
                  pl.BlockSpec((N_EXPERTS, d), const2)],
        out_specs=(pl.BlockSpec((1, TQ, d), lambda bi, n: (bi, n, 0)),
                   pl.BlockSpec((TQ * ROW_TILE, LANES), lambda bi, n: (bi * nt + n, 0)),
                   pl.BlockSpec((N_EXPERTS, TQ), lambda bi, n: (0, bi * nt + n))),
        scratch_shapes=[pltpu.VMEM((TQ + 2 * QB, 4 * LANES), BF16),
                        pltpu.VMEM((TQ + 2 * QB, 4 * LANES), BF16),
                        pltpu.VMEM((TQ + 2 * QB, POOL_WIDTH), F32),
                        pltpu.VMEM((TQ, d), BF16),
                        pltpu.VMEM((N_HEADS, QB, 3 * QB + c), F32),
                        pltpu.VMEM((N_HEADS, QB, 3 * QB + c), BF16),
                        pltpu.VMEM((N_HEADS, QB, LANES), F32)],
        compiler_params=pltpu.CompilerParams(vmem_limit_bytes=VMEM_LIMIT),
        name="attn",
    )(sink, q, k4, k4, k4, v4, v4, v4, kc4, vc4, p, p, p, x, mod, band, poolw_bf, pscale, wout_bf,
      g2, wr_hi, wr_lo)


def _first_argmax_rows(v, row_iota, n_rows):
    m = jnp.max(v, axis=0, keepdims=True)
    idx = jnp.min(jnp.where(v == m, row_iota, n_rows), axis=0, keepdims=True)
    return m, idx


def _route_kernel(lg_ref, bias_ref, tri_ref, idx_ref, gate_ref, rank_ref, cnt_ref, carry):
    i = pl.program_id(0)

    @pl.when(i == 0)
    def _():
        carry[...] = jnp.zeros_like(carry)

    scores = 1.0 / (1.0 + jnp.exp(-lg_ref[...]))
    biased = scores + bias_ref[...]
    e_iota = lax.broadcasted_iota(jnp.int32, scores.shape, 0).astype(F32)
    g_iota = lax.broadcasted_iota(jnp.int32, (EXPERTS_PER_GROUP, TR), 0).astype(F32)
    neg = -jnp.inf

    grp = []
    for g in range(N_EXPERT_GROUPS):
        blk = biased[g * EXPERTS_PER_GROUP:(g + 1) * EXPERTS_PER_GROUP, :]
        m1, i1 = _first_argmax_rows(blk, g_iota, float(EXPERTS_PER_GROUP))
        m2 = jnp.max(jnp.where(g_iota == i1, neg, blk), axis=0, keepdims=True)
        grp.append(m1 + m2)
    grp = jnp.concatenate(grp, axis=0)
    gg_iota = lax.broadcasted_iota(jnp.int32, grp.shape, 0).astype(F32)
    grp_sel = jnp.zeros(grp.shape, F32)
    for _ in range(TOPK_GROUPS):
        _, gi = _first_argmax_rows(grp, gg_iota, float(N_EXPERT_GROUPS))
        hit = gg_iota == gi
        grp_sel = jnp.where(hit, 1.0, grp_sel)
        grp = jnp.where(hit, neg, grp)
    allowed = jnp.concatenate(
        [jnp.broadcast_to(grp_sel[g:g + 1, :], (EXPERTS_PER_GROUP, TR)) for g in range(N_EXPERT_GROUPS)], axis=0)
    masked = jnp.where(allowed > 0.5, biased, neg)

    idxs, gates = [], []
    onehot = jnp.zeros(scores.shape, F32)
    for _ in range(TOP_K):
        _, ei = _first_argmax_rows(masked, e_iota, float(N_EXPERTS))
        hit = e_iota == ei
        idxs.append(ei)
        gates.append(jnp.sum(jnp.where(hit, scores, 0.0), axis=0, keepdims=True))
        onehot = jnp.where(hit, 1.0, onehot)
        masked = jnp.where(hit, neg, masked)
    idx = jnp.concatenate(idxs, axis=0)
    gate = jnp.concatenate(gates, axis=0)
    gate = gate / jnp.sum(gate, axis=0, keepdims=True) * ROUTED_SCALE

    before = _dot(onehot.astype(BF16), tri_ref[...]) + carry[:, 0:1]
    ranks = [jnp.sum(jnp.where(e_iota == idxs[k], before, 0.0), axis=0, keepdims=True) for k in range(TOP_K)]
    idx_ref[...] = idx.astype(jnp.int32)
    gate_ref[...] = gate
    rank_ref[...] = jnp.concatenate(ranks, axis=0).astype(jnp.int32)
    total = carry[...] + jnp.sum(onehot, axis=1, keepdims=True)
    carry[...] = total
    cnt_ref[...] = total


def _route(lg_t, bias, tri):
    e, t = lg_t.shape
    tok = pl.BlockSpec((TOP_K, TR), lambda i: (0, i))
    return pl.pallas_call(
        _route_kernel,
        out_shape=(jax.ShapeDtypeStruct((TOP_K, t), jnp.int32),
                   jax.ShapeDtypeStruct((TOP_K, t), F32),
                   jax.ShapeDtypeStruct((TOP_K, t), jnp.int32),
                   jax.ShapeDtypeStruct((e, LANES), F32)),
        grid=(t // TR,),
        in_specs=[pl.BlockSpec((e, TR), lambda i: (0, i)),
                  pl.BlockSpec((e, 1), lambda i: (0, 0)),
                  pl.BlockSpec((TR, TR), lambda i: (0, 0))],
        out_specs=(tok, tok, tok, pl.BlockSpec((e, LANES), lambda i: (0, 0))),
        scratch_shapes=[pltpu.VMEM((e, LANES), F32)],
        compiler_params=pltpu.CompilerParams(vmem_limit_bytes=VMEM_LIMIT),
        name="route",
    )(lg_t, bias, tri)


def _tile_rows(ref, row):
    return ref.at[pl.ds(pl.multiple_of(row * ROW_TILE, ROW_TILE), ROW_TILE), :]


def _dispatch_kernel(fill_lo, fill_hi, dest_hbm, h_ref, xs_hbm, idx_smem, zero_row, sem_idx, sem_rows):
    i = pl.program_id(0)

    @pl.when(i == 0)
    def _():
        zero_row[...] = jnp.zeros_like(zero_row)

        def fill_range(e, carry):
            def fill_copy(r):
                return pltpu.make_async_copy(zero_row, _tile_rows(xs_hbm, r), sem_rows)

            def start(r, c):
                fill_copy(r).start()
                return c

            def wait(r, c):
                fill_copy(r).wait()
                return c

            lax.fori_loop(fill_lo[e], fill_hi[e], start, 0)
            lax.fori_loop(fill_lo[e], fill_hi[e], wait, 0)
            return carry

        lax.fori_loop(0, N_EXPERTS + 1, fill_range, 0)

    n_idx = TD * TOP_K
    cp = pltpu.make_async_copy(dest_hbm.at[pl.ds(i * n_idx, n_idx)], idx_smem, sem_idx)
    cp.start()
    cp.wait()

    def issue(t, carry):
        src = _tile_rows(h_ref, t)
        for k in range(TOP_K):
            pltpu.make_async_copy(src, _tile_rows(xs_hbm, idx_smem[t * TOP_K + k]), sem_rows).start(priority=k % 2)
        return carry

    lax.fori_loop(0, TD, issue, 0, unroll=2)
    for _ in range(TOP_K):
        pltpu.make_async_copy(h_ref, xs_hbm.at[pl.ds(0, TD * ROW_TILE), :], sem_rows).wait()


def _dispatch(fill_lo, fill_hi, dest_flat, h2t, n_rows):
    t = h2t.shape[0] // ROW_TILE
    grid_spec = pltpu.PrefetchScalarGridSpec(
        num_scalar_prefetch=2,
        grid=(t // TD,),
        in_specs=[pl.BlockSpec(memory_space=pl.ANY),
                  pl.BlockSpec((TD * ROW_TILE, LANES), lambda i, lo, hi: (i, 0))],
        out_specs=pl.BlockSpec(memory_space=pl.ANY),
        scratch_shapes=[pltpu.SMEM((TD * TOP_K,), jnp.int32),
                        pltpu.VMEM((ROW_TILE, LANES), U32),
                        pltpu.SemaphoreType.DMA, pltpu.SemaphoreType.DMA],
    )
    return pl.pallas_call(
        _dispatch_kernel,
        out_shape=jax.ShapeDtypeStruct((n_rows * ROW_TILE, LANES), U32),
        grid_spec=grid_spec,
        compiler_params=pltpu.CompilerParams(has_side_effects=True, disable_bounds_checks=True),
        name="dispatch",
    )(fill_lo, fill_hi, dest_flat, h2t)


def _experts_kernel(be_ref, nact_ref, xs_ref, wg_ref, wu_ref, wd_ref, ys_ref, wg_bf, wu_bf, wd_bf):
    i = pl.program_id(0)
    prev = be_ref[jnp.maximum(i - 1, 0)]

    @pl.when((i == 0) | (be_ref[i] != prev))
    def _():
        wg_bf[...] = wg_ref[0].astype(BF16)
        wu_bf[...] = wu_ref[0].astype(BF16)
        wd_bf[...] = wd_ref[0].astype(BF16)

    @pl.when(i < nact_ref[0])
    def _():
        xb = _load_row_tiles(xs_ref, TM_EXP).astype(BF16)
        hid = _silu(_dot(xb, wg_bf[...])) * _dot(xb, wu_bf[...])
        _store_row_tiles(ys_ref, _dot(hid.astype(BF16), wd_bf[...]))

    @pl.when(i >= nact_ref[0])
    def _():
        ys_ref[...] = jnp.zeros_like(ys_ref)


def _experts(block_e, n_active, xs, w_gate, w_up, w_down):
    d = w_gate.shape[1]
    nb = xs.shape[0] // (TM_EXP * ROW_TILE)
    grid_spec = pltpu.PrefetchScalarGridSpec(
        num_scalar_prefetch=2,
        grid=(nb,),
        in_specs=[pl.BlockSpec((TM_EXP * ROW_TILE, LANES), lambda i, be, na: (jnp.minimum(i, na[0] - 1), 0)),
                  pl.BlockSpec((1, d, D_EXPERT), lambda i, be, na: (be[i], 0, 0)),
                  pl.BlockSpec((1, d, D_EXPERT), lambda i, be, na: (be[i], 0, 0)),
                  pl.BlockSpec((1, D_EXPERT, d), lambda i, be, na: (be[i], 0, 0))],
        out_specs=pl.BlockSpec((TM_EXP * ROW_TILE, LANES), lambda i, be, na: (i, 0)),
        scratch_shapes=[pltpu.VMEM((d, D_EXPERT), BF16), pltpu.VMEM((d, D_EXPERT), BF16),
                        pltpu.VMEM((D_EXPERT, d), BF16)],
    )
    return pl.pallas_call(
        _experts_kernel,
        out_shape=jax.ShapeDtypeStruct(xs.shape, U32),
        grid_spec=grid_spec,
        compiler_params=pltpu.CompilerParams(vmem_limit_bytes=VMEM_LIMIT),
        name="experts",
    )(block_e, n_active, xs, w_gate, w_up, w_down)


def _combine_kernel(dest_hbm, ys_hbm, x1_ref, h2_ref, gate_ref, mod_ref, wsg_ref, wsu_ref, wsd_ref, fg_ref,
                    out_ref, idx_smem, buf, sem_idx, sem_rows):
    i = pl.program_id(0)
    n_idx = TF * TOP_K
    cp = pltpu.make_async_copy(dest_hbm.at[pl.ds(i * n_idx, n_idx)], idx_smem, sem_idx)
    cp.start()
    cp.wait()

    def issue(t, carry):
        for k in range(TOP_K):
            pltpu.make_async_copy(_tile_rows(ys_hbm, idx_smem[t * TOP_K + k]), _tile_rows(buf, k * TF + t),
                                  sem_rows).start(priority=k % 2)
        return carry

    lax.fori_loop(0, TF, issue, 0, unroll=2)

    hb = _load_row_tiles(h2_ref, TF).astype(BF16)
    hid = _silu(_dot(hb, wsg_ref[...])) * _dot(hb, wsu_ref[...])
    ffn = _dot(hid.astype(BF16), wsd_ref[...])

    pltpu.make_async_copy(ys_hbm.at[pl.ds(0, TOP_K * TF * ROW_TILE), :], buf, sem_rows).wait()
    gate = gate_ref[...]
    for k in range(TOP_K):
        ffn = ffn + gate[:, k:k + 1] * _load_row_tiles(buf, TF, base=k * TF * ROW_TILE)
    x2 = x1_ref[...] + mod_ref[0, 5:6, :] * ffn
    ms = jnp.mean(x2 * x2, axis=-1, keepdims=True)
    out_ref[...] = x2 * lax.rsqrt(ms + EPS) * fg_ref[...]


def _combine(dest_flat, ys, x1, h2t, gate_tk, mod, wsg_bf, wsu_bf, wsd_bf, final_g, seq_len):
    t, d = x1.shape
    tiles_per_seq = seq_len // TF
    tok = pl.BlockSpec((TF, d), lambda i: (i, 0))
    return pl.pallas_call(
        _combine_kernel,
        out_shape=jax.ShapeDtypeStruct((t, d), F32),
        grid=(t // TF,),
        in_specs=[pl.BlockSpec(memory_space=pl.ANY), pl.BlockSpec(memory_space=pl.ANY),
                  tok, pl.BlockSpec((TF * ROW_TILE, LANES), lambda i: (i, 0)),
                  pl.BlockSpec((TF, TOP_K), lambda i: (i, 0)),
                  pl.BlockSpec((1, 6, d), lambda i: (i // tiles_per_seq, 0, 0)),
                  pl.BlockSpec((d, D_SHARED), lambda i: (0, 0)),
                  pl.BlockSpec((d, D_SHARED), lambda i: (0, 0)),
                  pl.BlockSpec((D_SHARED, d), lambda i: (0, 0)),
                  pl.BlockSpec((1, d), lambda i: (0, 0))],
        out_specs=tok,
        scratch_shapes=[pltpu.SMEM((TF * TOP_K,), jnp.int32),
                        pltpu.VMEM((TOP_K * TF * ROW_TILE, LANES), U32),
                        pltpu.SemaphoreType.DMA, pltpu.SemaphoreType.DMA],
        compiler_params=pltpu.CompilerParams(vmem_limit_bytes=VMEM_LIMIT, disable_bounds_checks=True),
        name="combine",
    )(dest_flat, ys, x1, h2t, gate_tk, mod, wsg_bf, wsu_bf, wsd_bf, final_g)


def _rope_tables(n_tokens):
    n_rows = n_tokens // GRID_W
    n_freq = HEAD_DIM // 4
    inv_freq = ROPE_THETA ** (-jnp.arange(n_freq, dtype=F32) / n_freq)
    ang_r = jnp.arange(n_rows).astype(F32)[:, None] * inv_freq[None, :]
    ang_c = jnp.arange(GRID_W).astype(F32)[:, None] * inv_freq[None, :]

    def per_token(row_part, col_part):
        rows = jnp.broadcast_to(row_part[:, None, :], (n_rows, GRID_W, n_freq))
        cols = jnp.broadcast_to(col_part[None, :, :], (n_rows, GRID_W, n_freq))
        return rows.reshape(n_tokens, n_freq), cols.reshape(n_tokens, n_freq)

    cos_r, cos_c = per_token(jnp.cos(ang_r), jnp.cos(ang_c))
    sin_r, sin_c = per_token(jnp.sin(ang_r), jnp.sin(ang_c))
    cos = jnp.concatenate([cos_r, cos_r, cos_c, cos_c], axis=1)
    sin = jnp.concatenate([-sin_r, sin_r, -sin_c, sin_c], axis=1)
    reps = LANES // HEAD_DIM
    return jnp.tile(cos, (1, reps)), jnp.tile(sin, (1, reps))


def _pool_bands():
    i = jnp.arange(QB)[:, None]
    r = jnp.arange(POOL_SLAB)[None, :]
    return jnp.stack([((r >= i + POOL_OFF - w // 2) & (r < i + POOL_OFF + w // 2)).astype(BF16)
                      for w in POOL_WINDOWS])


def kernel(x, c, ctx, c_ctx, w_ada, b_ada, norm1_g, norm2_g, w_in, attn_sink, pool_w, pool_scale, w_out,
           w_router, router_bias, w_gate, w_up, w_down, ws_gate, ws_up, ws_down, final_g):
    b, s, d = x.shape
    t = b * s
    assert w_ada.shape[0] == 1 and d == D_MODEL and s % TQ == 0 and b + 1 <= 8

    c8 = jnp.zeros((8, d), F32).at[:b].set(c).at[b].set(c_ctx)
    mod = _ada(c8, w_ada[0], b_ada[0]).reshape(8, 6, d)
    g1 = norm1_g[0].reshape(1, d)
    g2 = norm2_g[0].reshape(1, d)
    w_in_bf = w_in[0].astype(BF16)
    cos_t, sin_t = _rope_tables(s)

    q, k4, v4, p = _inproj(x, mod, g1, w_in_bf, cos_t, sin_t)
    kc4, vc4 = _ctxproj(ctx, mod[b:b + 1], g1, w_in_bf[:, ATTN_WIDTH:ATTN_WIDTH + 2 * KV_WIDTH])

    wr_t = w_router[0].T
    wr_hi = wr_t.astype(BF16)
    wr_lo = (wr_t - wr_hi.astype(F32)).astype(BF16)
    x1, h2t, lg_t = _attn(attn_sink[0], q, k4, v4, kc4, vc4, p, x, mod, _pool_bands(),
                         pool_w[0].astype(BF16), pool_scale[0].reshape(1, POOL_WIDTH), w_out[0].astype(BF16),
                         g2, wr_hi, wr_lo)

    tri = jnp.triu(jnp.ones((TR, TR), BF16), k=1)
    idx_kt, gate_kt, rank_kt, counts = _route(lg_t, router_bias[0].reshape(N_EXPERTS, 1), tri)

    sizes = counts[:, 0].astype(jnp.int32)
    padded = (sizes + TM_EXP - 1) // TM_EXP * TM_EXP
    pad_end = jnp.cumsum(padded)
    pad_start = pad_end - padded
    n_rows = -(-(t * TOP_K + N_EXPERTS * (TM_EXP - 1)) // TM_EXP) * TM_EXP
    n_blocks = n_rows // TM_EXP
    experts = jnp.arange(N_EXPERTS, dtype=jnp.int32)
    start_of = jnp.sum(jnp.where(idx_kt[None] == experts[:, None, None], pad_start[:, None, None], 0), axis=0)
    dest_flat = (start_of + rank_kt).T.reshape(t * TOP_K)
    block_row = jnp.arange(n_blocks, dtype=jnp.int32) * TM_EXP
    block_e = jnp.minimum(jnp.sum((pad_end[None, :] <= block_row[:, None]).astype(jnp.int32), axis=1),
                          N_EXPERTS - 1)
    n_active = (pad_end[-1:] // TM_EXP).astype(jnp.int32)

    fill_lo = jnp.concatenate([pad_start + sizes, pad_end[-1:]]).astype(jnp.int32)
    fill_hi = jnp.concatenate([pad_end, jnp.full((1,), n_rows)]).astype(jnp.int32)
    xs = _dispatch(fill_lo, fill_hi, dest_flat, h2t, n_rows)
    ys = _experts(block_e, n_active, xs, w_gate[0], w_up[0], w_down[0])
    out = _combine(dest_flat, ys, x1.reshape(t, d), h2t, gate_kt.T, mod,
                   ws_gate[0].astype(BF16), ws_up[0].astype(BF16), ws_down[0].astype(BF16),
                   final_g.reshape(1, d), s)
    return out.reshape(b, s, d)
```

```python
import functools

import jax
import jax.numpy as jnp
from jax import lax
from jax.experimental import pallas as pl
from jax.experimental.pallas import tpu as pltpu

F32 = jnp.float32
BF16 = jnp.bfloat16

D_MODEL = 1024
GRID_W = 64
N_HEADS = 8
N_KV_HEADS = 2
HEAD_DIM = 64
ATTN_WIDTH = N_HEADS * HEAD_DIM
KV_WIDTH = N_KV_HEADS * HEAD_DIM
WINDOW = 128
ROPE_THETA = 10000.0
POOL_WINDOWS = (2, 4, 8, 16)
POOL_WIDTH = D_MODEL - ATTN_WIDTH
POOL_GROUP_DIM = POOL_WIDTH // len(POOL_WINDOWS)
IN_COLS = ATTN_WIDTH + 2 * KV_WIDTH + POOL_WIDTH
N_EXPERTS = 64
TOP_K = 8
N_EXPERT_GROUPS = 8
EXPERTS_PER_GROUP = N_EXPERTS // N_EXPERT_GROUPS
TOPK_GROUPS = 4
D_EXPERT = 256
D_SHARED = 256
ROUTED_SCALE = 2.5
EPS = 1e-6
LOG2E = 1.4426950408889634

LANES = 128
ROW_TILE = D_MODEL // (2 * LANES)
U32 = jnp.uint32
VMEM_LIMIT = 48 * 1024 * 1024

TM_PROJ = 512
TQ = 512
QB = 128
POOL_SLAB = 256
POOL_OFF = 64
TR = 512
TM_EXP = 512
TD = 256
TF = 256


def _silu(x):
    return x * (1.0 / (1.0 + jnp.exp(-x)))


def _split_bf16(x):
    hi = x.astype(BF16)
    lo = (x - hi.astype(F32)).astype(BF16)
    return hi, lo


def _dot(a, b):
    return jnp.dot(a, b, preferred_element_type=F32)


def _store_row_tiles(ref, val):
    rows, half = val.shape[0], val.shape[1] // 2
    lo = lax.bitcast_convert_type(val[:, :half].astype(BF16).astype(F32), U32)
    hi = lax.bitcast_convert_type(val[:, half:].astype(BF16).astype(F32), U32)
    words = lax.shift_right_logical(lo, jnp.uint32(16)) | hi
    for c in range(ROW_TILE):
        ref[pl.ds(c, rows, stride=ROW_TILE), :] = words[:, c * LANES:(c + 1) * LANES]


def _load_row_tiles(ref, rows, base=0):
    words = jnp.concatenate([ref[pl.ds(base + c, rows, stride=ROW_TILE), :] for c in range(ROW_TILE)], axis=1)
    lo = lax.bitcast_convert_type(lax.shift_left(words, jnp.uint32(16)), F32)
    hi = lax.bitcast_convert_type(words & jnp.uint32(0xFFFF0000), F32)
    return jnp.concatenate([lo, hi], axis=1)


def _dot_nt(a, b):
    return lax.dot_general(a, b, (((1,), (1,)), ((), ())), preferred_element_type=F32)


def _ada_kernel(c_ref, w_ref, b_ref, o_ref):
    a_hi, a_lo = _split_bf16(_silu(c_ref[...]))
    w_hi, w_lo = _split_bf16(w_ref[...])
    o_ref[...] = _dot(a_hi, w_hi) + _dot(a_lo, w_hi) + _dot(a_hi, w_lo) + b_ref[...]


def _ada(c8, w_ada, b_ada):
    d = c8.shape[1]
    n = w_ada.shape[1]
    tn = 512
    return pl.pallas_call(
        _ada_kernel,
        out_shape=jax.ShapeDtypeStruct((8, n), F32),
        grid=(n // tn,),
        in_specs=[pl.BlockSpec((8, d), lambda j: (0, 0)),
                  pl.BlockSpec((d, tn), lambda j: (0, j)),
                  pl.BlockSpec((1, tn), lambda j: (0, j))],
        out_specs=pl.BlockSpec((8, tn), lambda j: (0, j)),
        compiler_params=pltpu.CompilerParams(vmem_limit_bytes=VMEM_LIMIT),
        name="ada",
    )(c8, w_ada, b_ada.reshape(1, n))


def _norm_mod(x, g, shift, scale):
    ms = jnp.mean(x * x, axis=-1, keepdims=True)
    return (x * lax.rsqrt(ms + EPS) * g) * (1.0 + scale) + shift


def _lane_variants(t):
    lane = lax.broadcasted_iota(jnp.int32, t.shape, 1)
    lo = lane < HEAD_DIM
    tr = pltpu.roll(t, HEAD_DIM, 1)
    zero = jnp.zeros_like(t)
    return (jnp.where(lo, t, zero), jnp.where(lo, zero, tr),
            jnp.where(lo, tr, zero), jnp.where(lo, zero, t))


def _store_variants(ref, t):
    for i, var in enumerate(_lane_variants(t)):
        ref[0, :, i * LANES:(i + 1) * LANES] = var.astype(BF16)


def _inproj_kernel(x_ref, mod_ref, g_ref, w_ref, cos_ref, sin_ref, q_ref, k_ref, v_ref, p_ref):
    h = _norm_mod(x_ref[0], g_ref[...], mod_ref[0, 0:1, :], mod_ref[0, 1:2, :])
    z = _dot(h.astype(BF16), w_ref[...])
    cos = cos_ref[...]
    sin = sin_ref[...]
    lane = lax.broadcasted_iota(jnp.int32, cos.shape, 1)
    first_half = (lane & 16) == 0

    def rope(zc):
        partner = jnp.where(first_half, pltpu.roll(zc, LANES - 16, 1), pltpu.roll(zc, 16, 1))
        return zc * cos + partner * sin

    scale = HEAD_DIM ** -0.5 * LOG2E
    for c in range(ATTN_WIDTH // LANES):
        q_ref[0, :, c * LANES:(c + 1) * LANES] = (rope(z[:, c * LANES:(c + 1) * LANES]) * scale).astype(BF16)
    _store_variants(k_ref, rope(z[:, ATTN_WIDTH:ATTN_WIDTH + KV_WIDTH]))
    _store_variants(v_ref, z[:, ATTN_WIDTH + KV_WIDTH:ATTN_WIDTH + 2 * KV_WIDTH])
    p_ref[0] = z[:, ATTN_WIDTH + 2 * KV_WIDTH:]


def _inproj(x, mod, g1, w_in_bf, cos_t, sin_t):
    b, s, d = x.shape
    tm = TM_PROJ
    return pl.pallas_call(
        _inproj_kernel,
        out_shape=(jax.ShapeDtypeStruct((b, s, ATTN_WIDTH), BF16),
                   jax.ShapeDtypeStruct((b, s, 4 * LANES), BF16),
                   jax.ShapeDtypeStruct((b, s, 4 * LANES), BF16),
                   jax.ShapeDtypeStruct((b, s, POOL_WIDTH), F32)),
        grid=(s // tm, b),
        in_specs=[pl.BlockSpec((1, tm, d), lambda n, bi: (bi, n, 0)),
                  pl.BlockSpec((1, 6, d), lambda n, bi: (bi, 0, 0)),
                  pl.BlockSpec((1, d), lambda n, bi: (0, 0)),
                  pl.BlockSpec((d, IN_COLS), lambda n, bi: (0, 0)),
                  pl.BlockSpec((tm, LANES), lambda n, bi: (n, 0)),
                  pl.BlockSpec((tm, LANES), lambda n, bi: (n, 0))],
        out_specs=(pl.BlockSpec((1, tm, ATTN_WIDTH), lambda n, bi: (bi, n, 0)),
                   pl.BlockSpec((1, tm, 4 * LANES), lambda n, bi: (bi, n, 0)),
                   pl.BlockSpec((1, tm, 4 * LANES), lambda n, bi: (bi, n, 0)),
                   pl.BlockSpec((1, tm, POOL_WIDTH), lambda n, bi: (bi, n, 0))),
        compiler_params=pltpu.CompilerParams(vmem_limit_bytes=VMEM_LIMIT),
        name="inproj",
    )(x, mod, g1, w_in_bf, cos_t, sin_t)


def _ctxproj_kernel(x_ref, mod_ref, g_ref, w_ref, k_ref, v_ref):
    h = _norm_mod(x_ref[0], g_ref[...], mod_ref[0, 0:1, :], mod_ref[0, 1:2, :])
    z = _dot(h.astype(BF16), w_ref[...])
    _store_variants(k_ref, z[:, :KV_WIDTH])
    _store_variants(v_ref, z[:, KV_WIDTH:])


def _ctxproj(ctx, mod_c, g1, w_kv_bf):
    b, c, d = ctx.shape
    return pl.pallas_call(
        _ctxproj_kernel,
        out_shape=(jax.ShapeDtypeStruct((b, c, 4 * LANES), BF16),
                   jax.ShapeDtypeStruct((b, c, 4 * LANES), BF16)),
        grid=(b,),
        in_specs=[pl.BlockSpec((1, c, d), lambda bi: (bi, 0, 0)),
                  pl.BlockSpec((1, 6, d), lambda bi: (0, 0, 0)),
                  pl.BlockSpec((1, d), lambda bi: (0, 0)),
                  pl.BlockSpec((d, 2 * KV_WIDTH), lambda bi: (0, 0))],
        out_specs=(pl.BlockSpec((1, c, 4 * LANES), lambda bi: (bi, 0, 0)),
                   pl.BlockSpec((1, c, 4 * LANES), lambda bi: (bi, 0, 0))),
        compiler_params=pltpu.CompilerParams(vmem_limit_bytes=VMEM_LIMIT),
        name="ctxproj",
    )(ctx, mod_c, g1, w_kv_bf)


def _fold_lane_tiles(op, *arrays):
    tiles = [a[:, i * LANES:(i + 1) * LANES] for a in arrays for i in range(a.shape[1] // LANES)]
    while len(tiles) > 1:
        tiles = [op(tiles[i], tiles[i + 1]) if i + 1 < len(tiles) else tiles[i] for i in range(0, len(tiles), 2)]
    return tiles[0]


def _attn_kernel(seq_len, sink_ref, q_ref, k_ref, kp_ref, kn_ref, v_ref, vp_ref, vn_ref, kc_ref, vc_ref,
                 p_ref, pp_ref, pn_ref, x_ref, mod_ref, band_ref, poolw_ref, pscale_ref, wout_ref,
                 g2_ref, wrh_ref, wrl_ref, x1_ref, h2_ref, lg_ref, kwin, vwin, pext, mix, s_scr, p_scr, m_scr):
    n = pl.program_id(1)
    n_last = pl.num_programs(1) - 1

    kwin[0:QB, :] = kp_ref[0]
    kwin[QB:QB + TQ, :] = k_ref[0]
    kwin[QB + TQ:, :] = kn_ref[0]
    vwin[0:QB, :] = vp_ref[0]
    vwin[QB:QB + TQ, :] = v_ref[0]
    vwin[QB + TQ:, :] = vn_ref[0]

    pext[0:QB - 8, :] = jnp.zeros((QB - 8, POOL_WIDTH), F32)
    pext[QB - 8:QB, :] = jnp.where(n > 0, pp_ref[0], 0.0)
    pext[QB:QB + TQ, :] = p_ref[0]
    pext[QB + TQ:QB + TQ + 8, :] = jnp.where(n < n_last, pn_ref[0], 0.0)
    pext[QB + TQ + 8:, :] = jnp.zeros((QB - 8, POOL_WIDTH), F32)

    row = lax.broadcasted_iota(jnp.int32, (QB, 3 * QB), 0)
    col = lax.broadcasted_iota(jnp.int32, (QB, 3 * QB), 1)
    in_band = (col >= row) & (col <= row + 2 * WINDOW)
    tok = lax.broadcasted_iota(jnp.int32, (QB, 1), 0)
    kc = kc_ref[0]
    vc = vc_ref[0]

    def sub_block(j, carry):
        r0 = pl.multiple_of(j * QB, QB)
        qj = q_ref[0, pl.ds(r0, QB), :]
        kw = kwin[pl.ds(r0, 3 * QB), :]
        vw = vwin[pl.ds(r0, 3 * QB), :]
        kpos = col + (n * TQ + j * QB - QB)
        ok = in_band & (kpos >= 0) & (kpos < seq_len)
        bias = jnp.where(ok, 0.0, -jnp.inf)
        n_loc = 3 * QB

        def kv_lanes(head):
            var = 2 * (head // (N_HEADS // N_KV_HEADS)) + head % 2
            return slice(var * LANES, (var + 1) * LANES)

        for head in range(N_HEADS):
            qc = qj[:, (head // 2) * LANES:(head // 2 + 1) * LANES]
            s_scr[head, :, 0:n_loc] = _dot_nt(qc, kw[:, kv_lanes(head)]) + bias
            s_scr[head, :, n_loc:] = _dot_nt(qc, kc[:, kv_lanes(head)])
        n_tiles = (n_loc + kc.shape[0]) // LANES
        for head in range(N_HEADS):
            row_max = jnp.max(_fold_lane_tiles(jnp.maximum, s_scr[head]), axis=1, keepdims=True)
            m_scr[head] = jnp.broadcast_to(jnp.maximum(row_max, sink_ref[head] * LOG2E), (QB, LANES))
        for head in range(N_HEADS):
            m = m_scr[head]
            acc = None
            for i in range(n_tiles):
                p = jnp.exp2(s_scr[head, :, i * LANES:(i + 1) * LANES] - m)
                p_scr[head, :, i * LANES:(i + 1) * LANES] = p.astype(BF16)
                acc = p if acc is None else acc + p
            denom = (jnp.broadcast_to(jnp.sum(acc, axis=1, keepdims=True), (QB, LANES))
                     + jnp.exp2(sink_ref[head] * LOG2E - m))
            m_scr[head] = 1.0 / denom
        for c in range(N_HEADS // 2):
            pair = None
            for head in (2 * c, 2 * c + 1):
                o = (_dot(p_scr[head, :, 0:n_loc], vw[:, kv_lanes(head)])
                     + _dot(p_scr[head, :, n_loc:], vc[:, kv_lanes(head)])) * m_scr[head]
                pair = o if pair is None else pair + o
            mix[pl.ds(r0, QB), c * LANES:(c + 1) * LANES] = pair.astype(BF16)

        slab = pext[pl.ds(pl.multiple_of(r0 + POOL_OFF, 8), POOL_SLAB), :]
        tpos = tok + (n * TQ + j * QB)
        for g, w in enumerate(POOL_WINDOWS):
            sg = slab[:, g * LANES:(g + 1) * LANES]
            hi, lo = _split_bf16(sg)
            band = band_ref[g]
            wsum = _dot(band, hi) + _dot(band, lo)
            cnt = (jnp.minimum(tpos - w // 2 + w, seq_len) - jnp.maximum(tpos - w // 2, 0)).astype(F32)
            dlt = wsum / cnt - sg[POOL_OFF:POOL_OFF + QB, :]
            y = _dot(dlt.astype(BF16), poolw_ref[g]) * pscale_ref[:, g * LANES:(g + 1) * LANES]
            mix[pl.ds(r0, QB), ATTN_WIDTH + g * LANES:ATTN_WIDTH + (g + 1) * LANES] = y.astype(BF16)
        return carry

    lax.fori_loop(0, TQ // QB, sub_block, 0)

    proj = _dot(mix[...], wout_ref[...])
    x1 = x_ref[0] + mod_ref[0, 2:3, :] * proj
    x1_ref[0] = x1
    h2 = _norm_mod(x1, g2_ref[...], mod_ref[0, 3:4, :], mod_ref[0, 4:5, :])
    _store_row_tiles(h2_ref, h2)
    h_hi, h_lo = _split_bf16(h2)
    wrh = wrh_ref[...]
    lg_ref[...] = _dot_nt(wrh, h_hi) + _dot_nt(wrh, h_lo) + _dot_nt(wrl_ref[...], h_hi)


def _attn(sink, q, k4, v4, kc4, vc4, p, x, mod, band, poolw_bf, pscale, wout_bf, g2, wr_hi, wr_lo):
    b, s, d = x.shape
    c = kc4.shape[1]
    nt = s // TQ
    hb = TQ // QB
    pb = TQ // 8
    kv_main = pl.BlockSpec((1, TQ, 4 * LANES), lambda bi, n: (bi, n, 0))
    kv_prev = pl.BlockSpec((1, QB, 4 * LANES), lambda bi, n: (bi, jnp.maximum(n * hb - 1, 0), 0))
    kv_next = pl.BlockSpec((1, QB, 4 * LANES), lambda bi, n: (bi, jnp.minimum((n + 1) * hb, s // QB - 1), 0))
    const2 = lambda bi, n: (0, 0)
    const3 = lambda bi, n: (0, 0, 0)
    return pl.pallas_call(
        functools.partial(_attn_kernel, s),
        out_shape=(jax.ShapeDtypeStruct((b, s, d), F32),
                   jax.ShapeDtypeStruct((b * s * ROW_TILE, LANES), U32),
                   jax.ShapeDtypeStruct((N_EXPERTS, b * s), F32)),
        grid=(b, nt),
        in_specs=[pl.BlockSpec(memory_space=pltpu.SMEM),
                  pl.BlockSpec((1, TQ, ATTN_WIDTH), lambda bi, n: (bi, n, 0)),
                  kv_main, kv_prev, kv_next, kv_main, kv_prev, kv_next,
                  pl.BlockSpec((1, c, 4 * LANES), lambda bi, n: (bi, 0, 0)),
                  pl.BlockSpec((1, c, 4 * LANES), lambda bi, n: (bi, 0, 0)),
                  pl.BlockSpec((1, TQ, POOL_WIDTH), lambda bi, n: (bi, n, 0)),
                  pl.BlockSpec((1, 8, POOL_WIDTH), lambda bi, n: (bi, jnp.maximum(n * pb - 1, 0), 0)),
                  pl.BlockSpec((1, 8, POOL_WIDTH), lambda bi, n: (bi, jnp.minimum((n + 1) * pb, s // 8 - 1), 0)),
                  pl.BlockSpec((1, TQ, d), lambda bi, n: (bi, n, 0)),
                  pl.BlockSpec((1, 6, d), lambda bi, n: (bi, 0, 0)),
                  pl.BlockSpec((len(POOL_WINDOWS), QB, POOL_SLAB), const3),
                  pl.BlockSpec((len(POOL_WINDOWS), POOL_GROUP_DIM, POOL_GROUP_DIM), const3),
                  pl.BlockSpec((1, POOL_WIDTH), const2),
                  pl.BlockSpec((d, d), const2),
                  pl.BlockSpec((1, d), const2),
                  pl.BlockSpec((N_EXPERTS, d), const2),
                  pl.BlockSpec((N_EXPERTS, d), const2)],
        out_specs=(pl.BlockSpec((1, TQ, d), lambda bi, n: (bi, n, 0)),
                   pl.BlockSpec((TQ * ROW_TILE, LANES), lambda bi, n: (bi * nt + n, 0)),
                   pl.BlockSpec((N_EXPERTS, TQ), lambda bi, n: (0, bi * nt + n))),
        scratch_shapes=[pltpu.VMEM((TQ + 2 * QB, 4 * LANES), BF16),
                        pltpu.VMEM((TQ + 2 * QB, 4 * LANES), BF16),
                        pltpu.VMEM((TQ + 2 * QB, POOL_WIDTH), F32),
                        pltpu.VMEM((TQ, d), BF16),
                        pltpu.VMEM((N_HEADS, QB, 3 * QB + c), F32),
                        pltpu.VMEM((N_HEADS, QB, 3 * QB + c), BF16),
                        pltpu.VMEM((N_HEADS, QB, LANES), F32)],
        compiler_params=pltpu.CompilerParams(vmem_limit_bytes=VMEM_LIMIT),
        name="attn",
    )(sink, q, k4, k4, k4, v4, v4, v4, kc4, vc4, p, p, p, x, mod, band, poolw_bf, pscale, wout_bf,
      g2, wr_hi, wr_lo)


def _first_argmax_rows(v, row_iota, n_rows):
    m = jnp.max(v, axis=0, keepdims=True)
    idx = jnp.min(jnp.where(v == m, row_iota, n_rows), axis=0, keepdims=True)
    return m, idx


def _route_kernel(lg_ref, bias_ref, tri_ref, idx_ref, gate_ref, rank_ref, cnt_ref, carry):
    i = pl.program_id(0)

    @pl.when(i == 0)
    def _():
        carry[...] = jnp.zeros_like(carry)

    scores = 1.0 / (1.0 + jnp.exp(-lg_ref[...]))
    biased = scores + bias_ref[...]
    e_iota = lax.broadcasted_iota(jnp.int32, scores.shape, 0).astype(F32)
    g_iota = lax.broadcasted_iota(jnp.int32, (EXPERTS_PER_GROUP, TR), 0).astype(F32)
    neg = -jnp.inf

    grp = []
    for g in range(N_EXPERT_GROUPS):
        blk = biased[g * EXPERTS_PER_GROUP:(g + 1) * EXPERTS_PER_GROUP, :]
        m1, i1 = _first_argmax_rows(blk, g_iota, float(EXPERTS_PER_GROUP))
        m2 = jnp.max(jnp.where(g_iota == i1, neg, blk), axis=0, keepdims=True)
        grp.append(m1 + m2)
    grp = jnp.concatenate(grp, axis=0)
    gg_iota = lax.broadcasted_iota(jnp.int32, grp.shape, 0).astype(F32)
    grp_sel = jnp.zeros(grp.shape, F32)
    for _ in range(TOPK_GROUPS):
        _, gi = _first_argmax_rows(grp, gg_iota, float(N_EXPERT_GROUPS))
        hit = gg_iota == gi
        grp_sel = jnp.where(hit, 1.0, grp_sel)
        grp = jnp.where(hit, neg, grp)
    allowed = jnp.concatenate(
        [jnp.broadcast_to(grp_sel[g:g + 1, :], (EXPERTS_PER_GROUP, TR)) for g in range(N_EXPERT_GROUPS)], axis=0)
    masked = jnp.where(allowed > 0.5, biased, neg)

    idxs, gates = [], []
    onehot = jnp.zeros(scores.shape, F32)
    for _ in range(TOP_K):
        _, ei = _first_argmax_rows(masked, e_iota, float(N_EXPERTS))
        hit = e_iota == ei
        idxs.append(ei)
        gates.append(jnp.sum(jnp.where(hit, scores, 0.0), axis=0, keepdims=True))
        onehot = jnp.where(hit, 1.0, onehot)
        masked = jnp.where(hit, neg, masked)
    idx = jnp.concatenate(idxs, axis=0)
    gate = jnp.concatenate(gates, axis=0)
    gate = gate / jnp.sum(gate, axis=0, keepdims=True) * ROUTED_SCALE

    before = _dot(onehot.astype(BF16), tri_ref[...]) + carry[:, 0:1]
    ranks = [jnp.sum(jnp.where(e_iota == idxs[k], before, 0.0), axis=0, keepdims=True) for k in range(TOP_K)]
    idx_ref[...] = idx.astype(jnp.int32)
    gate_ref[...] = gate
    rank_ref[...] = jnp.concatenate(ranks, axis=0).astype(jnp.int32)
    total = carry[...] + jnp.sum(onehot, axis=1, keepdims=True)
    carry[...] = total
    cnt_ref[...] = total


def _route(lg_t, bias, tri):
    e, t = lg_t.shape
    tok = pl.BlockSpec((TOP_K, TR), lambda i: (0, i))
    return pl.pallas_call(
        _route_kernel,
        out_shape=(jax.ShapeDtypeStruct((TOP_K, t), jnp.int32),
                   jax.ShapeDtypeStruct((TOP_K, t), F32),
                   jax.ShapeDtypeStruct((TOP_K, t), jnp.int32),
                   jax.ShapeDtypeStruct((e, LANES), F32)),
        grid=(t // TR,),
        in_specs=[pl.BlockSpec((e, TR), lambda i: (0, i)),
                  pl.BlockSpec((e, 1), lambda i: (0, 0)),
                  pl.BlockSpec((TR, TR), lambda i: (0, 0))],
        out_specs=(tok, tok, tok, pl.BlockSpec((e, LANES), lambda i: (0, 0))),
        scratch_shapes=[pltpu.VMEM((e, LANES), F32)],
        compiler_params=pltpu.CompilerParams(vmem_limit_bytes=VMEM_LIMIT),
        name="route",
    )(lg_t, bias, tri)


def _tile_rows(ref, row):
    return ref.at[pl.ds(pl.multiple_of(row * ROW_TILE, ROW_TILE), ROW_TILE), :]


def _dispatch_kernel(fill_lo, fill_hi, dest_hbm, h_ref, xs_hbm, idx_smem, zero_rows, sem_idx, sem_rows):
    i = pl.program_id(0)

    @pl.when(i == 0)
    def _():
        zero_rows[...] = jnp.zeros_like(zero_rows)

        def fill_copy(first_row, n):
            dst = xs_hbm.at[pl.ds(pl.multiple_of(first_row * ROW_TILE, ROW_TILE), n * ROW_TILE), :]
            return pltpu.make_async_copy(zero_rows.at[0:n * ROW_TILE, :], dst, sem_rows)

        def fill_expert(e, carry):
            n_pad = fill_hi[e] - fill_lo[e]
            for wait in (False, True):
                row = fill_lo[e]
                bit = TM_EXP // 2
                while bit:
                    take = (n_pad & bit) != 0

                    @pl.when(take)
                    def _(row=row, bit=bit, wait=wait):
                        cp = fill_copy(row, bit)
                        cp.wait() if wait else cp.start()

                    row = row + jnp.where(take, bit, 0)
                    bit //= 2
            return carry

        lax.fori_loop(0, N_EXPERTS, fill_expert, 0)

        def start_block(blk, c):
            fill_copy(blk * TM_EXP, TM_EXP).start()
            return c

        def wait_block(blk, c):
            fill_copy(blk * TM_EXP, TM_EXP).wait()
            return c

        first_tail = fill_lo[N_EXPERTS] // TM_EXP
        n_blocks = xs_hbm.shape[0] // (TM_EXP * ROW_TILE)
        lax.fori_loop(first_tail, n_blocks, start_block, 0)
        lax.fori_loop(first_tail, n_blocks, wait_block, 0)

    n_idx = TD * TOP_K
    cp = pltpu.make_async_copy(dest_hbm.at[pl.ds(i * n_idx, n_idx)], idx_smem, sem_idx)
    cp.start()
    cp.wait()

    def issue(t, carry):
        src = _tile_rows(h_ref, t)
        for k in range(TOP_K):
            pltpu.make_async_copy(src, _tile_rows(xs_hbm, idx_smem[t * TOP_K + k]), sem_rows).start(priority=k % 2)
        return carry

    lax.fori_loop(0, TD, issue, 0, unroll=2)
    for _ in range(TOP_K):
        pltpu.make_async_copy(h_ref, xs_hbm.at[pl.ds(0, TD * ROW_TILE), :], sem_rows).wait()


def _dispatch(fill_lo, fill_hi, dest_flat, h2t, n_rows):
    t = h2t.shape[0] // ROW_TILE
    grid_spec = pltpu.PrefetchScalarGridSpec(
        num_scalar_prefetch=2,
        grid=(t // TD,),
        in_specs=[pl.BlockSpec(memory_space=pl.ANY),
                  pl.BlockSpec((TD * ROW_TILE, LANES), lambda i, lo, hi: (i, 0))],
        out_specs=pl.BlockSpec(memory_space=pl.ANY),
        scratch_shapes=[pltpu.SMEM((TD * TOP_K,), jnp.int32),
                        pltpu.VMEM((TM_EXP * ROW_TILE, LANES), U32),
                        pltpu.SemaphoreType.DMA, pltpu.SemaphoreType.DMA],
    )
    return pl.pallas_call(
        _dispatch_kernel,
        out_shape=jax.ShapeDtypeStruct((n_rows * ROW_TILE, LANES), U32),
        grid_spec=grid_spec,
        compiler_params=pltpu.CompilerParams(has_side_effects=True, disable_bounds_checks=True),
        name="dispatch",
    )(fill_lo, fill_hi, dest_flat, h2t)


def _experts_kernel(be_ref, nact_ref, xs_ref, wg_ref, wu_ref, wd_ref, ys_ref, wg_bf, wu_bf, wd_bf):
    i = pl.program_id(0)
    prev = be_ref[jnp.maximum(i - 1, 0)]

    @pl.when((i == 0) | (be_ref[i] != prev))
    def _():
        wg_bf[...] = wg_ref[0].astype(BF16)
        wu_bf[...] = wu_ref[0].astype(BF16)
        wd_bf[...] = wd_ref[0].astype(BF16)

    @pl.when(i < nact_ref[0])
    def _():
        xb = _load_row_tiles(xs_ref, TM_EXP).astype(BF16)
        hid = _silu(_dot(xb, wg_bf[...])) * _dot(xb, wu_bf[...])
        _store_row_tiles(ys_ref, _dot(hid.astype(BF16), wd_bf[...]))

    @pl.when(i >= nact_ref[0])
    def _():
        ys_ref[...] = jnp.zeros_like(ys_ref)


def _experts(block_e, n_active, xs, w_gate, w_up, w_down):
    d = w_gate.shape[1]
    nb = xs.shape[0] // (TM_EXP * ROW_TILE)
    grid_spec = pltpu.PrefetchScalarGridSpec(
        num_scalar_prefetch=2,
        grid=(nb,),
        in_specs=[pl.BlockSpec((TM_EXP * ROW_TILE, LANES), lambda i, be, na: (jnp.minimum(i, na[0] - 1), 0)),
                  pl.BlockSpec((1, d, D_EXPERT), lambda i, be, na: (be[i], 0, 0)),
                  pl.BlockSpec((1, d, D_EXPERT), lambda i, be, na: (be[i], 0, 0)),
                  pl.BlockSpec((1, D_EXPERT, d), lambda i, be, na: (be[i], 0, 0))],
        out_specs=pl.BlockSpec((TM_EXP * ROW_TILE, LANES), lambda i, be, na: (i, 0)),
        scratch_shapes=[pltpu.VMEM((d, D_EXPERT), BF16), pltpu.VMEM((d, D_EXPERT), BF16),
                        pltpu.VMEM((D_EXPERT, d), BF16)],
    )
    return pl.pallas_call(
        _experts_kernel,
        out_shape=jax.ShapeDtypeStruct(xs.shape, U32),
        grid_spec=grid_spec,
        compiler_params=pltpu.CompilerParams(vmem_limit_bytes=VMEM_LIMIT),
        name="experts",
    )(block_e, n_active, xs, w_gate, w_up, w_down)


def _combine_kernel(dest_hbm, ys_hbm, x1_ref, h2_ref, gate_ref, mod_ref, wsg_ref, wsu_ref, wsd_ref, fg_ref,
                    out_ref, idx_smem, buf, sem_idx, sem_rows):
    i = pl.program_id(0)
    n_idx = TF * TOP_K
    cp = pltpu.make_async_copy(dest_hbm.at[pl.ds(i * n_idx, n_idx)], idx_smem, sem_idx)
    cp.start()
    cp.wait()

    def issue(t, carry):
        for k in range(TOP_K):
            pltpu.make_async_copy(_tile_rows(ys_hbm, idx_smem[t * TOP_K + k]), _tile_rows(buf, k * TF + t),
                                  sem_rows).start(priority=k % 2)
        return carry

    lax.fori_loop(0, TF, issue, 0, unroll=2)

    hb = _load_row_tiles(h2_ref, TF).astype(BF16)
    hid = _silu(_dot(hb, wsg_ref[...])) * _dot(hb, wsu_ref[...])
    ffn = _dot(hid.astype(BF16), wsd_ref[...])

    pltpu.make_async_copy(ys_hbm.at[pl.ds(0, TOP_K * TF * ROW_TILE), :], buf, sem_rows).wait()
    gate = gate_ref[...]
    for k in range(TOP_K):
        ffn = ffn + gate[:, k:k + 1] * _load_row_tiles(buf, TF, base=k * TF * ROW_TILE)
    x2 = x1_ref[...] + mod_ref[0, 5:6, :] * ffn
    ms = jnp.mean(x2 * x2, axis=-1, keepdims=True)
    out_ref[...] = x2 * lax.rsqrt(ms + EPS) * fg_ref[...]


def _combine(dest_flat, ys, x1, h2t, gate_tk, mod, wsg_bf, wsu_bf, wsd_bf, final_g, seq_len):
    t, d = x1.shape
    tiles_per_seq = seq_len // TF
    tok = pl.BlockSpec((TF, d), lambda i: (i, 0))
    return pl.pallas_call(
        _combine_kernel,
        out_shape=jax.ShapeDtypeStruct((t, d), F32),
        grid=(t // TF,),
        in_specs=[pl.BlockSpec(memory_space=pl.ANY), pl.BlockSpec(memory_space=pl.ANY),
                  tok, pl.BlockSpec((TF * ROW_TILE, LANES), lambda i: (i, 0)),
                  pl.BlockSpec((TF, TOP_K), lambda i: (i, 0)),
                  pl.BlockSpec((1, 6, d), lambda i: (i // tiles_per_seq, 0, 0)),
                  pl.BlockSpec((d, D_SHARED), lambda i: (0, 0)),
                  pl.BlockSpec((d, D_SHARED), lambda i: (0, 0)),
                  pl.BlockSpec((D_SHARED, d), lambda i: (0, 0)),
                  pl.BlockSpec((1, d), lambda i: (0, 0))],
        out_specs=tok,
        scratch_shapes=[pltpu.SMEM((TF * TOP_K,), jnp.int32),
                        pltpu.VMEM((TOP_K * TF * ROW_TILE, LANES), U32),
                        pltpu.SemaphoreType.DMA, pltpu.SemaphoreType.DMA],
        compiler_params=pltpu.CompilerParams(vmem_limit_bytes=VMEM_LIMIT, disable_bounds_checks=True),
        name="combine",
    )(dest_flat, ys, x1, h2t, gate_tk, mod, wsg_bf, wsu_bf, wsd_bf, final_g)


def _rope_tables(n_tokens):
    n_rows = n_tokens // GRID_W
    n_freq = HEAD_DIM // 4
    inv_freq = ROPE_THETA ** (-jnp.arange(n_freq, dtype=F32) / n_freq)
    ang_r = jnp.arange(n_rows).astype(F32)[:, None] * inv_freq[None, :]
    ang_c = jnp.arange(GRID_W).astype(F32)[:, None] * inv_freq[None, :]

    def per_token(row_part, col_part):
        rows = jnp.broadcast_to(row_part[:, None, :], (n_rows, GRID_W, n_freq))
        cols = jnp.broadcast_to(col_part[None, :, :], (n_rows, GRID_W, n_freq))
        return rows.reshape(n_tokens, n_freq), cols.reshape(n_tokens, n_freq)

    cos_r, cos_c = per_token(jnp.cos(ang_r), jnp.cos(ang_c))
    sin_r, sin_c = per_token(jnp.sin(ang_r), jnp.sin(ang_c))
    cos = jnp.concatenate([cos_r, cos_r, cos_c, cos_c], axis=1)
    sin = jnp.concatenate([-sin_r, sin_r, -sin_c, sin_c], axis=1)
    reps = LANES // HEAD_DIM
    return jnp.tile(cos, (1, reps)), jnp.tile(sin, (1, reps))


def _pool_bands():
    i = jnp.arange(QB)[:, None]
    r = jnp.arange(POOL_SLAB)[None, :]
    return jnp.stack([((r >= i + POOL_OFF - w // 2) & (r < i + POOL_OFF + w // 2)).astype(BF16)
                      for w in POOL_WINDOWS])


def kernel(x, c, ctx, c_ctx, w_ada, b_ada, norm1_g, norm2_g, w_in, attn_sink, pool_w, pool_scale, w_out,
           w_router, router_bias, w_gate, w_up, w_down, ws_gate, ws_up, ws_down, final_g):
    b, s, d = x.shape
    t = b * s
    assert w_ada.shape[0] == 1 and d == D_MODEL and s % TQ == 0 and b + 1 <= 8

    c8 = jnp.zeros((8, d), F32).at[:b].set(c).at[b].set(c_ctx)
    mod = _ada(c8, w_ada[0], b_ada[0]).reshape(8, 6, d)
    g1 = norm1_g[0].reshape(1, d)
    g2 = norm2_g[0].reshape(1, d)
    w_in_bf = w_in[0].astype(BF16)
    cos_t, sin_t = _rope_tables(s)

    q, k4, v4, p = _inproj(x, mod, g1, w_in_bf, cos_t, sin_t)
    kc4, vc4 = _ctxproj(ctx, mod[b:b + 1], g1, w_in_bf[:, ATTN_WIDTH:ATTN_WIDTH + 2 * KV_WIDTH])

    wr_t = w_router[0].T
    wr_hi = wr_t.astype(BF16)
    wr_lo = (wr_t - wr_hi.astype(F32)).astype(BF16)
    x1, h2t, lg_t = _attn(attn_sink[0], q, k4, v4, kc4, vc4, p, x, mod, _pool_bands(),
                         pool_w[0].astype(BF16), pool_scale[0].reshape(1, POOL_WIDTH), w_out[0].astype(BF16),
                         g2, wr_hi, wr_lo)

    tri = jnp.triu(jnp.ones((TR, TR), BF16), k=1)
    idx_kt, gate_kt, rank_kt, counts = _route(lg_t, router_bias[0].reshape(N_EXPERTS, 1), tri)

    sizes = counts[:, 0].astype(jnp.int32)
    padded = (sizes + TM_EXP - 1) // TM_EXP * TM_EXP
    pad_end = jnp.cumsum(padded)
    pad_start = pad_end - padded
    n_rows = -(-(t * TOP_K + N_EXPERTS * (TM_EXP - 1)) // TM_EXP) * TM_EXP
    n_blocks = n_rows // TM_EXP
    experts = jnp.arange(N_EXPERTS, dtype=jnp.int32)
    start_of = jnp.sum(jnp.where(idx_kt[None] == experts[:, None, None], pad_start[:, None, None], 0), axis=0)
    dest_flat = (start_of + rank_kt).T.reshape(t * TOP_K)
    block_row = jnp.arange(n_blocks, dtype=jnp.int32) * TM_EXP
    block_e = jnp.minimum(jnp.sum((pad_end[None, :] <= block_row[:, None]).astype(jnp.int32), axis=1),
                          N_EXPERTS - 1)
    n_active = (pad_end[-1:] // TM_EXP).astype(jnp.int32)

    fill_lo = jnp.concatenate([pad_start + sizes, pad_end[-1:]]).astype(jnp.int32)
    fill_hi = jnp.concatenate([pad_end, jnp.full((1,), n_rows)]).astype(jnp.int32)
    xs = _dispatch(fill_lo, fill_hi, dest_flat, h2t, n_rows)
    ys = _experts(block_e, n_active, xs, w_gate[0], w_up[0], w_down[0])
    out = _combine(dest_flat, ys, x1.reshape(t, d), h2t, gate_kt.T, mod,
                   ws_gate[0].astype(BF16), ws_up[0].astype(BF16), ws_down[0].astype(BF16),
                   final_g.reshape(1, d), s)
    return out.reshape(b, s, d)
```

```python
import functools

import jax
import jax.numpy as jnp
from jax import lax
from jax.experimental import pallas as pl
from jax.experimental.pallas import tpu as pltpu
from jax.experimental.pallas import tpu_sc as plsc

F32 = jnp.float32
BF16 = jnp.bfloat16

D_MODEL = 1024
GRID_W = 64
N_HEADS = 8
N_KV_HEADS = 2
HEAD_DIM = 64
ATTN_WIDTH = N_HEADS * HEAD_DIM
KV_WIDTH = N_KV_HEADS * HEAD_DIM
WINDOW = 128
ROPE_THETA = 10000.0
POOL_WINDOWS = (2, 4, 8, 16)
POOL_WIDTH = D_MODEL - ATTN_WIDTH
POOL_GROUP_DIM = POOL_WIDTH // len(POOL_WINDOWS)
IN_COLS = ATTN_WIDTH + 2 * KV_WIDTH + POOL_WIDTH
N_EXPERTS = 64
TOP_K = 8
N_EXPERT_GROUPS = 8
EXPERTS_PER_GROUP = N_EXPERTS // N_EXPERT_GROUPS
TOPK_GROUPS = 4
D_EXPERT = 256
D_SHARED = 256
ROUTED_SCALE = 2.5
EPS = 1e-6
LOG2E = 1.4426950408889634

LANES = 128
ROW_TILE = D_MODEL // (2 * LANES)
U32 = jnp.uint32
VMEM_LIMIT = 48 * 1024 * 1024

TM_PROJ = 512
TQ = 512
QB = 128
POOL_SLAB = 256
POOL_OFF = 64
TR = 512
TM_EXP = 512
TD = 256
TF = 256
SC_CORES = 2
SC_WORKERS = 32
SC_TOKENS = 128


def _silu(x):
    return x * (1.0 / (1.0 + jnp.exp(-x)))


def _split_bf16(x):
    hi = x.astype(BF16)
    lo = (x - hi.astype(F32)).astype(BF16)
    return hi, lo


def _dot(a, b):
    return jnp.dot(a, b, preferred_element_type=F32)


def _pack_words(val):
    half = val.shape[1] // 2
    lo = lax.bitcast_convert_type(val[:, :half].astype(BF16).astype(F32), U32)
    hi = lax.bitcast_convert_type(val[:, half:].astype(BF16).astype(F32), U32)
    return lax.shift_right_logical(lo, jnp.uint32(16)) | hi


def _unpack_words(words):
    lo = lax.bitcast_convert_type(lax.shift_left(words, jnp.uint32(16)), F32)
    hi = lax.bitcast_convert_type(words & jnp.uint32(0xFFFF0000), F32)
    return jnp.concatenate([lo, hi], axis=1)


def _store_row_tiles(ref, val):
    rows = val.shape[0]
    words = _pack_words(val)
    for c in range(ROW_TILE):
        ref[pl.ds(c, rows, stride=ROW_TILE), :] = words[:, c * LANES:(c + 1) * LANES]


def _load_row_tiles(ref, rows, base=0):
    return _unpack_words(
        jnp.concatenate([ref[pl.ds(base + c, rows, stride=ROW_TILE), :] for c in range(ROW_TILE)], axis=1))


def _dot_nt(a, b):
    return lax.dot_general(a, b, (((1,), (1,)), ((), ())), preferred_element_type=F32)


def _ada_kernel(c_ref, w_ref, b_ref, o_ref):
    a_hi, a_lo = _split_bf16(_silu(c_ref[...]))
    w_hi, w_lo = _split_bf16(w_ref[...])
    o_ref[...] = _dot(a_hi, w_hi) + _dot(a_lo, w_hi) + _dot(a_hi, w_lo) + b_ref[...]


def _ada(c8, w_ada, b_ada):
    d = c8.shape[1]
    n = w_ada.shape[1]
    tn = 512
    return pl.pallas_call(
        _ada_kernel,
        out_shape=jax.ShapeDtypeStruct((8, n), F32),
        grid=(n // tn,),
        in_specs=[pl.BlockSpec((8, d), lambda j: (0, 0)),
                  pl.BlockSpec((d, tn), lambda j: (0, j)),
                  pl.BlockSpec((1, tn), lambda j: (0, j))],
        out_specs=pl.BlockSpec((8, tn), lambda j: (0, j)),
        compiler_params=pltpu.CompilerParams(vmem_limit_bytes=VMEM_LIMIT),
        name="ada",
    )(c8, w_ada, b_ada.reshape(1, n))


def _norm_mod(x, g, shift, scale):
    ms = jnp.mean(x * x, axis=-1, keepdims=True)
    return (x * lax.rsqrt(ms + EPS) * g) * (1.0 + scale) + shift


def _lane_variants(t):
    lane = lax.broadcasted_iota(jnp.int32, t.shape, 1)
    lo = lane < HEAD_DIM
    tr = pltpu.roll(t, HEAD_DIM, 1)
    zero = jnp.zeros_like(t)
    return (jnp.where(lo, t, zero), jnp.where(lo, zero, tr),
            jnp.where(lo, tr, zero), jnp.where(lo, zero, t))


def _store_variants(ref, t):
    for i, var in enumerate(_lane_variants(t)):
        ref[0, :, i * LANES:(i + 1) * LANES] = var.astype(BF16)


def _inproj_kernel(x_ref, mod_ref, g_ref, w_ref, cos_ref, sin_ref, q_ref, k_ref, v_ref, p_ref):
    h = _norm_mod(x_ref[0], g_ref[...], mod_ref[0, 0:1, :], mod_ref[0, 1:2, :])
    z = _dot(h.astype(BF16), w_ref[...])
    cos = cos_ref[...]
    sin = sin_ref[...]
    lane = lax.broadcasted_iota(jnp.int32, cos.shape, 1)
    first_half = (lane & 16) == 0

    def rope(zc):
        partner = jnp.where(first_half, pltpu.roll(zc, LANES - 16, 1), pltpu.roll(zc, 16, 1))
        return zc * cos + partner * sin

    scale = HEAD_DIM ** -0.5 * LOG2E
    for c in range(ATTN_WIDTH // LANES):
        q_ref[0, :, c * LANES:(c + 1) * LANES] = (rope(z[:, c * LANES:(c + 1) * LANES]) * scale).astype(BF16)
    _store_variants(k_ref, rope(z[:, ATTN_WIDTH:ATTN_WIDTH + KV_WIDTH]))
    _store_variants(v_ref, z[:, ATTN_WIDTH + KV_WIDTH:ATTN_WIDTH + 2 * KV_WIDTH])
    p_ref[0] = z[:, ATTN_WIDTH + 2 * KV_WIDTH:]


def _inproj(x, mod, g1, w_in_bf, cos_t, sin_t):
    b, s, d = x.shape
    tm = TM_PROJ
    return pl.pallas_call(
        _inproj_kernel,
        out_shape=(jax.ShapeDtypeStruct((b, s, ATTN_WIDTH), BF16),
                   jax.ShapeDtypeStruct((b, s, 4 * LANES), BF16),
                   jax.ShapeDtypeStruct((b, s, 4 * LANES), BF16),
                   jax.ShapeDtypeStruct((b, s, POOL_WIDTH), F32)),
        grid=(s // tm, b),
        in_specs=[pl.BlockSpec((1, tm, d), lambda n, bi: (bi, n, 0)),
                  pl.BlockSpec((1, 6, d), lambda n, bi: (bi, 0, 0)),
                  pl.BlockSpec((1, d), lambda n, bi: (0, 0)),
                  pl.BlockSpec((d, IN_COLS), lambda n, bi: (0, 0)),
                  pl.BlockSpec((tm, LANES), lambda n, bi: (n, 0)),
                  pl.BlockSpec((tm, LANES), lambda n, bi: (n, 0))],
        out_specs=(pl.BlockSpec((1, tm, ATTN_WIDTH), lambda n, bi: (bi, n, 0)),
                   pl.BlockSpec((1, tm, 4 * LANES), lambda n, bi: (bi, n, 0)),
                   pl.BlockSpec((1, tm, 4 * LANES), lambda n, bi: (bi, n, 0)),
                   pl.BlockSpec((1, tm, POOL_WIDTH), lambda n, bi: (bi, n, 0))),
        compiler_params=pltpu.CompilerParams(vmem_limit_bytes=VMEM_LIMIT),
        name="inproj",
    )(x, mod, g1, w_in_bf, cos_t, sin_t)


def _ctxproj_kernel(x_ref, mod_ref, g_ref, w_ref, k_ref, v_ref):
    h = _norm_mod(x_ref[0], g_ref[...], mod_ref[0, 0:1, :], mod_ref[0, 1:2, :])
    z = _dot(h.astype(BF16), w_ref[...])
    _store_variants(k_ref, z[:, :KV_WIDTH])
    _store_variants(v_ref, z[:, KV_WIDTH:])


def _ctxproj(ctx, mod_c, g1, w_kv_bf):
    b, c, d = ctx.shape
    return pl.pallas_call(
        _ctxproj_kernel,
        out_shape=(jax.ShapeDtypeStruct((b, c, 4 * LANES), BF16),
                   jax.ShapeDtypeStruct((b, c, 4 * LANES), BF16)),
        grid=(b,),
        in_specs=[pl.BlockSpec((1, c, d), lambda bi: (bi, 0, 0)),
                  pl.BlockSpec((1, 6, d), lambda bi: (0, 0, 0)),
                  pl.BlockSpec((1, d), lambda bi: (0, 0)),
                  pl.BlockSpec((d, 2 * KV_WIDTH), lambda bi: (0, 0))],
        out_specs=(pl.BlockSpec((1, c, 4 * LANES), lambda bi: (bi, 0, 0)),
                   pl.BlockSpec((1, c, 4 * LANES), lambda bi: (bi, 0, 0))),
        compiler_params=pltpu.CompilerParams(vmem_limit_bytes=VMEM_LIMIT),
        name="ctxproj",
    )(ctx, mod_c, g1, w_kv_bf)


def _fold_lane_tiles(op, *arrays):
    tiles = [a[:, i * LANES:(i + 1) * LANES] for a in arrays for i in range(a.shape[1] // LANES)]
    while len(tiles) > 1:
        tiles = [op(tiles[i], tiles[i + 1]) if i + 1 < len(tiles) else tiles[i] for i in range(0, len(tiles), 2)]
    return tiles[0]


def _attn_kernel(seq_len, sink_ref, q_ref, k_ref, kp_ref, kn_ref, v_ref, vp_ref, vn_ref, kc_ref, vc_ref,
                 p_ref, pp_ref, pn_ref, x_ref, mod_ref, band_ref, poolw_ref, pscale_ref, wout_ref,
                 g2_ref, wrh_ref, wrl_ref, x1_ref, h2_ref, lg_ref, kwin, vwin, pext, mix, s_scr, p_scr, m_scr):
    n = pl.program_id(1)
    n_last = pl.num_programs(1) - 1

    kwin[0:QB, :] = kp_ref[0]
    kwin[QB:QB + TQ, :] = k_ref[0]
    kwin[QB + TQ:, :] = kn_ref[0]
    vwin[0:QB, :] = vp_ref[0]
    vwin[QB:QB + TQ, :] = v_ref[0]
    vwin[QB + TQ:, :] = vn_ref[0]

    pext[0:QB - 8, :] = jnp.zeros((QB - 8, POOL_WIDTH), F32)
    pext[QB - 8:QB, :] = jnp.where(n > 0, pp_ref[0], 0.0)
    pext[QB:QB + TQ, :] = p_ref[0]
    pext[QB + TQ:QB + TQ + 8, :] = jnp.where(n < n_last, pn_ref[0], 0.0)
    pext[QB + TQ + 8:, :] = jnp.zeros((QB - 8, POOL_WIDTH), F32)

    row = lax.broadcasted_iota(jnp.int32, (QB, 3 * QB), 0)
    col = lax.broadcasted_iota(jnp.int32, (QB, 3 * QB), 1)
    in_band = (col >= row) & (col <= row + 2 * WINDOW)
    tok = lax.broadcasted_iota(jnp.int32, (QB, 1), 0)
    kc = kc_ref[0]
    vc = vc_ref[0]

    def sub_block(j, carry):
        r0 = pl.multiple_of(j * QB, QB)
        qj = q_ref[0, pl.ds(r0, QB), :]
        kw = kwin[pl.ds(r0, 3 * QB), :]
        vw = vwin[pl.ds(r0, 3 * QB), :]
        kpos = col + (n * TQ + j * QB - QB)
        ok = in_band & (kpos >= 0) & (kpos < seq_len)
        bias = jnp.where(ok, 0.0, -jnp.inf)
        n_loc = 3 * QB

        def kv_lanes(head):
            var = 2 * (head // (N_HEADS // N_KV_HEADS)) + head % 2
            return slice(var * LANES, (var + 1) * LANES)

        for head in range(N_HEADS):
            qc = qj[:, (head // 2) * LANES:(head // 2 + 1) * LANES]
            s_scr[head, :, 0:n_loc] = _dot_nt(qc, kw[:, kv_lanes(head)]) + bias
            s_scr[head, :, n_loc:] = _dot_nt(qc, kc[:, kv_lanes(head)])
        n_tiles = (n_loc + kc.shape[0]) // LANES
        for head in range(N_HEADS):
            row_max = jnp.max(_fold_lane_tiles(jnp.maximum, s_scr[head]), axis=1, keepdims=True)
            m_scr[head] = jnp.broadcast_to(jnp.maximum(row_max, sink_ref[head] * LOG2E), (QB, LANES))
        for head in range(N_HEADS):
            m = m_scr[head]
            acc = None
            for i in range(n_tiles):
                p = jnp.exp2(s_scr[head, :, i * LANES:(i + 1) * LANES] - m)
                p_scr[head, :, i * LANES:(i + 1) * LANES] = p.astype(BF16)
                acc = p if acc is None else acc + p
            denom = (jnp.broadcast_to(jnp.sum(acc, axis=1, keepdims=True), (QB, LANES))
                     + jnp.exp2(sink_ref[head] * LOG2E - m))
            m_scr[head] = 1.0 / denom
        for c in range(N_HEADS // 2):
            pair = None
            for head in (2 * c, 2 * c + 1):
                o = (_dot(p_scr[head, :, 0:n_loc], vw[:, kv_lanes(head)])
                     + _dot(p_scr[head, :, n_loc:], vc[:, kv_lanes(head)])) * m_scr[head]
                pair = o if pair is None else pair + o
            mix[pl.ds(r0, QB), c * LANES:(c + 1) * LANES] = pair.astype(BF16)

        slab = pext[pl.ds(pl.multiple_of(r0 + POOL_OFF, 8), POOL_SLAB), :]
        tpos = tok + (n * TQ + j * QB)
        for g, w in enumerate(POOL_WINDOWS):
            sg = slab[:, g * LANES:(g + 1) * LANES]
            hi, lo = _split_bf16(sg)
            band = band_ref[g]
            wsum = _dot(band, hi) + _dot(band, lo)
            cnt = (jnp.minimum(tpos - w // 2 + w, seq_len) - jnp.maximum(tpos - w // 2, 0)).astype(F32)
            dlt = wsum / cnt - sg[POOL_OFF:POOL_OFF + QB, :]
            y = _dot(dlt.astype(BF16), poolw_ref[g]) * pscale_ref[:, g * LANES:(g + 1) * LANES]
            mix[pl.ds(r0, QB), ATTN_WIDTH + g * LANES:ATTN_WIDTH + (g + 1) * LANES] = y.astype(BF16)
        return carry

    lax.fori_loop(0, TQ // QB, sub_block, 0)

    proj = _dot(mix[...], wout_ref[...])
    x1 = x_ref[0] + mod_ref[0, 2:3, :] * proj
    x1_ref[0] = x1
    h2 = _norm_mod(x1, g2_ref[...], mod_ref[0, 3:4, :], mod_ref[0, 4:5, :])
    h2_ref[...] = _pack_words(h2)
    h_hi, h_lo = _split_bf16(h2)
    wrh = wrh_ref[...]
    lg_ref[...] = _dot_nt(wrh, h_hi) + _dot_nt(wrh, h_lo) + _dot_nt(wrl_ref[...], h_hi)


def _attn(sink, q, k4, v4, kc4, vc4, p, x, mod, band, poolw_bf, pscale, wout_bf, g2, wr_hi, wr_lo):
    b, s, d = x.shape
    c = kc4.shape[1]
    nt = s // TQ
    hb = TQ // QB
    pb = TQ // 8
    kv_main = pl.BlockSpec((1, TQ, 4 * LANES), lambda bi, n: (bi, n, 0))
    kv_prev = pl.BlockSpec((1, QB, 4 * LANES), lambda bi, n: (bi, jnp.maximum(n * hb - 1, 0), 0))
    kv_next = pl.BlockSpec((1, QB, 4 * LANES), lambda bi, n: (bi, jnp.minimum((n + 1) * hb, s // QB - 1), 0))
    const2 = lambda bi, n: (0, 0)
    const3 = lambda bi, n: (0, 0, 0)
    return pl.pallas_call(
        functools.partial(_attn_kernel, s),
        out_shape=(jax.ShapeDtypeStruct((b, s, d), F32),
                   jax.ShapeDtypeStruct((b * s, d // 2), U32),
                   jax.ShapeDtypeStruct((N_EXPERTS, b * s), F32)),
        grid=(b, nt),
        in_specs=[pl.BlockSpec(memory_space=pltpu.SMEM),
                  pl.BlockSpec((1, TQ, ATTN_WIDTH), lambda bi, n: (bi, n, 0)),
                  kv_main, kv_prev, kv_next, kv_main, kv_prev, kv_next,
                  pl.BlockSpec((1, c, 4 * LANES), lambda bi, n: (bi, 0, 0)),
                  pl.BlockSpec((1, c, 4 * LANES), lambda bi, n: (bi, 0, 0)),
                  pl.BlockSpec((1, TQ, POOL_WIDTH), lambda bi, n: (bi, n, 0)),
                  pl.BlockSpec((1, 8, POOL_WIDTH), lambda bi, n: (bi, jnp.maximum(n * pb - 1, 0), 0)),
                  pl.BlockSpec((1, 8, POOL_WIDTH), lambda bi, n: (bi, jnp.minimum((n + 1) * pb, s // 8 - 1), 0)),
                  pl.BlockSpec((1, TQ, d), lambda bi, n: (bi, n, 0)),
                  pl.BlockSpec((1, 6, d), lambda bi, n: (bi, 0, 0)),
                  pl.BlockSpec((len(POOL_WINDOWS), QB, POOL_SLAB), const3),
                  pl.BlockSpec((len(POOL_WINDOWS), POOL_GROUP_DIM, POOL_GROUP_DIM), const3),
                  pl.BlockSpec((1, POOL_WIDTH), const2),
                  pl.BlockSpec((d, d), const2),
                  pl.BlockSpec((1, d), const2),
                  pl.BlockSpec((N_EXPERTS, d), const2),
                  pl.BlockSpec((N_EXPERTS, d), const2)],
        out_specs=(pl.BlockSpec((1, TQ, d), lambda bi, n: (bi, n, 0)),
                   pl.BlockSpec((TQ, d // 2), lambda bi, n: (bi * nt + n, 0)),
                   pl.BlockSpec((N_EXPERTS, TQ), lambda bi, n: (0, bi * nt + n))),
        scratch_shapes=[pltpu.VMEM((TQ + 2 * QB, 4 * LANES), BF16),
                        pltpu.VMEM((TQ + 2 * QB, 4 * LANES), BF16),
                        pltpu.VMEM((TQ + 2 * QB, POOL_WIDTH), F32),
                        pltpu.VMEM((TQ, d), BF16),
                        pltpu.VMEM((N_HEADS, QB, 3 * QB + c), F32),
                        pltpu.VMEM((N_HEADS, QB, 3 * QB + c), BF16),
                        pltpu.VMEM((N_HEADS, QB, LANES), F32)],
        compiler_params=pltpu.CompilerParams(vmem_limit_bytes=VMEM_LIMIT),
        name="attn",
    )(sink, q, k4, k4, k4, v4, v4, v4, kc4, vc4, p, p, p, x, mod, band, poolw_bf, pscale, wout_bf,
      g2, wr_hi, wr_lo)


def _first_argmax_rows(v, row_iota, n_rows):
    m = jnp.max(v, axis=0, keepdims=True)
    idx = jnp.min(jnp.where(v == m, row_iota, n_rows), axis=0, keepdims=True)
    return m, idx


def _route_kernel(lg_ref, bias_ref, tri_ref, idx_ref, gate_ref, rank_ref, cnt_ref, carry):
    i = pl.program_id(0)

    @pl.when(i == 0)
    def _():
        carry[...] = jnp.zeros_like(carry)

    scores = 1.0 / (1.0 + jnp.exp(-lg_ref[...]))
    biased = scores + bias_ref[...]
    e_iota = lax.broadcasted_iota(jnp.int32, scores.shape, 0).astype(F32)
    g_iota = lax.broadcasted_iota(jnp.int32, (EXPERTS_PER_GROUP, TR), 0).astype(F32)
    neg = -jnp.inf

    grp = []
    for g in range(N_EXPERT_GROUPS):
        blk = biased[g * EXPERTS_PER_GROUP:(g + 1) * EXPERTS_PER_GROUP, :]
        m1, i1 = _first_argmax_rows(blk, g_iota, float(EXPERTS_PER_GROUP))
        m2 = jnp.max(jnp.where(g_iota == i1, neg, blk), axis=0, keepdims=True)
        grp.append(m1 + m2)
    grp = jnp.concatenate(grp, axis=0)
    gg_iota = lax.broadcasted_iota(jnp.int32, grp.shape, 0).astype(F32)
    grp_sel = jnp.zeros(grp.shape, F32)
    for _ in range(TOPK_GROUPS):
        _, gi = _first_argmax_rows(grp, gg_iota, float(N_EXPERT_GROUPS))
        hit = gg_iota == gi
        grp_sel = jnp.where(hit, 1.0, grp_sel)
        grp = jnp.where(hit, neg, grp)
    allowed = jnp.concatenate(
        [jnp.broadcast_to(grp_sel[g:g + 1, :], (EXPERTS_PER_GROUP, TR)) for g in range(N_EXPERT_GROUPS)], axis=0)
    masked = jnp.where(allowed > 0.5, biased, neg)

    idxs, gates = [], []
    onehot = jnp.zeros(scores.shape, F32)
    for _ in range(TOP_K):
        _, ei = _first_argmax_rows(masked, e_iota, float(N_EXPERTS))
        hit = e_iota == ei
        idxs.append(ei)
        gates.append(jnp.sum(jnp.where(hit, scores, 0.0), axis=0, keepdims=True))
        onehot = jnp.where(hit, 1.0, onehot)
        masked = jnp.where(hit, neg, masked)
    idx = jnp.concatenate(idxs, axis=0)
    gate = jnp.concatenate(gates, axis=0)
    gate = gate / jnp.sum(gate, axis=0, keepdims=True) * ROUTED_SCALE

    before = _dot(onehot.astype(BF16), tri_ref[...]) + carry[:, 0:1]
    ranks = [jnp.sum(jnp.where(e_iota == idxs[k], before, 0.0), axis=0, keepdims=True) for k in range(TOP_K)]
    idx_ref[...] = idx.astype(jnp.int32)
    gate_ref[...] = gate
    rank_ref[...] = jnp.concatenate(ranks, axis=0).astype(jnp.int32)
    total = carry[...] + jnp.sum(onehot, axis=1, keepdims=True)
    carry[...] = total
    cnt_ref[...] = total


def _route(lg_t, bias, tri):
    e, t = lg_t.shape
    tok = pl.BlockSpec((TOP_K, TR), lambda i: (0, i))
    return pl.pallas_call(
        _route_kernel,
        out_shape=(jax.ShapeDtypeStruct((TOP_K, t), jnp.int32),
                   jax.ShapeDtypeStruct((TOP_K, t), F32),
                   jax.ShapeDtypeStruct((TOP_K, t), jnp.int32),
                   jax.ShapeDtypeStruct((e, LANES), F32)),
        grid=(t // TR,),
        in_specs=[pl.BlockSpec((e, TR), lambda i: (0, i)),
                  pl.BlockSpec((e, 1), lambda i: (0, 0)),
                  pl.BlockSpec((TR, TR), lambda i: (0, 0))],
        out_specs=(tok, tok, tok, pl.BlockSpec((e, LANES), lambda i: (0, 0))),
        scratch_shapes=[pltpu.VMEM((e, LANES), F32)],
        compiler_params=pltpu.CompilerParams(vmem_limit_bytes=VMEM_LIMIT),
        name="route",
    )(lg_t, bias, tri)


def _tile_rows(ref, row):
    return ref.at[pl.ds(pl.multiple_of(row * ROW_TILE, ROW_TILE), ROW_TILE), :]


def _dispatch_kernel(fill_lo, fill_hi, dest_hbm, h_ref, xs_hbm, idx_smem, zero_rows, sem_idx, sem_rows):
    i = pl.program_id(0)

    @pl.when(i == 0)
    def _():
        zero_rows[...] = jnp.zeros_like(zero_rows)

        def fill_copy(first_row, n):
            dst = xs_hbm.at[pl.ds(pl.multiple_of(first_row * ROW_TILE, ROW_TILE), n * ROW_TILE), :]
            return pltpu.make_async_copy(zero_rows.at[0:n * ROW_TILE, :], dst, sem_rows)

        def fill_expert(e, carry):
            n_pad = fill_hi[e] - fill_lo[e]
            for wait in (False, True):
                row = fill_lo[e]
                bit = TM_EXP // 2
                while bit:
                    take = (n_pad & bit) != 0

                    @pl.when(take)
                    def _(row=row, bit=bit, wait=wait):
                        cp = fill_copy(row, bit)
                        cp.wait() if wait else cp.start()

                    row = row + jnp.where(take, bit, 0)
                    bit //= 2
            return carry

        lax.fori_loop(0, N_EXPERTS, fill_expert, 0)

        def start_block(blk, c):
            fill_copy(blk * TM_EXP, TM_EXP).start()
            return c

        def wait_block(blk, c):
            fill_copy(blk * TM_EXP, TM_EXP).wait()
            return c

        first_tail = fill_lo[N_EXPERTS] // TM_EXP
        n_blocks = xs_hbm.shape[0] // (TM_EXP * ROW_TILE)
        lax.fori_loop(first_tail, n_blocks, start_block, 0)
        lax.fori_loop(first_tail, n_blocks, wait_block, 0)

    n_idx = TD * TOP_K
    cp = pltpu.make_async_copy(dest_hbm.at[pl.ds(i * n_idx, n_idx)], idx_smem, sem_idx)
    cp.start()
    cp.wait()

    def issue(t, carry):
        src = _tile_rows(h_ref, t)
        for k in range(TOP_K):
            pltpu.make_async_copy(src, _tile_rows(xs_hbm, idx_smem[t * TOP_K + k]), sem_rows).start(priority=k % 2)
        return carry

    lax.fori_loop(0, TD, issue, 0, unroll=2)
    for _ in range(TOP_K):
        pltpu.make_async_copy(h_ref, xs_hbm.at[pl.ds(0, TD * ROW_TILE), :], sem_rows).wait()


def _dispatch(fill_lo, fill_hi, dest_flat, h2t, n_rows):
    t = h2t.shape[0] // ROW_TILE
    grid_spec = pltpu.PrefetchScalarGridSpec(
        num_scalar_prefetch=2,
        grid=(t // TD,),
        in_specs=[pl.BlockSpec(memory_space=pl.ANY),
                  pl.BlockSpec((TD * ROW_TILE, LANES), lambda i, lo, hi: (i, 0))],
        out_specs=pl.BlockSpec(memory_space=pl.ANY),
        scratch_shapes=[pltpu.SMEM((TD * TOP_K,), jnp.int32),
                        pltpu.VMEM((TM_EXP * ROW_TILE, LANES), U32),
                        pltpu.SemaphoreType.DMA, pltpu.SemaphoreType.DMA],
    )
    return pl.pallas_call(
        _dispatch_kernel,
        out_shape=jax.ShapeDtypeStruct((n_rows * ROW_TILE, LANES), U32),
        grid_spec=grid_spec,
        compiler_params=pltpu.CompilerParams(has_side_effects=True, disable_bounds_checks=True),
        name="dispatch",
    )(fill_lo, fill_hi, dest_flat, h2t)


def _sc_dispatch(dest_kt, h2v, n_rows):
    t, width = h2v.shape
    steps = t // (SC_WORKERS * SC_TOKENS)
    mesh = plsc.VectorSubcoreMesh(core_axis_name="c", subcore_axis_name="s")

    @functools.partial(
        pl.kernel, mesh=mesh,
        out_type=jax.ShapeDtypeStruct((n_rows, width), U32),
        scratch_types=[pltpu.VMEM((TOP_K, SC_TOKENS), jnp.int32),
                       pltpu.VMEM((SC_TOKENS, width), U32),
                       pltpu.SemaphoreType.DMA],
        name="sc_dispatch",
    )
    def body(dest_hbm, h_hbm, xs_hbm, idx_v, rows_v, sem):
        wid = lax.axis_index("s") * SC_CORES + lax.axis_index("c")

        @pl.loop(0, steps)
        def _(j):
            base = (wid * steps + j) * SC_TOKENS
            pltpu.sync_copy(dest_hbm.at[:, pl.ds(base, SC_TOKENS)], idx_v)
            pltpu.sync_copy(h_hbm.at[pl.ds(base, SC_TOKENS)], rows_v)
            copies = [pltpu.async_copy(rows_v, xs_hbm.at[idx_v.at[k]], sem) for k in range(TOP_K)]
            for cp in copies:
                cp.wait()

    return body(dest_kt, h2v)


def _experts_kernel(be_ref, valid_ref, nact_ref, xs_ref, wg_ref, wu_ref, wd_ref, ys_ref, wg_bf, wu_bf, wd_bf):
    i = pl.program_id(0)
    prev = be_ref[jnp.maximum(i - 1, 0)]

    @pl.when((i == 0) | (be_ref[i] != prev))
    def _():
        wg_bf[...] = wg_ref[0].astype(BF16)
        wu_bf[...] = wu_ref[0].astype(BF16)
        wd_bf[...] = wd_ref[0].astype(BF16)

    @pl.when(i < nact_ref[0])
    def _():
        row = lax.broadcasted_iota(jnp.int32, xs_ref.shape, 0)
        words = jnp.where(row < valid_ref[i], xs_ref[...], jnp.uint32(0))
        xb = _unpack_words(words).astype(BF16)
        hid = _silu(_dot(xb, wg_bf[...])) * _dot(xb, wu_bf[...])
        _store_row_tiles(ys_ref, _dot(hid.astype(BF16), wd_bf[...]))

    @pl.when(i >= nact_ref[0])
    def _():
        ys_ref[...] = jnp.zeros_like(ys_ref)


def _experts(block_e, block_valid, n_active, xs, w_gate, w_up, w_down):
    n_rows, half = xs.shape
    d = w_gate.shape[1]
    nb = n_rows // TM_EXP
    grid_spec = pltpu.PrefetchScalarGridSpec(
        num_scalar_prefetch=3,
        grid=(nb,),
        in_specs=[pl.BlockSpec((TM_EXP, half), lambda i, be, bv, na: (jnp.minimum(i, na[0] - 1), 0)),
                  pl.BlockSpec((1, d, D_EXPERT), lambda i, be, bv, na: (be[i], 0, 0)),
                  pl.BlockSpec((1, d, D_EXPERT), lambda i, be, bv, na: (be[i], 0, 0)),
                  pl.BlockSpec((1, D_EXPERT, d), lambda i, be, bv, na: (be[i], 0, 0))],
        out_specs=pl.BlockSpec((TM_EXP * ROW_TILE, LANES), lambda i, be, bv, na: (i, 0)),
        scratch_shapes=[pltpu.VMEM((d, D_EXPERT), BF16), pltpu.VMEM((d, D_EXPERT), BF16),
                        pltpu.VMEM((D_EXPERT, d), BF16)],
    )
    return pl.pallas_call(
        _experts_kernel,
        out_shape=jax.ShapeDtypeStruct((n_rows * ROW_TILE, LANES), U32),
        grid_spec=grid_spec,
        compiler_params=pltpu.CompilerParams(vmem_limit_bytes=VMEM_LIMIT),
        name="experts",
    )(block_e, block_valid, n_active, xs, w_gate, w_up, w_down)


def _combine_kernel(dest_hbm, ys_hbm, x1_ref, h2_ref, gate_ref, mod_ref, wsg_ref, wsu_ref, wsd_ref, fg_ref,
                    out_ref, idx_smem, buf, sem_idx, sem_rows):
    i = pl.program_id(0)
    n_idx = TF * TOP_K
    cp = pltpu.make_async_copy(dest_hbm.at[pl.ds(i * n_idx, n_idx)], idx_smem, sem_idx)
    cp.start()
    cp.wait()

    def issue(t, carry):
        for k in range(TOP_K):
            pltpu.make_async_copy(_tile_rows(ys_hbm, idx_smem[t * TOP_K + k]), _tile_rows(buf, k * TF + t),
                                  sem_rows).start(priority=k % 2)
        return carry

    lax.fori_loop(0, TF, issue, 0, unroll=2)

    hb = _unpack_words(h2_ref[...]).astype(BF16)
    hid = _silu(_dot(hb, wsg_ref[...])) * _dot(hb, wsu_ref[...])
    ffn = _dot(hid.astype(BF16), wsd_ref[...])

    pltpu.make_async_copy(ys_hbm.at[pl.ds(0, TOP_K * TF * ROW_TILE), :], buf, sem_rows).wait()
    gate = gate_ref[...]
    for k in range(TOP_K):
        ffn = ffn + gate[:, k:k + 1] * _load_row_tiles(buf, TF, base=k * TF * ROW_TILE)
    x2 = x1_ref[...] + mod_ref[0, 5:6, :] * ffn
    ms = jnp.mean(x2 * x2, axis=-1, keepdims=True)
    out_ref[...] = x2 * lax.rsqrt(ms + EPS) * fg_ref[...]


def _combine(dest_flat, ys, x1, h2t, gate_tk, mod, wsg_bf, wsu_bf, wsd_bf, final_g, seq_len):
    t, d = x1.shape
    tiles_per_seq = seq_len // TF
    tok = pl.BlockSpec((TF, d), lambda i: (i, 0))
    return pl.pallas_call(
        _combine_kernel,
        out_shape=jax.ShapeDtypeStruct((t, d), F32),
        grid=(t // TF,),
        in_specs=[pl.BlockSpec(memory_space=pl.ANY), pl.BlockSpec(memory_space=pl.ANY),
                  tok, pl.BlockSpec((TF, d // 2), lambda i: (i, 0)),
                  pl.BlockSpec((TF, TOP_K), lambda i: (i, 0)),
                  pl.BlockSpec((1, 6, d), lambda i: (i // tiles_per_seq, 0, 0)),
                  pl.BlockSpec((d, D_SHARED), lambda i: (0, 0)),
                  pl.BlockSpec((d, D_SHARED), lambda i: (0, 0)),
                  pl.BlockSpec((D_SHARED, d), lambda i: (0, 0)),
                  pl.BlockSpec((1, d), lambda i: (0, 0))],
        out_specs=tok,
        scratch_shapes=[pltpu.SMEM((TF * TOP_K,), jnp.int32),
                        pltpu.VMEM((TOP_K * TF * ROW_TILE, LANES), U32),
                        pltpu.SemaphoreType.DMA, pltpu.SemaphoreType.DMA],
        compiler_params=pltpu.CompilerParams(vmem_limit_bytes=VMEM_LIMIT, disable_bounds_checks=True),
        name="combine",
    )(dest_flat, ys, x1, h2t, gate_tk, mod, wsg_bf, wsu_bf, wsd_bf, final_g)


def _rope_tables(n_tokens):
    n_rows = n_tokens // GRID_W
    n_freq = HEAD_DIM // 4
    inv_freq = ROPE_THETA ** (-jnp.arange(n_freq, dtype=F32) / n_freq)
    ang_r = jnp.arange(n_rows).astype(F32)[:, None] * inv_freq[None, :]
    ang_c = jnp.arange(GRID_W).astype(F32)[:, None] * inv_freq[None, :]

    def per_token(row_part, col_part):
        rows = jnp.broadcast_to(row_part[:, None, :], (n_rows, GRID_W, n_freq))
        cols = jnp.broadcast_to(col_part[None, :, :], (n_rows, GRID_W, n_freq))
        return rows.reshape(n_tokens, n_freq), cols.reshape(n_tokens, n_freq)

    cos_r, cos_c = per_token(jnp.cos(ang_r), jnp.cos(ang_c))
    sin_r, sin_c = per_token(jnp.sin(ang_r), jnp.sin(ang_c))
    cos = jnp.concatenate([cos_r, cos_r, cos_c, cos_c], axis=1)
    sin = jnp.concatenate([-sin_r, sin_r, -sin_c, sin_c], axis=1)
    reps = LANES // HEAD_DIM
    return jnp.tile(cos, (1, reps)), jnp.tile(sin, (1, reps))


def _pool_bands():
    i = jnp.arange(QB)[:, None]
    r = jnp.arange(POOL_SLAB)[None, :]
    return jnp.stack([((r >= i + POOL_OFF - w // 2) & (r < i + POOL_OFF + w // 2)).astype(BF16)
                      for w in POOL_WINDOWS])


def kernel(x, c, ctx, c_ctx, w_ada, b_ada, norm1_g, norm2_g, w_in, attn_sink, pool_w, pool_scale, w_out,
           w_router, router_bias, w_gate, w_up, w_down, ws_gate, ws_up, ws_down, final_g):
    b, s, d = x.shape
    t = b * s
    assert w_ada.shape[0] == 1 and d == D_MODEL and s % TQ == 0 and b + 1 <= 8

    c8 = jnp.zeros((8, d), F32).at[:b].set(c).at[b].set(c_ctx)
    mod = _ada(c8, w_ada[0], b_ada[0]).reshape(8, 6, d)
    g1 = norm1_g[0].reshape(1, d)
    g2 = norm2_g[0].reshape(1, d)
    w_in_bf = w_in[0].astype(BF16)
    cos_t, sin_t = _rope_tables(s)

    q, k4, v4, p = _inproj(x, mod, g1, w_in_bf, cos_t, sin_t)
    kc4, vc4 = _ctxproj(ctx, mod[b:b + 1], g1, w_in_bf[:, ATTN_WIDTH:ATTN_WIDTH + 2 * KV_WIDTH])

    wr_t = w_router[0].T
    wr_hi = wr_t.astype(BF16)
    wr_lo = (wr_t - wr_hi.astype(F32)).astype(BF16)
    x1, h2t, lg_t = _attn(attn_sink[0], q, k4, v4, kc4, vc4, p, x, mod, _pool_bands(),
                         pool_w[0].astype(BF16), pool_scale[0].reshape(1, POOL_WIDTH), w_out[0].astype(BF16),
                         g2, wr_hi, wr_lo)

    tri = jnp.triu(jnp.ones((TR, TR), BF16), k=1)
    idx_kt, gate_kt, rank_kt, counts = _route(lg_t, router_bias[0].reshape(N_EXPERTS, 1), tri)

    sizes = counts[:, 0].astype(jnp.int32)
    padded = (sizes + TM_EXP - 1) // TM_EXP * TM_EXP
    pad_end = jnp.cumsum(padded)
    pad_start = pad_end - padded
    n_rows = -(-(t * TOP_K + N_EXPERTS * (TM_EXP - 1)) // TM_EXP) * TM_EXP
    n_blocks = n_rows // TM_EXP
    experts = jnp.arange(N_EXPERTS, dtype=jnp.int32)
    start_of = jnp.sum(jnp.where(idx_kt[None] == experts[:, None, None], pad_start[:, None, None], 0), axis=0)
    dest_flat = (start_of + rank_kt).T.reshape(t * TOP_K)
    block_row = jnp.arange(n_blocks, dtype=jnp.int32) * TM_EXP
    block_e = jnp.minimum(jnp.sum((pad_end[None, :] <= block_row[:, None]).astype(jnp.int32), axis=1),
                          N_EXPERTS - 1)
    n_active = (pad_end[-1:] // TM_EXP).astype(jnp.int32)

    routed_end = (pad_start + sizes)[block_e]
    block_valid = jnp.clip(routed_end - block_row, 0, TM_EXP).astype(jnp.int32)

    xs = _sc_dispatch(start_of + rank_kt, h2t, n_rows)
    ys = _experts(block_e, block_valid, n_active, xs, w_gate[0], w_up[0], w_down[0])
    out = _combine(dest_flat, ys, x1.reshape(t, d), h2t, gate_kt.T, mod,
                   ws_gate[0].astype(BF16), ws_up[0].astype(BF16), ws_down[0].astype(BF16),
                   final_g.reshape(1, d), s)
    return out.reshape(b, s, d)
```

```python
import functools

import jax
import jax.numpy as jnp
from jax import lax
from jax.experimental import pallas as pl
from jax.experimental.pallas import tpu as pltpu
from jax.experimental.pallas import tpu_sc as plsc

F32 = jnp.float32
BF16 = jnp.bfloat16

D_MODEL = 1024
GRID_W = 64
N_HEADS = 8
N_KV_HEADS = 2
HEAD_DIM = 64
ATTN_WIDTH = N_HEADS * HEAD_DIM
KV_WIDTH = N_KV_HEADS * HEAD_DIM
WINDOW = 128
ROPE_THETA = 10000.0
POOL_WINDOWS = (2, 4, 8, 16)
POOL_WIDTH = D_MODEL - ATTN_WIDTH
POOL_GROUP_DIM = POOL_WIDTH // len(POOL_WINDOWS)
IN_COLS = ATTN_WIDTH + 2 * KV_WIDTH + POOL_WIDTH
N_EXPERTS = 64
TOP_K = 8
N_EXPERT_GROUPS = 8
EXPERTS_PER_GROUP = N_EXPERTS // N_EXPERT_GROUPS
TOPK_GROUPS = 4
D_EXPERT = 256
D_SHARED = 256
ROUTED_SCALE = 2.5
EPS = 1e-6
LOG2E = 1.4426950408889634

LANES = 128
U32 = jnp.uint32
VMEM_LIMIT = 48 * 1024 * 1024

TM_PROJ = 512
TQ = 512
QB = 128
POOL_SLAB = 256
POOL_OFF = 64
TR = 512
TM_EXP = 512
TF = 256
SC_CORES = 2
SC_WORKERS = 32
SC_TOKENS = 128


def _silu(x):
    return x * (1.0 / (1.0 + jnp.exp(-x)))


def _split_bf16(x):
    hi = x.astype(BF16)
    lo = (x - hi.astype(F32)).astype(BF16)
    return hi, lo


def _dot(a, b):
    return jnp.dot(a, b, preferred_element_type=F32)


def _pack_words(val):
    half = val.shape[1] // 2
    lo = lax.bitcast_convert_type(val[:, :half].astype(BF16).astype(F32), U32)
    hi = lax.bitcast_convert_type(val[:, half:].astype(BF16).astype(F32), U32)
    return lax.shift_right_logical(lo, jnp.uint32(16)) | hi


def _unpack_words(words):
    lo = lax.bitcast_convert_type(lax.shift_left(words, jnp.uint32(16)), F32)
    hi = lax.bitcast_convert_type(words & jnp.uint32(0xFFFF0000), F32)
    return jnp.concatenate([lo, hi], axis=1)


def _dot_nt(a, b):
    return lax.dot_general(a, b, (((1,), (1,)), ((), ())), preferred_element_type=F32)


def _ada_kernel(c_ref, w_ref, b_ref, o_ref):
    a_hi, a_lo = _split_bf16(_silu(c_ref[...]))
    w_hi, w_lo = _split_bf16(w_ref[...])
    o_ref[...] = _dot(a_hi, w_hi) + _dot(a_lo, w_hi) + _dot(a_hi, w_lo) + b_ref[...]


def _ada(c8, w_ada, b_ada):
    d = c8.shape[1]
    n = w_ada.shape[1]
    tn = 512
    return pl.pallas_call(
        _ada_kernel,
        out_shape=jax.ShapeDtypeStruct((8, n), F32),
        grid=(n // tn,),
        in_specs=[pl.BlockSpec((8, d), lambda j: (0, 0)),
                  pl.BlockSpec((d, tn), lambda j: (0, j)),
                  pl.BlockSpec((1, tn), lambda j: (0, j))],
        out_specs=pl.BlockSpec((8, tn), lambda j: (0, j)),
        compiler_params=pltpu.CompilerParams(vmem_limit_bytes=VMEM_LIMIT),
        name="ada",
    )(c8, w_ada, b_ada.reshape(1, n))


def _norm_mod(x, g, shift, scale):
    ms = jnp.mean(x * x, axis=-1, keepdims=True)
    return (x * lax.rsqrt(ms + EPS) * g) * (1.0 + scale) + shift


def _lane_variants(t):
    lane = lax.broadcasted_iota(jnp.int32, t.shape, 1)
    lo = lane < HEAD_DIM
    tr = pltpu.roll(t, HEAD_DIM, 1)
    zero = jnp.zeros_like(t)
    return (jnp.where(lo, t, zero), jnp.where(lo, zero, tr),
            jnp.where(lo, tr, zero), jnp.where(lo, zero, t))


def _store_variants(ref, t):
    for i, var in enumerate(_lane_variants(t)):
        ref[0, :, i * LANES:(i + 1) * LANES] = var.astype(BF16)


def _inproj_kernel(x_ref, mod_ref, g_ref, w_ref, cos_ref, sin_ref, q_ref, k_ref, v_ref, p_ref):
    h = _norm_mod(x_ref[0], g_ref[...], mod_ref[0, 0:1, :], mod_ref[0, 1:2, :])
    z = _dot(h.astype(BF16), w_ref[...])
    cos = cos_ref[...]
    sin = sin_ref[...]
    lane = lax.broadcasted_iota(jnp.int32, cos.shape, 1)
    first_half = (lane & 16) == 0

    def rope(zc):
        partner = jnp.where(first_half, pltpu.roll(zc, LANES - 16, 1), pltpu.roll(zc, 16, 1))
        return zc * cos + partner * sin

    scale = HEAD_DIM ** -0.5 * LOG2E
    for c in range(ATTN_WIDTH // LANES):
        q_ref[0, :, c * LANES:(c + 1) * LANES] = (rope(z[:, c * LANES:(c + 1) * LANES]) * scale).astype(BF16)
    _store_variants(k_ref, rope(z[:, ATTN_WIDTH:ATTN_WIDTH + KV_WIDTH]))
    _store_variants(v_ref, z[:, ATTN_WIDTH + KV_WIDTH:ATTN_WIDTH + 2 * KV_WIDTH])
    p_ref[0] = z[:, ATTN_WIDTH + 2 * KV_WIDTH:]


def _inproj(x, mod, g1, w_in_bf, cos_t, sin_t):
    b, s, d = x.shape
    tm = TM_PROJ
    return pl.pallas_call(
        _inproj_kernel,
        out_shape=(jax.ShapeDtypeStruct((b, s, ATTN_WIDTH), BF16),
                   jax.ShapeDtypeStruct((b, s, 4 * LANES), BF16),
                   jax.ShapeDtypeStruct((b, s, 4 * LANES), BF16),
                   jax.ShapeDtypeStruct((b, s, POOL_WIDTH), F32)),
        grid=(s // tm, b),
        in_specs=[pl.BlockSpec((1, tm, d), lambda n, bi: (bi, n, 0)),
                  pl.BlockSpec((1, 6, d), lambda n, bi: (bi, 0, 0)),
                  pl.BlockSpec((1, d), lambda n, bi: (0, 0)),
                  pl.BlockSpec((d, IN_COLS), lambda n, bi: (0, 0)),
                  pl.BlockSpec((tm, LANES), lambda n, bi: (n, 0)),
                  pl.BlockSpec((tm, LANES), lambda n, bi: (n, 0))],
        out_specs=(pl.BlockSpec((1, tm, ATTN_WIDTH), lambda n, bi: (bi, n, 0)),
                   pl.BlockSpec((1, tm, 4 * LANES), lambda n, bi: (bi, n, 0)),
                   pl.BlockSpec((1, tm, 4 * LANES), lambda n, bi: (bi, n, 0)),
                   pl.BlockSpec((1, tm, POOL_WIDTH), lambda n, bi: (bi, n, 0))),
        compiler_params=pltpu.CompilerParams(vmem_limit_bytes=VMEM_LIMIT),
        name="inproj",
    )(x, mod, g1, w_in_bf, cos_t, sin_t)


def _ctxproj_kernel(x_ref, mod_ref, g_ref, w_ref, k_ref, v_ref):
    h = _norm_mod(x_ref[0], g_ref[...], mod_ref[0, 0:1, :], mod_ref[0, 1:2, :])
    z = _dot(h.astype(BF16), w_ref[...])
    _store_variants(k_ref, z[:, :KV_WIDTH])
    _store_variants(v_ref, z[:, KV_WIDTH:])


def _ctxproj(ctx, mod_c, g1, w_kv_bf):
    b, c, d = ctx.shape
    return pl.pallas_call(
        _ctxproj_kernel,
        out_shape=(jax.ShapeDtypeStruct((b, c, 4 * LANES), BF16),
                   jax.ShapeDtypeStruct((b, c, 4 * LANES), BF16)),
        grid=(b,),
        in_specs=[pl.BlockSpec((1, c, d), lambda bi: (bi, 0, 0)),
                  pl.BlockSpec((1, 6, d), lambda bi: (0, 0, 0)),
                  pl.BlockSpec((1, d), lambda bi: (0, 0)),
                  pl.BlockSpec((d, 2 * KV_WIDTH), lambda bi: (0, 0))],
        out_specs=(pl.BlockSpec((1, c, 4 * LANES), lambda bi: (bi, 0, 0)),
                   pl.BlockSpec((1, c, 4 * LANES), lambda bi: (bi, 0, 0))),
        compiler_params=pltpu.CompilerParams(vmem_limit_bytes=VMEM_LIMIT),
        name="ctxproj",
    )(ctx, mod_c, g1, w_kv_bf)


def _fold_lane_tiles(op, *arrays):
    tiles = [a[:, i * LANES:(i + 1) * LANES] for a in arrays for i in range(a.shape[1] // LANES)]
    while len(tiles) > 1:
        tiles = [op(tiles[i], tiles[i + 1]) if i + 1 < len(tiles) else tiles[i] for i in range(0, len(tiles), 2)]
    return tiles[0]


def _attn_kernel(seq_len, sink_ref, q_ref, k_ref, kp_ref, kn_ref, v_ref, vp_ref, vn_ref, kc_ref, vc_ref,
                 p_ref, pp_ref, pn_ref, x_ref, mod_ref, band_ref, poolw_ref, pscale_ref, wout_ref,
                 g2_ref, wrh_ref, wrl_ref, x1_ref, h2_ref, lg_ref, kwin, vwin, pext, mix, s_scr, p_scr, m_scr):
    n = pl.program_id(1)
    n_last = pl.num_programs(1) - 1

    kwin[0:QB, :] = kp_ref[0]
    kwin[QB:QB + TQ, :] = k_ref[0]
    kwin[QB + TQ:, :] = kn_ref[0]
    vwin[0:QB, :] = vp_ref[0]
    vwin[QB:QB + TQ, :] = v_ref[0]
    vwin[QB + TQ:, :] = vn_ref[0]

    pext[0:QB - 8, :] = jnp.zeros((QB - 8, POOL_WIDTH), F32)
    pext[QB - 8:QB, :] = jnp.where(n > 0, pp_ref[0], 0.0)
    pext[QB:QB + TQ, :] = p_ref[0]
    pext[QB + TQ:QB + TQ + 8, :] = jnp.where(n < n_last, pn_ref[0], 0.0)
    pext[QB + TQ + 8:, :] = jnp.zeros((QB - 8, POOL_WIDTH), F32)

    row = lax.broadcasted_iota(jnp.int32, (QB, 3 * QB), 0)
    col = lax.broadcasted_iota(jnp.int32, (QB, 3 * QB), 1)
    in_band = (col >= row) & (col <= row + 2 * WINDOW)
    tok = lax.broadcasted_iota(jnp.int32, (QB, 1), 0)
    kc = kc_ref[0]
    vc = vc_ref[0]

    def sub_block(j, carry):
        r0 = pl.multiple_of(j * QB, QB)
        qj = q_ref[0, pl.ds(r0, QB), :]
        kw = kwin[pl.ds(r0, 3 * QB), :]
        vw = vwin[pl.ds(r0, 3 * QB), :]
        kpos = col + (n * TQ + j * QB - QB)
        ok = in_band & (kpos >= 0) & (kpos < seq_len)
        bias = jnp.where(ok, 0.0, -jnp.inf)
        n_loc = 3 * QB

        def kv_lanes(head):
            var = 2 * (head // (N_HEADS // N_KV_HEADS)) + head % 2
            return slice(var * LANES, (var + 1) * LANES)

        for head in range(N_HEADS):
            qc = qj[:, (head // 2) * LANES:(head // 2 + 1) * LANES]
            s_scr[head, :, 0:n_loc] = _dot_nt(qc, kw[:, kv_lanes(head)]) + bias
            s_scr[head, :, n_loc:] = _dot_nt(qc, kc[:, kv_lanes(head)])
        n_tiles = (n_loc + kc.shape[0]) // LANES
        for head in range(N_HEADS):
            row_max = jnp.max(_fold_lane_tiles(jnp.maximum, s_scr[head]), axis=1, keepdims=True)
            m_scr[head] = jnp.broadcast_to(jnp.maximum(row_max, sink_ref[head] * LOG2E), (QB, LANES))
        for head in range(N_HEADS):
            m = m_scr[head]
            acc = None
            for i in range(n_tiles):
                p = jnp.exp2(s_scr[head, :, i * LANES:(i + 1) * LANES] - m)
                p_scr[head, :, i * LANES:(i + 1) * LANES] = p.astype(BF16)
                acc = p if acc is None else acc + p
            denom = (jnp.broadcast_to(jnp.sum(acc, axis=1, keepdims=True), (QB, LANES))
                     + jnp.exp2(sink_ref[head] * LOG2E - m))
            m_scr[head] = 1.0 / denom
        for c in range(N_HEADS // 2):
            pair = None
            for head in (2 * c, 2 * c + 1):
                o = (_dot(p_scr[head, :, 0:n_loc], vw[:, kv_lanes(head)])
                     + _dot(p_scr[head, :, n_loc:], vc[:, kv_lanes(head)])) * m_scr[head]
                pair = o if pair is None else pair + o
            mix[pl.ds(r0, QB), c * LANES:(c + 1) * LANES] = pair.astype(BF16)

        slab = pext[pl.ds(pl.multiple_of(r0 + POOL_OFF, 8), POOL_SLAB), :]
        tpos = tok + (n * TQ + j * QB)
        for g, w in enumerate(POOL_WINDOWS):
            sg = slab[:, g * LANES:(g + 1) * LANES]
            hi, lo = _split_bf16(sg)
            band = band_ref[g]
            wsum = _dot(band, hi) + _dot(band, lo)
            cnt = (jnp.minimum(tpos - w // 2 + w, seq_len) - jnp.maximum(tpos - w // 2, 0)).astype(F32)
            dlt = wsum / cnt - sg[POOL_OFF:POOL_OFF + QB, :]
            y = _dot(dlt.astype(BF16), poolw_ref[g]) * pscale_ref[:, g * LANES:(g + 1) * LANES]
            mix[pl.ds(r0, QB), ATTN_WIDTH + g * LANES:ATTN_WIDTH + (g + 1) * LANES] = y.astype(BF16)
        return carry

    lax.fori_loop(0, TQ // QB, sub_block, 0)

    proj = _dot(mix[...], wout_ref[...])
    x1 = x_ref[0] + mod_ref[0, 2:3, :] * proj
    x1_ref[0] = x1
    h2 = _norm_mod(x1, g2_ref[...], mod_ref[0, 3:4, :], mod_ref[0, 4:5, :])
    h2_ref[...] = _pack_words(h2)
    h_hi, h_lo = _split_bf16(h2)
    wrh = wrh_ref[...]
    lg_ref[...] = _dot_nt(wrh, h_hi) + _dot_nt(wrh, h_lo) + _dot_nt(wrl_ref[...], h_hi)


def _attn(sink, q, k4, v4, kc4, vc4, p, x, mod, band, poolw_bf, pscale, wout_bf, g2, wr_hi, wr_lo):
    b, s, d = x.shape
    c = kc4.shape[1]
    nt = s // TQ
    hb = TQ // QB
    pb = TQ // 8
    kv_main = pl.BlockSpec((1, TQ, 4 * LANES), lambda bi, n: (bi, n, 0))
    kv_prev = pl.BlockSpec((1, QB, 4 * LANES), lambda bi, n: (bi, jnp.maximum(n * hb - 1, 0), 0))
    kv_next = pl.BlockSpec((1, QB, 4 * LANES), lambda bi, n: (bi, jnp.minimum((n + 1) * hb, s // QB - 1), 0))
    const2 = lambda bi, n: (0, 0)
    const3 = lambda bi, n: (0, 0, 0)
    return pl.pallas_call(
        functools.partial(_attn_kernel, s),
        out_shape=(jax.ShapeDtypeStruct((b, s, d), F32),
                   jax.ShapeDtypeStruct((b * s, d // 2), U32),
                   jax.ShapeDtypeStruct((N_EXPERTS, b * s), F32)),
        grid=(b, nt),
        in_specs=[pl.BlockSpec(memory_space=pltpu.SMEM),
                  pl.BlockSpec((1, TQ, ATTN_WIDTH), lambda bi, n: (bi, n, 0)),
                  kv_main, kv_prev, kv_next, kv_main, kv_prev, kv_next,
                  pl.BlockSpec((1, c, 4 * LANES), lambda bi, n: (bi, 0, 0)),
                  pl.BlockSpec((1, c, 4 * LANES), lambda bi, n: (bi, 0, 0)),
                  pl.BlockSpec((1, TQ, POOL_WIDTH), lambda bi, n: (bi, n, 0)),
                  pl.BlockSpec((1, 8, POOL_WIDTH), lambda bi, n: (bi, jnp.maximum(n * pb - 1, 0), 0)),
                  pl.BlockSpec((1, 8, POOL_WIDTH), lambda bi, n: (bi, jnp.minimum((n + 1) * pb, s // 8 - 1), 0)),
                  pl.BlockSpec((1, TQ, d), lambda bi, n: (bi, n, 0)),
                  pl.BlockSpec((1, 6, d), lambda bi, n: (bi, 0, 0)),
                  pl.BlockSpec((len(POOL_WINDOWS), QB, POOL_SLAB), const3),
                  pl.BlockSpec((len(POOL_WINDOWS), POOL_GROUP_DIM, POOL_GROUP_DIM), const3),
                  pl.BlockSpec((1, POOL_WIDTH), const2),
                  pl.BlockSpec((d, d), const2),
                  pl.BlockSpec((1, d), const2),
                  pl.BlockSpec((N_EXPERTS, d), const2),
                  pl.BlockSpec((N_EXPERTS, d), const2)],
        out_specs=(pl.BlockSpec((1, TQ, d), lambda bi, n: (bi, n, 0)),
                   pl.BlockSpec((TQ, d // 2), lambda bi, n: (bi * nt + n, 0)),
                   pl.BlockSpec((N_EXPERTS, TQ), lambda bi, n: (0, bi * nt + n))),
        scratch_shapes=[pltpu.VMEM((TQ + 2 * QB, 4 * LANES), BF16),
                        pltpu.VMEM((TQ + 2 * QB, 4 * LANES), BF16),
                        pltpu.VMEM((TQ + 2 * QB, POOL_WIDTH), F32),
                        pltpu.VMEM((TQ, d), BF16),
                        pltpu.VMEM((N_HEADS, QB, 3 * QB + c), F32),
                        pltpu.VMEM((N_HEADS, QB, 3 * QB + c), BF16),
                        pltpu.VMEM((N_HEADS, QB, LANES), F32)],
        compiler_params=pltpu.CompilerParams(vmem_limit_bytes=VMEM_LIMIT),
        name="attn",
    )(sink, q, k4, k4, k4, v4, v4, v4, kc4, vc4, p, p, p, x, mod, band, poolw_bf, pscale, wout_bf,
      g2, wr_hi, wr_lo)


def _first_argmax_rows(v, row_iota, n_rows):
    m = jnp.max(v, axis=0, keepdims=True)
    idx = jnp.min(jnp.where(v == m, row_iota, n_rows), axis=0, keepdims=True)
    return m, idx


def _route_kernel(lg_ref, bias_ref, tri_ref, idx_ref, gate_ref, rank_ref, cnt_ref, carry):
    i = pl.program_id(0)

    @pl.when(i == 0)
    def _():
        carry[...] = jnp.zeros_like(carry)

    scores = 1.0 / (1.0 + jnp.exp(-lg_ref[...]))
    biased = scores + bias_ref[...]
    e_iota = lax.broadcasted_iota(jnp.int32, scores.shape, 0).astype(F32)
    g_iota = lax.broadcasted_iota(jnp.int32, (EXPERTS_PER_GROUP, TR), 0).astype(F32)
    neg = -jnp.inf

    grp = []
    for g in range(N_EXPERT_GROUPS):
        blk = biased[g * EXPERTS_PER_GROUP:(g + 1) * EXPERTS_PER_GROUP, :]
        m1, i1 = _first_argmax_rows(blk, g_iota, float(EXPERTS_PER_GROUP))
        m2 = jnp.max(jnp.where(g_iota == i1, neg, blk), axis=0, keepdims=True)
        grp.append(m1 + m2)
    grp = jnp.concatenate(grp, axis=0)
    gg_iota = lax.broadcasted_iota(jnp.int32, grp.shape, 0).astype(F32)
    grp_sel = jnp.zeros(grp.shape, F32)
    for _ in range(TOPK_GROUPS):
        _, gi = _first_argmax_rows(grp, gg_iota, float(N_EXPERT_GROUPS))
        hit = gg_iota == gi
        grp_sel = jnp.where(hit, 1.0, grp_sel)
        grp = jnp.where(hit, neg, grp)
    allowed = jnp.concatenate(
        [jnp.broadcast_to(grp_sel[g:g + 1, :], (EXPERTS_PER_GROUP, TR)) for g in range(N_EXPERT_GROUPS)], axis=0)
    masked = jnp.where(allowed > 0.5, biased, neg)

    idxs, gates = [], []
    onehot = jnp.zeros(scores.shape, F32)
    for _ in range(TOP_K):
        _, ei = _first_argmax_rows(masked, e_iota, float(N_EXPERTS))
        hit = e_iota == ei
        idxs.append(ei)
        gates.append(jnp.sum(jnp.where(hit, scores, 0.0), axis=0, keepdims=True))
        onehot = jnp.where(hit, 1.0, onehot)
        masked = jnp.where(hit, neg, masked)
    idx = jnp.concatenate(idxs, axis=0)
    gate = jnp.concatenate(gates, axis=0)
    gate = gate / jnp.sum(gate, axis=0, keepdims=True) * ROUTED_SCALE

    before = _dot(onehot.astype(BF16), tri_ref[...]) + carry[:, 0:1]
    ranks = [jnp.sum(jnp.where(e_iota == idxs[k], before, 0.0), axis=0, keepdims=True) for k in range(TOP_K)]
    idx_ref[...] = idx.astype(jnp.int32)
    gate_ref[...] = gate
    rank_ref[...] = jnp.concatenate(ranks, axis=0).astype(jnp.int32)
    total = carry[...] + jnp.sum(onehot, axis=1, keepdims=True)
    carry[...] = total
    cnt_ref[...] = total


def _route(lg_t, bias, tri):
    e, t = lg_t.shape
    tok = pl.BlockSpec((TOP_K, TR), lambda i: (0, i))
    return pl.pallas_call(
        _route_kernel,
        out_shape=(jax.ShapeDtypeStruct((TOP_K, t), jnp.int32),
                   jax.ShapeDtypeStruct((TOP_K, t), F32),
                   jax.ShapeDtypeStruct((TOP_K, t), jnp.int32),
                   jax.ShapeDtypeStruct((e, LANES), F32)),
        grid=(t // TR,),
        in_specs=[pl.BlockSpec((e, TR), lambda i: (0, i)),
                  pl.BlockSpec((e, 1), lambda i: (0, 0)),
                  pl.BlockSpec((TR, TR), lambda i: (0, 0))],
        out_specs=(tok, tok, tok, pl.BlockSpec((e, LANES), lambda i: (0, 0))),
        scratch_shapes=[pltpu.VMEM((e, LANES), F32)],
        compiler_params=pltpu.CompilerParams(vmem_limit_bytes=VMEM_LIMIT),
        name="route",
    )(lg_t, bias, tri)


def _sc_mesh():
    return plsc.VectorSubcoreMesh(core_axis_name="c", subcore_axis_name="s")


def _sc_token_base(steps, j):
    worker = lax.axis_index("s") * SC_CORES + lax.axis_index("c")
    return (worker * steps + j) * SC_TOKENS


def _sc_scatter(dest_kt, h2p, n_rows):
    t, width = h2p.shape
    steps = t // (SC_WORKERS * SC_TOKENS)

    @functools.partial(
        pl.kernel, mesh=_sc_mesh(),
        out_type=jax.ShapeDtypeStruct((n_rows, width), U32),
        scratch_types=[pltpu.VMEM((TOP_K, SC_TOKENS), jnp.int32),
                       pltpu.VMEM((SC_TOKENS, width), U32),
                       pltpu.SemaphoreType.DMA],
        name="sc_scatter",
    )
    def body(dest_hbm, h_hbm, xs_hbm, idx_v, rows_v, sem):
        @pl.loop(0, steps)
        def _(j):
            base = _sc_token_base(steps, j)
            pltpu.sync_copy(dest_hbm.at[:, pl.ds(base, SC_TOKENS)], idx_v)
            pltpu.sync_copy(h_hbm.at[pl.ds(base, SC_TOKENS)], rows_v)
            copies = [pltpu.async_copy(rows_v, xs_hbm.at[idx_v.at[k]], sem) for k in range(TOP_K)]
            for cp in copies:
                cp.wait()

    return body(dest_kt, h2p)


def _sc_gather(dest_kt, ys):
    n_k, t = dest_kt.shape
    width = ys.shape[1]
    steps = t // (SC_WORKERS * SC_TOKENS)

    @functools.partial(
        pl.kernel, mesh=_sc_mesh(),
        out_type=jax.ShapeDtypeStruct((n_k, t, width), U32),
        scratch_types=[pltpu.VMEM((n_k, SC_TOKENS), jnp.int32),
                       pltpu.VMEM((SC_TOKENS, width), U32),
                       pltpu.SemaphoreType.DMA],
        name="sc_gather",
    )
    def body(dest_hbm, ys_hbm, yk_hbm, idx_v, rows_v, sem):
        @pl.loop(0, steps)
        def _(j):
            base = _sc_token_base(steps, j)
            pltpu.sync_copy(dest_hbm.at[:, pl.ds(base, SC_TOKENS)], idx_v)
            for k in range(n_k):
                pltpu.async_copy(ys_hbm.at[idx_v.at[k]], rows_v, sem).wait()
                pltpu.sync_copy(rows_v, yk_hbm.at[k, pl.ds(base, SC_TOKENS)])

    return body(dest_kt, ys)


def _experts_kernel(be_ref, valid_ref, nact_ref, xs_ref, wg_ref, wu_ref, wd_ref, ys_ref, wg_bf, wu_bf, wd_bf):
    i = pl.program_id(0)
    prev = be_ref[jnp.maximum(i - 1, 0)]

    @pl.when((i == 0) | (be_ref[i] != prev))
    def _():
        wg_bf[...] = wg_ref[0].astype(BF16)
        wu_bf[...] = wu_ref[0].astype(BF16)
        wd_bf[...] = wd_ref[0].astype(BF16)

    @pl.when(i < nact_ref[0])
    def _():
        row = lax.broadcasted_iota(jnp.int32, xs_ref.shape, 0)
        words = jnp.where(row < valid_ref[i], xs_ref[...], jnp.uint32(0))
        xb = _unpack_words(words).astype(BF16)
        hid = _silu(_dot(xb, wg_bf[...])) * _dot(xb, wu_bf[...])
        ys_ref[...] = _pack_words(_dot(hid.astype(BF16), wd_bf[...]))

    @pl.when(i >= nact_ref[0])
    def _():
        ys_ref[...] = jnp.zeros_like(ys_ref)


def _experts(block_e, block_valid, n_active, xs, w_gate, w_up, w_down):
    n_rows, half = xs.shape
    d = w_gate.shape[1]
    nb = n_rows // TM_EXP
    grid_spec = pltpu.PrefetchScalarGridSpec(
        num_scalar_prefetch=3,
        grid=(nb,),
        in_specs=[pl.BlockSpec((TM_EXP, half), lambda i, be, bv, na: (jnp.minimum(i, na[0] - 1), 0)),
                  pl.BlockSpec((1, d, D_EXPERT), lambda i, be, bv, na: (be[i], 0, 0)),
                  pl.BlockSpec((1, d, D_EXPERT), lambda i, be, bv, na: (be[i], 0, 0)),
                  pl.BlockSpec((1, D_EXPERT, d), lambda i, be, bv, na: (be[i], 0, 0))],
        out_specs=pl.BlockSpec((TM_EXP, half), lambda i, be, bv, na: (i, 0)),
        scratch_shapes=[pltpu.VMEM((d, D_EXPERT), BF16), pltpu.VMEM((d, D_EXPERT), BF16),
                        pltpu.VMEM((D_EXPERT, d), BF16)],
    )
    return pl.pallas_call(
        _experts_kernel,
        out_shape=jax.ShapeDtypeStruct((n_rows, half), U32),
        grid_spec=grid_spec,
        compiler_params=pltpu.CompilerParams(vmem_limit_bytes=VMEM_LIMIT),
        name="experts",
    )(block_e, block_valid, n_active, xs, w_gate, w_up, w_down)


def _combine_kernel(yk_ref, x1_ref, h2_ref, gate_ref, mod_ref, wsg_ref, wsu_ref, wsd_ref, fg_ref, out_ref):
    hb = _unpack_words(h2_ref[...]).astype(BF16)
    hid = _silu(_dot(hb, wsg_ref[...])) * _dot(hb, wsu_ref[...])
    ffn = _dot(hid.astype(BF16), wsd_ref[...])
    gate = gate_ref[...]
    for k in range(TOP_K):
        ffn = ffn + gate[:, k:k + 1] * _unpack_words(yk_ref[k])
    x2 = x1_ref[...] + mod_ref[0, 5:6, :] * ffn
    ms = jnp.mean(x2 * x2, axis=-1, keepdims=True)
    out_ref[...] = x2 * lax.rsqrt(ms + EPS) * fg_ref[...]


def _combine(yk, x1, h2p, gate_tk, mod, wsg_bf, wsu_bf, wsd_bf, final_g, seq_len):
    t, d = x1.shape
    tiles_per_seq = seq_len // TF
    tok = pl.BlockSpec((TF, d), lambda i: (i, 0))
    return pl.pallas_call(
        _combine_kernel,
        out_shape=jax.ShapeDtypeStruct((t, d), F32),
        grid=(t // TF,),
        in_specs=[pl.BlockSpec((TOP_K, TF, d // 2), lambda i: (0, i, 0)),
                  tok, pl.BlockSpec((TF, d // 2), lambda i: (i, 0)),
                  pl.BlockSpec((TF, TOP_K), lambda i: (i, 0)),
                  pl.BlockSpec((1, 6, d), lambda i: (i // tiles_per_seq, 0, 0)),
                  pl.BlockSpec((d, D_SHARED), lambda i: (0, 0)),
                  pl.BlockSpec((d, D_SHARED), lambda i: (0, 0)),
                  pl.BlockSpec((D_SHARED, d), lambda i: (0, 0)),
                  pl.BlockSpec((1, d), lambda i: (0, 0))],
        out_specs=tok,
        compiler_params=pltpu.CompilerParams(vmem_limit_bytes=VMEM_LIMIT),
        name="combine",
    )(yk, x1, h2p, gate_tk, mod, wsg_bf, wsu_bf, wsd_bf, final_g)


def _rope_tables(n_tokens):
    n_rows = n_tokens // GRID_W
    n_freq = HEAD_DIM // 4
    inv_freq = ROPE_THETA ** (-jnp.arange(n_freq, dtype=F32) / n_freq)
    ang_r = jnp.arange(n_rows).astype(F32)[:, None] * inv_freq[None, :]
    ang_c = jnp.arange(GRID_W).astype(F32)[:, None] * inv_freq[None, :]

    def per_token(row_part, col_part):
        rows = jnp.broadcast_to(row_part[:, None, :], (n_rows, GRID_W, n_freq))
        cols = jnp.broadcast_to(col_part[None, :, :], (n_rows, GRID_W, n_freq))
        return rows.reshape(n_tokens, n_freq), cols.reshape(n_tokens, n_freq)

    cos_r, cos_c = per_token(jnp.cos(ang_r), jnp.cos(ang_c))
    sin_r, sin_c = per_token(jnp.sin(ang_r), jnp.sin(ang_c))
    cos = jnp.concatenate([cos_r, cos_r, cos_c, cos_c], axis=1)
    sin = jnp.concatenate([-sin_r, sin_r, -sin_c, sin_c], axis=1)
    reps = LANES // HEAD_DIM
    return jnp.tile(cos, (1, reps)), jnp.tile(sin, (1, reps))


def _pool_bands():
    i = jnp.arange(QB)[:, None]
    r = jnp.arange(POOL_SLAB)[None, :]
    return jnp.stack([((r >= i + POOL_OFF - w // 2) & (r < i + POOL_OFF + w // 2)).astype(BF16)
                      for w in POOL_WINDOWS])


def kernel(x, c, ctx, c_ctx, w_ada, b_ada, norm1_g, norm2_g, w_in, attn_sink, pool_w, pool_scale, w_out,
           w_router, router_bias, w_gate, w_up, w_down, ws_gate, ws_up, ws_down, final_g):
    b, s, d = x.shape
    t = b * s
    assert w_ada.shape[0] == 1 and d == D_MODEL and s % TQ == 0 and b + 1 <= 8

    c8 = jnp.zeros((8, d), F32).at[:b].set(c).at[b].set(c_ctx)
    mod = _ada(c8, w_ada[0], b_ada[0]).reshape(8, 6, d)
    g1 = norm1_g[0].reshape(1, d)
    g2 = norm2_g[0].reshape(1, d)
    w_in_bf = w_in[0].astype(BF16)
    cos_t, sin_t = _rope_tables(s)

    q, k4, v4, p = _inproj(x, mod, g1, w_in_bf, cos_t, sin_t)
    kc4, vc4 = _ctxproj(ctx, mod[b:b + 1], g1, w_in_bf[:, ATTN_WIDTH:ATTN_WIDTH + 2 * KV_WIDTH])

    wr_t = w_router[0].T
    wr_hi = wr_t.astype(BF16)
    wr_lo = (wr_t - wr_hi.astype(F32)).astype(BF16)
    x1, h2p, lg_t = _attn(attn_sink[0], q, k4, v4, kc4, vc4, p, x, mod, _pool_bands(),
                         pool_w[0].astype(BF16), pool_scale[0].reshape(1, POOL_WIDTH), w_out[0].astype(BF16),
                         g2, wr_hi, wr_lo)

    tri = jnp.triu(jnp.ones((TR, TR), BF16), k=1)
    idx_kt, gate_kt, rank_kt, counts = _route(lg_t, router_bias[0].reshape(N_EXPERTS, 1), tri)

    sizes = counts[:, 0].astype(jnp.int32)
    padded = (sizes + TM_EXP - 1) // TM_EXP * TM_EXP
    pad_end = jnp.cumsum(padded)
    pad_start = pad_end - padded
    n_rows = -(-(t * TOP_K + N_EXPERTS * (TM_EXP - 1)) // TM_EXP) * TM_EXP
    n_blocks = n_rows // TM_EXP
    experts = jnp.arange(N_EXPERTS, dtype=jnp.int32)
    start_of = jnp.sum(jnp.where(idx_kt[None] == experts[:, None, None], pad_start[:, None, None], 0), axis=0)
    dest_kt = start_of + rank_kt
    block_row = jnp.arange(n_blocks, dtype=jnp.int32) * TM_EXP
    block_e = jnp.minimum(jnp.sum((pad_end[None, :] <= block_row[:, None]).astype(jnp.int32), axis=1),
                          N_EXPERTS - 1)
    n_active = (pad_end[-1:] // TM_EXP).astype(jnp.int32)

    routed_end = (pad_start + sizes)[block_e]
    block_valid = jnp.clip(routed_end - block_row, 0, TM_EXP).astype(jnp.int32)

    xs = _sc_scatter(dest_kt, h2p, n_rows)
    ys = _experts(block_e, block_valid, n_active, xs, w_gate[0], w_up[0], w_down[0])
    out = _combine(_sc_gather(dest_kt, ys), x1.reshape(t, d), h2p, gate_kt.T, mod,
                   ws_gate[0].astype(BF16), ws_up[0].astype(BF16), ws_down[0].astype(BF16),
                   final_g.reshape(1, d), s)
    return out.reshape(b, s, d)
```

```python
import functools

import jax
import jax.numpy as jnp
from jax import lax
from jax.experimental import pallas as pl
from jax.experimental.pallas import tpu as pltpu
from jax.experimental.pallas import tpu_sc as plsc

F32 = jnp.float32
BF16 = jnp.bfloat16

D_MODEL = 1024
GRID_W = 64
N_HEADS = 8
N_KV_HEADS = 2
HEAD_DIM = 64
ATTN_WIDTH = N_HEADS * HEAD_DIM
KV_WIDTH = N_KV_HEADS * HEAD_DIM
WINDOW = 128
ROPE_THETA = 10000.0
POOL_WINDOWS = (2, 4, 8, 16)
POOL_WIDTH = D_MODEL - ATTN_WIDTH
POOL_GROUP_DIM = POOL_WIDTH // len(POOL_WINDOWS)
IN_COLS = ATTN_WIDTH + 2 * KV_WIDTH + POOL_WIDTH
N_EXPERTS = 64
TOP_K = 8
N_EXPERT_GROUPS = 8
EXPERTS_PER_GROUP = N_EXPERTS // N_EXPERT_GROUPS
TOPK_GROUPS = 4
D_EXPERT = 256
D_SHARED = 256
ROUTED_SCALE = 2.5
EPS = 1e-6
LOG2E = 1.4426950408889634

LANES = 128
U32 = jnp.uint32
VMEM_LIMIT = 48 * 1024 * 1024

TM_PROJ = 512
TQ = 512
QB = 128
POOL_SLAB = 256
POOL_OFF = 64
TR = 512
TM_EXP = 512
TF = 256
COMBINE_CHUNKS = 4
SC_CORES = 2
SC_WORKERS = 32
SC_TOKENS = 128


def _silu(x):
    return x * (1.0 / (1.0 + jnp.exp(-x)))


def _split_bf16(x):
    hi = x.astype(BF16)
    lo = (x - hi.astype(F32)).astype(BF16)
    return hi, lo


def _dot(a, b):
    return jnp.dot(a, b, preferred_element_type=F32)


def _pack_words(val):
    half = val.shape[1] // 2
    lo = lax.bitcast_convert_type(val[:, :half].astype(BF16).astype(F32), U32)
    hi = lax.bitcast_convert_type(val[:, half:].astype(BF16).astype(F32), U32)
    return lax.shift_right_logical(lo, jnp.uint32(16)) | hi


def _unpack_words(words):
    lo = lax.bitcast_convert_type(lax.shift_left(words, jnp.uint32(16)), F32)
    hi = lax.bitcast_convert_type(words & jnp.uint32(0xFFFF0000), F32)
    return jnp.concatenate([lo, hi], axis=1)


def _dot_nt(a, b):
    return lax.dot_general(a, b, (((1,), (1,)), ((), ())), preferred_element_type=F32)


def _ada_kernel(c_ref, w_ref, b_ref, o_ref):
    a_hi, a_lo = _split_bf16(_silu(c_ref[...]))
    w_hi, w_lo = _split_bf16(w_ref[...])
    o_ref[...] = _dot(a_hi, w_hi) + _dot(a_lo, w_hi) + _dot(a_hi, w_lo) + b_ref[...]


def _ada(c8, w_ada, b_ada):
    d = c8.shape[1]
    n = w_ada.shape[1]
    tn = 512
    return pl.pallas_call(
        _ada_kernel,
        out_shape=jax.ShapeDtypeStruct((8, n), F32),
        grid=(n // tn,),
        in_specs=[pl.BlockSpec((8, d), lambda j: (0, 0)),
                  pl.BlockSpec((d, tn), lambda j: (0, j)),
                  pl.BlockSpec((1, tn), lambda j: (0, j))],
        out_specs=pl.BlockSpec((8, tn), lambda j: (0, j)),
        compiler_params=pltpu.CompilerParams(vmem_limit_bytes=VMEM_LIMIT),
        name="ada",
    )(c8, w_ada, b_ada.reshape(1, n))


def _norm_mod(x, g, shift, scale):
    ms = jnp.mean(x * x, axis=-1, keepdims=True)
    return (x * lax.rsqrt(ms + EPS) * g) * (1.0 + scale) + shift


def _lane_variants(t):
    lane = lax.broadcasted_iota(jnp.int32, t.shape, 1)
    lo = lane < HEAD_DIM
    tr = pltpu.roll(t, HEAD_DIM, 1)
    zero = jnp.zeros_like(t)
    return (jnp.where(lo, t, zero), jnp.where(lo, zero, tr),
            jnp.where(lo, tr, zero), jnp.where(lo, zero, t))


def _store_variants(ref, t):
    for i, var in enumerate(_lane_variants(t)):
        ref[0, :, i * LANES:(i + 1) * LANES] = var.astype(BF16)


def _inproj_kernel(x_ref, mod_ref, g_ref, w_ref, cos_ref, sin_ref, q_ref, k_ref, v_ref, p_ref):
    h = _norm_mod(x_ref[0], g_ref[...], mod_ref[0, 0:1, :], mod_ref[0, 1:2, :])
    z = _dot(h.astype(BF16), w_ref[...])
    cos = cos_ref[...]
    sin = sin_ref[...]
    lane = lax.broadcasted_iota(jnp.int32, cos.shape, 1)
    first_half = (lane & 16) == 0

    def rope(zc):
        partner = jnp.where(first_half, pltpu.roll(zc, LANES - 16, 1), pltpu.roll(zc, 16, 1))
        return zc * cos + partner * sin

    scale = HEAD_DIM ** -0.5 * LOG2E
    for c in range(ATTN_WIDTH // LANES):
        q_ref[0, :, c * LANES:(c + 1) * LANES] = (rope(z[:, c * LANES:(c + 1) * LANES]) * scale).astype(BF16)
    _store_variants(k_ref, rope(z[:, ATTN_WIDTH:ATTN_WIDTH + KV_WIDTH]))
    _store_variants(v_ref, z[:, ATTN_WIDTH + KV_WIDTH:ATTN_WIDTH + 2 * KV_WIDTH])
    p_ref[0] = z[:, ATTN_WIDTH + 2 * KV_WIDTH:]


def _inproj(x, mod, g1, w_in_bf, cos_t, sin_t):
    b, s, d = x.shape
    tm = TM_PROJ
    return pl.pallas_call(
        _inproj_kernel,
        out_shape=(jax.ShapeDtypeStruct((b, s, ATTN_WIDTH), BF16),
                   jax.ShapeDtypeStruct((b, s, 4 * LANES), BF16),
                   jax.ShapeDtypeStruct((b, s, 4 * LANES), BF16),
                   jax.ShapeDtypeStruct((b, s, POOL_WIDTH), F32)),
        grid=(s // tm, b),
        in_specs=[pl.BlockSpec((1, tm, d), lambda n, bi: (bi, n, 0)),
                  pl.BlockSpec((1, 6, d), lambda n, bi: (bi, 0, 0)),
                  pl.BlockSpec((1, d), lambda n, bi: (0, 0)),
                  pl.BlockSpec((d, IN_COLS), lambda n, bi: (0, 0)),
                  pl.BlockSpec((tm, LANES), lambda n, bi: (n, 0)),
                  pl.BlockSpec((tm, LANES), lambda n, bi: (n, 0))],
        out_specs=(pl.BlockSpec((1, tm, ATTN_WIDTH), lambda n, bi: (bi, n, 0)),
                   pl.BlockSpec((1, tm, 4 * LANES), lambda n, bi: (bi, n, 0)),
                   pl.BlockSpec((1, tm, 4 * LANES), lambda n, bi: (bi, n, 0)),
                   pl.BlockSpec((1, tm, POOL_WIDTH), lambda n, bi: (bi, n, 0))),
        compiler_params=pltpu.CompilerParams(vmem_limit_bytes=VMEM_LIMIT),
        name="inproj",
    )(x, mod, g1, w_in_bf, cos_t, sin_t)


def _ctxproj_kernel(x_ref, mod_ref, g_ref, w_ref, k_ref, v_ref):
    h = _norm_mod(x_ref[0], g_ref[...], mod_ref[0, 0:1, :], mod_ref[0, 1:2, :])
    z = _dot(h.astype(BF16), w_ref[...])
    _store_variants(k_ref, z[:, :KV_WIDTH])
    _store_variants(v_ref, z[:, KV_WIDTH:])


def _ctxproj(ctx, mod_c, g1, w_kv_bf):
    b, c, d = ctx.shape
    return pl.pallas_call(
        _ctxproj_kernel,
        out_shape=(jax.ShapeDtypeStruct((b, c, 4 * LANES), BF16),
                   jax.ShapeDtypeStruct((b, c, 4 * LANES), BF16)),
        grid=(b,),
        in_specs=[pl.BlockSpec((1, c, d), lambda bi: (bi, 0, 0)),
                  pl.BlockSpec((1, 6, d), lambda bi: (0, 0, 0)),
                  pl.BlockSpec((1, d), lambda bi: (0, 0)),
                  pl.BlockSpec((d, 2 * KV_WIDTH), lambda bi: (0, 0))],
        out_specs=(pl.BlockSpec((1, c, 4 * LANES), lambda bi: (bi, 0, 0)),
                   pl.BlockSpec((1, c, 4 * LANES), lambda bi: (bi, 0, 0))),
        compiler_params=pltpu.CompilerParams(vmem_limit_bytes=VMEM_LIMIT),
        name="ctxproj",
    )(ctx, mod_c, g1, w_kv_bf)


def _fold_lane_tiles(op, *arrays):
    tiles = [a[:, i * LANES:(i + 1) * LANES] for a in arrays for i in range(a.shape[1] // LANES)]
    while len(tiles) > 1:
        tiles = [op(tiles[i], tiles[i + 1]) if i + 1 < len(tiles) else tiles[i] for i in range(0, len(tiles), 2)]
    return tiles[0]


def _attn_kernel(seq_len, sink_ref, q_ref, k_ref, kp_ref, kn_ref, v_ref, vp_ref, vn_ref, kc_ref, vc_ref,
                 p_ref, pp_ref, pn_ref, x_ref, mod_ref, band_ref, poolw_ref, pscale_ref, wout_ref,
                 g2_ref, wrh_ref, wrl_ref, x1_ref, h2_ref, lg_ref, kwin, vwin, pext, mix, s_scr, p_scr, m_scr):
    n = pl.program_id(1)
    n_last = pl.num_programs(1) - 1

    kwin[0:QB, :] = kp_ref[0]
    kwin[QB:QB + TQ, :] = k_ref[0]
    kwin[QB + TQ:, :] = kn_ref[0]
    vwin[0:QB, :] = vp_ref[0]
    vwin[QB:QB + TQ, :] = v_ref[0]
    vwin[QB + TQ:, :] = vn_ref[0]

    pext[0:QB - 8, :] = jnp.zeros((QB - 8, POOL_WIDTH), F32)
    pext[QB - 8:QB, :] = jnp.where(n > 0, pp_ref[0], 0.0)
    pext[QB:QB + TQ, :] = p_ref[0]
    pext[QB + TQ:QB + TQ + 8, :] = jnp.where(n < n_last, pn_ref[0], 0.0)
    pext[QB + TQ + 8:, :] = jnp.zeros((QB - 8, POOL_WIDTH), F32)

    row = lax.broadcasted_iota(jnp.int32, (QB, 3 * QB), 0)
    col = lax.broadcasted_iota(jnp.int32, (QB, 3 * QB), 1)
    in_band = (col >= row) & (col <= row + 2 * WINDOW)
    tok = lax.broadcasted_iota(jnp.int32, (QB, 1), 0)
    kc = kc_ref[0]
    vc = vc_ref[0]

    def sub_block(j, carry):
        r0 = pl.multiple_of(j * QB, QB)
        qj = q_ref[0, pl.ds(r0, QB), :]
        kw = kwin[pl.ds(r0, 3 * QB), :]
        vw = vwin[pl.ds(r0, 3 * QB), :]
        kpos = col + (n * TQ + j * QB - QB)
        ok = in_band & (kpos >= 0) & (kpos < seq_len)
        bias = jnp.where(ok, 0.0, -jnp.inf)
        n_loc = 3 * QB

        def kv_lanes(head):
            var = 2 * (head // (N_HEADS // N_KV_HEADS)) + head % 2
            return slice(var * LANES, (var + 1) * LANES)

        for head in range(N_HEADS):
            qc = qj[:, (head // 2) * LANES:(head // 2 + 1) * LANES]
            s_scr[head, :, 0:n_loc] = _dot_nt(qc, kw[:, kv_lanes(head)]) + bias
            s_scr[head, :, n_loc:] = _dot_nt(qc, kc[:, kv_lanes(head)])
        n_tiles = (n_loc + kc.shape[0]) // LANES
        for head in range(N_HEADS):
            row_max = jnp.max(_fold_lane_tiles(jnp.maximum, s_scr[head]), axis=1, keepdims=True)
            m_scr[head] = jnp.broadcast_to(jnp.maximum(row_max, sink_ref[head] * LOG2E), (QB, LANES))
        for head in range(N_HEADS):
            m = m_scr[head]
            acc = None
            for i in range(n_tiles):
                p = jnp.exp2(s_scr[head, :, i * LANES:(i + 1) * LANES] - m)
                p_scr[head, :, i * LANES:(i + 1) * LANES] = p.astype(BF16)
                acc = p if acc is None else acc + p
            denom = (jnp.broadcast_to(jnp.sum(acc, axis=1, keepdims=True), (QB, LANES))
                     + jnp.exp2(sink_ref[head] * LOG2E - m))
            m_scr[head] = 1.0 / denom
        for c in range(N_HEADS // 2):
            pair = None
            for head in (2 * c, 2 * c + 1):
                o = (_dot(p_scr[head, :, 0:n_loc], vw[:, kv_lanes(head)])
                     + _dot(p_scr[head, :, n_loc:], vc[:, kv_lanes(head)])) * m_scr[head]
                pair = o if pair is None else pair + o
            mix[pl.ds(r0, QB), c * LANES:(c + 1) * LANES] = pair.astype(BF16)

        slab = pext[pl.ds(pl.multiple_of(r0 + POOL_OFF, 8), POOL_SLAB), :]
        tpos = tok + (n * TQ + j * QB)
        for g, w in enumerate(POOL_WINDOWS):
            sg = slab[:, g * LANES:(g + 1) * LANES]
            hi, lo = _split_bf16(sg)
            band = band_ref[g]
            wsum = _dot(band, hi) + _dot(band, lo)
            cnt = (jnp.minimum(tpos - w // 2 + w, seq_len) - jnp.maximum(tpos - w // 2, 0)).astype(F32)
            dlt = wsum / cnt - sg[POOL_OFF:POOL_OFF + QB, :]
            y = _dot(dlt.astype(BF16), poolw_ref[g]) * pscale_ref[:, g * LANES:(g + 1) * LANES]
            mix[pl.ds(r0, QB), ATTN_WIDTH + g * LANES:ATTN_WIDTH + (g + 1) * LANES] = y.astype(BF16)
        return carry

    lax.fori_loop(0, TQ // QB, sub_block, 0)

    proj = _dot(mix[...], wout_ref[...])
    x1 = x_ref[0] + mod_ref[0, 2:3, :] * proj
    x1_ref[0] = x1
    h2 = _norm_mod(x1, g2_ref[...], mod_ref[0, 3:4, :], mod_ref[0, 4:5, :])
    h2_ref[...] = _pack_words(h2)
    h_hi, h_lo = _split_bf16(h2)
    wrh = wrh_ref[...]
    lg_ref[...] = _dot_nt(wrh, h_hi) + _dot_nt(wrh, h_lo) + _dot_nt(wrl_ref[...], h_hi)


def _attn(sink, q, k4, v4, kc4, vc4, p, x, mod, band, poolw_bf, pscale, wout_bf, g2, wr_hi, wr_lo):
    b, s, d = x.shape
    c = kc4.shape[1]
    nt = s // TQ
    hb = TQ // QB
    pb = TQ // 8
    kv_main = pl.BlockSpec((1, TQ, 4 * LANES), lambda bi, n: (bi, n, 0))
    kv_prev = pl.BlockSpec((1, QB, 4 * LANES), lambda bi, n: (bi, jnp.maximum(n * hb - 1, 0), 0))
    kv_next = pl.BlockSpec((1, QB, 4 * LANES), lambda bi, n: (bi, jnp.minimum((n + 1) * hb, s // QB - 1), 0))
    const2 = lambda bi, n: (0, 0)
    const3 = lambda bi, n: (0, 0, 0)
    return pl.pallas_call(
        functools.partial(_attn_kernel, s),
        out_shape=(jax.ShapeDtypeStruct((b, s, d), F32),
                   jax.ShapeDtypeStruct((b * s, d // 2), U32),
                   jax.ShapeDtypeStruct((N_EXPERTS, b * s), F32)),
        grid=(b, nt),
        in_specs=[pl.BlockSpec(memory_space=pltpu.SMEM),
                  pl.BlockSpec((1, TQ, ATTN_WIDTH), lambda bi, n: (bi, n, 0)),
                  kv_main, kv_prev, kv_next, kv_main, kv_prev, kv_next,
                  pl.BlockSpec((1, c, 4 * LANES), lambda bi, n: (bi, 0, 0)),
                  pl.BlockSpec((1, c, 4 * LANES), lambda bi, n: (bi, 0, 0)),
                  pl.BlockSpec((1, TQ, POOL_WIDTH), lambda bi, n: (bi, n, 0)),
                  pl.BlockSpec((1, 8, POOL_WIDTH), lambda bi, n: (bi, jnp.maximum(n * pb - 1, 0), 0)),
                  pl.BlockSpec((1, 8, POOL_WIDTH), lambda bi, n: (bi, jnp.minimum((n + 1) * pb, s // 8 - 1), 0)),
                  pl.BlockSpec((1, TQ, d), lambda bi, n: (bi, n, 0)),
                  pl.BlockSpec((1, 6, d), lambda bi, n: (bi, 0, 0)),
                  pl.BlockSpec((len(POOL_WINDOWS), QB, POOL_SLAB), const3),
                  pl.BlockSpec((len(POOL_WINDOWS), POOL_GROUP_DIM, POOL_GROUP_DIM), const3),
                  pl.BlockSpec((1, POOL_WIDTH), const2),
                  pl.BlockSpec((d, d), const2),
                  pl.BlockSpec((1, d), const2),
                  pl.BlockSpec((N_EXPERTS, d), const2),
                  pl.BlockSpec((N_EXPERTS, d), const2)],
        out_specs=(pl.BlockSpec((1, TQ, d), lambda bi, n: (bi, n, 0)),
                   pl.BlockSpec((TQ, d // 2), lambda bi, n: (bi * nt + n, 0)),
                   pl.BlockSpec((N_EXPERTS, TQ), lambda bi, n: (0, bi * nt + n))),
        scratch_shapes=[pltpu.VMEM((TQ + 2 * QB, 4 * LANES), BF16),
                        pltpu.VMEM((TQ + 2 * QB, 4 * LANES), BF16),
                        pltpu.VMEM((TQ + 2 * QB, POOL_WIDTH), F32),
                        pltpu.VMEM((TQ, d), BF16),
                        pltpu.VMEM((N_HEADS, QB, 3 * QB + c), F32),
                        pltpu.VMEM((N_HEADS, QB, 3 * QB + c), BF16),
                        pltpu.VMEM((N_HEADS, QB, LANES), F32)],
        compiler_params=pltpu.CompilerParams(vmem_limit_bytes=VMEM_LIMIT),
        name="attn",
    )(sink, q, k4, k4, k4, v4, v4, v4, kc4, vc4, p, p, p, x, mod, band, poolw_bf, pscale, wout_bf,
      g2, wr_hi, wr_lo)


def _first_argmax_rows(v, row_iota, n_rows):
    m = jnp.max(v, axis=0, keepdims=True)
    idx = jnp.min(jnp.where(v == m, row_iota, n_rows), axis=0, keepdims=True)
    return m, idx


def _route_kernel(lg_ref, bias_ref, tri_ref, idx_ref, gate_ref, rank_ref, cnt_ref, carry):
    i = pl.program_id(0)

    @pl.when(i == 0)
    def _():
        carry[...] = jnp.zeros_like(carry)

    scores = 1.0 / (1.0 + jnp.exp(-lg_ref[...]))
    biased = scores + bias_ref[...]
    e_iota = lax.broadcasted_iota(jnp.int32, scores.shape, 0).astype(F32)
    g_iota = lax.broadcasted_iota(jnp.int32, (EXPERTS_PER_GROUP, TR), 0).astype(F32)
    neg = -jnp.inf

    grp = []
    for g in range(N_EXPERT_GROUPS):
        blk = biased[g * EXPERTS_PER_GROUP:(g + 1) * EXPERTS_PER_GROUP, :]
        m1, i1 = _first_argmax_rows(blk, g_iota, float(EXPERTS_PER_GROUP))
        m2 = jnp.max(jnp.where(g_iota == i1, neg, blk), axis=0, keepdims=True)
        grp.append(m1 + m2)
    grp = jnp.concatenate(grp, axis=0)
    gg_iota = lax.broadcasted_iota(jnp.int32, grp.shape, 0).astype(F32)
    grp_sel = jnp.zeros(grp.shape, F32)
    for _ in range(TOPK_GROUPS):
        _, gi = _first_argmax_rows(grp, gg_iota, float(N_EXPERT_GROUPS))
        hit = gg_iota == gi
        grp_sel = jnp.where(hit, 1.0, grp_sel)
        grp = jnp.where(hit, neg, grp)
    allowed = jnp.concatenate(
        [jnp.broadcast_to(grp_sel[g:g + 1, :], (EXPERTS_PER_GROUP, TR)) for g in range(N_EXPERT_GROUPS)], axis=0)
    masked = jnp.where(allowed > 0.5, biased, neg)

    idxs, gates = [], []
    onehot = jnp.zeros(scores.shape, F32)
    for _ in range(TOP_K):
        _, ei = _first_argmax_rows(masked, e_iota, float(N_EXPERTS))
        hit = e_iota == ei
        idxs.append(ei)
        gates.append(jnp.sum(jnp.where(hit, scores, 0.0), axis=0, keepdims=True))
        onehot = jnp.where(hit, 1.0, onehot)
        masked = jnp.where(hit, neg, masked)
    idx = jnp.concatenate(idxs, axis=0)
    gate = jnp.concatenate(gates, axis=0)
    gate = gate / jnp.sum(gate, axis=0, keepdims=True) * ROUTED_SCALE

    before = _dot(onehot.astype(BF16), tri_ref[...]) + carry[:, 0:1]
    ranks = [jnp.sum(jnp.where(e_iota == idxs[k], before, 0.0), axis=0, keepdims=True) for k in range(TOP_K)]
    idx_ref[...] = idx.astype(jnp.int32)
    gate_ref[...] = gate
    rank_ref[...] = jnp.concatenate(ranks, axis=0).astype(jnp.int32)
    total = carry[...] + jnp.sum(onehot, axis=1, keepdims=True)
    carry[...] = total
    cnt_ref[...] = total


def _route(lg_t, bias, tri):
    e, t = lg_t.shape
    tok = pl.BlockSpec((TOP_K, TR), lambda i: (0, i))
    return pl.pallas_call(
        _route_kernel,
        out_shape=(jax.ShapeDtypeStruct((TOP_K, t), jnp.int32),
                   jax.ShapeDtypeStruct((TOP_K, t), F32),
                   jax.ShapeDtypeStruct((TOP_K, t), jnp.int32),
                   jax.ShapeDtypeStruct((e, LANES), F32)),
        grid=(t // TR,),
        in_specs=[pl.BlockSpec((e, TR), lambda i: (0, i)),
                  pl.BlockSpec((e, 1), lambda i: (0, 0)),
                  pl.BlockSpec((TR, TR), lambda i: (0, 0))],
        out_specs=(tok, tok, tok, pl.BlockSpec((e, LANES), lambda i: (0, 0))),
        scratch_shapes=[pltpu.VMEM((e, LANES), F32)],
        compiler_params=pltpu.CompilerParams(vmem_limit_bytes=VMEM_LIMIT),
        name="route",
    )(lg_t, bias, tri)


def _sc_mesh():
    return plsc.VectorSubcoreMesh(core_axis_name="c", subcore_axis_name="s")


def _sc_token_base(steps, j):
    worker = lax.axis_index("s") * SC_CORES + lax.axis_index("c")
    return (worker * steps + j) * SC_TOKENS


def _sc_scatter(dest_kt, h2p, n_rows):
    t, width = h2p.shape
    steps = t // (SC_WORKERS * SC_TOKENS)

    @functools.partial(
        pl.kernel, mesh=_sc_mesh(),
        out_type=jax.ShapeDtypeStruct((n_rows, width), U32),
        scratch_types=[pltpu.VMEM((TOP_K, SC_TOKENS), jnp.int32),
                       pltpu.VMEM((SC_TOKENS, width), U32),
                       pltpu.SemaphoreType.DMA],
        name="sc_scatter",
    )
    def body(dest_hbm, h_hbm, xs_hbm, idx_v, rows_v, sem):
        @pl.loop(0, steps)
        def _(j):
            base = _sc_token_base(steps, j)
            pltpu.sync_copy(dest_hbm.at[:, pl.ds(base, SC_TOKENS)], idx_v)
            pltpu.sync_copy(h_hbm.at[pl.ds(base, SC_TOKENS)], rows_v)
            copies = [pltpu.async_copy(rows_v, xs_hbm.at[idx_v.at[k]], sem) for k in range(TOP_K)]
            for cp in copies:
                cp.wait()

    return body(dest_kt, h2p)


def _sc_gather(dest_kt, ys, token0, n_tokens):
    n_k = dest_kt.shape[0]
    width = ys.shape[1]
    steps = n_tokens // (SC_WORKERS * SC_TOKENS)

    @functools.partial(
        pl.kernel, mesh=_sc_mesh(),
        out_type=jax.ShapeDtypeStruct((n_k, n_tokens, width), U32),
        scratch_types=[pltpu.VMEM((n_k, SC_TOKENS), jnp.int32),
                       pltpu.VMEM((SC_TOKENS, width), U32),
                       pltpu.SemaphoreType.DMA],
        name="sc_gather",
    )
    def body(dest_hbm, ys_hbm, yk_hbm, idx_v, rows_v, sem):
        @pl.loop(0, steps)
        def _(j):
            base = _sc_token_base(steps, j)
            pltpu.sync_copy(dest_hbm.at[:, pl.ds(token0 + base, SC_TOKENS)], idx_v)
            for k in range(n_k):
                pltpu.async_copy(ys_hbm.at[idx_v.at[k]], rows_v, sem).wait()
                pltpu.sync_copy(rows_v, yk_hbm.at[k, pl.ds(base, SC_TOKENS)])

    return body(dest_kt, ys)


def _experts_kernel(be_ref, valid_ref, nact_ref, xs_ref, wg_ref, wu_ref, wd_ref, ys_ref, wg_bf, wu_bf, wd_bf):
    i = pl.program_id(0)
    prev = be_ref[jnp.maximum(i - 1, 0)]

    @pl.when((i == 0) | (be_ref[i] != prev))
    def _():
        wg_bf[...] = wg_ref[0].astype(BF16)
        wu_bf[...] = wu_ref[0].astype(BF16)
        wd_bf[...] = wd_ref[0].astype(BF16)

    @pl.when(i < nact_ref[0])
    def _():
        row = lax.broadcasted_iota(jnp.int32, xs_ref.shape, 0)
        words = jnp.where(row < valid_ref[i], xs_ref[...], jnp.uint32(0))
        xb = _unpack_words(words).astype(BF16)
        hid = _silu(_dot(xb, wg_bf[...])) * _dot(xb, wu_bf[...])
        ys_ref[...] = _pack_words(_dot(hid.astype(BF16), wd_bf[...]))

    @pl.when(i >= nact_ref[0])
    def _():
        ys_ref[...] = jnp.zeros_like(ys_ref)


def _experts(block_e, block_valid, n_active, xs, w_gate, w_up, w_down):
    n_rows, half = xs.shape
    d = w_gate.shape[1]
    nb = n_rows // TM_EXP
    grid_spec = pltpu.PrefetchScalarGridSpec(
        num_scalar_prefetch=3,
        grid=(nb,),
        in_specs=[pl.BlockSpec((TM_EXP, half), lambda i, be, bv, na: (jnp.minimum(i, na[0] - 1), 0)),
                  pl.BlockSpec((1, d, D_EXPERT), lambda i, be, bv, na: (be[i], 0, 0)),
                  pl.BlockSpec((1, d, D_EXPERT), lambda i, be, bv, na: (be[i], 0, 0)),
                  pl.BlockSpec((1, D_EXPERT, d), lambda i, be, bv, na: (be[i], 0, 0))],
        out_specs=pl.BlockSpec((TM_EXP, half), lambda i, be, bv, na: (i, 0)),
        scratch_shapes=[pltpu.VMEM((d, D_EXPERT), BF16), pltpu.VMEM((d, D_EXPERT), BF16),
                        pltpu.VMEM((D_EXPERT, d), BF16)],
    )
    return pl.pallas_call(
        _experts_kernel,
        out_shape=jax.ShapeDtypeStruct((n_rows, half), U32),
        grid_spec=grid_spec,
        compiler_params=pltpu.CompilerParams(vmem_limit_bytes=VMEM_LIMIT),
        name="experts",
    )(block_e, block_valid, n_active, xs, w_gate, w_up, w_down)


def _combine_kernel(yk_ref, x1_ref, h2_ref, gate_ref, mod_ref, wsg_ref, wsu_ref, wsd_ref, fg_ref, *out_refs):
    out_ref = out_refs[-1]
    hb = _unpack_words(h2_ref[...]).astype(BF16)
    hid = _silu(_dot(hb, wsg_ref[...])) * _dot(hb, wsu_ref[...])
    ffn = _dot(hid.astype(BF16), wsd_ref[...])
    gate = gate_ref[...]
    for k in range(TOP_K):
        ffn = ffn + gate[:, k:k + 1] * _unpack_words(yk_ref[k])
    x2 = x1_ref[...] + mod_ref[0, 5:6, :] * ffn
    ms = jnp.mean(x2 * x2, axis=-1, keepdims=True)
    out_ref[...] = x2 * lax.rsqrt(ms + EPS) * fg_ref[...]


def _combine(yk, token0, prev_out, x1, h2p, gate_tk, mod, wsg_bf, wsu_bf, wsd_bf, final_g, seq_len):
    t, d = x1.shape
    tiles_per_seq = seq_len // TF
    tile0 = token0 // TF
    tok = pl.BlockSpec((TF, d), lambda i: (tile0 + i, 0))
    in_specs = [pl.BlockSpec((TOP_K, TF, d // 2), lambda i: (0, i, 0)),
                tok, pl.BlockSpec((TF, d // 2), lambda i: (tile0 + i, 0)),
                pl.BlockSpec((TF, TOP_K), lambda i: (tile0 + i, 0)),
                pl.BlockSpec((1, 6, d), lambda i: ((tile0 + i) // tiles_per_seq, 0, 0)),
                pl.BlockSpec((d, D_SHARED), lambda i: (0, 0)),
                pl.BlockSpec((d, D_SHARED), lambda i: (0, 0)),
                pl.BlockSpec((D_SHARED, d), lambda i: (0, 0)),
                pl.BlockSpec((1, d), lambda i: (0, 0))]
    args = [yk, x1, h2p, gate_tk, mod, wsg_bf, wsu_bf, wsd_bf, final_g]
    aliases = {}
    if prev_out is not None:
        in_specs.append(pl.BlockSpec(memory_space=pl.ANY))
        args.append(prev_out)
        aliases = {len(args) - 1: 0}
    return pl.pallas_call(
        _combine_kernel,
        out_shape=jax.ShapeDtypeStruct((t, d), F32),
        grid=(yk.shape[1] // TF,),
        in_specs=in_specs,
        out_specs=tok,
        input_output_aliases=aliases,
        compiler_params=pltpu.CompilerParams(vmem_limit_bytes=VMEM_LIMIT),
        name="combine",
    )(*args)


def _rope_tables(n_tokens):
    n_rows = n_tokens // GRID_W
    n_freq = HEAD_DIM // 4
    inv_freq = ROPE_THETA ** (-jnp.arange(n_freq, dtype=F32) / n_freq)
    ang_r = jnp.arange(n_rows).astype(F32)[:, None] * inv_freq[None, :]
    ang_c = jnp.arange(GRID_W).astype(F32)[:, None] * inv_freq[None, :]

    def per_token(row_part, col_part):
        rows = jnp.broadcast_to(row_part[:, None, :], (n_rows, GRID_W, n_freq))
        cols = jnp.broadcast_to(col_part[None, :, :], (n_rows, GRID_W, n_freq))
        return rows.reshape(n_tokens, n_freq), cols.reshape(n_tokens, n_freq)

    cos_r, cos_c = per_token(jnp.cos(ang_r), jnp.cos(ang_c))
    sin_r, sin_c = per_token(jnp.sin(ang_r), jnp.sin(ang_c))
    cos = jnp.concatenate([cos_r, cos_r, cos_c, cos_c], axis=1)
    sin = jnp.concatenate([-sin_r, sin_r, -sin_c, sin_c], axis=1)
    reps = LANES // HEAD_DIM
    return jnp.tile(cos, (1, reps)), jnp.tile(sin, (1, reps))


def _pool_bands():
    i = jnp.arange(QB)[:, None]
    r = jnp.arange(POOL_SLAB)[None, :]
    return jnp.stack([((r >= i + POOL_OFF - w // 2) & (r < i + POOL_OFF + w // 2)).astype(BF16)
                      for w in POOL_WINDOWS])


def kernel(x, c, ctx, c_ctx, w_ada, b_ada, norm1_g, norm2_g, w_in, attn_sink, pool_w, pool_scale, w_out,
           w_router, router_bias, w_gate, w_up, w_down, ws_gate, ws_up, ws_down, final_g):
    b, s, d = x.shape
    t = b * s
    assert w_ada.shape[0] == 1 and d == D_MODEL and s % TQ == 0 and b + 1 <= 8

    c8 = jnp.zeros((8, d), F32).at[:b].set(c).at[b].set(c_ctx)
    mod = _ada(c8, w_ada[0], b_ada[0]).reshape(8, 6, d)
    g1 = norm1_g[0].reshape(1, d)
    g2 = norm2_g[0].reshape(1, d)
    w_in_bf = w_in[0].astype(BF16)
    cos_t, sin_t = _rope_tables(s)

    q, k4, v4, p = _inproj(x, mod, g1, w_in_bf, cos_t, sin_t)
    kc4, vc4 = _ctxproj(ctx, mod[b:b + 1], g1, w_in_bf[:, ATTN_WIDTH:ATTN_WIDTH + 2 * KV_WIDTH])

    wr_t = w_router[0].T
    wr_hi = wr_t.astype(BF16)
    wr_lo = (wr_t - wr_hi.astype(F32)).astype(BF16)
    x1, h2p, lg_t = _attn(attn_sink[0], q, k4, v4, kc4, vc4, p, x, mod, _pool_bands(),
                         pool_w[0].astype(BF16), pool_scale[0].reshape(1, POOL_WIDTH), w_out[0].astype(BF16),
                         g2, wr_hi, wr_lo)

    tri = jnp.triu(jnp.ones((TR, TR), BF16), k=1)
    idx_kt, gate_kt, rank_kt, counts = _route(lg_t, router_bias[0].reshape(N_EXPERTS, 1), tri)

    sizes = counts[:, 0].astype(jnp.int32)
    padded = (sizes + TM_EXP - 1) // TM_EXP * TM_EXP
    pad_end = jnp.cumsum(padded)
    pad_start = pad_end - padded
    n_rows = -(-(t * TOP_K + N_EXPERTS * (TM_EXP - 1)) // TM_EXP) * TM_EXP
    n_blocks = n_rows // TM_EXP
    experts = jnp.arange(N_EXPERTS, dtype=jnp.int32)
    start_of = jnp.sum(jnp.where(idx_kt[None] == experts[:, None, None], pad_start[:, None, None], 0), axis=0)
    dest_kt = start_of + rank_kt
    block_row = jnp.arange(n_blocks, dtype=jnp.int32) * TM_EXP
    block_e = jnp.minimum(jnp.sum((pad_end[None, :] <= block_row[:, None]).astype(jnp.int32), axis=1),
                          N_EXPERTS - 1)
    n_active = (pad_end[-1:] // TM_EXP).astype(jnp.int32)

    routed_end = (pad_start + sizes)[block_e]
    block_valid = jnp.clip(routed_end - block_row, 0, TM_EXP).astype(jnp.int32)

    xs = _sc_scatter(dest_kt, h2p, n_rows)
    ys = _experts(block_e, block_valid, n_active, xs, w_gate[0], w_up[0], w_down[0])
    chunk = t // COMBINE_CHUNKS
    shared_w = (ws_gate[0].astype(BF16), ws_up[0].astype(BF16), ws_down[0].astype(BF16))
    gate_tk = gate_kt.T
    out = None
    for token0 in range(0, t, chunk):
        out = _combine(_sc_gather(dest_kt, ys, token0, chunk), token0, out, x1.reshape(t, d), h2p, gate_tk, mod,
                       *shared_w, final_g.reshape(1, d), s)
    return out.reshape(b, s, d)
```

```python
import functools

import jax
import jax.numpy as jnp
from jax import lax
from jax.experimental import pallas as pl
from jax.experimental.pallas import tpu as pltpu
from jax.experimental.pallas import tpu_sc as plsc

F32 = jnp.float32
BF16 = jnp.bfloat16

D_MODEL = 1024
GRID_W = 64
N_HEADS = 8
N_KV_HEADS = 2
HEAD_DIM = 64
ATTN_WIDTH = N_HEADS * HEAD_DIM
KV_WIDTH = N_KV_HEADS * HEAD_DIM
WINDOW = 128
ROPE_THETA = 10000.0
POOL_WINDOWS = (2, 4, 8, 16)
POOL_WIDTH = D_MODEL - ATTN_WIDTH
POOL_GROUP_DIM = POOL_WIDTH // len(POOL_WINDOWS)
IN_COLS = ATTN_WIDTH + 2 * KV_WIDTH + POOL_WIDTH
N_EXPERTS = 64
TOP_K = 8
N_EXPERT_GROUPS = 8
EXPERTS_PER_GROUP = N_EXPERTS // N_EXPERT_GROUPS
TOPK_GROUPS = 4
D_EXPERT = 256
D_SHARED = 256
ROUTED_SCALE = 2.5
EPS = 1e-6
LOG2E = 1.4426950408889634

LANES = 128
U32 = jnp.uint32
VMEM_LIMIT = 48 * 1024 * 1024

TM_PROJ = 512
TQ = 512
QB = 128
POOL_SLAB = 256
POOL_OFF = 64
TR = 512
TM_EXP = 512
TF = 256
COMBINE_CHUNKS = 4
SC_CORES = 2
SC_WORKERS = 32
SC_TOKENS = 128


def _silu(x):
    return x * (1.0 / (1.0 + jnp.exp(-x)))


def _split_bf16(x):
    hi = x.astype(BF16)
    lo = (x - hi.astype(F32)).astype(BF16)
    return hi, lo


def _dot(a, b):
    return jnp.dot(a, b, preferred_element_type=F32)


def _pack_words(val):
    half = val.shape[1] // 2
    lo = lax.bitcast_convert_type(val[:, :half].astype(BF16).astype(F32), U32)
    hi = lax.bitcast_convert_type(val[:, half:].astype(BF16).astype(F32), U32)
    return lax.shift_right_logical(lo, jnp.uint32(16)) | hi


def _unpack_words(words):
    lo = lax.bitcast_convert_type(lax.shift_left(words, jnp.uint32(16)), F32)
    hi = lax.bitcast_convert_type(words & jnp.uint32(0xFFFF0000), F32)
    return jnp.concatenate([lo, hi], axis=1)


def _dot_nt(a, b):
    return lax.dot_general(a, b, (((1,), (1,)), ((), ())), preferred_element_type=F32)


def _ada_kernel(c_ref, w_ref, b_ref, o_ref):
    a_hi, a_lo = _split_bf16(_silu(c_ref[...]))
    w_hi, w_lo = _split_bf16(w_ref[...])
    o_ref[...] = _dot(a_hi, w_hi) + _dot(a_lo, w_hi) + _dot(a_hi, w_lo) + b_ref[...]


def _ada(c8, w_ada, b_ada):
    d = c8.shape[1]
    n = w_ada.shape[1]
    tn = 512
    return pl.pallas_call(
        _ada_kernel,
        out_shape=jax.ShapeDtypeStruct((8, n), F32),
        grid=(n // tn,),
        in_specs=[pl.BlockSpec((8, d), lambda j: (0, 0)),
                  pl.BlockSpec((d, tn), lambda j: (0, j)),
                  pl.BlockSpec((1, tn), lambda j: (0, j))],
        out_specs=pl.BlockSpec((8, tn), lambda j: (0, j)),
        compiler_params=pltpu.CompilerParams(vmem_limit_bytes=VMEM_LIMIT),
        name="ada",
    )(c8, w_ada, b_ada.reshape(1, n))


def _norm_mod(x, g, shift, scale):
    ms = jnp.mean(x * x, axis=-1, keepdims=True)
    return (x * lax.rsqrt(ms + EPS) * g) * (1.0 + scale) + shift


def _lane_variants(t):
    lane = lax.broadcasted_iota(jnp.int32, t.shape, 1)
    lo = lane < HEAD_DIM
    tr = pltpu.roll(t, HEAD_DIM, 1)
    zero = jnp.zeros_like(t)
    return (jnp.where(lo, t, zero), jnp.where(lo, zero, tr),
            jnp.where(lo, tr, zero), jnp.where(lo, zero, t))


def _store_variants(ref, t):
    for i, var in enumerate(_lane_variants(t)):
        ref[0, :, i * LANES:(i + 1) * LANES] = var.astype(BF16)


def _inproj_kernel(x_ref, mod_ref, g_ref, w_ref, cos_ref, sin_ref, q_ref, k_ref, v_ref, p_ref):
    h = _norm_mod(x_ref[0], g_ref[...], mod_ref[0, 0:1, :], mod_ref[0, 1:2, :])
    z = _dot(h.astype(BF16), w_ref[...])
    cos = cos_ref[...]
    sin = sin_ref[...]
    lane = lax.broadcasted_iota(jnp.int32, cos.shape, 1)
    first_half = (lane & 16) == 0

    def rope(zc):
        partner = jnp.where(first_half, pltpu.roll(zc, LANES - 16, 1), pltpu.roll(zc, 16, 1))
        return zc * cos + partner * sin

    scale = HEAD_DIM ** -0.5 * LOG2E
    for c in range(ATTN_WIDTH // LANES):
        q_ref[0, :, c * LANES:(c + 1) * LANES] = (rope(z[:, c * LANES:(c + 1) * LANES]) * scale).astype(BF16)
    _store_variants(k_ref, rope(z[:, ATTN_WIDTH:ATTN_WIDTH + KV_WIDTH]))
    _store_variants(v_ref, z[:, ATTN_WIDTH + KV_WIDTH:ATTN_WIDTH + 2 * KV_WIDTH])
    p_ref[0] = z[:, ATTN_WIDTH + 2 * KV_WIDTH:]


def _inproj(x, mod, g1, w_in_bf, cos_t, sin_t):
    b, s, d = x.shape
    tm = TM_PROJ
    return pl.pallas_call(
        _inproj_kernel,
        out_shape=(jax.ShapeDtypeStruct((b, s, ATTN_WIDTH), BF16),
                   jax.ShapeDtypeStruct((b, s, 4 * LANES), BF16),
                   jax.ShapeDtypeStruct((b, s, 4 * LANES), BF16),
                   jax.ShapeDtypeStruct((b, s, POOL_WIDTH), F32)),
        grid=(s // tm, b),
        in_specs=[pl.BlockSpec((1, tm, d), lambda n, bi: (bi, n, 0)),
                  pl.BlockSpec((1, 6, d), lambda n, bi: (bi, 0, 0)),
                  pl.BlockSpec((1, d), lambda n, bi: (0, 0)),
                  pl.BlockSpec((d, IN_COLS), lambda n, bi: (0, 0)),
                  pl.BlockSpec((tm, LANES), lambda n, bi: (n, 0)),
                  pl.BlockSpec((tm, LANES), lambda n, bi: (n, 0))],
        out_specs=(pl.BlockSpec((1, tm, ATTN_WIDTH), lambda n, bi: (bi, n, 0)),
                   pl.BlockSpec((1, tm, 4 * LANES), lambda n, bi: (bi, n, 0)),
                   pl.BlockSpec((1, tm, 4 * LANES), lambda n, bi: (bi, n, 0)),
                   pl.BlockSpec((1, tm, POOL_WIDTH), lambda n, bi: (bi, n, 0))),
        compiler_params=pltpu.CompilerParams(vmem_limit_bytes=VMEM_LIMIT),
        name="inproj",
    )(x, mod, g1, w_in_bf, cos_t, sin_t)


def _ctxproj_kernel(x_ref, mod_ref, g_ref, w_ref, k_ref, v_ref):
    h = _norm_mod(x_ref[0], g_ref[...], mod_ref[0, 0:1, :], mod_ref[0, 1:2, :])
    z = _dot(h.astype(BF16), w_ref[...])
    _store_variants(k_ref, z[:, :KV_WIDTH])
    _store_variants(v_ref, z[:, KV_WIDTH:])


def _ctxproj(ctx, mod_c, g1, w_kv_bf):
    b, c, d = ctx.shape
    return pl.pallas_call(
        _ctxproj_kernel,
        out_shape=(jax.ShapeDtypeStruct((b, c, 4 * LANES), BF16),
                   jax.ShapeDtypeStruct((b, c, 4 * LANES), BF16)),
        grid=(b,),
        in_specs=[pl.BlockSpec((1, c, d), lambda bi: (bi, 0, 0)),
                  pl.BlockSpec((1, 6, d), lambda bi: (0, 0, 0)),
                  pl.BlockSpec((1, d), lambda bi: (0, 0)),
                  pl.BlockSpec((d, 2 * KV_WIDTH), lambda bi: (0, 0))],
        out_specs=(pl.BlockSpec((1, c, 4 * LANES), lambda bi: (bi, 0, 0)),
                   pl.BlockSpec((1, c, 4 * LANES), lambda bi: (bi, 0, 0))),
        compiler_params=pltpu.CompilerParams(vmem_limit_bytes=VMEM_LIMIT),
        name="ctxproj",
    )(ctx, mod_c, g1, w_kv_bf)


def _fold_lane_tiles(op, *arrays):
    tiles = [a[:, i * LANES:(i + 1) * LANES] for a in arrays for i in range(a.shape[1] // LANES)]
    while len(tiles) > 1:
        tiles = [op(tiles[i], tiles[i + 1]) if i + 1 < len(tiles) else tiles[i] for i in range(0, len(tiles), 2)]
    return tiles[0]


def _attn_kernel(seq_len, sink_ref, q_ref, k_ref, kp_ref, kn_ref, v_ref, vp_ref, vn_ref, kc_ref, vc_ref,
                 p_ref, pp_ref, pn_ref, x_ref, mod_ref, band_ref, poolw_ref, pscale_ref, wout_ref,
                 g2_ref, wrh_ref, wrl_ref, x1_ref, h2_ref, lg_ref, kwin, vwin, pext, mix, s_scr, p_scr, m_scr):
    n = pl.program_id(1)
    n_last = pl.num_programs(1) - 1

    kwin[0:QB, :] = kp_ref[0]
    kwin[QB:QB + TQ, :] = k_ref[0]
    kwin[QB + TQ:, :] = kn_ref[0]
    vwin[0:QB, :] = vp_ref[0]
    vwin[QB:QB + TQ, :] = v_ref[0]
    vwin[QB + TQ:, :] = vn_ref[0]

    pext[0:QB - 8, :] = jnp.zeros((QB - 8, POOL_WIDTH), F32)
    pext[QB - 8:QB, :] = jnp.where(n > 0, pp_ref[0], 0.0)
    pext[QB:QB + TQ, :] = p_ref[0]
    pext[QB + TQ:QB + TQ + 8, :] = jnp.where(n < n_last, pn_ref[0], 0.0)
    pext[QB + TQ + 8:, :] = jnp.zeros((QB - 8, POOL_WIDTH), F32)

    row = lax.broadcasted_iota(jnp.int32, (QB, 3 * QB), 0)
    col = lax.broadcasted_iota(jnp.int32, (QB, 3 * QB), 1)
    in_band = (col >= row) & (col <= row + 2 * WINDOW)
    tok = lax.broadcasted_iota(jnp.int32, (QB, 1), 0)
    kc = kc_ref[0]
    vc = vc_ref[0]

    def sub_block(j, carry):
        r0 = pl.multiple_of(j * QB, QB)
        qj = q_ref[0, pl.ds(r0, QB), :]
        kw = kwin[pl.ds(r0, 3 * QB), :]
        vw = vwin[pl.ds(r0, 3 * QB), :]
        kpos = col + (n * TQ + j * QB - QB)
        ok = in_band & (kpos >= 0) & (kpos < seq_len)
        bias = jnp.where(ok, 0.0, -jnp.inf)
        n_loc = 3 * QB

        def kv_lanes(head):
            var = 2 * (head // (N_HEADS // N_KV_HEADS)) + head % 2
            return slice(var * LANES, (var + 1) * LANES)

        for head in range(N_HEADS):
            qc = qj[:, (head // 2) * LANES:(head // 2 + 1) * LANES]
            s_scr[head, :, 0:n_loc] = _dot_nt(qc, kw[:, kv_lanes(head)]) + bias
            s_scr[head, :, n_loc:] = _dot_nt(qc, kc[:, kv_lanes(head)])
        n_tiles = (n_loc + kc.shape[0]) // LANES
        for head in range(N_HEADS):
            row_max = jnp.max(_fold_lane_tiles(jnp.maximum, s_scr[head]), axis=1, keepdims=True)
            m_scr[head] = jnp.broadcast_to(jnp.maximum(row_max, sink_ref[head] * LOG2E), (QB, LANES))
        for head in range(N_HEADS):
            m = m_scr[head]
            acc = None
            for i in range(n_tiles):
                p = jnp.exp2(s_scr[head, :, i * LANES:(i + 1) * LANES] - m)
                p_scr[head, :, i * LANES:(i + 1) * LANES] = p.astype(BF16)
                acc = p if acc is None else acc + p
            denom = (jnp.broadcast_to(jnp.sum(acc, axis=1, keepdims=True), (QB, LANES))
                     + jnp.exp2(sink_ref[head] * LOG2E - m))
            m_scr[head] = 1.0 / denom
        for c in range(N_HEADS // 2):
            pair = None
            for head in (2 * c, 2 * c + 1):
                o = (_dot(p_scr[head, :, 0:n_loc], vw[:, kv_lanes(head)])
                     + _dot(p_scr[head, :, n_loc:], vc[:, kv_lanes(head)])) * m_scr[head]
                pair = o if pair is None else pair + o
            mix[pl.ds(r0, QB), c * LANES:(c + 1) * LANES] = pair.astype(BF16)

        slab = pext[pl.ds(pl.multiple_of(r0 + POOL_OFF, 8), POOL_SLAB), :]
        tpos = tok + (n * TQ + j * QB)
        for g, w in enumerate(POOL_WINDOWS):
            sg = slab[:, g * LANES:(g + 1) * LANES]
            hi, lo = _split_bf16(sg)
            band = band_ref[g]
            wsum = _dot(band, hi) + _dot(band, lo)
            cnt = (jnp.minimum(tpos - w // 2 + w, seq_len) - jnp.maximum(tpos - w // 2, 0)).astype(F32)
            dlt = wsum / cnt - sg[POOL_OFF:POOL_OFF + QB, :]
            y = _dot(dlt.astype(BF16), poolw_ref[g]) * pscale_ref[:, g * LANES:(g + 1) * LANES]
            mix[pl.ds(r0, QB), ATTN_WIDTH + g * LANES:ATTN_WIDTH + (g + 1) * LANES] = y.astype(BF16)
        return carry

    lax.fori_loop(0, TQ // QB, sub_block, 0)

    proj = _dot(mix[...], wout_ref[...])
    x1 = x_ref[0] + mod_ref[0, 2:3, :] * proj
    x1_ref[0] = x1
    h2 = _norm_mod(x1, g2_ref[...], mod_ref[0, 3:4, :], mod_ref[0, 4:5, :])
    h2_ref[...] = _pack_words(h2)
    h_hi, h_lo = _split_bf16(h2)
    wrh = wrh_ref[...]
    lg_ref[...] = _dot_nt(wrh, h_hi) + _dot_nt(wrh, h_lo) + _dot_nt(wrl_ref[...], h_hi)


def _attn(sink, q, k4, v4, kc4, vc4, p, x, mod, band, poolw_bf, pscale, wout_bf, g2, wr_hi, wr_lo):
    b, s, d = x.shape
    c = kc4.shape[1]
    nt = s // TQ
    hb = TQ // QB
    pb = TQ // 8
    kv_main = pl.BlockSpec((1, TQ, 4 * LANES), lambda bi, n: (bi, n, 0))
    kv_prev = pl.BlockSpec((1, QB, 4 * LANES), lambda bi, n: (bi, jnp.maximum(n * hb - 1, 0), 0))
    kv_next = pl.BlockSpec((1, QB, 4 * LANES), lambda bi, n: (bi, jnp.minimum((n + 1) * hb, s // QB - 1), 0))
    const2 = lambda bi, n: (0, 0)
    const3 = lambda bi, n: (0, 0, 0)
    return pl.pallas_call(
        functools.partial(_attn_kernel, s),
        out_shape=(jax.ShapeDtypeStruct((b, s, d), F32),
                   jax.ShapeDtypeStruct((b * s, d // 2), U32),
                   jax.ShapeDtypeStruct((N_EXPERTS, b * s), F32)),
        grid=(b, nt),
        in_specs=[pl.BlockSpec(memory_space=pltpu.SMEM),
                  pl.BlockSpec((1, TQ, ATTN_WIDTH), lambda bi, n: (bi, n, 0)),
                  kv_main, kv_prev, kv_next, kv_main, kv_prev, kv_next,
                  pl.BlockSpec((1, c, 4 * LANES), lambda bi, n: (bi, 0, 0)),
                  pl.BlockSpec((1, c, 4 * LANES), lambda bi, n: (bi, 0, 0)),
                  pl.BlockSpec((1, TQ, POOL_WIDTH), lambda bi, n: (bi, n, 0)),
                  pl.BlockSpec((1, 8, POOL_WIDTH), lambda bi, n: (bi, jnp.maximum(n * pb - 1, 0), 0)),
                  pl.BlockSpec((1, 8, POOL_WIDTH), lambda bi, n: (bi, jnp.minimum((n + 1) * pb, s // 8 - 1), 0)),
                  pl.BlockSpec((1, TQ, d), lambda bi, n: (bi, n, 0)),
                  pl.BlockSpec((1, 6, d), lambda bi, n: (bi, 0, 0)),
                  pl.BlockSpec((len(POOL_WINDOWS), QB, POOL_SLAB), const3),
                  pl.BlockSpec((len(POOL_WINDOWS), POOL_GROUP_DIM, POOL_GROUP_DIM), const3),
                  pl.BlockSpec((1, POOL_WIDTH), const2),
                  pl.BlockSpec((d, d), const2),
                  pl.BlockSpec((1, d), const2),
                  pl.BlockSpec((N_EXPERTS, d), const2),
                  pl.BlockSpec((N_EXPERTS, d), const2)],
        out_specs=(pl.BlockSpec((1, TQ, d), lambda bi, n: (bi, n, 0)),
                   pl.BlockSpec((TQ, d // 2), lambda bi, n: (bi * nt + n, 0)),
                   pl.BlockSpec((N_EXPERTS, TQ), lambda bi, n: (0, bi * nt + n))),
        scratch_shapes=[pltpu.VMEM((TQ + 2 * QB, 4 * LANES), BF16),
                        pltpu.VMEM((TQ + 2 * QB, 4 * LANES), BF16),
                        pltpu.VMEM((TQ + 2 * QB, POOL_WIDTH), F32),
                        pltpu.VMEM((TQ, d), BF16),
                        pltpu.VMEM((N_HEADS, QB, 3 * QB + c), F32),
                        pltpu.VMEM((N_HEADS, QB, 3 * QB + c), BF16),
                        pltpu.VMEM((N_HEADS, QB, LANES), F32)],
        compiler_params=pltpu.CompilerParams(vmem_limit_bytes=VMEM_LIMIT),
        name="attn",
    )(sink, q, k4, k4, k4, v4, v4, v4, kc4, vc4, p, p, p, x, mod, band, poolw_bf, pscale, wout_bf,
      g2, wr_hi, wr_lo)


def _first_argmax_rows(v, row_iota, n_rows):
    m = jnp.max(v, axis=0, keepdims=True)
    idx = jnp.min(jnp.where(v == m, row_iota, n_rows), axis=0, keepdims=True)
    return m, idx


def _route_kernel(lg_ref, bias_ref, tri_ref, idx_ref, gate_ref, rank_ref, cnt_ref, carry):
    i = pl.program_id(0)

    @pl.when(i == 0)
    def _():
        carry[...] = jnp.zeros_like(carry)

    scores = 1.0 / (1.0 + jnp.exp(-lg_ref[...]))
    biased = scores + bias_ref[...]
    e_iota = lax.broadcasted_iota(jnp.int32, scores.shape, 0).astype(F32)
    g_iota = lax.broadcasted_iota(jnp.int32, (EXPERTS_PER_GROUP, TR), 0).astype(F32)
    neg = -jnp.inf

    grp = []
    for g in range(N_EXPERT_GROUPS):
        blk = biased[g * EXPERTS_PER_GROUP:(g + 1) * EXPERTS_PER_GROUP, :]
        m1, i1 = _first_argmax_rows(blk, g_iota, float(EXPERTS_PER_GROUP))
        m2 = jnp.max(jnp.where(g_iota == i1, neg, blk), axis=0, keepdims=True)
        grp.append(m1 + m2)
    grp = jnp.concatenate(grp, axis=0)
    gg_iota = lax.broadcasted_iota(jnp.int32, grp.shape, 0).astype(F32)
    grp_sel = jnp.zeros(grp.shape, F32)
    for _ in range(TOPK_GROUPS):
        _, gi = _first_argmax_rows(grp, gg_iota, float(N_EXPERT_GROUPS))
        hit = gg_iota == gi
        grp_sel = jnp.where(hit, 1.0, grp_sel)
        grp = jnp.where(hit, neg, grp)
    allowed = jnp.concatenate(
        [jnp.broadcast_to(grp_sel[g:g + 1, :], (EXPERTS_PER_GROUP, TR)) for g in range(N_EXPERT_GROUPS)], axis=0)
    masked = jnp.where(allowed > 0.5, biased, neg)

    idxs, gates = [], []
    onehot = jnp.zeros(scores.shape, F32)
    for _ in range(TOP_K):
        _, ei = _first_argmax_rows(masked, e_iota, float(N_EXPERTS))
        hit = e_iota == ei
        idxs.append(ei)
        gates.append(jnp.sum(jnp.where(hit, scores, 0.0), axis=0, keepdims=True))
        onehot = jnp.where(hit, 1.0, onehot)
        masked = jnp.where(hit, neg, masked)
    idx = jnp.concatenate(idxs, axis=0)
    gate = jnp.concatenate(gates, axis=0)
    gate = gate / jnp.sum(gate, axis=0, keepdims=True) * ROUTED_SCALE

    before = _dot(onehot.astype(BF16), tri_ref[...]) + carry[:, 0:1]
    ranks = [jnp.sum(jnp.where(e_iota == idxs[k], before, 0.0), axis=0, keepdims=True) for k in range(TOP_K)]
    idx_ref[...] = idx.astype(jnp.int32)
    gate_ref[...] = gate
    rank_ref[...] = jnp.concatenate(ranks, axis=0).astype(jnp.int32)
    total = carry[...] + jnp.sum(onehot, axis=1, keepdims=True)
    carry[...] = total
    cnt_ref[...] = total


def _route(lg_t, bias, tri):
    e, t = lg_t.shape
    tok = pl.BlockSpec((TOP_K, TR), lambda i: (0, i))
    return pl.pallas_call(
        _route_kernel,
        out_shape=(jax.ShapeDtypeStruct((TOP_K, t), jnp.int32),
                   jax.ShapeDtypeStruct((TOP_K, t), F32),
                   jax.ShapeDtypeStruct((TOP_K, t), jnp.int32),
                   jax.ShapeDtypeStruct((e, LANES), F32)),
        grid=(t // TR,),
        in_specs=[pl.BlockSpec((e, TR), lambda i: (0, i)),
                  pl.BlockSpec((e, 1), lambda i: (0, 0)),
                  pl.BlockSpec((TR, TR), lambda i: (0, 0))],
        out_specs=(tok, tok, tok, pl.BlockSpec((e, LANES), lambda i: (0, 0))),
        scratch_shapes=[pltpu.VMEM((e, LANES), F32)],
        compiler_params=pltpu.CompilerParams(vmem_limit_bytes=VMEM_LIMIT),
        name="route",
    )(lg_t, bias, tri)


def _dest_kernel(start_ref, idx_ref, rank_ref, dest_ref):
    idx = idx_ref[...]
    dest = rank_ref[...]
    for e in range(N_EXPERTS):
        dest = dest + jnp.where(idx == e, start_ref[e], 0)
    dest_ref[...] = dest


def _dest_rows(pad_start, idx_kt, rank_kt):
    n_k, t = idx_kt.shape
    tile = 4096
    blk = pl.BlockSpec((n_k, tile), lambda i: (0, i))
    return pl.pallas_call(
        _dest_kernel,
        out_shape=jax.ShapeDtypeStruct((n_k, t), jnp.int32),
        grid=(t // tile,),
        in_specs=[pl.BlockSpec(memory_space=pltpu.SMEM), blk, blk],
        out_specs=blk,
        name="dest_rows",
    )(pad_start, idx_kt, rank_kt)


def _sc_mesh():
    return plsc.VectorSubcoreMesh(core_axis_name="c", subcore_axis_name="s")


def _sc_token_base(steps, j):
    worker = lax.axis_index("s") * SC_CORES + lax.axis_index("c")
    return (worker * steps + j) * SC_TOKENS


def _sc_scatter(dest_kt, h2p, n_rows):
    t, width = h2p.shape
    steps = t // (SC_WORKERS * SC_TOKENS)

    @functools.partial(
        pl.kernel, mesh=_sc_mesh(),
        out_type=jax.ShapeDtypeStruct((n_rows, width), U32),
        scratch_types=[pltpu.VMEM((TOP_K, SC_TOKENS), jnp.int32),
                       pltpu.VMEM((SC_TOKENS, width), U32),
                       pltpu.SemaphoreType.DMA],
        name="sc_scatter",
    )
    def body(dest_hbm, h_hbm, xs_hbm, idx_v, rows_v, sem):
        @pl.loop(0, steps)
        def _(j):
            base = _sc_token_base(steps, j)
            pltpu.sync_copy(dest_hbm.at[:, pl.ds(base, SC_TOKENS)], idx_v)
            pltpu.sync_copy(h_hbm.at[pl.ds(base, SC_TOKENS)], rows_v)
            copies = [pltpu.async_copy(rows_v, xs_hbm.at[idx_v.at[k]], sem) for k in range(TOP_K)]
            for cp in copies:
                cp.wait()

    return body(dest_kt, h2p)


def _sc_gather(dest_kt, ys, token0, n_tokens):
    n_k = dest_kt.shape[0]
    width = ys.shape[1]
    steps = n_tokens // (SC_WORKERS * SC_TOKENS)

    half = SC_TOKENS // 2
    units = [(k, h) for k in range(n_k) for h in range(2)]
    n_buf = 3

    @functools.partial(
        pl.kernel, mesh=_sc_mesh(),
        out_type=jax.ShapeDtypeStruct((n_k, n_tokens, width), U32),
        scratch_types=[pltpu.VMEM((n_k, SC_TOKENS), jnp.int32),
                       pltpu.VMEM((n_buf, half, width), U32),
                       pltpu.SemaphoreType.DMA((n_buf,))],
        name="sc_gather",
    )
    def body(dest_hbm, ys_hbm, yk_hbm, idx_v, rows_v, sems):
        @pl.loop(0, steps)
        def _(j):
            base = _sc_token_base(steps, j)
            pltpu.sync_copy(dest_hbm.at[:, pl.ds(token0 + base, SC_TOKENS)], idx_v)

            def gather(u):
                k, h = units[u]
                slot = u % n_buf
                return pltpu.make_async_copy(ys_hbm.at[idx_v.at[k, pl.ds(h * half, half)]], rows_v.at[slot],
                                             sems.at[slot])

            gather(0).start()
            gather(1).start()
            for u, (k, h) in enumerate(units):
                gather(u).wait()
                pltpu.sync_copy(rows_v.at[u % n_buf], yk_hbm.at[k, pl.ds(base + h * half, half)])
                if u + 2 < len(units):
                    gather(u + 2).start()

    return body(dest_kt, ys)


def _experts_kernel(be_ref, valid_ref, nact_ref, xs_ref, wg_ref, wu_ref, wd_ref, ys_ref, wg_bf, wu_bf, wd_bf):
    i = pl.program_id(0)
    prev = be_ref[jnp.maximum(i - 1, 0)]

    @pl.when((i == 0) | (be_ref[i] != prev))
    def _():
        wg_bf[...] = wg_ref[0].astype(BF16)
        wu_bf[...] = wu_ref[0].astype(BF16)
        wd_bf[...] = wd_ref[0].astype(BF16)

    @pl.when(i < nact_ref[0])
    def _():
        row = lax.broadcasted_iota(jnp.int32, xs_ref.shape, 0)
        words = jnp.where(row < valid_ref[i], xs_ref[...], jnp.uint32(0))
        xb = _unpack_words(words).astype(BF16)
        hid = _silu(_dot(xb, wg_bf[...])) * _dot(xb, wu_bf[...])
        ys_ref[...] = _pack_words(_dot(hid.astype(BF16), wd_bf[...]))

    @pl.when(i >= nact_ref[0])
    def _():
        ys_ref[...] = jnp.zeros_like(ys_ref)


def _experts(block_e, block_valid, n_active, xs, w_gate, w_up, w_down):
    n_rows, half = xs.shape
    d = w_gate.shape[1]
    nb = n_rows // TM_EXP
    grid_spec = pltpu.PrefetchScalarGridSpec(
        num_scalar_prefetch=3,
        grid=(nb,),
        in_specs=[pl.BlockSpec((TM_EXP, half), lambda i, be, bv, na: (jnp.minimum(i, na[0] - 1), 0)),
                  pl.BlockSpec((1, d, D_EXPERT), lambda i, be, bv, na: (be[i], 0, 0)),
                  pl.BlockSpec((1, d, D_EXPERT), lambda i, be, bv, na: (be[i], 0, 0)),
                  pl.BlockSpec((1, D_EXPERT, d), lambda i, be, bv, na: (be[i], 0, 0))],
        out_specs=pl.BlockSpec((TM_EXP, half), lambda i, be, bv, na: (i, 0)),
        scratch_shapes=[pltpu.VMEM((d, D_EXPERT), BF16), pltpu.VMEM((d, D_EXPERT), BF16),
                        pltpu.VMEM((D_EXPERT, d), BF16)],
    )
    return pl.pallas_call(
        _experts_kernel,
        out_shape=jax.ShapeDtypeStruct((n_rows, half), U32),
        grid_spec=grid_spec,
        compiler_params=pltpu.CompilerParams(vmem_limit_bytes=VMEM_LIMIT),
        name="experts",
    )(block_e, block_valid, n_active, xs, w_gate, w_up, w_down)


def _combine_kernel(yk_ref, x1_ref, h2_ref, gate_ref, mod_ref, wsg_ref, wsu_ref, wsd_ref, fg_ref, *out_refs):
    out_ref = out_refs[-1]
    hb = _unpack_words(h2_ref[...]).astype(BF16)
    hid = _silu(_dot(hb, wsg_ref[...])) * _dot(hb, wsu_ref[...])
    ffn = _dot(hid.astype(BF16), wsd_ref[...])
    gate = gate_ref[...]
    for k in range(TOP_K):
        ffn = ffn + gate[:, k:k + 1] * _unpack_words(yk_ref[k])
    x2 = x1_ref[...] + mod_ref[0, 5:6, :] * ffn
    ms = jnp.mean(x2 * x2, axis=-1, keepdims=True)
    out_ref[...] = x2 * lax.rsqrt(ms + EPS) * fg_ref[...]


def _combine(yk, token0, prev_out, x1, h2p, gate_tk, mod, wsg_bf, wsu_bf, wsd_bf, final_g, seq_len):
    t, d = x1.shape
    tiles_per_seq = seq_len // TF
    tile0 = token0 // TF
    tok = pl.BlockSpec((TF, d), lambda i: (tile0 + i, 0))
    in_specs = [pl.BlockSpec((TOP_K, TF, d // 2), lambda i: (0, i, 0)),
                tok, pl.BlockSpec((TF, d // 2), lambda i: (tile0 + i, 0)),
                pl.BlockSpec((TF, TOP_K), lambda i: (tile0 + i, 0)),
                pl.BlockSpec((1, 6, d), lambda i: ((tile0 + i) // tiles_per_seq, 0, 0)),
                pl.BlockSpec((d, D_SHARED), lambda i: (0, 0)),
                pl.BlockSpec((d, D_SHARED), lambda i: (0, 0)),
                pl.BlockSpec((D_SHARED, d), lambda i: (0, 0)),
                pl.BlockSpec((1, d), lambda i: (0, 0))]
    args = [yk, x1, h2p, gate_tk, mod, wsg_bf, wsu_bf, wsd_bf, final_g]
    aliases = {}
    if prev_out is not None:
        in_specs.append(pl.BlockSpec(memory_space=pl.ANY))
        args.append(prev_out)
        aliases = {len(args) - 1: 0}
    return pl.pallas_call(
        _combine_kernel,
        out_shape=jax.ShapeDtypeStruct((t, d), F32),
        grid=(yk.shape[1] // TF,),
        in_specs=in_specs,
        out_specs=tok,
        input_output_aliases=aliases,
        compiler_params=pltpu.CompilerParams(vmem_limit_bytes=VMEM_LIMIT),
        name="combine",
    )(*args)


def _rope_tables(n_tokens):
    n_rows = n_tokens // GRID_W
    n_freq = HEAD_DIM // 4
    inv_freq = ROPE_THETA ** (-jnp.arange(n_freq, dtype=F32) / n_freq)
    ang_r = jnp.arange(n_rows).astype(F32)[:, None] * inv_freq[None, :]
    ang_c = jnp.arange(GRID_W).astype(F32)[:, None] * inv_freq[None, :]

    def per_token(row_part, col_part):
        rows = jnp.broadcast_to(row_part[:, None, :], (n_rows, GRID_W, n_freq))
        cols = jnp.broadcast_to(col_part[None, :, :], (n_rows, GRID_W, n_freq))
        return rows.reshape(n_tokens, n_freq), cols.reshape(n_tokens, n_freq)

    cos_r, cos_c = per_token(jnp.cos(ang_r), jnp.cos(ang_c))
    sin_r, sin_c = per_token(jnp.sin(ang_r), jnp.sin(ang_c))
    cos = jnp.concatenate([cos_r, cos_r, cos_c, cos_c], axis=1)
    sin = jnp.concatenate([-sin_r, sin_r, -sin_c, sin_c], axis=1)
    reps = LANES // HEAD_DIM
    return jnp.tile(cos, (1, reps)), jnp.tile(sin, (1, reps))


def _pool_bands():
    i = jnp.arange(QB)[:, None]
    r = jnp.arange(POOL_SLAB)[None, :]
    return jnp.stack([((r >= i + POOL_OFF - w // 2) & (r < i + POOL_OFF + w // 2)).astype(BF16)
                      for w in POOL_WINDOWS])


def kernel(x, c, ctx, c_ctx, w_ada, b_ada, norm1_g, norm2_g, w_in, attn_sink, pool_w, pool_scale, w_out,
           w_router, router_bias, w_gate, w_up, w_down, ws_gate, ws_up, ws_down, final_g):
    b, s, d = x.shape
    t = b * s
    assert w_ada.shape[0] == 1 and d == D_MODEL and s % TQ == 0 and b + 1 <= 8

    c8 = jnp.zeros((8, d), F32).at[:b].set(c).at[b].set(c_ctx)
    mod = _ada(c8, w_ada[0], b_ada[0]).reshape(8, 6, d)
    g1 = norm1_g[0].reshape(1, d)
    g2 = norm2_g[0].reshape(1, d)
    w_in_bf = w_in[0].astype(BF16)
    cos_t, sin_t = _rope_tables(s)

    q, k4, v4, p = _inproj(x, mod, g1, w_in_bf, cos_t, sin_t)
    kc4, vc4 = _ctxproj(ctx, mod[b:b + 1], g1, w_in_bf[:, ATTN_WIDTH:ATTN_WIDTH + 2 * KV_WIDTH])

    wr_t = w_router[0].T
    wr_hi = wr_t.astype(BF16)
    wr_lo = (wr_t - wr_hi.astype(F32)).astype(BF16)
    x1, h2p, lg_t = _attn(attn_sink[0], q, k4, v4, kc4, vc4, p, x, mod, _pool_bands(),
                         pool_w[0].astype(BF16), pool_scale[0].reshape(1, POOL_WIDTH), w_out[0].astype(BF16),
                         g2, wr_hi, wr_lo)

    tri = jnp.triu(jnp.ones((TR, TR), BF16), k=1)
    idx_kt, gate_kt, rank_kt, counts = _route(lg_t, router_bias[0].reshape(N_EXPERTS, 1), tri)

    sizes = counts[:, 0].astype(jnp.int32)
    padded = (sizes + TM_EXP - 1) // TM_EXP * TM_EXP
    pad_end = jnp.cumsum(padded)
    pad_start = pad_end - padded
    n_rows = -(-(t * TOP_K + N_EXPERTS * (TM_EXP - 1)) // TM_EXP) * TM_EXP
    n_blocks = n_rows // TM_EXP
    dest_kt = _dest_rows(pad_start, idx_kt, rank_kt)
    block_row = jnp.arange(n_blocks, dtype=jnp.int32) * TM_EXP
    block_e = jnp.minimum(jnp.sum((pad_end[None, :] <= block_row[:, None]).astype(jnp.int32), axis=1),
                          N_EXPERTS - 1)
    n_active = (pad_end[-1:] // TM_EXP).astype(jnp.int32)

    routed_end = (pad_start + sizes)[block_e]
    block_valid = jnp.clip(routed_end - block_row, 0, TM_EXP).astype(jnp.int32)

    xs = _sc_scatter(dest_kt, h2p, n_rows)
    ys = _experts(block_e, block_valid, n_active, xs, w_gate[0], w_up[0], w_down[0])
    chunk = t // COMBINE_CHUNKS
    shared_w = (ws_gate[0].astype(BF16), ws_up[0].astype(BF16), ws_down[0].astype(BF16))
    gate_tk = gate_kt.T
    out = None
    for token0 in range(0, t, chunk):
        out = _combine(_sc_gather(dest_kt, ys, token0, chunk), token0, out, x1.reshape(t, d), h2p, gate_tk, mod,
                       *shared_w, final_g.reshape(1, d), s)
    return out.reshape(b, s, d)
```

```python
import functools

import jax
import jax.numpy as jnp
from jax import lax
from jax.experimental import pallas as pl
from jax.experimental.pallas import tpu as pltpu
from jax.experimental.pallas import tpu_sc as plsc

F32 = jnp.float32
BF16 = jnp.bfloat16

D_MODEL = 1024
GRID_W = 64
N_HEADS = 8
N_KV_HEADS = 2
HEAD_DIM = 64
ATTN_WIDTH = N_HEADS * HEAD_DIM
KV_WIDTH = N_KV_HEADS * HEAD_DIM
WINDOW = 128
ROPE_THETA = 10000.0
POOL_WINDOWS = (2, 4, 8, 16)
POOL_WIDTH = D_MODEL - ATTN_WIDTH
POOL_GROUP_DIM = POOL_WIDTH // len(POOL_WINDOWS)
IN_COLS = ATTN_WIDTH + 2 * KV_WIDTH + POOL_WIDTH
N_EXPERTS = 64
TOP_K = 8
N_EXPERT_GROUPS = 8
EXPERTS_PER_GROUP = N_EXPERTS // N_EXPERT_GROUPS
TOPK_GROUPS = 4
D_EXPERT = 256
D_SHARED = 256
ROUTED_SCALE = 2.5
EPS = 1e-6
LOG2E = 1.4426950408889634

LANES = 128
U32 = jnp.uint32
VMEM_LIMIT = 48 * 1024 * 1024

TM_PROJ = 512
TQ = 512
QB = 128
POOL_SLAB = 256
POOL_OFF = 64
TR = 512
TM_EXP = 512
TF = 256
COMBINE_CHUNKS = 4
SC_CORES = 2
SC_WORKERS = 32
SC_TOKENS = 128


def _silu(x):
    return x * (1.0 / (1.0 + jnp.exp(-x)))


def _split_bf16(x):
    hi = x.astype(BF16)
    lo = (x - hi.astype(F32)).astype(BF16)
    return hi, lo


def _dot(a, b):
    return jnp.dot(a, b, preferred_element_type=F32)


def _pack_words(val):
    half = val.shape[1] // 2
    lo = lax.bitcast_convert_type(val[:, :half].astype(BF16).astype(F32), U32)
    hi = lax.bitcast_convert_type(val[:, half:].astype(BF16).astype(F32), U32)
    return lax.shift_right_logical(lo, jnp.uint32(16)) | hi


def _unpack_words(words):
    lo = lax.bitcast_convert_type(lax.shift_left(words, jnp.uint32(16)), F32)
    hi = lax.bitcast_convert_type(words & jnp.uint32(0xFFFF0000), F32)
    return jnp.concatenate([lo, hi], axis=1)


def _dot_nt(a, b):
    return lax.dot_general(a, b, (((1,), (1,)), ((), ())), preferred_element_type=F32)


def _ada_kernel(c_ref, w_ref, b_ref, o_ref):
    a_hi, a_lo = _split_bf16(_silu(c_ref[...]))
    w_hi, w_lo = _split_bf16(w_ref[...])
    o_ref[...] = _dot(a_hi, w_hi) + _dot(a_lo, w_hi) + _dot(a_hi, w_lo) + b_ref[...]


def _ada(c8, w_ada, b_ada):
    d = c8.shape[1]
    n = w_ada.shape[1]
    tn = 512
    return pl.pallas_call(
        _ada_kernel,
        out_shape=jax.ShapeDtypeStruct((8, n), F32),
        grid=(n // tn,),
        in_specs=[pl.BlockSpec((8, d), lambda j: (0, 0)),
                  pl.BlockSpec((d, tn), lambda j: (0, j)),
                  pl.BlockSpec((1, tn), lambda j: (0, j))],
        out_specs=pl.BlockSpec((8, tn), lambda j: (0, j)),
        compiler_params=pltpu.CompilerParams(vmem_limit_bytes=VMEM_LIMIT),
        name="ada",
    )(c8, w_ada, b_ada.reshape(1, n))


def _norm_mod(x, g, shift, scale):
    ms = jnp.mean(x * x, axis=-1, keepdims=True)
    return (x * lax.rsqrt(ms + EPS) * g) * (1.0 + scale) + shift


def _lane_variants(t):
    lane = lax.broadcasted_iota(jnp.int32, t.shape, 1)
    lo = lane < HEAD_DIM
    tr = pltpu.roll(t, HEAD_DIM, 1)
    zero = jnp.zeros_like(t)
    return (jnp.where(lo, t, zero), jnp.where(lo, zero, tr),
            jnp.where(lo, tr, zero), jnp.where(lo, zero, t))


def _store_variants(ref, t):
    for i, var in enumerate(_lane_variants(t)):
        ref[0, :, i * LANES:(i + 1) * LANES] = var.astype(BF16)


def _inproj_kernel(x_ref, mod_ref, g_ref, w_ref, cos_ref, sin_ref, q_ref, k_ref, v_ref, p_ref):
    h = _norm_mod(x_ref[0], g_ref[...], mod_ref[0, 0:1, :], mod_ref[0, 1:2, :])
    z = _dot(h.astype(BF16), w_ref[...])
    cos = cos_ref[...]
    sin = sin_ref[...]
    lane = lax.broadcasted_iota(jnp.int32, cos.shape, 1)
    first_half = (lane & 16) == 0

    def rope(zc):
        partner = jnp.where(first_half, pltpu.roll(zc, LANES - 16, 1), pltpu.roll(zc, 16, 1))
        return zc * cos + partner * sin

    scale = HEAD_DIM ** -0.5 * LOG2E
    for c in range(ATTN_WIDTH // LANES):
        q_ref[0, :, c * LANES:(c + 1) * LANES] = (rope(z[:, c * LANES:(c + 1) * LANES]) * scale).astype(BF16)
    _store_variants(k_ref, rope(z[:, ATTN_WIDTH:ATTN_WIDTH + KV_WIDTH]))
    _store_variants(v_ref, z[:, ATTN_WIDTH + KV_WIDTH:ATTN_WIDTH + 2 * KV_WIDTH])
    p_ref[0] = z[:, ATTN_WIDTH + 2 * KV_WIDTH:]


def _inproj(x, mod, g1, w_in_bf, cos_t, sin_t):
    b, s, d = x.shape
    tm = TM_PROJ
    return pl.pallas_call(
        _inproj_kernel,
        out_shape=(jax.ShapeDtypeStruct((b, s, ATTN_WIDTH), BF16),
                   jax.ShapeDtypeStruct((b, s, 4 * LANES), BF16),
                   jax.ShapeDtypeStruct((b, s, 4 * LANES), BF16),
                   jax.ShapeDtypeStruct((b, s, POOL_WIDTH), F32)),
        grid=(s // tm, b),
        in_specs=[pl.BlockSpec((1, tm, d), lambda n, bi: (bi, n, 0)),
                  pl.BlockSpec((1, 6, d), lambda n, bi: (bi, 0, 0)),
                  pl.BlockSpec((1, d), lambda n, bi: (0, 0)),
                  pl.BlockSpec((d, IN_COLS), lambda n, bi: (0, 0)),
                  pl.BlockSpec((tm, LANES), lambda n, bi: (n, 0)),
                  pl.BlockSpec((tm, LANES), lambda n, bi: (n, 0))],
        out_specs=(pl.BlockSpec((1, tm, ATTN_WIDTH), lambda n, bi: (bi, n, 0)),
                   pl.BlockSpec((1, tm, 4 * LANES), lambda n, bi: (bi, n, 0)),
                   pl.BlockSpec((1, tm, 4 * LANES), lambda n, bi: (bi, n, 0)),
                   pl.BlockSpec((1, tm, POOL_WIDTH), lambda n, bi: (bi, n, 0))),
        compiler_params=pltpu.CompilerParams(vmem_limit_bytes=VMEM_LIMIT),
        name="inproj",
    )(x, mod, g1, w_in_bf, cos_t, sin_t)


def _ctxproj_kernel(x_ref, mod_ref, g_ref, w_ref, k_ref, v_ref):
    h = _norm_mod(x_ref[0], g_ref[...], mod_ref[0, 0:1, :], mod_ref[0, 1:2, :])
    z = _dot(h.astype(BF16), w_ref[...])
    _store_variants(k_ref, z[:, :KV_WIDTH])
    _store_variants(v_ref, z[:, KV_WIDTH:])


def _ctxproj(ctx, mod_c, g1, w_kv_bf):
    b, c, d = ctx.shape
    return pl.pallas_call(
        _ctxproj_kernel,
        out_shape=(jax.ShapeDtypeStruct((b, c, 4 * LANES), BF16),
                   jax.ShapeDtypeStruct((b, c, 4 * LANES), BF16)),
        grid=(b,),
        in_specs=[pl.BlockSpec((1, c, d), lambda bi: (bi, 0, 0)),
                  pl.BlockSpec((1, 6, d), lambda bi: (0, 0, 0)),
                  pl.BlockSpec((1, d), lambda bi: (0, 0)),
                  pl.BlockSpec((d, 2 * KV_WIDTH), lambda bi: (0, 0))],
        out_specs=(pl.BlockSpec((1, c, 4 * LANES), lambda bi: (bi, 0, 0)),
                   pl.BlockSpec((1, c, 4 * LANES), lambda bi: (bi, 0, 0))),
        compiler_params=pltpu.CompilerParams(vmem_limit_bytes=VMEM_LIMIT),
        name="ctxproj",
    )(ctx, mod_c, g1, w_kv_bf)


def _fold(op, tiles):
    while len(tiles) > 1:
        tiles = [op(tiles[i], tiles[i + 1]) if i + 1 < len(tiles) else tiles[i] for i in range(0, len(tiles), 2)]
    return tiles[0]


def _stack_variants(t4, kv):
    return jnp.concatenate([t4[:, (2 * kv) * LANES:(2 * kv + 1) * LANES],
                            t4[:, (2 * kv + 1) * LANES:(2 * kv + 2) * LANES]], axis=0)


def _attn_kernel(seq_len, sink_ref, q_ref, k_ref, kp_ref, kn_ref, v_ref, vp_ref, vn_ref, kc_ref, vc_ref,
                 p_ref, pp_ref, pn_ref, x_ref, mod_ref, band_ref, poolw_ref, pscale_ref, wout_ref,
                 g2_ref, wrh_ref, wrl_ref, x1_ref, h2_ref, lg_ref, kwin, vwin, pext, mix, s_scr, p_scr, m_scr):
    n = pl.program_id(1)
    n_last = pl.num_programs(1) - 1

    kwin[0:QB, :] = kp_ref[0]
    kwin[QB:QB + TQ, :] = k_ref[0]
    kwin[QB + TQ:, :] = kn_ref[0]
    vwin[0:QB, :] = vp_ref[0]
    vwin[QB:QB + TQ, :] = v_ref[0]
    vwin[QB + TQ:, :] = vn_ref[0]

    pext[0:QB - 8, :] = jnp.zeros((QB - 8, POOL_WIDTH), F32)
    pext[QB - 8:QB, :] = jnp.where(n > 0, pp_ref[0], 0.0)
    pext[QB:QB + TQ, :] = p_ref[0]
    pext[QB + TQ:QB + TQ + 8, :] = jnp.where(n < n_last, pn_ref[0], 0.0)
    pext[QB + TQ + 8:, :] = jnp.zeros((QB - 8, POOL_WIDTH), F32)

    row = lax.broadcasted_iota(jnp.int32, (QB, 3 * QB), 0)
    col = lax.broadcasted_iota(jnp.int32, (QB, 3 * QB), 1)
    in_band = (col >= row) & (col <= row + 2 * WINDOW)
    tok = lax.broadcasted_iota(jnp.int32, (QB, 1), 0)
    kc = kc_ref[0]
    kc_rows = [_stack_variants(kc, kv) for kv in range(N_KV_HEADS)]
    vc_rows = [_stack_variants(vc_ref[0], kv) for kv in range(N_KV_HEADS)]

    def sub_block(j, carry):
        r0 = pl.multiple_of(j * QB, QB)
        qj = q_ref[0, pl.ds(r0, QB), :]
        kw = kwin[pl.ds(r0, 3 * QB), :]
        vw = vwin[pl.ds(r0, 3 * QB), :]
        kpos = col + (n * TQ + j * QB - QB)
        ok = in_band & (kpos >= 0) & (kpos < seq_len)
        bias = jnp.where(ok, 0.0, -jnp.inf)
        n_loc, n_ctx = 3 * QB, kc.shape[0]
        bias2 = jnp.concatenate([bias, bias], axis=1)
        k_rows = [_stack_variants(kw, kv) for kv in range(N_KV_HEADS)]
        v_rows = [_stack_variants(vw, kv) for kv in range(N_KV_HEADS)]
        group = N_HEADS // N_KV_HEADS

        def head_tiles(head):
            loc0 = (head % 2) * n_loc
            ctx0 = 2 * n_loc + (head % 2) * n_ctx
            return ([loc0 + i * LANES for i in range(n_loc // LANES)]
                    + [ctx0 + i * LANES for i in range(n_ctx // LANES)])

        for c in range(N_HEADS // 2):
            qc = qj[:, c * LANES:(c + 1) * LANES]
            s_scr[c, :, 0:2 * n_loc] = _dot_nt(qc, k_rows[2 * c // group]) + bias2
            s_scr[c, :, 2 * n_loc:] = _dot_nt(qc, kc_rows[2 * c // group])
        for head in range(N_HEADS):
            tiles = [s_scr[head // 2, :, st:st + LANES] for st in head_tiles(head)]
            row_max = jnp.max(_fold(jnp.maximum, tiles), axis=1, keepdims=True)
            m_scr[head] = jnp.broadcast_to(jnp.maximum(row_max, sink_ref[head] * LOG2E), (QB, LANES))
        for head in range(N_HEADS):
            m = m_scr[head]
            acc = None
            for st in head_tiles(head):
                p = jnp.exp2(s_scr[head // 2, :, st:st + LANES] - m)
                p_scr[head // 2, :, st:st + LANES] = p.astype(BF16)
                acc = p if acc is None else acc + p
            denom = (jnp.broadcast_to(jnp.sum(acc, axis=1, keepdims=True), (QB, LANES))
                     + jnp.exp2(sink_ref[head] * LOG2E - m))
            m_scr[head] = 1.0 / denom
        lane = lax.broadcasted_iota(jnp.int32, (QB, LANES), 1)
        for c in range(N_HEADS // 2):
            o = (_dot(p_scr[c, :, 0:2 * n_loc], v_rows[2 * c // group])
                 + _dot(p_scr[c, :, 2 * n_loc:], vc_rows[2 * c // group]))
            o = o * jnp.where(lane < HEAD_DIM, m_scr[2 * c], m_scr[2 * c + 1])
            mix[pl.ds(r0, QB), c * LANES:(c + 1) * LANES] = o.astype(BF16)

        slab = pext[pl.ds(pl.multiple_of(r0 + POOL_OFF, 8), POOL_SLAB), :]
        tpos = tok + (n * TQ + j * QB)
        for g, w in enumerate(POOL_WINDOWS):
            sg = slab[:, g * LANES:(g + 1) * LANES]
            hi, lo = _split_bf16(sg)
            band = band_ref[g]
            wsum = _dot(band, hi) + _dot(band, lo)
            cnt = (jnp.minimum(tpos - w // 2 + w, seq_len) - jnp.maximum(tpos - w // 2, 0)).astype(F32)
            dlt = wsum / cnt - sg[POOL_OFF:POOL_OFF + QB, :]
            y = _dot(dlt.astype(BF16), poolw_ref[g]) * pscale_ref[:, g * LANES:(g + 1) * LANES]
            mix[pl.ds(r0, QB), ATTN_WIDTH + g * LANES:ATTN_WIDTH + (g + 1) * LANES] = y.astype(BF16)
        return carry

    lax.fori_loop(0, TQ // QB, sub_block, 0)

    proj = _dot(mix[...], wout_ref[...])
    x1 = x_ref[0] + mod_ref[0, 2:3, :] * proj
    x1_ref[0] = x1
    h2 = _norm_mod(x1, g2_ref[...], mod_ref[0, 3:4, :], mod_ref[0, 4:5, :])
    h2_ref[...] = _pack_words(h2)
    h_hi, h_lo = _split_bf16(h2)
    wrh = wrh_ref[...]
    lg_ref[...] = _dot_nt(wrh, h_hi) + _dot_nt(wrh, h_lo) + _dot_nt(wrl_ref[...], h_hi)


def _attn(sink, q, k4, v4, kc4, vc4, p, x, mod, band, poolw_bf, pscale, wout_bf, g2, wr_hi, wr_lo):
    b, s, d = x.shape
    c = kc4.shape[1]
    nt = s // TQ
    hb = TQ // QB
    pb = TQ // 8
    kv_main = pl.BlockSpec((1, TQ, 4 * LANES), lambda bi, n: (bi, n, 0))
    kv_prev = pl.BlockSpec((1, QB, 4 * LANES), lambda bi, n: (bi, jnp.maximum(n * hb - 1, 0), 0))
    kv_next = pl.BlockSpec((1, QB, 4 * LANES), lambda bi, n: (bi, jnp.minimum((n + 1) * hb, s // QB - 1), 0))
    const2 = lambda bi, n: (0, 0)
    const3 = lambda bi, n: (0, 0, 0)
    return pl.pallas_call(
        functools.partial(_attn_kernel, s),
        out_shape=(jax.ShapeDtypeStruct((b, s, d), F32),
                   jax.ShapeDtypeStruct((b * s, d // 2), U32),
                   jax.ShapeDtypeStruct((N_EXPERTS, b * s), F32)),
        grid=(b, nt),
        in_specs=[pl.BlockSpec(memory_space=pltpu.SMEM),
                  pl.BlockSpec((1, TQ, ATTN_WIDTH), lambda bi, n: (bi, n, 0)),
                  kv_main, kv_prev, kv_next, kv_main, kv_prev, kv_next,
                  pl.BlockSpec((1, c, 4 * LANES), lambda bi, n: (bi, 0, 0)),
                  pl.BlockSpec((1, c, 4 * LANES), lambda bi, n: (bi, 0, 0)),
                  pl.BlockSpec((1, TQ, POOL_WIDTH), lambda bi, n: (bi, n, 0)),
                  pl.BlockSpec((1, 8, POOL_WIDTH), lambda bi, n: (bi, jnp.maximum(n * pb - 1, 0), 0)),
                  pl.BlockSpec((1, 8, POOL_WIDTH), lambda bi, n: (bi, jnp.minimum((n + 1) * pb, s // 8 - 1), 0)),
                  pl.BlockSpec((1, TQ, d), lambda bi, n: (bi, n, 0)),
                  pl.BlockSpec((1, 6, d), lambda bi, n: (bi, 0, 0)),
                  pl.BlockSpec((len(POOL_WINDOWS), QB, POOL_SLAB), const3),
                  pl.BlockSpec((len(POOL_WINDOWS), POOL_GROUP_DIM, POOL_GROUP_DIM), const3),
                  pl.BlockSpec((1, POOL_WIDTH), const2),
                  pl.BlockSpec((d, d), const2),
                  pl.BlockSpec((1, d), const2),
                  pl.BlockSpec((N_EXPERTS, d), const2),
                  pl.BlockSpec((N_EXPERTS, d), const2)],
        out_specs=(pl.BlockSpec((1, TQ, d), lambda bi, n: (bi, n, 0)),
                   pl.BlockSpec((TQ, d // 2), lambda bi, n: (bi * nt + n, 0)),
                   pl.BlockSpec((N_EXPERTS, TQ), lambda bi, n: (0, bi * nt + n))),
        scratch_shapes=[pltpu.VMEM((TQ + 2 * QB, 4 * LANES), BF16),
                        pltpu.VMEM((TQ + 2 * QB, 4 * LANES), BF16),
                        pltpu.VMEM((TQ + 2 * QB, POOL_WIDTH), F32),
                        pltpu.VMEM((TQ, d), BF16),
                        pltpu.VMEM((N_HEADS // 2, QB, 2 * (3 * QB + c)), F32),
                        pltpu.VMEM((N_HEADS // 2, QB, 2 * (3 * QB + c)), BF16),
                        pltpu.VMEM((N_HEADS, QB, LANES), F32)],
        compiler_params=pltpu.CompilerParams(vmem_limit_bytes=VMEM_LIMIT),
        name="attn",
    )(sink, q, k4, k4, k4, v4, v4, v4, kc4, vc4, p, p, p, x, mod, band, poolw_bf, pscale, wout_bf,
      g2, wr_hi, wr_lo)


def _first_argmax_rows(v, row_iota, n_rows):
    m = jnp.max(v, axis=0, keepdims=True)
    idx = jnp.min(jnp.where(v == m, row_iota, n_rows), axis=0, keepdims=True)
    return m, idx


def _route_kernel(lg_ref, bias_ref, tri_ref, idx_ref, gate_ref, rank_ref, cnt_ref, carry):
    i = pl.program_id(0)

    @pl.when(i == 0)
    def _():
        carry[...] = jnp.zeros_like(carry)

    scores = 1.0 / (1.0 + jnp.exp(-lg_ref[...]))
    biased = scores + bias_ref[...]
    e_iota = lax.broadcasted_iota(jnp.int32, scores.shape, 0).astype(F32)
    g_iota = lax.broadcasted_iota(jnp.int32, (EXPERTS_PER_GROUP, TR), 0).astype(F32)
    neg = -jnp.inf

    grp = []
    for g in range(N_EXPERT_GROUPS):
        blk = biased[g * EXPERTS_PER_GROUP:(g + 1) * EXPERTS_PER_GROUP, :]
        m1, i1 = _first_argmax_rows(blk, g_iota, float(EXPERTS_PER_GROUP))
        m2 = jnp.max(jnp.where(g_iota == i1, neg, blk), axis=0, keepdims=True)
        grp.append(m1 + m2)
    grp = jnp.concatenate(grp, axis=0)
    gg_iota = lax.broadcasted_iota(jnp.int32, grp.shape, 0).astype(F32)
    grp_sel = jnp.zeros(grp.shape, F32)
    for _ in range(TOPK_GROUPS):
        _, gi = _first_argmax_rows(grp, gg_iota, float(N_EXPERT_GROUPS))
        hit = gg_iota == gi
        grp_sel = jnp.where(hit, 1.0, grp_sel)
        grp = jnp.where(hit, neg, grp)
    allowed = jnp.concatenate(
        [jnp.broadcast_to(grp_sel[g:g + 1, :], (EXPERTS_PER_GROUP, TR)) for g in range(N_EXPERT_GROUPS)], axis=0)
    masked = jnp.where(allowed > 0.5, biased, neg)

    idxs, gates = [], []
    onehot = jnp.zeros(scores.shape, F32)
    for _ in range(TOP_K):
        _, ei = _first_argmax_rows(masked, e_iota, float(N_EXPERTS))
        hit = e_iota == ei
        idxs.append(ei)
        gates.append(jnp.sum(jnp.where(hit, scores, 0.0), axis=0, keepdims=True))
        onehot = jnp.where(hit, 1.0, onehot)
        masked = jnp.where(hit, neg, masked)
    idx = jnp.concatenate(idxs, axis=0)
    gate = jnp.concatenate(gates, axis=0)
    gate = gate / jnp.sum(gate, axis=0, keepdims=True) * ROUTED_SCALE

    before = _dot(onehot.astype(BF16), tri_ref[...]) + carry[:, 0:1]
    ranks = [jnp.sum(jnp.where(e_iota == idxs[k], before, 0.0), axis=0, keepdims=True) for k in range(TOP_K)]
    idx_ref[...] = idx.astype(jnp.int32)
    gate_ref[...] = gate
    rank_ref[...] = jnp.concatenate(ranks, axis=0).astype(jnp.int32)
    total = carry[...] + jnp.sum(onehot, axis=1, keepdims=True)
    carry[...] = total
    cnt_ref[...] = total


def _route(lg_t, bias, tri):
    e, t = lg_t.shape
    tok = pl.BlockSpec((TOP_K, TR), lambda i: (0, i))
    return pl.pallas_call(
        _route_kernel,
        out_shape=(jax.ShapeDtypeStruct((TOP_K, t), jnp.int32),
                   jax.ShapeDtypeStruct((TOP_K, t), F32),
                   jax.ShapeDtypeStruct((TOP_K, t), jnp.int32),
                   jax.ShapeDtypeStruct((e, LANES), F32)),
        grid=(t // TR,),
        in_specs=[pl.BlockSpec((e, TR), lambda i: (0, i)),
                  pl.BlockSpec((e, 1), lambda i: (0, 0)),
                  pl.BlockSpec((TR, TR), lambda i: (0, 0))],
        out_specs=(tok, tok, tok, pl.BlockSpec((e, LANES), lambda i: (0, 0))),
        scratch_shapes=[pltpu.VMEM((e, LANES), F32)],
        compiler_params=pltpu.CompilerParams(vmem_limit_bytes=VMEM_LIMIT),
        name="route",
    )(lg_t, bias, tri)


def _plan_kernel(n_blocks, size_ref, start_ref, expert_ref, valid_ref, nact_ref):
    def per_expert(e, first_block):
        size = size_ref[e]
        n_blk = (size + TM_EXP - 1) // TM_EXP
        start_ref[e] = first_block * TM_EXP

        def per_block(j, carry):
            expert_ref[first_block + j] = e
            valid_ref[first_block + j] = jnp.minimum(size - j * TM_EXP, TM_EXP)
            return carry

        lax.fori_loop(0, n_blk, per_block, 0)
        return first_block + n_blk

    n_active = lax.fori_loop(0, N_EXPERTS, per_expert, 0)
    nact_ref[0] = n_active

    def unused(i, carry):
        expert_ref[i] = N_EXPERTS - 1
        valid_ref[i] = 0
        return carry

    lax.fori_loop(n_active, n_blocks, unused, 0)


def _plan_blocks(sizes, n_blocks):
    smem = pl.BlockSpec(memory_space=pltpu.SMEM)
    return pl.pallas_call(
        functools.partial(_plan_kernel, n_blocks),
        out_shape=(jax.ShapeDtypeStruct((N_EXPERTS,), jnp.int32),
                   jax.ShapeDtypeStruct((n_blocks,), jnp.int32),
                   jax.ShapeDtypeStruct((n_blocks,), jnp.int32),
                   jax.ShapeDtypeStruct((1,), jnp.int32)),
        in_specs=[smem],
        out_specs=(smem, smem, smem, smem),
        name="plan_blocks",
    )(sizes)


def _dest_kernel(start_ref, idx_ref, rank_ref, dest_ref):
    idx = idx_ref[...]
    dest = rank_ref[...]
    for e in range(N_EXPERTS):
        dest = dest + jnp.where(idx == e, start_ref[e], 0)
    dest_ref[...] = dest


def _dest_rows(pad_start, idx_kt, rank_kt):
    n_k, t = idx_kt.shape
    tile = 4096
    blk = pl.BlockSpec((n_k, tile), lambda i: (0, i))
    return pl.pallas_call(
        _dest_kernel,
        out_shape=jax.ShapeDtypeStruct((n_k, t), jnp.int32),
        grid=(t // tile,),
        in_specs=[pl.BlockSpec(memory_space=pltpu.SMEM), blk, blk],
        out_specs=blk,
        name="dest_rows",
    )(pad_start, idx_kt, rank_kt)


def _sc_mesh():
    return plsc.VectorSubcoreMesh(core_axis_name="c", subcore_axis_name="s")


def _sc_token_base(steps, j):
    worker = lax.axis_index("s") * SC_CORES + lax.axis_index("c")
    return (worker * steps + j) * SC_TOKENS


def _sc_scatter(dest_kt, h2p, n_rows):
    t, width = h2p.shape
    steps = t // (SC_WORKERS * SC_TOKENS)

    @functools.partial(
        pl.kernel, mesh=_sc_mesh(),
        out_type=jax.ShapeDtypeStruct((n_rows, width), U32),
        scratch_types=[pltpu.VMEM((TOP_K, SC_TOKENS), jnp.int32),
                       pltpu.VMEM((SC_TOKENS, width), U32),
                       pltpu.SemaphoreType.DMA],
        name="sc_scatter",
    )
    def body(dest_hbm, h_hbm, xs_hbm, idx_v, rows_v, sem):
        @pl.loop(0, steps)
        def _(j):
            base = _sc_token_base(steps, j)
            pltpu.sync_copy(dest_hbm.at[:, pl.ds(base, SC_TOKENS)], idx_v)
            pltpu.sync_copy(h_hbm.at[pl.ds(base, SC_TOKENS)], rows_v)
            copies = [pltpu.async_copy(rows_v, xs_hbm.at[idx_v.at[k]], sem) for k in range(TOP_K)]
            for cp in copies:
                cp.wait()

    return body(dest_kt, h2p)


def _sc_gather(dest_kt, ys, token0, n_tokens):
    n_k = dest_kt.shape[0]
    width = ys.shape[1]
    steps = n_tokens // (SC_WORKERS * SC_TOKENS)

    half = SC_TOKENS // 2
    units = [(k, h) for k in range(n_k) for h in range(2)]
    n_buf = 3

    @functools.partial(
        pl.kernel, mesh=_sc_mesh(),
        out_type=jax.ShapeDtypeStruct((n_k, n_tokens, width), U32),
        scratch_types=[pltpu.VMEM((n_k, SC_TOKENS), jnp.int32),
                       pltpu.VMEM((n_buf, half, width), U32),
                       pltpu.SemaphoreType.DMA((n_buf,))],
        name="sc_gather",
    )
    def body(dest_hbm, ys_hbm, yk_hbm, idx_v, rows_v, sems):
        @pl.loop(0, steps)
        def _(j):
            base = _sc_token_base(steps, j)
            pltpu.sync_copy(dest_hbm.at[:, pl.ds(token0 + base, SC_TOKENS)], idx_v)

            def gather(u):
                k, h = units[u]
                slot = u % n_buf
                return pltpu.make_async_copy(ys_hbm.at[idx_v.at[k, pl.ds(h * half, half)]], rows_v.at[slot],
                                             sems.at[slot])

            gather(0).start()
            gather(1).start()
            for u, (k, h) in enumerate(units):
                gather(u).wait()
                pltpu.sync_copy(rows_v.at[u % n_buf], yk_hbm.at[k, pl.ds(base + h * half, half)])
                if u + 2 < len(units):
                    gather(u + 2).start()

    return body(dest_kt, ys)


def _experts_kernel(be_ref, valid_ref, nact_ref, xs_ref, wg_ref, wu_ref, wd_ref, ys_ref, wg_bf, wu_bf, wd_bf):
    i = pl.program_id(0)
    prev = be_ref[jnp.maximum(i - 1, 0)]

    @pl.when((i == 0) | (be_ref[i] != prev))
    def _():
        wg_bf[...] = wg_ref[0].astype(BF16)
        wu_bf[...] = wu_ref[0].astype(BF16)
        wd_bf[...] = wd_ref[0].astype(BF16)

    @pl.when(i < nact_ref[0])
    def _():
        row = lax.broadcasted_iota(jnp.int32, xs_ref.shape, 0)
        words = jnp.where(row < valid_ref[i], xs_ref[...], jnp.uint32(0))
        xb = _unpack_words(words).astype(BF16)
        hid = _silu(_dot(xb, wg_bf[...])) * _dot(xb, wu_bf[...])
        ys_ref[...] = _pack_words(_dot(hid.astype(BF16), wd_bf[...]))

    @pl.when(i >= nact_ref[0])
    def _():
        ys_ref[...] = jnp.zeros_like(ys_ref)


def _experts(block_e, block_valid, n_active, xs, w_gate, w_up, w_down):
    n_rows, half = xs.shape
    d = w_gate.shape[1]
    nb = n_rows // TM_EXP
    grid_spec = pltpu.PrefetchScalarGridSpec(
        num_scalar_prefetch=3,
        grid=(nb,),
        in_specs=[pl.BlockSpec((TM_EXP, half), lambda i, be, bv, na: (jnp.minimum(i, na[0] - 1), 0)),
                  pl.BlockSpec((1, d, D_EXPERT), lambda i, be, bv, na: (be[i], 0, 0)),
                  pl.BlockSpec((1, d, D_EXPERT), lambda i, be, bv, na: (be[i], 0, 0)),
                  pl.BlockSpec((1, D_EXPERT, d), lambda i, be, bv, na: (be[i], 0, 0))],
        out_specs=pl.BlockSpec((TM_EXP, half), lambda i, be, bv, na: (i, 0)),
        scratch_shapes=[pltpu.VMEM((d, D_EXPERT), BF16), pltpu.VMEM((d, D_EXPERT), BF16),
                        pltpu.VMEM((D_EXPERT, d), BF16)],
    )
    return pl.pallas_call(
        _experts_kernel,
        out_shape=jax.ShapeDtypeStruct((n_rows, half), U32),
        grid_spec=grid_spec,
        compiler_params=pltpu.CompilerParams(vmem_limit_bytes=VMEM_LIMIT),
        name="experts",
    )(block_e, block_valid, n_active, xs, w_gate, w_up, w_down)


def _combine_kernel(yk_ref, x1_ref, h2_ref, gate_ref, mod_ref, wsg_ref, wsu_ref, wsd_ref, fg_ref, *out_refs):
    out_ref = out_refs[-1]
    hb = _unpack_words(h2_ref[...]).astype(BF16)
    hid = _silu(_dot(hb, wsg_ref[...])) * _dot(hb, wsu_ref[...])
    ffn = _dot(hid.astype(BF16), wsd_ref[...])
    gate = gate_ref[...]
    for k in range(TOP_K):
        ffn = ffn + gate[:, k:k + 1] * _unpack_words(yk_ref[k])
    x2 = x1_ref[...] + mod_ref[0, 5:6, :] * ffn
    ms = jnp.mean(x2 * x2, axis=-1, keepdims=True)
    out_ref[...] = x2 * lax.rsqrt(ms + EPS) * fg_ref[...]


def _combine(yk, token0, prev_out, x1, h2p, gate_tk, mod, wsg_bf, wsu_bf, wsd_bf, final_g, seq_len):
    t, d = x1.shape
    tiles_per_seq = seq_len // TF
    tile0 = token0 // TF
    tok = pl.BlockSpec((TF, d), lambda i: (tile0 + i, 0))
    in_specs = [pl.BlockSpec((TOP_K, TF, d // 2), lambda i: (0, i, 0)),
                tok, pl.BlockSpec((TF, d // 2), lambda i: (tile0 + i, 0)),
                pl.BlockSpec((TF, TOP_K), lambda i: (tile0 + i, 0)),
                pl.BlockSpec((1, 6, d), lambda i: ((tile0 + i) // tiles_per_seq, 0, 0)),
                pl.BlockSpec((d, D_SHARED), lambda i: (0, 0)),
                pl.BlockSpec((d, D_SHARED), lambda i: (0, 0)),
                pl.BlockSpec((D_SHARED, d), lambda i: (0, 0)),
                pl.BlockSpec((1, d), lambda i: (0, 0))]
    args = [yk, x1, h2p, gate_tk, mod, wsg_bf, wsu_bf, wsd_bf, final_g]
    aliases = {}
    if prev_out is not None:
        in_specs.append(pl.BlockSpec(memory_space=pl.ANY))
        args.append(prev_out)
        aliases = {len(args) - 1: 0}
    return pl.pallas_call(
        _combine_kernel,
        out_shape=jax.ShapeDtypeStruct((t, d), F32),
        grid=(yk.shape[1] // TF,),
        in_specs=in_specs,
        out_specs=tok,
        input_output_aliases=aliases,
        compiler_params=pltpu.CompilerParams(vmem_limit_bytes=VMEM_LIMIT),
        name="combine",
    )(*args)


def _rope_tables(n_tokens):
    n_rows = n_tokens // GRID_W
    n_freq = HEAD_DIM // 4
    inv_freq = ROPE_THETA ** (-jnp.arange(n_freq, dtype=F32) / n_freq)
    ang_r = jnp.arange(n_rows).astype(F32)[:, None] * inv_freq[None, :]
    ang_c = jnp.arange(GRID_W).astype(F32)[:, None] * inv_freq[None, :]

    def per_token(row_part, col_part):
        rows = jnp.broadcast_to(row_part[:, None, :], (n_rows, GRID_W, n_freq))
        cols = jnp.broadcast_to(col_part[None, :, :], (n_rows, GRID_W, n_freq))
        return rows.reshape(n_tokens, n_freq), cols.reshape(n_tokens, n_freq)

    cos_r, cos_c = per_token(jnp.cos(ang_r), jnp.cos(ang_c))
    sin_r, sin_c = per_token(jnp.sin(ang_r), jnp.sin(ang_c))
    cos = jnp.concatenate([cos_r, cos_r, cos_c, cos_c], axis=1)
    sin = jnp.concatenate([-sin_r, sin_r, -sin_c, sin_c], axis=1)
    reps = LANES // HEAD_DIM
    return jnp.tile(cos, (1, reps)), jnp.tile(sin, (1, reps))


def _pool_bands():
    i = jnp.arange(QB)[:, None]
    r = jnp.arange(POOL_SLAB)[None, :]
    return jnp.stack([((r >= i + POOL_OFF - w // 2) & (r < i + POOL_OFF + w // 2)).astype(BF16)
                      for w in POOL_WINDOWS])


def kernel(x, c, ctx, c_ctx, w_ada, b_ada, norm1_g, norm2_g, w_in, attn_sink, pool_w, pool_scale, w_out,
           w_router, router_bias, w_gate, w_up, w_down, ws_gate, ws_up, ws_down, final_g):
    b, s, d = x.shape
    t = b * s
    assert w_ada.shape[0] == 1 and d == D_MODEL and s % TQ == 0 and b + 1 <= 8

    c8 = jnp.zeros((8, d), F32).at[:b].set(c).at[b].set(c_ctx)
    mod = _ada(c8, w_ada[0], b_ada[0]).reshape(8, 6, d)
    g1 = norm1_g[0].reshape(1, d)
    g2 = norm2_g[0].reshape(1, d)
    w_in_bf = w_in[0].astype(BF16)
    cos_t, sin_t = _rope_tables(s)

    q, k4, v4, p = _inproj(x, mod, g1, w_in_bf, cos_t, sin_t)
    kc4, vc4 = _ctxproj(ctx, mod[b:b + 1], g1, w_in_bf[:, ATTN_WIDTH:ATTN_WIDTH + 2 * KV_WIDTH])

    wr_t = w_router[0].T
    wr_hi = wr_t.astype(BF16)
    wr_lo = (wr_t - wr_hi.astype(F32)).astype(BF16)
    x1, h2p, lg_t = _attn(attn_sink[0], q, k4, v4, kc4, vc4, p, x, mod, _pool_bands(),
                         pool_w[0].astype(BF16), pool_scale[0].reshape(1, POOL_WIDTH), w_out[0].astype(BF16),
                         g2, wr_hi, wr_lo)

    tri = jnp.triu(jnp.ones((TR, TR), BF16), k=1)
    idx_kt, gate_kt, rank_kt, counts = _route(lg_t, router_bias[0].reshape(N_EXPERTS, 1), tri)

    n_rows = -(-(t * TOP_K + N_EXPERTS * (TM_EXP - 1)) // TM_EXP) * TM_EXP
    pad_start, block_e, block_valid, n_active = _plan_blocks(counts[:, 0].astype(jnp.int32), n_rows // TM_EXP)
    dest_kt = _dest_rows(pad_start, idx_kt, rank_kt)

    xs = _sc_scatter(dest_kt, h2p, n_rows)
    ys = _experts(block_e, block_valid, n_active, xs, w_gate[0], w_up[0], w_down[0])
    chunk = t // COMBINE_CHUNKS
    shared_w = (ws_gate[0].astype(BF16), ws_up[0].astype(BF16), ws_down[0].astype(BF16))
    gate_tk = gate_kt.T
    out = None
    for token0 in range(0, t, chunk):
        out = _combine(_sc_gather(dest_kt, ys, token0, chunk), token0, out, x1.reshape(t, d), h2p, gate_tk, mod,
                       *shared_w, final_g.reshape(1, d), s)
    return out.reshape(b, s, d)
```

```python
import functools

import jax
import jax.numpy as jnp
from jax import lax
from jax.experimental import pallas as pl
from jax.experimental.pallas import tpu as pltpu
from jax.experimental.pallas import tpu_sc as plsc

F32 = jnp.float32
BF16 = jnp.bfloat16

D_MODEL = 1024
GRID_W = 64
N_HEADS = 8
N_KV_HEADS = 2
HEAD_DIM = 64
ATTN_WIDTH = N_HEADS * HEAD_DIM
KV_WIDTH = N_KV_HEADS * HEAD_DIM
WINDOW = 128
ROPE_THETA = 10000.0
POOL_WINDOWS = (2, 4, 8, 16)
POOL_WIDTH = D_MODEL - ATTN_WIDTH
POOL_GROUP_DIM = POOL_WIDTH // len(POOL_WINDOWS)
IN_COLS = ATTN_WIDTH + 2 * KV_WIDTH + POOL_WIDTH
N_EXPERTS = 64
TOP_K = 8
N_EXPERT_GROUPS = 8
EXPERTS_PER_GROUP = N_EXPERTS // N_EXPERT_GROUPS
TOPK_GROUPS = 4
D_EXPERT = 256
D_SHARED = 256
ROUTED_SCALE = 2.5
EPS = 1e-6
LOG2E = 1.4426950408889634

LANES = 128
U32 = jnp.uint32
VMEM_LIMIT = 48 * 1024 * 1024

TM_PROJ = 512
TQ = 512
QB = 128
POOL_SLAB = 256
POOL_OFF = 64
TR = 512
TM_EXP = 512
TF = 256
TOKEN_GROUPS = 2
COMBINE_CHUNKS = 2
SC_CORES = 2
SC_WORKERS = 32
SC_TOKENS = 128


def _silu(x):
    return x * (1.0 / (1.0 + jnp.exp(-x)))


def _split_bf16(x):
    hi = x.astype(BF16)
    lo = (x - hi.astype(F32)).astype(BF16)
    return hi, lo


def _dot(a, b):
    return jnp.dot(a, b, preferred_element_type=F32)


def _pack_words(val):
    half = val.shape[1] // 2
    lo = lax.bitcast_convert_type(val[:, :half].astype(BF16).astype(F32), U32)
    hi = lax.bitcast_convert_type(val[:, half:].astype(BF16).astype(F32), U32)
    return lax.shift_right_logical(lo, jnp.uint32(16)) | hi


def _unpack_words(words):
    lo = lax.bitcast_convert_type(lax.shift_left(words, jnp.uint32(16)), F32)
    hi = lax.bitcast_convert_type(words & jnp.uint32(0xFFFF0000), F32)
    return jnp.concatenate([lo, hi], axis=1)


def _dot_nt(a, b):
    return lax.dot_general(a, b, (((1,), (1,)), ((), ())), preferred_element_type=F32)


def _ada_kernel(c_ref, w_ref, b_ref, o_ref):
    a_hi, a_lo = _split_bf16(_silu(c_ref[...]))
    w_hi, w_lo = _split_bf16(w_ref[...])
    o_ref[...] = _dot(a_hi, w_hi) + _dot(a_lo, w_hi) + _dot(a_hi, w_lo) + b_ref[...]


def _ada(c8, w_ada, b_ada):
    d = c8.shape[1]
    n = w_ada.shape[1]
    tn = 512
    return pl.pallas_call(
        _ada_kernel,
        out_shape=jax.ShapeDtypeStruct((8, n), F32),
        grid=(n // tn,),
        in_specs=[pl.BlockSpec((8, d), lambda j: (0, 0)),
                  pl.BlockSpec((d, tn), lambda j: (0, j)),
                  pl.BlockSpec((1, tn), lambda j: (0, j))],
        out_specs=pl.BlockSpec((8, tn), lambda j: (0, j)),
        compiler_params=pltpu.CompilerParams(vmem_limit_bytes=VMEM_LIMIT),
        name="ada",
    )(c8, w_ada, b_ada.reshape(1, n))


def _norm_mod(x, g, shift, scale):
    ms = jnp.mean(x * x, axis=-1, keepdims=True)
    return (x * lax.rsqrt(ms + EPS) * g) * (1.0 + scale) + shift


def _lane_variants(t):
    lane = lax.broadcasted_iota(jnp.int32, t.shape, 1)
    lo = lane < HEAD_DIM
    tr = pltpu.roll(t, HEAD_DIM, 1)
    zero = jnp.zeros_like(t)
    return (jnp.where(lo, t, zero), jnp.where(lo, zero, tr),
            jnp.where(lo, tr, zero), jnp.where(lo, zero, t))


def _store_variants(ref, t):
    for i, var in enumerate(_lane_variants(t)):
        ref[0, :, i * LANES:(i + 1) * LANES] = var.astype(BF16)


def _inproj_kernel(x_ref, mod_ref, g_ref, w_ref, cos_ref, sin_ref, q_ref, k_ref, v_ref, p_ref):
    h = _norm_mod(x_ref[0], g_ref[...], mod_ref[0, 0:1, :], mod_ref[0, 1:2, :])
    z = _dot(h.astype(BF16), w_ref[...])
    cos = cos_ref[...]
    sin = sin_ref[...]
    lane = lax.broadcasted_iota(jnp.int32, cos.shape, 1)
    first_half = (lane & 16) == 0

    def rope(zc):
        partner = jnp.where(first_half, pltpu.roll(zc, LANES - 16, 1), pltpu.roll(zc, 16, 1))
        return zc * cos + partner * sin

    scale = HEAD_DIM ** -0.5 * LOG2E
    for c in range(ATTN_WIDTH // LANES):
        q_ref[0, :, c * LANES:(c + 1) * LANES] = (rope(z[:, c * LANES:(c + 1) * LANES]) * scale).astype(BF16)
    _store_variants(k_ref, rope(z[:, ATTN_WIDTH:ATTN_WIDTH + KV_WIDTH]))
    _store_variants(v_ref, z[:, ATTN_WIDTH + KV_WIDTH:ATTN_WIDTH + 2 * KV_WIDTH])
    p_ref[0] = z[:, ATTN_WIDTH + 2 * KV_WIDTH:]


def _inproj(x, mod, g1, w_in_bf, cos_t, sin_t):
    b, s, d = x.shape
    tm = TM_PROJ
    return pl.pallas_call(
        _inproj_kernel,
        out_shape=(jax.ShapeDtypeStruct((b, s, ATTN_WIDTH), BF16),
                   jax.ShapeDtypeStruct((b, s, 4 * LANES), BF16),
                   jax.ShapeDtypeStruct((b, s, 4 * LANES), BF16),
                   jax.ShapeDtypeStruct((b, s, POOL_WIDTH), F32)),
        grid=(s // tm, b),
        in_specs=[pl.BlockSpec((1, tm, d), lambda n, bi: (bi, n, 0)),
                  pl.BlockSpec((1, 6, d), lambda n, bi: (bi, 0, 0)),
                  pl.BlockSpec((1, d), lambda n, bi: (0, 0)),
                  pl.BlockSpec((d, IN_COLS), lambda n, bi: (0, 0)),
                  pl.BlockSpec((tm, LANES), lambda n, bi: (n, 0)),
                  pl.BlockSpec((tm, LANES), lambda n, bi: (n, 0))],
        out_specs=(pl.BlockSpec((1, tm, ATTN_WIDTH), lambda n, bi: (bi, n, 0)),
                   pl.BlockSpec((1, tm, 4 * LANES), lambda n, bi: (bi, n, 0)),
                   pl.BlockSpec((1, tm, 4 * LANES), lambda n, bi: (bi, n, 0)),
                   pl.BlockSpec((1, tm, POOL_WIDTH), lambda n, bi: (bi, n, 0))),
        compiler_params=pltpu.CompilerParams(vmem_limit_bytes=VMEM_LIMIT),
        name="inproj",
    )(x, mod, g1, w_in_bf, cos_t, sin_t)


def _ctxproj_kernel(x_ref, mod_ref, g_ref, w_ref, k_ref, v_ref):
    h = _norm_mod(x_ref[0], g_ref[...], mod_ref[0, 0:1, :], mod_ref[0, 1:2, :])
    z = _dot(h.astype(BF16), w_ref[...])
    _store_variants(k_ref, z[:, :KV_WIDTH])
    _store_variants(v_ref, z[:, KV_WIDTH:])


def _ctxproj(ctx, mod_c, g1, w_kv_bf):
    b, c, d = ctx.shape
    return pl.pallas_call(
        _ctxproj_kernel,
        out_shape=(jax.ShapeDtypeStruct((b, c, 4 * LANES), BF16),
                   jax.ShapeDtypeStruct((b, c, 4 * LANES), BF16)),
        grid=(b,),
        in_specs=[pl.BlockSpec((1, c, d), lambda bi: (bi, 0, 0)),
                  pl.BlockSpec((1, 6, d), lambda bi: (0, 0, 0)),
                  pl.BlockSpec((1, d), lambda bi: (0, 0)),
                  pl.BlockSpec((d, 2 * KV_WIDTH), lambda bi: (0, 0))],
        out_specs=(pl.BlockSpec((1, c, 4 * LANES), lambda bi: (bi, 0, 0)),
                   pl.BlockSpec((1, c, 4 * LANES), lambda bi: (bi, 0, 0))),
        compiler_params=pltpu.CompilerParams(vmem_limit_bytes=VMEM_LIMIT),
        name="ctxproj",
    )(ctx, mod_c, g1, w_kv_bf)


def _fold(op, tiles):
    while len(tiles) > 1:
        tiles = [op(tiles[i], tiles[i + 1]) if i + 1 < len(tiles) else tiles[i] for i in range(0, len(tiles), 2)]
    return tiles[0]


def _stack_variants(t4, kv):
    return jnp.concatenate([t4[:, (2 * kv) * LANES:(2 * kv + 1) * LANES],
                            t4[:, (2 * kv + 1) * LANES:(2 * kv + 2) * LANES]], axis=0)


def _attn_kernel(seq_len, sink_ref, q_ref, k_ref, kp_ref, kn_ref, v_ref, vp_ref, vn_ref, kc_ref, vc_ref,
                 p_ref, pp_ref, pn_ref, x_ref, mod_ref, band_ref, poolw_ref, pscale_ref, wout_ref,
                 g2_ref, wrh_ref, wrl_ref, x1_ref, h2_ref, lg_ref, kwin, vwin, pext, mix, s_scr, p_scr, m_scr):
    n = pl.program_id(1)
    n_last = pl.num_programs(1) - 1

    kwin[0:QB, :] = kp_ref[0]
    kwin[QB:QB + TQ, :] = k_ref[0]
    kwin[QB + TQ:, :] = kn_ref[0]
    vwin[0:QB, :] = vp_ref[0]
    vwin[QB:QB + TQ, :] = v_ref[0]
    vwin[QB + TQ:, :] = vn_ref[0]

    pext[0:QB - 8, :] = jnp.zeros((QB - 8, POOL_WIDTH), F32)
    pext[QB - 8:QB, :] = jnp.where(n > 0, pp_ref[0], 0.0)
    pext[QB:QB + TQ, :] = p_ref[0]
    pext[QB + TQ:QB + TQ + 8, :] = jnp.where(n < n_last, pn_ref[0], 0.0)
    pext[QB + TQ + 8:, :] = jnp.zeros((QB - 8, POOL_WIDTH), F32)

    row = lax.broadcasted_iota(jnp.int32, (QB, 3 * QB), 0)
    col = lax.broadcasted_iota(jnp.int32, (QB, 3 * QB), 1)
    in_band = (col >= row) & (col <= row + 2 * WINDOW)
    tok = lax.broadcasted_iota(jnp.int32, (QB, 1), 0)
    kc = kc_ref[0]
    kc_rows = [_stack_variants(kc, kv) for kv in range(N_KV_HEADS)]
    vc_rows = [_stack_variants(vc_ref[0], kv) for kv in range(N_KV_HEADS)]

    def sub_block(j, carry):
        r0 = pl.multiple_of(j * QB, QB)
        qj = q_ref[0, pl.ds(r0, QB), :]
        kw = kwin[pl.ds(r0, 3 * QB), :]
        vw = vwin[pl.ds(r0, 3 * QB), :]
        kpos = col + (n * TQ + j * QB - QB)
        ok = in_band & (kpos >= 0) & (kpos < seq_len)
        bias = jnp.where(ok, 0.0, -jnp.inf)
        n_loc, n_ctx = 3 * QB, kc.shape[0]
        bias2 = jnp.concatenate([bias, bias], axis=1)
        k_rows = [_stack_variants(kw, kv) for kv in range(N_KV_HEADS)]
        v_rows = [_stack_variants(vw, kv) for kv in range(N_KV_HEADS)]
        group = N_HEADS // N_KV_HEADS

        def head_tiles(head):
            loc0 = (head % 2) * n_loc
            ctx0 = 2 * n_loc + (head % 2) * n_ctx
            return ([loc0 + i * LANES for i in range(n_loc // LANES)]
                    + [ctx0 + i * LANES for i in range(n_ctx // LANES)])

        for c in range(N_HEADS // 2):
            qc = qj[:, c * LANES:(c + 1) * LANES]
            s_scr[c, :, 0:2 * n_loc] = _dot_nt(qc, k_rows[2 * c // group]) + bias2
            s_scr[c, :, 2 * n_loc:] = _dot_nt(qc, kc_rows[2 * c // group])
        for head in range(N_HEADS):
            tiles = [s_scr[head // 2, :, st:st + LANES] for st in head_tiles(head)]
            row_max = jnp.max(_fold(jnp.maximum, tiles), axis=1, keepdims=True)
            m_scr[head] = jnp.broadcast_to(jnp.maximum(row_max, sink_ref[head] * LOG2E), (QB, LANES))
        for head in range(N_HEADS):
            m = m_scr[head]
            acc = None
            for st in head_tiles(head):
                p = jnp.exp2(s_scr[head // 2, :, st:st + LANES] - m)
                p_scr[head // 2, :, st:st + LANES] = p.astype(BF16)
                acc = p if acc is None else acc + p
            denom = (jnp.broadcast_to(jnp.sum(acc, axis=1, keepdims=True), (QB, LANES))
                     + jnp.exp2(sink_ref[head] * LOG2E - m))
            m_scr[head] = 1.0 / denom
        lane = lax.broadcasted_iota(jnp.int32, (QB, LANES), 1)
        for c in range(N_HEADS // 2):
            o = (_dot(p_scr[c, :, 0:2 * n_loc], v_rows[2 * c // group])
                 + _dot(p_scr[c, :, 2 * n_loc:], vc_rows[2 * c // group]))
            o = o * jnp.where(lane < HEAD_DIM, m_scr[2 * c], m_scr[2 * c + 1])
            mix[pl.ds(r0, QB), c * LANES:(c + 1) * LANES] = o.astype(BF16)

        slab = pext[pl.ds(pl.multiple_of(r0 + POOL_OFF, 8), POOL_SLAB), :]
        tpos = tok + (n * TQ + j * QB)
        for g, w in enumerate(POOL_WINDOWS):
            sg = slab[:, g * LANES:(g + 1) * LANES]
            hi, lo = _split_bf16(sg)
            band = band_ref[g]
            wsum = _dot(band, hi) + _dot(band, lo)
            cnt = (jnp.minimum(tpos - w // 2 + w, seq_len) - jnp.maximum(tpos - w // 2, 0)).astype(F32)
            dlt = wsum / cnt - sg[POOL_OFF:POOL_OFF + QB, :]
            y = _dot(dlt.astype(BF16), poolw_ref[g]) * pscale_ref[:, g * LANES:(g + 1) * LANES]
            mix[pl.ds(r0, QB), ATTN_WIDTH + g * LANES:ATTN_WIDTH + (g + 1) * LANES] = y.astype(BF16)
        return carry

    lax.fori_loop(0, TQ // QB, sub_block, 0)

    proj = _dot(mix[...], wout_ref[...])
    x1 = x_ref[0] + mod_ref[0, 2:3, :] * proj
    x1_ref[0] = x1
    h2 = _norm_mod(x1, g2_ref[...], mod_ref[0, 3:4, :], mod_ref[0, 4:5, :])
    h2_ref[...] = _pack_words(h2)
    h_hi, h_lo = _split_bf16(h2)
    wrh = wrh_ref[...]
    lg_ref[...] = _dot_nt(wrh, h_hi) + _dot_nt(wrh, h_lo) + _dot_nt(wrl_ref[...], h_hi)


def _attn(b0, b, sink, q, k4, v4, kc4, vc4, p, x, mod, band, poolw_bf, pscale, wout_bf, g2, wr_hi, wr_lo):
    _, s, d = x.shape
    c = kc4.shape[1]
    nt = s // TQ
    hb = TQ // QB
    pb = TQ // 8
    kv_main = pl.BlockSpec((1, TQ, 4 * LANES), lambda bi, n: (b0 + bi, n, 0))
    kv_prev = pl.BlockSpec((1, QB, 4 * LANES), lambda bi, n: (b0 + bi, jnp.maximum(n * hb - 1, 0), 0))
    kv_next = pl.BlockSpec((1, QB, 4 * LANES), lambda bi, n: (b0 + bi, jnp.minimum((n + 1) * hb, s // QB - 1), 0))
    const2 = lambda bi, n: (0, 0)
    const3 = lambda bi, n: (0, 0, 0)
    return pl.pallas_call(
        functools.partial(_attn_kernel, s),
        out_shape=(jax.ShapeDtypeStruct((b, s, d), F32),
                   jax.ShapeDtypeStruct((b * s, d // 2), U32),
                   jax.ShapeDtypeStruct((N_EXPERTS, b * s), F32)),
        grid=(b, nt),
        in_specs=[pl.BlockSpec(memory_space=pltpu.SMEM),
                  pl.BlockSpec((1, TQ, ATTN_WIDTH), lambda bi, n: (b0 + bi, n, 0)),
                  kv_main, kv_prev, kv_next, kv_main, kv_prev, kv_next,
                  pl.BlockSpec((1, c, 4 * LANES), lambda bi, n: (b0 + bi, 0, 0)),
                  pl.BlockSpec((1, c, 4 * LANES), lambda bi, n: (b0 + bi, 0, 0)),
                  pl.BlockSpec((1, TQ, POOL_WIDTH), lambda bi, n: (b0 + bi, n, 0)),
                  pl.BlockSpec((1, 8, POOL_WIDTH), lambda bi, n: (b0 + bi, jnp.maximum(n * pb - 1, 0), 0)),
                  pl.BlockSpec((1, 8, POOL_WIDTH),
                               lambda bi, n: (b0 + bi, jnp.minimum((n + 1) * pb, s // 8 - 1), 0)),
                  pl.BlockSpec((1, TQ, d), lambda bi, n: (b0 + bi, n, 0)),
                  pl.BlockSpec((1, 6, d), lambda bi, n: (b0 + bi, 0, 0)),
                  pl.BlockSpec((len(POOL_WINDOWS), QB, POOL_SLAB), const3),
                  pl.BlockSpec((len(POOL_WINDOWS), POOL_GROUP_DIM, POOL_GROUP_DIM), const3),
                  pl.BlockSpec((1, POOL_WIDTH), const2),
                  pl.BlockSpec((d, d), const2),
                  pl.BlockSpec((1, d), const2),
                  pl.BlockSpec((N_EXPERTS, d), const2),
                  pl.BlockSpec((N_EXPERTS, d), const2)],
        out_specs=(pl.BlockSpec((1, TQ, d), lambda bi, n: (bi, n, 0)),
                   pl.BlockSpec((TQ, d // 2), lambda bi, n: (bi * nt + n, 0)),
                   pl.BlockSpec((N_EXPERTS, TQ), lambda bi, n: (0, bi * nt + n))),
        scratch_shapes=[pltpu.VMEM((TQ + 2 * QB, 4 * LANES), BF16),
                        pltpu.VMEM((TQ + 2 * QB, 4 * LANES), BF16),
                        pltpu.VMEM((TQ + 2 * QB, POOL_WIDTH), F32),
                        pltpu.VMEM((TQ, d), BF16),
                        pltpu.VMEM((N_HEADS // 2, QB, 2 * (3 * QB + c)), F32),
                        pltpu.VMEM((N_HEADS // 2, QB, 2 * (3 * QB + c)), BF16),
                        pltpu.VMEM((N_HEADS, QB, LANES), F32)],
        compiler_params=pltpu.CompilerParams(vmem_limit_bytes=VMEM_LIMIT),
        name="attn",
    )(sink, q, k4, k4, k4, v4, v4, v4, kc4, vc4, p, p, p, x, mod, band, poolw_bf, pscale, wout_bf,
      g2, wr_hi, wr_lo)


def _first_argmax_rows(v, row_iota, n_rows):
    m = jnp.max(v, axis=0, keepdims=True)
    idx = jnp.min(jnp.where(v == m, row_iota, n_rows), axis=0, keepdims=True)
    return m, idx


def _route_kernel(lg_ref, bias_ref, tri_ref, idx_ref, gate_ref, rank_ref, cnt_ref, carry):
    i = pl.program_id(0)

    @pl.when(i == 0)
    def _():
        carry[...] = jnp.zeros_like(carry)

    scores = 1.0 / (1.0 + jnp.exp(-lg_ref[...]))
    biased = scores + bias_ref[...]
    e_iota = lax.broadcasted_iota(jnp.int32, scores.shape, 0).astype(F32)
    g_iota = lax.broadcasted_iota(jnp.int32, (EXPERTS_PER_GROUP, TR), 0).astype(F32)
    neg = -jnp.inf

    grp = []
    for g in range(N_EXPERT_GROUPS):
        blk = biased[g * EXPERTS_PER_GROUP:(g + 1) * EXPERTS_PER_GROUP, :]
        m1, i1 = _first_argmax_rows(blk, g_iota, float(EXPERTS_PER_GROUP))
        m2 = jnp.max(jnp.where(g_iota == i1, neg, blk), axis=0, keepdims=True)
        grp.append(m1 + m2)
    grp = jnp.concatenate(grp, axis=0)
    gg_iota = lax.broadcasted_iota(jnp.int32, grp.shape, 0).astype(F32)
    grp_sel = jnp.zeros(grp.shape, F32)
    for _ in range(TOPK_GROUPS):
        _, gi = _first_argmax_rows(grp, gg_iota, float(N_EXPERT_GROUPS))
        hit = gg_iota == gi
        grp_sel = jnp.where(hit, 1.0, grp_sel)
        grp = jnp.where(hit, neg, grp)
    allowed = jnp.concatenate(
        [jnp.broadcast_to(grp_sel[g:g + 1, :], (EXPERTS_PER_GROUP, TR)) for g in range(N_EXPERT_GROUPS)], axis=0)
    masked = jnp.where(allowed > 0.5, biased, neg)

    idxs, gates = [], []
    onehot = jnp.zeros(scores.shape, F32)
    for _ in range(TOP_K):
        _, ei = _first_argmax_rows(masked, e_iota, float(N_EXPERTS))
        hit = e_iota == ei
        idxs.append(ei)
        gates.append(jnp.sum(jnp.where(hit, scores, 0.0), axis=0, keepdims=True))
        onehot = jnp.where(hit, 1.0, onehot)
        masked = jnp.where(hit, neg, masked)
    idx = jnp.concatenate(idxs, axis=0)
    gate = jnp.concatenate(gates, axis=0)
    gate = gate / jnp.sum(gate, axis=0, keepdims=True) * ROUTED_SCALE

    before = _dot(onehot.astype(BF16), tri_ref[...]) + carry[:, 0:1]
    ranks = [jnp.sum(jnp.where(e_iota == idxs[k], before, 0.0), axis=0, keepdims=True) for k in range(TOP_K)]
    idx_ref[...] = idx.astype(jnp.int32)
    gate_ref[...] = gate
    rank_ref[...] = jnp.concatenate(ranks, axis=0).astype(jnp.int32)
    total = carry[...] + jnp.sum(onehot, axis=1, keepdims=True)
    carry[...] = total
    cnt_ref[...] = total


def _route(lg_t, bias, tri):
    e, t = lg_t.shape
    tok = pl.BlockSpec((TOP_K, TR), lambda i: (0, i))
    return pl.pallas_call(
        _route_kernel,
        out_shape=(jax.ShapeDtypeStruct((TOP_K, t), jnp.int32),
                   jax.ShapeDtypeStruct((TOP_K, t), F32),
                   jax.ShapeDtypeStruct((TOP_K, t), jnp.int32),
                   jax.ShapeDtypeStruct((e, LANES), F32)),
        grid=(t // TR,),
        in_specs=[pl.BlockSpec((e, TR), lambda i: (0, i)),
                  pl.BlockSpec((e, 1), lambda i: (0, 0)),
                  pl.BlockSpec((TR, TR), lambda i: (0, 0))],
        out_specs=(tok, tok, tok, pl.BlockSpec((e, LANES), lambda i: (0, 0))),
        scratch_shapes=[pltpu.VMEM((e, LANES), F32)],
        compiler_params=pltpu.CompilerParams(vmem_limit_bytes=VMEM_LIMIT),
        name="route",
    )(lg_t, bias, tri)


def _plan_kernel(n_blocks, size_ref, start_ref, expert_ref, valid_ref, nact_ref):
    def per_expert(e, first_block):
        size = size_ref[e]
        n_blk = (size + TM_EXP - 1) // TM_EXP
        start_ref[e] = first_block * TM_EXP

        def per_block(j, carry):
            expert_ref[first_block + j] = e
            valid_ref[first_block + j] = jnp.minimum(size - j * TM_EXP, TM_EXP)
            return carry

        lax.fori_loop(0, n_blk, per_block, 0)
        return first_block + n_blk

    n_active = lax.fori_loop(0, N_EXPERTS, per_expert, 0)
    nact_ref[0] = n_active

    def unused(i, carry):
        expert_ref[i] = N_EXPERTS - 1
        valid_ref[i] = 0
        return carry

    lax.fori_loop(n_active, n_blocks, unused, 0)


def _plan_blocks(sizes, n_blocks):
    smem = pl.BlockSpec(memory_space=pltpu.SMEM)
    return pl.pallas_call(
        functools.partial(_plan_kernel, n_blocks),
        out_shape=(jax.ShapeDtypeStruct((N_EXPERTS,), jnp.int32),
                   jax.ShapeDtypeStruct((n_blocks,), jnp.int32),
                   jax.ShapeDtypeStruct((n_blocks,), jnp.int32),
                   jax.ShapeDtypeStruct((1,), jnp.int32)),
        in_specs=[smem],
        out_specs=(smem, smem, smem, smem),
        name="plan_blocks",
    )(sizes)


def _dest_kernel(start_ref, idx_ref, rank_ref, dest_ref):
    idx = idx_ref[...]
    dest = rank_ref[...]
    for e in range(N_EXPERTS):
        dest = dest + jnp.where(idx == e, start_ref[e], 0)
    dest_ref[...] = dest


def _dest_rows(pad_start, idx_kt, rank_kt):
    n_k, t = idx_kt.shape
    tile = 4096
    blk = pl.BlockSpec((n_k, tile), lambda i: (0, i))
    return pl.pallas_call(
        _dest_kernel,
        out_shape=jax.ShapeDtypeStruct((n_k, t), jnp.int32),
        grid=(t // tile,),
        in_specs=[pl.BlockSpec(memory_space=pltpu.SMEM), blk, blk],
        out_specs=blk,
        name="dest_rows",
    )(pad_start, idx_kt, rank_kt)


def _sc_mesh():
    return plsc.VectorSubcoreMesh(core_axis_name="c", subcore_axis_name="s")


def _sc_token_base(steps, j):
    worker = lax.axis_index("s") * SC_CORES + lax.axis_index("c")
    return (worker * steps + j) * SC_TOKENS


def _sc_scatter(dest_kt, h2p, n_rows):
    t, width = h2p.shape
    steps = t // (SC_WORKERS * SC_TOKENS)

    @functools.partial(
        pl.kernel, mesh=_sc_mesh(),
        out_type=jax.ShapeDtypeStruct((n_rows, width), U32),
        scratch_types=[pltpu.VMEM((TOP_K, SC_TOKENS), jnp.int32),
                       pltpu.VMEM((SC_TOKENS, width), U32),
                       pltpu.SemaphoreType.DMA],
        name="sc_scatter",
    )
    def body(dest_hbm, h_hbm, xs_hbm, idx_v, rows_v, sem):
        @pl.loop(0, steps)
        def _(j):
            base = _sc_token_base(steps, j)
            pltpu.sync_copy(dest_hbm.at[:, pl.ds(base, SC_TOKENS)], idx_v)
            pltpu.sync_copy(h_hbm.at[pl.ds(base, SC_TOKENS)], rows_v)
            copies = [pltpu.async_copy(rows_v, xs_hbm.at[idx_v.at[k]], sem) for k in range(TOP_K)]
            for cp in copies:
                cp.wait()

    return body(dest_kt, h2p)


def _sc_gather(dest_kt, ys, token0, n_tokens):
    n_k = dest_kt.shape[0]
    width = ys.shape[1]
    steps = n_tokens // (SC_WORKERS * SC_TOKENS)

    half = SC_TOKENS // 2
    units = [(k, h) for k in range(n_k) for h in range(2)]
    n_buf = 3

    @functools.partial(
        pl.kernel, mesh=_sc_mesh(),
        out_type=jax.ShapeDtypeStruct((n_k, n_tokens, width), U32),
        scratch_types=[pltpu.VMEM((n_k, SC_TOKENS), jnp.int32),
                       pltpu.VMEM((n_buf, half, width), U32),
                       pltpu.SemaphoreType.DMA((n_buf,))],
        name="sc_gather",
    )
    def body(dest_hbm, ys_hbm, yk_hbm, idx_v, rows_v, sems):
        @pl.loop(0, steps)
        def _(j):
            base = _sc_token_base(steps, j)
            pltpu.sync_copy(dest_hbm.at[:, pl.ds(token0 + base, SC_TOKENS)], idx_v)

            def gather(u):
                k, h = units[u]
                slot = u % n_buf
                return pltpu.make_async_copy(ys_hbm.at[idx_v.at[k, pl.ds(h * half, half)]], rows_v.at[slot],
                                             sems.at[slot])

            gather(0).start()
            gather(1).start()
            for u, (k, h) in enumerate(units):
                gather(u).wait()
                pltpu.sync_copy(rows_v.at[u % n_buf], yk_hbm.at[k, pl.ds(base + h * half, half)])
                if u + 2 < len(units):
                    gather(u + 2).start()

    return body(dest_kt, ys)


def _experts_kernel(be_ref, valid_ref, nact_ref, xs_ref, wg_ref, wu_ref, wd_ref, ys_ref, wg_bf, wu_bf, wd_bf):
    i = pl.program_id(0)
    prev = be_ref[jnp.maximum(i - 1, 0)]

    @pl.when((i == 0) | (be_ref[i] != prev))
    def _():
        wg_bf[...] = wg_ref[0].astype(BF16)
        wu_bf[...] = wu_ref[0].astype(BF16)
        wd_bf[...] = wd_ref[0].astype(BF16)

    @pl.when(i < nact_ref[0])
    def _():
        row = lax.broadcasted_iota(jnp.int32, xs_ref.shape, 0)
        words = jnp.where(row < valid_ref[i], xs_ref[...], jnp.uint32(0))
        xb = _unpack_words(words).astype(BF16)
        hid = _silu(_dot(xb, wg_bf[...])) * _dot(xb, wu_bf[...])
        ys_ref[...] = _pack_words(_dot(hid.astype(BF16), wd_bf[...]))

    @pl.when(i >= nact_ref[0])
    def _():
        ys_ref[...] = jnp.zeros_like(ys_ref)


def _experts(block_e, block_valid, n_active, xs, w_gate, w_up, w_down):
    n_rows, half = xs.shape
    d = w_gate.shape[1]
    nb = n_rows // TM_EXP
    grid_spec = pltpu.PrefetchScalarGridSpec(
        num_scalar_prefetch=3,
        grid=(nb,),
        in_specs=[pl.BlockSpec((TM_EXP, half), lambda i, be, bv, na: (jnp.minimum(i, na[0] - 1), 0)),
                  pl.BlockSpec((1, d, D_EXPERT), lambda i, be, bv, na: (be[i], 0, 0)),
                  pl.BlockSpec((1, d, D_EXPERT), lambda i, be, bv, na: (be[i], 0, 0)),
                  pl.BlockSpec((1, D_EXPERT, d), lambda i, be, bv, na: (be[i], 0, 0))],
        out_specs=pl.BlockSpec((TM_EXP, half), lambda i, be, bv, na: (i, 0)),
        scratch_shapes=[pltpu.VMEM((d, D_EXPERT), BF16), pltpu.VMEM((d, D_EXPERT), BF16),
                        pltpu.VMEM((D_EXPERT, d), BF16)],
    )
    return pl.pallas_call(
        _experts_kernel,
        out_shape=jax.ShapeDtypeStruct((n_rows, half), U32),
        grid_spec=grid_spec,
        compiler_params=pltpu.CompilerParams(vmem_limit_bytes=VMEM_LIMIT),
        name="experts",
    )(block_e, block_valid, n_active, xs, w_gate, w_up, w_down)


def _combine_kernel(yk_ref, x1_ref, h2_ref, gate_ref, mod_ref, wsg_ref, wsu_ref, wsd_ref, fg_ref, *out_refs):
    out_ref = out_refs[-1]
    hb = _unpack_words(h2_ref[...]).astype(BF16)
    hid = _silu(_dot(hb, wsg_ref[...])) * _dot(hb, wsu_ref[...])
    ffn = _dot(hid.astype(BF16), wsd_ref[...])
    gate = gate_ref[...]
    for k in range(TOP_K):
        ffn = ffn + gate[:, k:k + 1] * _unpack_words(yk_ref[k])
    x2 = x1_ref[...] + mod_ref[0, 5:6, :] * ffn
    ms = jnp.mean(x2 * x2, axis=-1, keepdims=True)
    out_ref[...] = x2 * lax.rsqrt(ms + EPS) * fg_ref[...]


def _combine(yk, token0, out_token0, n_out_tokens, prev_out, x1, h2p, gate_tk, mod, wsg_bf, wsu_bf, wsd_bf, final_g,
             seq_len):
    t, d = n_out_tokens, x1.shape[1]
    tiles_per_seq = seq_len // TF
    tile0 = token0 // TF
    out_tile0 = out_token0 // TF
    in_tok = pl.BlockSpec((TF, d), lambda i: (tile0 + i, 0))
    tok = pl.BlockSpec((TF, d), lambda i: (out_tile0 + i, 0))
    in_specs = [pl.BlockSpec((TOP_K, TF, d // 2), lambda i: (0, i, 0)),
                in_tok, pl.BlockSpec((TF, d // 2), lambda i: (tile0 + i, 0)),
                pl.BlockSpec((TF, TOP_K), lambda i: (tile0 + i, 0)),
                pl.BlockSpec((1, 6, d), lambda i: ((out_tile0 + i) // tiles_per_seq, 0, 0)),
                pl.BlockSpec((d, D_SHARED), lambda i: (0, 0)),
                pl.BlockSpec((d, D_SHARED), lambda i: (0, 0)),
                pl.BlockSpec((D_SHARED, d), lambda i: (0, 0)),
                pl.BlockSpec((1, d), lambda i: (0, 0))]
    args = [yk, x1, h2p, gate_tk, mod, wsg_bf, wsu_bf, wsd_bf, final_g]
    aliases = {}
    if prev_out is not None:
        in_specs.append(pl.BlockSpec(memory_space=pl.ANY))
        args.append(prev_out)
        aliases = {len(args) - 1: 0}
    return pl.pallas_call(
        _combine_kernel,
        out_shape=jax.ShapeDtypeStruct((t, d), F32),
        grid=(yk.shape[1] // TF,),
        in_specs=in_specs,
        out_specs=tok,
        input_output_aliases=aliases,
        compiler_params=pltpu.CompilerParams(vmem_limit_bytes=VMEM_LIMIT),
        name="combine",
    )(*args)


def _rope_tables(n_tokens):
    n_rows = n_tokens // GRID_W
    n_freq = HEAD_DIM // 4
    inv_freq = ROPE_THETA ** (-jnp.arange(n_freq, dtype=F32) / n_freq)
    ang_r = jnp.arange(n_rows).astype(F32)[:, None] * inv_freq[None, :]
    ang_c = jnp.arange(GRID_W).astype(F32)[:, None] * inv_freq[None, :]

    def per_token(row_part, col_part):
        rows = jnp.broadcast_to(row_part[:, None, :], (n_rows, GRID_W, n_freq))
        cols = jnp.broadcast_to(col_part[None, :, :], (n_rows, GRID_W, n_freq))
        return rows.reshape(n_tokens, n_freq), cols.reshape(n_tokens, n_freq)

    cos_r, cos_c = per_token(jnp.cos(ang_r), jnp.cos(ang_c))
    sin_r, sin_c = per_token(jnp.sin(ang_r), jnp.sin(ang_c))
    cos = jnp.concatenate([cos_r, cos_r, cos_c, cos_c], axis=1)
    sin = jnp.concatenate([-sin_r, sin_r, -sin_c, sin_c], axis=1)
    reps = LANES // HEAD_DIM
    return jnp.tile(cos, (1, reps)), jnp.tile(sin, (1, reps))


def _pool_bands():
    i = jnp.arange(QB)[:, None]
    r = jnp.arange(POOL_SLAB)[None, :]
    return jnp.stack([((r >= i + POOL_OFF - w // 2) & (r < i + POOL_OFF + w // 2)).astype(BF16)
                      for w in POOL_WINDOWS])


def kernel(x, c, ctx, c_ctx, w_ada, b_ada, norm1_g, norm2_g, w_in, attn_sink, pool_w, pool_scale, w_out,
           w_router, router_bias, w_gate, w_up, w_down, ws_gate, ws_up, ws_down, final_g):
    b, s, d = x.shape
    t = b * s
    assert w_ada.shape[0] == 1 and d == D_MODEL and s % TQ == 0 and b + 1 <= 8

    c8 = jnp.zeros((8, d), F32).at[:b].set(c).at[b].set(c_ctx)
    mod = _ada(c8, w_ada[0], b_ada[0]).reshape(8, 6, d)
    g1 = norm1_g[0].reshape(1, d)
    g2 = norm2_g[0].reshape(1, d)
    w_in_bf = w_in[0].astype(BF16)
    cos_t, sin_t = _rope_tables(s)

    q, k4, v4, p = _inproj(x, mod, g1, w_in_bf, cos_t, sin_t)
    kc4, vc4 = _ctxproj(ctx, mod[b:b + 1], g1, w_in_bf[:, ATTN_WIDTH:ATTN_WIDTH + 2 * KV_WIDTH])

    wr_t = w_router[0].T
    wr_hi = wr_t.astype(BF16)
    wr_lo = (wr_t - wr_hi.astype(F32)).astype(BF16)
    attn_consts = (_pool_bands(), pool_w[0].astype(BF16), pool_scale[0].reshape(1, POOL_WIDTH),
                   w_out[0].astype(BF16), g2, wr_hi, wr_lo)
    tri = jnp.triu(jnp.ones((TR, TR), BF16), k=1)
    shared_w = (ws_gate[0].astype(BF16), ws_up[0].astype(BF16), ws_down[0].astype(BF16))

    nb = b // TOKEN_GROUPS
    tg = nb * s
    n_rows = -(-(tg * TOP_K + N_EXPERTS * (TM_EXP - 1)) // TM_EXP) * TM_EXP
    groups = []
    for g in range(TOKEN_GROUPS):
        x1, h2p, lg_t = _attn(g * nb, nb, attn_sink[0], q, k4, v4, kc4, vc4, p, x, mod, *attn_consts)
        idx_kt, gate_kt, rank_kt, counts = _route(lg_t, router_bias[0].reshape(N_EXPERTS, 1), tri)
        pad_start, block_e, block_valid, n_active = _plan_blocks(counts[:, 0].astype(jnp.int32), n_rows // TM_EXP)
        dest_kt = _dest_rows(pad_start, idx_kt, rank_kt)
        xs = _sc_scatter(dest_kt, h2p, n_rows)
        groups.append((x1.reshape(tg, d), h2p, gate_kt.T, dest_kt, xs, block_e, block_valid, n_active))

    out = None
    chunk = tg // COMBINE_CHUNKS
    for g, (x1, h2p, gate_tk, dest_kt, xs, block_e, block_valid, n_active) in enumerate(groups):
        ys = _experts(block_e, block_valid, n_active, xs, w_gate[0], w_up[0], w_down[0])
        for token0 in range(0, tg, chunk):
            out = _combine(_sc_gather(dest_kt, ys, token0, chunk), token0, g * tg + token0, t, out, x1, h2p,
                           gate_tk, mod, *shared_w, final_g.reshape(1, d), s)
    return out.reshape(b, s, d)
```

```python
import functools

import jax
import jax.numpy as jnp
from jax import lax
from jax.experimental import pallas as pl
from jax.experimental.pallas import tpu as pltpu
from jax.experimental.pallas import tpu_sc as plsc

F32 = jnp.float32
BF16 = jnp.bfloat16

D_MODEL = 1024
GRID_W = 64
N_HEADS = 8
N_KV_HEADS = 2
HEAD_DIM = 64
ATTN_WIDTH = N_HEADS * HEAD_DIM
KV_WIDTH = N_KV_HEADS * HEAD_DIM
WINDOW = 128
ROPE_THETA = 10000.0
POOL_WINDOWS = (2, 4, 8, 16)
POOL_WIDTH = D_MODEL - ATTN_WIDTH
POOL_GROUP_DIM = POOL_WIDTH // len(POOL_WINDOWS)
IN_COLS = ATTN_WIDTH + 2 * KV_WIDTH + POOL_WIDTH
N_EXPERTS = 64
TOP_K = 8
N_EXPERT_GROUPS = 8
EXPERTS_PER_GROUP = N_EXPERTS // N_EXPERT_GROUPS
TOPK_GROUPS = 4
D_EXPERT = 256
D_SHARED = 256
ROUTED_SCALE = 2.5
EPS = 1e-6
LOG2E = 1.4426950408889634

LANES = 128
U32 = jnp.uint32
VMEM_LIMIT = 48 * 1024 * 1024

TM_PROJ = 512
TQ = 512
QB = 128
POOL_SLAB = 256
POOL_OFF = 64
TR = 512
TM_EXP = 512
TF = 256
TOKEN_GROUPS = 2
COMBINE_CHUNKS = 2
SC_CORES = 2
SC_WORKERS = 32
SC_TOKENS = 128


def _silu(x):
    return x * (1.0 / (1.0 + jnp.exp(-x)))


def _split_bf16(x):
    hi = x.astype(BF16)
    lo = (x - hi.astype(F32)).astype(BF16)
    return hi, lo


def _dot(a, b):
    return jnp.dot(a, b, preferred_element_type=F32)


def _pack_words(val):
    half = val.shape[1] // 2
    lo = lax.bitcast_convert_type(val[:, :half].astype(BF16).astype(F32), U32)
    hi = lax.bitcast_convert_type(val[:, half:].astype(BF16).astype(F32), U32)
    return lax.shift_right_logical(lo, jnp.uint32(16)) | hi


def _unpack_words(words):
    lo = lax.bitcast_convert_type(lax.shift_left(words, jnp.uint32(16)), F32)
    hi = lax.bitcast_convert_type(words & jnp.uint32(0xFFFF0000), F32)
    return jnp.concatenate([lo, hi], axis=1)


def _dot_nt(a, b):
    return lax.dot_general(a, b, (((1,), (1,)), ((), ())), preferred_element_type=F32)


def _ada_kernel(c_ref, w_ref, b_ref, o_ref):
    a_hi, a_lo = _split_bf16(_silu(c_ref[...]))
    w_hi, w_lo = _split_bf16(w_ref[...])
    o_ref[...] = _dot(a_hi, w_hi) + _dot(a_lo, w_hi) + _dot(a_hi, w_lo) + b_ref[...]


def _ada(c8, w_ada, b_ada):
    d = c8.shape[1]
    n = w_ada.shape[1]
    tn = 512
    return pl.pallas_call(
        _ada_kernel,
        out_shape=jax.ShapeDtypeStruct((8, n), F32),
        grid=(n // tn,),
        in_specs=[pl.BlockSpec((8, d), lambda j: (0, 0)),
                  pl.BlockSpec((d, tn), lambda j: (0, j)),
                  pl.BlockSpec((1, tn), lambda j: (0, j))],
        out_specs=pl.BlockSpec((8, tn), lambda j: (0, j)),
        compiler_params=pltpu.CompilerParams(vmem_limit_bytes=VMEM_LIMIT),
        name="ada",
    )(c8, w_ada, b_ada.reshape(1, n))


def _norm_mod(x, g, shift, scale):
    ms = jnp.mean(x * x, axis=-1, keepdims=True)
    return (x * lax.rsqrt(ms + EPS) * g) * (1.0 + scale) + shift


def _lane_variants(t):
    lane = lax.broadcasted_iota(jnp.int32, t.shape, 1)
    lo = lane < HEAD_DIM
    tr = pltpu.roll(t, HEAD_DIM, 1)
    zero = jnp.zeros_like(t)
    return (jnp.where(lo, t, zero), jnp.where(lo, zero, tr),
            jnp.where(lo, tr, zero), jnp.where(lo, zero, t))


def _store_variants(ref, t):
    for i, var in enumerate(_lane_variants(t)):
        ref[0, :, i * LANES:(i + 1) * LANES] = var.astype(BF16)


def _inproj_kernel(x_ref, mod_ref, g_ref, w_ref, cos_ref, sin_ref, q_ref, k_ref, v_ref, p_ref):
    h = _norm_mod(x_ref[0], g_ref[...], mod_ref[0, 0:1, :], mod_ref[0, 1:2, :])
    z = _dot(h.astype(BF16), w_ref[...])
    cos = cos_ref[...]
    sin = sin_ref[...]
    lane = lax.broadcasted_iota(jnp.int32, cos.shape, 1)
    first_half = (lane & 16) == 0

    def rope(zc):
        partner = jnp.where(first_half, pltpu.roll(zc, LANES - 16, 1), pltpu.roll(zc, 16, 1))
        return zc * cos + partner * sin

    scale = HEAD_DIM ** -0.5 * LOG2E
    for c in range(ATTN_WIDTH // LANES):
        q_ref[0, :, c * LANES:(c + 1) * LANES] = (rope(z[:, c * LANES:(c + 1) * LANES]) * scale).astype(BF16)
    _store_variants(k_ref, rope(z[:, ATTN_WIDTH:ATTN_WIDTH + KV_WIDTH]))
    _store_variants(v_ref, z[:, ATTN_WIDTH + KV_WIDTH:ATTN_WIDTH + 2 * KV_WIDTH])
    p_ref[0] = z[:, ATTN_WIDTH + 2 * KV_WIDTH:]


def _inproj(x, mod, g1, w_in_bf, cos_t, sin_t):
    b, s, d = x.shape
    tm = TM_PROJ
    return pl.pallas_call(
        _inproj_kernel,
        out_shape=(jax.ShapeDtypeStruct((b, s, ATTN_WIDTH), BF16),
                   jax.ShapeDtypeStruct((b, s, 4 * LANES), BF16),
                   jax.ShapeDtypeStruct((b, s, 4 * LANES), BF16),
                   jax.ShapeDtypeStruct((b, s, POOL_WIDTH), F32)),
        grid=(s // tm, b),
        in_specs=[pl.BlockSpec((1, tm, d), lambda n, bi: (bi, n, 0)),
                  pl.BlockSpec((1, 6, d), lambda n, bi: (bi, 0, 0)),
                  pl.BlockSpec((1, d), lambda n, bi: (0, 0)),
                  pl.BlockSpec((d, IN_COLS), lambda n, bi: (0, 0)),
                  pl.BlockSpec((tm, LANES), lambda n, bi: (n, 0)),
                  pl.BlockSpec((tm, LANES), lambda n, bi: (n, 0))],
        out_specs=(pl.BlockSpec((1, tm, ATTN_WIDTH), lambda n, bi: (bi, n, 0)),
                   pl.BlockSpec((1, tm, 4 * LANES), lambda n, bi: (bi, n, 0)),
                   pl.BlockSpec((1, tm, 4 * LANES), lambda n, bi: (bi, n, 0)),
                   pl.BlockSpec((1, tm, POOL_WIDTH), lambda n, bi: (bi, n, 0))),
        compiler_params=pltpu.CompilerParams(vmem_limit_bytes=VMEM_LIMIT),
        name="inproj",
    )(x, mod, g1, w_in_bf, cos_t, sin_t)


def _ctxproj_kernel(x_ref, mod_ref, g_ref, w_ref, k_ref, v_ref):
    h = _norm_mod(x_ref[0], g_ref[...], mod_ref[0, 0:1, :], mod_ref[0, 1:2, :])
    z = _dot(h.astype(BF16), w_ref[...])
    _store_variants(k_ref, z[:, :KV_WIDTH])
    _store_variants(v_ref, z[:, KV_WIDTH:])


def _ctxproj(ctx, mod_c, g1, w_kv_bf):
    b, c, d = ctx.shape
    return pl.pallas_call(
        _ctxproj_kernel,
        out_shape=(jax.ShapeDtypeStruct((b, c, 4 * LANES), BF16),
                   jax.ShapeDtypeStruct((b, c, 4 * LANES), BF16)),
        grid=(b,),
        in_specs=[pl.BlockSpec((1, c, d), lambda bi: (bi, 0, 0)),
                  pl.BlockSpec((1, 6, d), lambda bi: (0, 0, 0)),
                  pl.BlockSpec((1, d), lambda bi: (0, 0)),
                  pl.BlockSpec((d, 2 * KV_WIDTH), lambda bi: (0, 0))],
        out_specs=(pl.BlockSpec((1, c, 4 * LANES), lambda bi: (bi, 0, 0)),
                   pl.BlockSpec((1, c, 4 * LANES), lambda bi: (bi, 0, 0))),
        compiler_params=pltpu.CompilerParams(vmem_limit_bytes=VMEM_LIMIT),
        name="ctxproj",
    )(ctx, mod_c, g1, w_kv_bf)


def _fold(op, tiles):
    while len(tiles) > 1:
        tiles = [op(tiles[i], tiles[i + 1]) if i + 1 < len(tiles) else tiles[i] for i in range(0, len(tiles), 2)]
    return tiles[0]


def _stack_variants(t4, kv):
    return jnp.concatenate([t4[:, (2 * kv) * LANES:(2 * kv + 1) * LANES],
                            t4[:, (2 * kv + 1) * LANES:(2 * kv + 2) * LANES]], axis=0)


def _attn_kernel(seq_len, sink_ref, q_ref, k_ref, kp_ref, kn_ref, v_ref, vp_ref, vn_ref, kc_ref, vc_ref,
                 p_ref, pp_ref, pn_ref, x_ref, mod_ref, band_ref, poolw_ref, pscale_ref, wout_ref,
                 g2_ref, wrh_ref, wrl_ref, x1_ref, h2_ref, lg_ref, kwin, vwin, pext, mix, s_scr, p_scr, m_scr):
    n = pl.program_id(1)
    n_last = pl.num_programs(1) - 1

    kwin[0:QB, :] = kp_ref[0]
    kwin[QB:QB + TQ, :] = k_ref[0]
    kwin[QB + TQ:, :] = kn_ref[0]
    vwin[0:QB, :] = vp_ref[0]
    vwin[QB:QB + TQ, :] = v_ref[0]
    vwin[QB + TQ:, :] = vn_ref[0]

    pext[0:QB - 8, :] = jnp.zeros((QB - 8, POOL_WIDTH), F32)
    pext[QB - 8:QB, :] = jnp.where(n > 0, pp_ref[0], 0.0)
    pext[QB:QB + TQ, :] = p_ref[0]
    pext[QB + TQ:QB + TQ + 8, :] = jnp.where(n < n_last, pn_ref[0], 0.0)
    pext[QB + TQ + 8:, :] = jnp.zeros((QB - 8, POOL_WIDTH), F32)

    row = lax.broadcasted_iota(jnp.int32, (QB, 3 * QB), 0)
    col = lax.broadcasted_iota(jnp.int32, (QB, 3 * QB), 1)
    in_band = (col >= row) & (col <= row + 2 * WINDOW)
    tok = lax.broadcasted_iota(jnp.int32, (QB, 1), 0)
    kc = kc_ref[0]
    kc_rows = [_stack_variants(kc, kv) for kv in range(N_KV_HEADS)]
    vc_rows = [_stack_variants(vc_ref[0], kv) for kv in range(N_KV_HEADS)]

    def sub_block(j, carry):
        r0 = pl.multiple_of(j * QB, QB)
        qj = q_ref[0, pl.ds(r0, QB), :]
        kw = kwin[pl.ds(r0, 3 * QB), :]
        vw = vwin[pl.ds(r0, 3 * QB), :]
        kpos = col + (n * TQ + j * QB - QB)
        ok = in_band & (kpos >= 0) & (kpos < seq_len)
        bias = jnp.where(ok, 0.0, -jnp.inf)
        n_loc, n_ctx = 3 * QB, kc.shape[0]
        bias2 = jnp.concatenate([bias, bias], axis=1)
        k_rows = [_stack_variants(kw, kv) for kv in range(N_KV_HEADS)]
        v_rows = [_stack_variants(vw, kv) for kv in range(N_KV_HEADS)]
        group = N_HEADS // N_KV_HEADS

        def head_tiles(head):
            loc0 = (head % 2) * n_loc
            ctx0 = 2 * n_loc + (head % 2) * n_ctx
            return ([loc0 + i * LANES for i in range(n_loc // LANES)]
                    + [ctx0 + i * LANES for i in range(n_ctx // LANES)])

        for c in range(N_HEADS // 2):
            qc = qj[:, c * LANES:(c + 1) * LANES]
            s_scr[c, :, 0:2 * n_loc] = _dot_nt(qc, k_rows[2 * c // group]) + bias2
            s_scr[c, :, 2 * n_loc:] = _dot_nt(qc, kc_rows[2 * c // group])
        for head in range(N_HEADS):
            tiles = [s_scr[head // 2, :, st:st + LANES] for st in head_tiles(head)]
            row_max = jnp.max(_fold(jnp.maximum, tiles), axis=1, keepdims=True)
            m_scr[head] = jnp.broadcast_to(jnp.maximum(row_max, sink_ref[head] * LOG2E), (QB, LANES))
        for head in range(N_HEADS):
            m = m_scr[head]
            acc = None
            for st in head_tiles(head):
                p = jnp.exp2(s_scr[head // 2, :, st:st + LANES] - m)
                p_scr[head // 2, :, st:st + LANES] = p.astype(BF16)
                acc = p if acc is None else acc + p
            denom = (jnp.broadcast_to(jnp.sum(acc, axis=1, keepdims=True), (QB, LANES))
                     + jnp.exp2(sink_ref[head] * LOG2E - m))
            m_scr[head] = 1.0 / denom
        lane = lax.broadcasted_iota(jnp.int32, (QB, LANES), 1)
        for c in range(N_HEADS // 2):
            o = (_dot(p_scr[c, :, 0:2 * n_loc], v_rows[2 * c // group])
                 + _dot(p_scr[c, :, 2 * n_loc:], vc_rows[2 * c // group]))
            o = o * jnp.where(lane < HEAD_DIM, m_scr[2 * c], m_scr[2 * c + 1])
            mix[pl.ds(r0, QB), c * LANES:(c + 1) * LANES] = o.astype(BF16)

        slab = pext[pl.ds(pl.multiple_of(r0 + POOL_OFF, 8), POOL_SLAB), :]
        tpos = tok + (n * TQ + j * QB)
        for g, w in enumerate(POOL_WINDOWS):
            sg = slab[:, g * LANES:(g + 1) * LANES]
            hi, lo = _split_bf16(sg)
            band = band_ref[g]
            wsum = _dot(band, hi) + _dot(band, lo)
            cnt = (jnp.minimum(tpos - w // 2 + w, seq_len) - jnp.maximum(tpos - w // 2, 0)).astype(F32)
            dlt = wsum / cnt - sg[POOL_OFF:POOL_OFF + QB, :]
            y = _dot(dlt.astype(BF16), poolw_ref[g]) * pscale_ref[:, g * LANES:(g + 1) * LANES]
            mix[pl.ds(r0, QB), ATTN_WIDTH + g * LANES:ATTN_WIDTH + (g + 1) * LANES] = y.astype(BF16)
        return carry

    lax.fori_loop(0, TQ // QB, sub_block, 0)

    proj = _dot(mix[...], wout_ref[...])
    x1 = x_ref[0] + mod_ref[0, 2:3, :] * proj
    x1_ref[0] = x1
    h2 = _norm_mod(x1, g2_ref[...], mod_ref[0, 3:4, :], mod_ref[0, 4:5, :])
    h2_ref[...] = _pack_words(h2)
    h_hi, h_lo = _split_bf16(h2)
    wrh = wrh_ref[...]
    lg_ref[...] = _dot_nt(wrh, h_hi) + _dot_nt(wrh, h_lo) + _dot_nt(wrl_ref[...], h_hi)


def _attn(b0, b, sink, q, k4, v4, kc4, vc4, p, x, mod, band, poolw_bf, pscale, wout_bf, g2, wr_hi, wr_lo):
    _, s, d = x.shape
    c = kc4.shape[1]
    nt = s // TQ
    hb = TQ // QB
    pb = TQ // 8
    kv_main = pl.BlockSpec((1, TQ, 4 * LANES), lambda bi, n: (b0 + bi, n, 0))
    kv_prev = pl.BlockSpec((1, QB, 4 * LANES), lambda bi, n: (b0 + bi, jnp.maximum(n * hb - 1, 0), 0))
    kv_next = pl.BlockSpec((1, QB, 4 * LANES), lambda bi, n: (b0 + bi, jnp.minimum((n + 1) * hb, s // QB - 1), 0))
    const2 = lambda bi, n: (0, 0)
    const3 = lambda bi, n: (0, 0, 0)
    return pl.pallas_call(
        functools.partial(_attn_kernel, s),
        out_shape=(jax.ShapeDtypeStruct((b, s, d), F32),
                   jax.ShapeDtypeStruct((b * s, d // 2), U32),
                   jax.ShapeDtypeStruct((N_EXPERTS, b * s), F32)),
        grid=(b, nt),
        in_specs=[pl.BlockSpec(memory_space=pltpu.SMEM),
                  pl.BlockSpec((1, TQ, ATTN_WIDTH), lambda bi, n: (b0 + bi, n, 0)),
                  kv_main, kv_prev, kv_next, kv_main, kv_prev, kv_next,
                  pl.BlockSpec((1, c, 4 * LANES), lambda bi, n: (b0 + bi, 0, 0)),
                  pl.BlockSpec((1, c, 4 * LANES), lambda bi, n: (b0 + bi, 0, 0)),
                  pl.BlockSpec((1, TQ, POOL_WIDTH), lambda bi, n: (b0 + bi, n, 0)),
                  pl.BlockSpec((1, 8, POOL_WIDTH), lambda bi, n: (b0 + bi, jnp.maximum(n * pb - 1, 0), 0)),
                  pl.BlockSpec((1, 8, POOL_WIDTH),
                               lambda bi, n: (b0 + bi, jnp.minimum((n + 1) * pb, s // 8 - 1), 0)),
                  pl.BlockSpec((1, TQ, d), lambda bi, n: (b0 + bi, n, 0)),
                  pl.BlockSpec((1, 6, d), lambda bi, n: (b0 + bi, 0, 0)),
                  pl.BlockSpec((len(POOL_WINDOWS), QB, POOL_SLAB), const3),
                  pl.BlockSpec((len(POOL_WINDOWS), POOL_GROUP_DIM, POOL_GROUP_DIM), const3),
                  pl.BlockSpec((1, POOL_WIDTH), const2),
                  pl.BlockSpec((d, d), const2),
                  pl.BlockSpec((1, d), const2),
                  pl.BlockSpec((N_EXPERTS, d), const2),
                  pl.BlockSpec((N_EXPERTS, d), const2)],
        out_specs=(pl.BlockSpec((1, TQ, d), lambda bi, n: (bi, n, 0)),
                   pl.BlockSpec((TQ, d // 2), lambda bi, n: (bi * nt + n, 0)),
                   pl.BlockSpec((N_EXPERTS, TQ), lambda bi, n: (0, bi * nt + n))),
        scratch_shapes=[pltpu.VMEM((TQ + 2 * QB, 4 * LANES), BF16),
                        pltpu.VMEM((TQ + 2 * QB, 4 * LANES), BF16),
                        pltpu.VMEM((TQ + 2 * QB, POOL_WIDTH), F32),
                        pltpu.VMEM((TQ, d), BF16),
                        pltpu.VMEM((N_HEADS // 2, QB, 2 * (3 * QB + c)), F32),
                        pltpu.VMEM((N_HEADS // 2, QB, 2 * (3 * QB + c)), BF16),
                        pltpu.VMEM((N_HEADS, QB, LANES), F32)],
        compiler_params=pltpu.CompilerParams(vmem_limit_bytes=VMEM_LIMIT),
        name="attn",
    )(sink, q, k4, k4, k4, v4, v4, v4, kc4, vc4, p, p, p, x, mod, band, poolw_bf, pscale, wout_bf,
      g2, wr_hi, wr_lo)


def _first_argmax_rows(v, row_iota, n_rows):
    m = jnp.max(v, axis=0, keepdims=True)
    idx = jnp.min(jnp.where(v == m, row_iota, n_rows), axis=0, keepdims=True)
    return m, idx


def _route_kernel(lg_ref, bias_ref, tri_ref, idx_ref, gate_ref, rank_ref, cnt_ref, carry):
    i = pl.program_id(0)

    @pl.when(i == 0)
    def _():
        carry[...] = jnp.zeros_like(carry)

    scores = 1.0 / (1.0 + jnp.exp(-lg_ref[...]))
    biased = scores + bias_ref[...]
    e_iota = lax.broadcasted_iota(jnp.int32, scores.shape, 0).astype(F32)
    g_iota = lax.broadcasted_iota(jnp.int32, (EXPERTS_PER_GROUP, TR), 0).astype(F32)
    neg = -jnp.inf

    grp = []
    for g in range(N_EXPERT_GROUPS):
        blk = biased[g * EXPERTS_PER_GROUP:(g + 1) * EXPERTS_PER_GROUP, :]
        m1, i1 = _first_argmax_rows(blk, g_iota, float(EXPERTS_PER_GROUP))
        m2 = jnp.max(jnp.where(g_iota == i1, neg, blk), axis=0, keepdims=True)
        grp.append(m1 + m2)
    grp = jnp.concatenate(grp, axis=0)
    gg_iota = lax.broadcasted_iota(jnp.int32, grp.shape, 0).astype(F32)
    grp_sel = jnp.zeros(grp.shape, F32)
    for _ in range(TOPK_GROUPS):
        _, gi = _first_argmax_rows(grp, gg_iota, float(N_EXPERT_GROUPS))
        hit = gg_iota == gi
        grp_sel = jnp.where(hit, 1.0, grp_sel)
        grp = jnp.where(hit, neg, grp)
    allowed = jnp.concatenate(
        [jnp.broadcast_to(grp_sel[g:g + 1, :], (EXPERTS_PER_GROUP, TR)) for g in range(N_EXPERT_GROUPS)], axis=0)
    masked = jnp.where(allowed > 0.5, biased, neg)

    idxs, gates = [], []
    onehot = jnp.zeros(scores.shape, F32)
    for _ in range(TOP_K):
        _, ei = _first_argmax_rows(masked, e_iota, float(N_EXPERTS))
        hit = e_iota == ei
        idxs.append(ei)
        gates.append(jnp.sum(jnp.where(hit, scores, 0.0), axis=0, keepdims=True))
        onehot = jnp.where(hit, 1.0, onehot)
        masked = jnp.where(hit, neg, masked)
    idx = jnp.concatenate(idxs, axis=0)
    gate = jnp.concatenate(gates, axis=0)
    gate = gate / jnp.sum(gate, axis=0, keepdims=True) * ROUTED_SCALE

    before = _dot(onehot.astype(BF16), tri_ref[...]) + carry[:, 0:1]
    ranks = [jnp.sum(jnp.where(e_iota == idxs[k], before, 0.0), axis=0, keepdims=True) for k in range(TOP_K)]
    idx_ref[...] = idx.astype(jnp.int32)
    gate_ref[...] = gate
    rank_ref[...] = jnp.concatenate(ranks, axis=0).astype(jnp.int32)
    total = carry[...] + jnp.sum(onehot, axis=1, keepdims=True)
    carry[...] = total
    cnt_ref[...] = total


def _route(lg_t, bias, tri):
    e, t = lg_t.shape
    tok = pl.BlockSpec((TOP_K, TR), lambda i: (0, i))
    return pl.pallas_call(
        _route_kernel,
        out_shape=(jax.ShapeDtypeStruct((TOP_K, t), jnp.int32),
                   jax.ShapeDtypeStruct((TOP_K, t), F32),
                   jax.ShapeDtypeStruct((TOP_K, t), jnp.int32),
                   jax.ShapeDtypeStruct((e, LANES), F32)),
        grid=(t // TR,),
        in_specs=[pl.BlockSpec((e, TR), lambda i: (0, i)),
                  pl.BlockSpec((e, 1), lambda i: (0, 0)),
                  pl.BlockSpec((TR, TR), lambda i: (0, 0))],
        out_specs=(tok, tok, tok, pl.BlockSpec((e, LANES), lambda i: (0, 0))),
        scratch_shapes=[pltpu.VMEM((e, LANES), F32)],
        compiler_params=pltpu.CompilerParams(vmem_limit_bytes=VMEM_LIMIT),
        name="route",
    )(lg_t, bias, tri)


def _plan_kernel(n_blocks, size_ref, start_ref, expert_ref, valid_ref, nact_ref):
    def per_expert(e, first_block):
        size = size_ref[e]
        n_blk = (size + TM_EXP - 1) // TM_EXP
        start_ref[e] = first_block * TM_EXP

        def per_block(j, carry):
            expert_ref[first_block + j] = e
            valid_ref[first_block + j] = jnp.minimum(size - j * TM_EXP, TM_EXP)
            return carry

        lax.fori_loop(0, n_blk, per_block, 0)
        return first_block + n_blk

    n_active = lax.fori_loop(0, N_EXPERTS, per_expert, 0)
    nact_ref[0] = n_active

    def unused(i, carry):
        expert_ref[i] = N_EXPERTS - 1
        valid_ref[i] = 0
        return carry

    lax.fori_loop(n_active, n_blocks, unused, 0)


def _plan_blocks(sizes, n_blocks):
    smem = pl.BlockSpec(memory_space=pltpu.SMEM)
    return pl.pallas_call(
        functools.partial(_plan_kernel, n_blocks),
        out_shape=(jax.ShapeDtypeStruct((N_EXPERTS,), jnp.int32),
                   jax.ShapeDtypeStruct((n_blocks,), jnp.int32),
                   jax.ShapeDtypeStruct((n_blocks,), jnp.int32),
                   jax.ShapeDtypeStruct((1,), jnp.int32)),
        in_specs=[smem],
        out_specs=(smem, smem, smem, smem),
        name="plan_blocks",
    )(sizes)


def _dest_kernel(start_ref, idx_ref, rank_ref, dest_ref):
    idx = idx_ref[...]
    dest = rank_ref[...]
    for e in range(N_EXPERTS):
        dest = dest + jnp.where(idx == e, start_ref[e], 0)
    dest_ref[...] = dest


def _dest_rows(pad_start, idx_kt, rank_kt):
    n_k, t = idx_kt.shape
    tile = 4096
    blk = pl.BlockSpec((n_k, tile), lambda i: (0, i))
    return pl.pallas_call(
        _dest_kernel,
        out_shape=jax.ShapeDtypeStruct((n_k, t), jnp.int32),
        grid=(t // tile,),
        in_specs=[pl.BlockSpec(memory_space=pltpu.SMEM), blk, blk],
        out_specs=blk,
        name="dest_rows",
    )(pad_start, idx_kt, rank_kt)


def _sc_mesh():
    return plsc.VectorSubcoreMesh(core_axis_name="c", subcore_axis_name="s")


def _sc_token_base(steps, j):
    worker = lax.axis_index("s") * SC_CORES + lax.axis_index("c")
    return (worker * steps + j) * SC_TOKENS


def _sc_scatter(dest_kt, h2p, n_rows):
    t, width = h2p.shape
    steps = t // (SC_WORKERS * SC_TOKENS)

    @functools.partial(
        pl.kernel, mesh=_sc_mesh(),
        out_type=jax.ShapeDtypeStruct((n_rows, width), U32),
        scratch_types=[pltpu.VMEM((TOP_K, SC_TOKENS), jnp.int32),
                       pltpu.VMEM((SC_TOKENS, width), U32),
                       pltpu.SemaphoreType.DMA],
        name="sc_scatter",
    )
    def body(dest_hbm, h_hbm, xs_hbm, idx_v, rows_v, sem):
        @pl.loop(0, steps)
        def _(j):
            base = _sc_token_base(steps, j)
            pltpu.sync_copy(dest_hbm.at[:, pl.ds(base, SC_TOKENS)], idx_v)
            pltpu.sync_copy(h_hbm.at[pl.ds(base, SC_TOKENS)], rows_v)
            copies = [pltpu.async_copy(rows_v, xs_hbm.at[idx_v.at[k]], sem) for k in range(TOP_K)]
            for cp in copies:
                cp.wait()

    return body(dest_kt, h2p)


def _sc_gather(dest_kt, ys, token0, n_tokens, after):
    n_k = dest_kt.shape[0]
    width = ys.shape[1]
    steps = n_tokens // (SC_WORKERS * SC_TOKENS)

    half = SC_TOKENS // 2
    units = [(k, h) for k in range(n_k) for h in range(2)]
    n_buf = 3

    @functools.partial(
        pl.kernel, mesh=_sc_mesh(),
        out_type=jax.ShapeDtypeStruct((n_k, n_tokens, width), U32),
        scratch_types=[pltpu.VMEM((n_k, SC_TOKENS), jnp.int32),
                       pltpu.VMEM((n_buf, half, width), U32),
                       pltpu.SemaphoreType.DMA((n_buf,))],
        name="sc_gather",
    )
    def body(dest_hbm, ys_hbm, after_hbm, yk_hbm, idx_v, rows_v, sems):
        del after_hbm

        @pl.loop(0, steps)
        def _(j):
            base = _sc_token_base(steps, j)
            pltpu.sync_copy(dest_hbm.at[:, pl.ds(token0 + base, SC_TOKENS)], idx_v)

            def gather(u):
                k, h = units[u]
                slot = u % n_buf
                return pltpu.make_async_copy(ys_hbm.at[idx_v.at[k, pl.ds(h * half, half)]], rows_v.at[slot],
                                             sems.at[slot])

            gather(0).start()
            gather(1).start()
            for u, (k, h) in enumerate(units):
                gather(u).wait()
                pltpu.sync_copy(rows_v.at[u % n_buf], yk_hbm.at[k, pl.ds(base + h * half, half)])
                if u + 2 < len(units):
                    gather(u + 2).start()

    return body(dest_kt, ys, after)


def _experts_kernel(be_ref, valid_ref, nact_ref, xs_ref, wg_ref, wu_ref, wd_ref, ys_ref, wg_bf, wu_bf, wd_bf):
    i = pl.program_id(0)
    prev = be_ref[jnp.maximum(i - 1, 0)]

    @pl.when((i == 0) | (be_ref[i] != prev))
    def _():
        wg_bf[...] = wg_ref[0].astype(BF16)
        wu_bf[...] = wu_ref[0].astype(BF16)
        wd_bf[...] = wd_ref[0].astype(BF16)

    @pl.when(i < nact_ref[0])
    def _():
        row = lax.broadcasted_iota(jnp.int32, xs_ref.shape, 0)
        words = jnp.where(row < valid_ref[i], xs_ref[...], jnp.uint32(0))
        xb = _unpack_words(words).astype(BF16)
        hid = _silu(_dot(xb, wg_bf[...])) * _dot(xb, wu_bf[...])
        ys_ref[...] = _pack_words(_dot(hid.astype(BF16), wd_bf[...]))

    @pl.when(i >= nact_ref[0])
    def _():
        ys_ref[...] = jnp.zeros_like(ys_ref)


def _experts(block_e, block_valid, n_active, xs, w_gate, w_up, w_down):
    n_rows, half = xs.shape
    d = w_gate.shape[1]
    nb = n_rows // TM_EXP
    grid_spec = pltpu.PrefetchScalarGridSpec(
        num_scalar_prefetch=3,
        grid=(nb,),
        in_specs=[pl.BlockSpec((TM_EXP, half), lambda i, be, bv, na: (jnp.minimum(i, na[0] - 1), 0)),
                  pl.BlockSpec((1, d, D_EXPERT), lambda i, be, bv, na: (be[i], 0, 0)),
                  pl.BlockSpec((1, d, D_EXPERT), lambda i, be, bv, na: (be[i], 0, 0)),
                  pl.BlockSpec((1, D_EXPERT, d), lambda i, be, bv, na: (be[i], 0, 0))],
        out_specs=pl.BlockSpec((TM_EXP, half), lambda i, be, bv, na: (i, 0)),
        scratch_shapes=[pltpu.VMEM((d, D_EXPERT), BF16), pltpu.VMEM((d, D_EXPERT), BF16),
                        pltpu.VMEM((D_EXPERT, d), BF16)],
    )
    return pl.pallas_call(
        _experts_kernel,
        out_shape=jax.ShapeDtypeStruct((n_rows, half), U32),
        grid_spec=grid_spec,
        compiler_params=pltpu.CompilerParams(vmem_limit_bytes=VMEM_LIMIT),
        name="experts",
    )(block_e, block_valid, n_active, xs, w_gate, w_up, w_down)


def _combine_kernel(yk_ref, x1_ref, h2_ref, gate_ref, mod_ref, wsg_ref, wsu_ref, wsd_ref, fg_ref, *out_refs):
    out_ref = out_refs[-1]
    hb = _unpack_words(h2_ref[...]).astype(BF16)
    hid = _silu(_dot(hb, wsg_ref[...])) * _dot(hb, wsu_ref[...])
    ffn = _dot(hid.astype(BF16), wsd_ref[...])
    gate = gate_ref[...]
    for k in range(TOP_K):
        ffn = ffn + gate[:, k:k + 1] * _unpack_words(yk_ref[k])
    x2 = x1_ref[...] + mod_ref[0, 5:6, :] * ffn
    ms = jnp.mean(x2 * x2, axis=-1, keepdims=True)
    out_ref[...] = x2 * lax.rsqrt(ms + EPS) * fg_ref[...]


def _combine(yk, token0, out_token0, n_out_tokens, prev_out, x1, h2p, gate_tk, mod, wsg_bf, wsu_bf, wsd_bf, final_g,
             seq_len):
    t, d = n_out_tokens, x1.shape[1]
    tiles_per_seq = seq_len // TF
    tile0 = token0 // TF
    out_tile0 = out_token0 // TF
    in_tok = pl.BlockSpec((TF, d), lambda i: (tile0 + i, 0))
    tok = pl.BlockSpec((TF, d), lambda i: (out_tile0 + i, 0))
    in_specs = [pl.BlockSpec((TOP_K, TF, d // 2), lambda i: (0, i, 0)),
                in_tok, pl.BlockSpec((TF, d // 2), lambda i: (tile0 + i, 0)),
                pl.BlockSpec((TF, TOP_K), lambda i: (tile0 + i, 0)),
                pl.BlockSpec((1, 6, d), lambda i: ((out_tile0 + i) // tiles_per_seq, 0, 0)),
                pl.BlockSpec((d, D_SHARED), lambda i: (0, 0)),
                pl.BlockSpec((d, D_SHARED), lambda i: (0, 0)),
                pl.BlockSpec((D_SHARED, d), lambda i: (0, 0)),
                pl.BlockSpec((1, d), lambda i: (0, 0))]
    args = [yk, x1, h2p, gate_tk, mod, wsg_bf, wsu_bf, wsd_bf, final_g]
    aliases = {}
    if prev_out is not None:
        in_specs.append(pl.BlockSpec(memory_space=pl.ANY))
        args.append(prev_out)
        aliases = {len(args) - 1: 0}
    return pl.pallas_call(
        _combine_kernel,
        out_shape=jax.ShapeDtypeStruct((t, d), F32),
        grid=(yk.shape[1] // TF,),
        in_specs=in_specs,
        out_specs=tok,
        input_output_aliases=aliases,
        compiler_params=pltpu.CompilerParams(vmem_limit_bytes=VMEM_LIMIT),
        name="combine",
    )(*args)


def _rope_tables(n_tokens):
    n_rows = n_tokens // GRID_W
    n_freq = HEAD_DIM // 4
    inv_freq = ROPE_THETA ** (-jnp.arange(n_freq, dtype=F32) / n_freq)
    ang_r = jnp.arange(n_rows).astype(F32)[:, None] * inv_freq[None, :]
    ang_c = jnp.arange(GRID_W).astype(F32)[:, None] * inv_freq[None, :]

    def per_token(row_part, col_part):
        rows = jnp.broadcast_to(row_part[:, None, :], (n_rows, GRID_W, n_freq))
        cols = jnp.broadcast_to(col_part[None, :, :], (n_rows, GRID_W, n_freq))
        return rows.reshape(n_tokens, n_freq), cols.reshape(n_tokens, n_freq)

    cos_r, cos_c = per_token(jnp.cos(ang_r), jnp.cos(ang_c))
    sin_r, sin_c = per_token(jnp.sin(ang_r), jnp.sin(ang_c))
    cos = jnp.concatenate([cos_r, cos_r, cos_c, cos_c], axis=1)
    sin = jnp.concatenate([-sin_r, sin_r, -sin_c, sin_c], axis=1)
    reps = LANES // HEAD_DIM
    return jnp.tile(cos, (1, reps)), jnp.tile(sin, (1, reps))


def _pool_bands():
    i = jnp.arange(QB)[:, None]
    r = jnp.arange(POOL_SLAB)[None, :]
    return jnp.stack([((r >= i + POOL_OFF - w // 2) & (r < i + POOL_OFF + w // 2)).astype(BF16)
                      for w in POOL_WINDOWS])


def kernel(x, c, ctx, c_ctx, w_ada, b_ada, norm1_g, norm2_g, w_in, attn_sink, pool_w, pool_scale, w_out,
           w_router, router_bias, w_gate, w_up, w_down, ws_gate, ws_up, ws_down, final_g):
    b, s, d = x.shape
    t = b * s
    assert w_ada.shape[0] == 1 and d == D_MODEL and s % TQ == 0 and b + 1 <= 8

    c8 = jnp.zeros((8, d), F32).at[:b].set(c).at[b].set(c_ctx)
    mod = _ada(c8, w_ada[0], b_ada[0]).reshape(8, 6, d)
    g1 = norm1_g[0].reshape(1, d)
    g2 = norm2_g[0].reshape(1, d)
    w_in_bf = w_in[0].astype(BF16)
    cos_t, sin_t = _rope_tables(s)

    q, k4, v4, p = _inproj(x, mod, g1, w_in_bf, cos_t, sin_t)
    kc4, vc4 = _ctxproj(ctx, mod[b:b + 1], g1, w_in_bf[:, ATTN_WIDTH:ATTN_WIDTH + 2 * KV_WIDTH])

    wr_t = w_router[0].T
    wr_hi = wr_t.astype(BF16)
    wr_lo = (wr_t - wr_hi.astype(F32)).astype(BF16)
    attn_consts = (_pool_bands(), pool_w[0].astype(BF16), pool_scale[0].reshape(1, POOL_WIDTH),
                   w_out[0].astype(BF16), g2, wr_hi, wr_lo)
    tri = jnp.triu(jnp.ones((TR, TR), BF16), k=1)
    shared_w = (ws_gate[0].astype(BF16), ws_up[0].astype(BF16), ws_down[0].astype(BF16))

    nb = b // TOKEN_GROUPS
    tg = nb * s
    assert b % TOKEN_GROUPS == 0 and tg % (COMBINE_CHUNKS * SC_WORKERS * SC_TOKENS) == 0
    n_rows = -(-(tg * TOP_K + N_EXPERTS * (TM_EXP - 1)) // TM_EXP) * TM_EXP
    groups = []
    for g in range(TOKEN_GROUPS):
        x1, h2p, lg_t = _attn(g * nb, nb, attn_sink[0], q, k4, v4, kc4, vc4, p, x, mod, *attn_consts)
        idx_kt, gate_kt, rank_kt, counts = _route(lg_t, router_bias[0].reshape(N_EXPERTS, 1), tri)
        pad_start, block_e, block_valid, n_active = _plan_blocks(counts[:, 0].astype(jnp.int32), n_rows // TM_EXP)
        dest_kt = _dest_rows(pad_start, idx_kt, rank_kt)
        xs = _sc_scatter(dest_kt, h2p, n_rows)
        groups.append((x1.reshape(tg, d), h2p, gate_kt.T, dest_kt, xs, block_e, block_valid, n_active))

    ys_all = [_experts(block_e, block_valid, n_active, xs, w_gate[0], w_up[0], w_down[0])
              for (_, _, _, _, xs, block_e, block_valid, n_active) in groups]
    out = None
    chunk = tg // COMBINE_CHUNKS
    for g, (x1, h2p, gate_tk, dest_kt, *_) in enumerate(groups):
        for token0 in range(0, tg, chunk):
            yk = _sc_gather(dest_kt, ys_all[g], token0, chunk, after=ys_all[-1])
            out = _combine(yk, token0, g * tg + token0, t, out, x1, h2p, gate_tk, mod, *shared_w,
                           final_g.reshape(1, d), s)
    return out.reshape(b, s, d)
```

```python
import functools

import jax
import jax.numpy as jnp
from jax import lax
from jax.experimental import pallas as pl
from jax.experimental.pallas import tpu as pltpu
from jax.experimental.pallas import tpu_sc as plsc

F32 = jnp.float32
BF16 = jnp.bfloat16

D_MODEL = 1024
GRID_W = 64
N_HEADS = 8
N_KV_HEADS = 2
HEAD_DIM = 64
ATTN_WIDTH = N_HEADS * HEAD_DIM
KV_WIDTH = N_KV_HEADS * HEAD_DIM
WINDOW = 128
ROPE_THETA = 10000.0
POOL_WINDOWS = (2, 4, 8, 16)
POOL_WIDTH = D_MODEL - ATTN_WIDTH
POOL_GROUP_DIM = POOL_WIDTH // len(POOL_WINDOWS)
IN_COLS = ATTN_WIDTH + 2 * KV_WIDTH + POOL_WIDTH
N_EXPERTS = 64
TOP_K = 8
N_EXPERT_GROUPS = 8
EXPERTS_PER_GROUP = N_EXPERTS // N_EXPERT_GROUPS
TOPK_GROUPS = 4
D_EXPERT = 256
D_SHARED = 256
ROUTED_SCALE = 2.5
EPS = 1e-6
LOG2E = 1.4426950408889634

LANES = 128
U32 = jnp.uint32
VMEM_LIMIT = 48 * 1024 * 1024

TM_PROJ = 512
TQ = 512
QB = 128
POOL_SLAB = 256
POOL_OFF = 64
TR = 512
TM_EXP = 1024
EXP_MIN_ROWS = 256
TF = 256
TOKEN_GROUPS = 1
COMBINE_CHUNKS = 4
SC_CORES = 2
SC_WORKERS = 32
SC_TOKENS = 128


def _silu(x):
    return x * (1.0 / (1.0 + jnp.exp(-x)))


def _split_bf16(x):
    hi = x.astype(BF16)
    lo = (x - hi.astype(F32)).astype(BF16)
    return hi, lo


def _dot(a, b):
    return jnp.dot(a, b, preferred_element_type=F32)


def _pack_words(val):
    half = val.shape[1] // 2
    lo = lax.bitcast_convert_type(val[:, :half].astype(BF16).astype(F32), U32)
    hi = lax.bitcast_convert_type(val[:, half:].astype(BF16).astype(F32), U32)
    return lax.shift_right_logical(lo, jnp.uint32(16)) | hi


def _unpack_words(words):
    lo = lax.bitcast_convert_type(lax.shift_left(words, jnp.uint32(16)), F32)
    hi = lax.bitcast_convert_type(words & jnp.uint32(0xFFFF0000), F32)
    return jnp.concatenate([lo, hi], axis=1)


def _dot_nt(a, b):
    return lax.dot_general(a, b, (((1,), (1,)), ((), ())), preferred_element_type=F32)


def _ada_kernel(c_ref, w_ref, b_ref, o_ref):
    a_hi, a_lo = _split_bf16(_silu(c_ref[...]))
    w_hi, w_lo = _split_bf16(w_ref[...])
    o_ref[...] = _dot(a_hi, w_hi) + _dot(a_lo, w_hi) + _dot(a_hi, w_lo) + b_ref[...]


def _ada(c8, w_ada, b_ada):
    d = c8.shape[1]
    n = w_ada.shape[1]
    tn = 512
    return pl.pallas_call(
        _ada_kernel,
        out_shape=jax.ShapeDtypeStruct((8, n), F32),
        grid=(n // tn,),
        in_specs=[pl.BlockSpec((8, d), lambda j: (0, 0)),
                  pl.BlockSpec((d, tn), lambda j: (0, j)),
                  pl.BlockSpec((1, tn), lambda j: (0, j))],
        out_specs=pl.BlockSpec((8, tn), lambda j: (0, j)),
        compiler_params=pltpu.CompilerParams(vmem_limit_bytes=VMEM_LIMIT),
        name="ada",
    )(c8, w_ada, b_ada.reshape(1, n))


def _norm_mod(x, g, shift, scale):
    ms = jnp.mean(x * x, axis=-1, keepdims=True)
    return (x * lax.rsqrt(ms + EPS) * g) * (1.0 + scale) + shift


def _lane_variants(t):
    lane = lax.broadcasted_iota(jnp.int32, t.shape, 1)
    lo = lane < HEAD_DIM
    tr = pltpu.roll(t, HEAD_DIM, 1)
    zero = jnp.zeros_like(t)
    return (jnp.where(lo, t, zero), jnp.where(lo, zero, tr),
            jnp.where(lo, tr, zero), jnp.where(lo, zero, t))


def _store_variants(ref, t):
    for i, var in enumerate(_lane_variants(t)):
        ref[0, :, i * LANES:(i + 1) * LANES] = var.astype(BF16)


def _inproj_kernel(x_ref, mod_ref, g_ref, w_ref, cos_ref, sin_ref, q_ref, k_ref, v_ref, p_ref):
    h = _norm_mod(x_ref[0], g_ref[...], mod_ref[0, 0:1, :], mod_ref[0, 1:2, :])
    z = _dot(h.astype(BF16), w_ref[...])
    cos = cos_ref[...]
    sin = sin_ref[...]
    lane = lax.broadcasted_iota(jnp.int32, cos.shape, 1)
    first_half = (lane & 16) == 0

    def rope(zc):
        partner = jnp.where(first_half, pltpu.roll(zc, LANES - 16, 1), pltpu.roll(zc, 16, 1))
        return zc * cos + partner * sin

    scale = HEAD_DIM ** -0.5 * LOG2E
    for c in range(ATTN_WIDTH // LANES):
        q_ref[0, :, c * LANES:(c + 1) * LANES] = (rope(z[:, c * LANES:(c + 1) * LANES]) * scale).astype(BF16)
    _store_variants(k_ref, rope(z[:, ATTN_WIDTH:ATTN_WIDTH + KV_WIDTH]))
    _store_variants(v_ref, z[:, ATTN_WIDTH + KV_WIDTH:ATTN_WIDTH + 2 * KV_WIDTH])
    p_ref[0] = z[:, ATTN_WIDTH + 2 * KV_WIDTH:]


def _inproj(x, mod, g1, w_in_bf, cos_t, sin_t):
    b, s, d = x.shape
    tm = TM_PROJ
    return pl.pallas_call(
        _inproj_kernel,
        out_shape=(jax.ShapeDtypeStruct((b, s, ATTN_WIDTH), BF16),
                   jax.ShapeDtypeStruct((b, s, 4 * LANES), BF16),
                   jax.ShapeDtypeStruct((b, s, 4 * LANES), BF16),
                   jax.ShapeDtypeStruct((b, s, POOL_WIDTH), F32)),
        grid=(s // tm, b),
        in_specs=[pl.BlockSpec((1, tm, d), lambda n, bi: (bi, n, 0)),
                  pl.BlockSpec((1, 6, d), lambda n, bi: (bi, 0, 0)),
                  pl.BlockSpec((1, d), lambda n, bi: (0, 0)),
                  pl.BlockSpec((d, IN_COLS), lambda n, bi: (0, 0)),
                  pl.BlockSpec((tm, LANES), lambda n, bi: (n, 0)),
                  pl.BlockSpec((tm, LANES), lambda n, bi: (n, 0))],
        out_specs=(pl.BlockSpec((1, tm, ATTN_WIDTH), lambda n, bi: (bi, n, 0)),
                   pl.BlockSpec((1, tm, 4 * LANES), lambda n, bi: (bi, n, 0)),
                   pl.BlockSpec((1, tm, 4 * LANES), lambda n, bi: (bi, n, 0)),
                   pl.BlockSpec((1, tm, POOL_WIDTH), lambda n, bi: (bi, n, 0))),
        compiler_params=pltpu.CompilerParams(vmem_limit_bytes=VMEM_LIMIT),
        name="inproj",
    )(x, mod, g1, w_in_bf, cos_t, sin_t)


def _ctxproj_kernel(x_ref, mod_ref, g_ref, w_ref, k_ref, v_ref):
    h = _norm_mod(x_ref[0], g_ref[...], mod_ref[0, 0:1, :], mod_ref[0, 1:2, :])
    z = _dot(h.astype(BF16), w_ref[...])
    _store_variants(k_ref, z[:, :KV_WIDTH])
    _store_variants(v_ref, z[:, KV_WIDTH:])


def _ctxproj(ctx, mod_c, g1, w_kv_bf):
    b, c, d = ctx.shape
    return pl.pallas_call(
        _ctxproj_kernel,
        out_shape=(jax.ShapeDtypeStruct((b, c, 4 * LANES), BF16),
                   jax.ShapeDtypeStruct((b, c, 4 * LANES), BF16)),
        grid=(b,),
        in_specs=[pl.BlockSpec((1, c, d), lambda bi: (bi, 0, 0)),
                  pl.BlockSpec((1, 6, d), lambda bi: (0, 0, 0)),
                  pl.BlockSpec((1, d), lambda bi: (0, 0)),
                  pl.BlockSpec((d, 2 * KV_WIDTH), lambda bi: (0, 0))],
        out_specs=(pl.BlockSpec((1, c, 4 * LANES), lambda bi: (bi, 0, 0)),
                   pl.BlockSpec((1, c, 4 * LANES), lambda bi: (bi, 0, 0))),
        compiler_params=pltpu.CompilerParams(vmem_limit_bytes=VMEM_LIMIT),
        name="ctxproj",
    )(ctx, mod_c, g1, w_kv_bf)


def _fold(op, tiles):
    while len(tiles) > 1:
        tiles = [op(tiles[i], tiles[i + 1]) if i + 1 < len(tiles) else tiles[i] for i in range(0, len(tiles), 2)]
    return tiles[0]


def _stack_variants(t4, kv):
    return jnp.concatenate([t4[:, (2 * kv) * LANES:(2 * kv + 1) * LANES],
                            t4[:, (2 * kv + 1) * LANES:(2 * kv + 2) * LANES]], axis=0)


def _attn_kernel(seq_len, sink_ref, q_ref, k_ref, kp_ref, kn_ref, v_ref, vp_ref, vn_ref, kc_ref, vc_ref,
                 p_ref, pp_ref, pn_ref, x_ref, mod_ref, band_ref, poolw_ref, pscale_ref, wout_ref,
                 g2_ref, wrh_ref, wrl_ref, x1_ref, h2_ref, lg_ref, kwin, vwin, pext, mix, s_scr, p_scr, m_scr):
    n = pl.program_id(1)
    n_last = pl.num_programs(1) - 1

    kwin[0:QB, :] = kp_ref[0]
    kwin[QB:QB + TQ, :] = k_ref[0]
    kwin[QB + TQ:, :] = kn_ref[0]
    vwin[0:QB, :] = vp_ref[0]
    vwin[QB:QB + TQ, :] = v_ref[0]
    vwin[QB + TQ:, :] = vn_ref[0]

    pext[0:QB - 8, :] = jnp.zeros((QB - 8, POOL_WIDTH), F32)
    pext[QB - 8:QB, :] = jnp.where(n > 0, pp_ref[0], 0.0)
    pext[QB:QB + TQ, :] = p_ref[0]
    pext[QB + TQ:QB + TQ + 8, :] = jnp.where(n < n_last, pn_ref[0], 0.0)
    pext[QB + TQ + 8:, :] = jnp.zeros((QB - 8, POOL_WIDTH), F32)

    row = lax.broadcasted_iota(jnp.int32, (QB, 3 * QB), 0)
    col = lax.broadcasted_iota(jnp.int32, (QB, 3 * QB), 1)
    in_band = (col >= row) & (col <= row + 2 * WINDOW)
    tok = lax.broadcasted_iota(jnp.int32, (QB, 1), 0)
    kc = kc_ref[0]
    kc_rows = [_stack_variants(kc, kv) for kv in range(N_KV_HEADS)]
    vc_rows = [_stack_variants(vc_ref[0], kv) for kv in range(N_KV_HEADS)]

    def sub_block(j, carry):
        r0 = pl.multiple_of(j * QB, QB)
        qj = q_ref[0, pl.ds(r0, QB), :]
        kw = kwin[pl.ds(r0, 3 * QB), :]
        vw = vwin[pl.ds(r0, 3 * QB), :]
        kpos = col + (n * TQ + j * QB - QB)
        ok = in_band & (kpos >= 0) & (kpos < seq_len)
        bias = jnp.where(ok, 0.0, -jnp.inf)
        n_loc, n_ctx = 3 * QB, kc.shape[0]
        bias2 = jnp.concatenate([bias, bias], axis=1)
        k_rows = [_stack_variants(kw, kv) for kv in range(N_KV_HEADS)]
        v_rows = [_stack_variants(vw, kv) for kv in range(N_KV_HEADS)]
        group = N_HEADS // N_KV_HEADS

        def head_tiles(head):
            loc0 = (head % 2) * n_loc
            ctx0 = 2 * n_loc + (head % 2) * n_ctx
            return ([loc0 + i * LANES for i in range(n_loc // LANES)]
                    + [ctx0 + i * LANES for i in range(n_ctx // LANES)])

        for c in range(N_HEADS // 2):
            qc = qj[:, c * LANES:(c + 1) * LANES]
            s_scr[c, :, 0:2 * n_loc] = _dot_nt(qc, k_rows[2 * c // group]) + bias2
            s_scr[c, :, 2 * n_loc:] = _dot_nt(qc, kc_rows[2 * c // group])
        for head in range(N_HEADS):
            tiles = [s_scr[head // 2, :, st:st + LANES] for st in head_tiles(head)]
            row_max = jnp.max(_fold(jnp.maximum, tiles), axis=1, keepdims=True)
            m_scr[head] = jnp.broadcast_to(jnp.maximum(row_max, sink_ref[head] * LOG2E), (QB, LANES))
        for head in range(N_HEADS):
            m = m_scr[head]
            acc = None
            for st in head_tiles(head):
                p = jnp.exp2(s_scr[head // 2, :, st:st + LANES] - m)
                p_scr[head // 2, :, st:st + LANES] = p.astype(BF16)
                acc = p if acc is None else acc + p
            denom = (jnp.broadcast_to(jnp.sum(acc, axis=1, keepdims=True), (QB, LANES))
                     + jnp.exp2(sink_ref[head] * LOG2E - m))
            m_scr[head] = 1.0 / denom
        lane = lax.broadcasted_iota(jnp.int32, (QB, LANES), 1)
        for c in range(N_HEADS // 2):
            o = (_dot(p_scr[c, :, 0:2 * n_loc], v_rows[2 * c // group])
                 + _dot(p_scr[c, :, 2 * n_loc:], vc_rows[2 * c // group]))
            o = o * jnp.where(lane < HEAD_DIM, m_scr[2 * c], m_scr[2 * c + 1])
            mix[pl.ds(r0, QB), c * LANES:(c + 1) * LANES] = o.astype(BF16)

        slab = pext[pl.ds(pl.multiple_of(r0 + POOL_OFF, 8), POOL_SLAB), :]
        tpos = tok + (n * TQ + j * QB)
        for g, w in enumerate(POOL_WINDOWS):
            sg = slab[:, g * LANES:(g + 1) * LANES]
            hi, lo = _split_bf16(sg)
            band = band_ref[g]
            wsum = _dot(band, hi) + _dot(band, lo)
            cnt = (jnp.minimum(tpos - w // 2 + w, seq_len) - jnp.maximum(tpos - w // 2, 0)).astype(F32)
            dlt = wsum / cnt - sg[POOL_OFF:POOL_OFF + QB, :]
            y = _dot(dlt.astype(BF16), poolw_ref[g]) * pscale_ref[:, g * LANES:(g + 1) * LANES]
            mix[pl.ds(r0, QB), ATTN_WIDTH + g * LANES:ATTN_WIDTH + (g + 1) * LANES] = y.astype(BF16)
        return carry

    lax.fori_loop(0, TQ // QB, sub_block, 0)

    proj = _dot(mix[...], wout_ref[...])
    x1 = x_ref[0] + mod_ref[0, 2:3, :] * proj
    x1_ref[0] = x1
    h2 = _norm_mod(x1, g2_ref[...], mod_ref[0, 3:4, :], mod_ref[0, 4:5, :])
    h2_ref[...] = _pack_words(h2)
    h_hi, h_lo = _split_bf16(h2)
    wrh = wrh_ref[...]
    lg_ref[...] = _dot_nt(wrh, h_hi) + _dot_nt(wrh, h_lo) + _dot_nt(wrl_ref[...], h_hi)


def _attn(b0, b, sink, q, k4, v4, kc4, vc4, p, x, mod, band, poolw_bf, pscale, wout_bf, g2, wr_hi, wr_lo):
    _, s, d = x.shape
    c = kc4.shape[1]
    nt = s // TQ
    hb = TQ // QB
    pb = TQ // 8
    kv_main = pl.BlockSpec((1, TQ, 4 * LANES), lambda bi, n: (b0 + bi, n, 0))
    kv_prev = pl.BlockSpec((1, QB, 4 * LANES), lambda bi, n: (b0 + bi, jnp.maximum(n * hb - 1, 0), 0))
    kv_next = pl.BlockSpec((1, QB, 4 * LANES), lambda bi, n: (b0 + bi, jnp.minimum((n + 1) * hb, s // QB - 1), 0))
    const2 = lambda bi, n: (0, 0)
    const3 = lambda bi, n: (0, 0, 0)
    return pl.pallas_call(
        functools.partial(_attn_kernel, s),
        out_shape=(jax.ShapeDtypeStruct((b, s, d), F32),
                   jax.ShapeDtypeStruct((b * s, d // 2), U32),
                   jax.ShapeDtypeStruct((N_EXPERTS, b * s), F32)),
        grid=(b, nt),
        in_specs=[pl.BlockSpec(memory_space=pltpu.SMEM),
                  pl.BlockSpec((1, TQ, ATTN_WIDTH), lambda bi, n: (b0 + bi, n, 0)),
                  kv_main, kv_prev, kv_next, kv_main, kv_prev, kv_next,
                  pl.BlockSpec((1, c, 4 * LANES), lambda bi, n: (b0 + bi, 0, 0)),
                  pl.BlockSpec((1, c, 4 * LANES), lambda bi, n: (b0 + bi, 0, 0)),
                  pl.BlockSpec((1, TQ, POOL_WIDTH), lambda bi, n: (b0 + bi, n, 0)),
                  pl.BlockSpec((1, 8, POOL_WIDTH), lambda bi, n: (b0 + bi, jnp.maximum(n * pb - 1, 0), 0)),
                  pl.BlockSpec((1, 8, POOL_WIDTH),
                               lambda bi, n: (b0 + bi, jnp.minimum((n + 1) * pb, s // 8 - 1), 0)),
                  pl.BlockSpec((1, TQ, d), lambda bi, n: (b0 + bi, n, 0)),
                  pl.BlockSpec((1, 6, d), lambda bi, n: (b0 + bi, 0, 0)),
                  pl.BlockSpec((len(POOL_WINDOWS), QB, POOL_SLAB), const3),
                  pl.BlockSpec((len(POOL_WINDOWS), POOL_GROUP_DIM, POOL_GROUP_DIM), const3),
                  pl.BlockSpec((1, POOL_WIDTH), const2),
                  pl.BlockSpec((d, d), const2),
                  pl.BlockSpec((1, d), const2),
                  pl.BlockSpec((N_EXPERTS, d), const2),
                  pl.BlockSpec((N_EXPERTS, d), const2)],
        out_specs=(pl.BlockSpec((1, TQ, d), lambda bi, n: (bi, n, 0)),
                   pl.BlockSpec((TQ, d // 2), lambda bi, n: (bi * nt + n, 0)),
                   pl.BlockSpec((N_EXPERTS, TQ), lambda bi, n: (0, bi * nt + n))),
        scratch_shapes=[pltpu.VMEM((TQ + 2 * QB, 4 * LANES), BF16),
                        pltpu.VMEM((TQ + 2 * QB, 4 * LANES), BF16),
                        pltpu.VMEM((TQ + 2 * QB, POOL_WIDTH), F32),
                        pltpu.VMEM((TQ, d), BF16),
                        pltpu.VMEM((N_HEADS // 2, QB, 2 * (3 * QB + c)), F32),
                        pltpu.VMEM((N_HEADS // 2, QB, 2 * (3 * QB + c)), BF16),
                        pltpu.VMEM((N_HEADS, QB, LANES), F32)],
        compiler_params=pltpu.CompilerParams(vmem_limit_bytes=VMEM_LIMIT),
        name="attn",
    )(sink, q, k4, k4, k4, v4, v4, v4, kc4, vc4, p, p, p, x, mod, band, poolw_bf, pscale, wout_bf,
      g2, wr_hi, wr_lo)


def _first_argmax_rows(v, row_iota, n_rows):
    m = jnp.max(v, axis=0, keepdims=True)
    idx = jnp.min(jnp.where(v == m, row_iota, n_rows), axis=0, keepdims=True)
    return m, idx


def _route_kernel(lg_ref, bias_ref, tri_ref, idx_ref, gate_ref, rank_ref, cnt_ref, carry):
    i = pl.program_id(0)

    @pl.when(i == 0)
    def _():
        carry[...] = jnp.zeros_like(carry)

    scores = 1.0 / (1.0 + jnp.exp(-lg_ref[...]))
    biased = scores + bias_ref[...]
    e_iota = lax.broadcasted_iota(jnp.int32, scores.shape, 0).astype(F32)
    g_iota = lax.broadcasted_iota(jnp.int32, (EXPERTS_PER_GROUP, TR), 0).astype(F32)
    neg = -jnp.inf

    grp = []
    for g in range(N_EXPERT_GROUPS):
        blk = biased[g * EXPERTS_PER_GROUP:(g + 1) * EXPERTS_PER_GROUP, :]
        m1, i1 = _first_argmax_rows(blk, g_iota, float(EXPERTS_PER_GROUP))
        m2 = jnp.max(jnp.where(g_iota == i1, neg, blk), axis=0, keepdims=True)
        grp.append(m1 + m2)
    grp = jnp.concatenate(grp, axis=0)
    gg_iota = lax.broadcasted_iota(jnp.int32, grp.shape, 0).astype(F32)
    grp_sel = jnp.zeros(grp.shape, F32)
    for _ in range(TOPK_GROUPS):
        _, gi = _first_argmax_rows(grp, gg_iota, float(N_EXPERT_GROUPS))
        hit = gg_iota == gi
        grp_sel = jnp.where(hit, 1.0, grp_sel)
        grp = jnp.where(hit, neg, grp)
    allowed = jnp.concatenate(
        [jnp.broadcast_to(grp_sel[g:g + 1, :], (EXPERTS_PER_GROUP, TR)) for g in range(N_EXPERT_GROUPS)], axis=0)
    masked = jnp.where(allowed > 0.5, biased, neg)

    idxs, gates = [], []
    onehot = jnp.zeros(scores.shape, F32)
    for _ in range(TOP_K):
        _, ei = _first_argmax_rows(masked, e_iota, float(N_EXPERTS))
        hit = e_iota == ei
        idxs.append(ei)
        gates.append(jnp.sum(jnp.where(hit, scores, 0.0), axis=0, keepdims=True))
        onehot = jnp.where(hit, 1.0, onehot)
        masked = jnp.where(hit, neg, masked)
    idx = jnp.concatenate(idxs, axis=0)
    gate = jnp.concatenate(gates, axis=0)
    gate = gate / jnp.sum(gate, axis=0, keepdims=True) * ROUTED_SCALE

    before = _dot(onehot.astype(BF16), tri_ref[...]) + carry[:, 0:1]
    ranks = [jnp.sum(jnp.where(e_iota == idxs[k], before, 0.0), axis=0, keepdims=True) for k in range(TOP_K)]
    idx_ref[...] = idx.astype(jnp.int32)
    gate_ref[...] = gate
    rank_ref[...] = jnp.concatenate(ranks, axis=0).astype(jnp.int32)
    total = carry[...] + jnp.sum(onehot, axis=1, keepdims=True)
    carry[...] = total
    cnt_ref[...] = total


def _route(lg_t, bias, tri):
    e, t = lg_t.shape
    tok = pl.BlockSpec((TOP_K, TR), lambda i: (0, i))
    return pl.pallas_call(
        _route_kernel,
        out_shape=(jax.ShapeDtypeStruct((TOP_K, t), jnp.int32),
                   jax.ShapeDtypeStruct((TOP_K, t), F32),
                   jax.ShapeDtypeStruct((TOP_K, t), jnp.int32),
                   jax.ShapeDtypeStruct((e, LANES), F32)),
        grid=(t // TR,),
        in_specs=[pl.BlockSpec((e, TR), lambda i: (0, i)),
                  pl.BlockSpec((e, 1), lambda i: (0, 0)),
                  pl.BlockSpec((TR, TR), lambda i: (0, 0))],
        out_specs=(tok, tok, tok, pl.BlockSpec((e, LANES), lambda i: (0, 0))),
        scratch_shapes=[pltpu.VMEM((e, LANES), F32)],
        compiler_params=pltpu.CompilerParams(vmem_limit_bytes=VMEM_LIMIT),
        name="route",
    )(lg_t, bias, tri)


def _plan_kernel(n_blocks, size_ref, start_ref, expert_ref, valid_ref, nact_ref):
    def per_expert(e, first_block):
        size = size_ref[e]
        n_blk = (size + TM_EXP - 1) // TM_EXP
        start_ref[e] = first_block * TM_EXP

        def per_block(j, carry):
            expert_ref[first_block + j] = e
            valid_ref[first_block + j] = jnp.minimum(size - j * TM_EXP, TM_EXP)
            return carry

        lax.fori_loop(0, n_blk, per_block, 0)
        return first_block + n_blk

    n_active = lax.fori_loop(0, N_EXPERTS, per_expert, 0)
    nact_ref[0] = n_active

    def unused(i, carry):
        expert_ref[i] = N_EXPERTS - 1
        valid_ref[i] = 0
        return carry

    lax.fori_loop(n_active, n_blocks, unused, 0)


def _plan_blocks(sizes, n_blocks):
    smem = pl.BlockSpec(memory_space=pltpu.SMEM)
    return pl.pallas_call(
        functools.partial(_plan_kernel, n_blocks),
        out_shape=(jax.ShapeDtypeStruct((N_EXPERTS,), jnp.int32),
                   jax.ShapeDtypeStruct((n_blocks,), jnp.int32),
                   jax.ShapeDtypeStruct((n_blocks,), jnp.int32),
                   jax.ShapeDtypeStruct((1,), jnp.int32)),
        in_specs=[smem],
        out_specs=(smem, smem, smem, smem),
        name="plan_blocks",
    )(sizes)


def _dest_kernel(start_ref, idx_ref, rank_ref, dest_ref):
    idx = idx_ref[...]
    dest = rank_ref[...]
    for e in range(N_EXPERTS):
        dest = dest + jnp.where(idx == e, start_ref[e], 0)
    dest_ref[...] = dest


def _dest_rows(pad_start, idx_kt, rank_kt):
    n_k, t = idx_kt.shape
    tile = 4096
    blk = pl.BlockSpec((n_k, tile), lambda i: (0, i))
    return pl.pallas_call(
        _dest_kernel,
        out_shape=jax.ShapeDtypeStruct((n_k, t), jnp.int32),
        grid=(t // tile,),
        in_specs=[pl.BlockSpec(memory_space=pltpu.SMEM), blk, blk],
        out_specs=blk,
        name="dest_rows",
    )(pad_start, idx_kt, rank_kt)


def _sc_mesh():
    return plsc.VectorSubcoreMesh(core_axis_name="c", subcore_axis_name="s")


def _sc_token_base(steps, j):
    worker = lax.axis_index("s") * SC_CORES + lax.axis_index("c")
    return (worker * steps + j) * SC_TOKENS


def _sc_scatter(dest_kt, h2p, n_rows):
    t, width = h2p.shape
    steps = t // (SC_WORKERS * SC_TOKENS)

    @functools.partial(
        pl.kernel, mesh=_sc_mesh(),
        out_type=jax.ShapeDtypeStruct((n_rows, width), U32),
        scratch_types=[pltpu.VMEM((TOP_K, SC_TOKENS), jnp.int32),
                       pltpu.VMEM((SC_TOKENS, width), U32),
                       pltpu.SemaphoreType.DMA],
        name="sc_scatter",
    )
    def body(dest_hbm, h_hbm, xs_hbm, idx_v, rows_v, sem):
        @pl.loop(0, steps)
        def _(j):
            base = _sc_token_base(steps, j)
            pltpu.sync_copy(dest_hbm.at[:, pl.ds(base, SC_TOKENS)], idx_v)
            pltpu.sync_copy(h_hbm.at[pl.ds(base, SC_TOKENS)], rows_v)
            copies = [pltpu.async_copy(rows_v, xs_hbm.at[idx_v.at[k]], sem) for k in range(TOP_K)]
            for cp in copies:
                cp.wait()

    return body(dest_kt, h2p)


def _sc_gather(dest_kt, ys, token0, n_tokens):
    n_k = dest_kt.shape[0]
    width = ys.shape[1]
    steps = n_tokens // (SC_WORKERS * SC_TOKENS)

    half = SC_TOKENS // 2
    units = [(k, h) for k in range(n_k) for h in range(2)]
    n_buf = 3

    @functools.partial(
        pl.kernel, mesh=_sc_mesh(),
        out_type=jax.ShapeDtypeStruct((n_k, n_tokens, width), U32),
        scratch_types=[pltpu.VMEM((n_k, SC_TOKENS), jnp.int32),
                       pltpu.VMEM((n_buf, half, width), U32),
                       pltpu.SemaphoreType.DMA((n_buf,))],
        name="sc_gather",
    )
    def body(dest_hbm, ys_hbm, yk_hbm, idx_v, rows_v, sems):
        @pl.loop(0, steps)
        def _(j):
            base = _sc_token_base(steps, j)
            pltpu.sync_copy(dest_hbm.at[:, pl.ds(token0 + base, SC_TOKENS)], idx_v)

            def gather(u):
                k, h = units[u]
                slot = u % n_buf
                return pltpu.make_async_copy(ys_hbm.at[idx_v.at[k, pl.ds(h * half, half)]], rows_v.at[slot],
                                             sems.at[slot])

            gather(0).start()
            gather(1).start()
            for u, (k, h) in enumerate(units):
                gather(u).wait()
                pltpu.sync_copy(rows_v.at[u % n_buf], yk_hbm.at[k, pl.ds(base + h * half, half)])
                if u + 2 < len(units):
                    gather(u + 2).start()

    return body(dest_kt, ys)


def _experts_kernel(be_ref, valid_ref, nact_ref, xs_ref, wg_ref, wu_ref, wd_ref, ys_ref, wg_bf, wu_bf, wd_bf):
    i = pl.program_id(0)
    prev = be_ref[jnp.maximum(i - 1, 0)]

    @pl.when((i == 0) | (be_ref[i] != prev))
    def _():
        wg_bf[...] = wg_ref[0].astype(BF16)
        wu_bf[...] = wu_ref[0].astype(BF16)
        wd_bf[...] = wd_ref[0].astype(BF16)

    valid = jnp.where(i < nact_ref[0], valid_ref[i], 0)

    def run(rows):
        row = lax.broadcasted_iota(jnp.int32, (rows, xs_ref.shape[1]), 0)
        words = jnp.where(row < valid, xs_ref[0:rows, :], jnp.uint32(0))
        xb = _unpack_words(words).astype(BF16)
        hid = _silu(_dot(xb, wg_bf[...])) * _dot(xb, wu_bf[...])
        ys_ref[0:rows, :] = _pack_words(_dot(hid.astype(BF16), wd_bf[...]))
        if rows < TM_EXP:
            ys_ref[rows:, :] = jnp.zeros((TM_EXP - rows, ys_ref.shape[1]), U32)

    rows, lower = TM_EXP, TM_EXP // 2
    while rows >= EXP_MIN_ROWS:
        lo = lower if rows > EXP_MIN_ROWS else 0
        pl.when((valid > lo) & (valid <= rows))(functools.partial(run, rows))
        rows, lower = rows // 2, lower // 2

    @pl.when(valid == 0)
    def _():
        ys_ref[...] = jnp.zeros_like(ys_ref)


def _experts(block_e, block_valid, n_active, xs, w_gate, w_up, w_down):
    n_rows, half = xs.shape
    d = w_gate.shape[1]
    nb = n_rows // TM_EXP
    grid_spec = pltpu.PrefetchScalarGridSpec(
        num_scalar_prefetch=3,
        grid=(nb,),
        in_specs=[pl.BlockSpec((TM_EXP, half), lambda i, be, bv, na: (jnp.minimum(i, na[0] - 1), 0)),
                  pl.BlockSpec((1, d, D_EXPERT), lambda i, be, bv, na: (be[i], 0, 0)),
                  pl.BlockSpec((1, d, D_EXPERT), lambda i, be, bv, na: (be[i], 0, 0)),
                  pl.BlockSpec((1, D_EXPERT, d), lambda i, be, bv, na: (be[i], 0, 0))],
        out_specs=pl.BlockSpec((TM_EXP, half), lambda i, be, bv, na: (i, 0)),
        scratch_shapes=[pltpu.VMEM((d, D_EXPERT), BF16), pltpu.VMEM((d, D_EXPERT), BF16),
                        pltpu.VMEM((D_EXPERT, d), BF16)],
    )
    return pl.pallas_call(
        _experts_kernel,
        out_shape=jax.ShapeDtypeStruct((n_rows, half), U32),
        grid_spec=grid_spec,
        compiler_params=pltpu.CompilerParams(vmem_limit_bytes=VMEM_LIMIT),
        name="experts",
    )(block_e, block_valid, n_active, xs, w_gate, w_up, w_down)


def _combine_kernel(yk_ref, x1_ref, h2_ref, gate_ref, mod_ref, wsg_ref, wsu_ref, wsd_ref, fg_ref, *out_refs):
    out_ref = out_refs[-1]
    hb = _unpack_words(h2_ref[...]).astype(BF16)
    hid = _silu(_dot(hb, wsg_ref[...])) * _dot(hb, wsu_ref[...])
    ffn = _dot(hid.astype(BF16), wsd_ref[...])
    gate = gate_ref[...]
    for k in range(TOP_K):
        ffn = ffn + gate[:, k:k + 1] * _unpack_words(yk_ref[k])
    x2 = x1_ref[...] + mod_ref[0, 5:6, :] * ffn
    ms = jnp.mean(x2 * x2, axis=-1, keepdims=True)
    out_ref[...] = x2 * lax.rsqrt(ms + EPS) * fg_ref[...]


def _combine(yk, token0, out_token0, n_out_tokens, prev_out, x1, h2p, gate_tk, mod, wsg_bf, wsu_bf, wsd_bf, final_g,
             seq_len):
    t, d = n_out_tokens, x1.shape[1]
    tiles_per_seq = seq_len // TF
    tile0 = token0 // TF
    out_tile0 = out_token0 // TF
    in_tok = pl.BlockSpec((TF, d), lambda i: (tile0 + i, 0))
    tok = pl.BlockSpec((TF, d), lambda i: (out_tile0 + i, 0))
    in_specs = [pl.BlockSpec((TOP_K, TF, d // 2), lambda i: (0, i, 0)),
                in_tok, pl.BlockSpec((TF, d // 2), lambda i: (tile0 + i, 0)),
                pl.BlockSpec((TF, TOP_K), lambda i: (tile0 + i, 0)),
                pl.BlockSpec((1, 6, d), lambda i: ((out_tile0 + i) // tiles_per_seq, 0, 0)),
                pl.BlockSpec((d, D_SHARED), lambda i: (0, 0)),
                pl.BlockSpec((d, D_SHARED), lambda i: (0, 0)),
                pl.BlockSpec((D_SHARED, d), lambda i: (0, 0)),
                pl.BlockSpec((1, d), lambda i: (0, 0))]
    args = [yk, x1, h2p, gate_tk, mod, wsg_bf, wsu_bf, wsd_bf, final_g]
    aliases = {}
    if prev_out is not None:
        in_specs.append(pl.BlockSpec(memory_space=pl.ANY))
        args.append(prev_out)
        aliases = {len(args) - 1: 0}
    return pl.pallas_call(
        _combine_kernel,
        out_shape=jax.ShapeDtypeStruct((t, d), F32),
        grid=(yk.shape[1] // TF,),
        in_specs=in_specs,
        out_specs=tok,
        input_output_aliases=aliases,
        compiler_params=pltpu.CompilerParams(vmem_limit_bytes=VMEM_LIMIT),
        name="combine",
    )(*args)


def _rope_tables(n_tokens):
    n_rows = n_tokens // GRID_W
    n_freq = HEAD_DIM // 4
    inv_freq = ROPE_THETA ** (-jnp.arange(n_freq, dtype=F32) / n_freq)
    ang_r = jnp.arange(n_rows).astype(F32)[:, None] * inv_freq[None, :]
    ang_c = jnp.arange(GRID_W).astype(F32)[:, None] * inv_freq[None, :]

    def per_token(row_part, col_part):
        rows = jnp.broadcast_to(row_part[:, None, :], (n_rows, GRID_W, n_freq))
        cols = jnp.broadcast_to(col_part[None, :, :], (n_rows, GRID_W, n_freq))
        return rows.reshape(n_tokens, n_freq), cols.reshape(n_tokens, n_freq)

    cos_r, cos_c = per_token(jnp.cos(ang_r), jnp.cos(ang_c))
    sin_r, sin_c = per_token(jnp.sin(ang_r), jnp.sin(ang_c))
    cos = jnp.concatenate([cos_r, cos_r, cos_c, cos_c], axis=1)
    sin = jnp.concatenate([-sin_r, sin_r, -sin_c, sin_c], axis=1)
    reps = LANES // HEAD_DIM
    return jnp.tile(cos, (1, reps)), jnp.tile(sin, (1, reps))


def _pool_bands():
    i = jnp.arange(QB)[:, None]
    r = jnp.arange(POOL_SLAB)[None, :]
    return jnp.stack([((r >= i + POOL_OFF - w // 2) & (r < i + POOL_OFF + w // 2)).astype(BF16)
                      for w in POOL_WINDOWS])


def kernel(x, c, ctx, c_ctx, w_ada, b_ada, norm1_g, norm2_g, w_in, attn_sink, pool_w, pool_scale, w_out,
           w_router, router_bias, w_gate, w_up, w_down, ws_gate, ws_up, ws_down, final_g):
    b, s, d = x.shape
    t = b * s
    assert w_ada.shape[0] == 1 and d == D_MODEL and s % TQ == 0 and b + 1 <= 8

    c8 = jnp.zeros((8, d), F32).at[:b].set(c).at[b].set(c_ctx)
    mod = _ada(c8, w_ada[0], b_ada[0]).reshape(8, 6, d)
    g1 = norm1_g[0].reshape(1, d)
    g2 = norm2_g[0].reshape(1, d)
    w_in_bf = w_in[0].astype(BF16)
    cos_t, sin_t = _rope_tables(s)

    q, k4, v4, p = _inproj(x, mod, g1, w_in_bf, cos_t, sin_t)
    kc4, vc4 = _ctxproj(ctx, mod[b:b + 1], g1, w_in_bf[:, ATTN_WIDTH:ATTN_WIDTH + 2 * KV_WIDTH])

    wr_t = w_router[0].T
    wr_hi = wr_t.astype(BF16)
    wr_lo = (wr_t - wr_hi.astype(F32)).astype(BF16)
    attn_consts = (_pool_bands(), pool_w[0].astype(BF16), pool_scale[0].reshape(1, POOL_WIDTH),
                   w_out[0].astype(BF16), g2, wr_hi, wr_lo)
    tri = jnp.triu(jnp.ones((TR, TR), BF16), k=1)
    shared_w = (ws_gate[0].astype(BF16), ws_up[0].astype(BF16), ws_down[0].astype(BF16))

    nb = b // TOKEN_GROUPS
    tg = nb * s
    assert b % TOKEN_GROUPS == 0 and tg % (COMBINE_CHUNKS * SC_WORKERS * SC_TOKENS) == 0
    n_rows = -(-(tg * TOP_K + N_EXPERTS * (TM_EXP - 1)) // TM_EXP) * TM_EXP
    groups = []
    for g in range(TOKEN_GROUPS):
        x1, h2p, lg_t = _attn(g * nb, nb, attn_sink[0], q, k4, v4, kc4, vc4, p, x, mod, *attn_consts)
        idx_kt, gate_kt, rank_kt, counts = _route(lg_t, router_bias[0].reshape(N_EXPERTS, 1), tri)
        pad_start, block_e, block_valid, n_active = _plan_blocks(counts[:, 0].astype(jnp.int32), n_rows // TM_EXP)
        dest_kt = _dest_rows(pad_start, idx_kt, rank_kt)
        xs = _sc_scatter(dest_kt, h2p, n_rows)
        groups.append((x1.reshape(tg, d), h2p, gate_kt.T, dest_kt, xs, block_e, block_valid, n_active))

    out = None
    chunk = tg // COMBINE_CHUNKS
    for g, (x1, h2p, gate_tk, dest_kt, xs, block_e, block_valid, n_active) in enumerate(groups):
        ys = _experts(block_e, block_valid, n_active, xs, w_gate[0], w_up[0], w_down[0])
        for token0 in range(0, tg, chunk):
            out = _combine(_sc_gather(dest_kt, ys, token0, chunk), token0, g * tg + token0, t, out, x1, h2p,
                           gate_tk, mod, *shared_w, final_g.reshape(1, d), s)
    return out.reshape(b, s, d)
```

```python
import functools

import jax
import jax.numpy as jnp
from jax import lax
from jax.experimental import pallas as pl
from jax.experimental.pallas import tpu as pltpu
from jax.experimental.pallas import tpu_sc as plsc

F32 = jnp.float32
BF16 = jnp.bfloat16

D_MODEL = 1024
GRID_W = 64
N_HEADS = 8
N_KV_HEADS = 2
HEAD_DIM = 64
ATTN_WIDTH = N_HEADS * HEAD_DIM
KV_WIDTH = N_KV_HEADS * HEAD_DIM
WINDOW = 128
ROPE_THETA = 10000.0
POOL_WINDOWS = (2, 4, 8, 16)
POOL_WIDTH = D_MODEL - ATTN_WIDTH
POOL_GROUP_DIM = POOL_WIDTH // len(POOL_WINDOWS)
IN_COLS = ATTN_WIDTH + 2 * KV_WIDTH + POOL_WIDTH
N_EXPERTS = 64
TOP_K = 8
N_EXPERT_GROUPS = 8
EXPERTS_PER_GROUP = N_EXPERTS // N_EXPERT_GROUPS
TOPK_GROUPS = 4
D_EXPERT = 256
D_SHARED = 256
ROUTED_SCALE = 2.5
EPS = 1e-6
LOG2E = 1.4426950408889634

LANES = 128
U32 = jnp.uint32
VMEM_LIMIT = 48 * 1024 * 1024

TM_PROJ = 512
TQ = 512
QB = 128
POOL_SLAB = 256
POOL_OFF = 64
TR = 512
TM_EXP = 2048
EXP_MIN_ROWS = 256
TF = 256
TOKEN_GROUPS = 1
COMBINE_CHUNKS = 4
SC_CORES = 2
SC_WORKERS = 32
SC_TOKENS = 128


def _silu(x):
    return x * (1.0 / (1.0 + jnp.exp(-x)))


def _split_bf16(x):
    hi = x.astype(BF16)
    lo = (x - hi.astype(F32)).astype(BF16)
    return hi, lo


def _dot(a, b):
    return jnp.dot(a, b, preferred_element_type=F32)


def _pack_words(val):
    half = val.shape[1] // 2
    lo = lax.bitcast_convert_type(val[:, :half].astype(BF16).astype(F32), U32)
    hi = lax.bitcast_convert_type(val[:, half:].astype(BF16).astype(F32), U32)
    return lax.shift_right_logical(lo, jnp.uint32(16)) | hi


def _unpack_words(words):
    lo = lax.bitcast_convert_type(lax.shift_left(words, jnp.uint32(16)), F32)
    hi = lax.bitcast_convert_type(words & jnp.uint32(0xFFFF0000), F32)
    return jnp.concatenate([lo, hi], axis=1)


def _dot_nt(a, b):
    return lax.dot_general(a, b, (((1,), (1,)), ((), ())), preferred_element_type=F32)


def _ada_kernel(c_ref, w_ref, b_ref, o_ref):
    a_hi, a_lo = _split_bf16(_silu(c_ref[...]))
    w_hi, w_lo = _split_bf16(w_ref[...])
    o_ref[...] = _dot(a_hi, w_hi) + _dot(a_lo, w_hi) + _dot(a_hi, w_lo) + b_ref[...]


def _ada(c8, w_ada, b_ada):
    d = c8.shape[1]
    n = w_ada.shape[1]
    tn = 512
    return pl.pallas_call(
        _ada_kernel,
        out_shape=jax.ShapeDtypeStruct((8, n), F32),
        grid=(n // tn,),
        in_specs=[pl.BlockSpec((8, d), lambda j: (0, 0)),
                  pl.BlockSpec((d, tn), lambda j: (0, j)),
                  pl.BlockSpec((1, tn), lambda j: (0, j))],
        out_specs=pl.BlockSpec((8, tn), lambda j: (0, j)),
        compiler_params=pltpu.CompilerParams(vmem_limit_bytes=VMEM_LIMIT),
        name="ada",
    )(c8, w_ada, b_ada.reshape(1, n))


def _norm_mod(x, g, shift, scale):
    ms = jnp.mean(x * x, axis=-1, keepdims=True)
    return (x * lax.rsqrt(ms + EPS) * g) * (1.0 + scale) + shift


def _lane_variants(t):
    lane = lax.broadcasted_iota(jnp.int32, t.shape, 1)
    lo = lane < HEAD_DIM
    tr = pltpu.roll(t, HEAD_DIM, 1)
    zero = jnp.zeros_like(t)
    return (jnp.where(lo, t, zero), jnp.where(lo, zero, tr),
            jnp.where(lo, tr, zero), jnp.where(lo, zero, t))


def _store_variants(ref, t):
    for i, var in enumerate(_lane_variants(t)):
        ref[0, :, i * LANES:(i + 1) * LANES] = var.astype(BF16)


def _inproj_kernel(x_ref, mod_ref, g_ref, w_ref, cos_ref, sin_ref, q_ref, k_ref, v_ref, p_ref):
    h = _norm_mod(x_ref[0], g_ref[...], mod_ref[0, 0:1, :], mod_ref[0, 1:2, :])
    z = _dot(h.astype(BF16), w_ref[...])
    cos = cos_ref[...]
    sin = sin_ref[...]
    lane = lax.broadcasted_iota(jnp.int32, cos.shape, 1)
    first_half = (lane & 16) == 0

    def rope(zc):
        partner = jnp.where(first_half, pltpu.roll(zc, LANES - 16, 1), pltpu.roll(zc, 16, 1))
        return zc * cos + partner * sin

    scale = HEAD_DIM ** -0.5 * LOG2E
    for c in range(ATTN_WIDTH // LANES):
        q_ref[0, :, c * LANES:(c + 1) * LANES] = (rope(z[:, c * LANES:(c + 1) * LANES]) * scale).astype(BF16)
    _store_variants(k_ref, rope(z[:, ATTN_WIDTH:ATTN_WIDTH + KV_WIDTH]))
    _store_variants(v_ref, z[:, ATTN_WIDTH + KV_WIDTH:ATTN_WIDTH + 2 * KV_WIDTH])
    p_ref[0] = z[:, ATTN_WIDTH + 2 * KV_WIDTH:]


def _inproj(x, mod, g1, w_in_bf, cos_t, sin_t):
    b, s, d = x.shape
    tm = TM_PROJ
    return pl.pallas_call(
        _inproj_kernel,
        out_shape=(jax.ShapeDtypeStruct((b, s, ATTN_WIDTH), BF16),
                   jax.ShapeDtypeStruct((b, s, 4 * LANES), BF16),
                   jax.ShapeDtypeStruct((b, s, 4 * LANES), BF16),
                   jax.ShapeDtypeStruct((b, s, POOL_WIDTH), F32)),
        grid=(s // tm, b),
        in_specs=[pl.BlockSpec((1, tm, d), lambda n, bi: (bi, n, 0)),
                  pl.BlockSpec((1, 6, d), lambda n, bi: (bi, 0, 0)),
                  pl.BlockSpec((1, d), lambda n, bi: (0, 0)),
                  pl.BlockSpec((d, IN_COLS), lambda n, bi: (0, 0)),
                  pl.BlockSpec((tm, LANES), lambda n, bi: (n, 0)),
                  pl.BlockSpec((tm, LANES), lambda n, bi: (n, 0))],
        out_specs=(pl.BlockSpec((1, tm, ATTN_WIDTH), lambda n, bi: (bi, n, 0)),
                   pl.BlockSpec((1, tm, 4 * LANES), lambda n, bi: (bi, n, 0)),
                   pl.BlockSpec((1, tm, 4 * LANES), lambda n, bi: (bi, n, 0)),
                   pl.BlockSpec((1, tm, POOL_WIDTH), lambda n, bi: (bi, n, 0))),
        compiler_params=pltpu.CompilerParams(vmem_limit_bytes=VMEM_LIMIT),
        name="inproj",
    )(x, mod, g1, w_in_bf, cos_t, sin_t)


def _ctxproj_kernel(x_ref, mod_ref, g_ref, w_ref, k_ref, v_ref):
    h = _norm_mod(x_ref[0], g_ref[...], mod_ref[0, 0:1, :], mod_ref[0, 1:2, :])
    z = _dot(h.astype(BF16), w_ref[...])
    _store_variants(k_ref, z[:, :KV_WIDTH])
    _store_variants(v_ref, z[:, KV_WIDTH:])


def _ctxproj(ctx, mod_c, g1, w_kv_bf):
    b, c, d = ctx.shape
    return pl.pallas_call(
        _ctxproj_kernel,
        out_shape=(jax.ShapeDtypeStruct((b, c, 4 * LANES), BF16),
                   jax.ShapeDtypeStruct((b, c, 4 * LANES), BF16)),
        grid=(b,),
        in_specs=[pl.BlockSpec((1, c, d), lambda bi: (bi, 0, 0)),
                  pl.BlockSpec((1, 6, d), lambda bi: (0, 0, 0)),
                  pl.BlockSpec((1, d), lambda bi: (0, 0)),
                  pl.BlockSpec((d, 2 * KV_WIDTH), lambda bi: (0, 0))],
        out_specs=(pl.BlockSpec((1, c, 4 * LANES), lambda bi: (bi, 0, 0)),
                   pl.BlockSpec((1, c, 4 * LANES), lambda bi: (bi, 0, 0))),
        compiler_params=pltpu.CompilerParams(vmem_limit_bytes=VMEM_LIMIT),
        name="ctxproj",
    )(ctx, mod_c, g1, w_kv_bf)


def _fold(op, tiles):
    while len(tiles) > 1:
        tiles = [op(tiles[i], tiles[i + 1]) if i + 1 < len(tiles) else tiles[i] for i in range(0, len(tiles), 2)]
    return tiles[0]


def _stack_variants(t4, kv):
    return jnp.concatenate([t4[:, (2 * kv) * LANES:(2 * kv + 1) * LANES],
                            t4[:, (2 * kv + 1) * LANES:(2 * kv + 2) * LANES]], axis=0)


def _attn_kernel(seq_len, sink_ref, q_ref, k_ref, kp_ref, kn_ref, v_ref, vp_ref, vn_ref, kc_ref, vc_ref,
                 p_ref, pp_ref, pn_ref, x_ref, mod_ref, band_ref, poolw_ref, pscale_ref, wout_ref,
                 g2_ref, wrh_ref, wrl_ref, x1_ref, h2_ref, lg_ref, kwin, vwin, pext, mix, s_scr, p_scr, m_scr):
    n = pl.program_id(1)
    n_last = pl.num_programs(1) - 1

    kwin[0:QB, :] = kp_ref[0]
    kwin[QB:QB + TQ, :] = k_ref[0]
    kwin[QB + TQ:, :] = kn_ref[0]
    vwin[0:QB, :] = vp_ref[0]
    vwin[QB:QB + TQ, :] = v_ref[0]
    vwin[QB + TQ:, :] = vn_ref[0]

    pext[0:QB - 8, :] = jnp.zeros((QB - 8, POOL_WIDTH), F32)
    pext[QB - 8:QB, :] = jnp.where(n > 0, pp_ref[0], 0.0)
    pext[QB:QB + TQ, :] = p_ref[0]
    pext[QB + TQ:QB + TQ + 8, :] = jnp.where(n < n_last, pn_ref[0], 0.0)
    pext[QB + TQ + 8:, :] = jnp.zeros((QB - 8, POOL_WIDTH), F32)

    row = lax.broadcasted_iota(jnp.int32, (QB, 3 * QB), 0)
    col = lax.broadcasted_iota(jnp.int32, (QB, 3 * QB), 1)
    in_band = (col >= row) & (col <= row + 2 * WINDOW)
    tok = lax.broadcasted_iota(jnp.int32, (QB, 1), 0)
    kc = kc_ref[0]
    kc_rows = [_stack_variants(kc, kv) for kv in range(N_KV_HEADS)]
    vc_rows = [_stack_variants(vc_ref[0], kv) for kv in range(N_KV_HEADS)]

    def sub_block(j, carry):
        r0 = pl.multiple_of(j * QB, QB)
        qj = q_ref[0, pl.ds(r0, QB), :]
        kw = kwin[pl.ds(r0, 3 * QB), :]
        vw = vwin[pl.ds(r0, 3 * QB), :]
        kpos = col + (n * TQ + j * QB - QB)
        ok = in_band & (kpos >= 0) & (kpos < seq_len)
        bias = jnp.where(ok, 0.0, -jnp.inf)
        n_loc, n_ctx = 3 * QB, kc.shape[0]
        bias2 = jnp.concatenate([bias, bias], axis=1)
        k_rows = [_stack_variants(kw, kv) for kv in range(N_KV_HEADS)]
        v_rows = [_stack_variants(vw, kv) for kv in range(N_KV_HEADS)]
        group = N_HEADS // N_KV_HEADS

        def head_tiles(head):
            loc0 = (head % 2) * n_loc
            ctx0 = 2 * n_loc + (head % 2) * n_ctx
            return ([loc0 + i * LANES for i in range(n_loc // LANES)]
                    + [ctx0 + i * LANES for i in range(n_ctx // LANES)])

        for c in range(N_HEADS // 2):
            qc = qj[:, c * LANES:(c + 1) * LANES]
            s_scr[c, :, 0:2 * n_loc] = _dot_nt(qc, k_rows[2 * c // group]) + bias2
            s_scr[c, :, 2 * n_loc:] = _dot_nt(qc, kc_rows[2 * c // group])
        for head in range(N_HEADS):
            tiles = [s_scr[head // 2, :, st:st + LANES] for st in head_tiles(head)]
            row_max = jnp.max(_fold(jnp.maximum, tiles), axis=1, keepdims=True)
            m_scr[head] = jnp.broadcast_to(jnp.maximum(row_max, sink_ref[head] * LOG2E), (QB, LANES))
        for head in range(N_HEADS):
            m = m_scr[head]
            acc = None
            for st in head_tiles(head):
                p = jnp.exp2(s_scr[head // 2, :, st:st + LANES] - m)
                p_scr[head // 2, :, st:st + LANES] = p.astype(BF16)
                acc = p if acc is None else acc + p
            denom = (jnp.broadcast_to(jnp.sum(acc, axis=1, keepdims=True), (QB, LANES))
                     + jnp.exp2(sink_ref[head] * LOG2E - m))
            m_scr[head] = 1.0 / denom
        lane = lax.broadcasted_iota(jnp.int32, (QB, LANES), 1)
        for c in range(N_HEADS // 2):
            o = (_dot(p_scr[c, :, 0:2 * n_loc], v_rows[2 * c // group])
                 + _dot(p_scr[c, :, 2 * n_loc:], vc_rows[2 * c // group]))
            o = o * jnp.where(lane < HEAD_DIM, m_scr[2 * c], m_scr[2 * c + 1])
            mix[pl.ds(r0, QB), c * LANES:(c + 1) * LANES] = o.astype(BF16)

        slab = pext[pl.ds(pl.multiple_of(r0 + POOL_OFF, 8), POOL_SLAB), :]
        tpos = tok + (n * TQ + j * QB)
        for g, w in enumerate(POOL_WINDOWS):
            sg = slab[:, g * LANES:(g + 1) * LANES]
            hi, lo = _split_bf16(sg)
            band = band_ref[g]
            wsum = _dot(band, hi) + _dot(band, lo)
            cnt = (jnp.minimum(tpos - w // 2 + w, seq_len) - jnp.maximum(tpos - w // 2, 0)).astype(F32)
            dlt = wsum / cnt - sg[POOL_OFF:POOL_OFF + QB, :]
            y = _dot(dlt.astype(BF16), poolw_ref[g]) * pscale_ref[:, g * LANES:(g + 1) * LANES]
            mix[pl.ds(r0, QB), ATTN_WIDTH + g * LANES:ATTN_WIDTH + (g + 1) * LANES] = y.astype(BF16)
        return carry

    lax.fori_loop(0, TQ // QB, sub_block, 0)

    proj = _dot(mix[...], wout_ref[...])
    x1 = x_ref[0] + mod_ref[0, 2:3, :] * proj
    x1_ref[0] = x1
    h2 = _norm_mod(x1, g2_ref[...], mod_ref[0, 3:4, :], mod_ref[0, 4:5, :])
    h2_ref[...] = _pack_words(h2)
    h_hi, h_lo = _split_bf16(h2)
    wrh = wrh_ref[...]
    lg_ref[...] = _dot_nt(wrh, h_hi) + _dot_nt(wrh, h_lo) + _dot_nt(wrl_ref[...], h_hi)


def _attn(b0, b, sink, q, k4, v4, kc4, vc4, p, x, mod, band, poolw_bf, pscale, wout_bf, g2, wr_hi, wr_lo):
    _, s, d = x.shape
    c = kc4.shape[1]
    nt = s // TQ
    hb = TQ // QB
    pb = TQ // 8
    kv_main = pl.BlockSpec((1, TQ, 4 * LANES), lambda bi, n: (b0 + bi, n, 0))
    kv_prev = pl.BlockSpec((1, QB, 4 * LANES), lambda bi, n: (b0 + bi, jnp.maximum(n * hb - 1, 0), 0))
    kv_next = pl.BlockSpec((1, QB, 4 * LANES), lambda bi, n: (b0 + bi, jnp.minimum((n + 1) * hb, s // QB - 1), 0))
    const2 = lambda bi, n: (0, 0)
    const3 = lambda bi, n: (0, 0, 0)
    return pl.pallas_call(
        functools.partial(_attn_kernel, s),
        out_shape=(jax.ShapeDtypeStruct((b, s, d), F32),
                   jax.ShapeDtypeStruct((b * s, d // 2), U32),
                   jax.ShapeDtypeStruct((N_EXPERTS, b * s), F32)),
        grid=(b, nt),
        in_specs=[pl.BlockSpec(memory_space=pltpu.SMEM),
                  pl.BlockSpec((1, TQ, ATTN_WIDTH), lambda bi, n: (b0 + bi, n, 0)),
                  kv_main, kv_prev, kv_next, kv_main, kv_prev, kv_next,
                  pl.BlockSpec((1, c, 4 * LANES), lambda bi, n: (b0 + bi, 0, 0)),
                  pl.BlockSpec((1, c, 4 * LANES), lambda bi, n: (b0 + bi, 0, 0)),
                  pl.BlockSpec((1, TQ, POOL_WIDTH), lambda bi, n: (b0 + bi, n, 0)),
                  pl.BlockSpec((1, 8, POOL_WIDTH), lambda bi, n: (b0 + bi, jnp.maximum(n * pb - 1, 0), 0)),
                  pl.BlockSpec((1, 8, POOL_WIDTH),
                               lambda bi, n: (b0 + bi, jnp.minimum((n + 1) * pb, s // 8 - 1), 0)),
                  pl.BlockSpec((1, TQ, d), lambda bi, n: (b0 + bi, n, 0)),
                  pl.BlockSpec((1, 6, d), lambda bi, n: (b0 + bi, 0, 0)),
                  pl.BlockSpec((len(POOL_WINDOWS), QB, POOL_SLAB), const3),
                  pl.BlockSpec((len(POOL_WINDOWS), POOL_GROUP_DIM, POOL_GROUP_DIM), const3),
                  pl.BlockSpec((1, POOL_WIDTH), const2),
                  pl.BlockSpec((d, d), const2),
                  pl.BlockSpec((1, d), const2),
                  pl.BlockSpec((N_EXPERTS, d), const2),
                  pl.BlockSpec((N_EXPERTS, d), const2)],
        out_specs=(pl.BlockSpec((1, TQ, d), lambda bi, n: (bi, n, 0)),
                   pl.BlockSpec((TQ, d // 2), lambda bi, n: (bi * nt + n, 0)),
                   pl.BlockSpec((N_EXPERTS, TQ), lambda bi, n: (0, bi * nt + n))),
        scratch_shapes=[pltpu.VMEM((TQ + 2 * QB, 4 * LANES), BF16),
                        pltpu.VMEM((TQ + 2 * QB, 4 * LANES), BF16),
                        pltpu.VMEM((TQ + 2 * QB, POOL_WIDTH), F32),
                        pltpu.VMEM((TQ, d), BF16),
                        pltpu.VMEM((N_HEADS // 2, QB, 2 * (3 * QB + c)), F32),
                        pltpu.VMEM((N_HEADS // 2, QB, 2 * (3 * QB + c)), BF16),
                        pltpu.VMEM((N_HEADS, QB, LANES), F32)],
        compiler_params=pltpu.CompilerParams(vmem_limit_bytes=VMEM_LIMIT),
        name="attn",
    )(sink, q, k4, k4, k4, v4, v4, v4, kc4, vc4, p, p, p, x, mod, band, poolw_bf, pscale, wout_bf,
      g2, wr_hi, wr_lo)


def _first_argmax_rows(v, row_iota, n_rows):
    m = jnp.max(v, axis=0, keepdims=True)
    idx = jnp.min(jnp.where(v == m, row_iota, n_rows), axis=0, keepdims=True)
    return m, idx


def _route_kernel(lg_ref, bias_ref, tri_ref, idx_ref, gate_ref, rank_ref, cnt_ref, carry):
    i = pl.program_id(0)

    @pl.when(i == 0)
    def _():
        carry[...] = jnp.zeros_like(carry)

    scores = 1.0 / (1.0 + jnp.exp(-lg_ref[...]))
    biased = scores + bias_ref[...]
    e_iota = lax.broadcasted_iota(jnp.int32, scores.shape, 0).astype(F32)
    g_iota = lax.broadcasted_iota(jnp.int32, (EXPERTS_PER_GROUP, TR), 0).astype(F32)
    neg = -jnp.inf

    grp = []
    for g in range(N_EXPERT_GROUPS):
        blk = biased[g * EXPERTS_PER_GROUP:(g + 1) * EXPERTS_PER_GROUP, :]
        m1, i1 = _first_argmax_rows(blk, g_iota, float(EXPERTS_PER_GROUP))
        m2 = jnp.max(jnp.where(g_iota == i1, neg, blk), axis=0, keepdims=True)
        grp.append(m1 + m2)
    grp = jnp.concatenate(grp, axis=0)
    gg_iota = lax.broadcasted_iota(jnp.int32, grp.shape, 0).astype(F32)
    grp_sel = jnp.zeros(grp.shape, F32)
    for _ in range(TOPK_GROUPS):
        _, gi = _first_argmax_rows(grp, gg_iota, float(N_EXPERT_GROUPS))
        hit = gg_iota == gi
        grp_sel = jnp.where(hit, 1.0, grp_sel)
        grp = jnp.where(hit, neg, grp)
    allowed = jnp.concatenate(
        [jnp.broadcast_to(grp_sel[g:g + 1, :], (EXPERTS_PER_GROUP, TR)) for g in range(N_EXPERT_GROUPS)], axis=0)
    masked = jnp.where(allowed > 0.5, biased, neg)

    idxs, gates = [], []
    onehot = jnp.zeros(scores.shape, F32)
    for _ in range(TOP_K):
        _, ei = _first_argmax_rows(masked, e_iota, float(N_EXPERTS))
        hit = e_iota == ei
        idxs.append(ei)
        gates.append(jnp.sum(jnp.where(hit, scores, 0.0), axis=0, keepdims=True))
        onehot = jnp.where(hit, 1.0, onehot)
        masked = jnp.where(hit, neg, masked)
    idx = jnp.concatenate(idxs, axis=0)
    gate = jnp.concatenate(gates, axis=0)
    gate = gate / jnp.sum(gate, axis=0, keepdims=True) * ROUTED_SCALE

    before = _dot(onehot.astype(BF16), tri_ref[...]) + carry[:, 0:1]
    ranks = [jnp.sum(jnp.where(e_iota == idxs[k], before, 0.0), axis=0, keepdims=True) for k in range(TOP_K)]
    idx_ref[...] = idx.astype(jnp.int32)
    gate_ref[...] = gate
    rank_ref[...] = jnp.concatenate(ranks, axis=0).astype(jnp.int32)
    total = carry[...] + jnp.sum(onehot, axis=1, keepdims=True)
    carry[...] = total
    cnt_ref[...] = total


def _route(lg_t, bias, tri):
    e, t = lg_t.shape
    tok = pl.BlockSpec((TOP_K, TR), lambda i: (0, i))
    return pl.pallas_call(
        _route_kernel,
        out_shape=(jax.ShapeDtypeStruct((TOP_K, t), jnp.int32),
                   jax.ShapeDtypeStruct((TOP_K, t), F32),
                   jax.ShapeDtypeStruct((TOP_K, t), jnp.int32),
                   jax.ShapeDtypeStruct((e, LANES), F32)),
        grid=(t // TR,),
        in_specs=[pl.BlockSpec((e, TR), lambda i: (0, i)),
                  pl.BlockSpec((e, 1), lambda i: (0, 0)),
                  pl.BlockSpec((TR, TR), lambda i: (0, 0))],
        out_specs=(tok, tok, tok, pl.BlockSpec((e, LANES), lambda i: (0, 0))),
        scratch_shapes=[pltpu.VMEM((e, LANES), F32)],
        compiler_params=pltpu.CompilerParams(vmem_limit_bytes=VMEM_LIMIT),
        name="route",
    )(lg_t, bias, tri)


def _plan_kernel(n_blocks, size_ref, start_ref, expert_ref, valid_ref, nact_ref):
    def per_expert(e, first_block):
        size = size_ref[e]
        n_blk = (size + TM_EXP - 1) // TM_EXP
        start_ref[e] = first_block * TM_EXP

        def per_block(j, carry):
            expert_ref[first_block + j] = e
            valid_ref[first_block + j] = jnp.minimum(size - j * TM_EXP, TM_EXP)
            return carry

        lax.fori_loop(0, n_blk, per_block, 0)
        return first_block + n_blk

    n_active = lax.fori_loop(0, N_EXPERTS, per_expert, 0)
    nact_ref[0] = n_active

    def unused(i, carry):
        expert_ref[i] = N_EXPERTS - 1
        valid_ref[i] = 0
        return carry

    lax.fori_loop(n_active, n_blocks, unused, 0)


def _plan_blocks(sizes, n_blocks):
    smem = pl.BlockSpec(memory_space=pltpu.SMEM)
    return pl.pallas_call(
        functools.partial(_plan_kernel, n_blocks),
        out_shape=(jax.ShapeDtypeStruct((N_EXPERTS,), jnp.int32),
                   jax.ShapeDtypeStruct((n_blocks,), jnp.int32),
                   jax.ShapeDtypeStruct((n_blocks,), jnp.int32),
                   jax.ShapeDtypeStruct((1,), jnp.int32)),
        in_specs=[smem],
        out_specs=(smem, smem, smem, smem),
        name="plan_blocks",
    )(sizes)


def _dest_kernel(start_ref, idx_ref, rank_ref, dest_ref):
    idx = idx_ref[...]
    dest = rank_ref[...]
    for e in range(N_EXPERTS):
        dest = dest + jnp.where(idx == e, start_ref[e], 0)
    dest_ref[...] = dest


def _dest_rows(pad_start, idx_kt, rank_kt):
    n_k, t = idx_kt.shape
    tile = 4096
    blk = pl.BlockSpec((n_k, tile), lambda i: (0, i))
    return pl.pallas_call(
        _dest_kernel,
        out_shape=jax.ShapeDtypeStruct((n_k, t), jnp.int32),
        grid=(t // tile,),
        in_specs=[pl.BlockSpec(memory_space=pltpu.SMEM), blk, blk],
        out_specs=blk,
        name="dest_rows",
    )(pad_start, idx_kt, rank_kt)


def _sc_mesh():
    return plsc.VectorSubcoreMesh(core_axis_name="c", subcore_axis_name="s")


def _sc_token_base(steps, j):
    worker = lax.axis_index("s") * SC_CORES + lax.axis_index("c")
    return (worker * steps + j) * SC_TOKENS


def _sc_scatter(dest_kt, h2p, n_rows):
    t, width = h2p.shape
    steps = t // (SC_WORKERS * SC_TOKENS)

    @functools.partial(
        pl.kernel, mesh=_sc_mesh(),
        out_type=jax.ShapeDtypeStruct((n_rows, width), U32),
        scratch_types=[pltpu.VMEM((TOP_K, SC_TOKENS), jnp.int32),
                       pltpu.VMEM((SC_TOKENS, width), U32),
                       pltpu.SemaphoreType.DMA],
        name="sc_scatter",
    )
    def body(dest_hbm, h_hbm, xs_hbm, idx_v, rows_v, sem):
        @pl.loop(0, steps)
        def _(j):
            base = _sc_token_base(steps, j)
            pltpu.sync_copy(dest_hbm.at[:, pl.ds(base, SC_TOKENS)], idx_v)
            pltpu.sync_copy(h_hbm.at[pl.ds(base, SC_TOKENS)], rows_v)
            copies = [pltpu.async_copy(rows_v, xs_hbm.at[idx_v.at[k]], sem) for k in range(TOP_K)]
            for cp in copies:
                cp.wait()

    return body(dest_kt, h2p)


def _sc_gather(dest_kt, ys, token0, n_tokens):
    n_k = dest_kt.shape[0]
    width = ys.shape[1]
    steps = n_tokens // (SC_WORKERS * SC_TOKENS)

    half = SC_TOKENS // 2
    units = [(k, h) for k in range(n_k) for h in range(2)]
    n_buf = 3

    @functools.partial(
        pl.kernel, mesh=_sc_mesh(),
        out_type=jax.ShapeDtypeStruct((n_k, n_tokens, width), U32),
        scratch_types=[pltpu.VMEM((n_k, SC_TOKENS), jnp.int32),
                       pltpu.VMEM((n_buf, half, width), U32),
                       pltpu.SemaphoreType.DMA((n_buf,))],
        name="sc_gather",
    )
    def body(dest_hbm, ys_hbm, yk_hbm, idx_v, rows_v, sems):
        @pl.loop(0, steps)
        def _(j):
            base = _sc_token_base(steps, j)
            pltpu.sync_copy(dest_hbm.at[:, pl.ds(token0 + base, SC_TOKENS)], idx_v)

            def gather(u):
                k, h = units[u]
                slot = u % n_buf
                return pltpu.make_async_copy(ys_hbm.at[idx_v.at[k, pl.ds(h * half, half)]], rows_v.at[slot],
                                             sems.at[slot])

            gather(0).start()
            gather(1).start()
            for u, (k, h) in enumerate(units):
                gather(u).wait()
                pltpu.sync_copy(rows_v.at[u % n_buf], yk_hbm.at[k, pl.ds(base + h * half, half)])
                if u + 2 < len(units):
                    gather(u + 2).start()

    return body(dest_kt, ys)


def _experts_kernel(be_ref, valid_ref, nact_ref, xs_ref, wg_ref, wu_ref, wd_ref, ys_ref, wg_bf, wu_bf, wd_bf):
    i = pl.program_id(0)
    prev = be_ref[jnp.maximum(i - 1, 0)]

    @pl.when((i == 0) | (be_ref[i] != prev))
    def _():
        wg_bf[...] = wg_ref[0].astype(BF16)
        wu_bf[...] = wu_ref[0].astype(BF16)
        wd_bf[...] = wd_ref[0].astype(BF16)

    valid = jnp.where(i < nact_ref[0], valid_ref[i], 0)

    def run(rows):
        row = lax.broadcasted_iota(jnp.int32, (rows, xs_ref.shape[1]), 0)
        words = jnp.where(row < valid, xs_ref[0:rows, :], jnp.uint32(0))
        xb = _unpack_words(words).astype(BF16)
        hid = _silu(_dot(xb, wg_bf[...])) * _dot(xb, wu_bf[...])
        ys_ref[0:rows, :] = _pack_words(_dot(hid.astype(BF16), wd_bf[...]))
        if rows < TM_EXP:
            ys_ref[rows:, :] = jnp.zeros((TM_EXP - rows, ys_ref.shape[1]), U32)

    rows, lower = TM_EXP, TM_EXP // 2
    while rows >= EXP_MIN_ROWS:
        lo = lower if rows > EXP_MIN_ROWS else 0
        pl.when((valid > lo) & (valid <= rows))(functools.partial(run, rows))
        rows, lower = rows // 2, lower // 2

    @pl.when(valid == 0)
    def _():
        ys_ref[...] = jnp.zeros_like(ys_ref)


def _experts(block_e, block_valid, n_active, xs, w_gate, w_up, w_down):
    n_rows, half = xs.shape
    d = w_gate.shape[1]
    nb = n_rows // TM_EXP
    grid_spec = pltpu.PrefetchScalarGridSpec(
        num_scalar_prefetch=3,
        grid=(nb,),
        in_specs=[pl.BlockSpec((TM_EXP, half), lambda i, be, bv, na: (jnp.minimum(i, na[0] - 1), 0)),
                  pl.BlockSpec((1, d, D_EXPERT), lambda i, be, bv, na: (be[i], 0, 0)),
                  pl.BlockSpec((1, d, D_EXPERT), lambda i, be, bv, na: (be[i], 0, 0)),
                  pl.BlockSpec((1, D_EXPERT, d), lambda i, be, bv, na: (be[i], 0, 0))],
        out_specs=pl.BlockSpec((TM_EXP, half), lambda i, be, bv, na: (i, 0)),
        scratch_shapes=[pltpu.VMEM((d, D_EXPERT), BF16), pltpu.VMEM((d, D_EXPERT), BF16),
                        pltpu.VMEM((D_EXPERT, d), BF16)],
    )
    return pl.pallas_call(
        _experts_kernel,
        out_shape=jax.ShapeDtypeStruct((n_rows, half), U32),
        grid_spec=grid_spec,
        compiler_params=pltpu.CompilerParams(vmem_limit_bytes=VMEM_LIMIT),
        name="experts",
    )(block_e, block_valid, n_active, xs, w_gate, w_up, w_down)


def _combine_kernel(yk_ref, x1_ref, h2_ref, gate_ref, mod_ref, wsg_ref, wsu_ref, wsd_ref, fg_ref, *out_refs):
    out_ref = out_refs[-1]
    hb = _unpack_words(h2_ref[...]).astype(BF16)
    hid = _silu(_dot(hb, wsg_ref[...])) * _dot(hb, wsu_ref[...])
    ffn = _dot(hid.astype(BF16), wsd_ref[...])
    gate = gate_ref[...]
    for k in range(TOP_K):
        ffn = ffn + gate[:, k:k + 1] * _unpack_words(yk_ref[k])
    x2 = x1_ref[...] + mod_ref[0, 5:6, :] * ffn
    ms = jnp.mean(x2 * x2, axis=-1, keepdims=True)
    out_ref[...] = x2 * lax.rsqrt(ms + EPS) * fg_ref[...]


def _combine(yk, token0, out_token0, n_out_tokens, prev_out, x1, h2p, gate_tk, mod, wsg_bf, wsu_bf, wsd_bf, final_g,
             seq_len):
    t, d = n_out_tokens, x1.shape[1]
    tiles_per_seq = seq_len // TF
    tile0 = token0 // TF
    out_tile0 = out_token0 // TF
    in_tok = pl.BlockSpec((TF, d), lambda i: (tile0 + i, 0))
    tok = pl.BlockSpec((TF, d), lambda i: (out_tile0 + i, 0))
    in_specs = [pl.BlockSpec((TOP_K, TF, d // 2), lambda i: (0, i, 0)),
                in_tok, pl.BlockSpec((TF, d // 2), lambda i: (tile0 + i, 0)),
                pl.BlockSpec((TF, TOP_K), lambda i: (tile0 + i, 0)),
                pl.BlockSpec((1, 6, d), lambda i: ((out_tile0 + i) // tiles_per_seq, 0, 0)),
                pl.BlockSpec((d, D_SHARED), lambda i: (0, 0)),
                pl.BlockSpec((d, D_SHARED), lambda i: (0, 0)),
                pl.BlockSpec((D_SHARED, d), lambda i: (0, 0)),
                pl.BlockSpec((1, d), lambda i: (0, 0))]
    args = [yk, x1, h2p, gate_tk, mod, wsg_bf, wsu_bf, wsd_bf, final_g]
    aliases = {}
    if prev_out is not None:
        in_specs.append(pl.BlockSpec(memory_space=pl.ANY))
        args.append(prev_out)
        aliases = {len(args) - 1: 0}
    return pl.pallas_call(
        _combine_kernel,
        out_shape=jax.ShapeDtypeStruct((t, d), F32),
        grid=(yk.shape[1] // TF,),
        in_specs=in_specs,
        out_specs=tok,
        input_output_aliases=aliases,
        compiler_params=pltpu.CompilerParams(vmem_limit_bytes=VMEM_LIMIT),
        name="combine",
    )(*args)


def _rope_tables(n_tokens):
    n_rows = n_tokens // GRID_W
    n_freq = HEAD_DIM // 4
    inv_freq = ROPE_THETA ** (-jnp.arange(n_freq, dtype=F32) / n_freq)
    ang_r = jnp.arange(n_rows).astype(F32)[:, None] * inv_freq[None, :]
    ang_c = jnp.arange(GRID_W).astype(F32)[:, None] * inv_freq[None, :]

    def per_token(row_part, col_part):
        rows = jnp.broadcast_to(row_part[:, None, :], (n_rows, GRID_W, n_freq))
        cols = jnp.broadcast_to(col_part[None, :, :], (n_rows, GRID_W, n_freq))
        return rows.reshape(n_tokens, n_freq), cols.reshape(n_tokens, n_freq)

    cos_r, cos_c = per_token(jnp.cos(ang_r), jnp.cos(ang_c))
    sin_r, sin_c = per_token(jnp.sin(ang_r), jnp.sin(ang_c))
    cos = jnp.concatenate([cos_r, cos_r, cos_c, cos_c], axis=1)
    sin = jnp.concatenate([-sin_r, sin_r, -sin_c, sin_c], axis=1)
    reps = LANES // HEAD_DIM
    return jnp.tile(cos, (1, reps)), jnp.tile(sin, (1, reps))


def _pool_bands():
    i = jnp.arange(QB)[:, None]
    r = jnp.arange(POOL_SLAB)[None, :]
    return jnp.stack([((r >= i + POOL_OFF - w // 2) & (r < i + POOL_OFF + w // 2)).astype(BF16)
                      for w in POOL_WINDOWS])


def kernel(x, c, ctx, c_ctx, w_ada, b_ada, norm1_g, norm2_g, w_in, attn_sink, pool_w, pool_scale, w_out,
           w_router, router_bias, w_gate, w_up, w_down, ws_gate, ws_up, ws_down, final_g):
    b, s, d = x.shape
    t = b * s
    assert w_ada.shape[0] == 1 and d == D_MODEL and s % TQ == 0 and b + 1 <= 8

    c8 = jnp.zeros((8, d), F32).at[:b].set(c).at[b].set(c_ctx)
    mod = _ada(c8, w_ada[0], b_ada[0]).reshape(8, 6, d)
    g1 = norm1_g[0].reshape(1, d)
    g2 = norm2_g[0].reshape(1, d)
    w_in_bf = w_in[0].astype(BF16)
    cos_t, sin_t = _rope_tables(s)

    q, k4, v4, p = _inproj(x, mod, g1, w_in_bf, cos_t, sin_t)
    kc4, vc4 = _ctxproj(ctx, mod[b:b + 1], g1, w_in_bf[:, ATTN_WIDTH:ATTN_WIDTH + 2 * KV_WIDTH])

    wr_t = w_router[0].T
    wr_hi = wr_t.astype(BF16)
    wr_lo = (wr_t - wr_hi.astype(F32)).astype(BF16)
    attn_consts = (_pool_bands(), pool_w[0].astype(BF16), pool_scale[0].reshape(1, POOL_WIDTH),
                   w_out[0].astype(BF16), g2, wr_hi, wr_lo)
    tri = jnp.triu(jnp.ones((TR, TR), BF16), k=1)
    shared_w = (ws_gate[0].astype(BF16), ws_up[0].astype(BF16), ws_down[0].astype(BF16))

    nb = b // TOKEN_GROUPS
    tg = nb * s
    assert b % TOKEN_GROUPS == 0 and tg % (COMBINE_CHUNKS * SC_WORKERS * SC_TOKENS) == 0
    n_rows = -(-(tg * TOP_K + N_EXPERTS * (TM_EXP - 1)) // TM_EXP) * TM_EXP
    groups = []
    for g in range(TOKEN_GROUPS):
        x1, h2p, lg_t = _attn(g * nb, nb, attn_sink[0], q, k4, v4, kc4, vc4, p, x, mod, *attn_consts)
        idx_kt, gate_kt, rank_kt, counts = _route(lg_t, router_bias[0].reshape(N_EXPERTS, 1), tri)
        pad_start, block_e, block_valid, n_active = _plan_blocks(counts[:, 0].astype(jnp.int32), n_rows // TM_EXP)
        dest_kt = _dest_rows(pad_start, idx_kt, rank_kt)
        xs = _sc_scatter(dest_kt, h2p, n_rows)
        groups.append((x1.reshape(tg, d), h2p, gate_kt.T, dest_kt, xs, block_e, block_valid, n_active))

    out = None
    chunk = tg // COMBINE_CHUNKS
    for g, (x1, h2p, gate_tk, dest_kt, xs, block_e, block_valid, n_active) in enumerate(groups):
        ys = _experts(block_e, block_valid, n_active, xs, w_gate[0], w_up[0], w_down[0])
        for token0 in range(0, tg, chunk):
            out = _combine(_sc_gather(dest_kt, ys, token0, chunk), token0, g * tg + token0, t, out, x1, h2p,
                           gate_tk, mod, *shared_w, final_g.reshape(1, d), s)
    return out.reshape(b, s, d)
```

```python
import functools

import jax
import jax.numpy as jnp
from jax import lax
from jax.experimental import pallas as pl
from jax.experimental.pallas import tpu as pltpu
from jax.experimental.pallas import tpu_sc as plsc

F32 = jnp.float32
BF16 = jnp.bfloat16

D_MODEL = 1024
GRID_W = 64
N_HEADS = 8
N_KV_HEADS = 2
HEAD_DIM = 64
ATTN_WIDTH = N_HEADS * HEAD_DIM
KV_WIDTH = N_KV_HEADS * HEAD_DIM
WINDOW = 128
ROPE_THETA = 10000.0
POOL_WINDOWS = (2, 4, 8, 16)
POOL_WIDTH = D_MODEL - ATTN_WIDTH
POOL_GROUP_DIM = POOL_WIDTH // len(POOL_WINDOWS)
IN_COLS = ATTN_WIDTH + 2 * KV_WIDTH + POOL_WIDTH
N_EXPERTS = 64
TOP_K = 8
N_EXPERT_GROUPS = 8
EXPERTS_PER_GROUP = N_EXPERTS // N_EXPERT_GROUPS
TOPK_GROUPS = 4
D_EXPERT = 256
D_SHARED = 256
ROUTED_SCALE = 2.5
EPS = 1e-6
LOG2E = 1.4426950408889634

LANES = 128
U32 = jnp.uint32
VMEM_LIMIT = 48 * 1024 * 1024

TM_PROJ = 1024
TQ = 512
QB = 128
POOL_SLAB = 256
POOL_OFF = 64
TR = 512
TM_EXP = 1024
EXP_MIN_ROWS = 256
TF = 256
TOKEN_GROUPS = 1
COMBINE_CHUNKS = 4
SC_CORES = 2
SC_WORKERS = 32
SC_TOKENS = 128


def _silu(x):
    return x * (1.0 / (1.0 + jnp.exp(-x)))


def _split_bf16(x):
    hi = x.astype(BF16)
    lo = (x - hi.astype(F32)).astype(BF16)
    return hi, lo


def _dot(a, b):
    return jnp.dot(a, b, preferred_element_type=F32)


def _pack_words(val):
    half = val.shape[1] // 2
    lo = lax.bitcast_convert_type(val[:, :half].astype(BF16).astype(F32), U32)
    hi = lax.bitcast_convert_type(val[:, half:].astype(BF16).astype(F32), U32)
    return lax.shift_right_logical(lo, jnp.uint32(16)) | hi


def _unpack_words(words):
    lo = lax.bitcast_convert_type(lax.shift_left(words, jnp.uint32(16)), F32)
    hi = lax.bitcast_convert_type(words & jnp.uint32(0xFFFF0000), F32)
    return jnp.concatenate([lo, hi], axis=1)


def _dot_nt(a, b):
    return lax.dot_general(a, b, (((1,), (1,)), ((), ())), preferred_element_type=F32)


def _ada_kernel(c_ref, w_ref, b_ref, o_ref):
    a_hi, a_lo = _split_bf16(_silu(c_ref[...]))
    w_hi, w_lo = _split_bf16(w_ref[...])
    o_ref[...] = _dot(a_hi, w_hi) + _dot(a_lo, w_hi) + _dot(a_hi, w_lo) + b_ref[...]


def _ada(c8, w_ada, b_ada):
    d = c8.shape[1]
    n = w_ada.shape[1]
    tn = 512
    return pl.pallas_call(
        _ada_kernel,
        out_shape=jax.ShapeDtypeStruct((8, n), F32),
        grid=(n // tn,),
        in_specs=[pl.BlockSpec((8, d), lambda j: (0, 0)),
                  pl.BlockSpec((d, tn), lambda j: (0, j)),
                  pl.BlockSpec((1, tn), lambda j: (0, j))],
        out_specs=pl.BlockSpec((8, tn), lambda j: (0, j)),
        compiler_params=pltpu.CompilerParams(vmem_limit_bytes=VMEM_LIMIT),
        name="ada",
    )(c8, w_ada, b_ada.reshape(1, n))


def _norm_mod(x, g, shift, scale):
    ms = jnp.mean(x * x, axis=-1, keepdims=True)
    return (x * lax.rsqrt(ms + EPS) * g) * (1.0 + scale) + shift


def _lane_variants(t):
    lane = lax.broadcasted_iota(jnp.int32, t.shape, 1)
    lo = lane < HEAD_DIM
    tr = pltpu.roll(t, HEAD_DIM, 1)
    zero = jnp.zeros_like(t)
    return (jnp.where(lo, t, zero), jnp.where(lo, zero, tr),
            jnp.where(lo, tr, zero), jnp.where(lo, zero, t))


def _store_variants(ref, t):
    for i, var in enumerate(_lane_variants(t)):
        ref[0, :, i * LANES:(i + 1) * LANES] = var.astype(BF16)


def _inproj_kernel(x_ref, mod_ref, g_ref, w_ref, cos_ref, sin_ref, q_ref, k_ref, v_ref, p_ref):
    h = _norm_mod(x_ref[0], g_ref[...], mod_ref[0, 0:1, :], mod_ref[0, 1:2, :])
    z = _dot(h.astype(BF16), w_ref[...])
    cos = cos_ref[...]
    sin = sin_ref[...]
    lane = lax.broadcasted_iota(jnp.int32, cos.shape, 1)
    first_half = (lane & 16) == 0

    def rope(zc):
        partner = jnp.where(first_half, pltpu.roll(zc, LANES - 16, 1), pltpu.roll(zc, 16, 1))
        return zc * cos + partner * sin

    scale = HEAD_DIM ** -0.5 * LOG2E
    for c in range(ATTN_WIDTH // LANES):
        q_ref[0, :, c * LANES:(c + 1) * LANES] = (rope(z[:, c * LANES:(c + 1) * LANES]) * scale).astype(BF16)
    _store_variants(k_ref, rope(z[:, ATTN_WIDTH:ATTN_WIDTH + KV_WIDTH]))
    _store_variants(v_ref, z[:, ATTN_WIDTH + KV_WIDTH:ATTN_WIDTH + 2 * KV_WIDTH])
    p_ref[0] = z[:, ATTN_WIDTH + 2 * KV_WIDTH:]


def _inproj(x, mod, g1, w_in_bf, cos_t, sin_t):
    b, s, d = x.shape
    tm = TM_PROJ
    return pl.pallas_call(
        _inproj_kernel,
        out_shape=(jax.ShapeDtypeStruct((b, s, ATTN_WIDTH), BF16),
                   jax.ShapeDtypeStruct((b, s, 4 * LANES), BF16),
                   jax.ShapeDtypeStruct((b, s, 4 * LANES), BF16),
                   jax.ShapeDtypeStruct((b, s, POOL_WIDTH), F32)),
        grid=(s // tm, b),
        in_specs=[pl.BlockSpec((1, tm, d), lambda n, bi: (bi, n, 0)),
                  pl.BlockSpec((1, 6, d), lambda n, bi: (bi, 0, 0)),
                  pl.BlockSpec((1, d), lambda n, bi: (0, 0)),
                  pl.BlockSpec((d, IN_COLS), lambda n, bi: (0, 0)),
                  pl.BlockSpec((tm, LANES), lambda n, bi: (n, 0)),
                  pl.BlockSpec((tm, LANES), lambda n, bi: (n, 0))],
        out_specs=(pl.BlockSpec((1, tm, ATTN_WIDTH), lambda n, bi: (bi, n, 0)),
                   pl.BlockSpec((1, tm, 4 * LANES), lambda n, bi: (bi, n, 0)),
                   pl.BlockSpec((1, tm, 4 * LANES), lambda n, bi: (bi, n, 0)),
                   pl.BlockSpec((1, tm, POOL_WIDTH), lambda n, bi: (bi, n, 0))),
        compiler_params=pltpu.CompilerParams(vmem_limit_bytes=VMEM_LIMIT),
        name="inproj",
    )(x, mod, g1, w_in_bf, cos_t, sin_t)


def _ctxproj_kernel(x_ref, mod_ref, g_ref, w_ref, k_ref, v_ref):
    h = _norm_mod(x_ref[0], g_ref[...], mod_ref[0, 0:1, :], mod_ref[0, 1:2, :])
    z = _dot(h.astype(BF16), w_ref[...])
    _store_variants(k_ref, z[:, :KV_WIDTH])
    _store_variants(v_ref, z[:, KV_WIDTH:])


def _ctxproj(ctx, mod_c, g1, w_kv_bf):
    b, c, d = ctx.shape
    return pl.pallas_call(
        _ctxproj_kernel,
        out_shape=(jax.ShapeDtypeStruct((b, c, 4 * LANES), BF16),
                   jax.ShapeDtypeStruct((b, c, 4 * LANES), BF16)),
        grid=(b,),
        in_specs=[pl.BlockSpec((1, c, d), lambda bi: (bi, 0, 0)),
                  pl.BlockSpec((1, 6, d), lambda bi: (0, 0, 0)),
                  pl.BlockSpec((1, d), lambda bi: (0, 0)),
                  pl.BlockSpec((d, 2 * KV_WIDTH), lambda bi: (0, 0))],
        out_specs=(pl.BlockSpec((1, c, 4 * LANES), lambda bi: (bi, 0, 0)),
                   pl.BlockSpec((1, c, 4 * LANES), lambda bi: (bi, 0, 0))),
        compiler_params=pltpu.CompilerParams(vmem_limit_bytes=VMEM_LIMIT),
        name="ctxproj",
    )(ctx, mod_c, g1, w_kv_bf)


def _fold(op, tiles):
    while len(tiles) > 1:
        tiles = [op(tiles[i], tiles[i + 1]) if i + 1 < len(tiles) else tiles[i] for i in range(0, len(tiles), 2)]
    return tiles[0]


def _stack_variants(t4, kv):
    return jnp.concatenate([t4[:, (2 * kv) * LANES:(2 * kv + 1) * LANES],
                            t4[:, (2 * kv + 1) * LANES:(2 * kv + 2) * LANES]], axis=0)


def _attn_kernel(seq_len, sink_ref, q_ref, k_ref, kp_ref, kn_ref, v_ref, vp_ref, vn_ref, kc_ref, vc_ref,
                 p_ref, pp_ref, pn_ref, x_ref, mod_ref, band_ref, poolw_ref, pscale_ref, wout_ref,
                 g2_ref, wrh_ref, wrl_ref, x1_ref, h2_ref, lg_ref, kwin, vwin, pext, mix, s_scr, p_scr, m_scr):
    n = pl.program_id(1)
    n_last = pl.num_programs(1) - 1

    kwin[0:QB, :] = kp_ref[0]
    kwin[QB:QB + TQ, :] = k_ref[0]
    kwin[QB + TQ:, :] = kn_ref[0]
    vwin[0:QB, :] = vp_ref[0]
    vwin[QB:QB + TQ, :] = v_ref[0]
    vwin[QB + TQ:, :] = vn_ref[0]

    pext[0:QB - 8, :] = jnp.zeros((QB - 8, POOL_WIDTH), F32)
    pext[QB - 8:QB, :] = jnp.where(n > 0, pp_ref[0], 0.0)
    pext[QB:QB + TQ, :] = p_ref[0]
    pext[QB + TQ:QB + TQ + 8, :] = jnp.where(n < n_last, pn_ref[0], 0.0)
    pext[QB + TQ + 8:, :] = jnp.zeros((QB - 8, POOL_WIDTH), F32)

    row = lax.broadcasted_iota(jnp.int32, (QB, 3 * QB), 0)
    col = lax.broadcasted_iota(jnp.int32, (QB, 3 * QB), 1)
    in_band = (col >= row) & (col <= row + 2 * WINDOW)
    tok = lax.broadcasted_iota(jnp.int32, (QB, 1), 0)
    kc = kc_ref[0]
    kc_rows = [_stack_variants(kc, kv) for kv in range(N_KV_HEADS)]
    vc_rows = [_stack_variants(vc_ref[0], kv) for kv in range(N_KV_HEADS)]

    def sub_block(j, carry):
        r0 = pl.multiple_of(j * QB, QB)
        qj = q_ref[0, pl.ds(r0, QB), :]
        kw = kwin[pl.ds(r0, 3 * QB), :]
        vw = vwin[pl.ds(r0, 3 * QB), :]
        kpos = col + (n * TQ + j * QB - QB)
        ok = in_band & (kpos >= 0) & (kpos < seq_len)
        bias = jnp.where(ok, 0.0, -jnp.inf)
        n_loc, n_ctx = 3 * QB, kc.shape[0]
        bias2 = jnp.concatenate([bias, bias], axis=1)
        k_rows = [_stack_variants(kw, kv) for kv in range(N_KV_HEADS)]
        v_rows = [_stack_variants(vw, kv) for kv in range(N_KV_HEADS)]
        group = N_HEADS // N_KV_HEADS

        def head_tiles(head):
            loc0 = (head % 2) * n_loc
            ctx0 = 2 * n_loc + (head % 2) * n_ctx
            return ([loc0 + i * LANES for i in range(n_loc // LANES)]
                    + [ctx0 + i * LANES for i in range(n_ctx // LANES)])

        for c in range(N_HEADS // 2):
            qc = qj[:, c * LANES:(c + 1) * LANES]
            s_scr[c, :, 0:2 * n_loc] = _dot_nt(qc, k_rows[2 * c // group]) + bias2
            s_scr[c, :, 2 * n_loc:] = _dot_nt(qc, kc_rows[2 * c // group])
        for head in range(N_HEADS):
            tiles = [s_scr[head // 2, :, st:st + LANES] for st in head_tiles(head)]
            row_max = jnp.max(_fold(jnp.maximum, tiles), axis=1, keepdims=True)
            m_scr[head] = jnp.broadcast_to(jnp.maximum(row_max, sink_ref[head] * LOG2E), (QB, LANES))
        for head in range(N_HEADS):
            m = m_scr[head]
            acc = None
            for st in head_tiles(head):
                p = jnp.exp2(s_scr[head // 2, :, st:st + LANES] - m)
                p_scr[head // 2, :, st:st + LANES] = p.astype(BF16)
                acc = p if acc is None else acc + p
            denom = (jnp.broadcast_to(jnp.sum(acc, axis=1, keepdims=True), (QB, LANES))
                     + jnp.exp2(sink_ref[head] * LOG2E - m))
            m_scr[head] = 1.0 / denom
        lane = lax.broadcasted_iota(jnp.int32, (QB, LANES), 1)
        for c in range(N_HEADS // 2):
            o = (_dot(p_scr[c, :, 0:2 * n_loc], v_rows[2 * c // group])
                 + _dot(p_scr[c, :, 2 * n_loc:], vc_rows[2 * c // group]))
            o = o * jnp.where(lane < HEAD_DIM, m_scr[2 * c], m_scr[2 * c + 1])
            mix[pl.ds(r0, QB), c * LANES:(c + 1) * LANES] = o.astype(BF16)

        slab = pext[pl.ds(pl.multiple_of(r0 + POOL_OFF, 8), POOL_SLAB), :]
        tpos = tok + (n * TQ + j * QB)
        for g, w in enumerate(POOL_WINDOWS):
            sg = slab[:, g * LANES:(g + 1) * LANES]
            hi, lo = _split_bf16(sg)
            band = band_ref[g]
            wsum = _dot(band, hi) + _dot(band, lo)
            cnt = (jnp.minimum(tpos - w // 2 + w, seq_len) - jnp.maximum(tpos - w // 2, 0)).astype(F32)
            dlt = wsum / cnt - sg[POOL_OFF:POOL_OFF + QB, :]
            y = _dot(dlt.astype(BF16), poolw_ref[g]) * pscale_ref[:, g * LANES:(g + 1) * LANES]
            mix[pl.ds(r0, QB), ATTN_WIDTH + g * LANES:ATTN_WIDTH + (g + 1) * LANES] = y.astype(BF16)
        return carry

    lax.fori_loop(0, TQ // QB, sub_block, 0)

    proj = _dot(mix[...], wout_ref[...])
    x1 = x_ref[0] + mod_ref[0, 2:3, :] * proj
    x1_ref[0] = x1
    h2 = _norm_mod(x1, g2_ref[...], mod_ref[0, 3:4, :], mod_ref[0, 4:5, :])
    h2_ref[...] = _pack_words(h2)
    h_hi, h_lo = _split_bf16(h2)
    wrh = wrh_ref[...]
    lg_ref[...] = _dot_nt(wrh, h_hi) + _dot_nt(wrh, h_lo) + _dot_nt(wrl_ref[...], h_hi)


def _attn(b0, b, sink, q, k4, v4, kc4, vc4, p, x, mod, band, poolw_bf, pscale, wout_bf, g2, wr_hi, wr_lo):
    _, s, d = x.shape
    c = kc4.shape[1]
    nt = s // TQ
    hb = TQ // QB
    pb = TQ // 8
    kv_main = pl.BlockSpec((1, TQ, 4 * LANES), lambda bi, n: (b0 + bi, n, 0))
    kv_prev = pl.BlockSpec((1, QB, 4 * LANES), lambda bi, n: (b0 + bi, jnp.maximum(n * hb - 1, 0), 0))
    kv_next = pl.BlockSpec((1, QB, 4 * LANES), lambda bi, n: (b0 + bi, jnp.minimum((n + 1) * hb, s // QB - 1), 0))
    const2 = lambda bi, n: (0, 0)
    const3 = lambda bi, n: (0, 0, 0)
    return pl.pallas_call(
        functools.partial(_attn_kernel, s),
        out_shape=(jax.ShapeDtypeStruct((b, s, d), F32),
                   jax.ShapeDtypeStruct((b * s, d // 2), U32),
                   jax.ShapeDtypeStruct((N_EXPERTS, b * s), F32)),
        grid=(b, nt),
        in_specs=[pl.BlockSpec(memory_space=pltpu.SMEM),
                  pl.BlockSpec((1, TQ, ATTN_WIDTH), lambda bi, n: (b0 + bi, n, 0)),
                  kv_main, kv_prev, kv_next, kv_main, kv_prev, kv_next,
                  pl.BlockSpec((1, c, 4 * LANES), lambda bi, n: (b0 + bi, 0, 0)),
                  pl.BlockSpec((1, c, 4 * LANES), lambda bi, n: (b0 + bi, 0, 0)),
                  pl.BlockSpec((1, TQ, POOL_WIDTH), lambda bi, n: (b0 + bi, n, 0)),
                  pl.BlockSpec((1, 8, POOL_WIDTH), lambda bi, n: (b0 + bi, jnp.maximum(n * pb - 1, 0), 0)),
                  pl.BlockSpec((1, 8, POOL_WIDTH),
                               lambda bi, n: (b0 + bi, jnp.minimum((n + 1) * pb, s // 8 - 1), 0)),
                  pl.BlockSpec((1, TQ, d), lambda bi, n: (b0 + bi, n, 0)),
                  pl.BlockSpec((1, 6, d), lambda bi, n: (b0 + bi, 0, 0)),
                  pl.BlockSpec((len(POOL_WINDOWS), QB, POOL_SLAB), const3),
                  pl.BlockSpec((len(POOL_WINDOWS), POOL_GROUP_DIM, POOL_GROUP_DIM), const3),
                  pl.BlockSpec((1, POOL_WIDTH), const2),
                  pl.BlockSpec((d, d), const2),
                  pl.BlockSpec((1, d), const2),
                  pl.BlockSpec((N_EXPERTS, d), const2),
                  pl.BlockSpec((N_EXPERTS, d), const2)],
        out_specs=(pl.BlockSpec((1, TQ, d), lambda bi, n: (bi, n, 0)),
                   pl.BlockSpec((TQ, d // 2), lambda bi, n: (bi * nt + n, 0)),
                   pl.BlockSpec((N_EXPERTS, TQ), lambda bi, n: (0, bi * nt + n))),
        scratch_shapes=[pltpu.VMEM((TQ + 2 * QB, 4 * LANES), BF16),
                        pltpu.VMEM((TQ + 2 * QB, 4 * LANES), BF16),
                        pltpu.VMEM((TQ + 2 * QB, POOL_WIDTH), F32),
                        pltpu.VMEM((TQ, d), BF16),
                        pltpu.VMEM((N_HEADS // 2, QB, 2 * (3 * QB + c)), F32),
                        pltpu.VMEM((N_HEADS // 2, QB, 2 * (3 * QB + c)), BF16),
                        pltpu.VMEM((N_HEADS, QB, LANES), F32)],
        compiler_params=pltpu.CompilerParams(vmem_limit_bytes=VMEM_LIMIT),
        name="attn",
    )(sink, q, k4, k4, k4, v4, v4, v4, kc4, vc4, p, p, p, x, mod, band, poolw_bf, pscale, wout_bf,
      g2, wr_hi, wr_lo)


def _first_argmax_rows(v, row_iota, n_rows):
    m = jnp.max(v, axis=0, keepdims=True)
    idx = jnp.min(jnp.where(v == m, row_iota, n_rows), axis=0, keepdims=True)
    return m, idx


def _route_kernel(lg_ref, bias_ref, tri_ref, idx_ref, gate_ref, rank_ref, cnt_ref, carry):
    i = pl.program_id(0)

    @pl.when(i == 0)
    def _():
        carry[...] = jnp.zeros_like(carry)

    scores = 1.0 / (1.0 + jnp.exp(-lg_ref[...]))
    biased = scores + bias_ref[...]
    e_iota = lax.broadcasted_iota(jnp.int32, scores.shape, 0).astype(F32)
    g_iota = lax.broadcasted_iota(jnp.int32, (EXPERTS_PER_GROUP, TR), 0).astype(F32)
    neg = -jnp.inf

    grp = []
    for g in range(N_EXPERT_GROUPS):
        blk = biased[g * EXPERTS_PER_GROUP:(g + 1) * EXPERTS_PER_GROUP, :]
        m1, i1 = _first_argmax_rows(blk, g_iota, float(EXPERTS_PER_GROUP))
        m2 = jnp.max(jnp.where(g_iota == i1, neg, blk), axis=0, keepdims=True)
        grp.append(m1 + m2)
    grp = jnp.concatenate(grp, axis=0)
    gg_iota = lax.broadcasted_iota(jnp.int32, grp.shape, 0).astype(F32)
    grp_sel = jnp.zeros(grp.shape, F32)
    for _ in range(TOPK_GROUPS):
        _, gi = _first_argmax_rows(grp, gg_iota, float(N_EXPERT_GROUPS))
        hit = gg_iota == gi
        grp_sel = jnp.where(hit, 1.0, grp_sel)
        grp = jnp.where(hit, neg, grp)
    allowed = jnp.concatenate(
        [jnp.broadcast_to(grp_sel[g:g + 1, :], (EXPERTS_PER_GROUP, TR)) for g in range(N_EXPERT_GROUPS)], axis=0)
    masked = jnp.where(allowed > 0.5, biased, neg)

    idxs, gates = [], []
    onehot = jnp.zeros(scores.shape, F32)
    for _ in range(TOP_K):
        _, ei = _first_argmax_rows(masked, e_iota, float(N_EXPERTS))
        hit = e_iota == ei
        idxs.append(ei)
        gates.append(jnp.sum(jnp.where(hit, scores, 0.0), axis=0, keepdims=True))
        onehot = jnp.where(hit, 1.0, onehot)
        masked = jnp.where(hit, neg, masked)
    idx = jnp.concatenate(idxs, axis=0)
    gate = jnp.concatenate(gates, axis=0)
    gate = gate / jnp.sum(gate, axis=0, keepdims=True) * ROUTED_SCALE

    before = _dot(onehot.astype(BF16), tri_ref[...]) + carry[:, 0:1]
    ranks = [jnp.sum(jnp.where(e_iota == idxs[k], before, 0.0), axis=0, keepdims=True) for k in range(TOP_K)]
    idx_ref[...] = idx.astype(jnp.int32)
    gate_ref[...] = gate
    rank_ref[...] = jnp.concatenate(ranks, axis=0).astype(jnp.int32)
    total = carry[...] + jnp.sum(onehot, axis=1, keepdims=True)
    carry[...] = total
    cnt_ref[...] = total


def _route(lg_t, bias, tri):
    e, t = lg_t.shape
    tok = pl.BlockSpec((TOP_K, TR), lambda i: (0, i))
    return pl.pallas_call(
        _route_kernel,
        out_shape=(jax.ShapeDtypeStruct((TOP_K, t), jnp.int32),
                   jax.ShapeDtypeStruct((TOP_K, t), F32),
                   jax.ShapeDtypeStruct((TOP_K, t), jnp.int32),
                   jax.ShapeDtypeStruct((e, LANES), F32)),
        grid=(t // TR,),
        in_specs=[pl.BlockSpec((e, TR), lambda i: (0, i)),
                  pl.BlockSpec((e, 1), lambda i: (0, 0)),
                  pl.BlockSpec((TR, TR), lambda i: (0, 0))],
        out_specs=(tok, tok, tok, pl.BlockSpec((e, LANES), lambda i: (0, 0))),
        scratch_shapes=[pltpu.VMEM((e, LANES), F32)],
        compiler_params=pltpu.CompilerParams(vmem_limit_bytes=VMEM_LIMIT),
        name="route",
    )(lg_t, bias, tri)


def _plan_kernel(n_blocks, size_ref, start_ref, expert_ref, valid_ref, nact_ref):
    def per_expert(e, first_block):
        size = size_ref[e]
        n_blk = (size + TM_EXP - 1) // TM_EXP
        start_ref[e] = first_block * TM_EXP

        def per_block(j, carry):
            expert_ref[first_block + j] = e
            valid_ref[first_block + j] = jnp.minimum(size - j * TM_EXP, TM_EXP)
            return carry

        lax.fori_loop(0, n_blk, per_block, 0)
        return first_block + n_blk

    n_active = lax.fori_loop(0, N_EXPERTS, per_expert, 0)
    nact_ref[0] = n_active

    def unused(i, carry):
        expert_ref[i] = N_EXPERTS - 1
        valid_ref[i] = 0
        return carry

    lax.fori_loop(n_active, n_blocks, unused, 0)


def _plan_blocks(sizes, n_blocks):
    smem = pl.BlockSpec(memory_space=pltpu.SMEM)
    return pl.pallas_call(
        functools.partial(_plan_kernel, n_blocks),
        out_shape=(jax.ShapeDtypeStruct((N_EXPERTS,), jnp.int32),
                   jax.ShapeDtypeStruct((n_blocks,), jnp.int32),
                   jax.ShapeDtypeStruct((n_blocks,), jnp.int32),
                   jax.ShapeDtypeStruct((1,), jnp.int32)),
        in_specs=[smem],
        out_specs=(smem, smem, smem, smem),
        name="plan_blocks",
    )(sizes)


def _dest_kernel(start_ref, idx_ref, rank_ref, dest_ref):
    idx = idx_ref[...]
    dest = rank_ref[...]
    for e in range(N_EXPERTS):
        dest = dest + jnp.where(idx == e, start_ref[e], 0)
    dest_ref[...] = dest


def _dest_rows(pad_start, idx_kt, rank_kt):
    n_k, t = idx_kt.shape
    tile = 4096
    blk = pl.BlockSpec((n_k, tile), lambda i: (0, i))
    return pl.pallas_call(
        _dest_kernel,
        out_shape=jax.ShapeDtypeStruct((n_k, t), jnp.int32),
        grid=(t // tile,),
        in_specs=[pl.BlockSpec(memory_space=pltpu.SMEM), blk, blk],
        out_specs=blk,
        name="dest_rows",
    )(pad_start, idx_kt, rank_kt)


def _sc_mesh():
    return plsc.VectorSubcoreMesh(core_axis_name="c", subcore_axis_name="s")


def _sc_token_base(steps, j):
    worker = lax.axis_index("s") * SC_CORES + lax.axis_index("c")
    return (worker * steps + j) * SC_TOKENS


def _sc_scatter(dest_kt, h2p, n_rows):
    t, width = h2p.shape
    steps = t // (SC_WORKERS * SC_TOKENS)

    @functools.partial(
        pl.kernel, mesh=_sc_mesh(),
        out_type=jax.ShapeDtypeStruct((n_rows, width), U32),
        scratch_types=[pltpu.VMEM((TOP_K, SC_TOKENS), jnp.int32),
                       pltpu.VMEM((SC_TOKENS, width), U32),
                       pltpu.SemaphoreType.DMA],
        name="sc_scatter",
    )
    def body(dest_hbm, h_hbm, xs_hbm, idx_v, rows_v, sem):
        @pl.loop(0, steps)
        def _(j):
            base = _sc_token_base(steps, j)
            pltpu.sync_copy(dest_hbm.at[:, pl.ds(base, SC_TOKENS)], idx_v)
            pltpu.sync_copy(h_hbm.at[pl.ds(base, SC_TOKENS)], rows_v)
            copies = [pltpu.async_copy(rows_v, xs_hbm.at[idx_v.at[k]], sem) for k in range(TOP_K)]
            for cp in copies:
                cp.wait()

    return body(dest_kt, h2p)


def _sc_gather(dest_kt, ys, token0, n_tokens):
    n_k = dest_kt.shape[0]
    width = ys.shape[1]
    steps = n_tokens // (SC_WORKERS * SC_TOKENS)

    half = SC_TOKENS // 2
    units = [(k, h) for k in range(n_k) for h in range(2)]
    n_buf = 3

    @functools.partial(
        pl.kernel, mesh=_sc_mesh(),
        out_type=jax.ShapeDtypeStruct((n_k, n_tokens, width), U32),
        scratch_types=[pltpu.VMEM((n_k, SC_TOKENS), jnp.int32),
                       pltpu.VMEM((n_buf, half, width), U32),
                       pltpu.SemaphoreType.DMA((n_buf,))],
        name="sc_gather",
    )
    def body(dest_hbm, ys_hbm, yk_hbm, idx_v, rows_v, sems):
        @pl.loop(0, steps)
        def _(j):
            base = _sc_token_base(steps, j)
            pltpu.sync_copy(dest_hbm.at[:, pl.ds(token0 + base, SC_TOKENS)], idx_v)

            def gather(u):
                k, h = units[u]
                slot = u % n_buf
                return pltpu.make_async_copy(ys_hbm.at[idx_v.at[k, pl.ds(h * half, half)]], rows_v.at[slot],
                                             sems.at[slot])

            gather(0).start()
            gather(1).start()
            for u, (k, h) in enumerate(units):
                gather(u).wait()
                pltpu.sync_copy(rows_v.at[u % n_buf], yk_hbm.at[k, pl.ds(base + h * half, half)])
                if u + 2 < len(units):
                    gather(u + 2).start()

    return body(dest_kt, ys)


def _experts_kernel(be_ref, valid_ref, nact_ref, xs_ref, wg_ref, wu_ref, wd_ref, ys_ref, wg_bf, wu_bf, wd_bf):
    i = pl.program_id(0)
    prev = be_ref[jnp.maximum(i - 1, 0)]

    @pl.when((i == 0) | (be_ref[i] != prev))
    def _():
        wg_bf[...] = wg_ref[0].astype(BF16)
        wu_bf[...] = wu_ref[0].astype(BF16)
        wd_bf[...] = wd_ref[0].astype(BF16)

    valid = jnp.where(i < nact_ref[0], valid_ref[i], 0)

    def run(rows):
        row = lax.broadcasted_iota(jnp.int32, (rows, xs_ref.shape[1]), 0)
        words = jnp.where(row < valid, xs_ref[0:rows, :], jnp.uint32(0))
        xb = _unpack_words(words).astype(BF16)
        hid = _silu(_dot(xb, wg_bf[...])) * _dot(xb, wu_bf[...])
        ys_ref[0:rows, :] = _pack_words(_dot(hid.astype(BF16), wd_bf[...]))
        if rows < TM_EXP:
            ys_ref[rows:, :] = jnp.zeros((TM_EXP - rows, ys_ref.shape[1]), U32)

    rows, lower = TM_EXP, TM_EXP // 2
    while rows >= EXP_MIN_ROWS:
        lo = lower if rows > EXP_MIN_ROWS else 0
        pl.when((valid > lo) & (valid <= rows))(functools.partial(run, rows))
        rows, lower = rows // 2, lower // 2

    @pl.when(valid == 0)
    def _():
        ys_ref[...] = jnp.zeros_like(ys_ref)


def _experts(block_e, block_valid, n_active, xs, w_gate, w_up, w_down):
    n_rows, half = xs.shape
    d = w_gate.shape[1]
    nb = n_rows // TM_EXP
    grid_spec = pltpu.PrefetchScalarGridSpec(
        num_scalar_prefetch=3,
        grid=(nb,),
        in_specs=[pl.BlockSpec((TM_EXP, half), lambda i, be, bv, na: (jnp.minimum(i, na[0] - 1), 0)),
                  pl.BlockSpec((1, d, D_EXPERT), lambda i, be, bv, na: (be[i], 0, 0)),
                  pl.BlockSpec((1, d, D_EXPERT), lambda i, be, bv, na: (be[i], 0, 0)),
                  pl.BlockSpec((1, D_EXPERT, d), lambda i, be, bv, na: (be[i], 0, 0))],
        out_specs=pl.BlockSpec((TM_EXP, half), lambda i, be, bv, na: (i, 0)),
        scratch_shapes=[pltpu.VMEM((d, D_EXPERT), BF16), pltpu.VMEM((d, D_EXPERT), BF16),
                        pltpu.VMEM((D_EXPERT, d), BF16)],
    )
    return pl.pallas_call(
        _experts_kernel,
        out_shape=jax.ShapeDtypeStruct((n_rows, half), U32),
        grid_spec=grid_spec,
        compiler_params=pltpu.CompilerParams(vmem_limit_bytes=VMEM_LIMIT),
        name="experts",
    )(block_e, block_valid, n_active, xs, w_gate, w_up, w_down)


def _combine_kernel(yk_ref, x1_ref, h2_ref, gate_ref, mod_ref, wsg_ref, wsu_ref, wsd_ref, fg_ref, *out_refs):
    out_ref = out_refs[-1]
    hb = _unpack_words(h2_ref[...]).astype(BF16)
    hid = _silu(_dot(hb, wsg_ref[...])) * _dot(hb, wsu_ref[...])
    ffn = _dot(hid.astype(BF16), wsd_ref[...])
    gate = gate_ref[...]
    for k in range(TOP_K):
        ffn = ffn + gate[:, k:k + 1] * _unpack_words(yk_ref[k])
    x2 = x1_ref[...] + mod_ref[0, 5:6, :] * ffn
    ms = jnp.mean(x2 * x2, axis=-1, keepdims=True)
    out_ref[...] = x2 * lax.rsqrt(ms + EPS) * fg_ref[...]


def _combine(yk, token0, out_token0, n_out_tokens, prev_out, x1, h2p, gate_tk, mod, wsg_bf, wsu_bf, wsd_bf, final_g,
             seq_len):
    t, d = n_out_tokens, x1.shape[1]
    tiles_per_seq = seq_len // TF
    tile0 = token0 // TF
    out_tile0 = out_token0 // TF
    in_tok = pl.BlockSpec((TF, d), lambda i: (tile0 + i, 0))
    tok = pl.BlockSpec((TF, d), lambda i: (out_tile0 + i, 0))
    in_specs = [pl.BlockSpec((TOP_K, TF, d // 2), lambda i: (0, i, 0)),
                in_tok, pl.BlockSpec((TF, d // 2), lambda i: (tile0 + i, 0)),
                pl.BlockSpec((TF, TOP_K), lambda i: (tile0 + i, 0)),
                pl.BlockSpec((1, 6, d), lambda i: ((out_tile0 + i) // tiles_per_seq, 0, 0)),
                pl.BlockSpec((d, D_SHARED), lambda i: (0, 0)),
                pl.BlockSpec((d, D_SHARED), lambda i: (0, 0)),
                pl.BlockSpec((D_SHARED, d), lambda i: (0, 0)),
                pl.BlockSpec((1, d), lambda i: (0, 0))]
    args = [yk, x1, h2p, gate_tk, mod, wsg_bf, wsu_bf, wsd_bf, final_g]
    aliases = {}
    if prev_out is not None:
        in_specs.append(pl.BlockSpec(memory_space=pl.ANY))
        args.append(prev_out)
        aliases = {len(args) - 1: 0}
    return pl.pallas_call(
        _combine_kernel,
        out_shape=jax.ShapeDtypeStruct((t, d), F32),
        grid=(yk.shape[1] // TF,),
        in_specs=in_specs,
        out_specs=tok,
        input_output_aliases=aliases,
        compiler_params=pltpu.CompilerParams(vmem_limit_bytes=VMEM_LIMIT),
        name="combine",
    )(*args)


def _rope_tables(n_tokens):
    n_rows = n_tokens // GRID_W
    n_freq = HEAD_DIM // 4
    inv_freq = ROPE_THETA ** (-jnp.arange(n_freq, dtype=F32) / n_freq)
    ang_r = jnp.arange(n_rows).astype(F32)[:, None] * inv_freq[None, :]
    ang_c = jnp.arange(GRID_W).astype(F32)[:, None] * inv_freq[None, :]

    def per_token(row_part, col_part):
        rows = jnp.broadcast_to(row_part[:, None, :], (n_rows, GRID_W, n_freq))
        cols = jnp.broadcast_to(col_part[None, :, :], (n_rows, GRID_W, n_freq))
        return rows.reshape(n_tokens, n_freq), cols.reshape(n_tokens, n_freq)

    cos_r, cos_c = per_token(jnp.cos(ang_r), jnp.cos(ang_c))
    sin_r, sin_c = per_token(jnp.sin(ang_r), jnp.sin(ang_c))
    cos = jnp.concatenate([cos_r, cos_r, cos_c, cos_c], axis=1)
    sin = jnp.concatenate([-sin_r, sin_r, -sin_c, sin_c], axis=1)
    reps = LANES // HEAD_DIM
    return jnp.tile(cos, (1, reps)), jnp.tile(sin, (1, reps))


def _pool_bands():
    i = jnp.arange(QB)[:, None]
    r = jnp.arange(POOL_SLAB)[None, :]
    return jnp.stack([((r >= i + POOL_OFF - w // 2) & (r < i + POOL_OFF + w // 2)).astype(BF16)
                      for w in POOL_WINDOWS])


def kernel(x, c, ctx, c_ctx, w_ada, b_ada, norm1_g, norm2_g, w_in, attn_sink, pool_w, pool_scale, w_out,
           w_router, router_bias, w_gate, w_up, w_down, ws_gate, ws_up, ws_down, final_g):
    b, s, d = x.shape
    t = b * s
    assert w_ada.shape[0] == 1 and d == D_MODEL and s % TQ == 0 and b + 1 <= 8

    c8 = jnp.zeros((8, d), F32).at[:b].set(c).at[b].set(c_ctx)
    mod = _ada(c8, w_ada[0], b_ada[0]).reshape(8, 6, d)
    g1 = norm1_g[0].reshape(1, d)
    g2 = norm2_g[0].reshape(1, d)
    w_in_bf = w_in[0].astype(BF16)
    cos_t, sin_t = _rope_tables(s)

    q, k4, v4, p = _inproj(x, mod, g1, w_in_bf, cos_t, sin_t)
    kc4, vc4 = _ctxproj(ctx, mod[b:b + 1], g1, w_in_bf[:, ATTN_WIDTH:ATTN_WIDTH + 2 * KV_WIDTH])

    wr_t = w_router[0].T
    wr_hi = wr_t.astype(BF16)
    wr_lo = (wr_t - wr_hi.astype(F32)).astype(BF16)
    attn_consts = (_pool_bands(), pool_w[0].astype(BF16), pool_scale[0].reshape(1, POOL_WIDTH),
                   w_out[0].astype(BF16), g2, wr_hi, wr_lo)
    tri = jnp.triu(jnp.ones((TR, TR), BF16), k=1)
    shared_w = (ws_gate[0].astype(BF16), ws_up[0].astype(BF16), ws_down[0].astype(BF16))

    nb = b // TOKEN_GROUPS
    tg = nb * s
    assert b % TOKEN_GROUPS == 0 and tg % (COMBINE_CHUNKS * SC_WORKERS * SC_TOKENS) == 0
    n_rows = -(-(tg * TOP_K + N_EXPERTS * (TM_EXP - 1)) // TM_EXP) * TM_EXP
    groups = []
    for g in range(TOKEN_GROUPS):
        x1, h2p, lg_t = _attn(g * nb, nb, attn_sink[0], q, k4, v4, kc4, vc4, p, x, mod, *attn_consts)
        idx_kt, gate_kt, rank_kt, counts = _route(lg_t, router_bias[0].reshape(N_EXPERTS, 1), tri)
        pad_start, block_e, block_valid, n_active = _plan_blocks(counts[:, 0].astype(jnp.int32), n_rows // TM_EXP)
        dest_kt = _dest_rows(pad_start, idx_kt, rank_kt)
        xs = _sc_scatter(dest_kt, h2p, n_rows)
        groups.append((x1.reshape(tg, d), h2p, gate_kt.T, dest_kt, xs, block_e, block_valid, n_active))

    out = None
    chunk = tg // COMBINE_CHUNKS
    for g, (x1, h2p, gate_tk, dest_kt, xs, block_e, block_valid, n_active) in enumerate(groups):
        ys = _experts(block_e, block_valid, n_active, xs, w_gate[0], w_up[0], w_down[0])
        for token0 in range(0, tg, chunk):
            out = _combine(_sc_gather(dest_kt, ys, token0, chunk), token0, g * tg + token0, t, out, x1, h2p,
                           gate_tk, mod, *shared_w, final_g.reshape(1, d), s)
    return out.reshape(b, s, d)
```

```python
import functools

import jax
import jax.numpy as jnp
from jax import lax
from jax.experimental import pallas as pl
from jax.experimental.pallas import tpu as pltpu
from jax.experimental.pallas import tpu_sc as plsc

F32 = jnp.float32
BF16 = jnp.bfloat16

D_MODEL = 1024
GRID_W = 64
N_HEADS = 8
N_KV_HEADS = 2
HEAD_DIM = 64
ATTN_WIDTH = N_HEADS * HEAD_DIM
KV_WIDTH = N_KV_HEADS * HEAD_DIM
WINDOW = 128
ROPE_THETA = 10000.0
POOL_WINDOWS = (2, 4, 8, 16)
POOL_WIDTH = D_MODEL - ATTN_WIDTH
POOL_GROUP_DIM = POOL_WIDTH // len(POOL_WINDOWS)
IN_COLS = ATTN_WIDTH + 2 * KV_WIDTH + POOL_WIDTH
N_EXPERTS = 64
TOP_K = 8
N_EXPERT_GROUPS = 8
EXPERTS_PER_GROUP = N_EXPERTS // N_EXPERT_GROUPS
TOPK_GROUPS = 4
D_EXPERT = 256
D_SHARED = 256
ROUTED_SCALE = 2.5
EPS = 1e-6
LOG2E = 1.4426950408889634

LANES = 128
U32 = jnp.uint32
VMEM_LIMIT = 48 * 1024 * 1024

TM_PROJ = 1024
TQ = 1024
QB = 128
POOL_SLAB = 256
POOL_OFF = 64
TR = 512
TM_EXP = 1024
EXP_MIN_ROWS = 256
TF = 256
TOKEN_GROUPS = 1
COMBINE_CHUNKS = 4
SC_CORES = 2
SC_WORKERS = 32
SC_TOKENS = 128


def _silu(x):
    return x * (1.0 / (1.0 + jnp.exp(-x)))


def _split_bf16(x):
    hi = x.astype(BF16)
    lo = (x - hi.astype(F32)).astype(BF16)
    return hi, lo


def _dot(a, b):
    return jnp.dot(a, b, preferred_element_type=F32)


def _pack_words(val):
    half = val.shape[1] // 2
    lo = lax.bitcast_convert_type(val[:, :half].astype(BF16).astype(F32), U32)
    hi = lax.bitcast_convert_type(val[:, half:].astype(BF16).astype(F32), U32)
    return lax.shift_right_logical(lo, jnp.uint32(16)) | hi


def _unpack_words(words):
    lo = lax.bitcast_convert_type(lax.shift_left(words, jnp.uint32(16)), F32)
    hi = lax.bitcast_convert_type(words & jnp.uint32(0xFFFF0000), F32)
    return jnp.concatenate([lo, hi], axis=1)


def _dot_nt(a, b):
    return lax.dot_general(a, b, (((1,), (1,)), ((), ())), preferred_element_type=F32)


def _ada_kernel(c_ref, w_ref, b_ref, o_ref):
    a_hi, a_lo = _split_bf16(_silu(c_ref[...]))
    w_hi, w_lo = _split_bf16(w_ref[...])
    o_ref[...] = _dot(a_hi, w_hi) + _dot(a_lo, w_hi) + _dot(a_hi, w_lo) + b_ref[...]


def _ada(c8, w_ada, b_ada):
    d = c8.shape[1]
    n = w_ada.shape[1]
    tn = 512
    return pl.pallas_call(
        _ada_kernel,
        out_shape=jax.ShapeDtypeStruct((8, n), F32),
        grid=(n // tn,),
        in_specs=[pl.BlockSpec((8, d), lambda j: (0, 0)),
                  pl.BlockSpec((d, tn), lambda j: (0, j)),
                  pl.BlockSpec((1, tn), lambda j: (0, j))],
        out_specs=pl.BlockSpec((8, tn), lambda j: (0, j)),
        compiler_params=pltpu.CompilerParams(vmem_limit_bytes=VMEM_LIMIT),
        name="ada",
    )(c8, w_ada, b_ada.reshape(1, n))


def _norm_mod(x, g, shift, scale):
    ms = jnp.mean(x * x, axis=-1, keepdims=True)
    return (x * lax.rsqrt(ms + EPS) * g) * (1.0 + scale) + shift


def _lane_variants(t):
    lane = lax.broadcasted_iota(jnp.int32, t.shape, 1)
    lo = lane < HEAD_DIM
    tr = pltpu.roll(t, HEAD_DIM, 1)
    zero = jnp.zeros_like(t)
    return (jnp.where(lo, t, zero), jnp.where(lo, zero, tr),
            jnp.where(lo, tr, zero), jnp.where(lo, zero, t))


def _store_variants(ref, t):
    for i, var in enumerate(_lane_variants(t)):
        ref[0, :, i * LANES:(i + 1) * LANES] = var.astype(BF16)


def _inproj_kernel(x_ref, mod_ref, g_ref, w_ref, cos_ref, sin_ref, q_ref, k_ref, v_ref, p_ref):
    h = _norm_mod(x_ref[0], g_ref[...], mod_ref[0, 0:1, :], mod_ref[0, 1:2, :])
    z = _dot(h.astype(BF16), w_ref[...])
    cos = cos_ref[...]
    sin = sin_ref[...]
    lane = lax.broadcasted_iota(jnp.int32, cos.shape, 1)
    first_half = (lane & 16) == 0

    def rope(zc):
        partner = jnp.where(first_half, pltpu.roll(zc, LANES - 16, 1), pltpu.roll(zc, 16, 1))
        return zc * cos + partner * sin

    scale = HEAD_DIM ** -0.5 * LOG2E
    for c in range(ATTN_WIDTH // LANES):
        q_ref[0, :, c * LANES:(c + 1) * LANES] = (rope(z[:, c * LANES:(c + 1) * LANES]) * scale).astype(BF16)
    _store_variants(k_ref, rope(z[:, ATTN_WIDTH:ATTN_WIDTH + KV_WIDTH]))
    _store_variants(v_ref, z[:, ATTN_WIDTH + KV_WIDTH:ATTN_WIDTH + 2 * KV_WIDTH])
    p_ref[0] = z[:, ATTN_WIDTH + 2 * KV_WIDTH:]


def _inproj(x, mod, g1, w_in_bf, cos_t, sin_t):
    b, s, d = x.shape
    tm = TM_PROJ
    return pl.pallas_call(
        _inproj_kernel,
        out_shape=(jax.ShapeDtypeStruct((b, s, ATTN_WIDTH), BF16),
                   jax.ShapeDtypeStruct((b, s, 4 * LANES), BF16),
                   jax.ShapeDtypeStruct((b, s, 4 * LANES), BF16),
                   jax.ShapeDtypeStruct((b, s, POOL_WIDTH), F32)),
        grid=(s // tm, b),
        in_specs=[pl.BlockSpec((1, tm, d), lambda n, bi: (bi, n, 0)),
                  pl.BlockSpec((1, 6, d), lambda n, bi: (bi, 0, 0)),
                  pl.BlockSpec((1, d), lambda n, bi: (0, 0)),
                  pl.BlockSpec((d, IN_COLS), lambda n, bi: (0, 0)),
                  pl.BlockSpec((tm, LANES), lambda n, bi: (n, 0)),
                  pl.BlockSpec((tm, LANES), lambda n, bi: (n, 0))],
        out_specs=(pl.BlockSpec((1, tm, ATTN_WIDTH), lambda n, bi: (bi, n, 0)),
                   pl.BlockSpec((1, tm, 4 * LANES), lambda n, bi: (bi, n, 0)),
                   pl.BlockSpec((1, tm, 4 * LANES), lambda n, bi: (bi, n, 0)),
                   pl.BlockSpec((1, tm, POOL_WIDTH), lambda n, bi: (bi, n, 0))),
        compiler_params=pltpu.CompilerParams(vmem_limit_bytes=VMEM_LIMIT),
        name="inproj",
    )(x, mod, g1, w_in_bf, cos_t, sin_t)


def _ctxproj_kernel(x_ref, mod_ref, g_ref, w_ref, k_ref, v_ref):
    h = _norm_mod(x_ref[0], g_ref[...], mod_ref[0, 0:1, :], mod_ref[0, 1:2, :])
    z = _dot(h.astype(BF16), w_ref[...])
    _store_variants(k_ref, z[:, :KV_WIDTH])
    _store_variants(v_ref, z[:, KV_WIDTH:])


def _ctxproj(ctx, mod_c, g1, w_kv_bf):
    b, c, d = ctx.shape
    return pl.pallas_call(
        _ctxproj_kernel,
        out_shape=(jax.ShapeDtypeStruct((b, c, 4 * LANES), BF16),
                   jax.ShapeDtypeStruct((b, c, 4 * LANES), BF16)),
        grid=(b,),
        in_specs=[pl.BlockSpec((1, c, d), lambda bi: (bi, 0, 0)),
                  pl.BlockSpec((1, 6, d), lambda bi: (0, 0, 0)),
                  pl.BlockSpec((1, d), lambda bi: (0, 0)),
                  pl.BlockSpec((d, 2 * KV_WIDTH), lambda bi: (0, 0))],
        out_specs=(pl.BlockSpec((1, c, 4 * LANES), lambda bi: (bi, 0, 0)),
                   pl.BlockSpec((1, c, 4 * LANES), lambda bi: (bi, 0, 0))),
        compiler_params=pltpu.CompilerParams(vmem_limit_bytes=VMEM_LIMIT),
        name="ctxproj",
    )(ctx, mod_c, g1, w_kv_bf)


def _fold(op, tiles):
    while len(tiles) > 1:
        tiles = [op(tiles[i], tiles[i + 1]) if i + 1 < len(tiles) else tiles[i] for i in range(0, len(tiles), 2)]
    return tiles[0]


def _stack_variants(t4, kv):
    return jnp.concatenate([t4[:, (2 * kv) * LANES:(2 * kv + 1) * LANES],
                            t4[:, (2 * kv + 1) * LANES:(2 * kv + 2) * LANES]], axis=0)


def _attn_kernel(seq_len, sink_ref, q_ref, k_ref, kp_ref, kn_ref, v_ref, vp_ref, vn_ref, kc_ref, vc_ref,
                 p_ref, pp_ref, pn_ref, x_ref, mod_ref, band_ref, poolw_ref, pscale_ref, wout_ref,
                 g2_ref, wrh_ref, wrl_ref, x1_ref, h2_ref, lg_ref, kwin, vwin, pext, mix, s_scr, p_scr, m_scr):
    n = pl.program_id(1)
    n_last = pl.num_programs(1) - 1

    kwin[0:QB, :] = kp_ref[0]
    kwin[QB:QB + TQ, :] = k_ref[0]
    kwin[QB + TQ:, :] = kn_ref[0]
    vwin[0:QB, :] = vp_ref[0]
    vwin[QB:QB + TQ, :] = v_ref[0]
    vwin[QB + TQ:, :] = vn_ref[0]

    pext[0:QB - 8, :] = jnp.zeros((QB - 8, POOL_WIDTH), F32)
    pext[QB - 8:QB, :] = jnp.where(n > 0, pp_ref[0], 0.0)
    pext[QB:QB + TQ, :] = p_ref[0]
    pext[QB + TQ:QB + TQ + 8, :] = jnp.where(n < n_last, pn_ref[0], 0.0)
    pext[QB + TQ + 8:, :] = jnp.zeros((QB - 8, POOL_WIDTH), F32)

    row = lax.broadcasted_iota(jnp.int32, (QB, 3 * QB), 0)
    col = lax.broadcasted_iota(jnp.int32, (QB, 3 * QB), 1)
    in_band = (col >= row) & (col <= row + 2 * WINDOW)
    tok = lax.broadcasted_iota(jnp.int32, (QB, 1), 0)
    kc = kc_ref[0]
    kc_rows = [_stack_variants(kc, kv) for kv in range(N_KV_HEADS)]
    vc_rows = [_stack_variants(vc_ref[0], kv) for kv in range(N_KV_HEADS)]

    def sub_block(j, carry):
        r0 = pl.multiple_of(j * QB, QB)
        qj = q_ref[0, pl.ds(r0, QB), :]
        kw = kwin[pl.ds(r0, 3 * QB), :]
        vw = vwin[pl.ds(r0, 3 * QB), :]
        kpos = col + (n * TQ + j * QB - QB)
        ok = in_band & (kpos >= 0) & (kpos < seq_len)
        bias = jnp.where(ok, 0.0, -jnp.inf)
        n_loc, n_ctx = 3 * QB, kc.shape[0]
        bias2 = jnp.concatenate([bias, bias], axis=1)
        k_rows = [_stack_variants(kw, kv) for kv in range(N_KV_HEADS)]
        v_rows = [_stack_variants(vw, kv) for kv in range(N_KV_HEADS)]
        group = N_HEADS // N_KV_HEADS

        def head_tiles(head):
            loc0 = (head % 2) * n_loc
            ctx0 = 2 * n_loc + (head % 2) * n_ctx
            return ([loc0 + i * LANES for i in range(n_loc // LANES)]
                    + [ctx0 + i * LANES for i in range(n_ctx // LANES)])

        for c in range(N_HEADS // 2):
            qc = qj[:, c * LANES:(c + 1) * LANES]
            s_scr[c, :, 0:2 * n_loc] = _dot_nt(qc, k_rows[2 * c // group]) + bias2
            s_scr[c, :, 2 * n_loc:] = _dot_nt(qc, kc_rows[2 * c // group])
        for head in range(N_HEADS):
            tiles = [s_scr[head // 2, :, st:st + LANES] for st in head_tiles(head)]
            row_max = jnp.max(_fold(jnp.maximum, tiles), axis=1, keepdims=True)
            m_scr[head] = jnp.broadcast_to(jnp.maximum(row_max, sink_ref[head] * LOG2E), (QB, LANES))
        for head in range(N_HEADS):
            m = m_scr[head]
            acc = None
            for st in head_tiles(head):
                p = jnp.exp2(s_scr[head // 2, :, st:st + LANES] - m)
                p_scr[head // 2, :, st:st + LANES] = p.astype(BF16)
                acc = p if acc is None else acc + p
            denom = (jnp.broadcast_to(jnp.sum(acc, axis=1, keepdims=True), (QB, LANES))
                     + jnp.exp2(sink_ref[head] * LOG2E - m))
            m_scr[head] = 1.0 / denom
        lane = lax.broadcasted_iota(jnp.int32, (QB, LANES), 1)
        for c in range(N_HEADS // 2):
            o = (_dot(p_scr[c, :, 0:2 * n_loc], v_rows[2 * c // group])
                 + _dot(p_scr[c, :, 2 * n_loc:], vc_rows[2 * c // group]))
            o = o * jnp.where(lane < HEAD_DIM, m_scr[2 * c], m_scr[2 * c + 1])
            mix[pl.ds(r0, QB), c * LANES:(c + 1) * LANES] = o.astype(BF16)

        slab = pext[pl.ds(pl.multiple_of(r0 + POOL_OFF, 8), POOL_SLAB), :]
        tpos = tok + (n * TQ + j * QB)
        for g, w in enumerate(POOL_WINDOWS):
            sg = slab[:, g * LANES:(g + 1) * LANES]
            hi, lo = _split_bf16(sg)
            band = band_ref[g]
            wsum = _dot(band, hi) + _dot(band, lo)
            cnt = (jnp.minimum(tpos - w // 2 + w, seq_len) - jnp.maximum(tpos - w // 2, 0)).astype(F32)
            dlt = wsum / cnt - sg[POOL_OFF:POOL_OFF + QB, :]
            y = _dot(dlt.astype(BF16), poolw_ref[g]) * pscale_ref[:, g * LANES:(g + 1) * LANES]
            mix[pl.ds(r0, QB), ATTN_WIDTH + g * LANES:ATTN_WIDTH + (g + 1) * LANES] = y.astype(BF16)
        return carry

    lax.fori_loop(0, TQ // QB, sub_block, 0)

    proj = _dot(mix[...], wout_ref[...])
    x1 = x_ref[0] + mod_ref[0, 2:3, :] * proj
    x1_ref[0] = x1
    h2 = _norm_mod(x1, g2_ref[...], mod_ref[0, 3:4, :], mod_ref[0, 4:5, :])
    h2_ref[...] = _pack_words(h2)
    h_hi, h_lo = _split_bf16(h2)
    wrh = wrh_ref[...]
    lg_ref[...] = _dot_nt(wrh, h_hi) + _dot_nt(wrh, h_lo) + _dot_nt(wrl_ref[...], h_hi)


def _attn(b0, b, sink, q, k4, v4, kc4, vc4, p, x, mod, band, poolw_bf, pscale, wout_bf, g2, wr_hi, wr_lo):
    _, s, d = x.shape
    c = kc4.shape[1]
    nt = s // TQ
    hb = TQ // QB
    pb = TQ // 8
    kv_main = pl.BlockSpec((1, TQ, 4 * LANES), lambda bi, n: (b0 + bi, n, 0))
    kv_prev = pl.BlockSpec((1, QB, 4 * LANES), lambda bi, n: (b0 + bi, jnp.maximum(n * hb - 1, 0), 0))
    kv_next = pl.BlockSpec((1, QB, 4 * LANES), lambda bi, n: (b0 + bi, jnp.minimum((n + 1) * hb, s // QB - 1), 0))
    const2 = lambda bi, n: (0, 0)
    const3 = lambda bi, n: (0, 0, 0)
    return pl.pallas_call(
        functools.partial(_attn_kernel, s),
        out_shape=(jax.ShapeDtypeStruct((b, s, d), F32),
                   jax.ShapeDtypeStruct((b * s, d // 2), U32),
                   jax.ShapeDtypeStruct((N_EXPERTS, b * s), F32)),
        grid=(b, nt),
        in_specs=[pl.BlockSpec(memory_space=pltpu.SMEM),
                  pl.BlockSpec((1, TQ, ATTN_WIDTH), lambda bi, n: (b0 + bi, n, 0)),
                  kv_main, kv_prev, kv_next, kv_main, kv_prev, kv_next,
                  pl.BlockSpec((1, c, 4 * LANES), lambda bi, n: (b0 + bi, 0, 0)),
                  pl.BlockSpec((1, c, 4 * LANES), lambda bi, n: (b0 + bi, 0, 0)),
                  pl.BlockSpec((1, TQ, POOL_WIDTH), lambda bi, n: (b0 + bi, n, 0)),
                  pl.BlockSpec((1, 8, POOL_WIDTH), lambda bi, n: (b0 + bi, jnp.maximum(n * pb - 1, 0), 0)),
                  pl.BlockSpec((1, 8, POOL_WIDTH),
                               lambda bi, n: (b0 + bi, jnp.minimum((n + 1) * pb, s // 8 - 1), 0)),
                  pl.BlockSpec((1, TQ, d), lambda bi, n: (b0 + bi, n, 0)),
                  pl.BlockSpec((1, 6, d), lambda bi, n: (b0 + bi, 0, 0)),
                  pl.BlockSpec((len(POOL_WINDOWS), QB, POOL_SLAB), const3),
                  pl.BlockSpec((len(POOL_WINDOWS), POOL_GROUP_DIM, POOL_GROUP_DIM), const3),
                  pl.BlockSpec((1, POOL_WIDTH), const2),
                  pl.BlockSpec((d, d), const2),
                  pl.BlockSpec((1, d), const2),
                  pl.BlockSpec((N_EXPERTS, d), const2),
                  pl.BlockSpec((N_EXPERTS, d), const2)],
        out_specs=(pl.BlockSpec((1, TQ, d), lambda bi, n: (bi, n, 0)),
                   pl.BlockSpec((TQ, d // 2), lambda bi, n: (bi * nt + n, 0)),
                   pl.BlockSpec((N_EXPERTS, TQ), lambda bi, n: (0, bi * nt + n))),
        scratch_shapes=[pltpu.VMEM((TQ + 2 * QB, 4 * LANES), BF16),
                        pltpu.VMEM((TQ + 2 * QB, 4 * LANES), BF16),
                        pltpu.VMEM((TQ + 2 * QB, POOL_WIDTH), F32),
                        pltpu.VMEM((TQ, d), BF16),
                        pltpu.VMEM((N_HEADS // 2, QB, 2 * (3 * QB + c)), F32),
                        pltpu.VMEM((N_HEADS // 2, QB, 2 * (3 * QB + c)), BF16),
                        pltpu.VMEM((N_HEADS, QB, LANES), F32)],
        compiler_params=pltpu.CompilerParams(vmem_limit_bytes=VMEM_LIMIT),
        name="attn",
    )(sink, q, k4, k4, k4, v4, v4, v4, kc4, vc4, p, p, p, x, mod, band, poolw_bf, pscale, wout_bf,
      g2, wr_hi, wr_lo)


def _first_argmax_rows(v, row_iota, n_rows):
    m = jnp.max(v, axis=0, keepdims=True)
    idx = jnp.min(jnp.where(v == m, row_iota, n_rows), axis=0, keepdims=True)
    return m, idx


def _route_kernel(lg_ref, bias_ref, tri_ref, idx_ref, gate_ref, rank_ref, cnt_ref, carry):
    i = pl.program_id(0)

    @pl.when(i == 0)
    def _():
        carry[...] = jnp.zeros_like(carry)

    scores = 1.0 / (1.0 + jnp.exp(-lg_ref[...]))
    biased = scores + bias_ref[...]
    e_iota = lax.broadcasted_iota(jnp.int32, scores.shape, 0).astype(F32)
    g_iota = lax.broadcasted_iota(jnp.int32, (EXPERTS_PER_GROUP, TR), 0).astype(F32)
    neg = -jnp.inf

    grp = []
    for g in range(N_EXPERT_GROUPS):
        blk = biased[g * EXPERTS_PER_GROUP:(g + 1) * EXPERTS_PER_GROUP, :]
        m1, i1 = _first_argmax_rows(blk, g_iota, float(EXPERTS_PER_GROUP))
        m2 = jnp.max(jnp.where(g_iota == i1, neg, blk), axis=0, keepdims=True)
        grp.append(m1 + m2)
    grp = jnp.concatenate(grp, axis=0)
    gg_iota = lax.broadcasted_iota(jnp.int32, grp.shape, 0).astype(F32)
    grp_sel = jnp.zeros(grp.shape, F32)
    for _ in range(TOPK_GROUPS):
        _, gi = _first_argmax_rows(grp, gg_iota, float(N_EXPERT_GROUPS))
        hit = gg_iota == gi
        grp_sel = jnp.where(hit, 1.0, grp_sel)
        grp = jnp.where(hit, neg, grp)
    allowed = jnp.concatenate(
        [jnp.broadcast_to(grp_sel[g:g + 1, :], (EXPERTS_PER_GROUP, TR)) for g in range(N_EXPERT_GROUPS)], axis=0)
    masked = jnp.where(allowed > 0.5, biased, neg)

    idxs, gates = [], []
    onehot = jnp.zeros(scores.shape, F32)
    for _ in range(TOP_K):
        _, ei = _first_argmax_rows(masked, e_iota, float(N_EXPERTS))
        hit = e_iota == ei
        idxs.append(ei)
        gates.append(jnp.sum(jnp.where(hit, scores, 0.0), axis=0, keepdims=True))
        onehot = jnp.where(hit, 1.0, onehot)
        masked = jnp.where(hit, neg, masked)
    idx = jnp.concatenate(idxs, axis=0)
    gate = jnp.concatenate(gates, axis=0)
    gate = gate / jnp.sum(gate, axis=0, keepdims=True) * ROUTED_SCALE

    before = _dot(onehot.astype(BF16), tri_ref[...]) + carry[:, 0:1]
    ranks = [jnp.sum(jnp.where(e_iota == idxs[k], before, 0.0), axis=0, keepdims=True) for k in range(TOP_K)]
    idx_ref[...] = idx.astype(jnp.int32)
    gate_ref[...] = gate
    rank_ref[...] = jnp.concatenate(ranks, axis=0).astype(jnp.int32)
    total = carry[...] + jnp.sum(onehot, axis=1, keepdims=True)
    carry[...] = total
    cnt_ref[...] = total


def _route(lg_t, bias, tri):
    e, t = lg_t.shape
    tok = pl.BlockSpec((TOP_K, TR), lambda i: (0, i))
    return pl.pallas_call(
        _route_kernel,
        out_shape=(jax.ShapeDtypeStruct((TOP_K, t), jnp.int32),
                   jax.ShapeDtypeStruct((TOP_K, t), F32),
                   jax.ShapeDtypeStruct((TOP_K, t), jnp.int32),
                   jax.ShapeDtypeStruct((e, LANES), F32)),
        grid=(t // TR,),
        in_specs=[pl.BlockSpec((e, TR), lambda i: (0, i)),
                  pl.BlockSpec((e, 1), lambda i: (0, 0)),
                  pl.BlockSpec((TR, TR), lambda i: (0, 0))],
        out_specs=(tok, tok, tok, pl.BlockSpec((e, LANES), lambda i: (0, 0))),
        scratch_shapes=[pltpu.VMEM((e, LANES), F32)],
        compiler_params=pltpu.CompilerParams(vmem_limit_bytes=VMEM_LIMIT),
        name="route",
    )(lg_t, bias, tri)


def _plan_kernel(n_blocks, size_ref, start_ref, expert_ref, valid_ref, nact_ref):
    def per_expert(e, first_block):
        size = size_ref[e]
        n_blk = (size + TM_EXP - 1) // TM_EXP
        start_ref[e] = first_block * TM_EXP

        def per_block(j, carry):
            expert_ref[first_block + j] = e
            valid_ref[first_block + j] = jnp.minimum(size - j * TM_EXP, TM_EXP)
            return carry

        lax.fori_loop(0, n_blk, per_block, 0)
        return first_block + n_blk

    n_active = lax.fori_loop(0, N_EXPERTS, per_expert, 0)
    nact_ref[0] = n_active

    def unused(i, carry):
        expert_ref[i] = N_EXPERTS - 1
        valid_ref[i] = 0
        return carry

    lax.fori_loop(n_active, n_blocks, unused, 0)


def _plan_blocks(sizes, n_blocks):
    smem = pl.BlockSpec(memory_space=pltpu.SMEM)
    return pl.pallas_call(
        functools.partial(_plan_kernel, n_blocks),
        out_shape=(jax.ShapeDtypeStruct((N_EXPERTS,), jnp.int32),
                   jax.ShapeDtypeStruct((n_blocks,), jnp.int32),
                   jax.ShapeDtypeStruct((n_blocks,), jnp.int32),
                   jax.ShapeDtypeStruct((1,), jnp.int32)),
        in_specs=[smem],
        out_specs=(smem, smem, smem, smem),
        name="plan_blocks",
    )(sizes)


def _dest_kernel(start_ref, idx_ref, rank_ref, dest_ref):
    idx = idx_ref[...]
    dest = rank_ref[...]
    for e in range(N_EXPERTS):
        dest = dest + jnp.where(idx == e, start_ref[e], 0)
    dest_ref[...] = dest


def _dest_rows(pad_start, idx_kt, rank_kt):
    n_k, t = idx_kt.shape
    tile = 4096
    blk = pl.BlockSpec((n_k, tile), lambda i: (0, i))
    return pl.pallas_call(
        _dest_kernel,
        out_shape=jax.ShapeDtypeStruct((n_k, t), jnp.int32),
        grid=(t // tile,),
        in_specs=[pl.BlockSpec(memory_space=pltpu.SMEM), blk, blk],
        out_specs=blk,
        name="dest_rows",
    )(pad_start, idx_kt, rank_kt)


def _sc_mesh():
    return plsc.VectorSubcoreMesh(core_axis_name="c", subcore_axis_name="s")


def _sc_token_base(steps, j):
    worker = lax.axis_index("s") * SC_CORES + lax.axis_index("c")
    return (worker * steps + j) * SC_TOKENS


def _sc_scatter(dest_kt, h2p, n_rows):
    t, width = h2p.shape
    steps = t // (SC_WORKERS * SC_TOKENS)

    @functools.partial(
        pl.kernel, mesh=_sc_mesh(),
        out_type=jax.ShapeDtypeStruct((n_rows, width), U32),
        scratch_types=[pltpu.VMEM((TOP_K, SC_TOKENS), jnp.int32),
                       pltpu.VMEM((SC_TOKENS, width), U32),
                       pltpu.SemaphoreType.DMA],
        name="sc_scatter",
    )
    def body(dest_hbm, h_hbm, xs_hbm, idx_v, rows_v, sem):
        @pl.loop(0, steps)
        def _(j):
            base = _sc_token_base(steps, j)
            pltpu.sync_copy(dest_hbm.at[:, pl.ds(base, SC_TOKENS)], idx_v)
            pltpu.sync_copy(h_hbm.at[pl.ds(base, SC_TOKENS)], rows_v)
            copies = [pltpu.async_copy(rows_v, xs_hbm.at[idx_v.at[k]], sem) for k in range(TOP_K)]
            for cp in copies:
                cp.wait()

    return body(dest_kt, h2p)


def _sc_gather(dest_kt, ys, token0, n_tokens):
    n_k = dest_kt.shape[0]
    width = ys.shape[1]
    steps = n_tokens // (SC_WORKERS * SC_TOKENS)

    half = SC_TOKENS // 2
    units = [(k, h) for k in range(n_k) for h in range(2)]
    n_buf = 3

    @functools.partial(
        pl.kernel, mesh=_sc_mesh(),
        out_type=jax.ShapeDtypeStruct((n_k, n_tokens, width), U32),
        scratch_types=[pltpu.VMEM((n_k, SC_TOKENS), jnp.int32),
                       pltpu.VMEM((n_buf, half, width), U32),
                       pltpu.SemaphoreType.DMA((n_buf,))],
        name="sc_gather",
    )
    def body(dest_hbm, ys_hbm, yk_hbm, idx_v, rows_v, sems):
        @pl.loop(0, steps)
        def _(j):
            base = _sc_token_base(steps, j)
            pltpu.sync_copy(dest_hbm.at[:, pl.ds(token0 + base, SC_TOKENS)], idx_v)

            def gather(u):
                k, h = units[u]
                slot = u % n_buf
                return pltpu.make_async_copy(ys_hbm.at[idx_v.at[k, pl.ds(h * half, half)]], rows_v.at[slot],
                                             sems.at[slot])

            gather(0).start()
            gather(1).start()
            for u, (k, h) in enumerate(units):
                gather(u).wait()
                pltpu.sync_copy(rows_v.at[u % n_buf], yk_hbm.at[k, pl.ds(base + h * half, half)])
                if u + 2 < len(units):
                    gather(u + 2).start()

    return body(dest_kt, ys)


def _experts_kernel(be_ref, valid_ref, nact_ref, xs_ref, wg_ref, wu_ref, wd_ref, ys_ref, wg_bf, wu_bf, wd_bf):
    i = pl.program_id(0)
    prev = be_ref[jnp.maximum(i - 1, 0)]

    @pl.when((i == 0) | (be_ref[i] != prev))
    def _():
        wg_bf[...] = wg_ref[0].astype(BF16)
        wu_bf[...] = wu_ref[0].astype(BF16)
        wd_bf[...] = wd_ref[0].astype(BF16)

    valid = jnp.where(i < nact_ref[0], valid_ref[i], 0)

    def run(rows):
        row = lax.broadcasted_iota(jnp.int32, (rows, xs_ref.shape[1]), 0)
        words = jnp.where(row < valid, xs_ref[0:rows, :], jnp.uint32(0))
        xb = _unpack_words(words).astype(BF16)
        hid = _silu(_dot(xb, wg_bf[...])) * _dot(xb, wu_bf[...])
        ys_ref[0:rows, :] = _pack_words(_dot(hid.astype(BF16), wd_bf[...]))
        if rows < TM_EXP:
            ys_ref[rows:, :] = jnp.zeros((TM_EXP - rows, ys_ref.shape[1]), U32)

    rows, lower = TM_EXP, TM_EXP // 2
    while rows >= EXP_MIN_ROWS:
        lo = lower if rows > EXP_MIN_ROWS else 0
        pl.when((valid > lo) & (valid <= rows))(functools.partial(run, rows))
        rows, lower = rows // 2, lower // 2

    @pl.when(valid == 0)
    def _():
        ys_ref[...] = jnp.zeros_like(ys_ref)


def _experts(block_e, block_valid, n_active, xs, w_gate, w_up, w_down):
    n_rows, half = xs.shape
    d = w_gate.shape[1]
    nb = n_rows // TM_EXP
    grid_spec = pltpu.PrefetchScalarGridSpec(
        num_scalar_prefetch=3,
        grid=(nb,),
        in_specs=[pl.BlockSpec((TM_EXP, half), lambda i, be, bv, na: (jnp.minimum(i, na[0] - 1), 0)),
                  pl.BlockSpec((1, d, D_EXPERT), lambda i, be, bv, na: (be[i], 0, 0)),
                  pl.BlockSpec((1, d, D_EXPERT), lambda i, be, bv, na: (be[i], 0, 0)),
                  pl.BlockSpec((1, D_EXPERT, d), lambda i, be, bv, na: (be[i], 0, 0))],
        out_specs=pl.BlockSpec((TM_EXP, half), lambda i, be, bv, na: (i, 0)),
        scratch_shapes=[pltpu.VMEM((d, D_EXPERT), BF16), pltpu.VMEM((d, D_EXPERT), BF16),
                        pltpu.VMEM((D_EXPERT, d), BF16)],
    )
    return pl.pallas_call(
        _experts_kernel,
        out_shape=jax.ShapeDtypeStruct((n_rows, half), U32),
        grid_spec=grid_spec,
        compiler_params=pltpu.CompilerParams(vmem_limit_bytes=VMEM_LIMIT),
        name="experts",
    )(block_e, block_valid, n_active, xs, w_gate, w_up, w_down)


def _combine_kernel(yk_ref, x1_ref, h2_ref, gate_ref, mod_ref, wsg_ref, wsu_ref, wsd_ref, fg_ref, *out_refs):
    out_ref = out_refs[-1]
    hb = _unpack_words(h2_ref[...]).astype(BF16)
    hid = _silu(_dot(hb, wsg_ref[...])) * _dot(hb, wsu_ref[...])
    ffn = _dot(hid.astype(BF16), wsd_ref[...])
    gate = gate_ref[...]
    for k in range(TOP_K):
        ffn = ffn + gate[:, k:k + 1] * _unpack_words(yk_ref[k])
    x2 = x1_ref[...] + mod_ref[0, 5:6, :] * ffn
    ms = jnp.mean(x2 * x2, axis=-1, keepdims=True)
    out_ref[...] = x2 * lax.rsqrt(ms + EPS) * fg_ref[...]


def _combine(yk, token0, out_token0, n_out_tokens, prev_out, x1, h2p, gate_tk, mod, wsg_bf, wsu_bf, wsd_bf, final_g,
             seq_len):
    t, d = n_out_tokens, x1.shape[1]
    tiles_per_seq = seq_len // TF
    tile0 = token0 // TF
    out_tile0 = out_token0 // TF
    in_tok = pl.BlockSpec((TF, d), lambda i: (tile0 + i, 0))
    tok = pl.BlockSpec((TF, d), lambda i: (out_tile0 + i, 0))
    in_specs = [pl.BlockSpec((TOP_K, TF, d // 2), lambda i: (0, i, 0)),
                in_tok, pl.BlockSpec((TF, d // 2), lambda i: (tile0 + i, 0)),
                pl.BlockSpec((TF, TOP_K), lambda i: (tile0 + i, 0)),
                pl.BlockSpec((1, 6, d), lambda i: ((out_tile0 + i) // tiles_per_seq, 0, 0)),
                pl.BlockSpec((d, D_SHARED), lambda i: (0, 0)),
                pl.BlockSpec((d, D_SHARED), lambda i: (0, 0)),
                pl.BlockSpec((D_SHARED, d), lambda i: (0, 0)),
                pl.BlockSpec((1, d), lambda i: (0, 0))]
    args = [yk, x1, h2p, gate_tk, mod, wsg_bf, wsu_bf, wsd_bf, final_g]
    aliases = {}
    if prev_out is not None:
        in_specs.append(pl.BlockSpec(memory_space=pl.ANY))
        args.append(prev_out)
        aliases = {len(args) - 1: 0}
    return pl.pallas_call(
        _combine_kernel,
        out_shape=jax.ShapeDtypeStruct((t, d), F32),
        grid=(yk.shape[1] // TF,),
        in_specs=in_specs,
        out_specs=tok,
        input_output_aliases=aliases,
        compiler_params=pltpu.CompilerParams(vmem_limit_bytes=VMEM_LIMIT),
        name="combine",
    )(*args)


def _rope_tables(n_tokens):
    n_rows = n_tokens // GRID_W
    n_freq = HEAD_DIM // 4
    inv_freq = ROPE_THETA ** (-jnp.arange(n_freq, dtype=F32) / n_freq)
    ang_r = jnp.arange(n_rows).astype(F32)[:, None] * inv_freq[None, :]
    ang_c = jnp.arange(GRID_W).astype(F32)[:, None] * inv_freq[None, :]

    def per_token(row_part, col_part):
        rows = jnp.broadcast_to(row_part[:, None, :], (n_rows, GRID_W, n_freq))
        cols = jnp.broadcast_to(col_part[None, :, :], (n_rows, GRID_W, n_freq))
        return rows.reshape(n_tokens, n_freq), cols.reshape(n_tokens, n_freq)

    cos_r, cos_c = per_token(jnp.cos(ang_r), jnp.cos(ang_c))
    sin_r, sin_c = per_token(jnp.sin(ang_r), jnp.sin(ang_c))
    cos = jnp.concatenate([cos_r, cos_r, cos_c, cos_c], axis=1)
    sin = jnp.concatenate([-sin_r, sin_r, -sin_c, sin_c], axis=1)
    reps = LANES // HEAD_DIM
    return jnp.tile(cos, (1, reps)), jnp.tile(sin, (1, reps))


def _pool_bands():
    i = jnp.arange(QB)[:, None]
    r = jnp.arange(POOL_SLAB)[None, :]
    return jnp.stack([((r >= i + POOL_OFF - w // 2) & (r < i + POOL_OFF + w // 2)).astype(BF16)
                      for w in POOL_WINDOWS])


def kernel(x, c, ctx, c_ctx, w_ada, b_ada, norm1_g, norm2_g, w_in, attn_sink, pool_w, pool_scale, w_out,
           w_router, router_bias, w_gate, w_up, w_down, ws_gate, ws_up, ws_down, final_g):
    b, s, d = x.shape
    t = b * s
    assert w_ada.shape[0] == 1 and d == D_MODEL and s % TQ == 0 and b + 1 <= 8

    c8 = jnp.zeros((8, d), F32).at[:b].set(c).at[b].set(c_ctx)
    mod = _ada(c8, w_ada[0], b_ada[0]).reshape(8, 6, d)
    g1 = norm1_g[0].reshape(1, d)
    g2 = norm2_g[0].reshape(1, d)
    w_in_bf = w_in[0].astype(BF16)
    cos_t, sin_t = _rope_tables(s)

    q, k4, v4, p = _inproj(x, mod, g1, w_in_bf, cos_t, sin_t)
    kc4, vc4 = _ctxproj(ctx, mod[b:b + 1], g1, w_in_bf[:, ATTN_WIDTH:ATTN_WIDTH + 2 * KV_WIDTH])

    wr_t = w_router[0].T
    wr_hi = wr_t.astype(BF16)
    wr_lo = (wr_t - wr_hi.astype(F32)).astype(BF16)
    attn_consts = (_pool_bands(), pool_w[0].astype(BF16), pool_scale[0].reshape(1, POOL_WIDTH),
                   w_out[0].astype(BF16), g2, wr_hi, wr_lo)
    tri = jnp.triu(jnp.ones((TR, TR), BF16), k=1)
    shared_w = (ws_gate[0].astype(BF16), ws_up[0].astype(BF16), ws_down[0].astype(BF16))

    nb = b // TOKEN_GROUPS
    tg = nb * s
    assert b % TOKEN_GROUPS == 0 and tg % (COMBINE_CHUNKS * SC_WORKERS * SC_TOKENS) == 0
    n_rows = -(-(tg * TOP_K + N_EXPERTS * (TM_EXP - 1)) // TM_EXP) * TM_EXP
    groups = []
    for g in range(TOKEN_GROUPS):
        x1, h2p, lg_t = _attn(g * nb, nb, attn_sink[0], q, k4, v4, kc4, vc4, p, x, mod, *attn_consts)
        idx_kt, gate_kt, rank_kt, counts = _route(lg_t, router_bias[0].reshape(N_EXPERTS, 1), tri)
        pad_start, block_e, block_valid, n_active = _plan_blocks(counts[:, 0].astype(jnp.int32), n_rows // TM_EXP)
        dest_kt = _dest_rows(pad_start, idx_kt, rank_kt)
        xs = _sc_scatter(dest_kt, h2p, n_rows)
        groups.append((x1.reshape(tg, d), h2p, gate_kt.T, dest_kt, xs, block_e, block_valid, n_active))

    out = None
    chunk = tg // COMBINE_CHUNKS
    for g, (x1, h2p, gate_tk, dest_kt, xs, block_e, block_valid, n_active) in enumerate(groups):
        ys = _experts(block_e, block_valid, n_active, xs, w_gate[0], w_up[0], w_down[0])
        for token0 in range(0, tg, chunk):
            out = _combine(_sc_gather(dest_kt, ys, token0, chunk), token0, g * tg + token0, t, out, x1, h2p,
                           gate_tk, mod, *shared_w, final_g.reshape(1, d), s)
    return out.reshape(b, s, d)
```

```python
import functools

import jax
import jax.numpy as jnp
from jax import lax
from jax.experimental import pallas as pl
from jax.experimental.pallas import tpu as pltpu
from jax.experimental.pallas import tpu_sc as plsc

F32 = jnp.float32
BF16 = jnp.bfloat16

D_MODEL = 1024
GRID_W = 64
N_HEADS = 8
N_KV_HEADS = 2
HEAD_DIM = 64
ATTN_WIDTH = N_HEADS * HEAD_DIM
KV_WIDTH = N_KV_HEADS * HEAD_DIM
WINDOW = 128
ROPE_THETA = 10000.0
POOL_WINDOWS = (2, 4, 8, 16)
POOL_WIDTH = D_MODEL - ATTN_WIDTH
POOL_GROUP_DIM = POOL_WIDTH // len(POOL_WINDOWS)
IN_COLS = ATTN_WIDTH + 2 * KV_WIDTH + POOL_WIDTH
N_EXPERTS = 64
TOP_K = 8
N_EXPERT_GROUPS = 8
EXPERTS_PER_GROUP = N_EXPERTS // N_EXPERT_GROUPS
TOPK_GROUPS = 4
D_EXPERT = 256
D_SHARED = 256
ROUTED_SCALE = 2.5
EPS = 1e-6
LOG2E = 1.4426950408889634

LANES = 128
U32 = jnp.uint32
VMEM_LIMIT = 48 * 1024 * 1024

TM_PROJ = 1024
TQ = 512
QB = 128
POOL_SLAB = 256
POOL_OFF = 64
TR = 512
TM_EXP = 1024
EXP_MIN_ROWS = 256
TF = 256
TOKEN_GROUPS = 1
COMBINE_CHUNKS = 4
SC_CORES = 2
SC_WORKERS = 32
SC_TOKENS = 128


def _silu(x):
    return x * (1.0 / (1.0 + jnp.exp(-x)))


def _split_bf16(x):
    hi = x.astype(BF16)
    lo = (x - hi.astype(F32)).astype(BF16)
    return hi, lo


def _dot(a, b):
    return jnp.dot(a, b, preferred_element_type=F32)


def _pack_words(val):
    half = val.shape[1] // 2
    lo = lax.bitcast_convert_type(val[:, :half].astype(BF16).astype(F32), U32)
    hi = lax.bitcast_convert_type(val[:, half:].astype(BF16).astype(F32), U32)
    return lax.shift_right_logical(lo, jnp.uint32(16)) | hi


def _unpack_words(words):
    lo = lax.bitcast_convert_type(lax.shift_left(words, jnp.uint32(16)), F32)
    hi = lax.bitcast_convert_type(words & jnp.uint32(0xFFFF0000), F32)
    return jnp.concatenate([lo, hi], axis=1)


def _dot_nt(a, b):
    return lax.dot_general(a, b, (((1,), (1,)), ((), ())), preferred_element_type=F32)


def _ada_kernel(c_ref, w_ref, b_ref, o_ref):
    a_hi, a_lo = _split_bf16(_silu(c_ref[...]))
    w_hi, w_lo = _split_bf16(w_ref[...])
    o_ref[...] = _dot(a_hi, w_hi) + _dot(a_lo, w_hi) + _dot(a_hi, w_lo) + b_ref[...]


def _ada(c8, w_ada, b_ada):
    d = c8.shape[1]
    n = w_ada.shape[1]
    tn = 512
    return pl.pallas_call(
        _ada_kernel,
        out_shape=jax.ShapeDtypeStruct((8, n), F32),
        grid=(n // tn,),
        in_specs=[pl.BlockSpec((8, d), lambda j: (0, 0)),
                  pl.BlockSpec((d, tn), lambda j: (0, j)),
                  pl.BlockSpec((1, tn), lambda j: (0, j))],
        out_specs=pl.BlockSpec((8, tn), lambda j: (0, j)),
        compiler_params=pltpu.CompilerParams(vmem_limit_bytes=VMEM_LIMIT),
        name="ada",
    )(c8, w_ada, b_ada.reshape(1, n))


def _norm_mod(x, g, shift, scale):
    ms = jnp.mean(x * x, axis=-1, keepdims=True)
    return (x * lax.rsqrt(ms + EPS) * g) * (1.0 + scale) + shift


def _lane_variants(t):
    lane = lax.broadcasted_iota(jnp.int32, t.shape, 1)
    lo = lane < HEAD_DIM
    tr = pltpu.roll(t, HEAD_DIM, 1)
    zero = jnp.zeros_like(t)
    return (jnp.where(lo, t, zero), jnp.where(lo, zero, tr),
            jnp.where(lo, tr, zero), jnp.where(lo, zero, t))


def _store_variants(ref, t):
    for i, var in enumerate(_lane_variants(t)):
        ref[0, :, i * LANES:(i + 1) * LANES] = var.astype(BF16)


def _inproj_kernel(x_ref, mod_ref, g_ref, w_ref, cos_ref, sin_ref, q_ref, k_ref, v_ref, p_ref):
    h = _norm_mod(x_ref[0], g_ref[...], mod_ref[0, 0:1, :], mod_ref[0, 1:2, :])
    z = _dot(h.astype(BF16), w_ref[...])
    cos = cos_ref[...]
    sin = sin_ref[...]
    lane = lax.broadcasted_iota(jnp.int32, cos.shape, 1)
    first_half = (lane & 16) == 0

    def rope(zc):
        partner = jnp.where(first_half, pltpu.roll(zc, LANES - 16, 1), pltpu.roll(zc, 16, 1))
        return zc * cos + partner * sin

    scale = HEAD_DIM ** -0.5 * LOG2E
    for c in range(ATTN_WIDTH // LANES):
        q_ref[0, :, c * LANES:(c + 1) * LANES] = (rope(z[:, c * LANES:(c + 1) * LANES]) * scale).astype(BF16)
    _store_variants(k_ref, rope(z[:, ATTN_WIDTH:ATTN_WIDTH + KV_WIDTH]))
    _store_variants(v_ref, z[:, ATTN_WIDTH + KV_WIDTH:ATTN_WIDTH + 2 * KV_WIDTH])
    p_ref[0] = z[:, ATTN_WIDTH + 2 * KV_WIDTH:]


def _inproj(x, mod, g1, w_in_bf, cos_t, sin_t):
    b, s, d = x.shape
    tm = TM_PROJ
    return pl.pallas_call(
        _inproj_kernel,
        out_shape=(jax.ShapeDtypeStruct((b, s, ATTN_WIDTH), BF16),
                   jax.ShapeDtypeStruct((b, s, 4 * LANES), BF16),
                   jax.ShapeDtypeStruct((b, s, 4 * LANES), BF16),
                   jax.ShapeDtypeStruct((b, s, POOL_WIDTH), F32)),
        grid=(s // tm, b),
        in_specs=[pl.BlockSpec((1, tm, d), lambda n, bi: (bi, n, 0)),
                  pl.BlockSpec((1, 6, d), lambda n, bi: (bi, 0, 0)),
                  pl.BlockSpec((1, d), lambda n, bi: (0, 0)),
                  pl.BlockSpec((d, IN_COLS), lambda n, bi: (0, 0)),
                  pl.BlockSpec((tm, LANES), lambda n, bi: (n, 0)),
                  pl.BlockSpec((tm, LANES), lambda n, bi: (n, 0))],
        out_specs=(pl.BlockSpec((1, tm, ATTN_WIDTH), lambda n, bi: (bi, n, 0)),
                   pl.BlockSpec((1, tm, 4 * LANES), lambda n, bi: (bi, n, 0)),
                   pl.BlockSpec((1, tm, 4 * LANES), lambda n, bi: (bi, n, 0)),
                   pl.BlockSpec((1, tm, POOL_WIDTH), lambda n, bi: (bi, n, 0))),
        compiler_params=pltpu.CompilerParams(vmem_limit_bytes=VMEM_LIMIT),
        name="inproj",
    )(x, mod, g1, w_in_bf, cos_t, sin_t)


def _ctxproj_kernel(x_ref, mod_ref, g_ref, w_ref, k_ref, v_ref):
    h = _norm_mod(x_ref[0], g_ref[...], mod_ref[0, 0:1, :], mod_ref[0, 1:2, :])
    z = _dot(h.astype(BF16), w_ref[...])
    _store_variants(k_ref, z[:, :KV_WIDTH])
    _store_variants(v_ref, z[:, KV_WIDTH:])


def _ctxproj(ctx, mod_c, g1, w_kv_bf):
    b, c, d = ctx.shape
    return pl.pallas_call(
        _ctxproj_kernel,
        out_shape=(jax.ShapeDtypeStruct((b, c, 4 * LANES), BF16),
                   jax.ShapeDtypeStruct((b, c, 4 * LANES), BF16)),
        grid=(b,),
        in_specs=[pl.BlockSpec((1, c, d), lambda bi: (bi, 0, 0)),
                  pl.BlockSpec((1, 6, d), lambda bi: (0, 0, 0)),
                  pl.BlockSpec((1, d), lambda bi: (0, 0)),
                  pl.BlockSpec((d, 2 * KV_WIDTH), lambda bi: (0, 0))],
        out_specs=(pl.BlockSpec((1, c, 4 * LANES), lambda bi: (bi, 0, 0)),
                   pl.BlockSpec((1, c, 4 * LANES), lambda bi: (bi, 0, 0))),
        compiler_params=pltpu.CompilerParams(vmem_limit_bytes=VMEM_LIMIT),
        name="ctxproj",
    )(ctx, mod_c, g1, w_kv_bf)


def _fold(op, tiles):
    while len(tiles) > 1:
        tiles = [op(tiles[i], tiles[i + 1]) if i + 1 < len(tiles) else tiles[i] for i in range(0, len(tiles), 2)]
    return tiles[0]


def _stack_variants(t4, kv):
    return jnp.concatenate([t4[:, (2 * kv) * LANES:(2 * kv + 1) * LANES],
                            t4[:, (2 * kv + 1) * LANES:(2 * kv + 2) * LANES]], axis=0)


def _attn_kernel(seq_len, sink_ref, q_ref, k_ref, kp_ref, kn_ref, v_ref, vp_ref, vn_ref, kc_ref, vc_ref,
                 p_ref, pp_ref, pn_ref, x_ref, mod_ref, band_ref, poolw_ref, pscale_ref, wout_ref,
                 g2_ref, wrh_ref, wrl_ref, x1_ref, h2_ref, lg_ref, kwin, vwin, pext, mix, s_scr, p_scr, m_scr,
                 sctx_scr, pctx_scr, o_scr, r_scr):
    n = pl.program_id(1)
    n_last = pl.num_programs(1) - 1

    kwin[0:QB, :] = kp_ref[0]
    kwin[QB:QB + TQ, :] = k_ref[0]
    kwin[QB + TQ:, :] = kn_ref[0]
    vwin[0:QB, :] = vp_ref[0]
    vwin[QB:QB + TQ, :] = v_ref[0]
    vwin[QB + TQ:, :] = vn_ref[0]

    pext[0:QB - 8, :] = jnp.zeros((QB - 8, POOL_WIDTH), F32)
    pext[QB - 8:QB, :] = jnp.where(n > 0, pp_ref[0], 0.0)
    pext[QB:QB + TQ, :] = p_ref[0]
    pext[QB + TQ:QB + TQ + 8, :] = jnp.where(n < n_last, pn_ref[0], 0.0)
    pext[QB + TQ + 8:, :] = jnp.zeros((QB - 8, POOL_WIDTH), F32)

    row = lax.broadcasted_iota(jnp.int32, (QB, 3 * QB), 0)
    col = lax.broadcasted_iota(jnp.int32, (QB, 3 * QB), 1)
    in_band = (col >= row) & (col <= row + 2 * WINDOW)
    tok = lax.broadcasted_iota(jnp.int32, (QB, 1), 0)
    kc = kc_ref[0]
    kc_rows = [_stack_variants(kc, kv) for kv in range(N_KV_HEADS)]
    vc_rows = [_stack_variants(vc_ref[0], kv) for kv in range(N_KV_HEADS)]
    group = N_HEADS // N_KV_HEADS

    for c in range(N_HEADS // 2):
        sctx_scr[c] = _dot_nt(q_ref[0, :, c * LANES:(c + 1) * LANES], kc_rows[2 * c // group])

    def sub_block(j, carry):
        r0 = pl.multiple_of(j * QB, QB)
        qj = q_ref[0, pl.ds(r0, QB), :]
        kw = kwin[pl.ds(r0, 3 * QB), :]
        vw = vwin[pl.ds(r0, 3 * QB), :]
        kpos = col + (n * TQ + j * QB - QB)
        ok = in_band & (kpos >= 0) & (kpos < seq_len)
        bias = jnp.where(ok, 0.0, -jnp.inf)
        n_loc, n_ctx = 3 * QB, kc.shape[0]
        bias2 = jnp.concatenate([bias, bias], axis=1)
        k_rows = [_stack_variants(kw, kv) for kv in range(N_KV_HEADS)]
        v_rows = [_stack_variants(vw, kv) for kv in range(N_KV_HEADS)]

        rows = pl.ds(r0, QB)

        def score_tiles(head):
            c, par = head // 2, head % 2
            local = [(s_scr, p_scr, (c, slice(None), slice(par * n_loc + i * LANES, par * n_loc + (i + 1) * LANES)))
                     for i in range(n_loc // LANES)]
            context = [(sctx_scr, pctx_scr, (c, rows, slice(par * n_ctx + i * LANES, par * n_ctx + (i + 1) * LANES)))
                       for i in range(n_ctx // LANES)]
            return local + context

        for c in range(N_HEADS // 2):
            s_scr[c] = _dot_nt(qj[:, c * LANES:(c + 1) * LANES], k_rows[2 * c // group]) + bias2
        for head in range(N_HEADS):
            tiles = [s_ref[idx] for s_ref, _, idx in score_tiles(head)]
            row_max = jnp.max(_fold(jnp.maximum, tiles), axis=1, keepdims=True)
            m_scr[head] = jnp.broadcast_to(jnp.maximum(row_max, sink_ref[head] * LOG2E), (QB, LANES))
        for head in range(N_HEADS):
            m = m_scr[head]
            acc = None
            for s_ref, p_out, idx in score_tiles(head):
                p = jnp.exp2(s_ref[idx] - m)
                p_out[idx] = p.astype(BF16)
                acc = p if acc is None else acc + p
            denom = (jnp.broadcast_to(jnp.sum(acc, axis=1, keepdims=True), (QB, LANES))
                     + jnp.exp2(sink_ref[head] * LOG2E - m))
            m_scr[head] = 1.0 / denom
        lane = lax.broadcasted_iota(jnp.int32, (QB, LANES), 1)
        for c in range(N_HEADS // 2):
            o_scr[c, rows, :] = _dot(p_scr[c], v_rows[2 * c // group])
            r_scr[c, rows, :] = jnp.where(lane < HEAD_DIM, m_scr[2 * c], m_scr[2 * c + 1])

        slab = pext[pl.ds(pl.multiple_of(r0 + POOL_OFF, 8), POOL_SLAB), :]
        tpos = tok + (n * TQ + j * QB)
        for g, w in enumerate(POOL_WINDOWS):
            sg = slab[:, g * LANES:(g + 1) * LANES]
            hi, lo = _split_bf16(sg)
            band = band_ref[g]
            wsum = _dot(band, hi) + _dot(band, lo)
            cnt = (jnp.minimum(tpos - w // 2 + w, seq_len) - jnp.maximum(tpos - w // 2, 0)).astype(F32)
            dlt = wsum / cnt - sg[POOL_OFF:POOL_OFF + QB, :]
            y = _dot(dlt.astype(BF16), poolw_ref[g]) * pscale_ref[:, g * LANES:(g + 1) * LANES]
            mix[pl.ds(r0, QB), ATTN_WIDTH + g * LANES:ATTN_WIDTH + (g + 1) * LANES] = y.astype(BF16)
        return carry

    lax.fori_loop(0, TQ // QB, sub_block, 0)

    for c in range(N_HEADS // 2):
        o = (o_scr[c] + _dot(pctx_scr[c], vc_rows[2 * c // group])) * r_scr[c]
        mix[:, c * LANES:(c + 1) * LANES] = o.astype(BF16)

    proj = _dot(mix[...], wout_ref[...])
    x1 = x_ref[0] + mod_ref[0, 2:3, :] * proj
    x1_ref[0] = x1
    h2 = _norm_mod(x1, g2_ref[...], mod_ref[0, 3:4, :], mod_ref[0, 4:5, :])
    h2_ref[...] = _pack_words(h2)
    h_hi, h_lo = _split_bf16(h2)
    wrh = wrh_ref[...]
    lg_ref[...] = _dot_nt(wrh, h_hi) + _dot_nt(wrh, h_lo) + _dot_nt(wrl_ref[...], h_hi)


def _attn(b0, b, sink, q, k4, v4, kc4, vc4, p, x, mod, band, poolw_bf, pscale, wout_bf, g2, wr_hi, wr_lo):
    _, s, d = x.shape
    c = kc4.shape[1]
    nt = s // TQ
    hb = TQ // QB
    pb = TQ // 8
    kv_main = pl.BlockSpec((1, TQ, 4 * LANES), lambda bi, n: (b0 + bi, n, 0))
    kv_prev = pl.BlockSpec((1, QB, 4 * LANES), lambda bi, n: (b0 + bi, jnp.maximum(n * hb - 1, 0), 0))
    kv_next = pl.BlockSpec((1, QB, 4 * LANES), lambda bi, n: (b0 + bi, jnp.minimum((n + 1) * hb, s // QB - 1), 0))
    const2 = lambda bi, n: (0, 0)
    const3 = lambda bi, n: (0, 0, 0)
    return pl.pallas_call(
        functools.partial(_attn_kernel, s),
        out_shape=(jax.ShapeDtypeStruct((b, s, d), F32),
                   jax.ShapeDtypeStruct((b * s, d // 2), U32),
                   jax.ShapeDtypeStruct((N_EXPERTS, b * s), F32)),
        grid=(b, nt),
        in_specs=[pl.BlockSpec(memory_space=pltpu.SMEM),
                  pl.BlockSpec((1, TQ, ATTN_WIDTH), lambda bi, n: (b0 + bi, n, 0)),
                  kv_main, kv_prev, kv_next, kv_main, kv_prev, kv_next,
                  pl.BlockSpec((1, c, 4 * LANES), lambda bi, n: (b0 + bi, 0, 0)),
                  pl.BlockSpec((1, c, 4 * LANES), lambda bi, n: (b0 + bi, 0, 0)),
                  pl.BlockSpec((1, TQ, POOL_WIDTH), lambda bi, n: (b0 + bi, n, 0)),
                  pl.BlockSpec((1, 8, POOL_WIDTH), lambda bi, n: (b0 + bi, jnp.maximum(n * pb - 1, 0), 0)),
                  pl.BlockSpec((1, 8, POOL_WIDTH),
                               lambda bi, n: (b0 + bi, jnp.minimum((n + 1) * pb, s // 8 - 1), 0)),
                  pl.BlockSpec((1, TQ, d), lambda bi, n: (b0 + bi, n, 0)),
                  pl.BlockSpec((1, 6, d), lambda bi, n: (b0 + bi, 0, 0)),
                  pl.BlockSpec((len(POOL_WINDOWS), QB, POOL_SLAB), const3),
                  pl.BlockSpec((len(POOL_WINDOWS), POOL_GROUP_DIM, POOL_GROUP_DIM), const3),
                  pl.BlockSpec((1, POOL_WIDTH), const2),
                  pl.BlockSpec((d, d), const2),
                  pl.BlockSpec((1, d), const2),
                  pl.BlockSpec((N_EXPERTS, d), const2),
                  pl.BlockSpec((N_EXPERTS, d), const2)],
        out_specs=(pl.BlockSpec((1, TQ, d), lambda bi, n: (bi, n, 0)),
                   pl.BlockSpec((TQ, d // 2), lambda bi, n: (bi * nt + n, 0)),
                   pl.BlockSpec((N_EXPERTS, TQ), lambda bi, n: (0, bi * nt + n))),
        scratch_shapes=[pltpu.VMEM((TQ + 2 * QB, 4 * LANES), BF16),
                        pltpu.VMEM((TQ + 2 * QB, 4 * LANES), BF16),
                        pltpu.VMEM((TQ + 2 * QB, POOL_WIDTH), F32),
                        pltpu.VMEM((TQ, d), BF16),
                        pltpu.VMEM((N_HEADS // 2, QB, 2 * 3 * QB), F32),
                        pltpu.VMEM((N_HEADS // 2, QB, 2 * 3 * QB), BF16),
                        pltpu.VMEM((N_HEADS, QB, LANES), F32),
                        pltpu.VMEM((N_HEADS // 2, TQ, 2 * c), F32),
                        pltpu.VMEM((N_HEADS // 2, TQ, 2 * c), BF16),
                        pltpu.VMEM((N_HEADS // 2, TQ, LANES), F32),
                        pltpu.VMEM((N_HEADS // 2, TQ, LANES), F32)],
        compiler_params=pltpu.CompilerParams(vmem_limit_bytes=VMEM_LIMIT),
        name="attn",
    )(sink, q, k4, k4, k4, v4, v4, v4, kc4, vc4, p, p, p, x, mod, band, poolw_bf, pscale, wout_bf,
      g2, wr_hi, wr_lo)


def _first_argmax_rows(v, row_iota, n_rows):
    m = jnp.max(v, axis=0, keepdims=True)
    idx = jnp.min(jnp.where(v == m, row_iota, n_rows), axis=0, keepdims=True)
    return m, idx


def _route_kernel(lg_ref, bias_ref, tri_ref, idx_ref, gate_ref, rank_ref, cnt_ref, carry):
    i = pl.program_id(0)

    @pl.when(i == 0)
    def _():
        carry[...] = jnp.zeros_like(carry)

    scores = 1.0 / (1.0 + jnp.exp(-lg_ref[...]))
    biased = scores + bias_ref[...]
    e_iota = lax.broadcasted_iota(jnp.int32, scores.shape, 0).astype(F32)
    g_iota = lax.broadcasted_iota(jnp.int32, (EXPERTS_PER_GROUP, TR), 0).astype(F32)
    neg = -jnp.inf

    grp = []
    for g in range(N_EXPERT_GROUPS):
        blk = biased[g * EXPERTS_PER_GROUP:(g + 1) * EXPERTS_PER_GROUP, :]
        m1, i1 = _first_argmax_rows(blk, g_iota, float(EXPERTS_PER_GROUP))
        m2 = jnp.max(jnp.where(g_iota == i1, neg, blk), axis=0, keepdims=True)
        grp.append(m1 + m2)
    grp = jnp.concatenate(grp, axis=0)
    gg_iota = lax.broadcasted_iota(jnp.int32, grp.shape, 0).astype(F32)
    grp_sel = jnp.zeros(grp.shape, F32)
    for _ in range(TOPK_GROUPS):
        _, gi = _first_argmax_rows(grp, gg_iota, float(N_EXPERT_GROUPS))
        hit = gg_iota == gi
        grp_sel = jnp.where(hit, 1.0, grp_sel)
        grp = jnp.where(hit, neg, grp)
    allowed = jnp.concatenate(
        [jnp.broadcast_to(grp_sel[g:g + 1, :], (EXPERTS_PER_GROUP, TR)) for g in range(N_EXPERT_GROUPS)], axis=0)
    masked = jnp.where(allowed > 0.5, biased, neg)

    idxs, gates = [], []
    onehot = jnp.zeros(scores.shape, F32)
    for _ in range(TOP_K):
        _, ei = _first_argmax_rows(masked, e_iota, float(N_EXPERTS))
        hit = e_iota == ei
        idxs.append(ei)
        gates.append(jnp.sum(jnp.where(hit, scores, 0.0), axis=0, keepdims=True))
        onehot = jnp.where(hit, 1.0, onehot)
        masked = jnp.where(hit, neg, masked)
    idx = jnp.concatenate(idxs, axis=0)
    gate = jnp.concatenate(gates, axis=0)
    gate = gate / jnp.sum(gate, axis=0, keepdims=True) * ROUTED_SCALE

    before = _dot(onehot.astype(BF16), tri_ref[...]) + carry[:, 0:1]
    ranks = [jnp.sum(jnp.where(e_iota == idxs[k], before, 0.0), axis=0, keepdims=True) for k in range(TOP_K)]
    idx_ref[...] = idx.astype(jnp.int32)
    gate_ref[...] = gate
    rank_ref[...] = jnp.concatenate(ranks, axis=0).astype(jnp.int32)
    total = carry[...] + jnp.sum(onehot, axis=1, keepdims=True)
    carry[...] = total
    cnt_ref[...] = total


def _route(lg_t, bias, tri):
    e, t = lg_t.shape
    tok = pl.BlockSpec((TOP_K, TR), lambda i: (0, i))
    return pl.pallas_call(
        _route_kernel,
        out_shape=(jax.ShapeDtypeStruct((TOP_K, t), jnp.int32),
                   jax.ShapeDtypeStruct((TOP_K, t), F32),
                   jax.ShapeDtypeStruct((TOP_K, t), jnp.int32),
                   jax.ShapeDtypeStruct((e, LANES), F32)),
        grid=(t // TR,),
        in_specs=[pl.BlockSpec((e, TR), lambda i: (0, i)),
                  pl.BlockSpec((e, 1), lambda i: (0, 0)),
                  pl.BlockSpec((TR, TR), lambda i: (0, 0))],
        out_specs=(tok, tok, tok, pl.BlockSpec((e, LANES), lambda i: (0, 0))),
        scratch_shapes=[pltpu.VMEM((e, LANES), F32)],
        compiler_params=pltpu.CompilerParams(vmem_limit_bytes=VMEM_LIMIT),
        name="route",
    )(lg_t, bias, tri)


def _plan_kernel(n_blocks, size_ref, start_ref, expert_ref, valid_ref, nact_ref):
    def per_expert(e, first_block):
        size = size_ref[e]
        n_blk = (size + TM_EXP - 1) // TM_EXP
        start_ref[e] = first_block * TM_EXP

        def per_block(j, carry):
            expert_ref[first_block + j] = e
            valid_ref[first_block + j] = jnp.minimum(size - j * TM_EXP, TM_EXP)
            return carry

        lax.fori_loop(0, n_blk, per_block, 0)
        return first_block + n_blk

    n_active = lax.fori_loop(0, N_EXPERTS, per_expert, 0)
    nact_ref[0] = n_active

    def unused(i, carry):
        expert_ref[i] = N_EXPERTS - 1
        valid_ref[i] = 0
        return carry

    lax.fori_loop(n_active, n_blocks, unused, 0)


def _plan_blocks(sizes, n_blocks):
    smem = pl.BlockSpec(memory_space=pltpu.SMEM)
    return pl.pallas_call(
        functools.partial(_plan_kernel, n_blocks),
        out_shape=(jax.ShapeDtypeStruct((N_EXPERTS,), jnp.int32),
                   jax.ShapeDtypeStruct((n_blocks,), jnp.int32),
                   jax.ShapeDtypeStruct((n_blocks,), jnp.int32),
                   jax.ShapeDtypeStruct((1,), jnp.int32)),
        in_specs=[smem],
        out_specs=(smem, smem, smem, smem),
        name="plan_blocks",
    )(sizes)


def _dest_kernel(start_ref, idx_ref, rank_ref, dest_ref):
    idx = idx_ref[...]
    dest = rank_ref[...]
    for e in range(N_EXPERTS):
        dest = dest + jnp.where(idx == e, start_ref[e], 0)
    dest_ref[...] = dest


def _dest_rows(pad_start, idx_kt, rank_kt):
    n_k, t = idx_kt.shape
    tile = 4096
    blk = pl.BlockSpec((n_k, tile), lambda i: (0, i))
    return pl.pallas_call(
        _dest_kernel,
        out_shape=jax.ShapeDtypeStruct((n_k, t), jnp.int32),
        grid=(t // tile,),
        in_specs=[pl.BlockSpec(memory_space=pltpu.SMEM), blk, blk],
        out_specs=blk,
        name="dest_rows",
    )(pad_start, idx_kt, rank_kt)


def _sc_mesh():
    return plsc.VectorSubcoreMesh(core_axis_name="c", subcore_axis_name="s")


def _sc_token_base(steps, j):
    worker = lax.axis_index("s") * SC_CORES + lax.axis_index("c")
    return (worker * steps + j) * SC_TOKENS


def _sc_scatter(dest_kt, h2p, n_rows):
    t, width = h2p.shape
    steps = t // (SC_WORKERS * SC_TOKENS)

    @functools.partial(
        pl.kernel, mesh=_sc_mesh(),
        out_type=jax.ShapeDtypeStruct((n_rows, width), U32),
        scratch_types=[pltpu.VMEM((TOP_K, SC_TOKENS), jnp.int32),
                       pltpu.VMEM((SC_TOKENS, width), U32),
                       pltpu.SemaphoreType.DMA],
        name="sc_scatter",
    )
    def body(dest_hbm, h_hbm, xs_hbm, idx_v, rows_v, sem):
        @pl.loop(0, steps)
        def _(j):
            base = _sc_token_base(steps, j)
            pltpu.sync_copy(dest_hbm.at[:, pl.ds(base, SC_TOKENS)], idx_v)
            pltpu.sync_copy(h_hbm.at[pl.ds(base, SC_TOKENS)], rows_v)
            copies = [pltpu.async_copy(rows_v, xs_hbm.at[idx_v.at[k]], sem) for k in range(TOP_K)]
            for cp in copies:
                cp.wait()

    return body(dest_kt, h2p)


def _sc_gather(dest_kt, ys, token0, n_tokens):
    n_k = dest_kt.shape[0]
    width = ys.shape[1]
    steps = n_tokens // (SC_WORKERS * SC_TOKENS)

    half = SC_TOKENS // 2
    units = [(k, h) for k in range(n_k) for h in range(2)]
    n_buf = 3

    @functools.partial(
        pl.kernel, mesh=_sc_mesh(),
        out_type=jax.ShapeDtypeStruct((n_k, n_tokens, width), U32),
        scratch_types=[pltpu.VMEM((n_k, SC_TOKENS), jnp.int32),
                       pltpu.VMEM((n_buf, half, width), U32),
                       pltpu.SemaphoreType.DMA((n_buf,))],
        name="sc_gather",
    )
    def body(dest_hbm, ys_hbm, yk_hbm, idx_v, rows_v, sems):
        @pl.loop(0, steps)
        def _(j):
            base = _sc_token_base(steps, j)
            pltpu.sync_copy(dest_hbm.at[:, pl.ds(token0 + base, SC_TOKENS)], idx_v)

            def gather(u):
                k, h = units[u]
                slot = u % n_buf
                return pltpu.make_async_copy(ys_hbm.at[idx_v.at[k, pl.ds(h * half, half)]], rows_v.at[slot],
                                             sems.at[slot])

            gather(0).start()
            gather(1).start()
            for u, (k, h) in enumerate(units):
                gather(u).wait()
                pltpu.sync_copy(rows_v.at[u % n_buf], yk_hbm.at[k, pl.ds(base + h * half, half)])
                if u + 2 < len(units):
                    gather(u + 2).start()

    return body(dest_kt, ys)


def _experts_kernel(be_ref, valid_ref, nact_ref, xs_ref, wg_ref, wu_ref, wd_ref, ys_ref, wg_bf, wu_bf, wd_bf):
    i = pl.program_id(0)
    prev = be_ref[jnp.maximum(i - 1, 0)]

    @pl.when((i == 0) | (be_ref[i] != prev))
    def _():
        wg_bf[...] = wg_ref[0].astype(BF16)
        wu_bf[...] = wu_ref[0].astype(BF16)
        wd_bf[...] = wd_ref[0].astype(BF16)

    valid = jnp.where(i < nact_ref[0], valid_ref[i], 0)

    def run(rows):
        row = lax.broadcasted_iota(jnp.int32, (rows, xs_ref.shape[1]), 0)
        words = jnp.where(row < valid, xs_ref[0:rows, :], jnp.uint32(0))
        xb = _unpack_words(words).astype(BF16)
        hid = _silu(_dot(xb, wg_bf[...])) * _dot(xb, wu_bf[...])
        ys_ref[0:rows, :] = _pack_words(_dot(hid.astype(BF16), wd_bf[...]))
        if rows < TM_EXP:
            ys_ref[rows:, :] = jnp.zeros((TM_EXP - rows, ys_ref.shape[1]), U32)

    rows, lower = TM_EXP, TM_EXP // 2
    while rows >= EXP_MIN_ROWS:
        lo = lower if rows > EXP_MIN_ROWS else 0
        pl.when((valid > lo) & (valid <= rows))(functools.partial(run, rows))
        rows, lower = rows // 2, lower // 2

    @pl.when(valid == 0)
    def _():
        ys_ref[...] = jnp.zeros_like(ys_ref)


def _experts(block_e, block_valid, n_active, xs, w_gate, w_up, w_down):
    n_rows, half = xs.shape
    d = w_gate.shape[1]
    nb = n_rows // TM_EXP
    grid_spec = pltpu.PrefetchScalarGridSpec(
        num_scalar_prefetch=3,
        grid=(nb,),
        in_specs=[pl.BlockSpec((TM_EXP, half), lambda i, be, bv, na: (jnp.minimum(i, na[0] - 1), 0)),
                  pl.BlockSpec((1, d, D_EXPERT), lambda i, be, bv, na: (be[i], 0, 0)),
                  pl.BlockSpec((1, d, D_EXPERT), lambda i, be, bv, na: (be[i], 0, 0)),
                  pl.BlockSpec((1, D_EXPERT, d), lambda i, be, bv, na: (be[i], 0, 0))],
        out_specs=pl.BlockSpec((TM_EXP, half), lambda i, be, bv, na: (i, 0)),
        scratch_shapes=[pltpu.VMEM((d, D_EXPERT), BF16), pltpu.VMEM((d, D_EXPERT), BF16),
                        pltpu.VMEM((D_EXPERT, d), BF16)],
    )
    return pl.pallas_call(
        _experts_kernel,
        out_shape=jax.ShapeDtypeStruct((n_rows, half), U32),
        grid_spec=grid_spec,
        compiler_params=pltpu.CompilerParams(vmem_limit_bytes=VMEM_LIMIT),
        name="experts",
    )(block_e, block_valid, n_active, xs, w_gate, w_up, w_down)


def _combine_kernel(yk_ref, x1_ref, h2_ref, gate_ref, mod_ref, wsg_ref, wsu_ref, wsd_ref, fg_ref, *out_refs):
    out_ref = out_refs[-1]
    hb = _unpack_words(h2_ref[...]).astype(BF16)
    hid = _silu(_dot(hb, wsg_ref[...])) * _dot(hb, wsu_ref[...])
    ffn = _dot(hid.astype(BF16), wsd_ref[...])
    gate = gate_ref[...]
    for k in range(TOP_K):
        ffn = ffn + gate[:, k:k + 1] * _unpack_words(yk_ref[k])
    x2 = x1_ref[...] + mod_ref[0, 5:6, :] * ffn
    ms = jnp.mean(x2 * x2, axis=-1, keepdims=True)
    out_ref[...] = x2 * lax.rsqrt(ms + EPS) * fg_ref[...]


def _combine(yk, token0, out_token0, n_out_tokens, prev_out, x1, h2p, gate_tk, mod, wsg_bf, wsu_bf, wsd_bf, final_g,
             seq_len):
    t, d = n_out_tokens, x1.shape[1]
    tiles_per_seq = seq_len // TF
    tile0 = token0 // TF
    out_tile0 = out_token0 // TF
    in_tok = pl.BlockSpec((TF, d), lambda i: (tile0 + i, 0))
    tok = pl.BlockSpec((TF, d), lambda i: (out_tile0 + i, 0))
    in_specs = [pl.BlockSpec((TOP_K, TF, d // 2), lambda i: (0, i, 0)),
                in_tok, pl.BlockSpec((TF, d // 2), lambda i: (tile0 + i, 0)),
                pl.BlockSpec((TF, TOP_K), lambda i: (tile0 + i, 0)),
                pl.BlockSpec((1, 6, d), lambda i: ((out_tile0 + i) // tiles_per_seq, 0, 0)),
                pl.BlockSpec((d, D_SHARED), lambda i: (0, 0)),
                pl.BlockSpec((d, D_SHARED), lambda i: (0, 0)),
                pl.BlockSpec((D_SHARED, d), lambda i: (0, 0)),
                pl.BlockSpec((1, d), lambda i: (0, 0))]
    args = [yk, x1, h2p, gate_tk, mod, wsg_bf, wsu_bf, wsd_bf, final_g]
    aliases = {}
    if prev_out is not None:
        in_specs.append(pl.BlockSpec(memory_space=pl.ANY))
        args.append(prev_out)
        aliases = {len(args) - 1: 0}
    return pl.pallas_call(
        _combine_kernel,
        out_shape=jax.ShapeDtypeStruct((t, d), F32),
        grid=(yk.shape[1] // TF,),
        in_specs=in_specs,
        out_specs=tok,
        input_output_aliases=aliases,
        compiler_params=pltpu.CompilerParams(vmem_limit_bytes=VMEM_LIMIT),
        name="combine",
    )(*args)


def _rope_tables(n_tokens):
    n_rows = n_tokens // GRID_W
    n_freq = HEAD_DIM // 4
    inv_freq = ROPE_THETA ** (-jnp.arange(n_freq, dtype=F32) / n_freq)
    ang_r = jnp.arange(n_rows).astype(F32)[:, None] * inv_freq[None, :]
    ang_c = jnp.arange(GRID_W).astype(F32)[:, None] * inv_freq[None, :]

    def per_token(row_part, col_part):
        rows = jnp.broadcast_to(row_part[:, None, :], (n_rows, GRID_W, n_freq))
        cols = jnp.broadcast_to(col_part[None, :, :], (n_rows, GRID_W, n_freq))
        return rows.reshape(n_tokens, n_freq), cols.reshape(n_tokens, n_freq)

    cos_r, cos_c = per_token(jnp.cos(ang_r), jnp.cos(ang_c))
    sin_r, sin_c = per_token(jnp.sin(ang_r), jnp.sin(ang_c))
    cos = jnp.concatenate([cos_r, cos_r, cos_c, cos_c], axis=1)
    sin = jnp.concatenate([-sin_r, sin_r, -sin_c, sin_c], axis=1)
    reps = LANES // HEAD_DIM
    return jnp.tile(cos, (1, reps)), jnp.tile(sin, (1, reps))


def _pool_bands():
    i = jnp.arange(QB)[:, None]
    r = jnp.arange(POOL_SLAB)[None, :]
    return jnp.stack([((r >= i + POOL_OFF - w // 2) & (r < i + POOL_OFF + w // 2)).astype(BF16)
                      for w in POOL_WINDOWS])


def kernel(x, c, ctx, c_ctx, w_ada, b_ada, norm1_g, norm2_g, w_in, attn_sink, pool_w, pool_scale, w_out,
           w_router, router_bias, w_gate, w_up, w_down, ws_gate, ws_up, ws_down, final_g):
    b, s, d = x.shape
    t = b * s
    assert w_ada.shape[0] == 1 and d == D_MODEL and s % TQ == 0 and b + 1 <= 8

    c8 = jnp.zeros((8, d), F32).at[:b].set(c).at[b].set(c_ctx)
    mod = _ada(c8, w_ada[0], b_ada[0]).reshape(8, 6, d)
    g1 = norm1_g[0].reshape(1, d)
    g2 = norm2_g[0].reshape(1, d)
    w_in_bf = w_in[0].astype(BF16)
    cos_t, sin_t = _rope_tables(s)

    q, k4, v4, p = _inproj(x, mod, g1, w_in_bf, cos_t, sin_t)
    kc4, vc4 = _ctxproj(ctx, mod[b:b + 1], g1, w_in_bf[:, ATTN_WIDTH:ATTN_WIDTH + 2 * KV_WIDTH])

    wr_t = w_router[0].T
    wr_hi = wr_t.astype(BF16)
    wr_lo = (wr_t - wr_hi.astype(F32)).astype(BF16)
    attn_consts = (_pool_bands(), pool_w[0].astype(BF16), pool_scale[0].reshape(1, POOL_WIDTH),
                   w_out[0].astype(BF16), g2, wr_hi, wr_lo)
    tri = jnp.triu(jnp.ones((TR, TR), BF16), k=1)
    shared_w = (ws_gate[0].astype(BF16), ws_up[0].astype(BF16), ws_down[0].astype(BF16))

    nb = b // TOKEN_GROUPS
    tg = nb * s
    assert b % TOKEN_GROUPS == 0 and tg % (COMBINE_CHUNKS * SC_WORKERS * SC_TOKENS) == 0
    n_rows = -(-(tg * TOP_K + N_EXPERTS * (TM_EXP - 1)) // TM_EXP) * TM_EXP
    groups = []
    for g in range(TOKEN_GROUPS):
        x1, h2p, lg_t = _attn(g * nb, nb, attn_sink[0], q, k4, v4, kc4, vc4, p, x, mod, *attn_consts)
        idx_kt, gate_kt, rank_kt, counts = _route(lg_t, router_bias[0].reshape(N_EXPERTS, 1), tri)
        pad_start, block_e, block_valid, n_active = _plan_blocks(counts[:, 0].astype(jnp.int32), n_rows // TM_EXP)
        dest_kt = _dest_rows(pad_start, idx_kt, rank_kt)
        xs = _sc_scatter(dest_kt, h2p, n_rows)
        groups.append((x1.reshape(tg, d), h2p, gate_kt.T, dest_kt, xs, block_e, block_valid, n_active))

    out = None
    chunk = tg // COMBINE_CHUNKS
    for g, (x1, h2p, gate_tk, dest_kt, xs, block_e, block_valid, n_active) in enumerate(groups):
        ys = _experts(block_e, block_valid, n_active, xs, w_gate[0], w_up[0], w_down[0])
        for token0 in range(0, tg, chunk):
            out = _combine(_sc_gather(dest_kt, ys, token0, chunk), token0, g * tg + token0, t, out, x1, h2p,
                           gate_tk, mod, *shared_w, final_g.reshape(1, d), s)
    return out.reshape(b, s, d)
```

```python
import functools

import jax
import jax.numpy as jnp
from jax import lax
from jax.experimental import pallas as pl
from jax.experimental.pallas import tpu as pltpu
from jax.experimental.pallas import tpu_sc as plsc

F32 = jnp.float32
BF16 = jnp.bfloat16

D_MODEL = 1024
GRID_W = 64
N_HEADS = 8
N_KV_HEADS = 2
HEAD_DIM = 64
ATTN_WIDTH = N_HEADS * HEAD_DIM
KV_WIDTH = N_KV_HEADS * HEAD_DIM
WINDOW = 128
ROPE_THETA = 10000.0
POOL_WINDOWS = (2, 4, 8, 16)
POOL_WIDTH = D_MODEL - ATTN_WIDTH
POOL_GROUP_DIM = POOL_WIDTH // len(POOL_WINDOWS)
IN_COLS = ATTN_WIDTH + 2 * KV_WIDTH + POOL_WIDTH
N_EXPERTS = 64
TOP_K = 8
N_EXPERT_GROUPS = 8
EXPERTS_PER_GROUP = N_EXPERTS // N_EXPERT_GROUPS
TOPK_GROUPS = 4
D_EXPERT = 256
D_SHARED = 256
ROUTED_SCALE = 2.5
EPS = 1e-6
LOG2E = 1.4426950408889634

LANES = 128
U32 = jnp.uint32
VMEM_LIMIT = 48 * 1024 * 1024

TM_PROJ = 1024
TQ = 1024
QB = 128
POOL_SLAB = 256
POOL_OFF = 64
TR = 512
TM_EXP = 1024
EXP_MIN_ROWS = 256
TF = 256
TOKEN_GROUPS = 2
COMBINE_CHUNKS = 2
SC_CORES = 2
SC_WORKERS = 32
SC_TOKENS = 128


def _silu(x):
    return x * (1.0 / (1.0 + jnp.exp(-x)))


def _split_bf16(x):
    hi = x.astype(BF16)
    lo = (x - hi.astype(F32)).astype(BF16)
    return hi, lo


def _dot(a, b):
    return jnp.dot(a, b, preferred_element_type=F32)


def _pack_words(val):
    half = val.shape[1] // 2
    lo = lax.bitcast_convert_type(val[:, :half].astype(BF16).astype(F32), U32)
    hi = lax.bitcast_convert_type(val[:, half:].astype(BF16).astype(F32), U32)
    return lax.shift_right_logical(lo, jnp.uint32(16)) | hi


def _unpack_words(words):
    lo = lax.bitcast_convert_type(lax.shift_left(words, jnp.uint32(16)), F32)
    hi = lax.bitcast_convert_type(words & jnp.uint32(0xFFFF0000), F32)
    return jnp.concatenate([lo, hi], axis=1)


def _dot_nt(a, b):
    return lax.dot_general(a, b, (((1,), (1,)), ((), ())), preferred_element_type=F32)


def _ada_kernel(c_ref, w_ref, b_ref, o_ref):
    a_hi, a_lo = _split_bf16(_silu(c_ref[...]))
    w_hi, w_lo = _split_bf16(w_ref[...])
    o_ref[...] = _dot(a_hi, w_hi) + _dot(a_lo, w_hi) + _dot(a_hi, w_lo) + b_ref[...]


def _ada(c8, w_ada, b_ada):
    d = c8.shape[1]
    n = w_ada.shape[1]
    tn = 512
    return pl.pallas_call(
        _ada_kernel,
        out_shape=jax.ShapeDtypeStruct((8, n), F32),
        grid=(n // tn,),
        in_specs=[pl.BlockSpec((8, d), lambda j: (0, 0)),
                  pl.BlockSpec((d, tn), lambda j: (0, j)),
                  pl.BlockSpec((1, tn), lambda j: (0, j))],
        out_specs=pl.BlockSpec((8, tn), lambda j: (0, j)),
        compiler_params=pltpu.CompilerParams(vmem_limit_bytes=VMEM_LIMIT),
        name="ada",
    )(c8, w_ada, b_ada.reshape(1, n))


def _norm_mod(x, g, shift, scale):
    ms = jnp.mean(x * x, axis=-1, keepdims=True)
    return (x * lax.rsqrt(ms + EPS) * g) * (1.0 + scale) + shift


def _lane_variants(t):
    lane = lax.broadcasted_iota(jnp.int32, t.shape, 1)
    lo = lane < HEAD_DIM
    tr = pltpu.roll(t, HEAD_DIM, 1)
    zero = jnp.zeros_like(t)
    return (jnp.where(lo, t, zero), jnp.where(lo, zero, tr),
            jnp.where(lo, tr, zero), jnp.where(lo, zero, t))


def _store_variants(ref, t):
    for i, var in enumerate(_lane_variants(t)):
        ref[0, :, i * LANES:(i + 1) * LANES] = var.astype(BF16)


def _inproj_kernel(x_ref, mod_ref, g_ref, w_ref, cos_ref, sin_ref, q_ref, k_ref, v_ref, p_ref):
    h = _norm_mod(x_ref[0], g_ref[...], mod_ref[0, 0:1, :], mod_ref[0, 1:2, :])
    z = _dot(h.astype(BF16), w_ref[...])
    cos = cos_ref[...]
    sin = sin_ref[...]
    lane = lax.broadcasted_iota(jnp.int32, cos.shape, 1)
    first_half = (lane & 16) == 0

    def rope(zc):
        partner = jnp.where(first_half, pltpu.roll(zc, LANES - 16, 1), pltpu.roll(zc, 16, 1))
        return zc * cos + partner * sin

    scale = HEAD_DIM ** -0.5 * LOG2E
    for c in range(ATTN_WIDTH // LANES):
        q_ref[0, :, c * LANES:(c + 1) * LANES] = (rope(z[:, c * LANES:(c + 1) * LANES]) * scale).astype(BF16)
    _store_variants(k_ref, rope(z[:, ATTN_WIDTH:ATTN_WIDTH + KV_WIDTH]))
    _store_variants(v_ref, z[:, ATTN_WIDTH + KV_WIDTH:ATTN_WIDTH + 2 * KV_WIDTH])
    p_ref[0] = z[:, ATTN_WIDTH + 2 * KV_WIDTH:]


def _inproj(x, mod, g1, w_in_bf, cos_t, sin_t):
    b, s, d = x.shape
    tm = TM_PROJ
    return pl.pallas_call(
        _inproj_kernel,
        out_shape=(jax.ShapeDtypeStruct((b, s, ATTN_WIDTH), BF16),
                   jax.ShapeDtypeStruct((b, s, 4 * LANES), BF16),
                   jax.ShapeDtypeStruct((b, s, 4 * LANES), BF16),
                   jax.ShapeDtypeStruct((b, s, POOL_WIDTH), F32)),
        grid=(s // tm, b),
        in_specs=[pl.BlockSpec((1, tm, d), lambda n, bi: (bi, n, 0)),
                  pl.BlockSpec((1, 6, d), lambda n, bi: (bi, 0, 0)),
                  pl.BlockSpec((1, d), lambda n, bi: (0, 0)),
                  pl.BlockSpec((d, IN_COLS), lambda n, bi: (0, 0)),
                  pl.BlockSpec((tm, LANES), lambda n, bi: (n, 0)),
                  pl.BlockSpec((tm, LANES), lambda n, bi: (n, 0))],
        out_specs=(pl.BlockSpec((1, tm, ATTN_WIDTH), lambda n, bi: (bi, n, 0)),
                   pl.BlockSpec((1, tm, 4 * LANES), lambda n, bi: (bi, n, 0)),
                   pl.BlockSpec((1, tm, 4 * LANES), lambda n, bi: (bi, n, 0)),
                   pl.BlockSpec((1, tm, POOL_WIDTH), lambda n, bi: (bi, n, 0))),
        compiler_params=pltpu.CompilerParams(vmem_limit_bytes=VMEM_LIMIT),
        name="inproj",
    )(x, mod, g1, w_in_bf, cos_t, sin_t)


def _ctxproj_kernel(x_ref, mod_ref, g_ref, w_ref, k_ref, v_ref):
    h = _norm_mod(x_ref[0], g_ref[...], mod_ref[0, 0:1, :], mod_ref[0, 1:2, :])
    z = _dot(h.astype(BF16), w_ref[...])
    _store_variants(k_ref, z[:, :KV_WIDTH])
    _store_variants(v_ref, z[:, KV_WIDTH:])


def _ctxproj(ctx, mod_c, g1, w_kv_bf):
    b, c, d = ctx.shape
    return pl.pallas_call(
        _ctxproj_kernel,
        out_shape=(jax.ShapeDtypeStruct((b, c, 4 * LANES), BF16),
                   jax.ShapeDtypeStruct((b, c, 4 * LANES), BF16)),
        grid=(b,),
        in_specs=[pl.BlockSpec((1, c, d), lambda bi: (bi, 0, 0)),
                  pl.BlockSpec((1, 6, d), lambda bi: (0, 0, 0)),
                  pl.BlockSpec((1, d), lambda bi: (0, 0)),
                  pl.BlockSpec((d, 2 * KV_WIDTH), lambda bi: (0, 0))],
        out_specs=(pl.BlockSpec((1, c, 4 * LANES), lambda bi: (bi, 0, 0)),
                   pl.BlockSpec((1, c, 4 * LANES), lambda bi: (bi, 0, 0))),
        compiler_params=pltpu.CompilerParams(vmem_limit_bytes=VMEM_LIMIT),
        name="ctxproj",
    )(ctx, mod_c, g1, w_kv_bf)


def _fold(op, tiles):
    while len(tiles) > 1:
        tiles = [op(tiles[i], tiles[i + 1]) if i + 1 < len(tiles) else tiles[i] for i in range(0, len(tiles), 2)]
    return tiles[0]


def _stack_variants(t4, kv):
    return jnp.concatenate([t4[:, (2 * kv) * LANES:(2 * kv + 1) * LANES],
                            t4[:, (2 * kv + 1) * LANES:(2 * kv + 2) * LANES]], axis=0)


def _attn_kernel(seq_len, sink_ref, q_ref, k_ref, kp_ref, kn_ref, v_ref, vp_ref, vn_ref, kc_ref, vc_ref,
                 p_ref, pp_ref, pn_ref, x_ref, mod_ref, band_ref, poolw_ref, pscale_ref, wout_ref,
                 g2_ref, wrh_ref, wrl_ref, x1_ref, h2_ref, lg_ref, kwin, vwin, pext, mix, s_scr, p_scr, m_scr):
    n = pl.program_id(1)
    n_last = pl.num_programs(1) - 1

    kwin[0:QB, :] = kp_ref[0]
    kwin[QB:QB + TQ, :] = k_ref[0]
    kwin[QB + TQ:, :] = kn_ref[0]
    vwin[0:QB, :] = vp_ref[0]
    vwin[QB:QB + TQ, :] = v_ref[0]
    vwin[QB + TQ:, :] = vn_ref[0]

    pext[0:QB - 8, :] = jnp.zeros((QB - 8, POOL_WIDTH), F32)
    pext[QB - 8:QB, :] = jnp.where(n > 0, pp_ref[0], 0.0)
    pext[QB:QB + TQ, :] = p_ref[0]
    pext[QB + TQ:QB + TQ + 8, :] = jnp.where(n < n_last, pn_ref[0], 0.0)
    pext[QB + TQ + 8:, :] = jnp.zeros((QB - 8, POOL_WIDTH), F32)

    row = lax.broadcasted_iota(jnp.int32, (QB, 3 * QB), 0)
    col = lax.broadcasted_iota(jnp.int32, (QB, 3 * QB), 1)
    in_band = (col >= row) & (col <= row + 2 * WINDOW)
    tok = lax.broadcasted_iota(jnp.int32, (QB, 1), 0)
    kc = kc_ref[0]
    kc_rows = [_stack_variants(kc, kv) for kv in range(N_KV_HEADS)]
    vc_rows = [_stack_variants(vc_ref[0], kv) for kv in range(N_KV_HEADS)]

    def sub_block(j, carry):
        r0 = pl.multiple_of(j * QB, QB)
        qj = q_ref[0, pl.ds(r0, QB), :]
        kw = kwin[pl.ds(r0, 3 * QB), :]
        vw = vwin[pl.ds(r0, 3 * QB), :]
        kpos = col + (n * TQ + j * QB - QB)
        ok = in_band & (kpos >= 0) & (kpos < seq_len)
        bias = jnp.where(ok, 0.0, -jnp.inf)
        n_loc, n_ctx = 3 * QB, kc.shape[0]
        bias2 = jnp.concatenate([bias, bias], axis=1)
        k_rows = [_stack_variants(kw, kv) for kv in range(N_KV_HEADS)]
        v_rows = [_stack_variants(vw, kv) for kv in range(N_KV_HEADS)]
        group = N_HEADS // N_KV_HEADS

        def head_tiles(head):
            loc0 = (head % 2) * n_loc
            ctx0 = 2 * n_loc + (head % 2) * n_ctx
            return ([loc0 + i * LANES for i in range(n_loc // LANES)]
                    + [ctx0 + i * LANES for i in range(n_ctx // LANES)])

        for c in range(N_HEADS // 2):
            qc = qj[:, c * LANES:(c + 1) * LANES]
            s_scr[c, :, 0:2 * n_loc] = _dot_nt(qc, k_rows[2 * c // group]) + bias2
            s_scr[c, :, 2 * n_loc:] = _dot_nt(qc, kc_rows[2 * c // group])
        for head in range(N_HEADS):
            tiles = [s_scr[head // 2, :, st:st + LANES] for st in head_tiles(head)]
            row_max = jnp.max(_fold(jnp.maximum, tiles), axis=1, keepdims=True)
            m_scr[head] = jnp.broadcast_to(jnp.maximum(row_max, sink_ref[head] * LOG2E), (QB, LANES))
        for head in range(N_HEADS):
            m = m_scr[head]
            acc = None
            for st in head_tiles(head):
                p = jnp.exp2(s_scr[head // 2, :, st:st + LANES] - m)
                p_scr[head // 2, :, st:st + LANES] = p.astype(BF16)
                acc = p if acc is None else acc + p
            denom = (jnp.broadcast_to(jnp.sum(acc, axis=1, keepdims=True), (QB, LANES))
                     + jnp.exp2(sink_ref[head] * LOG2E - m))
            m_scr[head] = 1.0 / denom
        lane = lax.broadcasted_iota(jnp.int32, (QB, LANES), 1)
        for c in range(N_HEADS // 2):
            o = (_dot(p_scr[c, :, 0:2 * n_loc], v_rows[2 * c // group])
                 + _dot(p_scr[c, :, 2 * n_loc:], vc_rows[2 * c // group]))
            o = o * jnp.where(lane < HEAD_DIM, m_scr[2 * c], m_scr[2 * c + 1])
            mix[pl.ds(r0, QB), c * LANES:(c + 1) * LANES] = o.astype(BF16)

        slab = pext[pl.ds(pl.multiple_of(r0 + POOL_OFF, 8), POOL_SLAB), :]
        tpos = tok + (n * TQ + j * QB)
        for g, w in enumerate(POOL_WINDOWS):
            sg = slab[:, g * LANES:(g + 1) * LANES]
            hi, lo = _split_bf16(sg)
            band = band_ref[g]
            wsum = _dot(band, hi) + _dot(band, lo)
            cnt = (jnp.minimum(tpos - w // 2 + w, seq_len) - jnp.maximum(tpos - w // 2, 0)).astype(F32)
            dlt = wsum / cnt - sg[POOL_OFF:POOL_OFF + QB, :]
            y = _dot(dlt.astype(BF16), poolw_ref[g]) * pscale_ref[:, g * LANES:(g + 1) * LANES]
            mix[pl.ds(r0, QB), ATTN_WIDTH + g * LANES:ATTN_WIDTH + (g + 1) * LANES] = y.astype(BF16)
        return carry

    lax.fori_loop(0, TQ // QB, sub_block, 0)

    proj = _dot(mix[...], wout_ref[...])
    x1 = x_ref[0] + mod_ref[0, 2:3, :] * proj
    x1_ref[0] = x1
    h2 = _norm_mod(x1, g2_ref[...], mod_ref[0, 3:4, :], mod_ref[0, 4:5, :])
    h2_ref[...] = _pack_words(h2)
    h_hi, h_lo = _split_bf16(h2)
    wrh = wrh_ref[...]
    lg_ref[...] = _dot_nt(wrh, h_hi) + _dot_nt(wrh, h_lo) + _dot_nt(wrl_ref[...], h_hi)


def _attn(b0, b, sink, q, k4, v4, kc4, vc4, p, x, mod, band, poolw_bf, pscale, wout_bf, g2, wr_hi, wr_lo):
    _, s, d = x.shape
    c = kc4.shape[1]
    nt = s // TQ
    hb = TQ // QB
    pb = TQ // 8
    kv_main = pl.BlockSpec((1, TQ, 4 * LANES), lambda bi, n: (b0 + bi, n, 0))
    kv_prev = pl.BlockSpec((1, QB, 4 * LANES), lambda bi, n: (b0 + bi, jnp.maximum(n * hb - 1, 0), 0))
    kv_next = pl.BlockSpec((1, QB, 4 * LANES), lambda bi, n: (b0 + bi, jnp.minimum((n + 1) * hb, s // QB - 1), 0))
    const2 = lambda bi, n: (0, 0)
    const3 = lambda bi, n: (0, 0, 0)
    return pl.pallas_call(
        functools.partial(_attn_kernel, s),
        out_shape=(jax.ShapeDtypeStruct((b, s, d), F32),
                   jax.ShapeDtypeStruct((b * s, d // 2), U32),
                   jax.ShapeDtypeStruct((N_EXPERTS, b * s), F32)),
        grid=(b, nt),
        in_specs=[pl.BlockSpec(memory_space=pltpu.SMEM),
                  pl.BlockSpec((1, TQ, ATTN_WIDTH), lambda bi, n: (b0 + bi, n, 0)),
                  kv_main, kv_prev, kv_next, kv_main, kv_prev, kv_next,
                  pl.BlockSpec((1, c, 4 * LANES), lambda bi, n: (b0 + bi, 0, 0)),
                  pl.BlockSpec((1, c, 4 * LANES), lambda bi, n: (b0 + bi, 0, 0)),
                  pl.BlockSpec((1, TQ, POOL_WIDTH), lambda bi, n: (b0 + bi, n, 0)),
                  pl.BlockSpec((1, 8, POOL_WIDTH), lambda bi, n: (b0 + bi, jnp.maximum(n * pb - 1, 0), 0)),
                  pl.BlockSpec((1, 8, POOL_WIDTH),
                               lambda bi, n: (b0 + bi, jnp.minimum((n + 1) * pb, s // 8 - 1), 0)),
                  pl.BlockSpec((1, TQ, d), lambda bi, n: (b0 + bi, n, 0)),
                  pl.BlockSpec((1, 6, d), lambda bi, n: (b0 + bi, 0, 0)),
                  pl.BlockSpec((len(POOL_WINDOWS), QB, POOL_SLAB), const3),
                  pl.BlockSpec((len(POOL_WINDOWS), POOL_GROUP_DIM, POOL_GROUP_DIM), const3),
                  pl.BlockSpec((1, POOL_WIDTH), const2),
                  pl.BlockSpec((d, d), const2),
                  pl.BlockSpec((1, d), const2),
                  pl.BlockSpec((N_EXPERTS, d), const2),
                  pl.BlockSpec((N_EXPERTS, d), const2)],
        out_specs=(pl.BlockSpec((1, TQ, d), lambda bi, n: (bi, n, 0)),
                   pl.BlockSpec((TQ, d // 2), lambda bi, n: (bi * nt + n, 0)),
                   pl.BlockSpec((N_EXPERTS, TQ), lambda bi, n: (0, bi * nt + n))),
        scratch_shapes=[pltpu.VMEM((TQ + 2 * QB, 4 * LANES), BF16),
                        pltpu.VMEM((TQ + 2 * QB, 4 * LANES), BF16),
                        pltpu.VMEM((TQ + 2 * QB, POOL_WIDTH), F32),
                        pltpu.VMEM((TQ, d), BF16),
                        pltpu.VMEM((N_HEADS // 2, QB, 2 * (3 * QB + c)), F32),
                        pltpu.VMEM((N_HEADS // 2, QB, 2 * (3 * QB + c)), BF16),
                        pltpu.VMEM((N_HEADS, QB, LANES), F32)],
        compiler_params=pltpu.CompilerParams(vmem_limit_bytes=VMEM_LIMIT),
        name="attn",
    )(sink, q, k4, k4, k4, v4, v4, v4, kc4, vc4, p, p, p, x, mod, band, poolw_bf, pscale, wout_bf,
      g2, wr_hi, wr_lo)


def _first_argmax_rows(v, row_iota, n_rows):
    m = jnp.max(v, axis=0, keepdims=True)
    idx = jnp.min(jnp.where(v == m, row_iota, n_rows), axis=0, keepdims=True)
    return m, idx


def _route_kernel(lg_ref, bias_ref, tri_ref, idx_ref, gate_ref, rank_ref, cnt_ref, carry):
    i = pl.program_id(0)

    @pl.when(i == 0)
    def _():
        carry[...] = jnp.zeros_like(carry)

    scores = 1.0 / (1.0 + jnp.exp(-lg_ref[...]))
    biased = scores + bias_ref[...]
    e_iota = lax.broadcasted_iota(jnp.int32, scores.shape, 0).astype(F32)
    g_iota = lax.broadcasted_iota(jnp.int32, (EXPERTS_PER_GROUP, TR), 0).astype(F32)
    neg = -jnp.inf

    grp = []
    for g in range(N_EXPERT_GROUPS):
        blk = biased[g * EXPERTS_PER_GROUP:(g + 1) * EXPERTS_PER_GROUP, :]
        m1, i1 = _first_argmax_rows(blk, g_iota, float(EXPERTS_PER_GROUP))
        m2 = jnp.max(jnp.where(g_iota == i1, neg, blk), axis=0, keepdims=True)
        grp.append(m1 + m2)
    grp = jnp.concatenate(grp, axis=0)
    gg_iota = lax.broadcasted_iota(jnp.int32, grp.shape, 0).astype(F32)
    grp_sel = jnp.zeros(grp.shape, F32)
    for _ in range(TOPK_GROUPS):
        _, gi = _first_argmax_rows(grp, gg_iota, float(N_EXPERT_GROUPS))
        hit = gg_iota == gi
        grp_sel = jnp.where(hit, 1.0, grp_sel)
        grp = jnp.where(hit, neg, grp)
    allowed = jnp.concatenate(
        [jnp.broadcast_to(grp_sel[g:g + 1, :], (EXPERTS_PER_GROUP, TR)) for g in range(N_EXPERT_GROUPS)], axis=0)
    masked = jnp.where(allowed > 0.5, biased, neg)

    idxs, gates = [], []
    onehot = jnp.zeros(scores.shape, F32)
    for _ in range(TOP_K):
        _, ei = _first_argmax_rows(masked, e_iota, float(N_EXPERTS))
        hit = e_iota == ei
        idxs.append(ei)
        gates.append(jnp.sum(jnp.where(hit, scores, 0.0), axis=0, keepdims=True))
        onehot = jnp.where(hit, 1.0, onehot)
        masked = jnp.where(hit, neg, masked)
    idx = jnp.concatenate(idxs, axis=0)
    gate = jnp.concatenate(gates, axis=0)
    gate = gate / jnp.sum(gate, axis=0, keepdims=True) * ROUTED_SCALE

    before = _dot(onehot.astype(BF16), tri_ref[...]) + carry[:, 0:1]
    ranks = [jnp.sum(jnp.where(e_iota == idxs[k], before, 0.0), axis=0, keepdims=True) for k in range(TOP_K)]
    idx_ref[...] = idx.astype(jnp.int32)
    gate_ref[...] = gate
    rank_ref[...] = jnp.concatenate(ranks, axis=0).astype(jnp.int32)
    total = carry[...] + jnp.sum(onehot, axis=1, keepdims=True)
    carry[...] = total
    cnt_ref[...] = total


def _route(lg_t, bias, tri):
    e, t = lg_t.shape
    tok = pl.BlockSpec((TOP_K, TR), lambda i: (0, i))
    return pl.pallas_call(
        _route_kernel,
        out_shape=(jax.ShapeDtypeStruct((TOP_K, t), jnp.int32),
                   jax.ShapeDtypeStruct((TOP_K, t), F32),
                   jax.ShapeDtypeStruct((TOP_K, t), jnp.int32),
                   jax.ShapeDtypeStruct((e, LANES), F32)),
        grid=(t // TR,),
        in_specs=[pl.BlockSpec((e, TR), lambda i: (0, i)),
                  pl.BlockSpec((e, 1), lambda i: (0, 0)),
                  pl.BlockSpec((TR, TR), lambda i: (0, 0))],
        out_specs=(tok, tok, tok, pl.BlockSpec((e, LANES), lambda i: (0, 0))),
        scratch_shapes=[pltpu.VMEM((e, LANES), F32)],
        compiler_params=pltpu.CompilerParams(vmem_limit_bytes=VMEM_LIMIT),
        name="route",
    )(lg_t, bias, tri)


def _plan_kernel(n_blocks, size_ref, start_ref, expert_ref, valid_ref, nact_ref):
    def per_expert(e, first_block):
        size = size_ref[e]
        n_blk = (size + TM_EXP - 1) // TM_EXP
        start_ref[e] = first_block * TM_EXP

        def per_block(j, carry):
            expert_ref[first_block + j] = e
            valid_ref[first_block + j] = jnp.minimum(size - j * TM_EXP, TM_EXP)
            return carry

        lax.fori_loop(0, n_blk, per_block, 0)
        return first_block + n_blk

    n_active = lax.fori_loop(0, N_EXPERTS, per_expert, 0)
    nact_ref[0] = n_active

    def unused(i, carry):
        expert_ref[i] = N_EXPERTS - 1
        valid_ref[i] = 0
        return carry

    lax.fori_loop(n_active, n_blocks, unused, 0)


def _plan_blocks(sizes, n_blocks):
    smem = pl.BlockSpec(memory_space=pltpu.SMEM)
    return pl.pallas_call(
        functools.partial(_plan_kernel, n_blocks),
        out_shape=(jax.ShapeDtypeStruct((N_EXPERTS,), jnp.int32),
                   jax.ShapeDtypeStruct((n_blocks,), jnp.int32),
                   jax.ShapeDtypeStruct((n_blocks,), jnp.int32),
                   jax.ShapeDtypeStruct((1,), jnp.int32)),
        in_specs=[smem],
        out_specs=(smem, smem, smem, smem),
        name="plan_blocks",
    )(sizes)


def _dest_kernel(start_ref, idx_ref, rank_ref, dest_ref):
    idx = idx_ref[...]
    dest = rank_ref[...]
    for e in range(N_EXPERTS):
        dest = dest + jnp.where(idx == e, start_ref[e], 0)
    dest_ref[...] = dest


def _dest_rows(pad_start, idx_kt, rank_kt):
    n_k, t = idx_kt.shape
    tile = 4096
    blk = pl.BlockSpec((n_k, tile), lambda i: (0, i))
    return pl.pallas_call(
        _dest_kernel,
        out_shape=jax.ShapeDtypeStruct((n_k, t), jnp.int32),
        grid=(t // tile,),
        in_specs=[pl.BlockSpec(memory_space=pltpu.SMEM), blk, blk],
        out_specs=blk,
        name="dest_rows",
    )(pad_start, idx_kt, rank_kt)


def _sc_mesh():
    return plsc.VectorSubcoreMesh(core_axis_name="c", subcore_axis_name="s")


def _sc_token_base(steps, j):
    worker = lax.axis_index("s") * SC_CORES + lax.axis_index("c")
    return (worker * steps + j) * SC_TOKENS


def _sc_scatter(dest_kt, h2p, n_rows):
    t, width = h2p.shape
    steps = t // (SC_WORKERS * SC_TOKENS)

    @functools.partial(
        pl.kernel, mesh=_sc_mesh(),
        out_type=jax.ShapeDtypeStruct((n_rows, width), U32),
        scratch_types=[pltpu.VMEM((TOP_K, SC_TOKENS), jnp.int32),
                       pltpu.VMEM((SC_TOKENS, width), U32),
                       pltpu.SemaphoreType.DMA],
        name="sc_scatter",
    )
    def body(dest_hbm, h_hbm, xs_hbm, idx_v, rows_v, sem):
        @pl.loop(0, steps)
        def _(j):
            base = _sc_token_base(steps, j)
            pltpu.sync_copy(dest_hbm.at[:, pl.ds(base, SC_TOKENS)], idx_v)
            pltpu.sync_copy(h_hbm.at[pl.ds(base, SC_TOKENS)], rows_v)
            copies = [pltpu.async_copy(rows_v, xs_hbm.at[idx_v.at[k]], sem) for k in range(TOP_K)]
            for cp in copies:
                cp.wait()

    return body(dest_kt, h2p)


def _sc_gather(dest_kt, ys, token0, n_tokens):
    n_k = dest_kt.shape[0]
    width = ys.shape[1]
    steps = n_tokens // (SC_WORKERS * SC_TOKENS)

    half = SC_TOKENS // 2
    units = [(k, h) for k in range(n_k) for h in range(2)]
    n_buf = 3

    @functools.partial(
        pl.kernel, mesh=_sc_mesh(),
        out_type=jax.ShapeDtypeStruct((n_k, n_tokens, width), U32),
        scratch_types=[pltpu.VMEM((n_k, SC_TOKENS), jnp.int32),
                       pltpu.VMEM((n_buf, half, width), U32),
                       pltpu.SemaphoreType.DMA((n_buf,))],
        name="sc_gather",
    )
    def body(dest_hbm, ys_hbm, yk_hbm, idx_v, rows_v, sems):
        @pl.loop(0, steps)
        def _(j):
            base = _sc_token_base(steps, j)
            pltpu.sync_copy(dest_hbm.at[:, pl.ds(token0 + base, SC_TOKENS)], idx_v)

            def gather(u):
                k, h = units[u]
                slot = u % n_buf
                return pltpu.make_async_copy(ys_hbm.at[idx_v.at[k, pl.ds(h * half, half)]], rows_v.at[slot],
                                             sems.at[slot])

            gather(0).start()
            gather(1).start()
            for u, (k, h) in enumerate(units):
                gather(u).wait()
                pltpu.sync_copy(rows_v.at[u % n_buf], yk_hbm.at[k, pl.ds(base + h * half, half)])
                if u + 2 < len(units):
                    gather(u + 2).start()

    return body(dest_kt, ys)


def _experts_kernel(be_ref, valid_ref, nact_ref, xs_ref, wg_ref, wu_ref, wd_ref, ys_ref, wg_bf, wu_bf, wd_bf):
    i = pl.program_id(0)
    prev = be_ref[jnp.maximum(i - 1, 0)]

    @pl.when((i == 0) | (be_ref[i] != prev))
    def _():
        wg_bf[...] = wg_ref[0].astype(BF16)
        wu_bf[...] = wu_ref[0].astype(BF16)
        wd_bf[...] = wd_ref[0].astype(BF16)

    valid = jnp.where(i < nact_ref[0], valid_ref[i], 0)

    def run(rows):
        row = lax.broadcasted_iota(jnp.int32, (rows, xs_ref.shape[1]), 0)
        words = jnp.where(row < valid, xs_ref[0:rows, :], jnp.uint32(0))
        xb = _unpack_words(words).astype(BF16)
        hid = _silu(_dot(xb, wg_bf[...])) * _dot(xb, wu_bf[...])
        ys_ref[0:rows, :] = _pack_words(_dot(hid.astype(BF16), wd_bf[...]))
        if rows < TM_EXP:
            ys_ref[rows:, :] = jnp.zeros((TM_EXP - rows, ys_ref.shape[1]), U32)

    rows, lower = TM_EXP, TM_EXP // 2
    while rows >= EXP_MIN_ROWS:
        lo = lower if rows > EXP_MIN_ROWS else 0
        pl.when((valid > lo) & (valid <= rows))(functools.partial(run, rows))
        rows, lower = rows // 2, lower // 2

    @pl.when(valid == 0)
    def _():
        ys_ref[...] = jnp.zeros_like(ys_ref)


def _experts(block_e, block_valid, n_active, xs, w_gate, w_up, w_down):
    n_rows, half = xs.shape
    d = w_gate.shape[1]
    nb = n_rows // TM_EXP
    grid_spec = pltpu.PrefetchScalarGridSpec(
        num_scalar_prefetch=3,
        grid=(nb,),
        in_specs=[pl.BlockSpec((TM_EXP, half), lambda i, be, bv, na: (jnp.minimum(i, na[0] - 1), 0)),
                  pl.BlockSpec((1, d, D_EXPERT), lambda i, be, bv, na: (be[i], 0, 0)),
                  pl.BlockSpec((1, d, D_EXPERT), lambda i, be, bv, na: (be[i], 0, 0)),
                  pl.BlockSpec((1, D_EXPERT, d), lambda i, be, bv, na: (be[i], 0, 0))],
        out_specs=pl.BlockSpec((TM_EXP, half), lambda i, be, bv, na: (i, 0)),
        scratch_shapes=[pltpu.VMEM((d, D_EXPERT), BF16), pltpu.VMEM((d, D_EXPERT), BF16),
                        pltpu.VMEM((D_EXPERT, d), BF16)],
    )
    return pl.pallas_call(
        _experts_kernel,
        out_shape=jax.ShapeDtypeStruct((n_rows, half), U32),
        grid_spec=grid_spec,
        compiler_params=pltpu.CompilerParams(vmem_limit_bytes=VMEM_LIMIT),
        name="experts",
    )(block_e, block_valid, n_active, xs, w_gate, w_up, w_down)


def _combine_kernel(yk_ref, x1_ref, h2_ref, gate_ref, mod_ref, wsg_ref, wsu_ref, wsd_ref, fg_ref, *out_refs):
    out_ref = out_refs[-1]
    hb = _unpack_words(h2_ref[...]).astype(BF16)
    hid = _silu(_dot(hb, wsg_ref[...])) * _dot(hb, wsu_ref[...])
    ffn = _dot(hid.astype(BF16), wsd_ref[...])
    gate = gate_ref[...]
    for k in range(TOP_K):
        ffn = ffn + gate[:, k:k + 1] * _unpack_words(yk_ref[k])
    x2 = x1_ref[...] + mod_ref[0, 5:6, :] * ffn
    ms = jnp.mean(x2 * x2, axis=-1, keepdims=True)
    out_ref[...] = x2 * lax.rsqrt(ms + EPS) * fg_ref[...]


def _combine(yk, token0, out_token0, n_out_tokens, prev_out, x1, h2p, gate_tk, mod, wsg_bf, wsu_bf, wsd_bf, final_g,
             seq_len):
    t, d = n_out_tokens, x1.shape[1]
    tiles_per_seq = seq_len // TF
    tile0 = token0 // TF
    out_tile0 = out_token0 // TF
    in_tok = pl.BlockSpec((TF, d), lambda i: (tile0 + i, 0))
    tok = pl.BlockSpec((TF, d), lambda i: (out_tile0 + i, 0))
    in_specs = [pl.BlockSpec((TOP_K, TF, d // 2), lambda i: (0, i, 0)),
                in_tok, pl.BlockSpec((TF, d // 2), lambda i: (tile0 + i, 0)),
                pl.BlockSpec((TF, TOP_K), lambda i: (tile0 + i, 0)),
                pl.BlockSpec((1, 6, d), lambda i: ((out_tile0 + i) // tiles_per_seq, 0, 0)),
                pl.BlockSpec((d, D_SHARED), lambda i: (0, 0)),
                pl.BlockSpec((d, D_SHARED), lambda i: (0, 0)),
                pl.BlockSpec((D_SHARED, d), lambda i: (0, 0)),
                pl.BlockSpec((1, d), lambda i: (0, 0))]
    args = [yk, x1, h2p, gate_tk, mod, wsg_bf, wsu_bf, wsd_bf, final_g]
    aliases = {}
    if prev_out is not None:
        in_specs.append(pl.BlockSpec(memory_space=pl.ANY))
        args.append(prev_out)
        aliases = {len(args) - 1: 0}
    return pl.pallas_call(
        _combine_kernel,
        out_shape=jax.ShapeDtypeStruct((t, d), F32),
        grid=(yk.shape[1] // TF,),
        in_specs=in_specs,
        out_specs=tok,
        input_output_aliases=aliases,
        compiler_params=pltpu.CompilerParams(vmem_limit_bytes=VMEM_LIMIT),
        name="combine",
    )(*args)


def _rope_tables(n_tokens):
    n_rows = n_tokens // GRID_W
    n_freq = HEAD_DIM // 4
    inv_freq = ROPE_THETA ** (-jnp.arange(n_freq, dtype=F32) / n_freq)
    ang_r = jnp.arange(n_rows).astype(F32)[:, None] * inv_freq[None, :]
    ang_c = jnp.arange(GRID_W).astype(F32)[:, None] * inv_freq[None, :]

    def per_token(row_part, col_part):
        rows = jnp.broadcast_to(row_part[:, None, :], (n_rows, GRID_W, n_freq))
        cols = jnp.broadcast_to(col_part[None, :, :], (n_rows, GRID_W, n_freq))
        return rows.reshape(n_tokens, n_freq), cols.reshape(n_tokens, n_freq)

    cos_r, cos_c = per_token(jnp.cos(ang_r), jnp.cos(ang_c))
    sin_r, sin_c = per_token(jnp.sin(ang_r), jnp.sin(ang_c))
    cos = jnp.concatenate([cos_r, cos_r, cos_c, cos_c], axis=1)
    sin = jnp.concatenate([-sin_r, sin_r, -sin_c, sin_c], axis=1)
    reps = LANES // HEAD_DIM
    return jnp.tile(cos, (1, reps)), jnp.tile(sin, (1, reps))


def _pool_bands():
    i = jnp.arange(QB)[:, None]
    r = jnp.arange(POOL_SLAB)[None, :]
    return jnp.stack([((r >= i + POOL_OFF - w // 2) & (r < i + POOL_OFF + w // 2)).astype(BF16)
                      for w in POOL_WINDOWS])


def kernel(x, c, ctx, c_ctx, w_ada, b_ada, norm1_g, norm2_g, w_in, attn_sink, pool_w, pool_scale, w_out,
           w_router, router_bias, w_gate, w_up, w_down, ws_gate, ws_up, ws_down, final_g):
    b, s, d = x.shape
    t = b * s
    assert w_ada.shape[0] == 1 and d == D_MODEL and s % TQ == 0 and b + 1 <= 8

    c8 = jnp.zeros((8, d), F32).at[:b].set(c).at[b].set(c_ctx)
    mod = _ada(c8, w_ada[0], b_ada[0]).reshape(8, 6, d)
    g1 = norm1_g[0].reshape(1, d)
    g2 = norm2_g[0].reshape(1, d)
    w_in_bf = w_in[0].astype(BF16)
    cos_t, sin_t = _rope_tables(s)

    q, k4, v4, p = _inproj(x, mod, g1, w_in_bf, cos_t, sin_t)
    kc4, vc4 = _ctxproj(ctx, mod[b:b + 1], g1, w_in_bf[:, ATTN_WIDTH:ATTN_WIDTH + 2 * KV_WIDTH])

    wr_t = w_router[0].T
    wr_hi = wr_t.astype(BF16)
    wr_lo = (wr_t - wr_hi.astype(F32)).astype(BF16)
    attn_consts = (_pool_bands(), pool_w[0].astype(BF16), pool_scale[0].reshape(1, POOL_WIDTH),
                   w_out[0].astype(BF16), g2, wr_hi, wr_lo)
    tri = jnp.triu(jnp.ones((TR, TR), BF16), k=1)
    shared_w = (ws_gate[0].astype(BF16), ws_up[0].astype(BF16), ws_down[0].astype(BF16))

    nb = b // TOKEN_GROUPS
    tg = nb * s
    assert b % TOKEN_GROUPS == 0 and tg % (COMBINE_CHUNKS * SC_WORKERS * SC_TOKENS) == 0
    n_rows = -(-(tg * TOP_K + N_EXPERTS * (TM_EXP - 1)) // TM_EXP) * TM_EXP
    groups = []
    for g in range(TOKEN_GROUPS):
        x1, h2p, lg_t = _attn(g * nb, nb, attn_sink[0], q, k4, v4, kc4, vc4, p, x, mod, *attn_consts)
        idx_kt, gate_kt, rank_kt, counts = _route(lg_t, router_bias[0].reshape(N_EXPERTS, 1), tri)
        pad_start, block_e, block_valid, n_active = _plan_blocks(counts[:, 0].astype(jnp.int32), n_rows // TM_EXP)
        dest_kt = _dest_rows(pad_start, idx_kt, rank_kt)
        xs = _sc_scatter(dest_kt, h2p, n_rows)
        groups.append((x1.reshape(tg, d), h2p, gate_kt.T, dest_kt, xs, block_e, block_valid, n_active))

    out = None
    chunk = tg // COMBINE_CHUNKS
    for g, (x1, h2p, gate_tk, dest_kt, xs, block_e, block_valid, n_active) in enumerate(groups):
        ys = _experts(block_e, block_valid, n_active, xs, w_gate[0], w_up[0], w_down[0])
        for token0 in range(0, tg, chunk):
            out = _combine(_sc_gather(dest_kt, ys, token0, chunk), token0, g * tg + token0, t, out, x1, h2p,
                           gate_tk, mod, *shared_w, final_g.reshape(1, d), s)
    return out.reshape(b, s, d)
```

```python
import functools

import jax
import jax.numpy as jnp
from jax import lax
from jax.experimental import pallas as pl
from jax.experimental.pallas import tpu as pltpu
from jax.experimental.pallas import tpu_sc as plsc

F32 = jnp.float32
BF16 = jnp.bfloat16

D_MODEL = 1024
GRID_W = 64
N_HEADS = 8
N_KV_HEADS = 2
HEAD_DIM = 64
ATTN_WIDTH = N_HEADS * HEAD_DIM
KV_WIDTH = N_KV_HEADS * HEAD_DIM
WINDOW = 128
ROPE_THETA = 10000.0
POOL_WINDOWS = (2, 4, 8, 16)
POOL_WIDTH = D_MODEL - ATTN_WIDTH
POOL_GROUP_DIM = POOL_WIDTH // len(POOL_WINDOWS)
IN_COLS = ATTN_WIDTH + 2 * KV_WIDTH + POOL_WIDTH
N_EXPERTS = 64
TOP_K = 8
N_EXPERT_GROUPS = 8
EXPERTS_PER_GROUP = N_EXPERTS // N_EXPERT_GROUPS
TOPK_GROUPS = 4
D_EXPERT = 256
D_SHARED = 256
ROUTED_SCALE = 2.5
EPS = 1e-6
LOG2E = 1.4426950408889634

LANES = 128
U32 = jnp.uint32
VMEM_LIMIT = 48 * 1024 * 1024

TM_PROJ = 1024
TQ = 1024
QB = 128
POOL_SLAB = 256
POOL_OFF = 64
TR = 512
TM_EXP = 1024
EXP_MIN_ROWS = 256
EXP_SLOTS = 3
TF = 256
TOKEN_GROUPS = 2
COMBINE_CHUNKS = 2
SC_CORES = 2
SC_WORKERS = 32
SC_TOKENS = 128


def _silu(x):
    return x * (1.0 / (1.0 + jnp.exp(-x)))


def _split_bf16(x):
    hi = x.astype(BF16)
    lo = (x - hi.astype(F32)).astype(BF16)
    return hi, lo


def _dot(a, b):
    return jnp.dot(a, b, preferred_element_type=F32)


def _pack_words(val):
    half = val.shape[1] // 2
    lo = lax.bitcast_convert_type(val[:, :half].astype(BF16).astype(F32), U32)
    hi = lax.bitcast_convert_type(val[:, half:].astype(BF16).astype(F32), U32)
    return lax.shift_right_logical(lo, jnp.uint32(16)) | hi


def _unpack_words(words):
    lo = lax.bitcast_convert_type(lax.shift_left(words, jnp.uint32(16)), F32)
    hi = lax.bitcast_convert_type(words & jnp.uint32(0xFFFF0000), F32)
    return jnp.concatenate([lo, hi], axis=1)


def _dot_nt(a, b):
    return lax.dot_general(a, b, (((1,), (1,)), ((), ())), preferred_element_type=F32)


def _ada_kernel(c_ref, w_ref, b_ref, o_ref):
    a_hi, a_lo = _split_bf16(_silu(c_ref[...]))
    w_hi, w_lo = _split_bf16(w_ref[...])
    o_ref[...] = _dot(a_hi, w_hi) + _dot(a_lo, w_hi) + _dot(a_hi, w_lo) + b_ref[...]


def _ada(c8, w_ada, b_ada):
    d = c8.shape[1]
    n = w_ada.shape[1]
    tn = 512
    return pl.pallas_call(
        _ada_kernel,
        out_shape=jax.ShapeDtypeStruct((8, n), F32),
        grid=(n // tn,),
        in_specs=[pl.BlockSpec((8, d), lambda j: (0, 0)),
                  pl.BlockSpec((d, tn), lambda j: (0, j)),
                  pl.BlockSpec((1, tn), lambda j: (0, j))],
        out_specs=pl.BlockSpec((8, tn), lambda j: (0, j)),
        compiler_params=pltpu.CompilerParams(vmem_limit_bytes=VMEM_LIMIT),
        name="ada",
    )(c8, w_ada, b_ada.reshape(1, n))


def _norm_mod(x, g, shift, scale):
    ms = jnp.mean(x * x, axis=-1, keepdims=True)
    return (x * lax.rsqrt(ms + EPS) * g) * (1.0 + scale) + shift


def _lane_variants(t):
    lane = lax.broadcasted_iota(jnp.int32, t.shape, 1)
    lo = lane < HEAD_DIM
    tr = pltpu.roll(t, HEAD_DIM, 1)
    zero = jnp.zeros_like(t)
    return (jnp.where(lo, t, zero), jnp.where(lo, zero, tr),
            jnp.where(lo, tr, zero), jnp.where(lo, zero, t))


def _store_variants(ref, t):
    for i, var in enumerate(_lane_variants(t)):
        ref[0, :, i * LANES:(i + 1) * LANES] = var.astype(BF16)


def _inproj_kernel(x_ref, mod_ref, g_ref, w_ref, cos_ref, sin_ref, q_ref, k_ref, v_ref, p_ref):
    h = _norm_mod(x_ref[0], g_ref[...], mod_ref[0, 0:1, :], mod_ref[0, 1:2, :])
    z = _dot(h.astype(BF16), w_ref[...])
    cos = cos_ref[...]
    sin = sin_ref[...]
    lane = lax.broadcasted_iota(jnp.int32, cos.shape, 1)
    first_half = (lane & 16) == 0

    def rope(zc):
        partner = jnp.where(first_half, pltpu.roll(zc, LANES - 16, 1), pltpu.roll(zc, 16, 1))
        return zc * cos + partner * sin

    scale = HEAD_DIM ** -0.5 * LOG2E
    for c in range(ATTN_WIDTH // LANES):
        q_ref[0, :, c * LANES:(c + 1) * LANES] = (rope(z[:, c * LANES:(c + 1) * LANES]) * scale).astype(BF16)
    _store_variants(k_ref, rope(z[:, ATTN_WIDTH:ATTN_WIDTH + KV_WIDTH]))
    _store_variants(v_ref, z[:, ATTN_WIDTH + KV_WIDTH:ATTN_WIDTH + 2 * KV_WIDTH])
    p_ref[0] = z[:, ATTN_WIDTH + 2 * KV_WIDTH:]


def _inproj(x, mod, g1, w_in_bf, cos_t, sin_t):
    b, s, d = x.shape
    tm = TM_PROJ
    return pl.pallas_call(
        _inproj_kernel,
        out_shape=(jax.ShapeDtypeStruct((b, s, ATTN_WIDTH), BF16),
                   jax.ShapeDtypeStruct((b, s, 4 * LANES), BF16),
                   jax.ShapeDtypeStruct((b, s, 4 * LANES), BF16),
                   jax.ShapeDtypeStruct((b, s, POOL_WIDTH), F32)),
        grid=(s // tm, b),
        in_specs=[pl.BlockSpec((1, tm, d), lambda n, bi: (bi, n, 0)),
                  pl.BlockSpec((1, 6, d), lambda n, bi: (bi, 0, 0)),
                  pl.BlockSpec((1, d), lambda n, bi: (0, 0)),
                  pl.BlockSpec((d, IN_COLS), lambda n, bi: (0, 0)),
                  pl.BlockSpec((tm, LANES), lambda n, bi: (n, 0)),
                  pl.BlockSpec((tm, LANES), lambda n, bi: (n, 0))],
        out_specs=(pl.BlockSpec((1, tm, ATTN_WIDTH), lambda n, bi: (bi, n, 0)),
                   pl.BlockSpec((1, tm, 4 * LANES), lambda n, bi: (bi, n, 0)),
                   pl.BlockSpec((1, tm, 4 * LANES), lambda n, bi: (bi, n, 0)),
                   pl.BlockSpec((1, tm, POOL_WIDTH), lambda n, bi: (bi, n, 0))),
        compiler_params=pltpu.CompilerParams(vmem_limit_bytes=VMEM_LIMIT),
        name="inproj",
    )(x, mod, g1, w_in_bf, cos_t, sin_t)


def _ctxproj_kernel(x_ref, mod_ref, g_ref, w_ref, k_ref, v_ref):
    h = _norm_mod(x_ref[0], g_ref[...], mod_ref[0, 0:1, :], mod_ref[0, 1:2, :])
    z = _dot(h.astype(BF16), w_ref[...])
    _store_variants(k_ref, z[:, :KV_WIDTH])
    _store_variants(v_ref, z[:, KV_WIDTH:])


def _ctxproj(ctx, mod_c, g1, w_kv_bf):
    b, c, d = ctx.shape
    return pl.pallas_call(
        _ctxproj_kernel,
        out_shape=(jax.ShapeDtypeStruct((b, c, 4 * LANES), BF16),
                   jax.ShapeDtypeStruct((b, c, 4 * LANES), BF16)),
        grid=(b,),
        in_specs=[pl.BlockSpec((1, c, d), lambda bi: (bi, 0, 0)),
                  pl.BlockSpec((1, 6, d), lambda bi: (0, 0, 0)),
                  pl.BlockSpec((1, d), lambda bi: (0, 0)),
                  pl.BlockSpec((d, 2 * KV_WIDTH), lambda bi: (0, 0))],
        out_specs=(pl.BlockSpec((1, c, 4 * LANES), lambda bi: (bi, 0, 0)),
                   pl.BlockSpec((1, c, 4 * LANES), lambda bi: (bi, 0, 0))),
        compiler_params=pltpu.CompilerParams(vmem_limit_bytes=VMEM_LIMIT),
        name="ctxproj",
    )(ctx, mod_c, g1, w_kv_bf)


def _fold(op, tiles):
    while len(tiles) > 1:
        tiles = [op(tiles[i], tiles[i + 1]) if i + 1 < len(tiles) else tiles[i] for i in range(0, len(tiles), 2)]
    return tiles[0]


def _stack_variants(t4, kv):
    return jnp.concatenate([t4[:, (2 * kv) * LANES:(2 * kv + 1) * LANES],
                            t4[:, (2 * kv + 1) * LANES:(2 * kv + 2) * LANES]], axis=0)


def _attn_kernel(seq_len, sink_ref, q_ref, k_ref, kp_ref, kn_ref, v_ref, vp_ref, vn_ref, kc_ref, vc_ref,
                 p_ref, pp_ref, pn_ref, x_ref, mod_ref, band_ref, poolw_ref, pscale_ref, wout_ref,
                 g2_ref, wrh_ref, wrl_ref, x1_ref, h2_ref, lg_ref, kwin, vwin, pext, mix, s_scr, p_scr, m_scr):
    n = pl.program_id(1)
    n_last = pl.num_programs(1) - 1

    kwin[0:QB, :] = kp_ref[0]
    kwin[QB:QB + TQ, :] = k_ref[0]
    kwin[QB + TQ:, :] = kn_ref[0]
    vwin[0:QB, :] = vp_ref[0]
    vwin[QB:QB + TQ, :] = v_ref[0]
    vwin[QB + TQ:, :] = vn_ref[0]

    pext[0:QB - 8, :] = jnp.zeros((QB - 8, POOL_WIDTH), F32)
    pext[QB - 8:QB, :] = jnp.where(n > 0, pp_ref[0], 0.0)
    pext[QB:QB + TQ, :] = p_ref[0]
    pext[QB + TQ:QB + TQ + 8, :] = jnp.where(n < n_last, pn_ref[0], 0.0)
    pext[QB + TQ + 8:, :] = jnp.zeros((QB - 8, POOL_WIDTH), F32)

    row = lax.broadcasted_iota(jnp.int32, (QB, 3 * QB), 0)
    col = lax.broadcasted_iota(jnp.int32, (QB, 3 * QB), 1)
    in_band = (col >= row) & (col <= row + 2 * WINDOW)
    tok = lax.broadcasted_iota(jnp.int32, (QB, 1), 0)
    kc = kc_ref[0]
    kc_rows = [_stack_variants(kc, kv) for kv in range(N_KV_HEADS)]
    vc_rows = [_stack_variants(vc_ref[0], kv) for kv in range(N_KV_HEADS)]

    def sub_block(j, carry):
        r0 = pl.multiple_of(j * QB, QB)
        qj = q_ref[0, pl.ds(r0, QB), :]
        kw = kwin[pl.ds(r0, 3 * QB), :]
        vw = vwin[pl.ds(r0, 3 * QB), :]
        kpos = col + (n * TQ + j * QB - QB)
        ok = in_band & (kpos >= 0) & (kpos < seq_len)
        bias = jnp.where(ok, 0.0, -jnp.inf)
        n_loc, n_ctx = 3 * QB, kc.shape[0]
        bias2 = jnp.concatenate([bias, bias], axis=1)
        k_rows = [_stack_variants(kw, kv) for kv in range(N_KV_HEADS)]
        v_rows = [_stack_variants(vw, kv) for kv in range(N_KV_HEADS)]
        group = N_HEADS // N_KV_HEADS

        def head_tiles(head):
            loc0 = (head % 2) * n_loc
            ctx0 = 2 * n_loc + (head % 2) * n_ctx
            return ([loc0 + i * LANES for i in range(n_loc // LANES)]
                    + [ctx0 + i * LANES for i in range(n_ctx // LANES)])

        for c in range(N_HEADS // 2):
            qc = qj[:, c * LANES:(c + 1) * LANES]
            s_scr[c, :, 0:2 * n_loc] = _dot_nt(qc, k_rows[2 * c // group]) + bias2
            s_scr[c, :, 2 * n_loc:] = _dot_nt(qc, kc_rows[2 * c // group])
        for head in range(N_HEADS):
            tiles = [s_scr[head // 2, :, st:st + LANES] for st in head_tiles(head)]
            row_max = jnp.max(_fold(jnp.maximum, tiles), axis=1, keepdims=True)
            m_scr[head] = jnp.broadcast_to(jnp.maximum(row_max, sink_ref[head] * LOG2E), (QB, LANES))
        for head in range(N_HEADS):
            m = m_scr[head]
            acc = None
            for st in head_tiles(head):
                p = jnp.exp2(s_scr[head // 2, :, st:st + LANES] - m)
                p_scr[head // 2, :, st:st + LANES] = p.astype(BF16)
                acc = p if acc is None else acc + p
            denom = (jnp.broadcast_to(jnp.sum(acc, axis=1, keepdims=True), (QB, LANES))
                     + jnp.exp2(sink_ref[head] * LOG2E - m))
            m_scr[head] = 1.0 / denom
        lane = lax.broadcasted_iota(jnp.int32, (QB, LANES), 1)
        for c in range(N_HEADS // 2):
            o = (_dot(p_scr[c, :, 0:2 * n_loc], v_rows[2 * c // group])
                 + _dot(p_scr[c, :, 2 * n_loc:], vc_rows[2 * c // group]))
            o = o * jnp.where(lane < HEAD_DIM, m_scr[2 * c], m_scr[2 * c + 1])
            mix[pl.ds(r0, QB), c * LANES:(c + 1) * LANES] = o.astype(BF16)

        slab = pext[pl.ds(pl.multiple_of(r0 + POOL_OFF, 8), POOL_SLAB), :]
        tpos = tok + (n * TQ + j * QB)
        for g, w in enumerate(POOL_WINDOWS):
            sg = slab[:, g * LANES:(g + 1) * LANES]
            hi, lo = _split_bf16(sg)
            band = band_ref[g]
            wsum = _dot(band, hi) + _dot(band, lo)
            cnt = (jnp.minimum(tpos - w // 2 + w, seq_len) - jnp.maximum(tpos - w // 2, 0)).astype(F32)
            dlt = wsum / cnt - sg[POOL_OFF:POOL_OFF + QB, :]
            y = _dot(dlt.astype(BF16), poolw_ref[g]) * pscale_ref[:, g * LANES:(g + 1) * LANES]
            mix[pl.ds(r0, QB), ATTN_WIDTH + g * LANES:ATTN_WIDTH + (g + 1) * LANES] = y.astype(BF16)
        return carry

    lax.fori_loop(0, TQ // QB, sub_block, 0)

    proj = _dot(mix[...], wout_ref[...])
    x1 = x_ref[0] + mod_ref[0, 2:3, :] * proj
    x1_ref[0] = x1
    h2 = _norm_mod(x1, g2_ref[...], mod_ref[0, 3:4, :], mod_ref[0, 4:5, :])
    h2_ref[...] = _pack_words(h2)
    h_hi, h_lo = _split_bf16(h2)
    wrh = wrh_ref[...]
    lg_ref[...] = _dot_nt(wrh, h_hi) + _dot_nt(wrh, h_lo) + _dot_nt(wrl_ref[...], h_hi)


def _attn(b0, b, sink, q, k4, v4, kc4, vc4, p, x, mod, band, poolw_bf, pscale, wout_bf, g2, wr_hi, wr_lo):
    _, s, d = x.shape
    c = kc4.shape[1]
    nt = s // TQ
    hb = TQ // QB
    pb = TQ // 8
    kv_main = pl.BlockSpec((1, TQ, 4 * LANES), lambda bi, n: (b0 + bi, n, 0))
    kv_prev = pl.BlockSpec((1, QB, 4 * LANES), lambda bi, n: (b0 + bi, jnp.maximum(n * hb - 1, 0), 0))
    kv_next = pl.BlockSpec((1, QB, 4 * LANES), lambda bi, n: (b0 + bi, jnp.minimum((n + 1) * hb, s // QB - 1), 0))
    const2 = lambda bi, n: (0, 0)
    const3 = lambda bi, n: (0, 0, 0)
    return pl.pallas_call(
        functools.partial(_attn_kernel, s),
        out_shape=(jax.ShapeDtypeStruct((b, s, d), F32),
                   jax.ShapeDtypeStruct((b * s, d // 2), U32),
                   jax.ShapeDtypeStruct((N_EXPERTS, b * s), F32)),
        grid=(b, nt),
        in_specs=[pl.BlockSpec(memory_space=pltpu.SMEM),
                  pl.BlockSpec((1, TQ, ATTN_WIDTH), lambda bi, n: (b0 + bi, n, 0)),
                  kv_main, kv_prev, kv_next, kv_main, kv_prev, kv_next,
                  pl.BlockSpec((1, c, 4 * LANES), lambda bi, n: (b0 + bi, 0, 0)),
                  pl.BlockSpec((1, c, 4 * LANES), lambda bi, n: (b0 + bi, 0, 0)),
                  pl.BlockSpec((1, TQ, POOL_WIDTH), lambda bi, n: (b0 + bi, n, 0)),
                  pl.BlockSpec((1, 8, POOL_WIDTH), lambda bi, n: (b0 + bi, jnp.maximum(n * pb - 1, 0), 0)),
                  pl.BlockSpec((1, 8, POOL_WIDTH),
                               lambda bi, n: (b0 + bi, jnp.minimum((n + 1) * pb, s // 8 - 1), 0)),
                  pl.BlockSpec((1, TQ, d), lambda bi, n: (b0 + bi, n, 0)),
                  pl.BlockSpec((1, 6, d), lambda bi, n: (b0 + bi, 0, 0)),
                  pl.BlockSpec((len(POOL_WINDOWS), QB, POOL_SLAB), const3),
                  pl.BlockSpec((len(POOL_WINDOWS), POOL_GROUP_DIM, POOL_GROUP_DIM), const3),
                  pl.BlockSpec((1, POOL_WIDTH), const2),
                  pl.BlockSpec((d, d), const2),
                  pl.BlockSpec((1, d), const2),
                  pl.BlockSpec((N_EXPERTS, d), const2),
                  pl.BlockSpec((N_EXPERTS, d), const2)],
        out_specs=(pl.BlockSpec((1, TQ, d), lambda bi, n: (bi, n, 0)),
                   pl.BlockSpec((TQ, d // 2), lambda bi, n: (bi * nt + n, 0)),
                   pl.BlockSpec((N_EXPERTS, TQ), lambda bi, n: (0, bi * nt + n))),
        scratch_shapes=[pltpu.VMEM((TQ + 2 * QB, 4 * LANES), BF16),
                        pltpu.VMEM((TQ + 2 * QB, 4 * LANES), BF16),
                        pltpu.VMEM((TQ + 2 * QB, POOL_WIDTH), F32),
                        pltpu.VMEM((TQ, d), BF16),
                        pltpu.VMEM((N_HEADS // 2, QB, 2 * (3 * QB + c)), F32),
                        pltpu.VMEM((N_HEADS // 2, QB, 2 * (3 * QB + c)), BF16),
                        pltpu.VMEM((N_HEADS, QB, LANES), F32)],
        compiler_params=pltpu.CompilerParams(vmem_limit_bytes=VMEM_LIMIT),
        name="attn",
    )(sink, q, k4, k4, k4, v4, v4, v4, kc4, vc4, p, p, p, x, mod, band, poolw_bf, pscale, wout_bf,
      g2, wr_hi, wr_lo)


def _first_argmax_rows(v, row_iota, n_rows):
    m = jnp.max(v, axis=0, keepdims=True)
    idx = jnp.min(jnp.where(v == m, row_iota, n_rows), axis=0, keepdims=True)
    return m, idx


def _route_kernel(lg_ref, bias_ref, tri_ref, idx_ref, gate_ref, rank_ref, cnt_ref, carry):
    i = pl.program_id(0)

    @pl.when(i == 0)
    def _():
        carry[...] = jnp.zeros_like(carry)

    scores = 1.0 / (1.0 + jnp.exp(-lg_ref[...]))
    biased = scores + bias_ref[...]
    e_iota = lax.broadcasted_iota(jnp.int32, scores.shape, 0).astype(F32)
    g_iota = lax.broadcasted_iota(jnp.int32, (EXPERTS_PER_GROUP, TR), 0).astype(F32)
    neg = -jnp.inf

    grp = []
    for g in range(N_EXPERT_GROUPS):
        blk = biased[g * EXPERTS_PER_GROUP:(g + 1) * EXPERTS_PER_GROUP, :]
        m1, i1 = _first_argmax_rows(blk, g_iota, float(EXPERTS_PER_GROUP))
        m2 = jnp.max(jnp.where(g_iota == i1, neg, blk), axis=0, keepdims=True)
        grp.append(m1 + m2)
    grp = jnp.concatenate(grp, axis=0)
    gg_iota = lax.broadcasted_iota(jnp.int32, grp.shape, 0).astype(F32)
    grp_sel = jnp.zeros(grp.shape, F32)
    for _ in range(TOPK_GROUPS):
        _, gi = _first_argmax_rows(grp, gg_iota, float(N_EXPERT_GROUPS))
        hit = gg_iota == gi
        grp_sel = jnp.where(hit, 1.0, grp_sel)
        grp = jnp.where(hit, neg, grp)
    allowed = jnp.concatenate(
        [jnp.broadcast_to(grp_sel[g:g + 1, :], (EXPERTS_PER_GROUP, TR)) for g in range(N_EXPERT_GROUPS)], axis=0)
    masked = jnp.where(allowed > 0.5, biased, neg)

    idxs, gates = [], []
    onehot = jnp.zeros(scores.shape, F32)
    for _ in range(TOP_K):
        _, ei = _first_argmax_rows(masked, e_iota, float(N_EXPERTS))
        hit = e_iota == ei
        idxs.append(ei)
        gates.append(jnp.sum(jnp.where(hit, scores, 0.0), axis=0, keepdims=True))
        onehot = jnp.where(hit, 1.0, onehot)
        masked = jnp.where(hit, neg, masked)
    idx = jnp.concatenate(idxs, axis=0)
    gate = jnp.concatenate(gates, axis=0)
    gate = gate / jnp.sum(gate, axis=0, keepdims=True) * ROUTED_SCALE

    before = _dot(onehot.astype(BF16), tri_ref[...]) + carry[:, 0:1]
    ranks = [jnp.sum(jnp.where(e_iota == idxs[k], before, 0.0), axis=0, keepdims=True) for k in range(TOP_K)]
    idx_ref[...] = idx.astype(jnp.int32)
    gate_ref[...] = gate
    rank_ref[...] = jnp.concatenate(ranks, axis=0).astype(jnp.int32)
    total = carry[...] + jnp.sum(onehot, axis=1, keepdims=True)
    carry[...] = total
    cnt_ref[...] = total


def _route(lg_t, bias, tri):
    e, t = lg_t.shape
    tok = pl.BlockSpec((TOP_K, TR), lambda i: (0, i))
    return pl.pallas_call(
        _route_kernel,
        out_shape=(jax.ShapeDtypeStruct((TOP_K, t), jnp.int32),
                   jax.ShapeDtypeStruct((TOP_K, t), F32),
                   jax.ShapeDtypeStruct((TOP_K, t), jnp.int32),
                   jax.ShapeDtypeStruct((e, LANES), F32)),
        grid=(t // TR,),
        in_specs=[pl.BlockSpec((e, TR), lambda i: (0, i)),
                  pl.BlockSpec((e, 1), lambda i: (0, 0)),
                  pl.BlockSpec((TR, TR), lambda i: (0, 0))],
        out_specs=(tok, tok, tok, pl.BlockSpec((e, LANES), lambda i: (0, 0))),
        scratch_shapes=[pltpu.VMEM((e, LANES), F32)],
        compiler_params=pltpu.CompilerParams(vmem_limit_bytes=VMEM_LIMIT),
        name="route",
    )(lg_t, bias, tri)


def _plan_kernel(n_blocks, size_ref, start_ref, expert_ref, valid_ref, nact_ref):
    def per_expert(e, first_block):
        size = size_ref[e]
        n_blk = (size + TM_EXP - 1) // TM_EXP
        start_ref[e] = first_block * TM_EXP

        def per_block(j, carry):
            expert_ref[first_block + j] = e
            valid_ref[first_block + j] = jnp.minimum(size - j * TM_EXP, TM_EXP)
            return carry

        lax.fori_loop(0, n_blk, per_block, 0)
        return first_block + n_blk

    n_active = lax.fori_loop(0, N_EXPERTS, per_expert, 0)
    nact_ref[0] = n_active

    def unused(i, carry):
        expert_ref[i] = N_EXPERTS - 1
        valid_ref[i] = 0
        return carry

    lax.fori_loop(n_active, n_blocks, unused, 0)


def _plan_blocks(sizes, n_blocks):
    smem = pl.BlockSpec(memory_space=pltpu.SMEM)
    return pl.pallas_call(
        functools.partial(_plan_kernel, n_blocks),
        out_shape=(jax.ShapeDtypeStruct((N_EXPERTS,), jnp.int32),
                   jax.ShapeDtypeStruct((n_blocks,), jnp.int32),
                   jax.ShapeDtypeStruct((n_blocks,), jnp.int32),
                   jax.ShapeDtypeStruct((1,), jnp.int32)),
        in_specs=[smem],
        out_specs=(smem, smem, smem, smem),
        name="plan_blocks",
    )(sizes)


def _dest_kernel(start_ref, idx_ref, rank_ref, dest_ref):
    idx = idx_ref[...]
    dest = rank_ref[...]
    for e in range(N_EXPERTS):
        dest = dest + jnp.where(idx == e, start_ref[e], 0)
    dest_ref[...] = dest


def _dest_rows(pad_start, idx_kt, rank_kt):
    n_k, t = idx_kt.shape
    tile = 4096
    blk = pl.BlockSpec((n_k, tile), lambda i: (0, i))
    return pl.pallas_call(
        _dest_kernel,
        out_shape=jax.ShapeDtypeStruct((n_k, t), jnp.int32),
        grid=(t // tile,),
        in_specs=[pl.BlockSpec(memory_space=pltpu.SMEM), blk, blk],
        out_specs=blk,
        name="dest_rows",
    )(pad_start, idx_kt, rank_kt)


def _sc_mesh():
    return plsc.VectorSubcoreMesh(core_axis_name="c", subcore_axis_name="s")


def _sc_token_base(steps, j):
    worker = lax.axis_index("s") * SC_CORES + lax.axis_index("c")
    return (worker * steps + j) * SC_TOKENS


def _sc_scatter(dest_kt, h2p, n_rows):
    t, width = h2p.shape
    steps = t // (SC_WORKERS * SC_TOKENS)

    @functools.partial(
        pl.kernel, mesh=_sc_mesh(),
        out_type=jax.ShapeDtypeStruct((n_rows, width), U32),
        scratch_types=[pltpu.VMEM((TOP_K, SC_TOKENS), jnp.int32),
                       pltpu.VMEM((SC_TOKENS, width), U32),
                       pltpu.SemaphoreType.DMA],
        name="sc_scatter",
    )
    def body(dest_hbm, h_hbm, xs_hbm, idx_v, rows_v, sem):
        @pl.loop(0, steps)
        def _(j):
            base = _sc_token_base(steps, j)
            pltpu.sync_copy(dest_hbm.at[:, pl.ds(base, SC_TOKENS)], idx_v)
            pltpu.sync_copy(h_hbm.at[pl.ds(base, SC_TOKENS)], rows_v)
            copies = [pltpu.async_copy(rows_v, xs_hbm.at[idx_v.at[k]], sem) for k in range(TOP_K)]
            for cp in copies:
                cp.wait()

    return body(dest_kt, h2p)


def _sc_gather(dest_kt, ys, token0, n_tokens):
    n_k = dest_kt.shape[0]
    width = ys.shape[1]
    steps = n_tokens // (SC_WORKERS * SC_TOKENS)

    half = SC_TOKENS // 2
    units = [(k, h) for k in range(n_k) for h in range(2)]
    n_buf = 3

    @functools.partial(
        pl.kernel, mesh=_sc_mesh(),
        out_type=jax.ShapeDtypeStruct((n_k, n_tokens, width), U32),
        scratch_types=[pltpu.VMEM((n_k, SC_TOKENS), jnp.int32),
                       pltpu.VMEM((n_buf, half, width), U32),
                       pltpu.SemaphoreType.DMA((n_buf,))],
        name="sc_gather",
    )
    def body(dest_hbm, ys_hbm, yk_hbm, idx_v, rows_v, sems):
        @pl.loop(0, steps)
        def _(j):
            base = _sc_token_base(steps, j)
            pltpu.sync_copy(dest_hbm.at[:, pl.ds(token0 + base, SC_TOKENS)], idx_v)

            def gather(u):
                k, h = units[u]
                slot = u % n_buf
                return pltpu.make_async_copy(ys_hbm.at[idx_v.at[k, pl.ds(h * half, half)]], rows_v.at[slot],
                                             sems.at[slot])

            gather(0).start()
            gather(1).start()
            for u, (k, h) in enumerate(units):
                gather(u).wait()
                pltpu.sync_copy(rows_v.at[u % n_buf], yk_hbm.at[k, pl.ds(base + h * half, half)])
                if u + 2 < len(units):
                    gather(u + 2).start()

    return body(dest_kt, ys)


def _experts_kernel(be_ref, valid_ref, nact_ref, xs_hbm, wg_ref, wu_ref, wd_ref, ys_ref, wg_bf, wu_bf, wd_bf,
                    xbuf, sems):
    i = pl.program_id(0)
    n_active = nact_ref[0]
    slot = lax.rem(i, EXP_SLOTS)

    def fetch(j):
        src = xs_hbm.at[pl.ds(pl.multiple_of(j * TM_EXP, TM_EXP), TM_EXP), :]
        s = lax.rem(j, EXP_SLOTS)
        return pltpu.make_async_copy(src, xbuf.at[s], sems.at[s])

    @pl.when(i == 0)
    def _():
        for j in range(EXP_SLOTS - 1):
            pl.when(j < n_active)(lambda j=j: fetch(j).start())

    @pl.when(i + (EXP_SLOTS - 1) < n_active)
    def _():
        fetch(i + (EXP_SLOTS - 1)).start()

    prev = be_ref[jnp.maximum(i - 1, 0)]

    @pl.when((i == 0) | (be_ref[i] != prev))
    def _():
        wg_bf[...] = wg_ref[0].astype(BF16)
        wu_bf[...] = wu_ref[0].astype(BF16)
        wd_bf[...] = wd_ref[0].astype(BF16)

    @pl.when(i < n_active)
    def _():
        fetch(i).wait()

    valid = jnp.where(i < n_active, valid_ref[i], 0)

    def run(rows):
        row = lax.broadcasted_iota(jnp.int32, (rows, xbuf.shape[2]), 0)
        words = jnp.where(row < valid, xbuf[slot, 0:rows, :], jnp.uint32(0))
        xb = _unpack_words(words).astype(BF16)
        hid = _silu(_dot(xb, wg_bf[...])) * _dot(xb, wu_bf[...])
        ys_ref[0:rows, :] = _pack_words(_dot(hid.astype(BF16), wd_bf[...]))
        if rows < TM_EXP:
            ys_ref[rows:, :] = jnp.zeros((TM_EXP - rows, ys_ref.shape[1]), U32)

    rows, lower = TM_EXP, TM_EXP // 2
    while rows >= EXP_MIN_ROWS:
        lo = lower if rows > EXP_MIN_ROWS else 0
        pl.when((valid > lo) & (valid <= rows))(functools.partial(run, rows))
        rows, lower = rows // 2, lower // 2

    @pl.when(valid == 0)
    def _():
        ys_ref[...] = jnp.zeros_like(ys_ref)


def _experts(block_e, block_valid, n_active, xs, w_gate, w_up, w_down):
    n_rows, half = xs.shape
    d = w_gate.shape[1]
    nb = n_rows // TM_EXP
    grid_spec = pltpu.PrefetchScalarGridSpec(
        num_scalar_prefetch=3,
        grid=(nb,),
        in_specs=[pl.BlockSpec(memory_space=pl.ANY),
                  pl.BlockSpec((1, d, D_EXPERT), lambda i, be, bv, na: (be[i], 0, 0)),
                  pl.BlockSpec((1, d, D_EXPERT), lambda i, be, bv, na: (be[i], 0, 0)),
                  pl.BlockSpec((1, D_EXPERT, d), lambda i, be, bv, na: (be[i], 0, 0))],
        out_specs=pl.BlockSpec((TM_EXP, half), lambda i, be, bv, na: (i, 0)),
        scratch_shapes=[pltpu.VMEM((d, D_EXPERT), BF16), pltpu.VMEM((d, D_EXPERT), BF16),
                        pltpu.VMEM((D_EXPERT, d), BF16),
                        pltpu.VMEM((EXP_SLOTS, TM_EXP, half), U32),
                        pltpu.SemaphoreType.DMA((EXP_SLOTS,))],
    )
    return pl.pallas_call(
        _experts_kernel,
        out_shape=jax.ShapeDtypeStruct((n_rows, half), U32),
        grid_spec=grid_spec,
        compiler_params=pltpu.CompilerParams(vmem_limit_bytes=VMEM_LIMIT, dimension_semantics=("arbitrary",)),
        name="experts",
    )(block_e, block_valid, n_active, xs, w_gate, w_up, w_down)


def _combine_kernel(yk_ref, x1_ref, h2_ref, gate_ref, mod_ref, wsg_ref, wsu_ref, wsd_ref, fg_ref, *out_refs):
    out_ref = out_refs[-1]
    hb = _unpack_words(h2_ref[...]).astype(BF16)
    hid = _silu(_dot(hb, wsg_ref[...])) * _dot(hb, wsu_ref[...])
    ffn = _dot(hid.astype(BF16), wsd_ref[...])
    gate = gate_ref[...]
    for k in range(TOP_K):
        ffn = ffn + gate[:, k:k + 1] * _unpack_words(yk_ref[k])
    x2 = x1_ref[...] + mod_ref[0, 5:6, :] * ffn
    ms = jnp.mean(x2 * x2, axis=-1, keepdims=True)
    out_ref[...] = x2 * lax.rsqrt(ms + EPS) * fg_ref[...]


def _combine(yk, token0, out_token0, n_out_tokens, prev_out, x1, h2p, gate_tk, mod, wsg_bf, wsu_bf, wsd_bf, final_g,
             seq_len):
    t, d = n_out_tokens, x1.shape[1]
    tiles_per_seq = seq_len // TF
    tile0 = token0 // TF
    out_tile0 = out_token0 // TF
    in_tok = pl.BlockSpec((TF, d), lambda i: (tile0 + i, 0))
    tok = pl.BlockSpec((TF, d), lambda i: (out_tile0 + i, 0))
    in_specs = [pl.BlockSpec((TOP_K, TF, d // 2), lambda i: (0, i, 0)),
                in_tok, pl.BlockSpec((TF, d // 2), lambda i: (tile0 + i, 0)),
                pl.BlockSpec((TF, TOP_K), lambda i: (tile0 + i, 0)),
                pl.BlockSpec((1, 6, d), lambda i: ((out_tile0 + i) // tiles_per_seq, 0, 0)),
                pl.BlockSpec((d, D_SHARED), lambda i: (0, 0)),
                pl.BlockSpec((d, D_SHARED), lambda i: (0, 0)),
                pl.BlockSpec((D_SHARED, d), lambda i: (0, 0)),
                pl.BlockSpec((1, d), lambda i: (0, 0))]
    args = [yk, x1, h2p, gate_tk, mod, wsg_bf, wsu_bf, wsd_bf, final_g]
    aliases = {}
    if prev_out is not None:
        in_specs.append(pl.BlockSpec(memory_space=pl.ANY))
        args.append(prev_out)
        aliases = {len(args) - 1: 0}
    return pl.pallas_call(
        _combine_kernel,
        out_shape=jax.ShapeDtypeStruct((t, d), F32),
        grid=(yk.shape[1] // TF,),
        in_specs=in_specs,
        out_specs=tok,
        input_output_aliases=aliases,
        compiler_params=pltpu.CompilerParams(vmem_limit_bytes=VMEM_LIMIT),
        name="combine",
    )(*args)


def _rope_tables(n_tokens):
    n_rows = n_tokens // GRID_W
    n_freq = HEAD_DIM // 4
    inv_freq = ROPE_THETA ** (-jnp.arange(n_freq, dtype=F32) / n_freq)
    ang_r = jnp.arange(n_rows).astype(F32)[:, None] * inv_freq[None, :]
    ang_c = jnp.arange(GRID_W).astype(F32)[:, None] * inv_freq[None, :]

    def per_token(row_part, col_part):
        rows = jnp.broadcast_to(row_part[:, None, :], (n_rows, GRID_W, n_freq))
        cols = jnp.broadcast_to(col_part[None, :, :], (n_rows, GRID_W, n_freq))
        return rows.reshape(n_tokens, n_freq), cols.reshape(n_tokens, n_freq)

    cos_r, cos_c = per_token(jnp.cos(ang_r), jnp.cos(ang_c))
    sin_r, sin_c = per_token(jnp.sin(ang_r), jnp.sin(ang_c))
    cos = jnp.concatenate([cos_r, cos_r, cos_c, cos_c], axis=1)
    sin = jnp.concatenate([-sin_r, sin_r, -sin_c, sin_c], axis=1)
    reps = LANES // HEAD_DIM
    return jnp.tile(cos, (1, reps)), jnp.tile(sin, (1, reps))


def _pool_bands():
    i = jnp.arange(QB)[:, None]
    r = jnp.arange(POOL_SLAB)[None, :]
    return jnp.stack([((r >= i + POOL_OFF - w // 2) & (r < i + POOL_OFF + w // 2)).astype(BF16)
                      for w in POOL_WINDOWS])


def kernel(x, c, ctx, c_ctx, w_ada, b_ada, norm1_g, norm2_g, w_in, attn_sink, pool_w, pool_scale, w_out,
           w_router, router_bias, w_gate, w_up, w_down, ws_gate, ws_up, ws_down, final_g):
    b, s, d = x.shape
    t = b * s
    assert w_ada.shape[0] == 1 and d == D_MODEL and s % TQ == 0 and b + 1 <= 8

    c8 = jnp.zeros((8, d), F32).at[:b].set(c).at[b].set(c_ctx)
    mod = _ada(c8, w_ada[0], b_ada[0]).reshape(8, 6, d)
    g1 = norm1_g[0].reshape(1, d)
    g2 = norm2_g[0].reshape(1, d)
    w_in_bf = w_in[0].astype(BF16)
    cos_t, sin_t = _rope_tables(s)

    q, k4, v4, p = _inproj(x, mod, g1, w_in_bf, cos_t, sin_t)
    kc4, vc4 = _ctxproj(ctx, mod[b:b + 1], g1, w_in_bf[:, ATTN_WIDTH:ATTN_WIDTH + 2 * KV_WIDTH])

    wr_t = w_router[0].T
    wr_hi = wr_t.astype(BF16)
    wr_lo = (wr_t - wr_hi.astype(F32)).astype(BF16)
    attn_consts = (_pool_bands(), pool_w[0].astype(BF16), pool_scale[0].reshape(1, POOL_WIDTH),
                   w_out[0].astype(BF16), g2, wr_hi, wr_lo)
    tri = jnp.triu(jnp.ones((TR, TR), BF16), k=1)
    shared_w = (ws_gate[0].astype(BF16), ws_up[0].astype(BF16), ws_down[0].astype(BF16))

    nb = b // TOKEN_GROUPS
    tg = nb * s
    assert b % TOKEN_GROUPS == 0 and tg % (COMBINE_CHUNKS * SC_WORKERS * SC_TOKENS) == 0
    n_rows = -(-(tg * TOP_K + N_EXPERTS * (TM_EXP - 1)) // TM_EXP) * TM_EXP
    groups = []
    for g in range(TOKEN_GROUPS):
        x1, h2p, lg_t = _attn(g * nb, nb, attn_sink[0], q, k4, v4, kc4, vc4, p, x, mod, *attn_consts)
        idx_kt, gate_kt, rank_kt, counts = _route(lg_t, router_bias[0].reshape(N_EXPERTS, 1), tri)
        pad_start, block_e, block_valid, n_active = _plan_blocks(counts[:, 0].astype(jnp.int32), n_rows // TM_EXP)
        dest_kt = _dest_rows(pad_start, idx_kt, rank_kt)
        xs = _sc_scatter(dest_kt, h2p, n_rows)
        groups.append((x1.reshape(tg, d), h2p, gate_kt.T, dest_kt, xs, block_e, block_valid, n_active))

    out = None
    chunk = tg // COMBINE_CHUNKS
    for g, (x1, h2p, gate_tk, dest_kt, xs, block_e, block_valid, n_active) in enumerate(groups):
        ys = _experts(block_e, block_valid, n_active, xs, w_gate[0], w_up[0], w_down[0])
        for token0 in range(0, tg, chunk):
            out = _combine(_sc_gather(dest_kt, ys, token0, chunk), token0, g * tg + token0, t, out, x1, h2p,
                           gate_tk, mod, *shared_w, final_g.reshape(1, d), s)
    return out.reshape(b, s, d)
```

```python
import functools

import jax
import jax.numpy as jnp
from jax import lax
from jax.experimental import pallas as pl
from jax.experimental.pallas import tpu as pltpu
from jax.experimental.pallas import tpu_sc as plsc

F32 = jnp.float32
BF16 = jnp.bfloat16

D_MODEL = 1024
GRID_W = 64
N_HEADS = 8
N_KV_HEADS = 2
HEAD_DIM = 64
ATTN_WIDTH = N_HEADS * HEAD_DIM
KV_WIDTH = N_KV_HEADS * HEAD_DIM
WINDOW = 128
ROPE_THETA = 10000.0
POOL_WINDOWS = (2, 4, 8, 16)
POOL_WIDTH = D_MODEL - ATTN_WIDTH
POOL_GROUP_DIM = POOL_WIDTH // len(POOL_WINDOWS)
IN_COLS = ATTN_WIDTH + 2 * KV_WIDTH + POOL_WIDTH
N_EXPERTS = 64
TOP_K = 8
N_EXPERT_GROUPS = 8
EXPERTS_PER_GROUP = N_EXPERTS // N_EXPERT_GROUPS
TOPK_GROUPS = 4
D_EXPERT = 256
D_SHARED = 256
ROUTED_SCALE = 2.5
EPS = 1e-6
LOG2E = 1.4426950408889634

LANES = 128
U32 = jnp.uint32
VMEM_LIMIT = 48 * 1024 * 1024

TM_PROJ = 1024
TQ = 1024
QB = 128
POOL_SLAB = 256
POOL_OFF = 64
TR = 512
TM_EXP = 1024
EXP_MIN_ROWS = 256
EXP_SLOTS = 4
TF = 256
TOKEN_GROUPS = 2
COMBINE_CHUNKS = 2
SC_CORES = 2
SC_WORKERS = 32
SC_TOKENS = 128


def _silu(x):
    return x * (1.0 / (1.0 + jnp.exp(-x)))


def _split_bf16(x):
    hi = x.astype(BF16)
    lo = (x - hi.astype(F32)).astype(BF16)
    return hi, lo


def _dot(a, b):
    return jnp.dot(a, b, preferred_element_type=F32)


def _pack_words(val):
    half = val.shape[1] // 2
    lo = lax.bitcast_convert_type(val[:, :half].astype(BF16).astype(F32), U32)
    hi = lax.bitcast_convert_type(val[:, half:].astype(BF16).astype(F32), U32)
    return lax.shift_right_logical(lo, jnp.uint32(16)) | hi


def _unpack_words(words):
    lo = lax.bitcast_convert_type(lax.shift_left(words, jnp.uint32(16)), F32)
    hi = lax.bitcast_convert_type(words & jnp.uint32(0xFFFF0000), F32)
    return jnp.concatenate([lo, hi], axis=1)


def _dot_nt(a, b):
    return lax.dot_general(a, b, (((1,), (1,)), ((), ())), preferred_element_type=F32)


def _ada_kernel(c_ref, w_ref, b_ref, o_ref):
    a_hi, a_lo = _split_bf16(_silu(c_ref[...]))
    w_hi, w_lo = _split_bf16(w_ref[...])
    o_ref[...] = _dot(a_hi, w_hi) + _dot(a_lo, w_hi) + _dot(a_hi, w_lo) + b_ref[...]


def _ada(c8, w_ada, b_ada):
    d = c8.shape[1]
    n = w_ada.shape[1]
    tn = 512
    return pl.pallas_call(
        _ada_kernel,
        out_shape=jax.ShapeDtypeStruct((8, n), F32),
        grid=(n // tn,),
        in_specs=[pl.BlockSpec((8, d), lambda j: (0, 0)),
                  pl.BlockSpec((d, tn), lambda j: (0, j)),
                  pl.BlockSpec((1, tn), lambda j: (0, j))],
        out_specs=pl.BlockSpec((8, tn), lambda j: (0, j)),
        compiler_params=pltpu.CompilerParams(vmem_limit_bytes=VMEM_LIMIT),
        name="ada",
    )(c8, w_ada, b_ada.reshape(1, n))


def _norm_mod(x, g, shift, scale):
    ms = jnp.mean(x * x, axis=-1, keepdims=True)
    return (x * lax.rsqrt(ms + EPS) * g) * (1.0 + scale) + shift


def _lane_variants(t):
    lane = lax.broadcasted_iota(jnp.int32, t.shape, 1)
    lo = lane < HEAD_DIM
    tr = pltpu.roll(t, HEAD_DIM, 1)
    zero = jnp.zeros_like(t)
    return (jnp.where(lo, t, zero), jnp.where(lo, zero, tr),
            jnp.where(lo, tr, zero), jnp.where(lo, zero, t))


def _store_variants(ref, t):
    for i, var in enumerate(_lane_variants(t)):
        ref[0, :, i * LANES:(i + 1) * LANES] = var.astype(BF16)


def _inproj_kernel(x_ref, mod_ref, g_ref, w_ref, cos_ref, sin_ref, q_ref, k_ref, v_ref, p_ref):
    h = _norm_mod(x_ref[0], g_ref[...], mod_ref[0, 0:1, :], mod_ref[0, 1:2, :])
    z = _dot(h.astype(BF16), w_ref[...])
    cos = cos_ref[...]
    sin = sin_ref[...]
    lane = lax.broadcasted_iota(jnp.int32, cos.shape, 1)
    first_half = (lane & 16) == 0

    def rope(zc):
        partner = jnp.where(first_half, pltpu.roll(zc, LANES - 16, 1), pltpu.roll(zc, 16, 1))
        return zc * cos + partner * sin

    scale = HEAD_DIM ** -0.5 * LOG2E
    for c in range(ATTN_WIDTH // LANES):
        q_ref[0, :, c * LANES:(c + 1) * LANES] = (rope(z[:, c * LANES:(c + 1) * LANES]) * scale).astype(BF16)
    _store_variants(k_ref, rope(z[:, ATTN_WIDTH:ATTN_WIDTH + KV_WIDTH]))
    _store_variants(v_ref, z[:, ATTN_WIDTH + KV_WIDTH:ATTN_WIDTH + 2 * KV_WIDTH])
    p_ref[0] = z[:, ATTN_WIDTH + 2 * KV_WIDTH:]


def _inproj(x, mod, g1, w_in_bf, cos_t, sin_t):
    b, s, d = x.shape
    tm = TM_PROJ
    return pl.pallas_call(
        _inproj_kernel,
        out_shape=(jax.ShapeDtypeStruct((b, s, ATTN_WIDTH), BF16),
                   jax.ShapeDtypeStruct((b, s, 4 * LANES), BF16),
                   jax.ShapeDtypeStruct((b, s, 4 * LANES), BF16),
                   jax.ShapeDtypeStruct((b, s, POOL_WIDTH), F32)),
        grid=(s // tm, b),
        in_specs=[pl.BlockSpec((1, tm, d), lambda n, bi: (bi, n, 0)),
                  pl.BlockSpec((1, 6, d), lambda n, bi: (bi, 0, 0)),
                  pl.BlockSpec((1, d), lambda n, bi: (0, 0)),
                  pl.BlockSpec((d, IN_COLS), lambda n, bi: (0, 0)),
                  pl.BlockSpec((tm, LANES), lambda n, bi: (n, 0)),
                  pl.BlockSpec((tm, LANES), lambda n, bi: (n, 0))],
        out_specs=(pl.BlockSpec((1, tm, ATTN_WIDTH), lambda n, bi: (bi, n, 0)),
                   pl.BlockSpec((1, tm, 4 * LANES), lambda n, bi: (bi, n, 0)),
                   pl.BlockSpec((1, tm, 4 * LANES), lambda n, bi: (bi, n, 0)),
                   pl.BlockSpec((1, tm, POOL_WIDTH), lambda n, bi: (bi, n, 0))),
        compiler_params=pltpu.CompilerParams(vmem_limit_bytes=VMEM_LIMIT),
        name="inproj",
    )(x, mod, g1, w_in_bf, cos_t, sin_t)


def _ctxproj_kernel(x_ref, mod_ref, g_ref, w_ref, k_ref, v_ref):
    h = _norm_mod(x_ref[0], g_ref[...], mod_ref[0, 0:1, :], mod_ref[0, 1:2, :])
    z = _dot(h.astype(BF16), w_ref[...])
    _store_variants(k_ref, z[:, :KV_WIDTH])
    _store_variants(v_ref, z[:, KV_WIDTH:])


def _ctxproj(ctx, mod_c, g1, w_kv_bf):
    b, c, d = ctx.shape
    return pl.pallas_call(
        _ctxproj_kernel,
        out_shape=(jax.ShapeDtypeStruct((b, c, 4 * LANES), BF16),
                   jax.ShapeDtypeStruct((b, c, 4 * LANES), BF16)),
        grid=(b,),
        in_specs=[pl.BlockSpec((1, c, d), lambda bi: (bi, 0, 0)),
                  pl.BlockSpec((1, 6, d), lambda bi: (0, 0, 0)),
                  pl.BlockSpec((1, d), lambda bi: (0, 0)),
                  pl.BlockSpec((d, 2 * KV_WIDTH), lambda bi: (0, 0))],
        out_specs=(pl.BlockSpec((1, c, 4 * LANES), lambda bi: (bi, 0, 0)),
                   pl.BlockSpec((1, c, 4 * LANES), lambda bi: (bi, 0, 0))),
        compiler_params=pltpu.CompilerParams(vmem_limit_bytes=VMEM_LIMIT),
        name="ctxproj",
    )(ctx, mod_c, g1, w_kv_bf)


def _fold(op, tiles):
    while len(tiles) > 1:
        tiles = [op(tiles[i], tiles[i + 1]) if i + 1 < len(tiles) else tiles[i] for i in range(0, len(tiles), 2)]
    return tiles[0]


def _stack_variants(t4, kv):
    return jnp.concatenate([t4[:, (2 * kv) * LANES:(2 * kv + 1) * LANES],
                            t4[:, (2 * kv + 1) * LANES:(2 * kv + 2) * LANES]], axis=0)


def _attn_kernel(seq_len, sink_ref, q_ref, k_ref, kp_ref, kn_ref, v_ref, vp_ref, vn_ref, kc_ref, vc_ref,
                 p_ref, pp_ref, pn_ref, x_ref, mod_ref, band_ref, poolw_ref, pscale_ref, wout_ref,
                 g2_ref, wrh_ref, wrl_ref, x1_ref, h2_ref, lg_ref, kwin, vwin, pext, mix, s_scr, p_scr, m_scr):
    n = pl.program_id(1)
    n_last = pl.num_programs(1) - 1

    kwin[0:QB, :] = kp_ref[0]
    kwin[QB:QB + TQ, :] = k_ref[0]
    kwin[QB + TQ:, :] = kn_ref[0]
    vwin[0:QB, :] = vp_ref[0]
    vwin[QB:QB + TQ, :] = v_ref[0]
    vwin[QB + TQ:, :] = vn_ref[0]

    pext[0:QB - 8, :] = jnp.zeros((QB - 8, POOL_WIDTH), F32)
    pext[QB - 8:QB, :] = jnp.where(n > 0, pp_ref[0], 0.0)
    pext[QB:QB + TQ, :] = p_ref[0]
    pext[QB + TQ:QB + TQ + 8, :] = jnp.where(n < n_last, pn_ref[0], 0.0)
    pext[QB + TQ + 8:, :] = jnp.zeros((QB - 8, POOL_WIDTH), F32)

    row = lax.broadcasted_iota(jnp.int32, (QB, 3 * QB), 0)
    col = lax.broadcasted_iota(jnp.int32, (QB, 3 * QB), 1)
    in_band = (col >= row) & (col <= row + 2 * WINDOW)
    tok = lax.broadcasted_iota(jnp.int32, (QB, 1), 0)
    kc = kc_ref[0]
    kc_rows = [_stack_variants(kc, kv) for kv in range(N_KV_HEADS)]
    vc_rows = [_stack_variants(vc_ref[0], kv) for kv in range(N_KV_HEADS)]

    def sub_block(j, carry):
        r0 = pl.multiple_of(j * QB, QB)
        qj = q_ref[0, pl.ds(r0, QB), :]
        kw = kwin[pl.ds(r0, 3 * QB), :]
        vw = vwin[pl.ds(r0, 3 * QB), :]
        kpos = col + (n * TQ + j * QB - QB)
        ok = in_band & (kpos >= 0) & (kpos < seq_len)
        bias = jnp.where(ok, 0.0, -jnp.inf)
        n_loc, n_ctx = 3 * QB, kc.shape[0]
        bias2 = jnp.concatenate([bias, bias], axis=1)
        k_rows = [_stack_variants(kw, kv) for kv in range(N_KV_HEADS)]
        v_rows = [_stack_variants(vw, kv) for kv in range(N_KV_HEADS)]
        group = N_HEADS // N_KV_HEADS

        def head_tiles(head):
            loc0 = (head % 2) * n_loc
            ctx0 = 2 * n_loc + (head % 2) * n_ctx
            return ([loc0 + i * LANES for i in range(n_loc // LANES)]
                    + [ctx0 + i * LANES for i in range(n_ctx // LANES)])

        for c in range(N_HEADS // 2):
            qc = qj[:, c * LANES:(c + 1) * LANES]
            s_scr[c, :, 0:2 * n_loc] = _dot_nt(qc, k_rows[2 * c // group]) + bias2
            s_scr[c, :, 2 * n_loc:] = _dot_nt(qc, kc_rows[2 * c // group])
        for head in range(N_HEADS):
            tiles = [s_scr[head // 2, :, st:st + LANES] for st in head_tiles(head)]
            row_max = jnp.max(_fold(jnp.maximum, tiles), axis=1, keepdims=True)
            m_scr[head] = jnp.broadcast_to(jnp.maximum(row_max, sink_ref[head] * LOG2E), (QB, LANES))
        for head in range(N_HEADS):
            m = m_scr[head]
            acc = None
            for st in head_tiles(head):
                p = jnp.exp2(s_scr[head // 2, :, st:st + LANES] - m)
                p_scr[head // 2, :, st:st + LANES] = p.astype(BF16)
                acc = p if acc is None else acc + p
            denom = (jnp.broadcast_to(jnp.sum(acc, axis=1, keepdims=True), (QB, LANES))
                     + jnp.exp2(sink_ref[head] * LOG2E - m))
            m_scr[head] = 1.0 / denom
        lane = lax.broadcasted_iota(jnp.int32, (QB, LANES), 1)
        for c in range(N_HEADS // 2):
            o = (_dot(p_scr[c, :, 0:2 * n_loc], v_rows[2 * c // group])
                 + _dot(p_scr[c, :, 2 * n_loc:], vc_rows[2 * c // group]))
            o = o * jnp.where(lane < HEAD_DIM, m_scr[2 * c], m_scr[2 * c + 1])
            mix[pl.ds(r0, QB), c * LANES:(c + 1) * LANES] = o.astype(BF16)

        slab = pext[pl.ds(pl.multiple_of(r0 + POOL_OFF, 8), POOL_SLAB), :]
        tpos = tok + (n * TQ + j * QB)
        for g, w in enumerate(POOL_WINDOWS):
            sg = slab[:, g * LANES:(g + 1) * LANES]
            hi, lo = _split_bf16(sg)
            band = band_ref[g]
            wsum = _dot(band, hi) + _dot(band, lo)
            cnt = (jnp.minimum(tpos - w // 2 + w, seq_len) - jnp.maximum(tpos - w // 2, 0)).astype(F32)
            dlt = wsum / cnt - sg[POOL_OFF:POOL_OFF + QB, :]
            y = _dot(dlt.astype(BF16), poolw_ref[g]) * pscale_ref[:, g * LANES:(g + 1) * LANES]
            mix[pl.ds(r0, QB), ATTN_WIDTH + g * LANES:ATTN_WIDTH + (g + 1) * LANES] = y.astype(BF16)
        return carry

    lax.fori_loop(0, TQ // QB, sub_block, 0)

    proj = _dot(mix[...], wout_ref[...])
    x1 = x_ref[0] + mod_ref[0, 2:3, :] * proj
    x1_ref[0] = x1
    h2 = _norm_mod(x1, g2_ref[...], mod_ref[0, 3:4, :], mod_ref[0, 4:5, :])
    h2_ref[...] = _pack_words(h2)
    h_hi, h_lo = _split_bf16(h2)
    wrh = wrh_ref[...]
    lg_ref[...] = _dot_nt(wrh, h_hi) + _dot_nt(wrh, h_lo) + _dot_nt(wrl_ref[...], h_hi)


def _attn(b0, b, sink, q, k4, v4, kc4, vc4, p, x, mod, band, poolw_bf, pscale, wout_bf, g2, wr_hi, wr_lo):
    _, s, d = x.shape
    c = kc4.shape[1]
    nt = s // TQ
    hb = TQ // QB
    pb = TQ // 8
    kv_main = pl.BlockSpec((1, TQ, 4 * LANES), lambda bi, n: (b0 + bi, n, 0))
    kv_prev = pl.BlockSpec((1, QB, 4 * LANES), lambda bi, n: (b0 + bi, jnp.maximum(n * hb - 1, 0), 0))
    kv_next = pl.BlockSpec((1, QB, 4 * LANES), lambda bi, n: (b0 + bi, jnp.minimum((n + 1) * hb, s // QB - 1), 0))
    const2 = lambda bi, n: (0, 0)
    const3 = lambda bi, n: (0, 0, 0)
    return pl.pallas_call(
        functools.partial(_attn_kernel, s),
        out_shape=(jax.ShapeDtypeStruct((b, s, d), F32),
                   jax.ShapeDtypeStruct((b * s, d // 2), U32),
                   jax.ShapeDtypeStruct((N_EXPERTS, b * s), F32)),
        grid=(b, nt),
        in_specs=[pl.BlockSpec(memory_space=pltpu.SMEM),
                  pl.BlockSpec((1, TQ, ATTN_WIDTH), lambda bi, n: (b0 + bi, n, 0)),
                  kv_main, kv_prev, kv_next, kv_main, kv_prev, kv_next,
                  pl.BlockSpec((1, c, 4 * LANES), lambda bi, n: (b0 + bi, 0, 0)),
                  pl.BlockSpec((1, c, 4 * LANES), lambda bi, n: (b0 + bi, 0, 0)),
                  pl.BlockSpec((1, TQ, POOL_WIDTH), lambda bi, n: (b0 + bi, n, 0)),
                  pl.BlockSpec((1, 8, POOL_WIDTH), lambda bi, n: (b0 + bi, jnp.maximum(n * pb - 1, 0), 0)),
                  pl.BlockSpec((1, 8, POOL_WIDTH),
                               lambda bi, n: (b0 + bi, jnp.minimum((n + 1) * pb, s // 8 - 1), 0)),
                  pl.BlockSpec((1, TQ, d), lambda bi, n: (b0 + bi, n, 0)),
                  pl.BlockSpec((1, 6, d), lambda bi, n: (b0 + bi, 0, 0)),
                  pl.BlockSpec((len(POOL_WINDOWS), QB, POOL_SLAB), const3),
                  pl.BlockSpec((len(POOL_WINDOWS), POOL_GROUP_DIM, POOL_GROUP_DIM), const3),
                  pl.BlockSpec((1, POOL_WIDTH), const2),
                  pl.BlockSpec((d, d), const2),
                  pl.BlockSpec((1, d), const2),
                  pl.BlockSpec((N_EXPERTS, d), const2),
                  pl.BlockSpec((N_EXPERTS, d), const2)],
        out_specs=(pl.BlockSpec((1, TQ, d), lambda bi, n: (bi, n, 0)),
                   pl.BlockSpec((TQ, d // 2), lambda bi, n: (bi * nt + n, 0)),
                   pl.BlockSpec((N_EXPERTS, TQ), lambda bi, n: (0, bi * nt + n))),
        scratch_shapes=[pltpu.VMEM((TQ + 2 * QB, 4 * LANES), BF16),
                        pltpu.VMEM((TQ + 2 * QB, 4 * LANES), BF16),
                        pltpu.VMEM((TQ + 2 * QB, POOL_WIDTH), F32),
                        pltpu.VMEM((TQ, d), BF16),
                        pltpu.VMEM((N_HEADS // 2, QB, 2 * (3 * QB + c)), F32),
                        pltpu.VMEM((N_HEADS // 2, QB, 2 * (3 * QB + c)), BF16),
                        pltpu.VMEM((N_HEADS, QB, LANES), F32)],
        compiler_params=pltpu.CompilerParams(vmem_limit_bytes=VMEM_LIMIT),
        name="attn",
    )(sink, q, k4, k4, k4, v4, v4, v4, kc4, vc4, p, p, p, x, mod, band, poolw_bf, pscale, wout_bf,
      g2, wr_hi, wr_lo)


def _first_argmax_rows(v, row_iota, n_rows):
    m = jnp.max(v, axis=0, keepdims=True)
    idx = jnp.min(jnp.where(v == m, row_iota, n_rows), axis=0, keepdims=True)
    return m, idx


def _route_kernel(lg_ref, bias_ref, tri_ref, idx_ref, gate_ref, rank_ref, cnt_ref, carry):
    i = pl.program_id(0)

    @pl.when(i == 0)
    def _():
        carry[...] = jnp.zeros_like(carry)

    scores = 1.0 / (1.0 + jnp.exp(-lg_ref[...]))
    biased = scores + bias_ref[...]
    e_iota = lax.broadcasted_iota(jnp.int32, scores.shape, 0).astype(F32)
    g_iota = lax.broadcasted_iota(jnp.int32, (EXPERTS_PER_GROUP, TR), 0).astype(F32)
    neg = -jnp.inf

    grp = []
    for g in range(N_EXPERT_GROUPS):
        blk = biased[g * EXPERTS_PER_GROUP:(g + 1) * EXPERTS_PER_GROUP, :]
        m1, i1 = _first_argmax_rows(blk, g_iota, float(EXPERTS_PER_GROUP))
        m2 = jnp.max(jnp.where(g_iota == i1, neg, blk), axis=0, keepdims=True)
        grp.append(m1 + m2)
    grp = jnp.concatenate(grp, axis=0)
    gg_iota = lax.broadcasted_iota(jnp.int32, grp.shape, 0).astype(F32)
    grp_sel = jnp.zeros(grp.shape, F32)
    for _ in range(TOPK_GROUPS):
        _, gi = _first_argmax_rows(grp, gg_iota, float(N_EXPERT_GROUPS))
        hit = gg_iota == gi
        grp_sel = jnp.where(hit, 1.0, grp_sel)
        grp = jnp.where(hit, neg, grp)
    allowed = jnp.concatenate(
        [jnp.broadcast_to(grp_sel[g:g + 1, :], (EXPERTS_PER_GROUP, TR)) for g in range(N_EXPERT_GROUPS)], axis=0)
    masked = jnp.where(allowed > 0.5, biased, neg)

    idxs, gates = [], []
    onehot = jnp.zeros(scores.shape, F32)
    for _ in range(TOP_K):
        _, ei = _first_argmax_rows(masked, e_iota, float(N_EXPERTS))
        hit = e_iota == ei
        idxs.append(ei)
        gates.append(jnp.sum(jnp.where(hit, scores, 0.0), axis=0, keepdims=True))
        onehot = jnp.where(hit, 1.0, onehot)
        masked = jnp.where(hit, neg, masked)
    idx = jnp.concatenate(idxs, axis=0)
    gate = jnp.concatenate(gates, axis=0)
    gate = gate / jnp.sum(gate, axis=0, keepdims=True) * ROUTED_SCALE

    before = _dot(onehot.astype(BF16), tri_ref[...]) + carry[:, 0:1]
    ranks = [jnp.sum(jnp.where(e_iota == idxs[k], before, 0.0), axis=0, keepdims=True) for k in range(TOP_K)]
    idx_ref[...] = idx.astype(jnp.int32)
    gate_ref[...] = gate
    rank_ref[...] = jnp.concatenate(ranks, axis=0).astype(jnp.int32)
    total = carry[...] + jnp.sum(onehot, axis=1, keepdims=True)
    carry[...] = total
    cnt_ref[...] = total


def _route(lg_t, bias, tri):
    e, t = lg_t.shape
    tok = pl.BlockSpec((TOP_K, TR), lambda i: (0, i))
    return pl.pallas_call(
        _route_kernel,
        out_shape=(jax.ShapeDtypeStruct((TOP_K, t), jnp.int32),
                   jax.ShapeDtypeStruct((TOP_K, t), F32),
                   jax.ShapeDtypeStruct((TOP_K, t), jnp.int32),
                   jax.ShapeDtypeStruct((e, LANES), F32)),
        grid=(t // TR,),
        in_specs=[pl.BlockSpec((e, TR), lambda i: (0, i)),
                  pl.BlockSpec((e, 1), lambda i: (0, 0)),
                  pl.BlockSpec((TR, TR), lambda i: (0, 0))],
        out_specs=(tok, tok, tok, pl.BlockSpec((e, LANES), lambda i: (0, 0))),
        scratch_shapes=[pltpu.VMEM((e, LANES), F32)],
        compiler_params=pltpu.CompilerParams(vmem_limit_bytes=VMEM_LIMIT),
        name="route",
    )(lg_t, bias, tri)


def _plan_kernel(n_blocks, size_ref, start_ref, expert_ref, valid_ref, nact_ref):
    def per_expert(e, first_block):
        size = size_ref[e]
        n_blk = (size + TM_EXP - 1) // TM_EXP
        start_ref[e] = first_block * TM_EXP

        def per_block(j, carry):
            expert_ref[first_block + j] = e
            valid_ref[first_block + j] = jnp.minimum(size - j * TM_EXP, TM_EXP)
            return carry

        lax.fori_loop(0, n_blk, per_block, 0)
        return first_block + n_blk

    n_active = lax.fori_loop(0, N_EXPERTS, per_expert, 0)
    nact_ref[0] = n_active

    def unused(i, carry):
        expert_ref[i] = N_EXPERTS - 1
        valid_ref[i] = 0
        return carry

    lax.fori_loop(n_active, n_blocks, unused, 0)


def _plan_blocks(sizes, n_blocks):
    smem = pl.BlockSpec(memory_space=pltpu.SMEM)
    return pl.pallas_call(
        functools.partial(_plan_kernel, n_blocks),
        out_shape=(jax.ShapeDtypeStruct((N_EXPERTS,), jnp.int32),
                   jax.ShapeDtypeStruct((n_blocks,), jnp.int32),
                   jax.ShapeDtypeStruct((n_blocks,), jnp.int32),
                   jax.ShapeDtypeStruct((1,), jnp.int32)),
        in_specs=[smem],
        out_specs=(smem, smem, smem, smem),
        name="plan_blocks",
    )(sizes)


def _dest_kernel(start_ref, idx_ref, rank_ref, dest_ref):
    idx = idx_ref[...]
    dest = rank_ref[...]
    for e in range(N_EXPERTS):
        dest = dest + jnp.where(idx == e, start_ref[e], 0)
    dest_ref[...] = dest


def _dest_rows(pad_start, idx_kt, rank_kt):
    n_k, t = idx_kt.shape
    tile = 4096
    blk = pl.BlockSpec((n_k, tile), lambda i: (0, i))
    return pl.pallas_call(
        _dest_kernel,
        out_shape=jax.ShapeDtypeStruct((n_k, t), jnp.int32),
        grid=(t // tile,),
        in_specs=[pl.BlockSpec(memory_space=pltpu.SMEM), blk, blk],
        out_specs=blk,
        name="dest_rows",
    )(pad_start, idx_kt, rank_kt)


def _sc_mesh():
    return plsc.VectorSubcoreMesh(core_axis_name="c", subcore_axis_name="s")


def _sc_token_base(steps, j):
    worker = lax.axis_index("s") * SC_CORES + lax.axis_index("c")
    return (worker * steps + j) * SC_TOKENS


def _sc_scatter(dest_kt, h2p, n_rows):
    t, width = h2p.shape
    steps = t // (SC_WORKERS * SC_TOKENS)

    @functools.partial(
        pl.kernel, mesh=_sc_mesh(),
        out_type=jax.ShapeDtypeStruct((n_rows, width), U32),
        scratch_types=[pltpu.VMEM((TOP_K, SC_TOKENS), jnp.int32),
                       pltpu.VMEM((SC_TOKENS, width), U32),
                       pltpu.SemaphoreType.DMA],
        name="sc_scatter",
    )
    def body(dest_hbm, h_hbm, xs_hbm, idx_v, rows_v, sem):
        @pl.loop(0, steps)
        def _(j):
            base = _sc_token_base(steps, j)
            pltpu.sync_copy(dest_hbm.at[:, pl.ds(base, SC_TOKENS)], idx_v)
            pltpu.sync_copy(h_hbm.at[pl.ds(base, SC_TOKENS)], rows_v)
            copies = [pltpu.async_copy(rows_v, xs_hbm.at[idx_v.at[k]], sem) for k in range(TOP_K)]
            for cp in copies:
                cp.wait()

    return body(dest_kt, h2p)


def _sc_gather(dest_kt, ys, token0, n_tokens):
    n_k = dest_kt.shape[0]
    width = ys.shape[1]
    steps = n_tokens // (SC_WORKERS * SC_TOKENS)

    half = SC_TOKENS // 2
    units = [(k, h) for k in range(n_k) for h in range(2)]
    n_buf = 3

    @functools.partial(
        pl.kernel, mesh=_sc_mesh(),
        out_type=jax.ShapeDtypeStruct((n_k, n_tokens, width), U32),
        scratch_types=[pltpu.VMEM((n_k, SC_TOKENS), jnp.int32),
                       pltpu.VMEM((n_buf, half, width), U32),
                       pltpu.SemaphoreType.DMA((n_buf,))],
        name="sc_gather",
    )
    def body(dest_hbm, ys_hbm, yk_hbm, idx_v, rows_v, sems):
        @pl.loop(0, steps)
        def _(j):
            base = _sc_token_base(steps, j)
            pltpu.sync_copy(dest_hbm.at[:, pl.ds(token0 + base, SC_TOKENS)], idx_v)

            def gather(u):
                k, h = units[u]
                slot = u % n_buf
                return pltpu.make_async_copy(ys_hbm.at[idx_v.at[k, pl.ds(h * half, half)]], rows_v.at[slot],
                                             sems.at[slot])

            gather(0).start()
            gather(1).start()
            for u, (k, h) in enumerate(units):
                gather(u).wait()
                pltpu.sync_copy(rows_v.at[u % n_buf], yk_hbm.at[k, pl.ds(base + h * half, half)])
                if u + 2 < len(units):
                    gather(u + 2).start()

    return body(dest_kt, ys)


def _experts_kernel(be_ref, valid_ref, nact_ref, xs_hbm, wg_ref, wu_ref, wd_ref, ys_ref, wg_bf, wu_bf, wd_bf,
                    xbuf, sems):
    i = pl.program_id(0)
    n_active = nact_ref[0]
    slot = lax.rem(i, EXP_SLOTS)

    def fetch(j):
        src = xs_hbm.at[pl.ds(pl.multiple_of(j * TM_EXP, TM_EXP), TM_EXP), :]
        s = lax.rem(j, EXP_SLOTS)
        return pltpu.make_async_copy(src, xbuf.at[s], sems.at[s])

    @pl.when(i == 0)
    def _():
        for j in range(EXP_SLOTS - 1):
            pl.when(j < n_active)(lambda j=j: fetch(j).start())

    @pl.when(i + (EXP_SLOTS - 1) < n_active)
    def _():
        fetch(i + (EXP_SLOTS - 1)).start()

    prev = be_ref[jnp.maximum(i - 1, 0)]

    @pl.when((i == 0) | (be_ref[i] != prev))
    def _():
        wg_bf[...] = wg_ref[0].astype(BF16)
        wu_bf[...] = wu_ref[0].astype(BF16)
        wd_bf[...] = wd_ref[0].astype(BF16)

    @pl.when(i < n_active)
    def _():
        fetch(i).wait()

    valid = jnp.where(i < n_active, valid_ref[i], 0)

    def run(rows):
        row = lax.broadcasted_iota(jnp.int32, (rows, xbuf.shape[2]), 0)
        words = jnp.where(row < valid, xbuf[slot, 0:rows, :], jnp.uint32(0))
        xb = _unpack_words(words).astype(BF16)
        hid = _silu(_dot(xb, wg_bf[...])) * _dot(xb, wu_bf[...])
        ys_ref[0:rows, :] = _pack_words(_dot(hid.astype(BF16), wd_bf[...]))
        if rows < TM_EXP:
            ys_ref[rows:, :] = jnp.zeros((TM_EXP - rows, ys_ref.shape[1]), U32)

    rows, lower = TM_EXP, TM_EXP // 2
    while rows >= EXP_MIN_ROWS:
        lo = lower if rows > EXP_MIN_ROWS else 0
        pl.when((valid > lo) & (valid <= rows))(functools.partial(run, rows))
        rows, lower = rows // 2, lower // 2

    @pl.when(valid == 0)
    def _():
        ys_ref[...] = jnp.zeros_like(ys_ref)


def _experts(block_e, block_valid, n_active, xs, w_gate, w_up, w_down):
    n_rows, half = xs.shape
    d = w_gate.shape[1]
    nb = n_rows // TM_EXP
    grid_spec = pltpu.PrefetchScalarGridSpec(
        num_scalar_prefetch=3,
        grid=(nb,),
        in_specs=[pl.BlockSpec(memory_space=pl.ANY),
                  pl.BlockSpec((1, d, D_EXPERT), lambda i, be, bv, na: (be[i], 0, 0)),
                  pl.BlockSpec((1, d, D_EXPERT), lambda i, be, bv, na: (be[i], 0, 0)),
                  pl.BlockSpec((1, D_EXPERT, d), lambda i, be, bv, na: (be[i], 0, 0))],
        out_specs=pl.BlockSpec((TM_EXP, half), lambda i, be, bv, na: (i, 0)),
        scratch_shapes=[pltpu.VMEM((d, D_EXPERT), BF16), pltpu.VMEM((d, D_EXPERT), BF16),
                        pltpu.VMEM((D_EXPERT, d), BF16),
                        pltpu.VMEM((EXP_SLOTS, TM_EXP, half), U32),
                        pltpu.SemaphoreType.DMA((EXP_SLOTS,))],
    )
    return pl.pallas_call(
        _experts_kernel,
        out_shape=jax.ShapeDtypeStruct((n_rows, half), U32),
        grid_spec=grid_spec,
        compiler_params=pltpu.CompilerParams(vmem_limit_bytes=VMEM_LIMIT, dimension_semantics=("arbitrary",)),
        name="experts",
    )(block_e, block_valid, n_active, xs, w_gate, w_up, w_down)


def _combine_kernel(yk_ref, x1_ref, h2_ref, gate_ref, mod_ref, wsg_ref, wsu_ref, wsd_ref, fg_ref, *out_refs):
    out_ref = out_refs[-1]
    hb = _unpack_words(h2_ref[...]).astype(BF16)
    hid = _silu(_dot(hb, wsg_ref[...])) * _dot(hb, wsu_ref[...])
    ffn = _dot(hid.astype(BF16), wsd_ref[...])
    gate = gate_ref[...]
    for k in range(TOP_K):
        ffn = ffn + gate[:, k:k + 1] * _unpack_words(yk_ref[k])
    x2 = x1_ref[...] + mod_ref[0, 5:6, :] * ffn
    ms = jnp.mean(x2 * x2, axis=-1, keepdims=True)
    out_ref[...] = x2 * lax.rsqrt(ms + EPS) * fg_ref[...]


def _combine(yk, token0, out_token0, n_out_tokens, prev_out, x1, h2p, gate_tk, mod, wsg_bf, wsu_bf, wsd_bf, final_g,
             seq_len):
    t, d = n_out_tokens, x1.shape[1]
    tiles_per_seq = seq_len // TF
    tile0 = token0 // TF
    out_tile0 = out_token0 // TF
    in_tok = pl.BlockSpec((TF, d), lambda i: (tile0 + i, 0))
    tok = pl.BlockSpec((TF, d), lambda i: (out_tile0 + i, 0))
    in_specs = [pl.BlockSpec((TOP_K, TF, d // 2), lambda i: (0, i, 0)),
                in_tok, pl.BlockSpec((TF, d // 2), lambda i: (tile0 + i, 0)),
                pl.BlockSpec((TF, TOP_K), lambda i: (tile0 + i, 0)),
                pl.BlockSpec((1, 6, d), lambda i: ((out_tile0 + i) // tiles_per_seq, 0, 0)),
                pl.BlockSpec((d, D_SHARED), lambda i: (0, 0)),
                pl.BlockSpec((d, D_SHARED), lambda i: (0, 0)),
                pl.BlockSpec((D_SHARED, d), lambda i: (0, 0)),
                pl.BlockSpec((1, d), lambda i: (0, 0))]
    args = [yk, x1, h2p, gate_tk, mod, wsg_bf, wsu_bf, wsd_bf, final_g]
    aliases = {}
    if prev_out is not None:
        in_specs.append(pl.BlockSpec(memory_space=pl.ANY))
        args.append(prev_out)
        aliases = {len(args) - 1: 0}
    return pl.pallas_call(
        _combine_kernel,
        out_shape=jax.ShapeDtypeStruct((t, d), F32),
        grid=(yk.shape[1] // TF,),
        in_specs=in_specs,
        out_specs=tok,
        input_output_aliases=aliases,
        compiler_params=pltpu.CompilerParams(vmem_limit_bytes=VMEM_LIMIT),
        name="combine",
    )(*args)


def _rope_tables(n_tokens):
    n_rows = n_tokens // GRID_W
    n_freq = HEAD_DIM // 4
    inv_freq = ROPE_THETA ** (-jnp.arange(n_freq, dtype=F32) / n_freq)
    ang_r = jnp.arange(n_rows).astype(F32)[:, None] * inv_freq[None, :]
    ang_c = jnp.arange(GRID_W).astype(F32)[:, None] * inv_freq[None, :]

    def per_token(row_part, col_part):
        rows = jnp.broadcast_to(row_part[:, None, :], (n_rows, GRID_W, n_freq))
        cols = jnp.broadcast_to(col_part[None, :, :], (n_rows, GRID_W, n_freq))
        return rows.reshape(n_tokens, n_freq), cols.reshape(n_tokens, n_freq)

    cos_r, cos_c = per_token(jnp.cos(ang_r), jnp.cos(ang_c))
    sin_r, sin_c = per_token(jnp.sin(ang_r), jnp.sin(ang_c))
    cos = jnp.concatenate([cos_r, cos_r, cos_c, cos_c], axis=1)
    sin = jnp.concatenate([-sin_r, sin_r, -sin_c, sin_c], axis=1)
    reps = LANES // HEAD_DIM
    return jnp.tile(cos, (1, reps)), jnp.tile(sin, (1, reps))


def _pool_bands():
    i = jnp.arange(QB)[:, None]
    r = jnp.arange(POOL_SLAB)[None, :]
    return jnp.stack([((r >= i + POOL_OFF - w // 2) & (r < i + POOL_OFF + w // 2)).astype(BF16)
                      for w in POOL_WINDOWS])


def kernel(x, c, ctx, c_ctx, w_ada, b_ada, norm1_g, norm2_g, w_in, attn_sink, pool_w, pool_scale, w_out,
           w_router, router_bias, w_gate, w_up, w_down, ws_gate, ws_up, ws_down, final_g):
    b, s, d = x.shape
    t = b * s
    assert w_ada.shape[0] == 1 and d == D_MODEL and s % TQ == 0 and b + 1 <= 8

    c8 = jnp.zeros((8, d), F32).at[:b].set(c).at[b].set(c_ctx)
    mod = _ada(c8, w_ada[0], b_ada[0]).reshape(8, 6, d)
    g1 = norm1_g[0].reshape(1, d)
    g2 = norm2_g[0].reshape(1, d)
    w_in_bf = w_in[0].astype(BF16)
    cos_t, sin_t = _rope_tables(s)

    q, k4, v4, p = _inproj(x, mod, g1, w_in_bf, cos_t, sin_t)
    kc4, vc4 = _ctxproj(ctx, mod[b:b + 1], g1, w_in_bf[:, ATTN_WIDTH:ATTN_WIDTH + 2 * KV_WIDTH])

    wr_t = w_router[0].T
    wr_hi = wr_t.astype(BF16)
    wr_lo = (wr_t - wr_hi.astype(F32)).astype(BF16)
    attn_consts = (_pool_bands(), pool_w[0].astype(BF16), pool_scale[0].reshape(1, POOL_WIDTH),
                   w_out[0].astype(BF16), g2, wr_hi, wr_lo)
    tri = jnp.triu(jnp.ones((TR, TR), BF16), k=1)
    shared_w = (ws_gate[0].astype(BF16), ws_up[0].astype(BF16), ws_down[0].astype(BF16))

    nb = b // TOKEN_GROUPS
    tg = nb * s
    assert b % TOKEN_GROUPS == 0 and tg % (COMBINE_CHUNKS * SC_WORKERS * SC_TOKENS) == 0
    n_rows = -(-(tg * TOP_K + N_EXPERTS * (TM_EXP - 1)) // TM_EXP) * TM_EXP
    groups = []
    for g in range(TOKEN_GROUPS):
        x1, h2p, lg_t = _attn(g * nb, nb, attn_sink[0], q, k4, v4, kc4, vc4, p, x, mod, *attn_consts)
        idx_kt, gate_kt, rank_kt, counts = _route(lg_t, router_bias[0].reshape(N_EXPERTS, 1), tri)
        pad_start, block_e, block_valid, n_active = _plan_blocks(counts[:, 0].astype(jnp.int32), n_rows // TM_EXP)
        dest_kt = _dest_rows(pad_start, idx_kt, rank_kt)
        xs = _sc_scatter(dest_kt, h2p, n_rows)
        groups.append((x1.reshape(tg, d), h2p, gate_kt.T, dest_kt, xs, block_e, block_valid, n_active))

    out = None
    chunk = tg // COMBINE_CHUNKS
    for g, (x1, h2p, gate_tk, dest_kt, xs, block_e, block_valid, n_active) in enumerate(groups):
        ys = _experts(block_e, block_valid, n_active, xs, w_gate[0], w_up[0], w_down[0])
        for token0 in range(0, tg, chunk):
            out = _combine(_sc_gather(dest_kt, ys, token0, chunk), token0, g * tg + token0, t, out, x1, h2p,
                           gate_tk, mod, *shared_w, final_g.reshape(1, d), s)
    return out.reshape(b, s, d)
```

```python
import functools

import jax
import jax.numpy as jnp
from jax import lax
from jax.experimental import pallas as pl
from jax.experimental.pallas import tpu as pltpu
from jax.experimental.pallas import tpu_sc as plsc

F32 = jnp.float32
BF16 = jnp.bfloat16

D_MODEL = 1024
GRID_W = 64
N_HEADS = 8
N_KV_HEADS = 2
HEAD_DIM = 64
ATTN_WIDTH = N_HEADS * HEAD_DIM
KV_WIDTH = N_KV_HEADS * HEAD_DIM
WINDOW = 128
ROPE_THETA = 10000.0
POOL_WINDOWS = (2, 4, 8, 16)
POOL_WIDTH = D_MODEL - ATTN_WIDTH
POOL_GROUP_DIM = POOL_WIDTH // len(POOL_WINDOWS)
IN_COLS = ATTN_WIDTH + 2 * KV_WIDTH + POOL_WIDTH
N_EXPERTS = 64
TOP_K = 8
N_EXPERT_GROUPS = 8
EXPERTS_PER_GROUP = N_EXPERTS // N_EXPERT_GROUPS
TOPK_GROUPS = 4
D_EXPERT = 256
D_SHARED = 256
ROUTED_SCALE = 2.5
EPS = 1e-6
LOG2E = 1.4426950408889634

LANES = 128
U32 = jnp.uint32
VMEM_LIMIT = 48 * 1024 * 1024

TM_PROJ = 1024
TQ = 1024
QB = 128
POOL_SLAB = 256
POOL_OFF = 64
TR = 512
TM_EXP = 1024
EXP_MIN_ROWS = 256
EXP_SLOTS = 4
EXP_OUT_SLOTS = 3
TF = 256
TOKEN_GROUPS = 2
COMBINE_CHUNKS = 2
SC_CORES = 2
SC_WORKERS = 32
SC_TOKENS = 128


def _silu(x):
    return x * (1.0 / (1.0 + jnp.exp(-x)))


def _split_bf16(x):
    hi = x.astype(BF16)
    lo = (x - hi.astype(F32)).astype(BF16)
    return hi, lo


def _dot(a, b):
    return jnp.dot(a, b, preferred_element_type=F32)


def _pack_words(val):
    half = val.shape[1] // 2
    lo = lax.bitcast_convert_type(val[:, :half].astype(BF16).astype(F32), U32)
    hi = lax.bitcast_convert_type(val[:, half:].astype(BF16).astype(F32), U32)
    return lax.shift_right_logical(lo, jnp.uint32(16)) | hi


def _unpack_words(words):
    lo = lax.bitcast_convert_type(lax.shift_left(words, jnp.uint32(16)), F32)
    hi = lax.bitcast_convert_type(words & jnp.uint32(0xFFFF0000), F32)
    return jnp.concatenate([lo, hi], axis=1)


def _dot_nt(a, b):
    return lax.dot_general(a, b, (((1,), (1,)), ((), ())), preferred_element_type=F32)


def _ada_kernel(c_ref, w_ref, b_ref, o_ref):
    a_hi, a_lo = _split_bf16(_silu(c_ref[...]))
    w_hi, w_lo = _split_bf16(w_ref[...])
    o_ref[...] = _dot(a_hi, w_hi) + _dot(a_lo, w_hi) + _dot(a_hi, w_lo) + b_ref[...]


def _ada(c8, w_ada, b_ada):
    d = c8.shape[1]
    n = w_ada.shape[1]
    tn = 512
    return pl.pallas_call(
        _ada_kernel,
        out_shape=jax.ShapeDtypeStruct((8, n), F32),
        grid=(n // tn,),
        in_specs=[pl.BlockSpec((8, d), lambda j: (0, 0)),
                  pl.BlockSpec((d, tn), lambda j: (0, j)),
                  pl.BlockSpec((1, tn), lambda j: (0, j))],
        out_specs=pl.BlockSpec((8, tn), lambda j: (0, j)),
        compiler_params=pltpu.CompilerParams(vmem_limit_bytes=VMEM_LIMIT),
        name="ada",
    )(c8, w_ada, b_ada.reshape(1, n))


def _norm_mod(x, g, shift, scale):
    ms = jnp.mean(x * x, axis=-1, keepdims=True)
    return (x * lax.rsqrt(ms + EPS) * g) * (1.0 + scale) + shift


def _lane_variants(t):
    lane = lax.broadcasted_iota(jnp.int32, t.shape, 1)
    lo = lane < HEAD_DIM
    tr = pltpu.roll(t, HEAD_DIM, 1)
    zero = jnp.zeros_like(t)
    return (jnp.where(lo, t, zero), jnp.where(lo, zero, tr),
            jnp.where(lo, tr, zero), jnp.where(lo, zero, t))


def _store_variants(ref, t):
    for i, var in enumerate(_lane_variants(t)):
        ref[0, :, i * LANES:(i + 1) * LANES] = var.astype(BF16)


def _inproj_kernel(x_ref, mod_ref, g_ref, w_ref, cos_ref, sin_ref, q_ref, k_ref, v_ref, p_ref):
    h = _norm_mod(x_ref[0], g_ref[...], mod_ref[0, 0:1, :], mod_ref[0, 1:2, :])
    z = _dot(h.astype(BF16), w_ref[...])
    cos = cos_ref[...]
    sin = sin_ref[...]
    lane = lax.broadcasted_iota(jnp.int32, cos.shape, 1)
    first_half = (lane & 16) == 0

    def rope(zc):
        partner = jnp.where(first_half, pltpu.roll(zc, LANES - 16, 1), pltpu.roll(zc, 16, 1))
        return zc * cos + partner * sin

    scale = HEAD_DIM ** -0.5 * LOG2E
    for c in range(ATTN_WIDTH // LANES):
        q_ref[0, :, c * LANES:(c + 1) * LANES] = (rope(z[:, c * LANES:(c + 1) * LANES]) * scale).astype(BF16)
    _store_variants(k_ref, rope(z[:, ATTN_WIDTH:ATTN_WIDTH + KV_WIDTH]))
    _store_variants(v_ref, z[:, ATTN_WIDTH + KV_WIDTH:ATTN_WIDTH + 2 * KV_WIDTH])
    p_ref[0] = z[:, ATTN_WIDTH + 2 * KV_WIDTH:]


def _inproj(x, mod, g1, w_in_bf, cos_t, sin_t):
    b, s, d = x.shape
    tm = TM_PROJ
    return pl.pallas_call(
        _inproj_kernel,
        out_shape=(jax.ShapeDtypeStruct((b, s, ATTN_WIDTH), BF16),
                   jax.ShapeDtypeStruct((b, s, 4 * LANES), BF16),
                   jax.ShapeDtypeStruct((b, s, 4 * LANES), BF16),
                   jax.ShapeDtypeStruct((b, s, POOL_WIDTH), F32)),
        grid=(s // tm, b),
        in_specs=[pl.BlockSpec((1, tm, d), lambda n, bi: (bi, n, 0)),
                  pl.BlockSpec((1, 6, d), lambda n, bi: (bi, 0, 0)),
                  pl.BlockSpec((1, d), lambda n, bi: (0, 0)),
                  pl.BlockSpec((d, IN_COLS), lambda n, bi: (0, 0)),
                  pl.BlockSpec((tm, LANES), lambda n, bi: (n, 0)),
                  pl.BlockSpec((tm, LANES), lambda n, bi: (n, 0))],
        out_specs=(pl.BlockSpec((1, tm, ATTN_WIDTH), lambda n, bi: (bi, n, 0)),
                   pl.BlockSpec((1, tm, 4 * LANES), lambda n, bi: (bi, n, 0)),
                   pl.BlockSpec((1, tm, 4 * LANES), lambda n, bi: (bi, n, 0)),
                   pl.BlockSpec((1, tm, POOL_WIDTH), lambda n, bi: (bi, n, 0))),
        compiler_params=pltpu.CompilerParams(vmem_limit_bytes=VMEM_LIMIT),
        name="inproj",
    )(x, mod, g1, w_in_bf, cos_t, sin_t)


def _ctxproj_kernel(x_ref, mod_ref, g_ref, w_ref, k_ref, v_ref):
    h = _norm_mod(x_ref[0], g_ref[...], mod_ref[0, 0:1, :], mod_ref[0, 1:2, :])
    z = _dot(h.astype(BF16), w_ref[...])
    _store_variants(k_ref, z[:, :KV_WIDTH])
    _store_variants(v_ref, z[:, KV_WIDTH:])


def _ctxproj(ctx, mod_c, g1, w_kv_bf):
    b, c, d = ctx.shape
    return pl.pallas_call(
        _ctxproj_kernel,
        out_shape=(jax.ShapeDtypeStruct((b, c, 4 * LANES), BF16),
                   jax.ShapeDtypeStruct((b, c, 4 * LANES), BF16)),
        grid=(b,),
        in_specs=[pl.BlockSpec((1, c, d), lambda bi: (bi, 0, 0)),
                  pl.BlockSpec((1, 6, d), lambda bi: (0, 0, 0)),
                  pl.BlockSpec((1, d), lambda bi: (0, 0)),
                  pl.BlockSpec((d, 2 * KV_WIDTH), lambda bi: (0, 0))],
        out_specs=(pl.BlockSpec((1, c, 4 * LANES), lambda bi: (bi, 0, 0)),
                   pl.BlockSpec((1, c, 4 * LANES), lambda bi: (bi, 0, 0))),
        compiler_params=pltpu.CompilerParams(vmem_limit_bytes=VMEM_LIMIT),
        name="ctxproj",
    )(ctx, mod_c, g1, w_kv_bf)


def _fold(op, tiles):
    while len(tiles) > 1:
        tiles = [op(tiles[i], tiles[i + 1]) if i + 1 < len(tiles) else tiles[i] for i in range(0, len(tiles), 2)]
    return tiles[0]


def _stack_variants(t4, kv):
    return jnp.concatenate([t4[:, (2 * kv) * LANES:(2 * kv + 1) * LANES],
                            t4[:, (2 * kv + 1) * LANES:(2 * kv + 2) * LANES]], axis=0)


def _attn_kernel(seq_len, sink_ref, q_ref, k_ref, kp_ref, kn_ref, v_ref, vp_ref, vn_ref, kc_ref, vc_ref,
                 p_ref, pp_ref, pn_ref, x_ref, mod_ref, band_ref, poolw_ref, pscale_ref, wout_ref,
                 g2_ref, wrh_ref, wrl_ref, x1_ref, h2_ref, lg_ref, kwin, vwin, pext, mix, s_scr, p_scr, m_scr):
    n = pl.program_id(1)
    n_last = pl.num_programs(1) - 1

    kwin[0:QB, :] = kp_ref[0]
    kwin[QB:QB + TQ, :] = k_ref[0]
    kwin[QB + TQ:, :] = kn_ref[0]
    vwin[0:QB, :] = vp_ref[0]
    vwin[QB:QB + TQ, :] = v_ref[0]
    vwin[QB + TQ:, :] = vn_ref[0]

    pext[0:QB - 8, :] = jnp.zeros((QB - 8, POOL_WIDTH), F32)
    pext[QB - 8:QB, :] = jnp.where(n > 0, pp_ref[0], 0.0)
    pext[QB:QB + TQ, :] = p_ref[0]
    pext[QB + TQ:QB + TQ + 8, :] = jnp.where(n < n_last, pn_ref[0], 0.0)
    pext[QB + TQ + 8:, :] = jnp.zeros((QB - 8, POOL_WIDTH), F32)

    row = lax.broadcasted_iota(jnp.int32, (QB, 3 * QB), 0)
    col = lax.broadcasted_iota(jnp.int32, (QB, 3 * QB), 1)
    in_band = (col >= row) & (col <= row + 2 * WINDOW)
    tok = lax.broadcasted_iota(jnp.int32, (QB, 1), 0)
    kc = kc_ref[0]
    kc_rows = [_stack_variants(kc, kv) for kv in range(N_KV_HEADS)]
    vc_rows = [_stack_variants(vc_ref[0], kv) for kv in range(N_KV_HEADS)]

    def sub_block(j, carry):
        r0 = pl.multiple_of(j * QB, QB)
        qj = q_ref[0, pl.ds(r0, QB), :]
        kw = kwin[pl.ds(r0, 3 * QB), :]
        vw = vwin[pl.ds(r0, 3 * QB), :]
        kpos = col + (n * TQ + j * QB - QB)
        ok = in_band & (kpos >= 0) & (kpos < seq_len)
        bias = jnp.where(ok, 0.0, -jnp.inf)
        n_loc, n_ctx = 3 * QB, kc.shape[0]
        bias2 = jnp.concatenate([bias, bias], axis=1)
        k_rows = [_stack_variants(kw, kv) for kv in range(N_KV_HEADS)]
        v_rows = [_stack_variants(vw, kv) for kv in range(N_KV_HEADS)]
        group = N_HEADS // N_KV_HEADS

        def head_tiles(head):
            loc0 = (head % 2) * n_loc
            ctx0 = 2 * n_loc + (head % 2) * n_ctx
            return ([loc0 + i * LANES for i in range(n_loc // LANES)]
                    + [ctx0 + i * LANES for i in range(n_ctx // LANES)])

        for c in range(N_HEADS // 2):
            qc = qj[:, c * LANES:(c + 1) * LANES]
            s_scr[c, :, 0:2 * n_loc] = _dot_nt(qc, k_rows[2 * c // group]) + bias2
            s_scr[c, :, 2 * n_loc:] = _dot_nt(qc, kc_rows[2 * c // group])
        for head in range(N_HEADS):
            tiles = [s_scr[head // 2, :, st:st + LANES] for st in head_tiles(head)]
            row_max = jnp.max(_fold(jnp.maximum, tiles), axis=1, keepdims=True)
            m_scr[head] = jnp.broadcast_to(jnp.maximum(row_max, sink_ref[head] * LOG2E), (QB, LANES))
        for head in range(N_HEADS):
            m = m_scr[head]
            acc = None
            for st in head_tiles(head):
                p = jnp.exp2(s_scr[head // 2, :, st:st + LANES] - m)
                p_scr[head // 2, :, st:st + LANES] = p.astype(BF16)
                acc = p if acc is None else acc + p
            denom = (jnp.broadcast_to(jnp.sum(acc, axis=1, keepdims=True), (QB, LANES))
                     + jnp.exp2(sink_ref[head] * LOG2E - m))
            m_scr[head] = 1.0 / denom
        lane = lax.broadcasted_iota(jnp.int32, (QB, LANES), 1)
        for c in range(N_HEADS // 2):
            o = (_dot(p_scr[c, :, 0:2 * n_loc], v_rows[2 * c // group])
                 + _dot(p_scr[c, :, 2 * n_loc:], vc_rows[2 * c // group]))
            o = o * jnp.where(lane < HEAD_DIM, m_scr[2 * c], m_scr[2 * c + 1])
            mix[pl.ds(r0, QB), c * LANES:(c + 1) * LANES] = o.astype(BF16)

        slab = pext[pl.ds(pl.multiple_of(r0 + POOL_OFF, 8), POOL_SLAB), :]
        tpos = tok + (n * TQ + j * QB)
        for g, w in enumerate(POOL_WINDOWS):
            sg = slab[:, g * LANES:(g + 1) * LANES]
            hi, lo = _split_bf16(sg)
            band = band_ref[g]
            wsum = _dot(band, hi) + _dot(band, lo)
            cnt = (jnp.minimum(tpos - w // 2 + w, seq_len) - jnp.maximum(tpos - w // 2, 0)).astype(F32)
            dlt = wsum / cnt - sg[POOL_OFF:POOL_OFF + QB, :]
            y = _dot(dlt.astype(BF16), poolw_ref[g]) * pscale_ref[:, g * LANES:(g + 1) * LANES]
            mix[pl.ds(r0, QB), ATTN_WIDTH + g * LANES:ATTN_WIDTH + (g + 1) * LANES] = y.astype(BF16)
        return carry

    lax.fori_loop(0, TQ // QB, sub_block, 0)

    proj = _dot(mix[...], wout_ref[...])
    x1 = x_ref[0] + mod_ref[0, 2:3, :] * proj
    x1_ref[0] = x1
    h2 = _norm_mod(x1, g2_ref[...], mod_ref[0, 3:4, :], mod_ref[0, 4:5, :])
    h2_ref[...] = _pack_words(h2)
    h_hi, h_lo = _split_bf16(h2)
    wrh = wrh_ref[...]
    lg_ref[...] = _dot_nt(wrh, h_hi) + _dot_nt(wrh, h_lo) + _dot_nt(wrl_ref[...], h_hi)


def _attn(b0, b, sink, q, k4, v4, kc4, vc4, p, x, mod, band, poolw_bf, pscale, wout_bf, g2, wr_hi, wr_lo):
    _, s, d = x.shape
    c = kc4.shape[1]
    nt = s // TQ
    hb = TQ // QB
    pb = TQ // 8
    kv_main = pl.BlockSpec((1, TQ, 4 * LANES), lambda bi, n: (b0 + bi, n, 0))
    kv_prev = pl.BlockSpec((1, QB, 4 * LANES), lambda bi, n: (b0 + bi, jnp.maximum(n * hb - 1, 0), 0))
    kv_next = pl.BlockSpec((1, QB, 4 * LANES), lambda bi, n: (b0 + bi, jnp.minimum((n + 1) * hb, s // QB - 1), 0))
    const2 = lambda bi, n: (0, 0)
    const3 = lambda bi, n: (0, 0, 0)
    return pl.pallas_call(
        functools.partial(_attn_kernel, s),
        out_shape=(jax.ShapeDtypeStruct((b, s, d), F32),
                   jax.ShapeDtypeStruct((b * s, d // 2), U32),
                   jax.ShapeDtypeStruct((N_EXPERTS, b * s), F32)),
        grid=(b, nt),
        in_specs=[pl.BlockSpec(memory_space=pltpu.SMEM),
                  pl.BlockSpec((1, TQ, ATTN_WIDTH), lambda bi, n: (b0 + bi, n, 0)),
                  kv_main, kv_prev, kv_next, kv_main, kv_prev, kv_next,
                  pl.BlockSpec((1, c, 4 * LANES), lambda bi, n: (b0 + bi, 0, 0)),
                  pl.BlockSpec((1, c, 4 * LANES), lambda bi, n: (b0 + bi, 0, 0)),
                  pl.BlockSpec((1, TQ, POOL_WIDTH), lambda bi, n: (b0 + bi, n, 0)),
                  pl.BlockSpec((1, 8, POOL_WIDTH), lambda bi, n: (b0 + bi, jnp.maximum(n * pb - 1, 0), 0)),
                  pl.BlockSpec((1, 8, POOL_WIDTH),
                               lambda bi, n: (b0 + bi, jnp.minimum((n + 1) * pb, s // 8 - 1), 0)),
                  pl.BlockSpec((1, TQ, d), lambda bi, n: (b0 + bi, n, 0)),
                  pl.BlockSpec((1, 6, d), lambda bi, n: (b0 + bi, 0, 0)),
                  pl.BlockSpec((len(POOL_WINDOWS), QB, POOL_SLAB), const3),
                  pl.BlockSpec((len(POOL_WINDOWS), POOL_GROUP_DIM, POOL_GROUP_DIM), const3),
                  pl.BlockSpec((1, POOL_WIDTH), const2),
                  pl.BlockSpec((d, d), const2),
                  pl.BlockSpec((1, d), const2),
                  pl.BlockSpec((N_EXPERTS, d), const2),
                  pl.BlockSpec((N_EXPERTS, d), const2)],
        out_specs=(pl.BlockSpec((1, TQ, d), lambda bi, n: (bi, n, 0)),
                   pl.BlockSpec((TQ, d // 2), lambda bi, n: (bi * nt + n, 0)),
                   pl.BlockSpec((N_EXPERTS, TQ), lambda bi, n: (0, bi * nt + n))),
        scratch_shapes=[pltpu.VMEM((TQ + 2 * QB, 4 * LANES), BF16),
                        pltpu.VMEM((TQ + 2 * QB, 4 * LANES), BF16),
                        pltpu.VMEM((TQ + 2 * QB, POOL_WIDTH), F32),
                        pltpu.VMEM((TQ, d), BF16),
                        pltpu.VMEM((N_HEADS // 2, QB, 2 * (3 * QB + c)), F32),
                        pltpu.VMEM((N_HEADS // 2, QB, 2 * (3 * QB + c)), BF16),
                        pltpu.VMEM((N_HEADS, QB, LANES), F32)],
        compiler_params=pltpu.CompilerParams(vmem_limit_bytes=VMEM_LIMIT),
        name="attn",
    )(sink, q, k4, k4, k4, v4, v4, v4, kc4, vc4, p, p, p, x, mod, band, poolw_bf, pscale, wout_bf,
      g2, wr_hi, wr_lo)


def _first_argmax_rows(v, row_iota, n_rows):
    m = jnp.max(v, axis=0, keepdims=True)
    idx = jnp.min(jnp.where(v == m, row_iota, n_rows), axis=0, keepdims=True)
    return m, idx


def _route_kernel(lg_ref, bias_ref, tri_ref, idx_ref, gate_ref, rank_ref, cnt_ref, carry):
    i = pl.program_id(0)

    @pl.when(i == 0)
    def _():
        carry[...] = jnp.zeros_like(carry)

    scores = 1.0 / (1.0 + jnp.exp(-lg_ref[...]))
    biased = scores + bias_ref[...]
    e_iota = lax.broadcasted_iota(jnp.int32, scores.shape, 0).astype(F32)
    g_iota = lax.broadcasted_iota(jnp.int32, (EXPERTS_PER_GROUP, TR), 0).astype(F32)
    neg = -jnp.inf

    grp = []
    for g in range(N_EXPERT_GROUPS):
        blk = biased[g * EXPERTS_PER_GROUP:(g + 1) * EXPERTS_PER_GROUP, :]
        m1, i1 = _first_argmax_rows(blk, g_iota, float(EXPERTS_PER_GROUP))
        m2 = jnp.max(jnp.where(g_iota == i1, neg, blk), axis=0, keepdims=True)
        grp.append(m1 + m2)
    grp = jnp.concatenate(grp, axis=0)
    gg_iota = lax.broadcasted_iota(jnp.int32, grp.shape, 0).astype(F32)
    grp_sel = jnp.zeros(grp.shape, F32)
    for _ in range(TOPK_GROUPS):
        _, gi = _first_argmax_rows(grp, gg_iota, float(N_EXPERT_GROUPS))
        hit = gg_iota == gi
        grp_sel = jnp.where(hit, 1.0, grp_sel)
        grp = jnp.where(hit, neg, grp)
    allowed = jnp.concatenate(
        [jnp.broadcast_to(grp_sel[g:g + 1, :], (EXPERTS_PER_GROUP, TR)) for g in range(N_EXPERT_GROUPS)], axis=0)
    masked = jnp.where(allowed > 0.5, biased, neg)

    idxs, gates = [], []
    onehot = jnp.zeros(scores.shape, F32)
    for _ in range(TOP_K):
        _, ei = _first_argmax_rows(masked, e_iota, float(N_EXPERTS))
        hit = e_iota == ei
        idxs.append(ei)
        gates.append(jnp.sum(jnp.where(hit, scores, 0.0), axis=0, keepdims=True))
        onehot = jnp.where(hit, 1.0, onehot)
        masked = jnp.where(hit, neg, masked)
    idx = jnp.concatenate(idxs, axis=0)
    gate = jnp.concatenate(gates, axis=0)
    gate = gate / jnp.sum(gate, axis=0, keepdims=True) * ROUTED_SCALE

    before = _dot(onehot.astype(BF16), tri_ref[...]) + carry[:, 0:1]
    ranks = [jnp.sum(jnp.where(e_iota == idxs[k], before, 0.0), axis=0, keepdims=True) for k in range(TOP_K)]
    idx_ref[...] = idx.astype(jnp.int32)
    gate_ref[...] = gate
    rank_ref[...] = jnp.concatenate(ranks, axis=0).astype(jnp.int32)
    total = carry[...] + jnp.sum(onehot, axis=1, keepdims=True)
    carry[...] = total
    cnt_ref[...] = total


def _route(lg_t, bias, tri):
    e, t = lg_t.shape
    tok = pl.BlockSpec((TOP_K, TR), lambda i: (0, i))
    return pl.pallas_call(
        _route_kernel,
        out_shape=(jax.ShapeDtypeStruct((TOP_K, t), jnp.int32),
                   jax.ShapeDtypeStruct((TOP_K, t), F32),
                   jax.ShapeDtypeStruct((TOP_K, t), jnp.int32),
                   jax.ShapeDtypeStruct((e, LANES), F32)),
        grid=(t // TR,),
        in_specs=[pl.BlockSpec((e, TR), lambda i: (0, i)),
                  pl.BlockSpec((e, 1), lambda i: (0, 0)),
                  pl.BlockSpec((TR, TR), lambda i: (0, 0))],
        out_specs=(tok, tok, tok, pl.BlockSpec((e, LANES), lambda i: (0, 0))),
        scratch_shapes=[pltpu.VMEM((e, LANES), F32)],
        compiler_params=pltpu.CompilerParams(vmem_limit_bytes=VMEM_LIMIT),
        name="route",
    )(lg_t, bias, tri)


def _plan_kernel(n_blocks, size_ref, start_ref, expert_ref, valid_ref, nact_ref):
    def per_expert(e, first_block):
        size = size_ref[e]
        n_blk = (size + TM_EXP - 1) // TM_EXP
        start_ref[e] = first_block * TM_EXP

        def per_block(j, carry):
            expert_ref[first_block + j] = e
            valid_ref[first_block + j] = jnp.minimum(size - j * TM_EXP, TM_EXP)
            return carry

        lax.fori_loop(0, n_blk, per_block, 0)
        return first_block + n_blk

    n_active = lax.fori_loop(0, N_EXPERTS, per_expert, 0)
    nact_ref[0] = n_active

    def unused(i, carry):
        expert_ref[i] = N_EXPERTS - 1
        valid_ref[i] = 0
        return carry

    lax.fori_loop(n_active, n_blocks, unused, 0)


def _plan_blocks(sizes, n_blocks):
    smem = pl.BlockSpec(memory_space=pltpu.SMEM)
    return pl.pallas_call(
        functools.partial(_plan_kernel, n_blocks),
        out_shape=(jax.ShapeDtypeStruct((N_EXPERTS,), jnp.int32),
                   jax.ShapeDtypeStruct((n_blocks,), jnp.int32),
                   jax.ShapeDtypeStruct((n_blocks,), jnp.int32),
                   jax.ShapeDtypeStruct((1,), jnp.int32)),
        in_specs=[smem],
        out_specs=(smem, smem, smem, smem),
        name="plan_blocks",
    )(sizes)


def _dest_kernel(start_ref, idx_ref, rank_ref, dest_ref):
    idx = idx_ref[...]
    dest = rank_ref[...]
    for e in range(N_EXPERTS):
        dest = dest + jnp.where(idx == e, start_ref[e], 0)
    dest_ref[...] = dest


def _dest_rows(pad_start, idx_kt, rank_kt):
    n_k, t = idx_kt.shape
    tile = 4096
    blk = pl.BlockSpec((n_k, tile), lambda i: (0, i))
    return pl.pallas_call(
        _dest_kernel,
        out_shape=jax.ShapeDtypeStruct((n_k, t), jnp.int32),
        grid=(t // tile,),
        in_specs=[pl.BlockSpec(memory_space=pltpu.SMEM), blk, blk],
        out_specs=blk,
        name="dest_rows",
    )(pad_start, idx_kt, rank_kt)


def _sc_mesh():
    return plsc.VectorSubcoreMesh(core_axis_name="c", subcore_axis_name="s")


def _sc_token_base(steps, j):
    worker = lax.axis_index("s") * SC_CORES + lax.axis_index("c")
    return (worker * steps + j) * SC_TOKENS


def _sc_scatter(dest_kt, h2p, n_rows):
    t, width = h2p.shape
    steps = t // (SC_WORKERS * SC_TOKENS)

    @functools.partial(
        pl.kernel, mesh=_sc_mesh(),
        out_type=jax.ShapeDtypeStruct((n_rows, width), U32),
        scratch_types=[pltpu.VMEM((TOP_K, SC_TOKENS), jnp.int32),
                       pltpu.VMEM((SC_TOKENS, width), U32),
                       pltpu.SemaphoreType.DMA],
        name="sc_scatter",
    )
    def body(dest_hbm, h_hbm, xs_hbm, idx_v, rows_v, sem):
        @pl.loop(0, steps)
        def _(j):
            base = _sc_token_base(steps, j)
            pltpu.sync_copy(dest_hbm.at[:, pl.ds(base, SC_TOKENS)], idx_v)
            pltpu.sync_copy(h_hbm.at[pl.ds(base, SC_TOKENS)], rows_v)
            copies = [pltpu.async_copy(rows_v, xs_hbm.at[idx_v.at[k]], sem) for k in range(TOP_K)]
            for cp in copies:
                cp.wait()

    return body(dest_kt, h2p)


def _sc_gather(dest_kt, ys, token0, n_tokens):
    n_k = dest_kt.shape[0]
    width = ys.shape[1]
    steps = n_tokens // (SC_WORKERS * SC_TOKENS)

    half = SC_TOKENS // 2
    units = [(k, h) for k in range(n_k) for h in range(2)]
    n_buf = 3

    @functools.partial(
        pl.kernel, mesh=_sc_mesh(),
        out_type=jax.ShapeDtypeStruct((n_k, n_tokens, width), U32),
        scratch_types=[pltpu.VMEM((n_k, SC_TOKENS), jnp.int32),
                       pltpu.VMEM((n_buf, half, width), U32),
                       pltpu.SemaphoreType.DMA((n_buf,))],
        name="sc_gather",
    )
    def body(dest_hbm, ys_hbm, yk_hbm, idx_v, rows_v, sems):
        @pl.loop(0, steps)
        def _(j):
            base = _sc_token_base(steps, j)
            pltpu.sync_copy(dest_hbm.at[:, pl.ds(token0 + base, SC_TOKENS)], idx_v)

            def gather(u):
                k, h = units[u]
                slot = u % n_buf
                return pltpu.make_async_copy(ys_hbm.at[idx_v.at[k, pl.ds(h * half, half)]], rows_v.at[slot],
                                             sems.at[slot])

            gather(0).start()
            gather(1).start()
            for u, (k, h) in enumerate(units):
                gather(u).wait()
                pltpu.sync_copy(rows_v.at[u % n_buf], yk_hbm.at[k, pl.ds(base + h * half, half)])
                if u + 2 < len(units):
                    gather(u + 2).start()

    return body(dest_kt, ys)


def _experts_kernel(be_ref, valid_ref, nact_ref, xs_hbm, wg_ref, wu_ref, wd_ref, ys_hbm, wg_bf, wu_bf, wd_bf,
                    xbuf, sems, ybuf, osems):
    i = pl.program_id(0)
    n_active = nact_ref[0]
    slot = lax.rem(i, EXP_SLOTS)

    def fetch(j):
        src = xs_hbm.at[pl.ds(pl.multiple_of(j * TM_EXP, TM_EXP), TM_EXP), :]
        s = lax.rem(j, EXP_SLOTS)
        return pltpu.make_async_copy(src, xbuf.at[s], sems.at[s])

    @pl.when(i == 0)
    def _():
        for j in range(EXP_SLOTS - 1):
            pl.when(j < n_active)(lambda j=j: fetch(j).start())

    @pl.when(i + (EXP_SLOTS - 1) < n_active)
    def _():
        fetch(i + (EXP_SLOTS - 1)).start()

    prev = be_ref[jnp.maximum(i - 1, 0)]

    @pl.when((i == 0) | (be_ref[i] != prev))
    def _():
        wg_bf[...] = wg_ref[0].astype(BF16)
        wu_bf[...] = wu_ref[0].astype(BF16)
        wd_bf[...] = wd_ref[0].astype(BF16)

    @pl.when(i < n_active)
    def _():
        fetch(i).wait()

    valid = jnp.where(i < n_active, valid_ref[i], 0)

    def run(rows):
        row = lax.broadcasted_iota(jnp.int32, (rows, xbuf.shape[2]), 0)
        words = jnp.where(row < valid, xbuf[slot, 0:rows, :], jnp.uint32(0))
        xb = _unpack_words(words).astype(BF16)
        hid = _silu(_dot(xb, wg_bf[...])) * _dot(xb, wu_bf[...])
        ybuf[oslot, 0:rows, :] = _pack_words(_dot(hid.astype(BF16), wd_bf[...]))
        if rows < TM_EXP:
            ybuf[oslot, rows:, :] = jnp.zeros((TM_EXP - rows, ybuf.shape[2]), U32)

    oslot = lax.rem(i, EXP_OUT_SLOTS)

    def put(j):
        dst = ys_hbm.at[pl.ds(pl.multiple_of(j * TM_EXP, TM_EXP), TM_EXP), :]
        s = lax.rem(j, EXP_OUT_SLOTS)
        return pltpu.make_async_copy(ybuf.at[s], dst, osems.at[s])

    @pl.when(i >= EXP_OUT_SLOTS)
    def _():
        put(i - EXP_OUT_SLOTS).wait()

    rows, lower = TM_EXP, TM_EXP // 2
    while rows >= EXP_MIN_ROWS:
        lo = lower if rows > EXP_MIN_ROWS else 0
        pl.when((valid > lo) & (valid <= rows))(functools.partial(run, rows))
        rows, lower = rows // 2, lower // 2

    @pl.when(valid == 0)
    def _():
        ybuf[oslot] = jnp.zeros(ybuf.shape[1:], U32)

    put(i).start()
    n_steps = pl.num_programs(0)

    @pl.when(i == n_steps - 1)
    def _():
        for back in range(EXP_OUT_SLOTS):
            put(n_steps - 1 - back).wait()


def _experts(block_e, block_valid, n_active, xs, w_gate, w_up, w_down):
    n_rows, half = xs.shape
    d = w_gate.shape[1]
    nb = n_rows // TM_EXP
    grid_spec = pltpu.PrefetchScalarGridSpec(
        num_scalar_prefetch=3,
        grid=(nb,),
        in_specs=[pl.BlockSpec(memory_space=pl.ANY),
                  pl.BlockSpec((1, d, D_EXPERT), lambda i, be, bv, na: (be[i], 0, 0)),
                  pl.BlockSpec((1, d, D_EXPERT), lambda i, be, bv, na: (be[i], 0, 0)),
                  pl.BlockSpec((1, D_EXPERT, d), lambda i, be, bv, na: (be[i], 0, 0))],
        out_specs=pl.BlockSpec(memory_space=pl.ANY),
        scratch_shapes=[pltpu.VMEM((d, D_EXPERT), BF16), pltpu.VMEM((d, D_EXPERT), BF16),
                        pltpu.VMEM((D_EXPERT, d), BF16),
                        pltpu.VMEM((EXP_SLOTS, TM_EXP, half), U32),
                        pltpu.SemaphoreType.DMA((EXP_SLOTS,)),
                        pltpu.VMEM((EXP_OUT_SLOTS, TM_EXP, half), U32),
                        pltpu.SemaphoreType.DMA((EXP_OUT_SLOTS,))],
    )
    assert nb >= EXP_OUT_SLOTS
    return pl.pallas_call(
        _experts_kernel,
        out_shape=jax.ShapeDtypeStruct((n_rows, half), U32),
        grid_spec=grid_spec,
        compiler_params=pltpu.CompilerParams(vmem_limit_bytes=VMEM_LIMIT, dimension_semantics=("arbitrary",)),
        name="experts",
    )(block_e, block_valid, n_active, xs, w_gate, w_up, w_down)


def _combine_kernel(yk_ref, x1_ref, h2_ref, gate_ref, mod_ref, wsg_ref, wsu_ref, wsd_ref, fg_ref, *out_refs):
    out_ref = out_refs[-1]
    hb = _unpack_words(h2_ref[...]).astype(BF16)
    hid = _silu(_dot(hb, wsg_ref[...])) * _dot(hb, wsu_ref[...])
    ffn = _dot(hid.astype(BF16), wsd_ref[...])
    gate = gate_ref[...]
    for k in range(TOP_K):
        ffn = ffn + gate[:, k:k + 1] * _unpack_words(yk_ref[k])
    x2 = x1_ref[...] + mod_ref[0, 5:6, :] * ffn
    ms = jnp.mean(x2 * x2, axis=-1, keepdims=True)
    out_ref[...] = x2 * lax.rsqrt(ms + EPS) * fg_ref[...]


def _combine(yk, token0, out_token0, n_out_tokens, prev_out, x1, h2p, gate_tk, mod, wsg_bf, wsu_bf, wsd_bf, final_g,
             seq_len):
    t, d = n_out_tokens, x1.shape[1]
    tiles_per_seq = seq_len // TF
    tile0 = token0 // TF
    out_tile0 = out_token0 // TF
    in_tok = pl.BlockSpec((TF, d), lambda i: (tile0 + i, 0))
    tok = pl.BlockSpec((TF, d), lambda i: (out_tile0 + i, 0))
    in_specs = [pl.BlockSpec((TOP_K, TF, d // 2), lambda i: (0, i, 0)),
                in_tok, pl.BlockSpec((TF, d // 2), lambda i: (tile0 + i, 0)),
                pl.BlockSpec((TF, TOP_K), lambda i: (tile0 + i, 0)),
                pl.BlockSpec((1, 6, d), lambda i: ((out_tile0 + i) // tiles_per_seq, 0, 0)),
                pl.BlockSpec((d, D_SHARED), lambda i: (0, 0)),
                pl.BlockSpec((d, D_SHARED), lambda i: (0, 0)),
                pl.BlockSpec((D_SHARED, d), lambda i: (0, 0)),
                pl.BlockSpec((1, d), lambda i: (0, 0))]
    args = [yk, x1, h2p, gate_tk, mod, wsg_bf, wsu_bf, wsd_bf, final_g]
    aliases = {}
    if prev_out is not None:
        in_specs.append(pl.BlockSpec(memory_space=pl.ANY))
        args.append(prev_out)
        aliases = {len(args) - 1: 0}
    return pl.pallas_call(
        _combine_kernel,
        out_shape=jax.ShapeDtypeStruct((t, d), F32),
        grid=(yk.shape[1] // TF,),
        in_specs=in_specs,
        out_specs=tok,
        input_output_aliases=aliases,
        compiler_params=pltpu.CompilerParams(vmem_limit_bytes=VMEM_LIMIT),
        name="combine",
    )(*args)


def _rope_tables(n_tokens):
    n_rows = n_tokens // GRID_W
    n_freq = HEAD_DIM // 4
    inv_freq = ROPE_THETA ** (-jnp.arange(n_freq, dtype=F32) / n_freq)
    ang_r = jnp.arange(n_rows).astype(F32)[:, None] * inv_freq[None, :]
    ang_c = jnp.arange(GRID_W).astype(F32)[:, None] * inv_freq[None, :]

    def per_token(row_part, col_part):
        rows = jnp.broadcast_to(row_part[:, None, :], (n_rows, GRID_W, n_freq))
        cols = jnp.broadcast_to(col_part[None, :, :], (n_rows, GRID_W, n_freq))
        return rows.reshape(n_tokens, n_freq), cols.reshape(n_tokens, n_freq)

    cos_r, cos_c = per_token(jnp.cos(ang_r), jnp.cos(ang_c))
    sin_r, sin_c = per_token(jnp.sin(ang_r), jnp.sin(ang_c))
    cos = jnp.concatenate([cos_r, cos_r, cos_c, cos_c], axis=1)
    sin = jnp.concatenate([-sin_r, sin_r, -sin_c, sin_c], axis=1)
    reps = LANES // HEAD_DIM
    return jnp.tile(cos, (1, reps)), jnp.tile(sin, (1, reps))


def _pool_bands():
    i = jnp.arange(QB)[:, None]
    r = jnp.arange(POOL_SLAB)[None, :]
    return jnp.stack([((r >= i + POOL_OFF - w // 2) & (r < i + POOL_OFF + w // 2)).astype(BF16)
                      for w in POOL_WINDOWS])


def kernel(x, c, ctx, c_ctx, w_ada, b_ada, norm1_g, norm2_g, w_in, attn_sink, pool_w, pool_scale, w_out,
           w_router, router_bias, w_gate, w_up, w_down, ws_gate, ws_up, ws_down, final_g):
    b, s, d = x.shape
    t = b * s
    assert w_ada.shape[0] == 1 and d == D_MODEL and s % TQ == 0 and b + 1 <= 8

    c8 = jnp.zeros((8, d), F32).at[:b].set(c).at[b].set(c_ctx)
    mod = _ada(c8, w_ada[0], b_ada[0]).reshape(8, 6, d)
    g1 = norm1_g[0].reshape(1, d)
    g2 = norm2_g[0].reshape(1, d)
    w_in_bf = w_in[0].astype(BF16)
    cos_t, sin_t = _rope_tables(s)

    q, k4, v4, p = _inproj(x, mod, g1, w_in_bf, cos_t, sin_t)
    kc4, vc4 = _ctxproj(ctx, mod[b:b + 1], g1, w_in_bf[:, ATTN_WIDTH:ATTN_WIDTH + 2 * KV_WIDTH])

    wr_t = w_router[0].T
    wr_hi = wr_t.astype(BF16)
    wr_lo = (wr_t - wr_hi.astype(F32)).astype(BF16)
    attn_consts = (_pool_bands(), pool_w[0].astype(BF16), pool_scale[0].reshape(1, POOL_WIDTH),
                   w_out[0].astype(BF16), g2, wr_hi, wr_lo)
    tri = jnp.triu(jnp.ones((TR, TR), BF16), k=1)
    shared_w = (ws_gate[0].astype(BF16), ws_up[0].astype(BF16), ws_down[0].astype(BF16))

    nb = b // TOKEN_GROUPS
    tg = nb * s
    assert b % TOKEN_GROUPS == 0 and tg % (COMBINE_CHUNKS * SC_WORKERS * SC_TOKENS) == 0
    n_rows = -(-(tg * TOP_K + N_EXPERTS * (TM_EXP - 1)) // TM_EXP) * TM_EXP
    groups = []
    for g in range(TOKEN_GROUPS):
        x1, h2p, lg_t = _attn(g * nb, nb, attn_sink[0], q, k4, v4, kc4, vc4, p, x, mod, *attn_consts)
        idx_kt, gate_kt, rank_kt, counts = _route(lg_t, router_bias[0].reshape(N_EXPERTS, 1), tri)
        pad_start, block_e, block_valid, n_active = _plan_blocks(counts[:, 0].astype(jnp.int32), n_rows // TM_EXP)
        dest_kt = _dest_rows(pad_start, idx_kt, rank_kt)
        xs = _sc_scatter(dest_kt, h2p, n_rows)
        groups.append((x1.reshape(tg, d), h2p, gate_kt.T, dest_kt, xs, block_e, block_valid, n_active))

    out = None
    chunk = tg // COMBINE_CHUNKS
    for g, (x1, h2p, gate_tk, dest_kt, xs, block_e, block_valid, n_active) in enumerate(groups):
        ys = _experts(block_e, block_valid, n_active, xs, w_gate[0], w_up[0], w_down[0])
        for token0 in range(0, tg, chunk):
            out = _combine(_sc_gather(dest_kt, ys, token0, chunk), token0, g * tg + token0, t, out, x1, h2p,
                           gate_tk, mod, *shared_w, final_g.reshape(1, d), s)
    return out.reshape(b, s, d)
```

```python
import functools

import jax
import jax.numpy as jnp
from jax import lax
from jax.experimental import pallas as pl
from jax.experimental.pallas import tpu as pltpu
from jax.experimental.pallas import tpu_sc as plsc

F32 = jnp.float32
BF16 = jnp.bfloat16

D_MODEL = 1024
GRID_W = 64
N_HEADS = 8
N_KV_HEADS = 2
HEAD_DIM = 64
ATTN_WIDTH = N_HEADS * HEAD_DIM
KV_WIDTH = N_KV_HEADS * HEAD_DIM
WINDOW = 128
ROPE_THETA = 10000.0
POOL_WINDOWS = (2, 4, 8, 16)
POOL_WIDTH = D_MODEL - ATTN_WIDTH
POOL_GROUP_DIM = POOL_WIDTH // len(POOL_WINDOWS)
IN_COLS = ATTN_WIDTH + 2 * KV_WIDTH + POOL_WIDTH
N_EXPERTS = 64
TOP_K = 8
N_EXPERT_GROUPS = 8
EXPERTS_PER_GROUP = N_EXPERTS // N_EXPERT_GROUPS
TOPK_GROUPS = 4
D_EXPERT = 256
D_SHARED = 256
ROUTED_SCALE = 2.5
EPS = 1e-6
LOG2E = 1.4426950408889634

LANES = 128
U32 = jnp.uint32
VMEM_LIMIT = 48 * 1024 * 1024

TM_PROJ = 1024
TQ = 1024
QB = 128
POOL_SLAB = 256
POOL_OFF = 64
TR = 512
TM_EXP = 1024
EXP_MIN_ROWS = 256
EXP_SLOTS = 4
TF = 512
TOKEN_GROUPS = 2
COMBINE_CHUNKS = 2
SC_CORES = 2
SC_WORKERS = 32
SC_TOKENS = 128


def _silu(x):
    return x * (1.0 / (1.0 + jnp.exp(-x)))


def _split_bf16(x):
    hi = x.astype(BF16)
    lo = (x - hi.astype(F32)).astype(BF16)
    return hi, lo


def _dot(a, b):
    return jnp.dot(a, b, preferred_element_type=F32)


def _pack_words(val):
    half = val.shape[1] // 2
    lo = lax.bitcast_convert_type(val[:, :half].astype(BF16).astype(F32), U32)
    hi = lax.bitcast_convert_type(val[:, half:].astype(BF16).astype(F32), U32)
    return lax.shift_right_logical(lo, jnp.uint32(16)) | hi


def _unpack_words(words):
    lo = lax.bitcast_convert_type(lax.shift_left(words, jnp.uint32(16)), F32)
    hi = lax.bitcast_convert_type(words & jnp.uint32(0xFFFF0000), F32)
    return jnp.concatenate([lo, hi], axis=1)


def _dot_nt(a, b):
    return lax.dot_general(a, b, (((1,), (1,)), ((), ())), preferred_element_type=F32)


def _ada_kernel(c_ref, w_ref, b_ref, o_ref):
    a_hi, a_lo = _split_bf16(_silu(c_ref[...]))
    w_hi, w_lo = _split_bf16(w_ref[...])
    o_ref[...] = _dot(a_hi, w_hi) + _dot(a_lo, w_hi) + _dot(a_hi, w_lo) + b_ref[...]


def _ada(c8, w_ada, b_ada):
    d = c8.shape[1]
    n = w_ada.shape[1]
    tn = 512
    return pl.pallas_call(
        _ada_kernel,
        out_shape=jax.ShapeDtypeStruct((8, n), F32),
        grid=(n // tn,),
        in_specs=[pl.BlockSpec((8, d), lambda j: (0, 0)),
                  pl.BlockSpec((d, tn), lambda j: (0, j)),
                  pl.BlockSpec((1, tn), lambda j: (0, j))],
        out_specs=pl.BlockSpec((8, tn), lambda j: (0, j)),
        compiler_params=pltpu.CompilerParams(vmem_limit_bytes=VMEM_LIMIT),
        name="ada",
    )(c8, w_ada, b_ada.reshape(1, n))


def _norm_mod(x, g, shift, scale):
    ms = jnp.mean(x * x, axis=-1, keepdims=True)
    return (x * lax.rsqrt(ms + EPS) * g) * (1.0 + scale) + shift


def _lane_variants(t):
    lane = lax.broadcasted_iota(jnp.int32, t.shape, 1)
    lo = lane < HEAD_DIM
    tr = pltpu.roll(t, HEAD_DIM, 1)
    zero = jnp.zeros_like(t)
    return (jnp.where(lo, t, zero), jnp.where(lo, zero, tr),
            jnp.where(lo, tr, zero), jnp.where(lo, zero, t))


def _store_variants(ref, t):
    for i, var in enumerate(_lane_variants(t)):
        ref[0, :, i * LANES:(i + 1) * LANES] = var.astype(BF16)


def _inproj_kernel(x_ref, mod_ref, g_ref, w_ref, cos_ref, sin_ref, q_ref, k_ref, v_ref, p_ref):
    h = _norm_mod(x_ref[0], g_ref[...], mod_ref[0, 0:1, :], mod_ref[0, 1:2, :])
    z = _dot(h.astype(BF16), w_ref[...])
    cos = cos_ref[...]
    sin = sin_ref[...]
    lane = lax.broadcasted_iota(jnp.int32, cos.shape, 1)
    first_half = (lane & 16) == 0

    def rope(zc):
        partner = jnp.where(first_half, pltpu.roll(zc, LANES - 16, 1), pltpu.roll(zc, 16, 1))
        return zc * cos + partner * sin

    scale = HEAD_DIM ** -0.5 * LOG2E
    for c in range(ATTN_WIDTH // LANES):
        q_ref[0, :, c * LANES:(c + 1) * LANES] = (rope(z[:, c * LANES:(c + 1) * LANES]) * scale).astype(BF16)
    _store_variants(k_ref, rope(z[:, ATTN_WIDTH:ATTN_WIDTH + KV_WIDTH]))
    _store_variants(v_ref, z[:, ATTN_WIDTH + KV_WIDTH:ATTN_WIDTH + 2 * KV_WIDTH])
    p_ref[0] = z[:, ATTN_WIDTH + 2 * KV_WIDTH:]


def _inproj(x, mod, g1, w_in_bf, cos_t, sin_t):
    b, s, d = x.shape
    tm = TM_PROJ
    return pl.pallas_call(
        _inproj_kernel,
        out_shape=(jax.ShapeDtypeStruct((b, s, ATTN_WIDTH), BF16),
                   jax.ShapeDtypeStruct((b, s, 4 * LANES), BF16),
                   jax.ShapeDtypeStruct((b, s, 4 * LANES), BF16),
                   jax.ShapeDtypeStruct((b, s, POOL_WIDTH), F32)),
        grid=(s // tm, b),
        in_specs=[pl.BlockSpec((1, tm, d), lambda n, bi: (bi, n, 0)),
                  pl.BlockSpec((1, 6, d), lambda n, bi: (bi, 0, 0)),
                  pl.BlockSpec((1, d), lambda n, bi: (0, 0)),
                  pl.BlockSpec((d, IN_COLS), lambda n, bi: (0, 0)),
                  pl.BlockSpec((tm, LANES), lambda n, bi: (n, 0)),
                  pl.BlockSpec((tm, LANES), lambda n, bi: (n, 0))],
        out_specs=(pl.BlockSpec((1, tm, ATTN_WIDTH), lambda n, bi: (bi, n, 0)),
                   pl.BlockSpec((1, tm, 4 * LANES), lambda n, bi: (bi, n, 0)),
                   pl.BlockSpec((1, tm, 4 * LANES), lambda n, bi: (bi, n, 0)),
                   pl.BlockSpec((1, tm, POOL_WIDTH), lambda n, bi: (bi, n, 0))),
        compiler_params=pltpu.CompilerParams(vmem_limit_bytes=VMEM_LIMIT),
        name="inproj",
    )(x, mod, g1, w_in_bf, cos_t, sin_t)


def _ctxproj_kernel(x_ref, mod_ref, g_ref, w_ref, k_ref, v_ref):
    h = _norm_mod(x_ref[0], g_ref[...], mod_ref[0, 0:1, :], mod_ref[0, 1:2, :])
    z = _dot(h.astype(BF16), w_ref[...])
    _store_variants(k_ref, z[:, :KV_WIDTH])
    _store_variants(v_ref, z[:, KV_WIDTH:])


def _ctxproj(ctx, mod_c, g1, w_kv_bf):
    b, c, d = ctx.shape
    return pl.pallas_call(
        _ctxproj_kernel,
        out_shape=(jax.ShapeDtypeStruct((b, c, 4 * LANES), BF16),
                   jax.ShapeDtypeStruct((b, c, 4 * LANES), BF16)),
        grid=(b,),
        in_specs=[pl.BlockSpec((1, c, d), lambda bi: (bi, 0, 0)),
                  pl.BlockSpec((1, 6, d), lambda bi: (0, 0, 0)),
                  pl.BlockSpec((1, d), lambda bi: (0, 0)),
                  pl.BlockSpec((d, 2 * KV_WIDTH), lambda bi: (0, 0))],
        out_specs=(pl.BlockSpec((1, c, 4 * LANES), lambda bi: (bi, 0, 0)),
                   pl.BlockSpec((1, c, 4 * LANES), lambda bi: (bi, 0, 0))),
        compiler_params=pltpu.CompilerParams(vmem_limit_bytes=VMEM_LIMIT),
        name="ctxproj",
    )(ctx, mod_c, g1, w_kv_bf)


def _fold(op, tiles):
    while len(tiles) > 1:
        tiles = [op(tiles[i], tiles[i + 1]) if i + 1 < len(tiles) else tiles[i] for i in range(0, len(tiles), 2)]
    return tiles[0]


def _stack_variants(t4, kv):
    return jnp.concatenate([t4[:, (2 * kv) * LANES:(2 * kv + 1) * LANES],
                            t4[:, (2 * kv + 1) * LANES:(2 * kv + 2) * LANES]], axis=0)


def _attn_kernel(seq_len, sink_ref, q_ref, k_ref, kp_ref, kn_ref, v_ref, vp_ref, vn_ref, kc_ref, vc_ref,
                 p_ref, pp_ref, pn_ref, x_ref, mod_ref, band_ref, poolw_ref, pscale_ref, wout_ref,
                 g2_ref, wrh_ref, wrl_ref, x1_ref, h2_ref, lg_ref, kwin, vwin, pext, mix, s_scr, p_scr, m_scr):
    n = pl.program_id(1)
    n_last = pl.num_programs(1) - 1

    kwin[0:QB, :] = kp_ref[0]
    kwin[QB:QB + TQ, :] = k_ref[0]
    kwin[QB + TQ:, :] = kn_ref[0]
    vwin[0:QB, :] = vp_ref[0]
    vwin[QB:QB + TQ, :] = v_ref[0]
    vwin[QB + TQ:, :] = vn_ref[0]

    pext[0:QB - 8, :] = jnp.zeros((QB - 8, POOL_WIDTH), F32)
    pext[QB - 8:QB, :] = jnp.where(n > 0, pp_ref[0], 0.0)
    pext[QB:QB + TQ, :] = p_ref[0]
    pext[QB + TQ:QB + TQ + 8, :] = jnp.where(n < n_last, pn_ref[0], 0.0)
    pext[QB + TQ + 8:, :] = jnp.zeros((QB - 8, POOL_WIDTH), F32)

    row = lax.broadcasted_iota(jnp.int32, (QB, 3 * QB), 0)
    col = lax.broadcasted_iota(jnp.int32, (QB, 3 * QB), 1)
    in_band = (col >= row) & (col <= row + 2 * WINDOW)
    tok = lax.broadcasted_iota(jnp.int32, (QB, 1), 0)
    kc = kc_ref[0]
    kc_rows = [_stack_variants(kc, kv) for kv in range(N_KV_HEADS)]
    vc_rows = [_stack_variants(vc_ref[0], kv) for kv in range(N_KV_HEADS)]

    def sub_block(j, carry):
        r0 = pl.multiple_of(j * QB, QB)
        qj = q_ref[0, pl.ds(r0, QB), :]
        kw = kwin[pl.ds(r0, 3 * QB), :]
        vw = vwin[pl.ds(r0, 3 * QB), :]
        kpos = col + (n * TQ + j * QB - QB)
        ok = in_band & (kpos >= 0) & (kpos < seq_len)
        bias = jnp.where(ok, 0.0, -jnp.inf)
        n_loc, n_ctx = 3 * QB, kc.shape[0]
        bias2 = jnp.concatenate([bias, bias], axis=1)
        k_rows = [_stack_variants(kw, kv) for kv in range(N_KV_HEADS)]
        v_rows = [_stack_variants(vw, kv) for kv in range(N_KV_HEADS)]
        group = N_HEADS // N_KV_HEADS

        def head_tiles(head):
            loc0 = (head % 2) * n_loc
            ctx0 = 2 * n_loc + (head % 2) * n_ctx
            return ([loc0 + i * LANES for i in range(n_loc // LANES)]
                    + [ctx0 + i * LANES for i in range(n_ctx // LANES)])

        for c in range(N_HEADS // 2):
            qc = qj[:, c * LANES:(c + 1) * LANES]
            s_scr[c, :, 0:2 * n_loc] = _dot_nt(qc, k_rows[2 * c // group]) + bias2
            s_scr[c, :, 2 * n_loc:] = _dot_nt(qc, kc_rows[2 * c // group])
        for head in range(N_HEADS):
            tiles = [s_scr[head // 2, :, st:st + LANES] for st in head_tiles(head)]
            row_max = jnp.max(_fold(jnp.maximum, tiles), axis=1, keepdims=True)
            m_scr[head] = jnp.broadcast_to(jnp.maximum(row_max, sink_ref[head] * LOG2E), (QB, LANES))
        for head in range(N_HEADS):
            m = m_scr[head]
            acc = None
            for st in head_tiles(head):
                p = jnp.exp2(s_scr[head // 2, :, st:st + LANES] - m)
                p_scr[head // 2, :, st:st + LANES] = p.astype(BF16)
                acc = p if acc is None else acc + p
            denom = (jnp.broadcast_to(jnp.sum(acc, axis=1, keepdims=True), (QB, LANES))
                     + jnp.exp2(sink_ref[head] * LOG2E - m))
            m_scr[head] = 1.0 / denom
        lane = lax.broadcasted_iota(jnp.int32, (QB, LANES), 1)
        for c in range(N_HEADS // 2):
            o = (_dot(p_scr[c, :, 0:2 * n_loc], v_rows[2 * c // group])
                 + _dot(p_scr[c, :, 2 * n_loc:], vc_rows[2 * c // group]))
            o = o * jnp.where(lane < HEAD_DIM, m_scr[2 * c], m_scr[2 * c + 1])
            mix[pl.ds(r0, QB), c * LANES:(c + 1) * LANES] = o.astype(BF16)

        slab = pext[pl.ds(pl.multiple_of(r0 + POOL_OFF, 8), POOL_SLAB), :]
        tpos = tok + (n * TQ + j * QB)
        for g, w in enumerate(POOL_WINDOWS):
            sg = slab[:, g * LANES:(g + 1) * LANES]
            hi, lo = _split_bf16(sg)
            band = band_ref[g]
            wsum = _dot(band, hi) + _dot(band, lo)
            cnt = (jnp.minimum(tpos - w // 2 + w, seq_len) - jnp.maximum(tpos - w // 2, 0)).astype(F32)
            dlt = wsum / cnt - sg[POOL_OFF:POOL_OFF + QB, :]
            y = _dot(dlt.astype(BF16), poolw_ref[g]) * pscale_ref[:, g * LANES:(g + 1) * LANES]
            mix[pl.ds(r0, QB), ATTN_WIDTH + g * LANES:ATTN_WIDTH + (g + 1) * LANES] = y.astype(BF16)
        return carry

    lax.fori_loop(0, TQ // QB, sub_block, 0)

    proj = _dot(mix[...], wout_ref[...])
    x1 = x_ref[0] + mod_ref[0, 2:3, :] * proj
    x1_ref[0] = x1
    h2 = _norm_mod(x1, g2_ref[...], mod_ref[0, 3:4, :], mod_ref[0, 4:5, :])
    h2_ref[...] = _pack_words(h2)
    h_hi, h_lo = _split_bf16(h2)
    wrh = wrh_ref[...]
    lg_ref[...] = _dot_nt(wrh, h_hi) + _dot_nt(wrh, h_lo) + _dot_nt(wrl_ref[...], h_hi)


def _attn(b0, b, sink, q, k4, v4, kc4, vc4, p, x, mod, band, poolw_bf, pscale, wout_bf, g2, wr_hi, wr_lo):
    _, s, d = x.shape
    c = kc4.shape[1]
    nt = s // TQ
    hb = TQ // QB
    pb = TQ // 8
    kv_main = pl.BlockSpec((1, TQ, 4 * LANES), lambda bi, n: (b0 + bi, n, 0))
    kv_prev = pl.BlockSpec((1, QB, 4 * LANES), lambda bi, n: (b0 + bi, jnp.maximum(n * hb - 1, 0), 0))
    kv_next = pl.BlockSpec((1, QB, 4 * LANES), lambda bi, n: (b0 + bi, jnp.minimum((n + 1) * hb, s // QB - 1), 0))
    const2 = lambda bi, n: (0, 0)
    const3 = lambda bi, n: (0, 0, 0)
    return pl.pallas_call(
        functools.partial(_attn_kernel, s),
        out_shape=(jax.ShapeDtypeStruct((b, s, d), F32),
                   jax.ShapeDtypeStruct((b * s, d // 2), U32),
                   jax.ShapeDtypeStruct((N_EXPERTS, b * s), F32)),
        grid=(b, nt),
        in_specs=[pl.BlockSpec(memory_space=pltpu.SMEM),
                  pl.BlockSpec((1, TQ, ATTN_WIDTH), lambda bi, n: (b0 + bi, n, 0)),
                  kv_main, kv_prev, kv_next, kv_main, kv_prev, kv_next,
                  pl.BlockSpec((1, c, 4 * LANES), lambda bi, n: (b0 + bi, 0, 0)),
                  pl.BlockSpec((1, c, 4 * LANES), lambda bi, n: (b0 + bi, 0, 0)),
                  pl.BlockSpec((1, TQ, POOL_WIDTH), lambda bi, n: (b0 + bi, n, 0)),
                  pl.BlockSpec((1, 8, POOL_WIDTH), lambda bi, n: (b0 + bi, jnp.maximum(n * pb - 1, 0), 0)),
                  pl.BlockSpec((1, 8, POOL_WIDTH),
                               lambda bi, n: (b0 + bi, jnp.minimum((n + 1) * pb, s // 8 - 1), 0)),
                  pl.BlockSpec((1, TQ, d), lambda bi, n: (b0 + bi, n, 0)),
                  pl.BlockSpec((1, 6, d), lambda bi, n: (b0 + bi, 0, 0)),
                  pl.BlockSpec((len(POOL_WINDOWS), QB, POOL_SLAB), const3),
                  pl.BlockSpec((len(POOL_WINDOWS), POOL_GROUP_DIM, POOL_GROUP_DIM), const3),
                  pl.BlockSpec((1, POOL_WIDTH), const2),
                  pl.BlockSpec((d, d), const2),
                  pl.BlockSpec((1, d), const2),
                  pl.BlockSpec((N_EXPERTS, d), const2),
                  pl.BlockSpec((N_EXPERTS, d), const2)],
        out_specs=(pl.BlockSpec((1, TQ, d), lambda bi, n: (bi, n, 0)),
                   pl.BlockSpec((TQ, d // 2), lambda bi, n: (bi * nt + n, 0)),
                   pl.BlockSpec((N_EXPERTS, TQ), lambda bi, n: (0, bi * nt + n))),
        scratch_shapes=[pltpu.VMEM((TQ + 2 * QB, 4 * LANES), BF16),
                        pltpu.VMEM((TQ + 2 * QB, 4 * LANES), BF16),
                        pltpu.VMEM((TQ + 2 * QB, POOL_WIDTH), F32),
                        pltpu.VMEM((TQ, d), BF16),
                        pltpu.VMEM((N_HEADS // 2, QB, 2 * (3 * QB + c)), F32),
                        pltpu.VMEM((N_HEADS // 2, QB, 2 * (3 * QB + c)), BF16),
                        pltpu.VMEM((N_HEADS, QB, LANES), F32)],
        compiler_params=pltpu.CompilerParams(vmem_limit_bytes=VMEM_LIMIT),
        name="attn",
    )(sink, q, k4, k4, k4, v4, v4, v4, kc4, vc4, p, p, p, x, mod, band, poolw_bf, pscale, wout_bf,
      g2, wr_hi, wr_lo)


def _first_argmax_rows(v, row_iota, n_rows):
    m = jnp.max(v, axis=0, keepdims=True)
    idx = jnp.min(jnp.where(v == m, row_iota, n_rows), axis=0, keepdims=True)
    return m, idx


def _route_kernel(lg_ref, bias_ref, tri_ref, idx_ref, gate_ref, rank_ref, cnt_ref, carry):
    i = pl.program_id(0)

    @pl.when(i == 0)
    def _():
        carry[...] = jnp.zeros_like(carry)

    scores = 1.0 / (1.0 + jnp.exp(-lg_ref[...]))
    biased = scores + bias_ref[...]
    e_iota = lax.broadcasted_iota(jnp.int32, scores.shape, 0).astype(F32)
    g_iota = lax.broadcasted_iota(jnp.int32, (EXPERTS_PER_GROUP, TR), 0).astype(F32)
    neg = -jnp.inf

    grp = []
    for g in range(N_EXPERT_GROUPS):
        blk = biased[g * EXPERTS_PER_GROUP:(g + 1) * EXPERTS_PER_GROUP, :]
        m1, i1 = _first_argmax_rows(blk, g_iota, float(EXPERTS_PER_GROUP))
        m2 = jnp.max(jnp.where(g_iota == i1, neg, blk), axis=0, keepdims=True)
        grp.append(m1 + m2)
    grp = jnp.concatenate(grp, axis=0)
    gg_iota = lax.broadcasted_iota(jnp.int32, grp.shape, 0).astype(F32)
    grp_sel = jnp.zeros(grp.shape, F32)
    for _ in range(TOPK_GROUPS):
        _, gi = _first_argmax_rows(grp, gg_iota, float(N_EXPERT_GROUPS))
        hit = gg_iota == gi
        grp_sel = jnp.where(hit, 1.0, grp_sel)
        grp = jnp.where(hit, neg, grp)
    allowed = jnp.concatenate(
        [jnp.broadcast_to(grp_sel[g:g + 1, :], (EXPERTS_PER_GROUP, TR)) for g in range(N_EXPERT_GROUPS)], axis=0)
    masked = jnp.where(allowed > 0.5, biased, neg)

    idxs, gates = [], []
    onehot = jnp.zeros(scores.shape, F32)
    for _ in range(TOP_K):
        _, ei = _first_argmax_rows(masked, e_iota, float(N_EXPERTS))
        hit = e_iota == ei
        idxs.append(ei)
        gates.append(jnp.sum(jnp.where(hit, scores, 0.0), axis=0, keepdims=True))
        onehot = jnp.where(hit, 1.0, onehot)
        masked = jnp.where(hit, neg, masked)
    idx = jnp.concatenate(idxs, axis=0)
    gate = jnp.concatenate(gates, axis=0)
    gate = gate / jnp.sum(gate, axis=0, keepdims=True) * ROUTED_SCALE

    before = _dot(onehot.astype(BF16), tri_ref[...]) + carry[:, 0:1]
    ranks = [jnp.sum(jnp.where(e_iota == idxs[k], before, 0.0), axis=0, keepdims=True) for k in range(TOP_K)]
    idx_ref[...] = idx.astype(jnp.int32)
    gate_ref[...] = gate
    rank_ref[...] = jnp.concatenate(ranks, axis=0).astype(jnp.int32)
    total = carry[...] + jnp.sum(onehot, axis=1, keepdims=True)
    carry[...] = total
    cnt_ref[...] = total


def _route(lg_t, bias, tri):
    e, t = lg_t.shape
    tok = pl.BlockSpec((TOP_K, TR), lambda i: (0, i))
    return pl.pallas_call(
        _route_kernel,
        out_shape=(jax.ShapeDtypeStruct((TOP_K, t), jnp.int32),
                   jax.ShapeDtypeStruct((TOP_K, t), F32),
                   jax.ShapeDtypeStruct((TOP_K, t), jnp.int32),
                   jax.ShapeDtypeStruct((e, LANES), F32)),
        grid=(t // TR,),
        in_specs=[pl.BlockSpec((e, TR), lambda i: (0, i)),
                  pl.BlockSpec((e, 1), lambda i: (0, 0)),
                  pl.BlockSpec((TR, TR), lambda i: (0, 0))],
        out_specs=(tok, tok, tok, pl.BlockSpec((e, LANES), lambda i: (0, 0))),
        scratch_shapes=[pltpu.VMEM((e, LANES), F32)],
        compiler_params=pltpu.CompilerParams(vmem_limit_bytes=VMEM_LIMIT),
        name="route",
    )(lg_t, bias, tri)


def _plan_kernel(n_blocks, size_ref, start_ref, expert_ref, valid_ref, nact_ref):
    def per_expert(e, first_block):
        size = size_ref[e]
        n_blk = (size + TM_EXP - 1) // TM_EXP
        start_ref[e] = first_block * TM_EXP

        def per_block(j, carry):
            expert_ref[first_block + j] = e
            valid_ref[first_block + j] = jnp.minimum(size - j * TM_EXP, TM_EXP)
            return carry

        lax.fori_loop(0, n_blk, per_block, 0)
        return first_block + n_blk

    n_active = lax.fori_loop(0, N_EXPERTS, per_expert, 0)
    nact_ref[0] = n_active

    def unused(i, carry):
        expert_ref[i] = N_EXPERTS - 1
        valid_ref[i] = 0
        return carry

    lax.fori_loop(n_active, n_blocks, unused, 0)


def _plan_blocks(sizes, n_blocks):
    smem = pl.BlockSpec(memory_space=pltpu.SMEM)
    return pl.pallas_call(
        functools.partial(_plan_kernel, n_blocks),
        out_shape=(jax.ShapeDtypeStruct((N_EXPERTS,), jnp.int32),
                   jax.ShapeDtypeStruct((n_blocks,), jnp.int32),
                   jax.ShapeDtypeStruct((n_blocks,), jnp.int32),
                   jax.ShapeDtypeStruct((1,), jnp.int32)),
        in_specs=[smem],
        out_specs=(smem, smem, smem, smem),
        name="plan_blocks",
    )(sizes)


def _dest_kernel(start_ref, idx_ref, rank_ref, dest_ref):
    idx = idx_ref[...]
    dest = rank_ref[...]
    for e in range(N_EXPERTS):
        dest = dest + jnp.where(idx == e, start_ref[e], 0)
    dest_ref[...] = dest


def _dest_rows(pad_start, idx_kt, rank_kt):
    n_k, t = idx_kt.shape
    tile = 4096
    blk = pl.BlockSpec((n_k, tile), lambda i: (0, i))
    return pl.pallas_call(
        _dest_kernel,
        out_shape=jax.ShapeDtypeStruct((n_k, t), jnp.int32),
        grid=(t // tile,),
        in_specs=[pl.BlockSpec(memory_space=pltpu.SMEM), blk, blk],
        out_specs=blk,
        name="dest_rows",
    )(pad_start, idx_kt, rank_kt)


def _sc_mesh():
    return plsc.VectorSubcoreMesh(core_axis_name="c", subcore_axis_name="s")


def _sc_token_base(steps, j):
    worker = lax.axis_index("s") * SC_CORES + lax.axis_index("c")
    return (worker * steps + j) * SC_TOKENS


def _sc_scatter(dest_kt, h2p, n_rows):
    t, width = h2p.shape
    steps = t // (SC_WORKERS * SC_TOKENS)

    @functools.partial(
        pl.kernel, mesh=_sc_mesh(),
        out_type=jax.ShapeDtypeStruct((n_rows, width), U32),
        scratch_types=[pltpu.VMEM((TOP_K, SC_TOKENS), jnp.int32),
                       pltpu.VMEM((SC_TOKENS, width), U32),
                       pltpu.SemaphoreType.DMA],
        name="sc_scatter",
    )
    def body(dest_hbm, h_hbm, xs_hbm, idx_v, rows_v, sem):
        @pl.loop(0, steps)
        def _(j):
            base = _sc_token_base(steps, j)
            pltpu.sync_copy(dest_hbm.at[:, pl.ds(base, SC_TOKENS)], idx_v)
            pltpu.sync_copy(h_hbm.at[pl.ds(base, SC_TOKENS)], rows_v)
            copies = [pltpu.async_copy(rows_v, xs_hbm.at[idx_v.at[k]], sem) for k in range(TOP_K)]
            for cp in copies:
                cp.wait()

    return body(dest_kt, h2p)


def _sc_gather(dest_kt, ys, token0, n_tokens):
    n_k = dest_kt.shape[0]
    width = ys.shape[1]
    steps = n_tokens // (SC_WORKERS * SC_TOKENS)

    half = SC_TOKENS // 2
    units = [(k, h) for k in range(n_k) for h in range(2)]
    n_buf = 3

    @functools.partial(
        pl.kernel, mesh=_sc_mesh(),
        out_type=jax.ShapeDtypeStruct((n_k, n_tokens, width), U32),
        scratch_types=[pltpu.VMEM((n_k, SC_TOKENS), jnp.int32),
                       pltpu.VMEM((n_buf, half, width), U32),
                       pltpu.SemaphoreType.DMA((n_buf,))],
        name="sc_gather",
    )
    def body(dest_hbm, ys_hbm, yk_hbm, idx_v, rows_v, sems):
        @pl.loop(0, steps)
        def _(j):
            base = _sc_token_base(steps, j)
            pltpu.sync_copy(dest_hbm.at[:, pl.ds(token0 + base, SC_TOKENS)], idx_v)

            def gather(u):
                k, h = units[u]
                slot = u % n_buf
                return pltpu.make_async_copy(ys_hbm.at[idx_v.at[k, pl.ds(h * half, half)]], rows_v.at[slot],
                                             sems.at[slot])

            gather(0).start()
            gather(1).start()
            for u, (k, h) in enumerate(units):
                gather(u).wait()
                pltpu.sync_copy(rows_v.at[u % n_buf], yk_hbm.at[k, pl.ds(base + h * half, half)])
                if u + 2 < len(units):
                    gather(u + 2).start()

    return body(dest_kt, ys)


def _experts_kernel(be_ref, valid_ref, nact_ref, xs_hbm, wg_ref, wu_ref, wd_ref, ys_ref, wg_bf, wu_bf, wd_bf,
                    xbuf, sems):
    i = pl.program_id(0)
    n_active = nact_ref[0]
    slot = lax.rem(i, EXP_SLOTS)

    def fetch(j):
        src = xs_hbm.at[pl.ds(pl.multiple_of(j * TM_EXP, TM_EXP), TM_EXP), :]
        s = lax.rem(j, EXP_SLOTS)
        return pltpu.make_async_copy(src, xbuf.at[s], sems.at[s])

    @pl.when(i == 0)
    def _():
        for j in range(EXP_SLOTS - 1):
            pl.when(j < n_active)(lambda j=j: fetch(j).start())

    @pl.when(i + (EXP_SLOTS - 1) < n_active)
    def _():
        fetch(i + (EXP_SLOTS - 1)).start()

    prev = be_ref[jnp.maximum(i - 1, 0)]

    @pl.when((i == 0) | (be_ref[i] != prev))
    def _():
        wg_bf[...] = wg_ref[0].astype(BF16)
        wu_bf[...] = wu_ref[0].astype(BF16)
        wd_bf[...] = wd_ref[0].astype(BF16)

    @pl.when(i < n_active)
    def _():
        fetch(i).wait()

    valid = jnp.where(i < n_active, valid_ref[i], 0)

    def run(rows):
        row = lax.broadcasted_iota(jnp.int32, (rows, xbuf.shape[2]), 0)
        words = jnp.where(row < valid, xbuf[slot, 0:rows, :], jnp.uint32(0))
        xb = _unpack_words(words).astype(BF16)
        hid = _silu(_dot(xb, wg_bf[...])) * _dot(xb, wu_bf[...])
        ys_ref[0:rows, :] = _pack_words(_dot(hid.astype(BF16), wd_bf[...]))
        if rows < TM_EXP:
            ys_ref[rows:, :] = jnp.zeros((TM_EXP - rows, ys_ref.shape[1]), U32)

    rows, lower = TM_EXP, TM_EXP // 2
    while rows >= EXP_MIN_ROWS:
        lo = lower if rows > EXP_MIN_ROWS else 0
        pl.when((valid > lo) & (valid <= rows))(functools.partial(run, rows))
        rows, lower = rows // 2, lower // 2

    @pl.when(valid == 0)
    def _():
        ys_ref[...] = jnp.zeros_like(ys_ref)


def _experts(block_e, block_valid, n_active, xs, w_gate, w_up, w_down):
    n_rows, half = xs.shape
    d = w_gate.shape[1]
    nb = n_rows // TM_EXP
    grid_spec = pltpu.PrefetchScalarGridSpec(
        num_scalar_prefetch=3,
        grid=(nb,),
        in_specs=[pl.BlockSpec(memory_space=pl.ANY),
                  pl.BlockSpec((1, d, D_EXPERT), lambda i, be, bv, na: (be[i], 0, 0)),
                  pl.BlockSpec((1, d, D_EXPERT), lambda i, be, bv, na: (be[i], 0, 0)),
                  pl.BlockSpec((1, D_EXPERT, d), lambda i, be, bv, na: (be[i], 0, 0))],
        out_specs=pl.BlockSpec((TM_EXP, half), lambda i, be, bv, na: (i, 0)),
        scratch_shapes=[pltpu.VMEM((d, D_EXPERT), BF16), pltpu.VMEM((d, D_EXPERT), BF16),
                        pltpu.VMEM((D_EXPERT, d), BF16),
                        pltpu.VMEM((EXP_SLOTS, TM_EXP, half), U32),
                        pltpu.SemaphoreType.DMA((EXP_SLOTS,))],
    )
    return pl.pallas_call(
        _experts_kernel,
        out_shape=jax.ShapeDtypeStruct((n_rows, half), U32),
        grid_spec=grid_spec,
        compiler_params=pltpu.CompilerParams(vmem_limit_bytes=VMEM_LIMIT, dimension_semantics=("arbitrary",)),
        name="experts",
    )(block_e, block_valid, n_active, xs, w_gate, w_up, w_down)


def _combine_kernel(yk_ref, x1_ref, h2_ref, gate_ref, mod_ref, wsg_ref, wsu_ref, wsd_ref, fg_ref, *out_refs):
    out_ref = out_refs[-1]
    hb = _unpack_words(h2_ref[...]).astype(BF16)
    hid = _silu(_dot(hb, wsg_ref[...])) * _dot(hb, wsu_ref[...])
    ffn = _dot(hid.astype(BF16), wsd_ref[...])
    gate = gate_ref[...]
    for k in range(TOP_K):
        ffn = ffn + gate[:, k:k + 1] * _unpack_words(yk_ref[k])
    x2 = x1_ref[...] + mod_ref[0, 5:6, :] * ffn
    ms = jnp.mean(x2 * x2, axis=-1, keepdims=True)
    out_ref[...] = x2 * lax.rsqrt(ms + EPS) * fg_ref[...]


def _combine(yk, token0, out_token0, n_out_tokens, prev_out, x1, h2p, gate_tk, mod, wsg_bf, wsu_bf, wsd_bf, final_g,
             seq_len):
    t, d = n_out_tokens, x1.shape[1]
    tiles_per_seq = seq_len // TF
    tile0 = token0 // TF
    out_tile0 = out_token0 // TF
    in_tok = pl.BlockSpec((TF, d), lambda i: (tile0 + i, 0))
    tok = pl.BlockSpec((TF, d), lambda i: (out_tile0 + i, 0))
    in_specs = [pl.BlockSpec((TOP_K, TF, d // 2), lambda i: (0, i, 0)),
                in_tok, pl.BlockSpec((TF, d // 2), lambda i: (tile0 + i, 0)),
                pl.BlockSpec((TF, TOP_K), lambda i: (tile0 + i, 0)),
                pl.BlockSpec((1, 6, d), lambda i: ((out_tile0 + i) // tiles_per_seq, 0, 0)),
                pl.BlockSpec((d, D_SHARED), lambda i: (0, 0)),
                pl.BlockSpec((d, D_SHARED), lambda i: (0, 0)),
                pl.BlockSpec((D_SHARED, d), lambda i: (0, 0)),
                pl.BlockSpec((1, d), lambda i: (0, 0))]
    args = [yk, x1, h2p, gate_tk, mod, wsg_bf, wsu_bf, wsd_bf, final_g]
    aliases = {}
    if prev_out is not None:
        in_specs.append(pl.BlockSpec(memory_space=pl.ANY))
        args.append(prev_out)
        aliases = {len(args) - 1: 0}
    return pl.pallas_call(
        _combine_kernel,
        out_shape=jax.ShapeDtypeStruct((t, d), F32),
        grid=(yk.shape[1] // TF,),
        in_specs=in_specs,
        out_specs=tok,
        input_output_aliases=aliases,
        compiler_params=pltpu.CompilerParams(vmem_limit_bytes=VMEM_LIMIT),
        name="combine",
    )(*args)


def _rope_tables(n_tokens):
    n_rows = n_tokens // GRID_W
    n_freq = HEAD_DIM // 4
    inv_freq = ROPE_THETA ** (-jnp.arange(n_freq, dtype=F32) / n_freq)
    ang_r = jnp.arange(n_rows).astype(F32)[:, None] * inv_freq[None, :]
    ang_c = jnp.arange(GRID_W).astype(F32)[:, None] * inv_freq[None, :]

    def per_token(row_part, col_part):
        rows = jnp.broadcast_to(row_part[:, None, :], (n_rows, GRID_W, n_freq))
        cols = jnp.broadcast_to(col_part[None, :, :], (n_rows, GRID_W, n_freq))
        return rows.reshape(n_tokens, n_freq), cols.reshape(n_tokens, n_freq)

    cos_r, cos_c = per_token(jnp.cos(ang_r), jnp.cos(ang_c))
    sin_r, sin_c = per_token(jnp.sin(ang_r), jnp.sin(ang_c))
    cos = jnp.concatenate([cos_r, cos_r, cos_c, cos_c], axis=1)
    sin = jnp.concatenate([-sin_r, sin_r, -sin_c, sin_c], axis=1)
    reps = LANES // HEAD_DIM
    return jnp.tile(cos, (1, reps)), jnp.tile(sin, (1, reps))


def _pool_bands():
    i = jnp.arange(QB)[:, None]
    r = jnp.arange(POOL_SLAB)[None, :]
    return jnp.stack([((r >= i + POOL_OFF - w // 2) & (r < i + POOL_OFF + w // 2)).astype(BF16)
                      for w in POOL_WINDOWS])


def kernel(x, c, ctx, c_ctx, w_ada, b_ada, norm1_g, norm2_g, w_in, attn_sink, pool_w, pool_scale, w_out,
           w_router, router_bias, w_gate, w_up, w_down, ws_gate, ws_up, ws_down, final_g):
    b, s, d = x.shape
    t = b * s
    assert w_ada.shape[0] == 1 and d == D_MODEL and s % TQ == 0 and b + 1 <= 8

    c8 = jnp.zeros((8, d), F32).at[:b].set(c).at[b].set(c_ctx)
    mod = _ada(c8, w_ada[0], b_ada[0]).reshape(8, 6, d)
    g1 = norm1_g[0].reshape(1, d)
    g2 = norm2_g[0].reshape(1, d)
    w_in_bf = w_in[0].astype(BF16)
    cos_t, sin_t = _rope_tables(s)

    q, k4, v4, p = _inproj(x, mod, g1, w_in_bf, cos_t, sin_t)
    kc4, vc4 = _ctxproj(ctx, mod[b:b + 1], g1, w_in_bf[:, ATTN_WIDTH:ATTN_WIDTH + 2 * KV_WIDTH])

    wr_t = w_router[0].T
    wr_hi = wr_t.astype(BF16)
    wr_lo = (wr_t - wr_hi.astype(F32)).astype(BF16)
    attn_consts = (_pool_bands(), pool_w[0].astype(BF16), pool_scale[0].reshape(1, POOL_WIDTH),
                   w_out[0].astype(BF16), g2, wr_hi, wr_lo)
    tri = jnp.triu(jnp.ones((TR, TR), BF16), k=1)
    shared_w = (ws_gate[0].astype(BF16), ws_up[0].astype(BF16), ws_down[0].astype(BF16))

    nb = b // TOKEN_GROUPS
    tg = nb * s
    assert b % TOKEN_GROUPS == 0 and tg % (COMBINE_CHUNKS * SC_WORKERS * SC_TOKENS) == 0
    n_rows = -(-(tg * TOP_K + N_EXPERTS * (TM_EXP - 1)) // TM_EXP) * TM_EXP
    groups = []
    for g in range(TOKEN_GROUPS):
        x1, h2p, lg_t = _attn(g * nb, nb, attn_sink[0], q, k4, v4, kc4, vc4, p, x, mod, *attn_consts)
        idx_kt, gate_kt, rank_kt, counts = _route(lg_t, router_bias[0].reshape(N_EXPERTS, 1), tri)
        pad_start, block_e, block_valid, n_active = _plan_blocks(counts[:, 0].astype(jnp.int32), n_rows // TM_EXP)
        dest_kt = _dest_rows(pad_start, idx_kt, rank_kt)
        xs = _sc_scatter(dest_kt, h2p, n_rows)
        groups.append((x1.reshape(tg, d), h2p, gate_kt.T, dest_kt, xs, block_e, block_valid, n_active))

    out = None
    chunk = tg // COMBINE_CHUNKS
    for g, (x1, h2p, gate_tk, dest_kt, xs, block_e, block_valid, n_active) in enumerate(groups):
        ys = _experts(block_e, block_valid, n_active, xs, w_gate[0], w_up[0], w_down[0])
        for token0 in range(0, tg, chunk):
            out = _combine(_sc_gather(dest_kt, ys, token0, chunk), token0, g * tg + token0, t, out, x1, h2p,
                           gate_tk, mod, *shared_w, final_g.reshape(1, d), s)
    return out.reshape(b, s, d)
```

```python
import functools

import jax
import jax.numpy as jnp
from jax import lax
from jax.experimental import pallas as pl
from jax.experimental.pallas import tpu as pltpu
from jax.experimental.pallas import tpu_sc as plsc

F32 = jnp.float32
BF16 = jnp.bfloat16

D_MODEL = 1024
GRID_W = 64
N_HEADS = 8
N_KV_HEADS = 2
HEAD_DIM = 64
ATTN_WIDTH = N_HEADS * HEAD_DIM
KV_WIDTH = N_KV_HEADS * HEAD_DIM
WINDOW = 128
ROPE_THETA = 10000.0
POOL_WINDOWS = (2, 4, 8, 16)
POOL_WIDTH = D_MODEL - ATTN_WIDTH
POOL_GROUP_DIM = POOL_WIDTH // len(POOL_WINDOWS)
IN_COLS = ATTN_WIDTH + 2 * KV_WIDTH + POOL_WIDTH
N_EXPERTS = 64
TOP_K = 8
N_EXPERT_GROUPS = 8
EXPERTS_PER_GROUP = N_EXPERTS // N_EXPERT_GROUPS
TOPK_GROUPS = 4
D_EXPERT = 256
D_SHARED = 256
ROUTED_SCALE = 2.5
EPS = 1e-6
LOG2E = 1.4426950408889634

LANES = 128
U32 = jnp.uint32
VMEM_LIMIT = 48 * 1024 * 1024

TM_PROJ = 1024
TQ = 1024
QB = 128
POOL_SLAB = 256
POOL_OFF = 64
TR = 512
TM_EXP = 1024
EXP_MIN_ROWS = 256
EXP_SLOTS = 4
TF = 512
TOKEN_GROUPS = 2
COMBINE_CHUNKS = 4
SC_CORES = 2
SC_WORKERS = 32
SC_TOKENS = 128


def _silu(x):
    return x * (1.0 / (1.0 + jnp.exp(-x)))


def _split_bf16(x):
    hi = x.astype(BF16)
    lo = (x - hi.astype(F32)).astype(BF16)
    return hi, lo


def _dot(a, b):
    return jnp.dot(a, b, preferred_element_type=F32)


def _pack_words(val):
    half = val.shape[1] // 2
    lo = lax.bitcast_convert_type(val[:, :half].astype(BF16).astype(F32), U32)
    hi = lax.bitcast_convert_type(val[:, half:].astype(BF16).astype(F32), U32)
    return lax.shift_right_logical(lo, jnp.uint32(16)) | hi


def _unpack_words(words):
    lo = lax.bitcast_convert_type(lax.shift_left(words, jnp.uint32(16)), F32)
    hi = lax.bitcast_convert_type(words & jnp.uint32(0xFFFF0000), F32)
    return jnp.concatenate([lo, hi], axis=1)


def _dot_nt(a, b):
    return lax.dot_general(a, b, (((1,), (1,)), ((), ())), preferred_element_type=F32)


def _ada_kernel(c_ref, w_ref, b_ref, o_ref):
    a_hi, a_lo = _split_bf16(_silu(c_ref[...]))
    w_hi, w_lo = _split_bf16(w_ref[...])
    o_ref[...] = _dot(a_hi, w_hi) + _dot(a_lo, w_hi) + _dot(a_hi, w_lo) + b_ref[...]


def _ada(c8, w_ada, b_ada):
    d = c8.shape[1]
    n = w_ada.shape[1]
    tn = 512
    return pl.pallas_call(
        _ada_kernel,
        out_shape=jax.ShapeDtypeStruct((8, n), F32),
        grid=(n // tn,),
        in_specs=[pl.BlockSpec((8, d), lambda j: (0, 0)),
                  pl.BlockSpec((d, tn), lambda j: (0, j)),
                  pl.BlockSpec((1, tn), lambda j: (0, j))],
        out_specs=pl.BlockSpec((8, tn), lambda j: (0, j)),
        compiler_params=pltpu.CompilerParams(vmem_limit_bytes=VMEM_LIMIT),
        name="ada",
    )(c8, w_ada, b_ada.reshape(1, n))


def _norm_mod(x, g, shift, scale):
    ms = jnp.mean(x * x, axis=-1, keepdims=True)
    return (x * lax.rsqrt(ms + EPS) * g) * (1.0 + scale) + shift


def _lane_variants(t):
    lane = lax.broadcasted_iota(jnp.int32, t.shape, 1)
    lo = lane < HEAD_DIM
    tr = pltpu.roll(t, HEAD_DIM, 1)
    zero = jnp.zeros_like(t)
    return (jnp.where(lo, t, zero), jnp.where(lo, zero, tr),
            jnp.where(lo, tr, zero), jnp.where(lo, zero, t))


def _store_variants(ref, t):
    for i, var in enumerate(_lane_variants(t)):
        ref[0, :, i * LANES:(i + 1) * LANES] = var.astype(BF16)


def _inproj_kernel(x_ref, mod_ref, g_ref, w_ref, cos_ref, sin_ref, q_ref, k_ref, v_ref, p_ref):
    h = _norm_mod(x_ref[0], g_ref[...], mod_ref[0, 0:1, :], mod_ref[0, 1:2, :])
    z = _dot(h.astype(BF16), w_ref[...])
    cos = cos_ref[...]
    sin = sin_ref[...]
    lane = lax.broadcasted_iota(jnp.int32, cos.shape, 1)
    first_half = (lane & 16) == 0

    def rope(zc):
        partner = jnp.where(first_half, pltpu.roll(zc, LANES - 16, 1), pltpu.roll(zc, 16, 1))
        return zc * cos + partner * sin

    scale = HEAD_DIM ** -0.5 * LOG2E
    for c in range(ATTN_WIDTH // LANES):
        q_ref[0, :, c * LANES:(c + 1) * LANES] = (rope(z[:, c * LANES:(c + 1) * LANES]) * scale).astype(BF16)
    _store_variants(k_ref, rope(z[:, ATTN_WIDTH:ATTN_WIDTH + KV_WIDTH]))
    _store_variants(v_ref, z[:, ATTN_WIDTH + KV_WIDTH:ATTN_WIDTH + 2 * KV_WIDTH])
    p_ref[0] = z[:, ATTN_WIDTH + 2 * KV_WIDTH:]


def _inproj(x, mod, g1, w_in_bf, cos_t, sin_t):
    b, s, d = x.shape
    tm = TM_PROJ
    return pl.pallas_call(
        _inproj_kernel,
        out_shape=(jax.ShapeDtypeStruct((b, s, ATTN_WIDTH), BF16),
                   jax.ShapeDtypeStruct((b, s, 4 * LANES), BF16),
                   jax.ShapeDtypeStruct((b, s, 4 * LANES), BF16),
                   jax.ShapeDtypeStruct((b, s, POOL_WIDTH), F32)),
        grid=(s // tm, b),
        in_specs=[pl.BlockSpec((1, tm, d), lambda n, bi: (bi, n, 0)),
                  pl.BlockSpec((1, 6, d), lambda n, bi: (bi, 0, 0)),
                  pl.BlockSpec((1, d), lambda n, bi: (0, 0)),
                  pl.BlockSpec((d, IN_COLS), lambda n, bi: (0, 0)),
                  pl.BlockSpec((tm, LANES), lambda n, bi: (n, 0)),
                  pl.BlockSpec((tm, LANES), lambda n, bi: (n, 0))],
        out_specs=(pl.BlockSpec((1, tm, ATTN_WIDTH), lambda n, bi: (bi, n, 0)),
                   pl.BlockSpec((1, tm, 4 * LANES), lambda n, bi: (bi, n, 0)),
                   pl.BlockSpec((1, tm, 4 * LANES), lambda n, bi: (bi, n, 0)),
                   pl.BlockSpec((1, tm, POOL_WIDTH), lambda n, bi: (bi, n, 0))),
        compiler_params=pltpu.CompilerParams(vmem_limit_bytes=VMEM_LIMIT),
        name="inproj",
    )(x, mod, g1, w_in_bf, cos_t, sin_t)


def _ctxproj_kernel(x_ref, mod_ref, g_ref, w_ref, k_ref, v_ref):
    h = _norm_mod(x_ref[0], g_ref[...], mod_ref[0, 0:1, :], mod_ref[0, 1:2, :])
    z = _dot(h.astype(BF16), w_ref[...])
    _store_variants(k_ref, z[:, :KV_WIDTH])
    _store_variants(v_ref, z[:, KV_WIDTH:])


def _ctxproj(ctx, mod_c, g1, w_kv_bf):
    b, c, d = ctx.shape
    return pl.pallas_call(
        _ctxproj_kernel,
        out_shape=(jax.ShapeDtypeStruct((b, c, 4 * LANES), BF16),
                   jax.ShapeDtypeStruct((b, c, 4 * LANES), BF16)),
        grid=(b,),
        in_specs=[pl.BlockSpec((1, c, d), lambda bi: (bi, 0, 0)),
                  pl.BlockSpec((1, 6, d), lambda bi: (0, 0, 0)),
                  pl.BlockSpec((1, d), lambda bi: (0, 0)),
                  pl.BlockSpec((d, 2 * KV_WIDTH), lambda bi: (0, 0))],
        out_specs=(pl.BlockSpec((1, c, 4 * LANES), lambda bi: (bi, 0, 0)),
                   pl.BlockSpec((1, c, 4 * LANES), lambda bi: (bi, 0, 0))),
        compiler_params=pltpu.CompilerParams(vmem_limit_bytes=VMEM_LIMIT),
        name="ctxproj",
    )(ctx, mod_c, g1, w_kv_bf)


def _fold(op, tiles):
    while len(tiles) > 1:
        tiles = [op(tiles[i], tiles[i + 1]) if i + 1 < len(tiles) else tiles[i] for i in range(0, len(tiles), 2)]
    return tiles[0]


def _stack_variants(t4, kv):
    return jnp.concatenate([t4[:, (2 * kv) * LANES:(2 * kv + 1) * LANES],
                            t4[:, (2 * kv + 1) * LANES:(2 * kv + 2) * LANES]], axis=0)


def _attn_kernel(seq_len, sink_ref, q_ref, k_ref, kp_ref, kn_ref, v_ref, vp_ref, vn_ref, kc_ref, vc_ref,
                 p_ref, pp_ref, pn_ref, x_ref, mod_ref, band_ref, poolw_ref, pscale_ref, wout_ref,
                 g2_ref, wrh_ref, wrl_ref, x1_ref, h2_ref, lg_ref, kwin, vwin, pext, mix, s_scr, p_scr, m_scr):
    n = pl.program_id(1)
    n_last = pl.num_programs(1) - 1

    kwin[0:QB, :] = kp_ref[0]
    kwin[QB:QB + TQ, :] = k_ref[0]
    kwin[QB + TQ:, :] = kn_ref[0]
    vwin[0:QB, :] = vp_ref[0]
    vwin[QB:QB + TQ, :] = v_ref[0]
    vwin[QB + TQ:, :] = vn_ref[0]

    pext[0:QB - 8, :] = jnp.zeros((QB - 8, POOL_WIDTH), F32)
    pext[QB - 8:QB, :] = jnp.where(n > 0, pp_ref[0], 0.0)
    pext[QB:QB + TQ, :] = p_ref[0]
    pext[QB + TQ:QB + TQ + 8, :] = jnp.where(n < n_last, pn_ref[0], 0.0)
    pext[QB + TQ + 8:, :] = jnp.zeros((QB - 8, POOL_WIDTH), F32)

    row = lax.broadcasted_iota(jnp.int32, (QB, 3 * QB), 0)
    col = lax.broadcasted_iota(jnp.int32, (QB, 3 * QB), 1)
    in_band = (col >= row) & (col <= row + 2 * WINDOW)
    tok = lax.broadcasted_iota(jnp.int32, (QB, 1), 0)
    kc = kc_ref[0]
    kc_rows = [_stack_variants(kc, kv) for kv in range(N_KV_HEADS)]
    vc_rows = [_stack_variants(vc_ref[0], kv) for kv in range(N_KV_HEADS)]

    def sub_block(j, carry):
        r0 = pl.multiple_of(j * QB, QB)
        qj = q_ref[0, pl.ds(r0, QB), :]
        kw = kwin[pl.ds(r0, 3 * QB), :]
        vw = vwin[pl.ds(r0, 3 * QB), :]
        kpos = col + (n * TQ + j * QB - QB)
        ok = in_band & (kpos >= 0) & (kpos < seq_len)
        bias = jnp.where(ok, 0.0, -jnp.inf)
        n_loc, n_ctx = 3 * QB, kc.shape[0]
        bias2 = jnp.concatenate([bias, bias], axis=1)
        k_rows = [_stack_variants(kw, kv) for kv in range(N_KV_HEADS)]
        v_rows = [_stack_variants(vw, kv) for kv in range(N_KV_HEADS)]
        group = N_HEADS // N_KV_HEADS

        def head_tiles(head):
            loc0 = (head % 2) * n_loc
            ctx0 = 2 * n_loc + (head % 2) * n_ctx
            return ([loc0 + i * LANES for i in range(n_loc // LANES)]
                    + [ctx0 + i * LANES for i in range(n_ctx // LANES)])

        for c in range(N_HEADS // 2):
            qc = qj[:, c * LANES:(c + 1) * LANES]
            s_scr[c, :, 0:2 * n_loc] = _dot_nt(qc, k_rows[2 * c // group]) + bias2
            s_scr[c, :, 2 * n_loc:] = _dot_nt(qc, kc_rows[2 * c // group])
        for head in range(N_HEADS):
            tiles = [s_scr[head // 2, :, st:st + LANES] for st in head_tiles(head)]
            row_max = jnp.max(_fold(jnp.maximum, tiles), axis=1, keepdims=True)
            m_scr[head] = jnp.broadcast_to(jnp.maximum(row_max, sink_ref[head] * LOG2E), (QB, LANES))
        for head in range(N_HEADS):
            m = m_scr[head]
            acc = None
            for st in head_tiles(head):
                p = jnp.exp2(s_scr[head // 2, :, st:st + LANES] - m)
                p_scr[head // 2, :, st:st + LANES] = p.astype(BF16)
                acc = p if acc is None else acc + p
            denom = (jnp.broadcast_to(jnp.sum(acc, axis=1, keepdims=True), (QB, LANES))
                     + jnp.exp2(sink_ref[head] * LOG2E - m))
            m_scr[head] = 1.0 / denom
        lane = lax.broadcasted_iota(jnp.int32, (QB, LANES), 1)
        for c in range(N_HEADS // 2):
            o = (_dot(p_scr[c, :, 0:2 * n_loc], v_rows[2 * c // group])
                 + _dot(p_scr[c, :, 2 * n_loc:], vc_rows[2 * c // group]))
            o = o * jnp.where(lane < HEAD_DIM, m_scr[2 * c], m_scr[2 * c + 1])
            mix[pl.ds(r0, QB), c * LANES:(c + 1) * LANES] = o.astype(BF16)

        slab = pext[pl.ds(pl.multiple_of(r0 + POOL_OFF, 8), POOL_SLAB), :]
        tpos = tok + (n * TQ + j * QB)
        for g, w in enumerate(POOL_WINDOWS):
            sg = slab[:, g * LANES:(g + 1) * LANES]
            hi, lo = _split_bf16(sg)
            band = band_ref[g]
            wsum = _dot(band, hi) + _dot(band, lo)
            cnt = (jnp.minimum(tpos - w // 2 + w, seq_len) - jnp.maximum(tpos - w // 2, 0)).astype(F32)
            dlt = wsum / cnt - sg[POOL_OFF:POOL_OFF + QB, :]
            y = _dot(dlt.astype(BF16), poolw_ref[g]) * pscale_ref[:, g * LANES:(g + 1) * LANES]
            mix[pl.ds(r0, QB), ATTN_WIDTH + g * LANES:ATTN_WIDTH + (g + 1) * LANES] = y.astype(BF16)
        return carry

    lax.fori_loop(0, TQ // QB, sub_block, 0)

    proj = _dot(mix[...], wout_ref[...])
    x1 = x_ref[0] + mod_ref[0, 2:3, :] * proj
    x1_ref[0] = x1
    h2 = _norm_mod(x1, g2_ref[...], mod_ref[0, 3:4, :], mod_ref[0, 4:5, :])
    h2_ref[...] = _pack_words(h2)
    h_hi, h_lo = _split_bf16(h2)
    wrh = wrh_ref[...]
    lg_ref[...] = _dot_nt(wrh, h_hi) + _dot_nt(wrh, h_lo) + _dot_nt(wrl_ref[...], h_hi)


def _attn(b0, b, sink, q, k4, v4, kc4, vc4, p, x, mod, band, poolw_bf, pscale, wout_bf, g2, wr_hi, wr_lo):
    _, s, d = x.shape
    c = kc4.shape[1]
    nt = s // TQ
    hb = TQ // QB
    pb = TQ // 8
    kv_main = pl.BlockSpec((1, TQ, 4 * LANES), lambda bi, n: (b0 + bi, n, 0))
    kv_prev = pl.BlockSpec((1, QB, 4 * LANES), lambda bi, n: (b0 + bi, jnp.maximum(n * hb - 1, 0), 0))
    kv_next = pl.BlockSpec((1, QB, 4 * LANES), lambda bi, n: (b0 + bi, jnp.minimum((n + 1) * hb, s // QB - 1), 0))
    const2 = lambda bi, n: (0, 0)
    const3 = lambda bi, n: (0, 0, 0)
    return pl.pallas_call(
        functools.partial(_attn_kernel, s),
        out_shape=(jax.ShapeDtypeStruct((b, s, d), F32),
                   jax.ShapeDtypeStruct((b * s, d // 2), U32),
                   jax.ShapeDtypeStruct((N_EXPERTS, b * s), F32)),
        grid=(b, nt),
        in_specs=[pl.BlockSpec(memory_space=pltpu.SMEM),
                  pl.BlockSpec((1, TQ, ATTN_WIDTH), lambda bi, n: (b0 + bi, n, 0)),
                  kv_main, kv_prev, kv_next, kv_main, kv_prev, kv_next,
                  pl.BlockSpec((1, c, 4 * LANES), lambda bi, n: (b0 + bi, 0, 0)),
                  pl.BlockSpec((1, c, 4 * LANES), lambda bi, n: (b0 + bi, 0, 0)),
                  pl.BlockSpec((1, TQ, POOL_WIDTH), lambda bi, n: (b0 + bi, n, 0)),
                  pl.BlockSpec((1, 8, POOL_WIDTH), lambda bi, n: (b0 + bi, jnp.maximum(n * pb - 1, 0), 0)),
                  pl.BlockSpec((1, 8, POOL_WIDTH),
                               lambda bi, n: (b0 + bi, jnp.minimum((n + 1) * pb, s // 8 - 1), 0)),
                  pl.BlockSpec((1, TQ, d), lambda bi, n: (b0 + bi, n, 0)),
                  pl.BlockSpec((1, 6, d), lambda bi, n: (b0 + bi, 0, 0)),
                  pl.BlockSpec((len(POOL_WINDOWS), QB, POOL_SLAB), const3),
                  pl.BlockSpec((len(POOL_WINDOWS), POOL_GROUP_DIM, POOL_GROUP_DIM), const3),
                  pl.BlockSpec((1, POOL_WIDTH), const2),
                  pl.BlockSpec((d, d), const2),
                  pl.BlockSpec((1, d), const2),
                  pl.BlockSpec((N_EXPERTS, d), const2),
                  pl.BlockSpec((N_EXPERTS, d), const2)],
        out_specs=(pl.BlockSpec((1, TQ, d), lambda bi, n: (bi, n, 0)),
                   pl.BlockSpec((TQ, d // 2), lambda bi, n: (bi * nt + n, 0)),
                   pl.BlockSpec((N_EXPERTS, TQ), lambda bi, n: (0, bi * nt + n))),
        scratch_shapes=[pltpu.VMEM((TQ + 2 * QB, 4 * LANES), BF16),
                        pltpu.VMEM((TQ + 2 * QB, 4 * LANES), BF16),
                        pltpu.VMEM((TQ + 2 * QB, POOL_WIDTH), F32),
                        pltpu.VMEM((TQ, d), BF16),
                        pltpu.VMEM((N_HEADS // 2, QB, 2 * (3 * QB + c)), F32),
                        pltpu.VMEM((N_HEADS // 2, QB, 2 * (3 * QB + c)), BF16),
                        pltpu.VMEM((N_HEADS, QB, LANES), F32)],
        compiler_params=pltpu.CompilerParams(vmem_limit_bytes=VMEM_LIMIT),
        name="attn",
    )(sink, q, k4, k4, k4, v4, v4, v4, kc4, vc4, p, p, p, x, mod, band, poolw_bf, pscale, wout_bf,
      g2, wr_hi, wr_lo)


def _first_argmax_rows(v, row_iota, n_rows):
    m = jnp.max(v, axis=0, keepdims=True)
    idx = jnp.min(jnp.where(v == m, row_iota, n_rows), axis=0, keepdims=True)
    return m, idx


def _route_kernel(lg_ref, bias_ref, tri_ref, idx_ref, gate_ref, rank_ref, cnt_ref, carry):
    i = pl.program_id(0)

    @pl.when(i == 0)
    def _():
        carry[...] = jnp.zeros_like(carry)

    scores = 1.0 / (1.0 + jnp.exp(-lg_ref[...]))
    biased = scores + bias_ref[...]
    e_iota = lax.broadcasted_iota(jnp.int32, scores.shape, 0).astype(F32)
    g_iota = lax.broadcasted_iota(jnp.int32, (EXPERTS_PER_GROUP, TR), 0).astype(F32)
    neg = -jnp.inf

    grp = []
    for g in range(N_EXPERT_GROUPS):
        blk = biased[g * EXPERTS_PER_GROUP:(g + 1) * EXPERTS_PER_GROUP, :]
        m1, i1 = _first_argmax_rows(blk, g_iota, float(EXPERTS_PER_GROUP))
        m2 = jnp.max(jnp.where(g_iota == i1, neg, blk), axis=0, keepdims=True)
        grp.append(m1 + m2)
    grp = jnp.concatenate(grp, axis=0)
    gg_iota = lax.broadcasted_iota(jnp.int32, grp.shape, 0).astype(F32)
    grp_sel = jnp.zeros(grp.shape, F32)
    for _ in range(TOPK_GROUPS):
        _, gi = _first_argmax_rows(grp, gg_iota, float(N_EXPERT_GROUPS))
        hit = gg_iota == gi
        grp_sel = jnp.where(hit, 1.0, grp_sel)
        grp = jnp.where(hit, neg, grp)
    allowed = jnp.concatenate(
        [jnp.broadcast_to(grp_sel[g:g + 1, :], (EXPERTS_PER_GROUP, TR)) for g in range(N_EXPERT_GROUPS)], axis=0)
    masked = jnp.where(allowed > 0.5, biased, neg)

    idxs, gates = [], []
    onehot = jnp.zeros(scores.shape, F32)
    for _ in range(TOP_K):
        _, ei = _first_argmax_rows(masked, e_iota, float(N_EXPERTS))
        hit = e_iota == ei
        idxs.append(ei)
        gates.append(jnp.sum(jnp.where(hit, scores, 0.0), axis=0, keepdims=True))
        onehot = jnp.where(hit, 1.0, onehot)
        masked = jnp.where(hit, neg, masked)
    idx = jnp.concatenate(idxs, axis=0)
    gate = jnp.concatenate(gates, axis=0)
    gate = gate / jnp.sum(gate, axis=0, keepdims=True) * ROUTED_SCALE

    before = _dot(onehot.astype(BF16), tri_ref[...]) + carry[:, 0:1]
    ranks = [jnp.sum(jnp.where(e_iota == idxs[k], before, 0.0), axis=0, keepdims=True) for k in range(TOP_K)]
    idx_ref[...] = idx.astype(jnp.int32)
    gate_ref[...] = gate
    rank_ref[...] = jnp.concatenate(ranks, axis=0).astype(jnp.int32)
    total = carry[...] + jnp.sum(onehot, axis=1, keepdims=True)
    carry[...] = total
    cnt_ref[...] = total


def _route(lg_t, bias, tri):
    e, t = lg_t.shape
    tok = pl.BlockSpec((TOP_K, TR), lambda i: (0, i))
    return pl.pallas_call(
        _route_kernel,
        out_shape=(jax.ShapeDtypeStruct((TOP_K, t), jnp.int32),
                   jax.ShapeDtypeStruct((TOP_K, t), F32),
                   jax.ShapeDtypeStruct((TOP_K, t), jnp.int32),
                   jax.ShapeDtypeStruct((e, LANES), F32)),
        grid=(t // TR,),
        in_specs=[pl.BlockSpec((e, TR), lambda i: (0, i)),
                  pl.BlockSpec((e, 1), lambda i: (0, 0)),
                  pl.BlockSpec((TR, TR), lambda i: (0, 0))],
        out_specs=(tok, tok, tok, pl.BlockSpec((e, LANES), lambda i: (0, 0))),
        scratch_shapes=[pltpu.VMEM((e, LANES), F32)],
        compiler_params=pltpu.CompilerParams(vmem_limit_bytes=VMEM_LIMIT),
        name="route",
    )(lg_t, bias, tri)


def _plan_kernel(n_blocks, size_ref, start_ref, expert_ref, valid_ref, nact_ref):
    def per_expert(e, first_block):
        size = size_ref[e]
        n_blk = (size + TM_EXP - 1) // TM_EXP
        start_ref[e] = first_block * TM_EXP

        def per_block(j, carry):
            expert_ref[first_block + j] = e
            valid_ref[first_block + j] = jnp.minimum(size - j * TM_EXP, TM_EXP)
            return carry

        lax.fori_loop(0, n_blk, per_block, 0)
        return first_block + n_blk

    n_active = lax.fori_loop(0, N_EXPERTS, per_expert, 0)
    nact_ref[0] = n_active

    def unused(i, carry):
        expert_ref[i] = N_EXPERTS - 1
        valid_ref[i] = 0
        return carry

    lax.fori_loop(n_active, n_blocks, unused, 0)


def _plan_blocks(sizes, n_blocks):
    smem = pl.BlockSpec(memory_space=pltpu.SMEM)
    return pl.pallas_call(
        functools.partial(_plan_kernel, n_blocks),
        out_shape=(jax.ShapeDtypeStruct((N_EXPERTS,), jnp.int32),
                   jax.ShapeDtypeStruct((n_blocks,), jnp.int32),
                   jax.ShapeDtypeStruct((n_blocks,), jnp.int32),
                   jax.ShapeDtypeStruct((1,), jnp.int32)),
        in_specs=[smem],
        out_specs=(smem, smem, smem, smem),
        name="plan_blocks",
    )(sizes)


def _dest_kernel(start_ref, idx_ref, rank_ref, dest_ref):
    idx = idx_ref[...]
    dest = rank_ref[...]
    for e in range(N_EXPERTS):
        dest = dest + jnp.where(idx == e, start_ref[e], 0)
    dest_ref[...] = dest


def _dest_rows(pad_start, idx_kt, rank_kt):
    n_k, t = idx_kt.shape
    tile = 4096
    blk = pl.BlockSpec((n_k, tile), lambda i: (0, i))
    return pl.pallas_call(
        _dest_kernel,
        out_shape=jax.ShapeDtypeStruct((n_k, t), jnp.int32),
        grid=(t // tile,),
        in_specs=[pl.BlockSpec(memory_space=pltpu.SMEM), blk, blk],
        out_specs=blk,
        name="dest_rows",
    )(pad_start, idx_kt, rank_kt)


def _sc_mesh():
    return plsc.VectorSubcoreMesh(core_axis_name="c", subcore_axis_name="s")


def _sc_token_base(steps, j):
    worker = lax.axis_index("s") * SC_CORES + lax.axis_index("c")
    return (worker * steps + j) * SC_TOKENS


def _sc_scatter(dest_kt, h2p, n_rows):
    t, width = h2p.shape
    steps = t // (SC_WORKERS * SC_TOKENS)

    @functools.partial(
        pl.kernel, mesh=_sc_mesh(),
        out_type=jax.ShapeDtypeStruct((n_rows, width), U32),
        scratch_types=[pltpu.VMEM((TOP_K, SC_TOKENS), jnp.int32),
                       pltpu.VMEM((SC_TOKENS, width), U32),
                       pltpu.SemaphoreType.DMA],
        name="sc_scatter",
    )
    def body(dest_hbm, h_hbm, xs_hbm, idx_v, rows_v, sem):
        @pl.loop(0, steps)
        def _(j):
            base = _sc_token_base(steps, j)
            pltpu.sync_copy(dest_hbm.at[:, pl.ds(base, SC_TOKENS)], idx_v)
            pltpu.sync_copy(h_hbm.at[pl.ds(base, SC_TOKENS)], rows_v)
            copies = [pltpu.async_copy(rows_v, xs_hbm.at[idx_v.at[k]], sem) for k in range(TOP_K)]
            for cp in copies:
                cp.wait()

    return body(dest_kt, h2p)


def _sc_gather(dest_kt, ys, token0, n_tokens):
    n_k = dest_kt.shape[0]
    width = ys.shape[1]
    steps = n_tokens // (SC_WORKERS * SC_TOKENS)

    half = SC_TOKENS // 2
    units = [(k, h) for k in range(n_k) for h in range(2)]
    n_buf = 3

    @functools.partial(
        pl.kernel, mesh=_sc_mesh(),
        out_type=jax.ShapeDtypeStruct((n_k, n_tokens, width), U32),
        scratch_types=[pltpu.VMEM((n_k, SC_TOKENS), jnp.int32),
                       pltpu.VMEM((n_buf, half, width), U32),
                       pltpu.SemaphoreType.DMA((n_buf,))],
        name="sc_gather",
    )
    def body(dest_hbm, ys_hbm, yk_hbm, idx_v, rows_v, sems):
        @pl.loop(0, steps)
        def _(j):
            base = _sc_token_base(steps, j)
            pltpu.sync_copy(dest_hbm.at[:, pl.ds(token0 + base, SC_TOKENS)], idx_v)

            def gather(u):
                k, h = units[u]
                slot = u % n_buf
                return pltpu.make_async_copy(ys_hbm.at[idx_v.at[k, pl.ds(h * half, half)]], rows_v.at[slot],
                                             sems.at[slot])

            gather(0).start()
            gather(1).start()
            for u, (k, h) in enumerate(units):
                gather(u).wait()
                pltpu.sync_copy(rows_v.at[u % n_buf], yk_hbm.at[k, pl.ds(base + h * half, half)])
                if u + 2 < len(units):
                    gather(u + 2).start()

    return body(dest_kt, ys)


def _experts_kernel(be_ref, valid_ref, nact_ref, xs_hbm, wg_ref, wu_ref, wd_ref, ys_ref, wg_bf, wu_bf, wd_bf,
                    xbuf, sems):
    i = pl.program_id(0)
    n_active = nact_ref[0]
    slot = lax.rem(i, EXP_SLOTS)

    def fetch(j):
        src = xs_hbm.at[pl.ds(pl.multiple_of(j * TM_EXP, TM_EXP), TM_EXP), :]
        s = lax.rem(j, EXP_SLOTS)
        return pltpu.make_async_copy(src, xbuf.at[s], sems.at[s])

    @pl.when(i == 0)
    def _():
        for j in range(EXP_SLOTS - 1):
            pl.when(j < n_active)(lambda j=j: fetch(j).start())

    @pl.when(i + (EXP_SLOTS - 1) < n_active)
    def _():
        fetch(i + (EXP_SLOTS - 1)).start()

    prev = be_ref[jnp.maximum(i - 1, 0)]

    @pl.when((i == 0) | (be_ref[i] != prev))
    def _():
        wg_bf[...] = wg_ref[0].astype(BF16)
        wu_bf[...] = wu_ref[0].astype(BF16)
        wd_bf[...] = wd_ref[0].astype(BF16)

    @pl.when(i < n_active)
    def _():
        fetch(i).wait()

    valid = jnp.where(i < n_active, valid_ref[i], 0)

    def run(rows):
        row = lax.broadcasted_iota(jnp.int32, (rows, xbuf.shape[2]), 0)
        words = jnp.where(row < valid, xbuf[slot, 0:rows, :], jnp.uint32(0))
        xb = _unpack_words(words).astype(BF16)
        hid = _silu(_dot(xb, wg_bf[...])) * _dot(xb, wu_bf[...])
        ys_ref[0:rows, :] = _pack_words(_dot(hid.astype(BF16), wd_bf[...]))
        if rows < TM_EXP:
            ys_ref[rows:, :] = jnp.zeros((TM_EXP - rows, ys_ref.shape[1]), U32)

    rows, lower = TM_EXP, TM_EXP // 2
    while rows >= EXP_MIN_ROWS:
        lo = lower if rows > EXP_MIN_ROWS else 0
        pl.when((valid > lo) & (valid <= rows))(functools.partial(run, rows))
        rows, lower = rows // 2, lower // 2

    @pl.when(valid == 0)
    def _():
        ys_ref[...] = jnp.zeros_like(ys_ref)


def _experts(block_e, block_valid, n_active, xs, w_gate, w_up, w_down):
    n_rows, half = xs.shape
    d = w_gate.shape[1]
    nb = n_rows // TM_EXP
    grid_spec = pltpu.PrefetchScalarGridSpec(
        num_scalar_prefetch=3,
        grid=(nb,),
        in_specs=[pl.BlockSpec(memory_space=pl.ANY),
                  pl.BlockSpec((1, d, D_EXPERT), lambda i, be, bv, na: (be[i], 0, 0)),
                  pl.BlockSpec((1, d, D_EXPERT), lambda i, be, bv, na: (be[i], 0, 0)),
                  pl.BlockSpec((1, D_EXPERT, d), lambda i, be, bv, na: (be[i], 0, 0))],
        out_specs=pl.BlockSpec((TM_EXP, half), lambda i, be, bv, na: (i, 0)),
        scratch_shapes=[pltpu.VMEM((d, D_EXPERT), BF16), pltpu.VMEM((d, D_EXPERT), BF16),
                        pltpu.VMEM((D_EXPERT, d), BF16),
                        pltpu.VMEM((EXP_SLOTS, TM_EXP, half), U32),
                        pltpu.SemaphoreType.DMA((EXP_SLOTS,))],
    )
    return pl.pallas_call(
        _experts_kernel,
        out_shape=jax.ShapeDtypeStruct((n_rows, half), U32),
        grid_spec=grid_spec,
        compiler_params=pltpu.CompilerParams(vmem_limit_bytes=VMEM_LIMIT, dimension_semantics=("arbitrary",)),
        name="experts",
    )(block_e, block_valid, n_active, xs, w_gate, w_up, w_down)


def _combine_kernel(yk_ref, x1_ref, h2_ref, gate_ref, mod_ref, wsg_ref, wsu_ref, wsd_ref, fg_ref, *out_refs):
    out_ref = out_refs[-1]
    hb = _unpack_words(h2_ref[...]).astype(BF16)
    hid = _silu(_dot(hb, wsg_ref[...])) * _dot(hb, wsu_ref[...])
    ffn = _dot(hid.astype(BF16), wsd_ref[...])
    gate = gate_ref[...]
    for k in range(TOP_K):
        ffn = ffn + gate[:, k:k + 1] * _unpack_words(yk_ref[k])
    x2 = x1_ref[...] + mod_ref[0, 5:6, :] * ffn
    ms = jnp.mean(x2 * x2, axis=-1, keepdims=True)
    out_ref[...] = x2 * lax.rsqrt(ms + EPS) * fg_ref[...]


def _combine(yk, token0, out_token0, n_out_tokens, prev_out, x1, h2p, gate_tk, mod, wsg_bf, wsu_bf, wsd_bf, final_g,
             seq_len):
    t, d = n_out_tokens, x1.shape[1]
    tiles_per_seq = seq_len // TF
    tile0 = token0 // TF
    out_tile0 = out_token0 // TF
    in_tok = pl.BlockSpec((TF, d), lambda i: (tile0 + i, 0))
    tok = pl.BlockSpec((TF, d), lambda i: (out_tile0 + i, 0))
    in_specs = [pl.BlockSpec((TOP_K, TF, d // 2), lambda i: (0, i, 0)),
                in_tok, pl.BlockSpec((TF, d // 2), lambda i: (tile0 + i, 0)),
                pl.BlockSpec((TF, TOP_K), lambda i: (tile0 + i, 0)),
                pl.BlockSpec((1, 6, d), lambda i: ((out_tile0 + i) // tiles_per_seq, 0, 0)),
                pl.BlockSpec((d, D_SHARED), lambda i: (0, 0)),
                pl.BlockSpec((d, D_SHARED), lambda i: (0, 0)),
                pl.BlockSpec((D_SHARED, d), lambda i: (0, 0)),
                pl.BlockSpec((1, d), lambda i: (0, 0))]
    args = [yk, x1, h2p, gate_tk, mod, wsg_bf, wsu_bf, wsd_bf, final_g]
    aliases = {}
    if prev_out is not None:
        in_specs.append(pl.BlockSpec(memory_space=pl.ANY))
        args.append(prev_out)
        aliases = {len(args) - 1: 0}
    return pl.pallas_call(
        _combine_kernel,
        out_shape=jax.ShapeDtypeStruct((t, d), F32),
        grid=(yk.shape[1] // TF,),
        in_specs=in_specs,
        out_specs=tok,
        input_output_aliases=aliases,
        compiler_params=pltpu.CompilerParams(vmem_limit_bytes=VMEM_LIMIT),
        name="combine",
    )(*args)


def _rope_tables(n_tokens):
    n_rows = n_tokens // GRID_W
    n_freq = HEAD_DIM // 4
    inv_freq = ROPE_THETA ** (-jnp.arange(n_freq, dtype=F32) / n_freq)
    ang_r = jnp.arange(n_rows).astype(F32)[:, None] * inv_freq[None, :]
    ang_c = jnp.arange(GRID_W).astype(F32)[:, None] * inv_freq[None, :]

    def per_token(row_part, col_part):
        rows = jnp.broadcast_to(row_part[:, None, :], (n_rows, GRID_W, n_freq))
        cols = jnp.broadcast_to(col_part[None, :, :], (n_rows, GRID_W, n_freq))
        return rows.reshape(n_tokens, n_freq), cols.reshape(n_tokens, n_freq)

    cos_r, cos_c = per_token(jnp.cos(ang_r), jnp.cos(ang_c))
    sin_r, sin_c = per_token(jnp.sin(ang_r), jnp.sin(ang_c))
    cos = jnp.concatenate([cos_r, cos_r, cos_c, cos_c], axis=1)
    sin = jnp.concatenate([-sin_r, sin_r, -sin_c, sin_c], axis=1)
    reps = LANES // HEAD_DIM
    return jnp.tile(cos, (1, reps)), jnp.tile(sin, (1, reps))


def _pool_bands():
    i = jnp.arange(QB)[:, None]
    r = jnp.arange(POOL_SLAB)[None, :]
    return jnp.stack([((r >= i + POOL_OFF - w // 2) & (r < i + POOL_OFF + w // 2)).astype(BF16)
                      for w in POOL_WINDOWS])


def kernel(x, c, ctx, c_ctx, w_ada, b_ada, norm1_g, norm2_g, w_in, attn_sink, pool_w, pool_scale, w_out,
           w_router, router_bias, w_gate, w_up, w_down, ws_gate, ws_up, ws_down, final_g):
    b, s, d = x.shape
    t = b * s
    assert w_ada.shape[0] == 1 and d == D_MODEL and s % TQ == 0 and b + 1 <= 8

    c8 = jnp.zeros((8, d), F32).at[:b].set(c).at[b].set(c_ctx)
    mod = _ada(c8, w_ada[0], b_ada[0]).reshape(8, 6, d)
    g1 = norm1_g[0].reshape(1, d)
    g2 = norm2_g[0].reshape(1, d)
    w_in_bf = w_in[0].astype(BF16)
    cos_t, sin_t = _rope_tables(s)

    q, k4, v4, p = _inproj(x, mod, g1, w_in_bf, cos_t, sin_t)
    kc4, vc4 = _ctxproj(ctx, mod[b:b + 1], g1, w_in_bf[:, ATTN_WIDTH:ATTN_WIDTH + 2 * KV_WIDTH])

    wr_t = w_router[0].T
    wr_hi = wr_t.astype(BF16)
    wr_lo = (wr_t - wr_hi.astype(F32)).astype(BF16)
    attn_consts = (_pool_bands(), pool_w[0].astype(BF16), pool_scale[0].reshape(1, POOL_WIDTH),
                   w_out[0].astype(BF16), g2, wr_hi, wr_lo)
    tri = jnp.triu(jnp.ones((TR, TR), BF16), k=1)
    shared_w = (ws_gate[0].astype(BF16), ws_up[0].astype(BF16), ws_down[0].astype(BF16))

    nb = b // TOKEN_GROUPS
    tg = nb * s
    assert b % TOKEN_GROUPS == 0 and tg % (COMBINE_CHUNKS * SC_WORKERS * SC_TOKENS) == 0
    n_rows = -(-(tg * TOP_K + N_EXPERTS * (TM_EXP - 1)) // TM_EXP) * TM_EXP
    groups = []
    for g in range(TOKEN_GROUPS):
        x1, h2p, lg_t = _attn(g * nb, nb, attn_sink[0], q, k4, v4, kc4, vc4, p, x, mod, *attn_consts)
        idx_kt, gate_kt, rank_kt, counts = _route(lg_t, router_bias[0].reshape(N_EXPERTS, 1), tri)
        pad_start, block_e, block_valid, n_active = _plan_blocks(counts[:, 0].astype(jnp.int32), n_rows // TM_EXP)
        dest_kt = _dest_rows(pad_start, idx_kt, rank_kt)
        xs = _sc_scatter(dest_kt, h2p, n_rows)
        groups.append((x1.reshape(tg, d), h2p, gate_kt.T, dest_kt, xs, block_e, block_valid, n_active))

    out = None
    chunk = tg // COMBINE_CHUNKS
    for g, (x1, h2p, gate_tk, dest_kt, xs, block_e, block_valid, n_active) in enumerate(groups):
        ys = _experts(block_e, block_valid, n_active, xs, w_gate[0], w_up[0], w_down[0])
        for token0 in range(0, tg, chunk):
            out = _combine(_sc_gather(dest_kt, ys, token0, chunk), token0, g * tg + token0, t, out, x1, h2p,
                           gate_tk, mod, *shared_w, final_g.reshape(1, d), s)
    return out.reshape(b, s, d)
```

```python
import functools

import jax
import jax.numpy as jnp
from jax import lax
from jax.experimental import pallas as pl
from jax.experimental.pallas import tpu as pltpu
from jax.experimental.pallas import tpu_sc as plsc

F32 = jnp.float32
BF16 = jnp.bfloat16

D_MODEL = 1024
GRID_W = 64
N_HEADS = 8
N_KV_HEADS = 2
HEAD_DIM = 64
ATTN_WIDTH = N_HEADS * HEAD_DIM
KV_WIDTH = N_KV_HEADS * HEAD_DIM
WINDOW = 128
ROPE_THETA = 10000.0
POOL_WINDOWS = (2, 4, 8, 16)
POOL_WIDTH = D_MODEL - ATTN_WIDTH
POOL_GROUP_DIM = POOL_WIDTH // len(POOL_WINDOWS)
IN_COLS = ATTN_WIDTH + 2 * KV_WIDTH + POOL_WIDTH
N_EXPERTS = 64
TOP_K = 8
N_EXPERT_GROUPS = 8
EXPERTS_PER_GROUP = N_EXPERTS // N_EXPERT_GROUPS
TOPK_GROUPS = 4
D_EXPERT = 256
D_SHARED = 256
ROUTED_SCALE = 2.5
EPS = 1e-6
LOG2E = 1.4426950408889634

LANES = 128
U32 = jnp.uint32
VMEM_LIMIT = 48 * 1024 * 1024

TM_PROJ = 1024
TQ = 1024
QB = 128
POOL_SLAB = 256
POOL_OFF = 64
TR = 512
TM_EXP = 1024
EXP_MIN_ROWS = 256
EXP_SLOTS = 4
TF = 512
TOKEN_GROUPS = 2
COMBINE_CHUNKS = 2
SC_CORES = 2
SC_WORKERS = 32
SC_TOKENS = 128


def _silu(x):
    return x * (1.0 / (1.0 + jnp.exp(-x)))


def _split_bf16(x):
    hi = x.astype(BF16)
    lo = (x - hi.astype(F32)).astype(BF16)
    return hi, lo


def _dot(a, b):
    return jnp.dot(a, b, preferred_element_type=F32)


def _pack_words(val):
    half = val.shape[1] // 2
    lo = lax.bitcast_convert_type(val[:, :half].astype(BF16).astype(F32), U32)
    hi = lax.bitcast_convert_type(val[:, half:].astype(BF16).astype(F32), U32)
    return lax.shift_right_logical(lo, jnp.uint32(16)) | hi


def _unpack_words(words):
    lo = lax.bitcast_convert_type(lax.shift_left(words, jnp.uint32(16)), F32)
    hi = lax.bitcast_convert_type(words & jnp.uint32(0xFFFF0000), F32)
    return jnp.concatenate([lo, hi], axis=1)


def _dot_nt(a, b):
    return lax.dot_general(a, b, (((1,), (1,)), ((), ())), preferred_element_type=F32)


def _ada_kernel(c_ref, w_ref, b_ref, o_ref):
    a_hi, a_lo = _split_bf16(_silu(c_ref[...]))
    w_hi, w_lo = _split_bf16(w_ref[...])
    o_ref[...] = _dot(a_hi, w_hi) + _dot(a_lo, w_hi) + _dot(a_hi, w_lo) + b_ref[...]


def _ada(c8, w_ada, b_ada):
    d = c8.shape[1]
    n = w_ada.shape[1]
    tn = 512
    return pl.pallas_call(
        _ada_kernel,
        out_shape=jax.ShapeDtypeStruct((8, n), F32),
        grid=(n // tn,),
        in_specs=[pl.BlockSpec((8, d), lambda j: (0, 0)),
                  pl.BlockSpec((d, tn), lambda j: (0, j)),
                  pl.BlockSpec((1, tn), lambda j: (0, j))],
        out_specs=pl.BlockSpec((8, tn), lambda j: (0, j)),
        compiler_params=pltpu.CompilerParams(vmem_limit_bytes=VMEM_LIMIT),
        name="ada",
    )(c8, w_ada, b_ada.reshape(1, n))


def _norm_mod(x, g, shift, scale):
    ms = jnp.mean(x * x, axis=-1, keepdims=True)
    return (x * lax.rsqrt(ms + EPS) * g) * (1.0 + scale) + shift


def _lane_variants(t):
    lane = lax.broadcasted_iota(jnp.int32, t.shape, 1)
    lo = lane < HEAD_DIM
    tr = pltpu.roll(t, HEAD_DIM, 1)
    zero = jnp.zeros_like(t)
    return (jnp.where(lo, t, zero), jnp.where(lo, zero, tr),
            jnp.where(lo, tr, zero), jnp.where(lo, zero, t))


def _store_variants(ref, t):
    for i, var in enumerate(_lane_variants(t)):
        ref[0, :, i * LANES:(i + 1) * LANES] = var.astype(BF16)


def _inproj_kernel(x_ref, mod_ref, g_ref, w_ref, cos_ref, sin_ref, q_ref, k_ref, v_ref, p_ref):
    h = _norm_mod(x_ref[0], g_ref[...], mod_ref[0, 0:1, :], mod_ref[0, 1:2, :])
    z = _dot(h.astype(BF16), w_ref[...])
    cos = cos_ref[...]
    sin = sin_ref[...]
    lane = lax.broadcasted_iota(jnp.int32, cos.shape, 1)
    first_half = (lane & 16) == 0

    def rope(zc):
        partner = jnp.where(first_half, pltpu.roll(zc, LANES - 16, 1), pltpu.roll(zc, 16, 1))
        return zc * cos + partner * sin

    scale = HEAD_DIM ** -0.5 * LOG2E
    for c in range(ATTN_WIDTH // LANES):
        q_ref[0, :, c * LANES:(c + 1) * LANES] = (rope(z[:, c * LANES:(c + 1) * LANES]) * scale).astype(BF16)
    _store_variants(k_ref, rope(z[:, ATTN_WIDTH:ATTN_WIDTH + KV_WIDTH]))
    _store_variants(v_ref, z[:, ATTN_WIDTH + KV_WIDTH:ATTN_WIDTH + 2 * KV_WIDTH])
    p_ref[0] = z[:, ATTN_WIDTH + 2 * KV_WIDTH:]


def _inproj(x, mod, g1, w_in_bf, cos_t, sin_t):
    b, s, d = x.shape
    tm = TM_PROJ
    return pl.pallas_call(
        _inproj_kernel,
        out_shape=(jax.ShapeDtypeStruct((b, s, ATTN_WIDTH), BF16),
                   jax.ShapeDtypeStruct((b, s, 4 * LANES), BF16),
                   jax.ShapeDtypeStruct((b, s, 4 * LANES), BF16),
                   jax.ShapeDtypeStruct((b, s, POOL_WIDTH), F32)),
        grid=(s // tm, b),
        in_specs=[pl.BlockSpec((1, tm, d), lambda n, bi: (bi, n, 0)),
                  pl.BlockSpec((1, 6, d), lambda n, bi: (bi, 0, 0)),
                  pl.BlockSpec((1, d), lambda n, bi: (0, 0)),
                  pl.BlockSpec((d, IN_COLS), lambda n, bi: (0, 0)),
                  pl.BlockSpec((tm, LANES), lambda n, bi: (n, 0)),
                  pl.BlockSpec((tm, LANES), lambda n, bi: (n, 0))],
        out_specs=(pl.BlockSpec((1, tm, ATTN_WIDTH), lambda n, bi: (bi, n, 0)),
                   pl.BlockSpec((1, tm, 4 * LANES), lambda n, bi: (bi, n, 0)),
                   pl.BlockSpec((1, tm, 4 * LANES), lambda n, bi: (bi, n, 0)),
                   pl.BlockSpec((1, tm, POOL_WIDTH), lambda n, bi: (bi, n, 0))),
        compiler_params=pltpu.CompilerParams(vmem_limit_bytes=VMEM_LIMIT),
        name="inproj",
    )(x, mod, g1, w_in_bf, cos_t, sin_t)


def _ctxproj_kernel(x_ref, mod_ref, g_ref, w_ref, k_ref, v_ref):
    h = _norm_mod(x_ref[0], g_ref[...], mod_ref[0, 0:1, :], mod_ref[0, 1:2, :])
    z = _dot(h.astype(BF16), w_ref[...])
    _store_variants(k_ref, z[:, :KV_WIDTH])
    _store_variants(v_ref, z[:, KV_WIDTH:])


def _ctxproj(ctx, mod_c, g1, w_kv_bf):
    b, c, d = ctx.shape
    return pl.pallas_call(
        _ctxproj_kernel,
        out_shape=(jax.ShapeDtypeStruct((b, c, 4 * LANES), BF16),
                   jax.ShapeDtypeStruct((b, c, 4 * LANES), BF16)),
        grid=(b,),
        in_specs=[pl.BlockSpec((1, c, d), lambda bi: (bi, 0, 0)),
                  pl.BlockSpec((1, 6, d), lambda bi: (0, 0, 0)),
                  pl.BlockSpec((1, d), lambda bi: (0, 0)),
                  pl.BlockSpec((d, 2 * KV_WIDTH), lambda bi: (0, 0))],
        out_specs=(pl.BlockSpec((1, c, 4 * LANES), lambda bi: (bi, 0, 0)),
                   pl.BlockSpec((1, c, 4 * LANES), lambda bi: (bi, 0, 0))),
        compiler_params=pltpu.CompilerParams(vmem_limit_bytes=VMEM_LIMIT),
        name="ctxproj",
    )(ctx, mod_c, g1, w_kv_bf)


def _fold(op, tiles):
    while len(tiles) > 1:
        tiles = [op(tiles[i], tiles[i + 1]) if i + 1 < len(tiles) else tiles[i] for i in range(0, len(tiles), 2)]
    return tiles[0]


def _stack_variants(t4, kv):
    return jnp.concatenate([t4[:, (2 * kv) * LANES:(2 * kv + 1) * LANES],
                            t4[:, (2 * kv + 1) * LANES:(2 * kv + 2) * LANES]], axis=0)


def _attn_kernel(seq_len, sink_ref, q_ref, k_ref, kp_ref, kn_ref, v_ref, vp_ref, vn_ref, kc_ref, vc_ref,
                 p_ref, pp_ref, pn_ref, x_ref, mod_ref, band_ref, poolw_ref, pscale_ref, wout_ref,
                 g2_ref, wrh_ref, wrl_ref, x1_ref, h2_ref, lg_ref, kwin, vwin, pext, mix, s_scr, p_scr, m_scr):
    n = pl.program_id(1)
    n_last = pl.num_programs(1) - 1

    kwin[0:QB, :] = kp_ref[0]
    kwin[QB:QB + TQ, :] = k_ref[0]
    kwin[QB + TQ:, :] = kn_ref[0]
    vwin[0:QB, :] = vp_ref[0]
    vwin[QB:QB + TQ, :] = v_ref[0]
    vwin[QB + TQ:, :] = vn_ref[0]

    pext[0:QB - 8, :] = jnp.zeros((QB - 8, POOL_WIDTH), F32)
    pext[QB - 8:QB, :] = jnp.where(n > 0, pp_ref[0], 0.0)
    pext[QB:QB + TQ, :] = p_ref[0]
    pext[QB + TQ:QB + TQ + 8, :] = jnp.where(n < n_last, pn_ref[0], 0.0)
    pext[QB + TQ + 8:, :] = jnp.zeros((QB - 8, POOL_WIDTH), F32)

    row = lax.broadcasted_iota(jnp.int32, (QB, 3 * QB), 0)
    col = lax.broadcasted_iota(jnp.int32, (QB, 3 * QB), 1)
    in_band = (col >= row) & (col <= row + 2 * WINDOW)
    tok = lax.broadcasted_iota(jnp.int32, (QB, 1), 0)
    kc = kc_ref[0]
    kc_rows = [_stack_variants(kc, kv) for kv in range(N_KV_HEADS)]
    vc_rows = [_stack_variants(vc_ref[0], kv) for kv in range(N_KV_HEADS)]

    def sub_block(j, carry):
        r0 = pl.multiple_of(j * QB, QB)
        qj = q_ref[0, pl.ds(r0, QB), :]
        kw = kwin[pl.ds(r0, 3 * QB), :]
        vw = vwin[pl.ds(r0, 3 * QB), :]
        kpos = col + (n * TQ + j * QB - QB)
        ok = in_band & (kpos >= 0) & (kpos < seq_len)
        bias = jnp.where(ok, 0.0, -jnp.inf)
        n_loc, n_ctx = 3 * QB, kc.shape[0]
        bias2 = jnp.concatenate([bias, bias], axis=1)
        k_rows = [_stack_variants(kw, kv) for kv in range(N_KV_HEADS)]
        v_rows = [_stack_variants(vw, kv) for kv in range(N_KV_HEADS)]
        group = N_HEADS // N_KV_HEADS

        def head_tiles(head):
            loc0 = (head % 2) * n_loc
            ctx0 = 2 * n_loc + (head % 2) * n_ctx
            return ([loc0 + i * LANES for i in range(n_loc // LANES)]
                    + [ctx0 + i * LANES for i in range(n_ctx // LANES)])

        for c in range(N_HEADS // 2):
            qc = qj[:, c * LANES:(c + 1) * LANES]
            s_scr[c, :, 0:2 * n_loc] = _dot_nt(qc, k_rows[2 * c // group]) + bias2
            s_scr[c, :, 2 * n_loc:] = _dot_nt(qc, kc_rows[2 * c // group])
        for head in range(N_HEADS):
            tiles = [s_scr[head // 2, :, st:st + LANES] for st in head_tiles(head)]
            row_max = jnp.max(_fold(jnp.maximum, tiles), axis=1, keepdims=True)
            m_scr[head] = jnp.broadcast_to(jnp.maximum(row_max, sink_ref[head] * LOG2E), (QB, LANES))
        for head in range(N_HEADS):
            m = m_scr[head]
            acc = None
            for st in head_tiles(head):
                p = jnp.exp2(s_scr[head // 2, :, st:st + LANES] - m)
                p_scr[head // 2, :, st:st + LANES] = p.astype(BF16)
                acc = p if acc is None else acc + p
            denom = (jnp.broadcast_to(jnp.sum(acc, axis=1, keepdims=True), (QB, LANES))
                     + jnp.exp2(sink_ref[head] * LOG2E - m))
            m_scr[head] = 1.0 / denom
        lane = lax.broadcasted_iota(jnp.int32, (QB, LANES), 1)
        for c in range(N_HEADS // 2):
            o = (_dot(p_scr[c, :, 0:2 * n_loc], v_rows[2 * c // group])
                 + _dot(p_scr[c, :, 2 * n_loc:], vc_rows[2 * c // group]))
            o = o * jnp.where(lane < HEAD_DIM, m_scr[2 * c], m_scr[2 * c + 1])
            mix[pl.ds(r0, QB), c * LANES:(c + 1) * LANES] = o.astype(BF16)

        slab = pext[pl.ds(pl.multiple_of(r0 + POOL_OFF, 8), POOL_SLAB), :]
        tpos = tok + (n * TQ + j * QB)
        for g, w in enumerate(POOL_WINDOWS):
            sg = slab[:, g * LANES:(g + 1) * LANES]
            hi, lo = _split_bf16(sg)
            band = band_ref[g]
            wsum = _dot(band, hi) + _dot(band, lo)
            cnt = (jnp.minimum(tpos - w // 2 + w, seq_len) - jnp.maximum(tpos - w // 2, 0)).astype(F32)
            dlt = wsum / cnt - sg[POOL_OFF:POOL_OFF + QB, :]
            y = _dot(dlt.astype(BF16), poolw_ref[g]) * pscale_ref[:, g * LANES:(g + 1) * LANES]
            mix[pl.ds(r0, QB), ATTN_WIDTH + g * LANES:ATTN_WIDTH + (g + 1) * LANES] = y.astype(BF16)
        return carry

    lax.fori_loop(0, TQ // QB, sub_block, 0)

    proj = _dot(mix[...], wout_ref[...])
    x1 = x_ref[0] + mod_ref[0, 2:3, :] * proj
    x1_ref[0] = x1
    h2 = _norm_mod(x1, g2_ref[...], mod_ref[0, 3:4, :], mod_ref[0, 4:5, :])
    h2_ref[...] = _pack_words(h2)
    h_hi, h_lo = _split_bf16(h2)
    wrh = wrh_ref[...]
    lg_ref[...] = _dot_nt(wrh, h_hi) + _dot_nt(wrh, h_lo) + _dot_nt(wrl_ref[...], h_hi)


def _attn(b0, b, sink, q, k4, v4, kc4, vc4, p, x, mod, band, poolw_bf, pscale, wout_bf, g2, wr_hi, wr_lo):
    _, s, d = x.shape
    c = kc4.shape[1]
    nt = s // TQ
    hb = TQ // QB
    pb = TQ // 8
    kv_main = pl.BlockSpec((1, TQ, 4 * LANES), lambda bi, n: (b0 + bi, n, 0))
    kv_prev = pl.BlockSpec((1, QB, 4 * LANES), lambda bi, n: (b0 + bi, jnp.maximum(n * hb - 1, 0), 0))
    kv_next = pl.BlockSpec((1, QB, 4 * LANES), lambda bi, n: (b0 + bi, jnp.minimum((n + 1) * hb, s // QB - 1), 0))
    const2 = lambda bi, n: (0, 0)
    const3 = lambda bi, n: (0, 0, 0)
    return pl.pallas_call(
        functools.partial(_attn_kernel, s),
        out_shape=(jax.ShapeDtypeStruct((b, s, d), F32),
                   jax.ShapeDtypeStruct((b * s, d // 2), U32),
                   jax.ShapeDtypeStruct((N_EXPERTS, b * s), F32)),
        grid=(b, nt),
        in_specs=[pl.BlockSpec(memory_space=pltpu.SMEM),
                  pl.BlockSpec((1, TQ, ATTN_WIDTH), lambda bi, n: (b0 + bi, n, 0)),
                  kv_main, kv_prev, kv_next, kv_main, kv_prev, kv_next,
                  pl.BlockSpec((1, c, 4 * LANES), lambda bi, n: (b0 + bi, 0, 0)),
                  pl.BlockSpec((1, c, 4 * LANES), lambda bi, n: (b0 + bi, 0, 0)),
                  pl.BlockSpec((1, TQ, POOL_WIDTH), lambda bi, n: (b0 + bi, n, 0)),
                  pl.BlockSpec((1, 8, POOL_WIDTH), lambda bi, n: (b0 + bi, jnp.maximum(n * pb - 1, 0), 0)),
                  pl.BlockSpec((1, 8, POOL_WIDTH),
                               lambda bi, n: (b0 + bi, jnp.minimum((n + 1) * pb, s // 8 - 1), 0)),
                  pl.BlockSpec((1, TQ, d), lambda bi, n: (b0 + bi, n, 0)),
                  pl.BlockSpec((1, 6, d), lambda bi, n: (b0 + bi, 0, 0)),
                  pl.BlockSpec((len(POOL_WINDOWS), QB, POOL_SLAB), const3),
                  pl.BlockSpec((len(POOL_WINDOWS), POOL_GROUP_DIM, POOL_GROUP_DIM), const3),
                  pl.BlockSpec((1, POOL_WIDTH), const2),
                  pl.BlockSpec((d, d), const2),
                  pl.BlockSpec((1, d), const2),
                  pl.BlockSpec((N_EXPERTS, d), const2),
                  pl.BlockSpec((N_EXPERTS, d), const2)],
        out_specs=(pl.BlockSpec((1, TQ, d), lambda bi, n: (bi, n, 0)),
                   pl.BlockSpec((TQ, d // 2), lambda bi, n: (bi * nt + n, 0)),
                   pl.BlockSpec((N_EXPERTS, TQ), lambda bi, n: (0, bi * nt + n))),
        scratch_shapes=[pltpu.VMEM((TQ + 2 * QB, 4 * LANES), BF16),
                        pltpu.VMEM((TQ + 2 * QB, 4 * LANES), BF16),
                        pltpu.VMEM((TQ + 2 * QB, POOL_WIDTH), F32),
                        pltpu.VMEM((TQ, d), BF16),
                        pltpu.VMEM((N_HEADS // 2, QB, 2 * (3 * QB + c)), F32),
                        pltpu.VMEM((N_HEADS // 2, QB, 2 * (3 * QB + c)), BF16),
                        pltpu.VMEM((N_HEADS, QB, LANES), F32)],
        compiler_params=pltpu.CompilerParams(vmem_limit_bytes=VMEM_LIMIT),
        name="attn",
    )(sink, q, k4, k4, k4, v4, v4, v4, kc4, vc4, p, p, p, x, mod, band, poolw_bf, pscale, wout_bf,
      g2, wr_hi, wr_lo)


def _first_argmax_rows(v, row_iota, n_rows):
    m = jnp.max(v, axis=0, keepdims=True)
    idx = jnp.min(jnp.where(v == m, row_iota, n_rows), axis=0, keepdims=True)
    return m, idx


def _route_kernel(lg_ref, bias_ref, tri_ref, idx_ref, gate_ref, rank_ref, cnt_ref, carry):
    i = pl.program_id(0)

    @pl.when(i == 0)
    def _():
        carry[...] = jnp.zeros_like(carry)

    scores = 1.0 / (1.0 + jnp.exp(-lg_ref[...]))
    biased = scores + bias_ref[...]
    e_iota = lax.broadcasted_iota(jnp.int32, scores.shape, 0).astype(F32)
    g_iota = lax.broadcasted_iota(jnp.int32, (EXPERTS_PER_GROUP, TR), 0).astype(F32)
    neg = -jnp.inf

    grp = []
    for g in range(N_EXPERT_GROUPS):
        blk = biased[g * EXPERTS_PER_GROUP:(g + 1) * EXPERTS_PER_GROUP, :]
        m1, i1 = _first_argmax_rows(blk, g_iota, float(EXPERTS_PER_GROUP))
        m2 = jnp.max(jnp.where(g_iota == i1, neg, blk), axis=0, keepdims=True)
        grp.append(m1 + m2)
    grp = jnp.concatenate(grp, axis=0)
    gg_iota = lax.broadcasted_iota(jnp.int32, grp.shape, 0).astype(F32)
    grp_sel = jnp.zeros(grp.shape, F32)
    for _ in range(TOPK_GROUPS):
        _, gi = _first_argmax_rows(grp, gg_iota, float(N_EXPERT_GROUPS))
        hit = gg_iota == gi
        grp_sel = jnp.where(hit, 1.0, grp_sel)
        grp = jnp.where(hit, neg, grp)
    allowed = jnp.concatenate(
        [jnp.broadcast_to(grp_sel[g:g + 1, :], (EXPERTS_PER_GROUP, TR)) for g in range(N_EXPERT_GROUPS)], axis=0)
    masked = jnp.where(allowed > 0.5, biased, neg)

    idxs, gates = [], []
    onehot = jnp.zeros(scores.shape, F32)
    for _ in range(TOP_K):
        _, ei = _first_argmax_rows(masked, e_iota, float(N_EXPERTS))
        hit = e_iota == ei
        idxs.append(ei)
        gates.append(jnp.sum(jnp.where(hit, scores, 0.0), axis=0, keepdims=True))
        onehot = jnp.where(hit, 1.0, onehot)
        masked = jnp.where(hit, neg, masked)
    idx = jnp.concatenate(idxs, axis=0)
    gate = jnp.concatenate(gates, axis=0)
    gate = gate / jnp.sum(gate, axis=0, keepdims=True) * ROUTED_SCALE

    before = _dot(onehot.astype(BF16), tri_ref[...]) + carry[:, 0:1]
    ranks = [jnp.sum(jnp.where(e_iota == idxs[k], before, 0.0), axis=0, keepdims=True) for k in range(TOP_K)]
    idx_ref[...] = idx.astype(jnp.int32)
    gate_ref[...] = gate
    rank_ref[...] = jnp.concatenate(ranks, axis=0).astype(jnp.int32)
    total = carry[...] + jnp.sum(onehot, axis=1, keepdims=True)
    carry[...] = total
    cnt_ref[...] = total


def _route(lg_t, bias, tri):
    e, t = lg_t.shape
    tok = pl.BlockSpec((TOP_K, TR), lambda i: (0, i))
    return pl.pallas_call(
        _route_kernel,
        out_shape=(jax.ShapeDtypeStruct((TOP_K, t), jnp.int32),
                   jax.ShapeDtypeStruct((TOP_K, t), F32),
                   jax.ShapeDtypeStruct((TOP_K, t), jnp.int32),
                   jax.ShapeDtypeStruct((e, LANES), F32)),
        grid=(t // TR,),
        in_specs=[pl.BlockSpec((e, TR), lambda i: (0, i)),
                  pl.BlockSpec((e, 1), lambda i: (0, 0)),
                  pl.BlockSpec((TR, TR), lambda i: (0, 0))],
        out_specs=(tok, tok, tok, pl.BlockSpec((e, LANES), lambda i: (0, 0))),
        scratch_shapes=[pltpu.VMEM((e, LANES), F32)],
        compiler_params=pltpu.CompilerParams(vmem_limit_bytes=VMEM_LIMIT),
        name="route",
    )(lg_t, bias, tri)


def _plan_kernel(n_blocks, size_ref, start_ref, expert_ref, valid_ref, nact_ref):
    def per_expert(e, first_block):
        size = size_ref[e]
        n_blk = (size + TM_EXP - 1) // TM_EXP
        start_ref[e] = first_block * TM_EXP

        def per_block(j, carry):
            expert_ref[first_block + j] = e
            valid_ref[first_block + j] = jnp.minimum(size - j * TM_EXP, TM_EXP)
            return carry

        lax.fori_loop(0, n_blk, per_block, 0)
        return first_block + n_blk

    n_active = lax.fori_loop(0, N_EXPERTS, per_expert, 0)
    nact_ref[0] = n_active

    def unused(i, carry):
        expert_ref[i] = N_EXPERTS - 1
        valid_ref[i] = 0
        return carry

    lax.fori_loop(n_active, n_blocks, unused, 0)


def _plan_blocks(sizes, n_blocks):
    smem = pl.BlockSpec(memory_space=pltpu.SMEM)
    return pl.pallas_call(
        functools.partial(_plan_kernel, n_blocks),
        out_shape=(jax.ShapeDtypeStruct((N_EXPERTS,), jnp.int32),
                   jax.ShapeDtypeStruct((n_blocks,), jnp.int32),
                   jax.ShapeDtypeStruct((n_blocks,), jnp.int32),
                   jax.ShapeDtypeStruct((1,), jnp.int32)),
        in_specs=[smem],
        out_specs=(smem, smem, smem, smem),
        name="plan_blocks",
    )(sizes)


def _dest_kernel(start_ref, idx_ref, rank_ref, dest_ref):
    idx = idx_ref[...]
    dest = rank_ref[...]
    for e in range(N_EXPERTS):
        dest = dest + jnp.where(idx == e, start_ref[e], 0)
    dest_ref[...] = dest


def _dest_rows(pad_start, idx_kt, rank_kt):
    n_k, t = idx_kt.shape
    tile = 4096
    blk = pl.BlockSpec((n_k, tile), lambda i: (0, i))
    return pl.pallas_call(
        _dest_kernel,
        out_shape=jax.ShapeDtypeStruct((n_k, t), jnp.int32),
        grid=(t // tile,),
        in_specs=[pl.BlockSpec(memory_space=pltpu.SMEM), blk, blk],
        out_specs=blk,
        name="dest_rows",
    )(pad_start, idx_kt, rank_kt)


def _sc_mesh():
    return plsc.VectorSubcoreMesh(core_axis_name="c", subcore_axis_name="s")


def _sc_token_base(steps, j):
    worker = lax.axis_index("s") * SC_CORES + lax.axis_index("c")
    return (worker * steps + j) * SC_TOKENS


def _sc_scatter(dest_kt, h2p, n_rows):
    t, width = h2p.shape
    steps = t // (SC_WORKERS * SC_TOKENS)

    @functools.partial(
        pl.kernel, mesh=_sc_mesh(),
        out_type=jax.ShapeDtypeStruct((n_rows, width), U32),
        scratch_types=[pltpu.VMEM((TOP_K, SC_TOKENS), jnp.int32),
                       pltpu.VMEM((SC_TOKENS, width), U32),
                       pltpu.SemaphoreType.DMA],
        name="sc_scatter",
    )
    def body(dest_hbm, h_hbm, xs_hbm, idx_v, rows_v, sem):
        @pl.loop(0, steps)
        def _(j):
            base = _sc_token_base(steps, j)
            pltpu.sync_copy(dest_hbm.at[:, pl.ds(base, SC_TOKENS)], idx_v)
            pltpu.sync_copy(h_hbm.at[pl.ds(base, SC_TOKENS)], rows_v)
            copies = [pltpu.async_copy(rows_v, xs_hbm.at[idx_v.at[k]], sem) for k in range(TOP_K)]
            for cp in copies:
                cp.wait()

    return body(dest_kt, h2p)


def _sc_gather(dest_kt, ys, token0, n_tokens):
    n_k = dest_kt.shape[0]
    width = ys.shape[1]
    steps = n_tokens // (SC_WORKERS * SC_TOKENS)

    half = SC_TOKENS // 2
    units = [(k, h) for k in range(n_k) for h in range(2)]
    n_buf = 3

    @functools.partial(
        pl.kernel, mesh=_sc_mesh(),
        out_type=jax.ShapeDtypeStruct((n_k, n_tokens, width), U32),
        scratch_types=[pltpu.VMEM((n_k, SC_TOKENS), jnp.int32),
                       pltpu.VMEM((n_buf, half, width), U32),
                       pltpu.SemaphoreType.DMA((n_buf,))],
        name="sc_gather",
    )
    def body(dest_hbm, ys_hbm, yk_hbm, idx_v, rows_v, sems):
        @pl.loop(0, steps)
        def _(j):
            base = _sc_token_base(steps, j)
            pltpu.sync_copy(dest_hbm.at[:, pl.ds(token0 + base, SC_TOKENS)], idx_v)

            def gather(u):
                k, h = units[u]
                slot = u % n_buf
                return pltpu.make_async_copy(ys_hbm.at[idx_v.at[k, pl.ds(h * half, half)]], rows_v.at[slot],
                                             sems.at[slot])

            gather(0).start()
            gather(1).start()
            for u, (k, h) in enumerate(units):
                gather(u).wait()
                pltpu.sync_copy(rows_v.at[u % n_buf], yk_hbm.at[k, pl.ds(base + h * half, half)])
                if u + 2 < len(units):
                    gather(u + 2).start()

    return body(dest_kt, ys)


def _experts_kernel(be_ref, valid_ref, nact_ref, xs_hbm, wg_ref, wu_ref, wd_ref, ys_ref, wg_bf, wu_bf, wd_bf,
                    xbuf, sems):
    i = pl.program_id(0)
    n_active = nact_ref[0]
    slot = lax.rem(i, EXP_SLOTS)

    def fetch(j):
        src = xs_hbm.at[pl.ds(pl.multiple_of(j * TM_EXP, TM_EXP), TM_EXP), :]
        s = lax.rem(j, EXP_SLOTS)
        return pltpu.make_async_copy(src, xbuf.at[s], sems.at[s])

    @pl.when(i == 0)
    def _():
        for j in range(EXP_SLOTS - 1):
            pl.when(j < n_active)(lambda j=j: fetch(j).start())

    @pl.when(i + (EXP_SLOTS - 1) < n_active)
    def _():
        fetch(i + (EXP_SLOTS - 1)).start()

    prev = be_ref[jnp.maximum(i - 1, 0)]

    @pl.when((i == 0) | (be_ref[i] != prev))
    def _():
        wg_bf[...] = wg_ref[0].astype(BF16)
        wu_bf[...] = wu_ref[0].astype(BF16)
        wd_bf[...] = wd_ref[0].astype(BF16)

    @pl.when(i < n_active)
    def _():
        fetch(i).wait()

    valid = jnp.where(i < n_active, valid_ref[i], 0)

    def run(rows):
        row = lax.broadcasted_iota(jnp.int32, (rows, xbuf.shape[2]), 0)
        words = jnp.where(row < valid, xbuf[slot, 0:rows, :], jnp.uint32(0))
        xb = _unpack_words(words).astype(BF16)
        hid = _silu(_dot(xb, wg_bf[...])) * _dot(xb, wu_bf[...])
        ys_ref[0:rows, :] = _pack_words(_dot(hid.astype(BF16), wd_bf[...]))
        if rows < TM_EXP:
            ys_ref[rows:, :] = jnp.zeros((TM_EXP - rows, ys_ref.shape[1]), U32)

    for rows in range(EXP_MIN_ROWS, TM_EXP + 1, EXP_MIN_ROWS):
        pl.when((valid > rows - EXP_MIN_ROWS) & (valid <= rows))(functools.partial(run, rows))

    @pl.when(valid == 0)
    def _():
        ys_ref[...] = jnp.zeros_like(ys_ref)


def _experts(block_e, block_valid, n_active, xs, w_gate, w_up, w_down):
    n_rows, half = xs.shape
    d = w_gate.shape[1]
    nb = n_rows // TM_EXP
    grid_spec = pltpu.PrefetchScalarGridSpec(
        num_scalar_prefetch=3,
        grid=(nb,),
        in_specs=[pl.BlockSpec(memory_space=pl.ANY),
                  pl.BlockSpec((1, d, D_EXPERT), lambda i, be, bv, na: (be[i], 0, 0)),
                  pl.BlockSpec((1, d, D_EXPERT), lambda i, be, bv, na: (be[i], 0, 0)),
                  pl.BlockSpec((1, D_EXPERT, d), lambda i, be, bv, na: (be[i], 0, 0))],
        out_specs=pl.BlockSpec((TM_EXP, half), lambda i, be, bv, na: (i, 0)),
        scratch_shapes=[pltpu.VMEM((d, D_EXPERT), BF16), pltpu.VMEM((d, D_EXPERT), BF16),
                        pltpu.VMEM((D_EXPERT, d), BF16),
                        pltpu.VMEM((EXP_SLOTS, TM_EXP, half), U32),
                        pltpu.SemaphoreType.DMA((EXP_SLOTS,))],
    )
    return pl.pallas_call(
        _experts_kernel,
        out_shape=jax.ShapeDtypeStruct((n_rows, half), U32),
        grid_spec=grid_spec,
        compiler_params=pltpu.CompilerParams(vmem_limit_bytes=VMEM_LIMIT, dimension_semantics=("arbitrary",)),
        name="experts",
    )(block_e, block_valid, n_active, xs, w_gate, w_up, w_down)


def _combine_kernel(yk_ref, x1_ref, h2_ref, gate_ref, mod_ref, wsg_ref, wsu_ref, wsd_ref, fg_ref, *out_refs):
    out_ref = out_refs[-1]
    hb = _unpack_words(h2_ref[...]).astype(BF16)
    hid = _silu(_dot(hb, wsg_ref[...])) * _dot(hb, wsu_ref[...])
    ffn = _dot(hid.astype(BF16), wsd_ref[...])
    gate = gate_ref[...]
    for k in range(TOP_K):
        ffn = ffn + gate[:, k:k + 1] * _unpack_words(yk_ref[k])
    x2 = x1_ref[...] + mod_ref[0, 5:6, :] * ffn
    ms = jnp.mean(x2 * x2, axis=-1, keepdims=True)
    out_ref[...] = x2 * lax.rsqrt(ms + EPS) * fg_ref[...]


def _combine(yk, token0, out_token0, n_out_tokens, prev_out, x1, h2p, gate_tk, mod, wsg_bf, wsu_bf, wsd_bf, final_g,
             seq_len):
    t, d = n_out_tokens, x1.shape[1]
    tiles_per_seq = seq_len // TF
    tile0 = token0 // TF
    out_tile0 = out_token0 // TF
    in_tok = pl.BlockSpec((TF, d), lambda i: (tile0 + i, 0))
    tok = pl.BlockSpec((TF, d), lambda i: (out_tile0 + i, 0))
    in_specs = [pl.BlockSpec((TOP_K, TF, d // 2), lambda i: (0, i, 0)),
                in_tok, pl.BlockSpec((TF, d // 2), lambda i: (tile0 + i, 0)),
                pl.BlockSpec((TF, TOP_K), lambda i: (tile0 + i, 0)),
                pl.BlockSpec((1, 6, d), lambda i: ((out_tile0 + i) // tiles_per_seq, 0, 0)),
                pl.BlockSpec((d, D_SHARED), lambda i: (0, 0)),
                pl.BlockSpec((d, D_SHARED), lambda i: (0, 0)),
                pl.BlockSpec((D_SHARED, d), lambda i: (0, 0)),
                pl.BlockSpec((1, d), lambda i: (0, 0))]
    args = [yk, x1, h2p, gate_tk, mod, wsg_bf, wsu_bf, wsd_bf, final_g]
    aliases = {}
    if prev_out is not None:
        in_specs.append(pl.BlockSpec(memory_space=pl.ANY))
        args.append(prev_out)
        aliases = {len(args) - 1: 0}
    return pl.pallas_call(
        _combine_kernel,
        out_shape=jax.ShapeDtypeStruct((t, d), F32),
        grid=(yk.shape[1] // TF,),
        in_specs=in_specs,
        out_specs=tok,
        input_output_aliases=aliases,
        compiler_params=pltpu.CompilerParams(vmem_limit_bytes=VMEM_LIMIT),
        name="combine",
    )(*args)


def _rope_tables(n_tokens):
    n_rows = n_tokens // GRID_W
    n_freq = HEAD_DIM // 4
    inv_freq = ROPE_THETA ** (-jnp.arange(n_freq, dtype=F32) / n_freq)
    ang_r = jnp.arange(n_rows).astype(F32)[:, None] * inv_freq[None, :]
    ang_c = jnp.arange(GRID_W).astype(F32)[:, None] * inv_freq[None, :]

    def per_token(row_part, col_part):
        rows = jnp.broadcast_to(row_part[:, None, :], (n_rows, GRID_W, n_freq))
        cols = jnp.broadcast_to(col_part[None, :, :], (n_rows, GRID_W, n_freq))
        return rows.reshape(n_tokens, n_freq), cols.reshape(n_tokens, n_freq)

    cos_r, cos_c = per_token(jnp.cos(ang_r), jnp.cos(ang_c))
    sin_r, sin_c = per_token(jnp.sin(ang_r), jnp.sin(ang_c))
    cos = jnp.concatenate([cos_r, cos_r, cos_c, cos_c], axis=1)
    sin = jnp.concatenate([-sin_r, sin_r, -sin_c, sin_c], axis=1)
    reps = LANES // HEAD_DIM
    return jnp.tile(cos, (1, reps)), jnp.tile(sin, (1, reps))


def _pool_bands():
    i = jnp.arange(QB)[:, None]
    r = jnp.arange(POOL_SLAB)[None, :]
    return jnp.stack([((r >= i + POOL_OFF - w // 2) & (r < i + POOL_OFF + w // 2)).astype(BF16)
                      for w in POOL_WINDOWS])


def kernel(x, c, ctx, c_ctx, w_ada, b_ada, norm1_g, norm2_g, w_in, attn_sink, pool_w, pool_scale, w_out,
           w_router, router_bias, w_gate, w_up, w_down, ws_gate, ws_up, ws_down, final_g):
    b, s, d = x.shape
    t = b * s
    assert w_ada.shape[0] == 1 and d == D_MODEL and s % TQ == 0 and b + 1 <= 8

    c8 = jnp.zeros((8, d), F32).at[:b].set(c).at[b].set(c_ctx)
    mod = _ada(c8, w_ada[0], b_ada[0]).reshape(8, 6, d)
    g1 = norm1_g[0].reshape(1, d)
    g2 = norm2_g[0].reshape(1, d)
    w_in_bf = w_in[0].astype(BF16)
    cos_t, sin_t = _rope_tables(s)

    q, k4, v4, p = _inproj(x, mod, g1, w_in_bf, cos_t, sin_t)
    kc4, vc4 = _ctxproj(ctx, mod[b:b + 1], g1, w_in_bf[:, ATTN_WIDTH:ATTN_WIDTH + 2 * KV_WIDTH])

    wr_t = w_router[0].T
    wr_hi = wr_t.astype(BF16)
    wr_lo = (wr_t - wr_hi.astype(F32)).astype(BF16)
    attn_consts = (_pool_bands(), pool_w[0].astype(BF16), pool_scale[0].reshape(1, POOL_WIDTH),
                   w_out[0].astype(BF16), g2, wr_hi, wr_lo)
    tri = jnp.triu(jnp.ones((TR, TR), BF16), k=1)
    shared_w = (ws_gate[0].astype(BF16), ws_up[0].astype(BF16), ws_down[0].astype(BF16))

    nb = b // TOKEN_GROUPS
    tg = nb * s
    assert b % TOKEN_GROUPS == 0 and tg % (COMBINE_CHUNKS * SC_WORKERS * SC_TOKENS) == 0
    n_rows = -(-(tg * TOP_K + N_EXPERTS * (TM_EXP - 1)) // TM_EXP) * TM_EXP
    groups = []
    for g in range(TOKEN_GROUPS):
        x1, h2p, lg_t = _attn(g * nb, nb, attn_sink[0], q, k4, v4, kc4, vc4, p, x, mod, *attn_consts)
        idx_kt, gate_kt, rank_kt, counts = _route(lg_t, router_bias[0].reshape(N_EXPERTS, 1), tri)
        pad_start, block_e, block_valid, n_active = _plan_blocks(counts[:, 0].astype(jnp.int32), n_rows // TM_EXP)
        dest_kt = _dest_rows(pad_start, idx_kt, rank_kt)
        xs = _sc_scatter(dest_kt, h2p, n_rows)
        groups.append((x1.reshape(tg, d), h2p, gate_kt.T, dest_kt, xs, block_e, block_valid, n_active))

    out = None
    chunk = tg // COMBINE_CHUNKS
    for g, (x1, h2p, gate_tk, dest_kt, xs, block_e, block_valid, n_active) in enumerate(groups):
        ys = _experts(block_e, block_valid, n_active, xs, w_gate[0], w_up[0], w_down[0])
        for token0 in range(0, tg, chunk):
            out = _combine(_sc_gather(dest_kt, ys, token0, chunk), token0, g * tg + token0, t, out, x1, h2p,
                           gate_tk, mod, *shared_w, final_g.reshape(1, d), s)
    return out.reshape(b, s, d)
```

```python
import functools

import jax
import jax.numpy as jnp
from jax import lax
from jax.experimental import pallas as pl
from jax.experimental.pallas import tpu as pltpu
from jax.experimental.pallas import tpu_sc as plsc

F32 = jnp.float32
BF16 = jnp.bfloat16

D_MODEL = 1024
GRID_W = 64
N_HEADS = 8
N_KV_HEADS = 2
HEAD_DIM = 64
ATTN_WIDTH = N_HEADS * HEAD_DIM
KV_WIDTH = N_KV_HEADS * HEAD_DIM
WINDOW = 128
ROPE_THETA = 10000.0
POOL_WINDOWS = (2, 4, 8, 16)
POOL_WIDTH = D_MODEL - ATTN_WIDTH
POOL_GROUP_DIM = POOL_WIDTH // len(POOL_WINDOWS)
IN_COLS = ATTN_WIDTH + 2 * KV_WIDTH + POOL_WIDTH
N_EXPERTS = 64
TOP_K = 8
N_EXPERT_GROUPS = 8
EXPERTS_PER_GROUP = N_EXPERTS // N_EXPERT_GROUPS
TOPK_GROUPS = 4
D_EXPERT = 256
D_SHARED = 256
ROUTED_SCALE = 2.5
EPS = 1e-6
LOG2E = 1.4426950408889634

LANES = 128
U32 = jnp.uint32
VMEM_LIMIT = 48 * 1024 * 1024

TM_PROJ = 1024
TQ = 1024
QB = 128
POOL_SLAB = 256
POOL_OFF = 64
TR = 512
TM_EXP = 1024
EXP_MIN_ROWS = 256
EXP_SLOTS = 4
TF = 512
TOKEN_GROUPS = 4
COMBINE_CHUNKS = 2
SC_CORES = 2
SC_WORKERS = 32
SC_TOKENS = 128


def _silu(x):
    return x * (1.0 / (1.0 + jnp.exp(-x)))


def _split_bf16(x):
    hi = x.astype(BF16)
    lo = (x - hi.astype(F32)).astype(BF16)
    return hi, lo


def _dot(a, b):
    return jnp.dot(a, b, preferred_element_type=F32)


def _pack_words(val):
    half = val.shape[1] // 2
    lo = lax.bitcast_convert_type(val[:, :half].astype(BF16).astype(F32), U32)
    hi = lax.bitcast_convert_type(val[:, half:].astype(BF16).astype(F32), U32)
    return lax.shift_right_logical(lo, jnp.uint32(16)) | hi


def _unpack_words(words):
    lo = lax.bitcast_convert_type(lax.shift_left(words, jnp.uint32(16)), F32)
    hi = lax.bitcast_convert_type(words & jnp.uint32(0xFFFF0000), F32)
    return jnp.concatenate([lo, hi], axis=1)


def _dot_nt(a, b):
    return lax.dot_general(a, b, (((1,), (1,)), ((), ())), preferred_element_type=F32)


def _ada_kernel(c_ref, w_ref, b_ref, o_ref):
    a_hi, a_lo = _split_bf16(_silu(c_ref[...]))
    w_hi, w_lo = _split_bf16(w_ref[...])
    o_ref[...] = _dot(a_hi, w_hi) + _dot(a_lo, w_hi) + _dot(a_hi, w_lo) + b_ref[...]


def _ada(c8, w_ada, b_ada):
    d = c8.shape[1]
    n = w_ada.shape[1]
    tn = 512
    return pl.pallas_call(
        _ada_kernel,
        out_shape=jax.ShapeDtypeStruct((8, n), F32),
        grid=(n // tn,),
        in_specs=[pl.BlockSpec((8, d), lambda j: (0, 0)),
                  pl.BlockSpec((d, tn), lambda j: (0, j)),
                  pl.BlockSpec((1, tn), lambda j: (0, j))],
        out_specs=pl.BlockSpec((8, tn), lambda j: (0, j)),
        compiler_params=pltpu.CompilerParams(vmem_limit_bytes=VMEM_LIMIT),
        name="ada",
    )(c8, w_ada, b_ada.reshape(1, n))


def _norm_mod(x, g, shift, scale):
    ms = jnp.mean(x * x, axis=-1, keepdims=True)
    return (x * lax.rsqrt(ms + EPS) * g) * (1.0 + scale) + shift


def _lane_variants(t):
    lane = lax.broadcasted_iota(jnp.int32, t.shape, 1)
    lo = lane < HEAD_DIM
    tr = pltpu.roll(t, HEAD_DIM, 1)
    zero = jnp.zeros_like(t)
    return (jnp.where(lo, t, zero), jnp.where(lo, zero, tr),
            jnp.where(lo, tr, zero), jnp.where(lo, zero, t))


def _store_variants(ref, t):
    for i, var in enumerate(_lane_variants(t)):
        ref[0, :, i * LANES:(i + 1) * LANES] = var.astype(BF16)


def _inproj_kernel(x_ref, mod_ref, g_ref, w_ref, cos_ref, sin_ref, q_ref, k_ref, v_ref, p_ref):
    h = _norm_mod(x_ref[0], g_ref[...], mod_ref[0, 0:1, :], mod_ref[0, 1:2, :])
    z = _dot(h.astype(BF16), w_ref[...])
    cos = cos_ref[...]
    sin = sin_ref[...]
    lane = lax.broadcasted_iota(jnp.int32, cos.shape, 1)
    first_half = (lane & 16) == 0

    def rope(zc):
        partner = jnp.where(first_half, pltpu.roll(zc, LANES - 16, 1), pltpu.roll(zc, 16, 1))
        return zc * cos + partner * sin

    scale = HEAD_DIM ** -0.5 * LOG2E
    for c in range(ATTN_WIDTH // LANES):
        q_ref[0, :, c * LANES:(c + 1) * LANES] = (rope(z[:, c * LANES:(c + 1) * LANES]) * scale).astype(BF16)
    _store_variants(k_ref, rope(z[:, ATTN_WIDTH:ATTN_WIDTH + KV_WIDTH]))
    _store_variants(v_ref, z[:, ATTN_WIDTH + KV_WIDTH:ATTN_WIDTH + 2 * KV_WIDTH])
    p_ref[0] = z[:, ATTN_WIDTH + 2 * KV_WIDTH:]


def _inproj(x, mod, g1, w_in_bf, cos_t, sin_t):
    b, s, d = x.shape
    tm = TM_PROJ
    return pl.pallas_call(
        _inproj_kernel,
        out_shape=(jax.ShapeDtypeStruct((b, s, ATTN_WIDTH), BF16),
                   jax.ShapeDtypeStruct((b, s, 4 * LANES), BF16),
                   jax.ShapeDtypeStruct((b, s, 4 * LANES), BF16),
                   jax.ShapeDtypeStruct((b, s, POOL_WIDTH), F32)),
        grid=(s // tm, b),
        in_specs=[pl.BlockSpec((1, tm, d), lambda n, bi: (bi, n, 0)),
                  pl.BlockSpec((1, 6, d), lambda n, bi: (bi, 0, 0)),
                  pl.BlockSpec((1, d), lambda n, bi: (0, 0)),
                  pl.BlockSpec((d, IN_COLS), lambda n, bi: (0, 0)),
                  pl.BlockSpec((tm, LANES), lambda n, bi: (n, 0)),
                  pl.BlockSpec((tm, LANES), lambda n, bi: (n, 0))],
        out_specs=(pl.BlockSpec((1, tm, ATTN_WIDTH), lambda n, bi: (bi, n, 0)),
                   pl.BlockSpec((1, tm, 4 * LANES), lambda n, bi: (bi, n, 0)),
                   pl.BlockSpec((1, tm, 4 * LANES), lambda n, bi: (bi, n, 0)),
                   pl.BlockSpec((1, tm, POOL_WIDTH), lambda n, bi: (bi, n, 0))),
        compiler_params=pltpu.CompilerParams(vmem_limit_bytes=VMEM_LIMIT),
        name="inproj",
    )(x, mod, g1, w_in_bf, cos_t, sin_t)


def _ctxproj_kernel(x_ref, mod_ref, g_ref, w_ref, k_ref, v_ref):
    h = _norm_mod(x_ref[0], g_ref[...], mod_ref[0, 0:1, :], mod_ref[0, 1:2, :])
    z = _dot(h.astype(BF16), w_ref[...])
    _store_variants(k_ref, z[:, :KV_WIDTH])
    _store_variants(v_ref, z[:, KV_WIDTH:])


def _ctxproj(ctx, mod_c, g1, w_kv_bf):
    b, c, d = ctx.shape
    return pl.pallas_call(
        _ctxproj_kernel,
        out_shape=(jax.ShapeDtypeStruct((b, c, 4 * LANES), BF16),
                   jax.ShapeDtypeStruct((b, c, 4 * LANES), BF16)),
        grid=(b,),
        in_specs=[pl.BlockSpec((1, c, d), lambda bi: (bi, 0, 0)),
                  pl.BlockSpec((1, 6, d), lambda bi: (0, 0, 0)),
                  pl.BlockSpec((1, d), lambda bi: (0, 0)),
                  pl.BlockSpec((d, 2 * KV_WIDTH), lambda bi: (0, 0))],
        out_specs=(pl.BlockSpec((1, c, 4 * LANES), lambda bi: (bi, 0, 0)),
                   pl.BlockSpec((1, c, 4 * LANES), lambda bi: (bi, 0, 0))),
        compiler_params=pltpu.CompilerParams(vmem_limit_bytes=VMEM_LIMIT),
        name="ctxproj",
    )(ctx, mod_c, g1, w_kv_bf)


def _fold(op, tiles):
    while len(tiles) > 1:
        tiles = [op(tiles[i], tiles[i + 1]) if i + 1 < len(tiles) else tiles[i] for i in range(0, len(tiles), 2)]
    return tiles[0]


def _stack_variants(t4, kv):
    return jnp.concatenate([t4[:, (2 * kv) * LANES:(2 * kv + 1) * LANES],
                            t4[:, (2 * kv + 1) * LANES:(2 * kv + 2) * LANES]], axis=0)


def _attn_kernel(seq_len, sink_ref, q_ref, k_ref, kp_ref, kn_ref, v_ref, vp_ref, vn_ref, kc_ref, vc_ref,
                 p_ref, pp_ref, pn_ref, x_ref, mod_ref, band_ref, poolw_ref, pscale_ref, wout_ref,
                 g2_ref, wrh_ref, wrl_ref, x1_ref, h2_ref, lg_ref, kwin, vwin, pext, mix, s_scr, p_scr, m_scr):
    n = pl.program_id(1)
    n_last = pl.num_programs(1) - 1

    kwin[0:QB, :] = kp_ref[0]
    kwin[QB:QB + TQ, :] = k_ref[0]
    kwin[QB + TQ:, :] = kn_ref[0]
    vwin[0:QB, :] = vp_ref[0]
    vwin[QB:QB + TQ, :] = v_ref[0]
    vwin[QB + TQ:, :] = vn_ref[0]

    pext[0:QB - 8, :] = jnp.zeros((QB - 8, POOL_WIDTH), F32)
    pext[QB - 8:QB, :] = jnp.where(n > 0, pp_ref[0], 0.0)
    pext[QB:QB + TQ, :] = p_ref[0]
    pext[QB + TQ:QB + TQ + 8, :] = jnp.where(n < n_last, pn_ref[0], 0.0)
    pext[QB + TQ + 8:, :] = jnp.zeros((QB - 8, POOL_WIDTH), F32)

    row = lax.broadcasted_iota(jnp.int32, (QB, 3 * QB), 0)
    col = lax.broadcasted_iota(jnp.int32, (QB, 3 * QB), 1)
    in_band = (col >= row) & (col <= row + 2 * WINDOW)
    tok = lax.broadcasted_iota(jnp.int32, (QB, 1), 0)
    kc = kc_ref[0]
    kc_rows = [_stack_variants(kc, kv) for kv in range(N_KV_HEADS)]
    vc_rows = [_stack_variants(vc_ref[0], kv) for kv in range(N_KV_HEADS)]

    def sub_block(j, carry):
        r0 = pl.multiple_of(j * QB, QB)
        qj = q_ref[0, pl.ds(r0, QB), :]
        kw = kwin[pl.ds(r0, 3 * QB), :]
        vw = vwin[pl.ds(r0, 3 * QB), :]
        kpos = col + (n * TQ + j * QB - QB)
        ok = in_band & (kpos >= 0) & (kpos < seq_len)
        bias = jnp.where(ok, 0.0, -jnp.inf)
        n_loc, n_ctx = 3 * QB, kc.shape[0]
        bias2 = jnp.concatenate([bias, bias], axis=1)
        k_rows = [_stack_variants(kw, kv) for kv in range(N_KV_HEADS)]
        v_rows = [_stack_variants(vw, kv) for kv in range(N_KV_HEADS)]
        group = N_HEADS // N_KV_HEADS

        def head_tiles(head):
            loc0 = (head % 2) * n_loc
            ctx0 = 2 * n_loc + (head % 2) * n_ctx
            return ([loc0 + i * LANES for i in range(n_loc // LANES)]
                    + [ctx0 + i * LANES for i in range(n_ctx // LANES)])

        for c in range(N_HEADS // 2):
            qc = qj[:, c * LANES:(c + 1) * LANES]
            s_scr[c, :, 0:2 * n_loc] = _dot_nt(qc, k_rows[2 * c // group]) + bias2
            s_scr[c, :, 2 * n_loc:] = _dot_nt(qc, kc_rows[2 * c // group])
        for head in range(N_HEADS):
            tiles = [s_scr[head // 2, :, st:st + LANES] for st in head_tiles(head)]
            row_max = jnp.max(_fold(jnp.maximum, tiles), axis=1, keepdims=True)
            m_scr[head] = jnp.broadcast_to(jnp.maximum(row_max, sink_ref[head] * LOG2E), (QB, LANES))
        for head in range(N_HEADS):
            m = m_scr[head]
            acc = None
            for st in head_tiles(head):
                p = jnp.exp2(s_scr[head // 2, :, st:st + LANES] - m)
                p_scr[head // 2, :, st:st + LANES] = p.astype(BF16)
                acc = p if acc is None else acc + p
            denom = (jnp.broadcast_to(jnp.sum(acc, axis=1, keepdims=True), (QB, LANES))
                     + jnp.exp2(sink_ref[head] * LOG2E - m))
            m_scr[head] = 1.0 / denom
        lane = lax.broadcasted_iota(jnp.int32, (QB, LANES), 1)
        for c in range(N_HEADS // 2):
            o = (_dot(p_scr[c, :, 0:2 * n_loc], v_rows[2 * c // group])
                 + _dot(p_scr[c, :, 2 * n_loc:], vc_rows[2 * c // group]))
            o = o * jnp.where(lane < HEAD_DIM, m_scr[2 * c], m_scr[2 * c + 1])
            mix[pl.ds(r0, QB), c * LANES:(c + 1) * LANES] = o.astype(BF16)

        slab = pext[pl.ds(pl.multiple_of(r0 + POOL_OFF, 8), POOL_SLAB), :]
        tpos = tok + (n * TQ + j * QB)
        for g, w in enumerate(POOL_WINDOWS):
            sg = slab[:, g * LANES:(g + 1) * LANES]
            hi, lo = _split_bf16(sg)
            band = band_ref[g]
            wsum = _dot(band, hi) + _dot(band, lo)
            cnt = (jnp.minimum(tpos - w // 2 + w, seq_len) - jnp.maximum(tpos - w // 2, 0)).astype(F32)
            dlt = wsum / cnt - sg[POOL_OFF:POOL_OFF + QB, :]
            y = _dot(dlt.astype(BF16), poolw_ref[g]) * pscale_ref[:, g * LANES:(g + 1) * LANES]
            mix[pl.ds(r0, QB), ATTN_WIDTH + g * LANES:ATTN_WIDTH + (g + 1) * LANES] = y.astype(BF16)
        return carry

    lax.fori_loop(0, TQ // QB, sub_block, 0)

    proj = _dot(mix[...], wout_ref[...])
    x1 = x_ref[0] + mod_ref[0, 2:3, :] * proj
    x1_ref[0] = x1
    h2 = _norm_mod(x1, g2_ref[...], mod_ref[0, 3:4, :], mod_ref[0, 4:5, :])
    h2_ref[...] = _pack_words(h2)
    h_hi, h_lo = _split_bf16(h2)
    wrh = wrh_ref[...]
    lg_ref[...] = _dot_nt(wrh, h_hi) + _dot_nt(wrh, h_lo) + _dot_nt(wrl_ref[...], h_hi)


def _attn(b0, b, sink, q, k4, v4, kc4, vc4, p, x, mod, band, poolw_bf, pscale, wout_bf, g2, wr_hi, wr_lo):
    _, s, d = x.shape
    c = kc4.shape[1]
    nt = s // TQ
    hb = TQ // QB
    pb = TQ // 8
    kv_main = pl.BlockSpec((1, TQ, 4 * LANES), lambda bi, n: (b0 + bi, n, 0))
    kv_prev = pl.BlockSpec((1, QB, 4 * LANES), lambda bi, n: (b0 + bi, jnp.maximum(n * hb - 1, 0), 0))
    kv_next = pl.BlockSpec((1, QB, 4 * LANES), lambda bi, n: (b0 + bi, jnp.minimum((n + 1) * hb, s // QB - 1), 0))
    const2 = lambda bi, n: (0, 0)
    const3 = lambda bi, n: (0, 0, 0)
    return pl.pallas_call(
        functools.partial(_attn_kernel, s),
        out_shape=(jax.ShapeDtypeStruct((b, s, d), F32),
                   jax.ShapeDtypeStruct((b * s, d // 2), U32),
                   jax.ShapeDtypeStruct((N_EXPERTS, b * s), F32)),
        grid=(b, nt),
        in_specs=[pl.BlockSpec(memory_space=pltpu.SMEM),
                  pl.BlockSpec((1, TQ, ATTN_WIDTH), lambda bi, n: (b0 + bi, n, 0)),
                  kv_main, kv_prev, kv_next, kv_main, kv_prev, kv_next,
                  pl.BlockSpec((1, c, 4 * LANES), lambda bi, n: (b0 + bi, 0, 0)),
                  pl.BlockSpec((1, c, 4 * LANES), lambda bi, n: (b0 + bi, 0, 0)),
                  pl.BlockSpec((1, TQ, POOL_WIDTH), lambda bi, n: (b0 + bi, n, 0)),
                  pl.BlockSpec((1, 8, POOL_WIDTH), lambda bi, n: (b0 + bi, jnp.maximum(n * pb - 1, 0), 0)),
                  pl.BlockSpec((1, 8, POOL_WIDTH),
                               lambda bi, n: (b0 + bi, jnp.minimum((n + 1) * pb, s // 8 - 1), 0)),
                  pl.BlockSpec((1, TQ, d), lambda bi, n: (b0 + bi, n, 0)),
                  pl.BlockSpec((1, 6, d), lambda bi, n: (b0 + bi, 0, 0)),
                  pl.BlockSpec((len(POOL_WINDOWS), QB, POOL_SLAB), const3),
                  pl.BlockSpec((len(POOL_WINDOWS), POOL_GROUP_DIM, POOL_GROUP_DIM), const3),
                  pl.BlockSpec((1, POOL_WIDTH), const2),
                  pl.BlockSpec((d, d), const2),
                  pl.BlockSpec((1, d), const2),
                  pl.BlockSpec((N_EXPERTS, d), const2),
                  pl.BlockSpec((N_EXPERTS, d), const2)],
        out_specs=(pl.BlockSpec((1, TQ, d), lambda bi, n: (bi, n, 0)),
                   pl.BlockSpec((TQ, d // 2), lambda bi, n: (bi * nt + n, 0)),
                   pl.BlockSpec((N_EXPERTS, TQ), lambda bi, n: (0, bi * nt + n))),
        scratch_shapes=[pltpu.VMEM((TQ + 2 * QB, 4 * LANES), BF16),
                        pltpu.VMEM((TQ + 2 * QB, 4 * LANES), BF16),
                        pltpu.VMEM((TQ + 2 * QB, POOL_WIDTH), F32),
                        pltpu.VMEM((TQ, d), BF16),
                        pltpu.VMEM((N_HEADS // 2, QB, 2 * (3 * QB + c)), F32),
                        pltpu.VMEM((N_HEADS // 2, QB, 2 * (3 * QB + c)), BF16),
                        pltpu.VMEM((N_HEADS, QB, LANES), F32)],
        compiler_params=pltpu.CompilerParams(vmem_limit_bytes=VMEM_LIMIT),
        name="attn",
    )(sink, q, k4, k4, k4, v4, v4, v4, kc4, vc4, p, p, p, x, mod, band, poolw_bf, pscale, wout_bf,
      g2, wr_hi, wr_lo)


def _first_argmax_rows(v, row_iota, n_rows):
    m = jnp.max(v, axis=0, keepdims=True)
    idx = jnp.min(jnp.where(v == m, row_iota, n_rows), axis=0, keepdims=True)
    return m, idx


def _route_kernel(lg_ref, bias_ref, tri_ref, idx_ref, gate_ref, rank_ref, cnt_ref, carry):
    i = pl.program_id(0)

    @pl.when(i == 0)
    def _():
        carry[...] = jnp.zeros_like(carry)

    scores = 1.0 / (1.0 + jnp.exp(-lg_ref[...]))
    biased = scores + bias_ref[...]
    e_iota = lax.broadcasted_iota(jnp.int32, scores.shape, 0).astype(F32)
    g_iota = lax.broadcasted_iota(jnp.int32, (EXPERTS_PER_GROUP, TR), 0).astype(F32)
    neg = -jnp.inf

    grp = []
    for g in range(N_EXPERT_GROUPS):
        blk = biased[g * EXPERTS_PER_GROUP:(g + 1) * EXPERTS_PER_GROUP, :]
        m1, i1 = _first_argmax_rows(blk, g_iota, float(EXPERTS_PER_GROUP))
        m2 = jnp.max(jnp.where(g_iota == i1, neg, blk), axis=0, keepdims=True)
        grp.append(m1 + m2)
    grp = jnp.concatenate(grp, axis=0)
    gg_iota = lax.broadcasted_iota(jnp.int32, grp.shape, 0).astype(F32)
    grp_sel = jnp.zeros(grp.shape, F32)
    for _ in range(TOPK_GROUPS):
        _, gi = _first_argmax_rows(grp, gg_iota, float(N_EXPERT_GROUPS))
        hit = gg_iota == gi
        grp_sel = jnp.where(hit, 1.0, grp_sel)
        grp = jnp.where(hit, neg, grp)
    allowed = jnp.concatenate(
        [jnp.broadcast_to(grp_sel[g:g + 1, :], (EXPERTS_PER_GROUP, TR)) for g in range(N_EXPERT_GROUPS)], axis=0)
    masked = jnp.where(allowed > 0.5, biased, neg)

    idxs, gates = [], []
    onehot = jnp.zeros(scores.shape, F32)
    for _ in range(TOP_K):
        _, ei = _first_argmax_rows(masked, e_iota, float(N_EXPERTS))
        hit = e_iota == ei
        idxs.append(ei)
        gates.append(jnp.sum(jnp.where(hit, scores, 0.0), axis=0, keepdims=True))
        onehot = jnp.where(hit, 1.0, onehot)
        masked = jnp.where(hit, neg, masked)
    idx = jnp.concatenate(idxs, axis=0)
    gate = jnp.concatenate(gates, axis=0)
    gate = gate / jnp.sum(gate, axis=0, keepdims=True) * ROUTED_SCALE

    before = _dot(onehot.astype(BF16), tri_ref[...]) + carry[:, 0:1]
    ranks = [jnp.sum(jnp.where(e_iota == idxs[k], before, 0.0), axis=0, keepdims=True) for k in range(TOP_K)]
    idx_ref[...] = idx.astype(jnp.int32)
    gate_ref[...] = gate
    rank_ref[...] = jnp.concatenate(ranks, axis=0).astype(jnp.int32)
    total = carry[...] + jnp.sum(onehot, axis=1, keepdims=True)
    carry[...] = total
    cnt_ref[...] = total


def _route(lg_t, bias, tri):
    e, t = lg_t.shape
    tok = pl.BlockSpec((TOP_K, TR), lambda i: (0, i))
    return pl.pallas_call(
        _route_kernel,
        out_shape=(jax.ShapeDtypeStruct((TOP_K, t), jnp.int32),
                   jax.ShapeDtypeStruct((TOP_K, t), F32),
                   jax.ShapeDtypeStruct((TOP_K, t), jnp.int32),
                   jax.ShapeDtypeStruct((e, LANES), F32)),
        grid=(t // TR,),
        in_specs=[pl.BlockSpec((e, TR), lambda i: (0, i)),
                  pl.BlockSpec((e, 1), lambda i: (0, 0)),
                  pl.BlockSpec((TR, TR), lambda i: (0, 0))],
        out_specs=(tok, tok, tok, pl.BlockSpec((e, LANES), lambda i: (0, 0))),
        scratch_shapes=[pltpu.VMEM((e, LANES), F32)],
        compiler_params=pltpu.CompilerParams(vmem_limit_bytes=VMEM_LIMIT),
        name="route",
    )(lg_t, bias, tri)


def _plan_kernel(n_blocks, size_ref, start_ref, expert_ref, valid_ref, nact_ref):
    def per_expert(e, first_block):
        size = size_ref[e]
        n_blk = (size + TM_EXP - 1) // TM_EXP
        start_ref[e] = first_block * TM_EXP

        def per_block(j, carry):
            expert_ref[first_block + j] = e
            valid_ref[first_block + j] = jnp.minimum(size - j * TM_EXP, TM_EXP)
            return carry

        lax.fori_loop(0, n_blk, per_block, 0)
        return first_block + n_blk

    n_active = lax.fori_loop(0, N_EXPERTS, per_expert, 0)
    nact_ref[0] = n_active

    def unused(i, carry):
        expert_ref[i] = N_EXPERTS - 1
        valid_ref[i] = 0
        return carry

    lax.fori_loop(n_active, n_blocks, unused, 0)


def _plan_blocks(sizes, n_blocks):
    smem = pl.BlockSpec(memory_space=pltpu.SMEM)
    return pl.pallas_call(
        functools.partial(_plan_kernel, n_blocks),
        out_shape=(jax.ShapeDtypeStruct((N_EXPERTS,), jnp.int32),
                   jax.ShapeDtypeStruct((n_blocks,), jnp.int32),
                   jax.ShapeDtypeStruct((n_blocks,), jnp.int32),
                   jax.ShapeDtypeStruct((1,), jnp.int32)),
        in_specs=[smem],
        out_specs=(smem, smem, smem, smem),
        name="plan_blocks",
    )(sizes)


def _dest_kernel(start_ref, idx_ref, rank_ref, dest_ref):
    idx = idx_ref[...]
    dest = rank_ref[...]
    for e in range(N_EXPERTS):
        dest = dest + jnp.where(idx == e, start_ref[e], 0)
    dest_ref[...] = dest


def _dest_rows(pad_start, idx_kt, rank_kt):
    n_k, t = idx_kt.shape
    tile = 4096
    blk = pl.BlockSpec((n_k, tile), lambda i: (0, i))
    return pl.pallas_call(
        _dest_kernel,
        out_shape=jax.ShapeDtypeStruct((n_k, t), jnp.int32),
        grid=(t // tile,),
        in_specs=[pl.BlockSpec(memory_space=pltpu.SMEM), blk, blk],
        out_specs=blk,
        name="dest_rows",
    )(pad_start, idx_kt, rank_kt)


def _sc_mesh():
    return plsc.VectorSubcoreMesh(core_axis_name="c", subcore_axis_name="s")


def _sc_token_base(steps, j):
    worker = lax.axis_index("s") * SC_CORES + lax.axis_index("c")
    return (worker * steps + j) * SC_TOKENS


def _sc_scatter(dest_kt, h2p, n_rows):
    t, width = h2p.shape
    steps = t // (SC_WORKERS * SC_TOKENS)

    @functools.partial(
        pl.kernel, mesh=_sc_mesh(),
        out_type=jax.ShapeDtypeStruct((n_rows, width), U32),
        scratch_types=[pltpu.VMEM((TOP_K, SC_TOKENS), jnp.int32),
                       pltpu.VMEM((SC_TOKENS, width), U32),
                       pltpu.SemaphoreType.DMA],
        name="sc_scatter",
    )
    def body(dest_hbm, h_hbm, xs_hbm, idx_v, rows_v, sem):
        @pl.loop(0, steps)
        def _(j):
            base = _sc_token_base(steps, j)
            pltpu.sync_copy(dest_hbm.at[:, pl.ds(base, SC_TOKENS)], idx_v)
            pltpu.sync_copy(h_hbm.at[pl.ds(base, SC_TOKENS)], rows_v)
            copies = [pltpu.async_copy(rows_v, xs_hbm.at[idx_v.at[k]], sem) for k in range(TOP_K)]
            for cp in copies:
                cp.wait()

    return body(dest_kt, h2p)


def _sc_gather(dest_kt, ys, token0, n_tokens):
    n_k = dest_kt.shape[0]
    width = ys.shape[1]
    steps = n_tokens // (SC_WORKERS * SC_TOKENS)

    half = SC_TOKENS // 2
    units = [(k, h) for k in range(n_k) for h in range(2)]
    n_buf = 3

    @functools.partial(
        pl.kernel, mesh=_sc_mesh(),
        out_type=jax.ShapeDtypeStruct((n_k, n_tokens, width), U32),
        scratch_types=[pltpu.VMEM((n_k, SC_TOKENS), jnp.int32),
                       pltpu.VMEM((n_buf, half, width), U32),
                       pltpu.SemaphoreType.DMA((n_buf,))],
        name="sc_gather",
    )
    def body(dest_hbm, ys_hbm, yk_hbm, idx_v, rows_v, sems):
        @pl.loop(0, steps)
        def _(j):
            base = _sc_token_base(steps, j)
            pltpu.sync_copy(dest_hbm.at[:, pl.ds(token0 + base, SC_TOKENS)], idx_v)

            def gather(u):
                k, h = units[u]
                slot = u % n_buf
                return pltpu.make_async_copy(ys_hbm.at[idx_v.at[k, pl.ds(h * half, half)]], rows_v.at[slot],
                                             sems.at[slot])

            gather(0).start()
            gather(1).start()
            for u, (k, h) in enumerate(units):
                gather(u).wait()
                pltpu.sync_copy(rows_v.at[u % n_buf], yk_hbm.at[k, pl.ds(base + h * half, half)])
                if u + 2 < len(units):
                    gather(u + 2).start()

    return body(dest_kt, ys)


def _experts_kernel(be_ref, valid_ref, nact_ref, xs_hbm, wg_ref, wu_ref, wd_ref, ys_ref, wg_bf, wu_bf, wd_bf,
                    xbuf, sems):
    i = pl.program_id(0)
    n_active = nact_ref[0]
    slot = lax.rem(i, EXP_SLOTS)

    def fetch(j):
        src = xs_hbm.at[pl.ds(pl.multiple_of(j * TM_EXP, TM_EXP), TM_EXP), :]
        s = lax.rem(j, EXP_SLOTS)
        return pltpu.make_async_copy(src, xbuf.at[s], sems.at[s])

    @pl.when(i == 0)
    def _():
        for j in range(EXP_SLOTS - 1):
            pl.when(j < n_active)(lambda j=j: fetch(j).start())

    @pl.when(i + (EXP_SLOTS - 1) < n_active)
    def _():
        fetch(i + (EXP_SLOTS - 1)).start()

    prev = be_ref[jnp.maximum(i - 1, 0)]

    @pl.when((i == 0) | (be_ref[i] != prev))
    def _():
        wg_bf[...] = wg_ref[0].astype(BF16)
        wu_bf[...] = wu_ref[0].astype(BF16)
        wd_bf[...] = wd_ref[0].astype(BF16)

    @pl.when(i < n_active)
    def _():
        fetch(i).wait()

    valid = jnp.where(i < n_active, valid_ref[i], 0)

    def run(rows):
        row = lax.broadcasted_iota(jnp.int32, (rows, xbuf.shape[2]), 0)
        words = jnp.where(row < valid, xbuf[slot, 0:rows, :], jnp.uint32(0))
        xb = _unpack_words(words).astype(BF16)
        hid = _silu(_dot(xb, wg_bf[...])) * _dot(xb, wu_bf[...])
        ys_ref[0:rows, :] = _pack_words(_dot(hid.astype(BF16), wd_bf[...]))
        if rows < TM_EXP:
            ys_ref[rows:, :] = jnp.zeros((TM_EXP - rows, ys_ref.shape[1]), U32)

    for rows in range(EXP_MIN_ROWS, TM_EXP + 1, EXP_MIN_ROWS):
        pl.when((valid > rows - EXP_MIN_ROWS) & (valid <= rows))(functools.partial(run, rows))

    @pl.when(valid == 0)
    def _():
        ys_ref[...] = jnp.zeros_like(ys_ref)


def _experts(block_e, block_valid, n_active, xs, w_gate, w_up, w_down):
    n_rows, half = xs.shape
    d = w_gate.shape[1]
    nb = n_rows // TM_EXP
    grid_spec = pltpu.PrefetchScalarGridSpec(
        num_scalar_prefetch=3,
        grid=(nb,),
        in_specs=[pl.BlockSpec(memory_space=pl.ANY),
                  pl.BlockSpec((1, d, D_EXPERT), lambda i, be, bv, na: (be[i], 0, 0)),
                  pl.BlockSpec((1, d, D_EXPERT), lambda i, be, bv, na: (be[i], 0, 0)),
                  pl.BlockSpec((1, D_EXPERT, d), lambda i, be, bv, na: (be[i], 0, 0))],
        out_specs=pl.BlockSpec((TM_EXP, half), lambda i, be, bv, na: (i, 0)),
        scratch_shapes=[pltpu.VMEM((d, D_EXPERT), BF16), pltpu.VMEM((d, D_EXPERT), BF16),
                        pltpu.VMEM((D_EXPERT, d), BF16),
                        pltpu.VMEM((EXP_SLOTS, TM_EXP, half), U32),
                        pltpu.SemaphoreType.DMA((EXP_SLOTS,))],
    )
    return pl.pallas_call(
        _experts_kernel,
        out_shape=jax.ShapeDtypeStruct((n_rows, half), U32),
        grid_spec=grid_spec,
        compiler_params=pltpu.CompilerParams(vmem_limit_bytes=VMEM_LIMIT, dimension_semantics=("arbitrary",)),
        name="experts",
    )(block_e, block_valid, n_active, xs, w_gate, w_up, w_down)


def _combine_kernel(yk_ref, x1_ref, h2_ref, gate_ref, mod_ref, wsg_ref, wsu_ref, wsd_ref, fg_ref, *out_refs):
    out_ref = out_refs[-1]
    hb = _unpack_words(h2_ref[...]).astype(BF16)
    hid = _silu(_dot(hb, wsg_ref[...])) * _dot(hb, wsu_ref[...])
    ffn = _dot(hid.astype(BF16), wsd_ref[...])
    gate = gate_ref[...]
    for k in range(TOP_K):
        ffn = ffn + gate[:, k:k + 1] * _unpack_words(yk_ref[k])
    x2 = x1_ref[...] + mod_ref[0, 5:6, :] * ffn
    ms = jnp.mean(x2 * x2, axis=-1, keepdims=True)
    out_ref[...] = x2 * lax.rsqrt(ms + EPS) * fg_ref[...]


def _combine(yk, token0, out_token0, n_out_tokens, prev_out, x1, h2p, gate_tk, mod, wsg_bf, wsu_bf, wsd_bf, final_g,
             seq_len):
    t, d = n_out_tokens, x1.shape[1]
    tiles_per_seq = seq_len // TF
    tile0 = token0 // TF
    out_tile0 = out_token0 // TF
    in_tok = pl.BlockSpec((TF, d), lambda i: (tile0 + i, 0))
    tok = pl.BlockSpec((TF, d), lambda i: (out_tile0 + i, 0))
    in_specs = [pl.BlockSpec((TOP_K, TF, d // 2), lambda i: (0, i, 0)),
                in_tok, pl.BlockSpec((TF, d // 2), lambda i: (tile0 + i, 0)),
                pl.BlockSpec((TF, TOP_K), lambda i: (tile0 + i, 0)),
                pl.BlockSpec((1, 6, d), lambda i: ((out_tile0 + i) // tiles_per_seq, 0, 0)),
                pl.BlockSpec((d, D_SHARED), lambda i: (0, 0)),
                pl.BlockSpec((d, D_SHARED), lambda i: (0, 0)),
                pl.BlockSpec((D_SHARED, d), lambda i: (0, 0)),
                pl.BlockSpec((1, d), lambda i: (0, 0))]
    args = [yk, x1, h2p, gate_tk, mod, wsg_bf, wsu_bf, wsd_bf, final_g]
    aliases = {}
    if prev_out is not None:
        in_specs.append(pl.BlockSpec(memory_space=pl.ANY))
        args.append(prev_out)
        aliases = {len(args) - 1: 0}
    return pl.pallas_call(
        _combine_kernel,
        out_shape=jax.ShapeDtypeStruct((t, d), F32),
        grid=(yk.shape[1] // TF,),
        in_specs=in_specs,
        out_specs=tok,
        input_output_aliases=aliases,
        compiler_params=pltpu.CompilerParams(vmem_limit_bytes=VMEM_LIMIT),
        name="combine",
    )(*args)


def _rope_tables(n_tokens):
    n_rows = n_tokens // GRID_W
    n_freq = HEAD_DIM // 4
    inv_freq = ROPE_THETA ** (-jnp.arange(n_freq, dtype=F32) / n_freq)
    ang_r = jnp.arange(n_rows).astype(F32)[:, None] * inv_freq[None, :]
    ang_c = jnp.arange(GRID_W).astype(F32)[:, None] * inv_freq[None, :]

    def per_token(row_part, col_part):
        rows = jnp.broadcast_to(row_part[:, None, :], (n_rows, GRID_W, n_freq))
        cols = jnp.broadcast_to(col_part[None, :, :], (n_rows, GRID_W, n_freq))
        return rows.reshape(n_tokens, n_freq), cols.reshape(n_tokens, n_freq)

    cos_r, cos_c = per_token(jnp.cos(ang_r), jnp.cos(ang_c))
    sin_r, sin_c = per_token(jnp.sin(ang_r), jnp.sin(ang_c))
    cos = jnp.concatenate([cos_r, cos_r, cos_c, cos_c], axis=1)
    sin = jnp.concatenate([-sin_r, sin_r, -sin_c, sin_c], axis=1)
    reps = LANES // HEAD_DIM
    return jnp.tile(cos, (1, reps)), jnp.tile(sin, (1, reps))


def _pool_bands():
    i = jnp.arange(QB)[:, None]
    r = jnp.arange(POOL_SLAB)[None, :]
    return jnp.stack([((r >= i + POOL_OFF - w // 2) & (r < i + POOL_OFF + w // 2)).astype(BF16)
                      for w in POOL_WINDOWS])


def kernel(x, c, ctx, c_ctx, w_ada, b_ada, norm1_g, norm2_g, w_in, attn_sink, pool_w, pool_scale, w_out,
           w_router, router_bias, w_gate, w_up, w_down, ws_gate, ws_up, ws_down, final_g):
    b, s, d = x.shape
    t = b * s
    assert w_ada.shape[0] == 1 and d == D_MODEL and s % TQ == 0 and b + 1 <= 8

    c8 = jnp.zeros((8, d), F32).at[:b].set(c).at[b].set(c_ctx)
    mod = _ada(c8, w_ada[0], b_ada[0]).reshape(8, 6, d)
    g1 = norm1_g[0].reshape(1, d)
    g2 = norm2_g[0].reshape(1, d)
    w_in_bf = w_in[0].astype(BF16)
    cos_t, sin_t = _rope_tables(s)

    q, k4, v4, p = _inproj(x, mod, g1, w_in_bf, cos_t, sin_t)
    kc4, vc4 = _ctxproj(ctx, mod[b:b + 1], g1, w_in_bf[:, ATTN_WIDTH:ATTN_WIDTH + 2 * KV_WIDTH])

    wr_t = w_router[0].T
    wr_hi = wr_t.astype(BF16)
    wr_lo = (wr_t - wr_hi.astype(F32)).astype(BF16)
    attn_consts = (_pool_bands(), pool_w[0].astype(BF16), pool_scale[0].reshape(1, POOL_WIDTH),
                   w_out[0].astype(BF16), g2, wr_hi, wr_lo)
    tri = jnp.triu(jnp.ones((TR, TR), BF16), k=1)
    shared_w = (ws_gate[0].astype(BF16), ws_up[0].astype(BF16), ws_down[0].astype(BF16))

    nb = b // TOKEN_GROUPS
    tg = nb * s
    assert b % TOKEN_GROUPS == 0 and tg % (COMBINE_CHUNKS * SC_WORKERS * SC_TOKENS) == 0
    n_rows = -(-(tg * TOP_K + N_EXPERTS * (TM_EXP - 1)) // TM_EXP) * TM_EXP
    groups = []
    for g in range(TOKEN_GROUPS):
        x1, h2p, lg_t = _attn(g * nb, nb, attn_sink[0], q, k4, v4, kc4, vc4, p, x, mod, *attn_consts)
        idx_kt, gate_kt, rank_kt, counts = _route(lg_t, router_bias[0].reshape(N_EXPERTS, 1), tri)
        pad_start, block_e, block_valid, n_active = _plan_blocks(counts[:, 0].astype(jnp.int32), n_rows // TM_EXP)
        dest_kt = _dest_rows(pad_start, idx_kt, rank_kt)
        xs = _sc_scatter(dest_kt, h2p, n_rows)
        groups.append((x1.reshape(tg, d), h2p, gate_kt.T, dest_kt, xs, block_e, block_valid, n_active))

    out = None
    chunk = tg // COMBINE_CHUNKS
    for g, (x1, h2p, gate_tk, dest_kt, xs, block_e, block_valid, n_active) in enumerate(groups):
        ys = _experts(block_e, block_valid, n_active, xs, w_gate[0], w_up[0], w_down[0])
        for token0 in range(0, tg, chunk):
            out = _combine(_sc_gather(dest_kt, ys, token0, chunk), token0, g * tg + token0, t, out, x1, h2p,
                           gate_tk, mod, *shared_w, final_g.reshape(1, d), s)
    return out.reshape(b, s, d)
```

```python
import functools

import jax
import jax.numpy as jnp
from jax import lax
from jax.experimental import pallas as pl
from jax.experimental.pallas import tpu as pltpu
from jax.experimental.pallas import tpu_sc as plsc

F32 = jnp.float32
BF16 = jnp.bfloat16

D_MODEL = 1024
GRID_W = 64
N_HEADS = 8
N_KV_HEADS = 2
HEAD_DIM = 64
ATTN_WIDTH = N_HEADS * HEAD_DIM
KV_WIDTH = N_KV_HEADS * HEAD_DIM
WINDOW = 128
ROPE_THETA = 10000.0
POOL_WINDOWS = (2, 4, 8, 16)
POOL_WIDTH = D_MODEL - ATTN_WIDTH
POOL_GROUP_DIM = POOL_WIDTH // len(POOL_WINDOWS)
IN_COLS = ATTN_WIDTH + 2 * KV_WIDTH + POOL_WIDTH
N_EXPERTS = 64
TOP_K = 8
N_EXPERT_GROUPS = 8
EXPERTS_PER_GROUP = N_EXPERTS // N_EXPERT_GROUPS
TOPK_GROUPS = 4
D_EXPERT = 256
D_SHARED = 256
ROUTED_SCALE = 2.5
EPS = 1e-6
LOG2E = 1.4426950408889634

LANES = 128
U32 = jnp.uint32
VMEM_LIMIT = 48 * 1024 * 1024

TM_PROJ = 1024
TQ = 1024
QB = 128
POOL_SLAB = 256
POOL_OFF = 64
TR = 512
TM_EXP = 1024
EXP_MIN_ROWS = 256
EXP_SLOTS = 4
TF = 512
TOKEN_GROUPS = 2
COMBINE_CHUNKS = 2
SC_CORES = 2
SC_WORKERS = 32
SC_TOKENS = 128


def _silu(x):
    return x * (1.0 / (1.0 + jnp.exp(-x)))


def _split_bf16(x):
    hi = x.astype(BF16)
    lo = (x - hi.astype(F32)).astype(BF16)
    return hi, lo


def _dot(a, b):
    return jnp.dot(a, b, preferred_element_type=F32)


def _pack_words(val):
    half = val.shape[1] // 2
    lo = lax.bitcast_convert_type(val[:, :half].astype(BF16).astype(F32), U32)
    hi = lax.bitcast_convert_type(val[:, half:].astype(BF16).astype(F32), U32)
    return lax.shift_right_logical(lo, jnp.uint32(16)) | hi


def _unpack_words(words):
    lo = lax.bitcast_convert_type(lax.shift_left(words, jnp.uint32(16)), F32)
    hi = lax.bitcast_convert_type(words & jnp.uint32(0xFFFF0000), F32)
    return jnp.concatenate([lo, hi], axis=1)


def _dot_nt(a, b):
    return lax.dot_general(a, b, (((1,), (1,)), ((), ())), preferred_element_type=F32)


def _ada_kernel(c_ref, w_ref, b_ref, o_ref):
    a_hi, a_lo = _split_bf16(_silu(c_ref[...]))
    w_hi, w_lo = _split_bf16(w_ref[...])
    o_ref[...] = _dot(a_hi, w_hi) + _dot(a_lo, w_hi) + _dot(a_hi, w_lo) + b_ref[...]


def _ada(c8, w_ada, b_ada):
    d = c8.shape[1]
    n = w_ada.shape[1]
    tn = 512
    return pl.pallas_call(
        _ada_kernel,
        out_shape=jax.ShapeDtypeStruct((8, n), F32),
        grid=(n // tn,),
        in_specs=[pl.BlockSpec((8, d), lambda j: (0, 0)),
                  pl.BlockSpec((d, tn), lambda j: (0, j)),
                  pl.BlockSpec((1, tn), lambda j: (0, j))],
        out_specs=pl.BlockSpec((8, tn), lambda j: (0, j)),
        compiler_params=pltpu.CompilerParams(vmem_limit_bytes=VMEM_LIMIT),
        name="ada",
    )(c8, w_ada, b_ada.reshape(1, n))


def _norm_mod(x, g, shift, scale):
    ms = jnp.mean(x * x, axis=-1, keepdims=True)
    return (x * lax.rsqrt(ms + EPS) * g) * (1.0 + scale) + shift


def _lane_variants(t):
    lane = lax.broadcasted_iota(jnp.int32, t.shape, 1)
    lo = lane < HEAD_DIM
    tr = pltpu.roll(t, HEAD_DIM, 1)
    zero = jnp.zeros_like(t)
    return (jnp.where(lo, t, zero), jnp.where(lo, zero, tr),
            jnp.where(lo, tr, zero), jnp.where(lo, zero, t))


def _store_variants(ref, t):
    for i, var in enumerate(_lane_variants(t)):
        ref[0, :, i * LANES:(i + 1) * LANES] = var.astype(BF16)


def _inproj_kernel(x_ref, mod_ref, g_ref, w_ref, cos_ref, sin_ref, q_ref, k_ref, v_ref, p_ref):
    h = _norm_mod(x_ref[0], g_ref[...], mod_ref[0, 0:1, :], mod_ref[0, 1:2, :])
    z = _dot(h.astype(BF16), w_ref[...])
    cos = cos_ref[...]
    sin = sin_ref[...]
    lane = lax.broadcasted_iota(jnp.int32, cos.shape, 1)
    first_half = (lane & 16) == 0

    def rope(zc):
        partner = jnp.where(first_half, pltpu.roll(zc, LANES - 16, 1), pltpu.roll(zc, 16, 1))
        return zc * cos + partner * sin

    scale = HEAD_DIM ** -0.5 * LOG2E
    for c in range(ATTN_WIDTH // LANES):
        q_ref[0, :, c * LANES:(c + 1) * LANES] = (rope(z[:, c * LANES:(c + 1) * LANES]) * scale).astype(BF16)
    _store_variants(k_ref, rope(z[:, ATTN_WIDTH:ATTN_WIDTH + KV_WIDTH]))
    _store_variants(v_ref, z[:, ATTN_WIDTH + KV_WIDTH:ATTN_WIDTH + 2 * KV_WIDTH])
    p_ref[0] = z[:, ATTN_WIDTH + 2 * KV_WIDTH:]


def _inproj(x, mod, g1, w_in_bf, cos_t, sin_t):
    b, s, d = x.shape
    tm = TM_PROJ
    return pl.pallas_call(
        _inproj_kernel,
        out_shape=(jax.ShapeDtypeStruct((b, s, ATTN_WIDTH), BF16),
                   jax.ShapeDtypeStruct((b, s, 4 * LANES), BF16),
                   jax.ShapeDtypeStruct((b, s, 4 * LANES), BF16),
                   jax.ShapeDtypeStruct((b, s, POOL_WIDTH), F32)),
        grid=(s // tm, b),
        in_specs=[pl.BlockSpec((1, tm, d), lambda n, bi: (bi, n, 0)),
                  pl.BlockSpec((1, 6, d), lambda n, bi: (bi, 0, 0)),
                  pl.BlockSpec((1, d), lambda n, bi: (0, 0)),
                  pl.BlockSpec((d, IN_COLS), lambda n, bi: (0, 0)),
                  pl.BlockSpec((tm, LANES), lambda n, bi: (n, 0)),
                  pl.BlockSpec((tm, LANES), lambda n, bi: (n, 0))],
        out_specs=(pl.BlockSpec((1, tm, ATTN_WIDTH), lambda n, bi: (bi, n, 0)),
                   pl.BlockSpec((1, tm, 4 * LANES), lambda n, bi: (bi, n, 0)),
                   pl.BlockSpec((1, tm, 4 * LANES), lambda n, bi: (bi, n, 0)),
                   pl.BlockSpec((1, tm, POOL_WIDTH), lambda n, bi: (bi, n, 0))),
        compiler_params=pltpu.CompilerParams(vmem_limit_bytes=VMEM_LIMIT),
        name="inproj",
    )(x, mod, g1, w_in_bf, cos_t, sin_t)


def _ctxproj_kernel(x_ref, mod_ref, g_ref, w_ref, k_ref, v_ref):
    h = _norm_mod(x_ref[0], g_ref[...], mod_ref[0, 0:1, :], mod_ref[0, 1:2, :])
    z = _dot(h.astype(BF16), w_ref[...])
    _store_variants(k_ref, z[:, :KV_WIDTH])
    _store_variants(v_ref, z[:, KV_WIDTH:])


def _ctxproj(ctx, mod_c, g1, w_kv_bf):
    b, c, d = ctx.shape
    return pl.pallas_call(
        _ctxproj_kernel,
        out_shape=(jax.ShapeDtypeStruct((b, c, 4 * LANES), BF16),
                   jax.ShapeDtypeStruct((b, c, 4 * LANES), BF16)),
        grid=(b,),
        in_specs=[pl.BlockSpec((1, c, d), lambda bi: (bi, 0, 0)),
                  pl.BlockSpec((1, 6, d), lambda bi: (0, 0, 0)),
                  pl.BlockSpec((1, d), lambda bi: (0, 0)),
                  pl.BlockSpec((d, 2 * KV_WIDTH), lambda bi: (0, 0))],
        out_specs=(pl.BlockSpec((1, c, 4 * LANES), lambda bi: (bi, 0, 0)),
                   pl.BlockSpec((1, c, 4 * LANES), lambda bi: (bi, 0, 0))),
        compiler_params=pltpu.CompilerParams(vmem_limit_bytes=VMEM_LIMIT),
        name="ctxproj",
    )(ctx, mod_c, g1, w_kv_bf)


def _fold(op, tiles):
    while len(tiles) > 1:
        tiles = [op(tiles[i], tiles[i + 1]) if i + 1 < len(tiles) else tiles[i] for i in range(0, len(tiles), 2)]
    return tiles[0]


def _stack_variants(t4, kv):
    return jnp.concatenate([t4[:, (2 * kv) * LANES:(2 * kv + 1) * LANES],
                            t4[:, (2 * kv + 1) * LANES:(2 * kv + 2) * LANES]], axis=0)


def _attn_kernel(seq_len, sink_ref, q_ref, k_ref, kp_ref, kn_ref, v_ref, vp_ref, vn_ref, kc_ref, vc_ref,
                 p_ref, pp_ref, pn_ref, x_ref, mod_ref, band_ref, poolw_ref, pscale_ref, wout_ref,
                 g2_ref, wrh_ref, wrl_ref, x1_ref, h2_ref, lg_ref, kwin, vwin, pext, mix, s_scr, p_scr, m_scr):
    n = pl.program_id(1)
    n_last = pl.num_programs(1) - 1

    kwin[0:QB, :] = kp_ref[0]
    kwin[QB:QB + TQ, :] = k_ref[0]
    kwin[QB + TQ:, :] = kn_ref[0]
    vwin[0:QB, :] = vp_ref[0]
    vwin[QB:QB + TQ, :] = v_ref[0]
    vwin[QB + TQ:, :] = vn_ref[0]

    pext[0:QB - 8, :] = jnp.zeros((QB - 8, POOL_WIDTH), F32)
    pext[QB - 8:QB, :] = jnp.where(n > 0, pp_ref[0], 0.0)
    pext[QB:QB + TQ, :] = p_ref[0]
    pext[QB + TQ:QB + TQ + 8, :] = jnp.where(n < n_last, pn_ref[0], 0.0)
    pext[QB + TQ + 8:, :] = jnp.zeros((QB - 8, POOL_WIDTH), F32)

    row = lax.broadcasted_iota(jnp.int32, (QB, 3 * QB), 0)
    col = lax.broadcasted_iota(jnp.int32, (QB, 3 * QB), 1)
    in_band = (col >= row) & (col <= row + 2 * WINDOW)
    tok = lax.broadcasted_iota(jnp.int32, (QB, 1), 0)
    kc = kc_ref[0]
    kc_rows = [_stack_variants(kc, kv) for kv in range(N_KV_HEADS)]
    vc_rows = [_stack_variants(vc_ref[0], kv) for kv in range(N_KV_HEADS)]

    def sub_block(j, carry):
        r0 = pl.multiple_of(j * QB, QB)
        qj = q_ref[0, pl.ds(r0, QB), :]
        kw = kwin[pl.ds(r0, 3 * QB), :]
        vw = vwin[pl.ds(r0, 3 * QB), :]
        kpos = col + (n * TQ + j * QB - QB)
        ok = in_band & (kpos >= 0) & (kpos < seq_len)
        bias = jnp.where(ok, 0.0, -jnp.inf)
        n_loc, n_ctx = 3 * QB, kc.shape[0]
        bias2 = jnp.concatenate([bias, bias], axis=1)
        k_rows = [_stack_variants(kw, kv) for kv in range(N_KV_HEADS)]
        v_rows = [_stack_variants(vw, kv) for kv in range(N_KV_HEADS)]
        group = N_HEADS // N_KV_HEADS

        def head_tiles(head):
            loc0 = (head % 2) * n_loc
            ctx0 = 2 * n_loc + (head % 2) * n_ctx
            return ([loc0 + i * LANES for i in range(n_loc // LANES)]
                    + [ctx0 + i * LANES for i in range(n_ctx // LANES)])

        for c in range(N_HEADS // 2):
            qc = qj[:, c * LANES:(c + 1) * LANES]
            s_scr[c, :, 0:2 * n_loc] = _dot_nt(qc, k_rows[2 * c // group]) + bias2
            s_scr[c, :, 2 * n_loc:] = _dot_nt(qc, kc_rows[2 * c // group])
        for head in range(N_HEADS):
            tiles = [s_scr[head // 2, :, st:st + LANES] for st in head_tiles(head)]
            row_max = jnp.max(_fold(jnp.maximum, tiles), axis=1, keepdims=True)
            m_scr[head] = jnp.broadcast_to(jnp.maximum(row_max, sink_ref[head] * LOG2E), (QB, LANES))
        for head in range(N_HEADS):
            m = m_scr[head]
            acc = None
            for st in head_tiles(head):
                p = jnp.exp2(s_scr[head // 2, :, st:st + LANES] - m)
                p_scr[head // 2, :, st:st + LANES] = p.astype(BF16)
                acc = p if acc is None else acc + p
            denom = (jnp.broadcast_to(jnp.sum(acc, axis=1, keepdims=True), (QB, LANES))
                     + jnp.exp2(sink_ref[head] * LOG2E - m))
            m_scr[head] = 1.0 / denom
        lane = lax.broadcasted_iota(jnp.int32, (QB, LANES), 1)
        for c in range(N_HEADS // 2):
            o = (_dot(p_scr[c, :, 0:2 * n_loc], v_rows[2 * c // group])
                 + _dot(p_scr[c, :, 2 * n_loc:], vc_rows[2 * c // group]))
            o = o * jnp.where(lane < HEAD_DIM, m_scr[2 * c], m_scr[2 * c + 1])
            mix[pl.ds(r0, QB), c * LANES:(c + 1) * LANES] = o.astype(BF16)

        slab = pext[pl.ds(pl.multiple_of(r0 + POOL_OFF, 8), POOL_SLAB), :]
        tpos = tok + (n * TQ + j * QB)
        for g, w in enumerate(POOL_WINDOWS):
            sg = slab[:, g * LANES:(g + 1) * LANES]
            hi, lo = _split_bf16(sg)
            band = band_ref[g]
            wsum = _dot(band, hi) + _dot(band, lo)
            cnt = (jnp.minimum(tpos - w // 2 + w, seq_len) - jnp.maximum(tpos - w // 2, 0)).astype(F32)
            dlt = wsum / cnt - sg[POOL_OFF:POOL_OFF + QB, :]
            y = _dot(dlt.astype(BF16), poolw_ref[g]) * pscale_ref[:, g * LANES:(g + 1) * LANES]
            mix[pl.ds(r0, QB), ATTN_WIDTH + g * LANES:ATTN_WIDTH + (g + 1) * LANES] = y.astype(BF16)
        return carry

    lax.fori_loop(0, TQ // QB, sub_block, 0)

    proj = _dot(mix[...], wout_ref[...])
    x1 = x_ref[0] + mod_ref[0, 2:3, :] * proj
    x1_ref[0] = x1
    h2 = _norm_mod(x1, g2_ref[...], mod_ref[0, 3:4, :], mod_ref[0, 4:5, :])
    h2_ref[...] = _pack_words(h2)
    h_hi, h_lo = _split_bf16(h2)
    wrh = wrh_ref[...]
    lg_ref[...] = _dot_nt(wrh, h_hi) + _dot_nt(wrh, h_lo) + _dot_nt(wrl_ref[...], h_hi)


def _attn(b0, b, sink, q, k4, v4, kc4, vc4, p, x, mod, band, poolw_bf, pscale, wout_bf, g2, wr_hi, wr_lo):
    _, s, d = x.shape
    c = kc4.shape[1]
    nt = s // TQ
    hb = TQ // QB
    pb = TQ // 8
    kv_main = pl.BlockSpec((1, TQ, 4 * LANES), lambda bi, n: (b0 + bi, n, 0))
    kv_prev = pl.BlockSpec((1, QB, 4 * LANES), lambda bi, n: (b0 + bi, jnp.maximum(n * hb - 1, 0), 0))
    kv_next = pl.BlockSpec((1, QB, 4 * LANES), lambda bi, n: (b0 + bi, jnp.minimum((n + 1) * hb, s // QB - 1), 0))
    const2 = lambda bi, n: (0, 0)
    const3 = lambda bi, n: (0, 0, 0)
    return pl.pallas_call(
        functools.partial(_attn_kernel, s),
        out_shape=(jax.ShapeDtypeStruct((b, s, d), F32),
                   jax.ShapeDtypeStruct((b * s, d // 2), U32),
                   jax.ShapeDtypeStruct((N_EXPERTS, b * s), F32)),
        grid=(b, nt),
        in_specs=[pl.BlockSpec(memory_space=pltpu.SMEM),
                  pl.BlockSpec((1, TQ, ATTN_WIDTH), lambda bi, n: (b0 + bi, n, 0)),
                  kv_main, kv_prev, kv_next, kv_main, kv_prev, kv_next,
                  pl.BlockSpec((1, c, 4 * LANES), lambda bi, n: (b0 + bi, 0, 0)),
                  pl.BlockSpec((1, c, 4 * LANES), lambda bi, n: (b0 + bi, 0, 0)),
                  pl.BlockSpec((1, TQ, POOL_WIDTH), lambda bi, n: (b0 + bi, n, 0)),
                  pl.BlockSpec((1, 8, POOL_WIDTH), lambda bi, n: (b0 + bi, jnp.maximum(n * pb - 1, 0), 0)),
                  pl.BlockSpec((1, 8, POOL_WIDTH),
                               lambda bi, n: (b0 + bi, jnp.minimum((n + 1) * pb, s // 8 - 1), 0)),
                  pl.BlockSpec((1, TQ, d), lambda bi, n: (b0 + bi, n, 0)),
                  pl.BlockSpec((1, 6, d), lambda bi, n: (b0 + bi, 0, 0)),
                  pl.BlockSpec((len(POOL_WINDOWS), QB, POOL_SLAB), const3),
                  pl.BlockSpec((len(POOL_WINDOWS), POOL_GROUP_DIM, POOL_GROUP_DIM), const3),
                  pl.BlockSpec((1, POOL_WIDTH), const2),
                  pl.BlockSpec((d, d), const2),
                  pl.BlockSpec((1, d), const2),
                  pl.BlockSpec((N_EXPERTS, d), const2),
                  pl.BlockSpec((N_EXPERTS, d), const2)],
        out_specs=(pl.BlockSpec((1, TQ, d), lambda bi, n: (bi, n, 0)),
                   pl.BlockSpec((TQ, d // 2), lambda bi, n: (bi * nt + n, 0)),
                   pl.BlockSpec((N_EXPERTS, TQ), lambda bi, n: (0, bi * nt + n))),
        scratch_shapes=[pltpu.VMEM((TQ + 2 * QB, 4 * LANES), BF16),
                        pltpu.VMEM((TQ + 2 * QB, 4 * LANES), BF16),
                        pltpu.VMEM((TQ + 2 * QB, POOL_WIDTH), F32),
                        pltpu.VMEM((TQ, d), BF16),
                        pltpu.VMEM((N_HEADS // 2, QB, 2 * (3 * QB + c)), F32),
                        pltpu.VMEM((N_HEADS // 2, QB, 2 * (3 * QB + c)), BF16),
                        pltpu.VMEM((N_HEADS, QB, LANES), F32)],
        compiler_params=pltpu.CompilerParams(vmem_limit_bytes=VMEM_LIMIT),
        name="attn",
    )(sink, q, k4, k4, k4, v4, v4, v4, kc4, vc4, p, p, p, x, mod, band, poolw_bf, pscale, wout_bf,
      g2, wr_hi, wr_lo)


def _first_argmax_rows(v, row_iota, n_rows):
    m = jnp.max(v, axis=0, keepdims=True)
    idx = jnp.min(jnp.where(v == m, row_iota, n_rows), axis=0, keepdims=True)
    return m, idx


def _route_kernel(lg_ref, bias_ref, tri_ref, idx_ref, gate_ref, rank_ref, cnt_ref, carry):
    i = pl.program_id(0)

    @pl.when(i == 0)
    def _():
        carry[...] = jnp.zeros_like(carry)

    scores = 1.0 / (1.0 + jnp.exp(-lg_ref[...]))
    biased = scores + bias_ref[...]
    e_iota = lax.broadcasted_iota(jnp.int32, scores.shape, 0).astype(F32)
    g_iota = lax.broadcasted_iota(jnp.int32, (EXPERTS_PER_GROUP, TR), 0).astype(F32)
    neg = -jnp.inf

    grp = []
    for g in range(N_EXPERT_GROUPS):
        blk = biased[g * EXPERTS_PER_GROUP:(g + 1) * EXPERTS_PER_GROUP, :]
        m1, i1 = _first_argmax_rows(blk, g_iota, float(EXPERTS_PER_GROUP))
        m2 = jnp.max(jnp.where(g_iota == i1, neg, blk), axis=0, keepdims=True)
        grp.append(m1 + m2)
    grp = jnp.concatenate(grp, axis=0)
    gg_iota = lax.broadcasted_iota(jnp.int32, grp.shape, 0).astype(F32)
    grp_sel = jnp.zeros(grp.shape, F32)
    for _ in range(TOPK_GROUPS):
        _, gi = _first_argmax_rows(grp, gg_iota, float(N_EXPERT_GROUPS))
        hit = gg_iota == gi
        grp_sel = jnp.where(hit, 1.0, grp_sel)
        grp = jnp.where(hit, neg, grp)
    allowed = jnp.concatenate(
        [jnp.broadcast_to(grp_sel[g:g + 1, :], (EXPERTS_PER_GROUP, TR)) for g in range(N_EXPERT_GROUPS)], axis=0)
    masked = jnp.where(allowed > 0.5, biased, neg)

    idxs, gates = [], []
    onehot = jnp.zeros(scores.shape, F32)
    for _ in range(TOP_K):
        _, ei = _first_argmax_rows(masked, e_iota, float(N_EXPERTS))
        hit = e_iota == ei
        idxs.append(ei)
        gates.append(jnp.sum(jnp.where(hit, scores, 0.0), axis=0, keepdims=True))
        onehot = jnp.where(hit, 1.0, onehot)
        masked = jnp.where(hit, neg, masked)
    idx = jnp.concatenate(idxs, axis=0)
    gate = jnp.concatenate(gates, axis=0)
    gate = gate / jnp.sum(gate, axis=0, keepdims=True) * ROUTED_SCALE

    before = _dot(onehot.astype(BF16), tri_ref[...]) + carry[:, 0:1]
    ranks = [jnp.sum(jnp.where(e_iota == idxs[k], before, 0.0), axis=0, keepdims=True) for k in range(TOP_K)]
    idx_ref[...] = idx.astype(jnp.int32)
    gate_ref[...] = gate
    rank_ref[...] = jnp.concatenate(ranks, axis=0).astype(jnp.int32)
    total = carry[...] + jnp.sum(onehot, axis=1, keepdims=True)
    carry[...] = total
    cnt_ref[...] = total


def _route(lg_t, bias, tri):
    e, t = lg_t.shape
    tok = pl.BlockSpec((TOP_K, TR), lambda i: (0, i))
    return pl.pallas_call(
        _route_kernel,
        out_shape=(jax.ShapeDtypeStruct((TOP_K, t), jnp.int32),
                   jax.ShapeDtypeStruct((TOP_K, t), F32),
                   jax.ShapeDtypeStruct((TOP_K, t), jnp.int32),
                   jax.ShapeDtypeStruct((e, LANES), F32)),
        grid=(t // TR,),
        in_specs=[pl.BlockSpec((e, TR), lambda i: (0, i)),
                  pl.BlockSpec((e, 1), lambda i: (0, 0)),
                  pl.BlockSpec((TR, TR), lambda i: (0, 0))],
        out_specs=(tok, tok, tok, pl.BlockSpec((e, LANES), lambda i: (0, 0))),
        scratch_shapes=[pltpu.VMEM((e, LANES), F32)],
        compiler_params=pltpu.CompilerParams(vmem_limit_bytes=VMEM_LIMIT),
        name="route",
    )(lg_t, bias, tri)


def _plan_kernel(n_blocks, size_ref, start_ref, expert_ref, valid_ref, nact_ref):
    def per_expert(e, first_block):
        size = size_ref[e]
        n_blk = (size + TM_EXP - 1) // TM_EXP
        start_ref[e] = first_block * TM_EXP

        def per_block(j, carry):
            expert_ref[first_block + j] = e
            valid_ref[first_block + j] = jnp.minimum(size - j * TM_EXP, TM_EXP)
            return carry

        lax.fori_loop(0, n_blk, per_block, 0)
        return first_block + n_blk

    n_active = lax.fori_loop(0, N_EXPERTS, per_expert, 0)
    nact_ref[0] = n_active

    def unused(i, carry):
        expert_ref[i] = N_EXPERTS - 1
        valid_ref[i] = 0
        return carry

    lax.fori_loop(n_active, n_blocks, unused, 0)


def _plan_blocks(sizes, n_blocks):
    smem = pl.BlockSpec(memory_space=pltpu.SMEM)
    return pl.pallas_call(
        functools.partial(_plan_kernel, n_blocks),
        out_shape=(jax.ShapeDtypeStruct((N_EXPERTS,), jnp.int32),
                   jax.ShapeDtypeStruct((n_blocks,), jnp.int32),
                   jax.ShapeDtypeStruct((n_blocks,), jnp.int32),
                   jax.ShapeDtypeStruct((1,), jnp.int32)),
        in_specs=[smem],
        out_specs=(smem, smem, smem, smem),
        name="plan_blocks",
    )(sizes)


def _dest_kernel(start_ref, idx_ref, rank_ref, dest_ref):
    idx = idx_ref[...]
    dest = rank_ref[...]
    for e in range(N_EXPERTS):
        dest = dest + jnp.where(idx == e, start_ref[e], 0)
    dest_ref[...] = dest


def _dest_rows(pad_start, idx_kt, rank_kt):
    n_k, t = idx_kt.shape
    tile = 4096
    blk = pl.BlockSpec((n_k, tile), lambda i: (0, i))
    return pl.pallas_call(
        _dest_kernel,
        out_shape=jax.ShapeDtypeStruct((n_k, t), jnp.int32),
        grid=(t // tile,),
        in_specs=[pl.BlockSpec(memory_space=pltpu.SMEM), blk, blk],
        out_specs=blk,
        name="dest_rows",
    )(pad_start, idx_kt, rank_kt)


def _sc_mesh():
    return plsc.VectorSubcoreMesh(core_axis_name="c", subcore_axis_name="s")


def _sc_token_base(steps, j):
    worker = lax.axis_index("s") * SC_CORES + lax.axis_index("c")
    return (worker * steps + j) * SC_TOKENS


def _sc_scatter(dest_kt, h2p, n_rows):
    t, width = h2p.shape
    steps = t // (SC_WORKERS * SC_TOKENS)

    @functools.partial(
        pl.kernel, mesh=_sc_mesh(),
        out_type=jax.ShapeDtypeStruct((n_rows, width), U32),
        scratch_types=[pltpu.VMEM((TOP_K, SC_TOKENS), jnp.int32),
                       pltpu.VMEM((SC_TOKENS, width), U32),
                       pltpu.SemaphoreType.DMA],
        name="sc_scatter",
    )
    def body(dest_hbm, h_hbm, xs_hbm, idx_v, rows_v, sem):
        @pl.loop(0, steps)
        def _(j):
            base = _sc_token_base(steps, j)
            pltpu.sync_copy(dest_hbm.at[:, pl.ds(base, SC_TOKENS)], idx_v)
            pltpu.sync_copy(h_hbm.at[pl.ds(base, SC_TOKENS)], rows_v)
            copies = [pltpu.async_copy(rows_v, xs_hbm.at[idx_v.at[k]], sem) for k in range(TOP_K)]
            for cp in copies:
                cp.wait()

    return body(dest_kt, h2p)


def _sc_gather(dest_kt, ys, token0, n_tokens):
    n_k = dest_kt.shape[0]
    width = ys.shape[1]
    steps = n_tokens // (SC_WORKERS * SC_TOKENS)

    half = SC_TOKENS // 2
    units = [(k, h) for k in range(n_k) for h in range(2)]
    n_buf = 2

    @functools.partial(
        pl.kernel, mesh=_sc_mesh(),
        out_type=jax.ShapeDtypeStruct((n_k, n_tokens, width), U32),
        scratch_types=[pltpu.VMEM((n_k, SC_TOKENS), jnp.int32),
                       pltpu.VMEM((n_buf, half, width), U32),
                       pltpu.SemaphoreType.DMA((n_buf,))],
        name="sc_gather",
    )
    def body(dest_hbm, ys_hbm, yk_hbm, idx_v, rows_v, sems):
        @pl.loop(0, steps)
        def _(j):
            base = _sc_token_base(steps, j)
            pltpu.sync_copy(dest_hbm.at[:, pl.ds(token0 + base, SC_TOKENS)], idx_v)

            def gather(u):
                k, h = units[u]
                slot = u % n_buf
                return pltpu.make_async_copy(ys_hbm.at[idx_v.at[k, pl.ds(h * half, half)]], rows_v.at[slot],
                                             sems.at[slot])

            ahead = n_buf - 1
            for u in range(ahead):
                gather(u).start()
            for u, (k, h) in enumerate(units):
                gather(u).wait()
                pltpu.sync_copy(rows_v.at[u % n_buf], yk_hbm.at[k, pl.ds(base + h * half, half)])
                if u + ahead < len(units):
                    gather(u + ahead).start()

    return body(dest_kt, ys)


def _experts_kernel(be_ref, valid_ref, nact_ref, xs_hbm, wg_ref, wu_ref, wd_ref, ys_ref, wg_bf, wu_bf, wd_bf,
                    xbuf, sems):
    i = pl.program_id(0)
    n_active = nact_ref[0]
    slot = lax.rem(i, EXP_SLOTS)

    def fetch(j):
        src = xs_hbm.at[pl.ds(pl.multiple_of(j * TM_EXP, TM_EXP), TM_EXP), :]
        s = lax.rem(j, EXP_SLOTS)
        return pltpu.make_async_copy(src, xbuf.at[s], sems.at[s])

    @pl.when(i == 0)
    def _():
        for j in range(EXP_SLOTS - 1):
            pl.when(j < n_active)(lambda j=j: fetch(j).start())

    @pl.when(i + (EXP_SLOTS - 1) < n_active)
    def _():
        fetch(i + (EXP_SLOTS - 1)).start()

    prev = be_ref[jnp.maximum(i - 1, 0)]

    @pl.when((i == 0) | (be_ref[i] != prev))
    def _():
        wg_bf[...] = wg_ref[0].astype(BF16)
        wu_bf[...] = wu_ref[0].astype(BF16)
        wd_bf[...] = wd_ref[0].astype(BF16)

    @pl.when(i < n_active)
    def _():
        fetch(i).wait()

    valid = jnp.where(i < n_active, valid_ref[i], 0)

    def run(rows):
        row = lax.broadcasted_iota(jnp.int32, (rows, xbuf.shape[2]), 0)
        words = jnp.where(row < valid, xbuf[slot, 0:rows, :], jnp.uint32(0))
        xb = _unpack_words(words).astype(BF16)
        hid = _silu(_dot(xb, wg_bf[...])) * _dot(xb, wu_bf[...])
        ys_ref[0:rows, :] = _pack_words(_dot(hid.astype(BF16), wd_bf[...]))
        if rows < TM_EXP:
            ys_ref[rows:, :] = jnp.zeros((TM_EXP - rows, ys_ref.shape[1]), U32)

    for rows in range(EXP_MIN_ROWS, TM_EXP + 1, EXP_MIN_ROWS):
        pl.when((valid > rows - EXP_MIN_ROWS) & (valid <= rows))(functools.partial(run, rows))

    @pl.when(valid == 0)
    def _():
        ys_ref[...] = jnp.zeros_like(ys_ref)


def _experts(block_e, block_valid, n_active, xs, w_gate, w_up, w_down):
    n_rows, half = xs.shape
    d = w_gate.shape[1]
    nb = n_rows // TM_EXP
    grid_spec = pltpu.PrefetchScalarGridSpec(
        num_scalar_prefetch=3,
        grid=(nb,),
        in_specs=[pl.BlockSpec(memory_space=pl.ANY),
                  pl.BlockSpec((1, d, D_EXPERT), lambda i, be, bv, na: (be[i], 0, 0)),
                  pl.BlockSpec((1, d, D_EXPERT), lambda i, be, bv, na: (be[i], 0, 0)),
                  pl.BlockSpec((1, D_EXPERT, d), lambda i, be, bv, na: (be[i], 0, 0))],
        out_specs=pl.BlockSpec((TM_EXP, half), lambda i, be, bv, na: (i, 0)),
        scratch_shapes=[pltpu.VMEM((d, D_EXPERT), BF16), pltpu.VMEM((d, D_EXPERT), BF16),
                        pltpu.VMEM((D_EXPERT, d), BF16),
                        pltpu.VMEM((EXP_SLOTS, TM_EXP, half), U32),
                        pltpu.SemaphoreType.DMA((EXP_SLOTS,))],
    )
    return pl.pallas_call(
        _experts_kernel,
        out_shape=jax.ShapeDtypeStruct((n_rows, half), U32),
        grid_spec=grid_spec,
        compiler_params=pltpu.CompilerParams(vmem_limit_bytes=VMEM_LIMIT, dimension_semantics=("arbitrary",)),
        name="experts",
    )(block_e, block_valid, n_active, xs, w_gate, w_up, w_down)


def _combine_kernel(yk_ref, x1_ref, h2_ref, gate_ref, mod_ref, wsg_ref, wsu_ref, wsd_ref, fg_ref, *out_refs):
    out_ref = out_refs[-1]
    hb = _unpack_words(h2_ref[...]).astype(BF16)
    hid = _silu(_dot(hb, wsg_ref[...])) * _dot(hb, wsu_ref[...])
    ffn = _dot(hid.astype(BF16), wsd_ref[...])
    gate = gate_ref[...]
    for k in range(TOP_K):
        ffn = ffn + gate[:, k:k + 1] * _unpack_words(yk_ref[k])
    x2 = x1_ref[...] + mod_ref[0, 5:6, :] * ffn
    ms = jnp.mean(x2 * x2, axis=-1, keepdims=True)
    out_ref[...] = x2 * lax.rsqrt(ms + EPS) * fg_ref[...]


def _combine(yk, token0, out_token0, n_out_tokens, prev_out, x1, h2p, gate_tk, mod, wsg_bf, wsu_bf, wsd_bf, final_g,
             seq_len):
    t, d = n_out_tokens, x1.shape[1]
    tiles_per_seq = seq_len // TF
    tile0 = token0 // TF
    out_tile0 = out_token0 // TF
    in_tok = pl.BlockSpec((TF, d), lambda i: (tile0 + i, 0))
    tok = pl.BlockSpec((TF, d), lambda i: (out_tile0 + i, 0))
    in_specs = [pl.BlockSpec((TOP_K, TF, d // 2), lambda i: (0, i, 0)),
                in_tok, pl.BlockSpec((TF, d // 2), lambda i: (tile0 + i, 0)),
                pl.BlockSpec((TF, TOP_K), lambda i: (tile0 + i, 0)),
                pl.BlockSpec((1, 6, d), lambda i: ((out_tile0 + i) // tiles_per_seq, 0, 0)),
                pl.BlockSpec((d, D_SHARED), lambda i: (0, 0)),
                pl.BlockSpec((d, D_SHARED), lambda i: (0, 0)),
                pl.BlockSpec((D_SHARED, d), lambda i: (0, 0)),
                pl.BlockSpec((1, d), lambda i: (0, 0))]
    args = [yk, x1, h2p, gate_tk, mod, wsg_bf, wsu_bf, wsd_bf, final_g]
    aliases = {}
    if prev_out is not None:
        in_specs.append(pl.BlockSpec(memory_space=pl.ANY))
        args.append(prev_out)
        aliases = {len(args) - 1: 0}
    return pl.pallas_call(
        _combine_kernel,
        out_shape=jax.ShapeDtypeStruct((t, d), F32),
        grid=(yk.shape[1] // TF,),
        in_specs=in_specs,
        out_specs=tok,
        input_output_aliases=aliases,
        compiler_params=pltpu.CompilerParams(vmem_limit_bytes=VMEM_LIMIT),
        name="combine",
    )(*args)


def _rope_tables(n_tokens):
    n_rows = n_tokens // GRID_W
    n_freq = HEAD_DIM // 4
    inv_freq = ROPE_THETA ** (-jnp.arange(n_freq, dtype=F32) / n_freq)
    ang_r = jnp.arange(n_rows).astype(F32)[:, None] * inv_freq[None, :]
    ang_c = jnp.arange(GRID_W).astype(F32)[:, None] * inv_freq[None, :]

    def per_token(row_part, col_part):
        rows = jnp.broadcast_to(row_part[:, None, :], (n_rows, GRID_W, n_freq))
        cols = jnp.broadcast_to(col_part[None, :, :], (n_rows, GRID_W, n_freq))
        return rows.reshape(n_tokens, n_freq), cols.reshape(n_tokens, n_freq)

    cos_r, cos_c = per_token(jnp.cos(ang_r), jnp.cos(ang_c))
    sin_r, sin_c = per_token(jnp.sin(ang_r), jnp.sin(ang_c))
    cos = jnp.concatenate([cos_r, cos_r, cos_c, cos_c], axis=1)
    sin = jnp.concatenate([-sin_r, sin_r, -sin_c, sin_c], axis=1)
    reps = LANES // HEAD_DIM
    return jnp.tile(cos, (1, reps)), jnp.tile(sin, (1, reps))


def _pool_bands():
    i = jnp.arange(QB)[:, None]
    r = jnp.arange(POOL_SLAB)[None, :]
    return jnp.stack([((r >= i + POOL_OFF - w // 2) & (r < i + POOL_OFF + w // 2)).astype(BF16)
                      for w in POOL_WINDOWS])


def kernel(x, c, ctx, c_ctx, w_ada, b_ada, norm1_g, norm2_g, w_in, attn_sink, pool_w, pool_scale, w_out,
           w_router, router_bias, w_gate, w_up, w_down, ws_gate, ws_up, ws_down, final_g):
    b, s, d = x.shape
    t = b * s
    assert w_ada.shape[0] == 1 and d == D_MODEL and s % TQ == 0 and b + 1 <= 8

    c8 = jnp.zeros((8, d), F32).at[:b].set(c).at[b].set(c_ctx)
    mod = _ada(c8, w_ada[0], b_ada[0]).reshape(8, 6, d)
    g1 = norm1_g[0].reshape(1, d)
    g2 = norm2_g[0].reshape(1, d)
    w_in_bf = w_in[0].astype(BF16)
    cos_t, sin_t = _rope_tables(s)

    q, k4, v4, p = _inproj(x, mod, g1, w_in_bf, cos_t, sin_t)
    kc4, vc4 = _ctxproj(ctx, mod[b:b + 1], g1, w_in_bf[:, ATTN_WIDTH:ATTN_WIDTH + 2 * KV_WIDTH])

    wr_t = w_router[0].T
    wr_hi = wr_t.astype(BF16)
    wr_lo = (wr_t - wr_hi.astype(F32)).astype(BF16)
    attn_consts = (_pool_bands(), pool_w[0].astype(BF16), pool_scale[0].reshape(1, POOL_WIDTH),
                   w_out[0].astype(BF16), g2, wr_hi, wr_lo)
    tri = jnp.triu(jnp.ones((TR, TR), BF16), k=1)
    shared_w = (ws_gate[0].astype(BF16), ws_up[0].astype(BF16), ws_down[0].astype(BF16))

    nb = b // TOKEN_GROUPS
    tg = nb * s
    assert b % TOKEN_GROUPS == 0 and tg % (COMBINE_CHUNKS * SC_WORKERS * SC_TOKENS) == 0
    n_rows = -(-(tg * TOP_K + N_EXPERTS * (TM_EXP - 1)) // TM_EXP) * TM_EXP
    groups = []
    for g in range(TOKEN_GROUPS):
        x1, h2p, lg_t = _attn(g * nb, nb, attn_sink[0], q, k4, v4, kc4, vc4, p, x, mod, *attn_consts)
        idx_kt, gate_kt, rank_kt, counts = _route(lg_t, router_bias[0].reshape(N_EXPERTS, 1), tri)
        pad_start, block_e, block_valid, n_active = _plan_blocks(counts[:, 0].astype(jnp.int32), n_rows // TM_EXP)
        dest_kt = _dest_rows(pad_start, idx_kt, rank_kt)
        xs = _sc_scatter(dest_kt, h2p, n_rows)
        groups.append((x1.reshape(tg, d), h2p, gate_kt.T, dest_kt, xs, block_e, block_valid, n_active))

    out = None
    chunk = tg // COMBINE_CHUNKS
    for g, (x1, h2p, gate_tk, dest_kt, xs, block_e, block_valid, n_active) in enumerate(groups):
        ys = _experts(block_e, block_valid, n_active, xs, w_gate[0], w_up[0], w_down[0])
        for token0 in range(0, tg, chunk):
            out = _combine(_sc_gather(dest_kt, ys, token0, chunk), token0, g * tg + token0, t, out, x1, h2p,
                           gate_tk, mod, *shared_w, final_g.reshape(1, d), s)
    return out.reshape(b, s, d)
```

```python
import functools

import jax
import jax.numpy as jnp
from jax import lax
from jax.experimental import pallas as pl
from jax.experimental.pallas import tpu as pltpu
from jax.experimental.pallas import tpu_sc as plsc

F32 = jnp.float32
BF16 = jnp.bfloat16

D_MODEL = 1024
GRID_W = 64
N_HEADS = 8
N_KV_HEADS = 2
HEAD_DIM = 64
ATTN_WIDTH = N_HEADS * HEAD_DIM
KV_WIDTH = N_KV_HEADS * HEAD_DIM
WINDOW = 128
ROPE_THETA = 10000.0
POOL_WINDOWS = (2, 4, 8, 16)
POOL_WIDTH = D_MODEL - ATTN_WIDTH
POOL_GROUP_DIM = POOL_WIDTH // len(POOL_WINDOWS)
IN_COLS = ATTN_WIDTH + 2 * KV_WIDTH + POOL_WIDTH
N_EXPERTS = 64
TOP_K = 8
N_EXPERT_GROUPS = 8
EXPERTS_PER_GROUP = N_EXPERTS // N_EXPERT_GROUPS
TOPK_GROUPS = 4
D_EXPERT = 256
D_SHARED = 256
ROUTED_SCALE = 2.5
EPS = 1e-6
LOG2E = 1.4426950408889634

LANES = 128
U32 = jnp.uint32
VMEM_LIMIT = 48 * 1024 * 1024

TM_PROJ = 1024
TQ = 1024
QB = 128
POOL_SLAB = 256
POOL_OFF = 64
TR = 512
TM_EXP = 1024
EXP_MIN_ROWS = 256
EXP_SLOTS = 4
TF = 512
TOKEN_GROUPS = 2
COMBINE_CHUNKS = 2
SC_CORES = 2
SC_WORKERS = 32
SC_TOKENS = 128


def _silu(x):
    return x * (1.0 / (1.0 + jnp.exp(-x)))


def _split_bf16(x):
    hi = x.astype(BF16)
    lo = (x - hi.astype(F32)).astype(BF16)
    return hi, lo


def _dot(a, b):
    return jnp.dot(a, b, preferred_element_type=F32)


def _pack_words(val):
    half = val.shape[1] // 2
    lo = lax.bitcast_convert_type(val[:, :half].astype(BF16).astype(F32), U32)
    hi = lax.bitcast_convert_type(val[:, half:].astype(BF16).astype(F32), U32)
    return lax.shift_right_logical(lo, jnp.uint32(16)) | hi


def _unpack_words(words):
    lo = lax.bitcast_convert_type(lax.shift_left(words, jnp.uint32(16)), F32)
    hi = lax.bitcast_convert_type(words & jnp.uint32(0xFFFF0000), F32)
    return jnp.concatenate([lo, hi], axis=1)


def _dot_nt(a, b):
    return lax.dot_general(a, b, (((1,), (1,)), ((), ())), preferred_element_type=F32)


def _ada_kernel(c_ref, w_ref, b_ref, o_ref):
    a_hi, a_lo = _split_bf16(_silu(c_ref[...]))
    w_hi, w_lo = _split_bf16(w_ref[...])
    o_ref[...] = _dot(a_hi, w_hi) + _dot(a_lo, w_hi) + _dot(a_hi, w_lo) + b_ref[...]


def _ada(c8, w_ada, b_ada):
    d = c8.shape[1]
    n = w_ada.shape[1]
    tn = 512
    return pl.pallas_call(
        _ada_kernel,
        out_shape=jax.ShapeDtypeStruct((8, n), F32),
        grid=(n // tn,),
        in_specs=[pl.BlockSpec((8, d), lambda j: (0, 0)),
                  pl.BlockSpec((d, tn), lambda j: (0, j)),
                  pl.BlockSpec((1, tn), lambda j: (0, j))],
        out_specs=pl.BlockSpec((8, tn), lambda j: (0, j)),
        compiler_params=pltpu.CompilerParams(vmem_limit_bytes=VMEM_LIMIT),
        name="ada",
    )(c8, w_ada, b_ada.reshape(1, n))


def _norm_mod(x, g, shift, scale):
    ms = jnp.mean(x * x, axis=-1, keepdims=True)
    return (x * lax.rsqrt(ms + EPS) * g) * (1.0 + scale) + shift


def _lane_variants(t):
    lane = lax.broadcasted_iota(jnp.int32, t.shape, 1)
    lo = lane < HEAD_DIM
    tr = pltpu.roll(t, HEAD_DIM, 1)
    zero = jnp.zeros_like(t)
    return (jnp.where(lo, t, zero), jnp.where(lo, zero, tr),
            jnp.where(lo, tr, zero), jnp.where(lo, zero, t))


def _store_variants(ref, t):
    for i, var in enumerate(_lane_variants(t)):
        ref[0, :, i * LANES:(i + 1) * LANES] = var.astype(BF16)


def _inproj_kernel(x_ref, mod_ref, g_ref, w_ref, cos_ref, sin_ref, q_ref, k_ref, v_ref, p_ref):
    h = _norm_mod(x_ref[0], g_ref[...], mod_ref[0, 0:1, :], mod_ref[0, 1:2, :])
    z = _dot(h.astype(BF16), w_ref[...])
    cos = cos_ref[...]
    sin = sin_ref[...]
    lane = lax.broadcasted_iota(jnp.int32, cos.shape, 1)
    first_half = (lane & 16) == 0

    def rope(zc):
        partner = jnp.where(first_half, pltpu.roll(zc, LANES - 16, 1), pltpu.roll(zc, 16, 1))
        return zc * cos + partner * sin

    scale = HEAD_DIM ** -0.5 * LOG2E
    for c in range(ATTN_WIDTH // LANES):
        q_ref[0, :, c * LANES:(c + 1) * LANES] = (rope(z[:, c * LANES:(c + 1) * LANES]) * scale).astype(BF16)
    _store_variants(k_ref, rope(z[:, ATTN_WIDTH:ATTN_WIDTH + KV_WIDTH]))
    _store_variants(v_ref, z[:, ATTN_WIDTH + KV_WIDTH:ATTN_WIDTH + 2 * KV_WIDTH])
    p_ref[0] = z[:, ATTN_WIDTH + 2 * KV_WIDTH:]


def _inproj(x, mod, g1, w_in_bf, cos_t, sin_t):
    b, s, d = x.shape
    tm = TM_PROJ
    return pl.pallas_call(
        _inproj_kernel,
        out_shape=(jax.ShapeDtypeStruct((b, s, ATTN_WIDTH), BF16),
                   jax.ShapeDtypeStruct((b, s, 4 * LANES), BF16),
                   jax.ShapeDtypeStruct((b, s, 4 * LANES), BF16),
                   jax.ShapeDtypeStruct((b, s, POOL_WIDTH), F32)),
        grid=(s // tm, b),
        in_specs=[pl.BlockSpec((1, tm, d), lambda n, bi: (bi, n, 0)),
                  pl.BlockSpec((1, 6, d), lambda n, bi: (bi, 0, 0)),
                  pl.BlockSpec((1, d), lambda n, bi: (0, 0)),
                  pl.BlockSpec((d, IN_COLS), lambda n, bi: (0, 0)),
                  pl.BlockSpec((tm, LANES), lambda n, bi: (n, 0)),
                  pl.BlockSpec((tm, LANES), lambda n, bi: (n, 0))],
        out_specs=(pl.BlockSpec((1, tm, ATTN_WIDTH), lambda n, bi: (bi, n, 0)),
                   pl.BlockSpec((1, tm, 4 * LANES), lambda n, bi: (bi, n, 0)),
                   pl.BlockSpec((1, tm, 4 * LANES), lambda n, bi: (bi, n, 0)),
                   pl.BlockSpec((1, tm, POOL_WIDTH), lambda n, bi: (bi, n, 0))),
        compiler_params=pltpu.CompilerParams(vmem_limit_bytes=VMEM_LIMIT),
        name="inproj",
    )(x, mod, g1, w_in_bf, cos_t, sin_t)


def _ctxproj_kernel(x_ref, mod_ref, g_ref, w_ref, k_ref, v_ref):
    h = _norm_mod(x_ref[0], g_ref[...], mod_ref[0, 0:1, :], mod_ref[0, 1:2, :])
    z = _dot(h.astype(BF16), w_ref[...])
    _store_variants(k_ref, z[:, :KV_WIDTH])
    _store_variants(v_ref, z[:, KV_WIDTH:])


def _ctxproj(ctx, mod_c, g1, w_kv_bf):
    b, c, d = ctx.shape
    return pl.pallas_call(
        _ctxproj_kernel,
        out_shape=(jax.ShapeDtypeStruct((b, c, 4 * LANES), BF16),
                   jax.ShapeDtypeStruct((b, c, 4 * LANES), BF16)),
        grid=(b,),
        in_specs=[pl.BlockSpec((1, c, d), lambda bi: (bi, 0, 0)),
                  pl.BlockSpec((1, 6, d), lambda bi: (0, 0, 0)),
                  pl.BlockSpec((1, d), lambda bi: (0, 0)),
                  pl.BlockSpec((d, 2 * KV_WIDTH), lambda bi: (0, 0))],
        out_specs=(pl.BlockSpec((1, c, 4 * LANES), lambda bi: (bi, 0, 0)),
                   pl.BlockSpec((1, c, 4 * LANES), lambda bi: (bi, 0, 0))),
        compiler_params=pltpu.CompilerParams(vmem_limit_bytes=VMEM_LIMIT),
        name="ctxproj",
    )(ctx, mod_c, g1, w_kv_bf)


def _fold(op, tiles):
    while len(tiles) > 1:
        tiles = [op(tiles[i], tiles[i + 1]) if i + 1 < len(tiles) else tiles[i] for i in range(0, len(tiles), 2)]
    return tiles[0]


def _stack_variants(t4, kv):
    return jnp.concatenate([t4[:, (2 * kv) * LANES:(2 * kv + 1) * LANES],
                            t4[:, (2 * kv + 1) * LANES:(2 * kv + 2) * LANES]], axis=0)


def _attn_kernel(seq_len, sink_ref, q_ref, k_ref, kp_ref, kn_ref, v_ref, vp_ref, vn_ref, kc_ref, vc_ref,
                 p_ref, pp_ref, pn_ref, x_ref, mod_ref, band_ref, poolw_ref, pscale_ref, wout_ref,
                 g2_ref, wrh_ref, wrl_ref, x1_ref, h2_ref, lg_ref, kwin, vwin, pext, mix, s_scr, p_scr, m_scr):
    n = pl.program_id(1)
    n_last = pl.num_programs(1) - 1

    kwin[0:QB, :] = kp_ref[0]
    kwin[QB:QB + TQ, :] = k_ref[0]
    kwin[QB + TQ:, :] = kn_ref[0]
    vwin[0:QB, :] = vp_ref[0]
    vwin[QB:QB + TQ, :] = v_ref[0]
    vwin[QB + TQ:, :] = vn_ref[0]

    pext[0:QB - 8, :] = jnp.zeros((QB - 8, POOL_WIDTH), F32)
    pext[QB - 8:QB, :] = jnp.where(n > 0, pp_ref[0], 0.0)
    pext[QB:QB + TQ, :] = p_ref[0]
    pext[QB + TQ:QB + TQ + 8, :] = jnp.where(n < n_last, pn_ref[0], 0.0)
    pext[QB + TQ + 8:, :] = jnp.zeros((QB - 8, POOL_WIDTH), F32)

    row = lax.broadcasted_iota(jnp.int32, (QB, 3 * QB), 0)
    col = lax.broadcasted_iota(jnp.int32, (QB, 3 * QB), 1)
    in_band = (col >= row) & (col <= row + 2 * WINDOW)
    tok = lax.broadcasted_iota(jnp.int32, (QB, 1), 0)
    kc = kc_ref[0]
    kc_rows = [_stack_variants(kc, kv) for kv in range(N_KV_HEADS)]
    vc_rows = [_stack_variants(vc_ref[0], kv) for kv in range(N_KV_HEADS)]

    def sub_block(j, carry):
        r0 = pl.multiple_of(j * QB, QB)
        qj = q_ref[0, pl.ds(r0, QB), :]
        kw = kwin[pl.ds(r0, 3 * QB), :]
        vw = vwin[pl.ds(r0, 3 * QB), :]
        kpos = col + (n * TQ + j * QB - QB)
        ok = in_band & (kpos >= 0) & (kpos < seq_len)
        bias = jnp.where(ok, 0.0, -jnp.inf)
        n_loc, n_ctx = 3 * QB, kc.shape[0]
        bias2 = jnp.concatenate([bias, bias], axis=1)
        k_rows = [_stack_variants(kw, kv) for kv in range(N_KV_HEADS)]
        v_rows = [_stack_variants(vw, kv) for kv in range(N_KV_HEADS)]
        group = N_HEADS // N_KV_HEADS

        def head_tiles(head):
            loc0 = (head % 2) * n_loc
            ctx0 = 2 * n_loc + (head % 2) * n_ctx
            return ([loc0 + i * LANES for i in range(n_loc // LANES)]
                    + [ctx0 + i * LANES for i in range(n_ctx // LANES)])

        for c in range(N_HEADS // 2):
            qc = qj[:, c * LANES:(c + 1) * LANES]
            s_scr[c, :, 0:2 * n_loc] = _dot_nt(qc, k_rows[2 * c // group]) + bias2
            s_scr[c, :, 2 * n_loc:] = _dot_nt(qc, kc_rows[2 * c // group])
        for head in range(N_HEADS):
            tiles = [s_scr[head // 2, :, st:st + LANES] for st in head_tiles(head)]
            row_max = jnp.max(_fold(jnp.maximum, tiles), axis=1, keepdims=True)
            m_scr[head] = jnp.broadcast_to(jnp.maximum(row_max, sink_ref[head] * LOG2E), (QB, LANES))
        for head in range(N_HEADS):
            m = m_scr[head]
            acc = None
            for st in head_tiles(head):
                p = jnp.exp2(s_scr[head // 2, :, st:st + LANES] - m)
                p_scr[head // 2, :, st:st + LANES] = p.astype(BF16)
                acc = p if acc is None else acc + p
            denom = (jnp.broadcast_to(jnp.sum(acc, axis=1, keepdims=True), (QB, LANES))
                     + jnp.exp2(sink_ref[head] * LOG2E - m))
            m_scr[head] = 1.0 / denom
        lane = lax.broadcasted_iota(jnp.int32, (QB, LANES), 1)
        for c in range(N_HEADS // 2):
            o = (_dot(p_scr[c, :, 0:2 * n_loc], v_rows[2 * c // group])
                 + _dot(p_scr[c, :, 2 * n_loc:], vc_rows[2 * c // group]))
            o = o * jnp.where(lane < HEAD_DIM, m_scr[2 * c], m_scr[2 * c + 1])
            mix[pl.ds(r0, QB), c * LANES:(c + 1) * LANES] = o.astype(BF16)

        slab = pext[pl.ds(pl.multiple_of(r0 + POOL_OFF, 8), POOL_SLAB), :]
        tpos = tok + (n * TQ + j * QB)
        for g, w in enumerate(POOL_WINDOWS):
            sg = slab[:, g * LANES:(g + 1) * LANES]
            hi, lo = _split_bf16(sg)
            band = band_ref[g]
            wsum = _dot(band, hi) + _dot(band, lo)
            cnt = (jnp.minimum(tpos - w // 2 + w, seq_len) - jnp.maximum(tpos - w // 2, 0)).astype(F32)
            dlt = wsum / cnt - sg[POOL_OFF:POOL_OFF + QB, :]
            y = _dot(dlt.astype(BF16), poolw_ref[g]) * pscale_ref[:, g * LANES:(g + 1) * LANES]
            mix[pl.ds(r0, QB), ATTN_WIDTH + g * LANES:ATTN_WIDTH + (g + 1) * LANES] = y.astype(BF16)
        return carry

    lax.fori_loop(0, TQ // QB, sub_block, 0)

    proj = _dot(mix[...], wout_ref[...])
    x1 = x_ref[0] + mod_ref[0, 2:3, :] * proj
    x1_ref[0] = x1
    h2 = _norm_mod(x1, g2_ref[...], mod_ref[0, 3:4, :], mod_ref[0, 4:5, :])
    h2_ref[...] = _pack_words(h2)
    h_hi, h_lo = _split_bf16(h2)
    wrh = wrh_ref[...]
    lg_ref[...] = _dot_nt(wrh, h_hi) + _dot_nt(wrh, h_lo) + _dot_nt(wrl_ref[...], h_hi)


def _attn(b0, b, sink, q, k4, v4, kc4, vc4, p, x, mod, band, poolw_bf, pscale, wout_bf, g2, wr_hi, wr_lo):
    _, s, d = x.shape
    c = kc4.shape[1]
    nt = s // TQ
    hb = TQ // QB
    pb = TQ // 8
    kv_main = pl.BlockSpec((1, TQ, 4 * LANES), lambda bi, n: (b0 + bi, n, 0))
    kv_prev = pl.BlockSpec((1, QB, 4 * LANES), lambda bi, n: (b0 + bi, jnp.maximum(n * hb - 1, 0), 0))
    kv_next = pl.BlockSpec((1, QB, 4 * LANES), lambda bi, n: (b0 + bi, jnp.minimum((n + 1) * hb, s // QB - 1), 0))
    const2 = lambda bi, n: (0, 0)
    const3 = lambda bi, n: (0, 0, 0)
    return pl.pallas_call(
        functools.partial(_attn_kernel, s),
        out_shape=(jax.ShapeDtypeStruct((b, s, d), F32),
                   jax.ShapeDtypeStruct((b * s, d // 2), U32),
                   jax.ShapeDtypeStruct((N_EXPERTS, b * s), F32)),
        grid=(b, nt),
        in_specs=[pl.BlockSpec(memory_space=pltpu.SMEM),
                  pl.BlockSpec((1, TQ, ATTN_WIDTH), lambda bi, n: (b0 + bi, n, 0)),
                  kv_main, kv_prev, kv_next, kv_main, kv_prev, kv_next,
                  pl.BlockSpec((1, c, 4 * LANES), lambda bi, n: (b0 + bi, 0, 0)),
                  pl.BlockSpec((1, c, 4 * LANES), lambda bi, n: (b0 + bi, 0, 0)),
                  pl.BlockSpec((1, TQ, POOL_WIDTH), lambda bi, n: (b0 + bi, n, 0)),
                  pl.BlockSpec((1, 8, POOL_WIDTH), lambda bi, n: (b0 + bi, jnp.maximum(n * pb - 1, 0), 0)),
                  pl.BlockSpec((1, 8, POOL_WIDTH),
                               lambda bi, n: (b0 + bi, jnp.minimum((n + 1) * pb, s // 8 - 1), 0)),
                  pl.BlockSpec((1, TQ, d), lambda bi, n: (b0 + bi, n, 0)),
                  pl.BlockSpec((1, 6, d), lambda bi, n: (b0 + bi, 0, 0)),
                  pl.BlockSpec((len(POOL_WINDOWS), QB, POOL_SLAB), const3),
                  pl.BlockSpec((len(POOL_WINDOWS), POOL_GROUP_DIM, POOL_GROUP_DIM), const3),
                  pl.BlockSpec((1, POOL_WIDTH), const2),
                  pl.BlockSpec((d, d), const2),
                  pl.BlockSpec((1, d), const2),
                  pl.BlockSpec((N_EXPERTS, d), const2),
                  pl.BlockSpec((N_EXPERTS, d), const2)],
        out_specs=(pl.BlockSpec((1, TQ, d), lambda bi, n: (bi, n, 0)),
                   pl.BlockSpec((TQ, d // 2), lambda bi, n: (bi * nt + n, 0)),
                   pl.BlockSpec((N_EXPERTS, TQ), lambda bi, n: (0, bi * nt + n))),
        scratch_shapes=[pltpu.VMEM((TQ + 2 * QB, 4 * LANES), BF16),
                        pltpu.VMEM((TQ + 2 * QB, 4 * LANES), BF16),
                        pltpu.VMEM((TQ + 2 * QB, POOL_WIDTH), F32),
                        pltpu.VMEM((TQ, d), BF16),
                        pltpu.VMEM((N_HEADS // 2, QB, 2 * (3 * QB + c)), F32),
                        pltpu.VMEM((N_HEADS // 2, QB, 2 * (3 * QB + c)), BF16),
                        pltpu.VMEM((N_HEADS, QB, LANES), F32)],
        compiler_params=pltpu.CompilerParams(vmem_limit_bytes=VMEM_LIMIT),
        name="attn",
    )(sink, q, k4, k4, k4, v4, v4, v4, kc4, vc4, p, p, p, x, mod, band, poolw_bf, pscale, wout_bf,
      g2, wr_hi, wr_lo)


def _first_argmax_rows(v, row_iota, n_rows):
    m = jnp.max(v, axis=0, keepdims=True)
    idx = jnp.min(jnp.where(v == m, row_iota, n_rows), axis=0, keepdims=True)
    return m, idx


def _route_kernel(lg_ref, bias_ref, tri_ref, idx_ref, gate_ref, rank_ref, cnt_ref, carry):
    i = pl.program_id(0)

    @pl.when(i == 0)
    def _():
        carry[...] = jnp.zeros_like(carry)

    scores = 1.0 / (1.0 + jnp.exp(-lg_ref[...]))
    biased = scores + bias_ref[...]
    e_iota = lax.broadcasted_iota(jnp.int32, scores.shape, 0).astype(F32)
    g_iota = lax.broadcasted_iota(jnp.int32, (EXPERTS_PER_GROUP, TR), 0).astype(F32)
    neg = -jnp.inf

    grp = []
    for g in range(N_EXPERT_GROUPS):
        blk = biased[g * EXPERTS_PER_GROUP:(g + 1) * EXPERTS_PER_GROUP, :]
        m1, i1 = _first_argmax_rows(blk, g_iota, float(EXPERTS_PER_GROUP))
        m2 = jnp.max(jnp.where(g_iota == i1, neg, blk), axis=0, keepdims=True)
        grp.append(m1 + m2)
    grp = jnp.concatenate(grp, axis=0)
    gg_iota = lax.broadcasted_iota(jnp.int32, grp.shape, 0).astype(F32)
    grp_sel = jnp.zeros(grp.shape, F32)
    for _ in range(TOPK_GROUPS):
        _, gi = _first_argmax_rows(grp, gg_iota, float(N_EXPERT_GROUPS))
        hit = gg_iota == gi
        grp_sel = jnp.where(hit, 1.0, grp_sel)
        grp = jnp.where(hit, neg, grp)
    allowed = jnp.concatenate(
        [jnp.broadcast_to(grp_sel[g:g + 1, :], (EXPERTS_PER_GROUP, TR)) for g in range(N_EXPERT_GROUPS)], axis=0)
    masked = jnp.where(allowed > 0.5, biased, neg)

    idxs, gates = [], []
    onehot = jnp.zeros(scores.shape, F32)
    for _ in range(TOP_K):
        _, ei = _first_argmax_rows(masked, e_iota, float(N_EXPERTS))
        hit = e_iota == ei
        idxs.append(ei)
        gates.append(jnp.sum(jnp.where(hit, scores, 0.0), axis=0, keepdims=True))
        onehot = jnp.where(hit, 1.0, onehot)
        masked = jnp.where(hit, neg, masked)
    idx = jnp.concatenate(idxs, axis=0)
    gate = jnp.concatenate(gates, axis=0)
    gate = gate / jnp.sum(gate, axis=0, keepdims=True) * ROUTED_SCALE

    before = _dot(onehot.astype(BF16), tri_ref[...]) + carry[:, 0:1]
    ranks = [jnp.sum(jnp.where(e_iota == idxs[k], before, 0.0), axis=0, keepdims=True) for k in range(TOP_K)]
    idx_ref[...] = idx.astype(jnp.int32)
    gate_ref[...] = gate
    rank_ref[...] = jnp.concatenate(ranks, axis=0).astype(jnp.int32)
    total = carry[...] + jnp.sum(onehot, axis=1, keepdims=True)
    carry[...] = total
    cnt_ref[...] = total


def _route(lg_t, bias, tri):
    e, t = lg_t.shape
    tok = pl.BlockSpec((TOP_K, TR), lambda i: (0, i))
    return pl.pallas_call(
        _route_kernel,
        out_shape=(jax.ShapeDtypeStruct((TOP_K, t), jnp.int32),
                   jax.ShapeDtypeStruct((TOP_K, t), F32),
                   jax.ShapeDtypeStruct((TOP_K, t), jnp.int32),
                   jax.ShapeDtypeStruct((e, LANES), F32)),
        grid=(t // TR,),
        in_specs=[pl.BlockSpec((e, TR), lambda i: (0, i)),
                  pl.BlockSpec((e, 1), lambda i: (0, 0)),
                  pl.BlockSpec((TR, TR), lambda i: (0, 0))],
        out_specs=(tok, tok, tok, pl.BlockSpec((e, LANES), lambda i: (0, 0))),
        scratch_shapes=[pltpu.VMEM((e, LANES), F32)],
        compiler_params=pltpu.CompilerParams(vmem_limit_bytes=VMEM_LIMIT),
        name="route",
    )(lg_t, bias, tri)


def _plan_kernel(n_blocks, size_ref, start_ref, expert_ref, valid_ref, nact_ref):
    def per_expert(e, first_block):
        size = size_ref[e]
        n_blk = (size + TM_EXP - 1) // TM_EXP
        start_ref[e] = first_block * TM_EXP

        def per_block(j, carry):
            expert_ref[first_block + j] = e
            valid_ref[first_block + j] = jnp.minimum(size - j * TM_EXP, TM_EXP)
            return carry

        lax.fori_loop(0, n_blk, per_block, 0)
        return first_block + n_blk

    n_active = lax.fori_loop(0, N_EXPERTS, per_expert, 0)
    nact_ref[0] = n_active

    def unused(i, carry):
        expert_ref[i] = N_EXPERTS - 1
        valid_ref[i] = 0
        return carry

    lax.fori_loop(n_active, n_blocks, unused, 0)


def _plan_blocks(sizes, n_blocks):
    smem = pl.BlockSpec(memory_space=pltpu.SMEM)
    return pl.pallas_call(
        functools.partial(_plan_kernel, n_blocks),
        out_shape=(jax.ShapeDtypeStruct((N_EXPERTS,), jnp.int32),
                   jax.ShapeDtypeStruct((n_blocks,), jnp.int32),
                   jax.ShapeDtypeStruct((n_blocks,), jnp.int32),
                   jax.ShapeDtypeStruct((1,), jnp.int32)),
        in_specs=[smem],
        out_specs=(smem, smem, smem, smem),
        name="plan_blocks",
    )(sizes)


def _dest_kernel(start_ref, idx_ref, rank_ref, dest_ref):
    idx = idx_ref[...]
    dest = rank_ref[...]
    for e in range(N_EXPERTS):
        dest = dest + jnp.where(idx == e, start_ref[e], 0)
    dest_ref[...] = dest


def _dest_rows(pad_start, idx_kt, rank_kt):
    n_k, t = idx_kt.shape
    tile = 4096
    blk = pl.BlockSpec((n_k, tile), lambda i: (0, i))
    return pl.pallas_call(
        _dest_kernel,
        out_shape=jax.ShapeDtypeStruct((n_k, t), jnp.int32),
        grid=(t // tile,),
        in_specs=[pl.BlockSpec(memory_space=pltpu.SMEM), blk, blk],
        out_specs=blk,
        name="dest_rows",
    )(pad_start, idx_kt, rank_kt)


def _sc_mesh():
    return plsc.VectorSubcoreMesh(core_axis_name="c", subcore_axis_name="s")


def _sc_token_base(steps, j):
    worker = lax.axis_index("s") * SC_CORES + lax.axis_index("c")
    return (worker * steps + j) * SC_TOKENS


def _sc_scatter(dest_kt, h2p, n_rows):
    t, width = h2p.shape
    steps = t // (SC_WORKERS * SC_TOKENS)

    @functools.partial(
        pl.kernel, mesh=_sc_mesh(),
        out_type=jax.ShapeDtypeStruct((n_rows, width), U32),
        scratch_types=[pltpu.VMEM((TOP_K, SC_TOKENS), jnp.int32),
                       pltpu.VMEM((SC_TOKENS, width), U32),
                       pltpu.SemaphoreType.DMA],
        name="sc_scatter",
    )
    def body(dest_hbm, h_hbm, xs_hbm, idx_v, rows_v, sem):
        @pl.loop(0, steps)
        def _(j):
            base = _sc_token_base(steps, j)
            pltpu.sync_copy(dest_hbm.at[:, pl.ds(base, SC_TOKENS)], idx_v)
            pltpu.sync_copy(h_hbm.at[pl.ds(base, SC_TOKENS)], rows_v)
            copies = [pltpu.async_copy(rows_v, xs_hbm.at[idx_v.at[k]], sem) for k in range(TOP_K)]
            for cp in copies:
                cp.wait()

    return body(dest_kt, h2p)


def _sc_gather(dest_kt, ys, token0, n_tokens):
    n_k = dest_kt.shape[0]
    width = ys.shape[1]
    steps = n_tokens // (SC_WORKERS * SC_TOKENS)

    half = SC_TOKENS // 2
    units = [(k, h) for k in range(n_k) for h in range(2)]
    n_buf = 1

    @functools.partial(
        pl.kernel, mesh=_sc_mesh(),
        out_type=jax.ShapeDtypeStruct((n_k, n_tokens, width), U32),
        scratch_types=[pltpu.VMEM((n_k, SC_TOKENS), jnp.int32),
                       pltpu.VMEM((n_buf, half, width), U32),
                       pltpu.SemaphoreType.DMA((n_buf,))],
        name="sc_gather",
    )
    def body(dest_hbm, ys_hbm, yk_hbm, idx_v, rows_v, sems):
        @pl.loop(0, steps)
        def _(j):
            base = _sc_token_base(steps, j)
            pltpu.sync_copy(dest_hbm.at[:, pl.ds(token0 + base, SC_TOKENS)], idx_v)

            def gather(u):
                k, h = units[u]
                slot = u % n_buf
                return pltpu.make_async_copy(ys_hbm.at[idx_v.at[k, pl.ds(h * half, half)]], rows_v.at[slot],
                                             sems.at[slot])

            ahead = n_buf - 1
            for u in range(ahead):
                gather(u).start()
            for u, (k, h) in enumerate(units):
                if ahead == 0:
                    gather(u).start()
                gather(u).wait()
                pltpu.sync_copy(rows_v.at[u % n_buf], yk_hbm.at[k, pl.ds(base + h * half, half)])
                if ahead and u + ahead < len(units):
                    gather(u + ahead).start()

    return body(dest_kt, ys)


def _experts_kernel(be_ref, valid_ref, nact_ref, xs_hbm, wg_ref, wu_ref, wd_ref, ys_ref, wg_bf, wu_bf, wd_bf,
                    xbuf, sems):
    i = pl.program_id(0)
    n_active = nact_ref[0]
    slot = lax.rem(i, EXP_SLOTS)

    def fetch(j):
        src = xs_hbm.at[pl.ds(pl.multiple_of(j * TM_EXP, TM_EXP), TM_EXP), :]
        s = lax.rem(j, EXP_SLOTS)
        return pltpu.make_async_copy(src, xbuf.at[s], sems.at[s])

    @pl.when(i == 0)
    def _():
        for j in range(EXP_SLOTS - 1):
            pl.when(j < n_active)(lambda j=j: fetch(j).start())

    @pl.when(i + (EXP_SLOTS - 1) < n_active)
    def _():
        fetch(i + (EXP_SLOTS - 1)).start()

    prev = be_ref[jnp.maximum(i - 1, 0)]

    @pl.when((i == 0) | (be_ref[i] != prev))
    def _():
        wg_bf[...] = wg_ref[0].astype(BF16)
        wu_bf[...] = wu_ref[0].astype(BF16)
        wd_bf[...] = wd_ref[0].astype(BF16)

    @pl.when(i < n_active)
    def _():
        fetch(i).wait()

    valid = jnp.where(i < n_active, valid_ref[i], 0)

    def run(rows):
        row = lax.broadcasted_iota(jnp.int32, (rows, xbuf.shape[2]), 0)
        words = jnp.where(row < valid, xbuf[slot, 0:rows, :], jnp.uint32(0))
        xb = _unpack_words(words).astype(BF16)
        hid = _silu(_dot(xb, wg_bf[...])) * _dot(xb, wu_bf[...])
        ys_ref[0:rows, :] = _pack_words(_dot(hid.astype(BF16), wd_bf[...]))
        if rows < TM_EXP:
            ys_ref[rows:, :] = jnp.zeros((TM_EXP - rows, ys_ref.shape[1]), U32)

    for rows in range(EXP_MIN_ROWS, TM_EXP + 1, EXP_MIN_ROWS):
        pl.when((valid > rows - EXP_MIN_ROWS) & (valid <= rows))(functools.partial(run, rows))

    @pl.when(valid == 0)
    def _():
        ys_ref[...] = jnp.zeros_like(ys_ref)


def _experts(block_e, block_valid, n_active, xs, w_gate, w_up, w_down):
    n_rows, half = xs.shape
    d = w_gate.shape[1]
    nb = n_rows // TM_EXP
    grid_spec = pltpu.PrefetchScalarGridSpec(
        num_scalar_prefetch=3,
        grid=(nb,),
        in_specs=[pl.BlockSpec(memory_space=pl.ANY),
                  pl.BlockSpec((1, d, D_EXPERT), lambda i, be, bv, na: (be[i], 0, 0)),
                  pl.BlockSpec((1, d, D_EXPERT), lambda i, be, bv, na: (be[i], 0, 0)),
                  pl.BlockSpec((1, D_EXPERT, d), lambda i, be, bv, na: (be[i], 0, 0))],
        out_specs=pl.BlockSpec((TM_EXP, half), lambda i, be, bv, na: (i, 0)),
        scratch_shapes=[pltpu.VMEM((d, D_EXPERT), BF16), pltpu.VMEM((d, D_EXPERT), BF16),
                        pltpu.VMEM((D_EXPERT, d), BF16),
                        pltpu.VMEM((EXP_SLOTS, TM_EXP, half), U32),
                        pltpu.SemaphoreType.DMA((EXP_SLOTS,))],
    )
    return pl.pallas_call(
        _experts_kernel,
        out_shape=jax.ShapeDtypeStruct((n_rows, half), U32),
        grid_spec=grid_spec,
        compiler_params=pltpu.CompilerParams(vmem_limit_bytes=VMEM_LIMIT, dimension_semantics=("arbitrary",)),
        name="experts",
    )(block_e, block_valid, n_active, xs, w_gate, w_up, w_down)


def _combine_kernel(yk_ref, x1_ref, h2_ref, gate_ref, mod_ref, wsg_ref, wsu_ref, wsd_ref, fg_ref, *out_refs):
    out_ref = out_refs[-1]
    hb = _unpack_words(h2_ref[...]).astype(BF16)
    hid = _silu(_dot(hb, wsg_ref[...])) * _dot(hb, wsu_ref[...])
    ffn = _dot(hid.astype(BF16), wsd_ref[...])
    gate = gate_ref[...]
    for k in range(TOP_K):
        ffn = ffn + gate[:, k:k + 1] * _unpack_words(yk_ref[k])
    x2 = x1_ref[...] + mod_ref[0, 5:6, :] * ffn
    ms = jnp.mean(x2 * x2, axis=-1, keepdims=True)
    out_ref[...] = x2 * lax.rsqrt(ms + EPS) * fg_ref[...]


def _combine(yk, token0, out_token0, n_out_tokens, prev_out, x1, h2p, gate_tk, mod, wsg_bf, wsu_bf, wsd_bf, final_g,
             seq_len):
    t, d = n_out_tokens, x1.shape[1]
    tiles_per_seq = seq_len // TF
    tile0 = token0 // TF
    out_tile0 = out_token0 // TF
    in_tok = pl.BlockSpec((TF, d), lambda i: (tile0 + i, 0))
    tok = pl.BlockSpec((TF, d), lambda i: (out_tile0 + i, 0))
    in_specs = [pl.BlockSpec((TOP_K, TF, d // 2), lambda i: (0, i, 0)),
                in_tok, pl.BlockSpec((TF, d // 2), lambda i: (tile0 + i, 0)),
                pl.BlockSpec((TF, TOP_K), lambda i: (tile0 + i, 0)),
                pl.BlockSpec((1, 6, d), lambda i: ((out_tile0 + i) // tiles_per_seq, 0, 0)),
                pl.BlockSpec((d, D_SHARED), lambda i: (0, 0)),
                pl.BlockSpec((d, D_SHARED), lambda i: (0, 0)),
                pl.BlockSpec((D_SHARED, d), lambda i: (0, 0)),
                pl.BlockSpec((1, d), lambda i: (0, 0))]
    args = [yk, x1, h2p, gate_tk, mod, wsg_bf, wsu_bf, wsd_bf, final_g]
    aliases = {}
    if prev_out is not None:
        in_specs.append(pl.BlockSpec(memory_space=pl.ANY))
        args.append(prev_out)
        aliases = {len(args) - 1: 0}
    return pl.pallas_call(
        _combine_kernel,
        out_shape=jax.ShapeDtypeStruct((t, d), F32),
        grid=(yk.shape[1] // TF,),
        in_specs=in_specs,
        out_specs=tok,
        input_output_aliases=aliases,
        compiler_params=pltpu.CompilerParams(vmem_limit_bytes=VMEM_LIMIT),
        name="combine",
    )(*args)


def _rope_tables(n_tokens):
    n_rows = n_tokens // GRID_W
    n_freq = HEAD_DIM // 4
    inv_freq = ROPE_THETA ** (-jnp.arange(n_freq, dtype=F32) / n_freq)
    ang_r = jnp.arange(n_rows).astype(F32)[:, None] * inv_freq[None, :]
    ang_c = jnp.arange(GRID_W).astype(F32)[:, None] * inv_freq[None, :]

    def per_token(row_part, col_part):
        rows = jnp.broadcast_to(row_part[:, None, :], (n_rows, GRID_W, n_freq))
        cols = jnp.broadcast_to(col_part[None, :, :], (n_rows, GRID_W, n_freq))
        return rows.reshape(n_tokens, n_freq), cols.reshape(n_tokens, n_freq)

    cos_r, cos_c = per_token(jnp.cos(ang_r), jnp.cos(ang_c))
    sin_r, sin_c = per_token(jnp.sin(ang_r), jnp.sin(ang_c))
    cos = jnp.concatenate([cos_r, cos_r, cos_c, cos_c], axis=1)
    sin = jnp.concatenate([-sin_r, sin_r, -sin_c, sin_c], axis=1)
    reps = LANES // HEAD_DIM
    return jnp.tile(cos, (1, reps)), jnp.tile(sin, (1, reps))


def _pool_bands():
    i = jnp.arange(QB)[:, None]
    r = jnp.arange(POOL_SLAB)[None, :]
    return jnp.stack([((r >= i + POOL_OFF - w // 2) & (r < i + POOL_OFF + w // 2)).astype(BF16)
                      for w in POOL_WINDOWS])


def kernel(x, c, ctx, c_ctx, w_ada, b_ada, norm1_g, norm2_g, w_in, attn_sink, pool_w, pool_scale, w_out,
           w_router, router_bias, w_gate, w_up, w_down, ws_gate, ws_up, ws_down, final_g):
    b, s, d = x.shape
    t = b * s
    assert w_ada.shape[0] == 1 and d == D_MODEL and s % TQ == 0 and b + 1 <= 8

    c8 = jnp.zeros((8, d), F32).at[:b].set(c).at[b].set(c_ctx)
    mod = _ada(c8, w_ada[0], b_ada[0]).reshape(8, 6, d)
    g1 = norm1_g[0].reshape(1, d)
    g2 = norm2_g[0].reshape(1, d)
    w_in_bf = w_in[0].astype(BF16)
    cos_t, sin_t = _rope_tables(s)

    q, k4, v4, p = _inproj(x, mod, g1, w_in_bf, cos_t, sin_t)
    kc4, vc4 = _ctxproj(ctx, mod[b:b + 1], g1, w_in_bf[:, ATTN_WIDTH:ATTN_WIDTH + 2 * KV_WIDTH])

    wr_t = w_router[0].T
    wr_hi = wr_t.astype(BF16)
    wr_lo = (wr_t - wr_hi.astype(F32)).astype(BF16)
    attn_consts = (_pool_bands(), pool_w[0].astype(BF16), pool_scale[0].reshape(1, POOL_WIDTH),
                   w_out[0].astype(BF16), g2, wr_hi, wr_lo)
    tri = jnp.triu(jnp.ones((TR, TR), BF16), k=1)
    shared_w = (ws_gate[0].astype(BF16), ws_up[0].astype(BF16), ws_down[0].astype(BF16))

    nb = b // TOKEN_GROUPS
    tg = nb * s
    assert b % TOKEN_GROUPS == 0 and tg % (COMBINE_CHUNKS * SC_WORKERS * SC_TOKENS) == 0
    n_rows = -(-(tg * TOP_K + N_EXPERTS * (TM_EXP - 1)) // TM_EXP) * TM_EXP
    groups = []
    for g in range(TOKEN_GROUPS):
        x1, h2p, lg_t = _attn(g * nb, nb, attn_sink[0], q, k4, v4, kc4, vc4, p, x, mod, *attn_consts)
        idx_kt, gate_kt, rank_kt, counts = _route(lg_t, router_bias[0].reshape(N_EXPERTS, 1), tri)
        pad_start, block_e, block_valid, n_active = _plan_blocks(counts[:, 0].astype(jnp.int32), n_rows // TM_EXP)
        dest_kt = _dest_rows(pad_start, idx_kt, rank_kt)
        xs = _sc_scatter(dest_kt, h2p, n_rows)
        groups.append((x1.reshape(tg, d), h2p, gate_kt.T, dest_kt, xs, block_e, block_valid, n_active))

    out = None
    chunk = tg // COMBINE_CHUNKS
    for g, (x1, h2p, gate_tk, dest_kt, xs, block_e, block_valid, n_active) in enumerate(groups):
        ys = _experts(block_e, block_valid, n_active, xs, w_gate[0], w_up[0], w_down[0])
        for token0 in range(0, tg, chunk):
            out = _combine(_sc_gather(dest_kt, ys, token0, chunk), token0, g * tg + token0, t, out, x1, h2p,
                           gate_tk, mod, *shared_w, final_g.reshape(1, d), s)
    return out.reshape(b, s, d)
```

```python
import functools

import jax
import jax.numpy as jnp
from jax import lax
from jax.experimental import pallas as pl
from jax.experimental.pallas import tpu as pltpu
from jax.experimental.pallas import tpu_sc as plsc

F32 = jnp.float32
BF16 = jnp.bfloat16

D_MODEL = 1024
GRID_W = 64
N_HEADS = 8
N_KV_HEADS = 2
HEAD_DIM = 64
ATTN_WIDTH = N_HEADS * HEAD_DIM
KV_WIDTH = N_KV_HEADS * HEAD_DIM
WINDOW = 128
ROPE_THETA = 10000.0
POOL_WINDOWS = (2, 4, 8, 16)
POOL_WIDTH = D_MODEL - ATTN_WIDTH
POOL_GROUP_DIM = POOL_WIDTH // len(POOL_WINDOWS)
IN_COLS = ATTN_WIDTH + 2 * KV_WIDTH + POOL_WIDTH
N_EXPERTS = 64
TOP_K = 8
N_EXPERT_GROUPS = 8
EXPERTS_PER_GROUP = N_EXPERTS // N_EXPERT_GROUPS
TOPK_GROUPS = 4
D_EXPERT = 256
D_SHARED = 256
ROUTED_SCALE = 2.5
EPS = 1e-6
LOG2E = 1.4426950408889634

LANES = 128
U32 = jnp.uint32
VMEM_LIMIT = 48 * 1024 * 1024

TM_PROJ = 1024
TQ = 1024
QB = 128
POOL_SLAB = 256
POOL_OFF = 64
TR = 512
TM_EXP = 1024
EXP_MIN_ROWS = 256
EXP_SLOTS = 4
TF = 512
TOKEN_GROUPS = 2
COMBINE_CHUNKS = 2
SC_CORES = 2
SC_WORKERS = 32
SC_TOKENS = 128
SC_SCATTER_BATCH = 2


def _silu(x):
    return x * (1.0 / (1.0 + jnp.exp(-x)))


def _split_bf16(x):
    hi = x.astype(BF16)
    lo = (x - hi.astype(F32)).astype(BF16)
    return hi, lo


def _dot(a, b):
    return jnp.dot(a, b, preferred_element_type=F32)


def _pack_words(val):
    half = val.shape[1] // 2
    lo = lax.bitcast_convert_type(val[:, :half].astype(BF16).astype(F32), U32)
    hi = lax.bitcast_convert_type(val[:, half:].astype(BF16).astype(F32), U32)
    return lax.shift_right_logical(lo, jnp.uint32(16)) | hi


def _unpack_words(words):
    lo = lax.bitcast_convert_type(lax.shift_left(words, jnp.uint32(16)), F32)
    hi = lax.bitcast_convert_type(words & jnp.uint32(0xFFFF0000), F32)
    return jnp.concatenate([lo, hi], axis=1)


def _dot_nt(a, b):
    return lax.dot_general(a, b, (((1,), (1,)), ((), ())), preferred_element_type=F32)


def _ada_kernel(c_ref, w_ref, b_ref, o_ref):
    a_hi, a_lo = _split_bf16(_silu(c_ref[...]))
    w_hi, w_lo = _split_bf16(w_ref[...])
    o_ref[...] = _dot(a_hi, w_hi) + _dot(a_lo, w_hi) + _dot(a_hi, w_lo) + b_ref[...]


def _ada(c8, w_ada, b_ada):
    d = c8.shape[1]
    n = w_ada.shape[1]
    tn = 512
    return pl.pallas_call(
        _ada_kernel,
        out_shape=jax.ShapeDtypeStruct((8, n), F32),
        grid=(n // tn,),
        in_specs=[pl.BlockSpec((8, d), lambda j: (0, 0)),
                  pl.BlockSpec((d, tn), lambda j: (0, j)),
                  pl.BlockSpec((1, tn), lambda j: (0, j))],
        out_specs=pl.BlockSpec((8, tn), lambda j: (0, j)),
        compiler_params=pltpu.CompilerParams(vmem_limit_bytes=VMEM_LIMIT),
        name="ada",
    )(c8, w_ada, b_ada.reshape(1, n))


def _norm_mod(x, g, shift, scale):
    ms = jnp.mean(x * x, axis=-1, keepdims=True)
    return (x * lax.rsqrt(ms + EPS) * g) * (1.0 + scale) + shift


def _lane_variants(t):
    lane = lax.broadcasted_iota(jnp.int32, t.shape, 1)
    lo = lane < HEAD_DIM
    tr = pltpu.roll(t, HEAD_DIM, 1)
    zero = jnp.zeros_like(t)
    return (jnp.where(lo, t, zero), jnp.where(lo, zero, tr),
            jnp.where(lo, tr, zero), jnp.where(lo, zero, t))


def _store_variants(ref, t):
    for i, var in enumerate(_lane_variants(t)):
        ref[0, :, i * LANES:(i + 1) * LANES] = var.astype(BF16)


def _inproj_kernel(x_ref, mod_ref, g_ref, w_ref, cos_ref, sin_ref, q_ref, k_ref, v_ref, p_ref):
    h = _norm_mod(x_ref[0], g_ref[...], mod_ref[0, 0:1, :], mod_ref[0, 1:2, :])
    z = _dot(h.astype(BF16), w_ref[...])
    cos = cos_ref[...]
    sin = sin_ref[...]
    lane = lax.broadcasted_iota(jnp.int32, cos.shape, 1)
    first_half = (lane & 16) == 0

    def rope(zc):
        partner = jnp.where(first_half, pltpu.roll(zc, LANES - 16, 1), pltpu.roll(zc, 16, 1))
        return zc * cos + partner * sin

    scale = HEAD_DIM ** -0.5 * LOG2E
    for c in range(ATTN_WIDTH // LANES):
        q_ref[0, :, c * LANES:(c + 1) * LANES] = (rope(z[:, c * LANES:(c + 1) * LANES]) * scale).astype(BF16)
    _store_variants(k_ref, rope(z[:, ATTN_WIDTH:ATTN_WIDTH + KV_WIDTH]))
    _store_variants(v_ref, z[:, ATTN_WIDTH + KV_WIDTH:ATTN_WIDTH + 2 * KV_WIDTH])
    p_ref[0] = z[:, ATTN_WIDTH + 2 * KV_WIDTH:]


def _inproj(x, mod, g1, w_in_bf, cos_t, sin_t):
    b, s, d = x.shape
    tm = TM_PROJ
    return pl.pallas_call(
        _inproj_kernel,
        out_shape=(jax.ShapeDtypeStruct((b, s, ATTN_WIDTH), BF16),
                   jax.ShapeDtypeStruct((b, s, 4 * LANES), BF16),
                   jax.ShapeDtypeStruct((b, s, 4 * LANES), BF16),
                   jax.ShapeDtypeStruct((b, s, POOL_WIDTH), F32)),
        grid=(s // tm, b),
        in_specs=[pl.BlockSpec((1, tm, d), lambda n, bi: (bi, n, 0)),
                  pl.BlockSpec((1, 6, d), lambda n, bi: (bi, 0, 0)),
                  pl.BlockSpec((1, d), lambda n, bi: (0, 0)),
                  pl.BlockSpec((d, IN_COLS), lambda n, bi: (0, 0)),
                  pl.BlockSpec((tm, LANES), lambda n, bi: (n, 0)),
                  pl.BlockSpec((tm, LANES), lambda n, bi: (n, 0))],
        out_specs=(pl.BlockSpec((1, tm, ATTN_WIDTH), lambda n, bi: (bi, n, 0)),
                   pl.BlockSpec((1, tm, 4 * LANES), lambda n, bi: (bi, n, 0)),
                   pl.BlockSpec((1, tm, 4 * LANES), lambda n, bi: (bi, n, 0)),
                   pl.BlockSpec((1, tm, POOL_WIDTH), lambda n, bi: (bi, n, 0))),
        compiler_params=pltpu.CompilerParams(vmem_limit_bytes=VMEM_LIMIT),
        name="inproj",
    )(x, mod, g1, w_in_bf, cos_t, sin_t)


def _ctxproj_kernel(x_ref, mod_ref, g_ref, w_ref, k_ref, v_ref):
    h = _norm_mod(x_ref[0], g_ref[...], mod_ref[0, 0:1, :], mod_ref[0, 1:2, :])
    z = _dot(h.astype(BF16), w_ref[...])
    _store_variants(k_ref, z[:, :KV_WIDTH])
    _store_variants(v_ref, z[:, KV_WIDTH:])


def _ctxproj(ctx, mod_c, g1, w_kv_bf):
    b, c, d = ctx.shape
    return pl.pallas_call(
        _ctxproj_kernel,
        out_shape=(jax.ShapeDtypeStruct((b, c, 4 * LANES), BF16),
                   jax.ShapeDtypeStruct((b, c, 4 * LANES), BF16)),
        grid=(b,),
        in_specs=[pl.BlockSpec((1, c, d), lambda bi: (bi, 0, 0)),
                  pl.BlockSpec((1, 6, d), lambda bi: (0, 0, 0)),
                  pl.BlockSpec((1, d), lambda bi: (0, 0)),
                  pl.BlockSpec((d, 2 * KV_WIDTH), lambda bi: (0, 0))],
        out_specs=(pl.BlockSpec((1, c, 4 * LANES), lambda bi: (bi, 0, 0)),
                   pl.BlockSpec((1, c, 4 * LANES), lambda bi: (bi, 0, 0))),
        compiler_params=pltpu.CompilerParams(vmem_limit_bytes=VMEM_LIMIT),
        name="ctxproj",
    )(ctx, mod_c, g1, w_kv_bf)


def _fold(op, tiles):
    while len(tiles) > 1:
        tiles = [op(tiles[i], tiles[i + 1]) if i + 1 < len(tiles) else tiles[i] for i in range(0, len(tiles), 2)]
    return tiles[0]


def _stack_variants(t4, kv):
    return jnp.concatenate([t4[:, (2 * kv) * LANES:(2 * kv + 1) * LANES],
                            t4[:, (2 * kv + 1) * LANES:(2 * kv + 2) * LANES]], axis=0)


def _attn_kernel(seq_len, sink_ref, q_ref, k_ref, kp_ref, kn_ref, v_ref, vp_ref, vn_ref, kc_ref, vc_ref,
                 p_ref, pp_ref, pn_ref, x_ref, mod_ref, band_ref, poolw_ref, pscale_ref, wout_ref,
                 g2_ref, wrh_ref, wrl_ref, x1_ref, h2_ref, lg_ref, kwin, vwin, pext, mix, s_scr, p_scr, m_scr):
    n = pl.program_id(1)
    n_last = pl.num_programs(1) - 1

    kwin[0:QB, :] = kp_ref[0]
    kwin[QB:QB + TQ, :] = k_ref[0]
    kwin[QB + TQ:, :] = kn_ref[0]
    vwin[0:QB, :] = vp_ref[0]
    vwin[QB:QB + TQ, :] = v_ref[0]
    vwin[QB + TQ:, :] = vn_ref[0]

    pext[0:QB - 8, :] = jnp.zeros((QB - 8, POOL_WIDTH), F32)
    pext[QB - 8:QB, :] = jnp.where(n > 0, pp_ref[0], 0.0)
    pext[QB:QB + TQ, :] = p_ref[0]
    pext[QB + TQ:QB + TQ + 8, :] = jnp.where(n < n_last, pn_ref[0], 0.0)
    pext[QB + TQ + 8:, :] = jnp.zeros((QB - 8, POOL_WIDTH), F32)

    row = lax.broadcasted_iota(jnp.int32, (QB, 3 * QB), 0)
    col = lax.broadcasted_iota(jnp.int32, (QB, 3 * QB), 1)
    in_band = (col >= row) & (col <= row + 2 * WINDOW)
    tok = lax.broadcasted_iota(jnp.int32, (QB, 1), 0)
    kc = kc_ref[0]
    kc_rows = [_stack_variants(kc, kv) for kv in range(N_KV_HEADS)]
    vc_rows = [_stack_variants(vc_ref[0], kv) for kv in range(N_KV_HEADS)]

    def sub_block(j, carry):
        r0 = pl.multiple_of(j * QB, QB)
        qj = q_ref[0, pl.ds(r0, QB), :]
        kw = kwin[pl.ds(r0, 3 * QB), :]
        vw = vwin[pl.ds(r0, 3 * QB), :]
        kpos = col + (n * TQ + j * QB - QB)
        ok = in_band & (kpos >= 0) & (kpos < seq_len)
        bias = jnp.where(ok, 0.0, -jnp.inf)
        n_loc, n_ctx = 3 * QB, kc.shape[0]
        bias2 = jnp.concatenate([bias, bias], axis=1)
        k_rows = [_stack_variants(kw, kv) for kv in range(N_KV_HEADS)]
        v_rows = [_stack_variants(vw, kv) for kv in range(N_KV_HEADS)]
        group = N_HEADS // N_KV_HEADS

        def head_tiles(head):
            loc0 = (head % 2) * n_loc
            ctx0 = 2 * n_loc + (head % 2) * n_ctx
            return ([loc0 + i * LANES for i in range(n_loc // LANES)]
                    + [ctx0 + i * LANES for i in range(n_ctx // LANES)])

        for c in range(N_HEADS // 2):
            qc = qj[:, c * LANES:(c + 1) * LANES]
            s_scr[c, :, 0:2 * n_loc] = _dot_nt(qc, k_rows[2 * c // group]) + bias2
            s_scr[c, :, 2 * n_loc:] = _dot_nt(qc, kc_rows[2 * c // group])
        for head in range(N_HEADS):
            tiles = [s_scr[head // 2, :, st:st + LANES] for st in head_tiles(head)]
            row_max = jnp.max(_fold(jnp.maximum, tiles), axis=1, keepdims=True)
            m_scr[head] = jnp.broadcast_to(jnp.maximum(row_max, sink_ref[head] * LOG2E), (QB, LANES))
        for head in range(N_HEADS):
            m = m_scr[head]
            acc = None
            for st in head_tiles(head):
                p = jnp.exp2(s_scr[head // 2, :, st:st + LANES] - m)
                p_scr[head // 2, :, st:st + LANES] = p.astype(BF16)
                acc = p if acc is None else acc + p
            denom = (jnp.broadcast_to(jnp.sum(acc, axis=1, keepdims=True), (QB, LANES))
                     + jnp.exp2(sink_ref[head] * LOG2E - m))
            m_scr[head] = 1.0 / denom
        lane = lax.broadcasted_iota(jnp.int32, (QB, LANES), 1)
        for c in range(N_HEADS // 2):
            o = (_dot(p_scr[c, :, 0:2 * n_loc], v_rows[2 * c // group])
                 + _dot(p_scr[c, :, 2 * n_loc:], vc_rows[2 * c // group]))
            o = o * jnp.where(lane < HEAD_DIM, m_scr[2 * c], m_scr[2 * c + 1])
            mix[pl.ds(r0, QB), c * LANES:(c + 1) * LANES] = o.astype(BF16)

        slab = pext[pl.ds(pl.multiple_of(r0 + POOL_OFF, 8), POOL_SLAB), :]
        tpos = tok + (n * TQ + j * QB)
        for g, w in enumerate(POOL_WINDOWS):
            sg = slab[:, g * LANES:(g + 1) * LANES]
            hi, lo = _split_bf16(sg)
            band = band_ref[g]
            wsum = _dot(band, hi) + _dot(band, lo)
            cnt = (jnp.minimum(tpos - w // 2 + w, seq_len) - jnp.maximum(tpos - w // 2, 0)).astype(F32)
            dlt = wsum / cnt - sg[POOL_OFF:POOL_OFF + QB, :]
            y = _dot(dlt.astype(BF16), poolw_ref[g]) * pscale_ref[:, g * LANES:(g + 1) * LANES]
            mix[pl.ds(r0, QB), ATTN_WIDTH + g * LANES:ATTN_WIDTH + (g + 1) * LANES] = y.astype(BF16)
        return carry

    lax.fori_loop(0, TQ // QB, sub_block, 0)

    proj = _dot(mix[...], wout_ref[...])
    x1 = x_ref[0] + mod_ref[0, 2:3, :] * proj
    x1_ref[0] = x1
    h2 = _norm_mod(x1, g2_ref[...], mod_ref[0, 3:4, :], mod_ref[0, 4:5, :])
    h2_ref[...] = _pack_words(h2)
    h_hi, h_lo = _split_bf16(h2)
    wrh = wrh_ref[...]
    lg_ref[...] = _dot_nt(wrh, h_hi) + _dot_nt(wrh, h_lo) + _dot_nt(wrl_ref[...], h_hi)


def _attn(b0, b, sink, q, k4, v4, kc4, vc4, p, x, mod, band, poolw_bf, pscale, wout_bf, g2, wr_hi, wr_lo):
    _, s, d = x.shape
    c = kc4.shape[1]
    nt = s // TQ
    hb = TQ // QB
    pb = TQ // 8
    kv_main = pl.BlockSpec((1, TQ, 4 * LANES), lambda bi, n: (b0 + bi, n, 0))
    kv_prev = pl.BlockSpec((1, QB, 4 * LANES), lambda bi, n: (b0 + bi, jnp.maximum(n * hb - 1, 0), 0))
    kv_next = pl.BlockSpec((1, QB, 4 * LANES), lambda bi, n: (b0 + bi, jnp.minimum((n + 1) * hb, s // QB - 1), 0))
    const2 = lambda bi, n: (0, 0)
    const3 = lambda bi, n: (0, 0, 0)
    return pl.pallas_call(
        functools.partial(_attn_kernel, s),
        out_shape=(jax.ShapeDtypeStruct((b, s, d), F32),
                   jax.ShapeDtypeStruct((b * s, d // 2), U32),
                   jax.ShapeDtypeStruct((N_EXPERTS, b * s), F32)),
        grid=(b, nt),
        in_specs=[pl.BlockSpec(memory_space=pltpu.SMEM),
                  pl.BlockSpec((1, TQ, ATTN_WIDTH), lambda bi, n: (b0 + bi, n, 0)),
                  kv_main, kv_prev, kv_next, kv_main, kv_prev, kv_next,
                  pl.BlockSpec((1, c, 4 * LANES), lambda bi, n: (b0 + bi, 0, 0)),
                  pl.BlockSpec((1, c, 4 * LANES), lambda bi, n: (b0 + bi, 0, 0)),
                  pl.BlockSpec((1, TQ, POOL_WIDTH), lambda bi, n: (b0 + bi, n, 0)),
                  pl.BlockSpec((1, 8, POOL_WIDTH), lambda bi, n: (b0 + bi, jnp.maximum(n * pb - 1, 0), 0)),
                  pl.BlockSpec((1, 8, POOL_WIDTH),
                               lambda bi, n: (b0 + bi, jnp.minimum((n + 1) * pb, s // 8 - 1), 0)),
                  pl.BlockSpec((1, TQ, d), lambda bi, n: (b0 + bi, n, 0)),
                  pl.BlockSpec((1, 6, d), lambda bi, n: (b0 + bi, 0, 0)),
                  pl.BlockSpec((len(POOL_WINDOWS), QB, POOL_SLAB), const3),
                  pl.BlockSpec((len(POOL_WINDOWS), POOL_GROUP_DIM, POOL_GROUP_DIM), const3),
                  pl.BlockSpec((1, POOL_WIDTH), const2),
                  pl.BlockSpec((d, d), const2),
                  pl.BlockSpec((1, d), const2),
                  pl.BlockSpec((N_EXPERTS, d), const2),
                  pl.BlockSpec((N_EXPERTS, d), const2)],
        out_specs=(pl.BlockSpec((1, TQ, d), lambda bi, n: (bi, n, 0)),
                   pl.BlockSpec((TQ, d // 2), lambda bi, n: (bi * nt + n, 0)),
                   pl.BlockSpec((N_EXPERTS, TQ), lambda bi, n: (0, bi * nt + n))),
        scratch_shapes=[pltpu.VMEM((TQ + 2 * QB, 4 * LANES), BF16),
                        pltpu.VMEM((TQ + 2 * QB, 4 * LANES), BF16),
                        pltpu.VMEM((TQ + 2 * QB, POOL_WIDTH), F32),
                        pltpu.VMEM((TQ, d), BF16),
                        pltpu.VMEM((N_HEADS // 2, QB, 2 * (3 * QB + c)), F32),
                        pltpu.VMEM((N_HEADS // 2, QB, 2 * (3 * QB + c)), BF16),
                        pltpu.VMEM((N_HEADS, QB, LANES), F32)],
        compiler_params=pltpu.CompilerParams(vmem_limit_bytes=VMEM_LIMIT),
        name="attn",
    )(sink, q, k4, k4, k4, v4, v4, v4, kc4, vc4, p, p, p, x, mod, band, poolw_bf, pscale, wout_bf,
      g2, wr_hi, wr_lo)


def _first_argmax_rows(v, row_iota, n_rows):
    m = jnp.max(v, axis=0, keepdims=True)
    idx = jnp.min(jnp.where(v == m, row_iota, n_rows), axis=0, keepdims=True)
    return m, idx


def _route_kernel(lg_ref, bias_ref, tri_ref, idx_ref, gate_ref, rank_ref, cnt_ref, carry):
    i = pl.program_id(0)

    @pl.when(i == 0)
    def _():
        carry[...] = jnp.zeros_like(carry)

    scores = 1.0 / (1.0 + jnp.exp(-lg_ref[...]))
    biased = scores + bias_ref[...]
    e_iota = lax.broadcasted_iota(jnp.int32, scores.shape, 0).astype(F32)
    g_iota = lax.broadcasted_iota(jnp.int32, (EXPERTS_PER_GROUP, TR), 0).astype(F32)
    neg = -jnp.inf

    grp = []
    for g in range(N_EXPERT_GROUPS):
        blk = biased[g * EXPERTS_PER_GROUP:(g + 1) * EXPERTS_PER_GROUP, :]
        m1, i1 = _first_argmax_rows(blk, g_iota, float(EXPERTS_PER_GROUP))
        m2 = jnp.max(jnp.where(g_iota == i1, neg, blk), axis=0, keepdims=True)
        grp.append(m1 + m2)
    grp = jnp.concatenate(grp, axis=0)
    gg_iota = lax.broadcasted_iota(jnp.int32, grp.shape, 0).astype(F32)
    grp_sel = jnp.zeros(grp.shape, F32)
    for _ in range(TOPK_GROUPS):
        _, gi = _first_argmax_rows(grp, gg_iota, float(N_EXPERT_GROUPS))
        hit = gg_iota == gi
        grp_sel = jnp.where(hit, 1.0, grp_sel)
        grp = jnp.where(hit, neg, grp)
    allowed = jnp.concatenate(
        [jnp.broadcast_to(grp_sel[g:g + 1, :], (EXPERTS_PER_GROUP, TR)) for g in range(N_EXPERT_GROUPS)], axis=0)
    masked = jnp.where(allowed > 0.5, biased, neg)

    idxs, gates = [], []
    onehot = jnp.zeros(scores.shape, F32)
    for _ in range(TOP_K):
        _, ei = _first_argmax_rows(masked, e_iota, float(N_EXPERTS))
        hit = e_iota == ei
        idxs.append(ei)
        gates.append(jnp.sum(jnp.where(hit, scores, 0.0), axis=0, keepdims=True))
        onehot = jnp.where(hit, 1.0, onehot)
        masked = jnp.where(hit, neg, masked)
    idx = jnp.concatenate(idxs, axis=0)
    gate = jnp.concatenate(gates, axis=0)
    gate = gate / jnp.sum(gate, axis=0, keepdims=True) * ROUTED_SCALE

    before = _dot(onehot.astype(BF16), tri_ref[...]) + carry[:, 0:1]
    ranks = [jnp.sum(jnp.where(e_iota == idxs[k], before, 0.0), axis=0, keepdims=True) for k in range(TOP_K)]
    idx_ref[...] = idx.astype(jnp.int32)
    gate_ref[...] = gate
    rank_ref[...] = jnp.concatenate(ranks, axis=0).astype(jnp.int32)
    total = carry[...] + jnp.sum(onehot, axis=1, keepdims=True)
    carry[...] = total
    cnt_ref[...] = total


def _route(lg_t, bias, tri):
    e, t = lg_t.shape
    tok = pl.BlockSpec((TOP_K, TR), lambda i: (0, i))
    return pl.pallas_call(
        _route_kernel,
        out_shape=(jax.ShapeDtypeStruct((TOP_K, t), jnp.int32),
                   jax.ShapeDtypeStruct((TOP_K, t), F32),
                   jax.ShapeDtypeStruct((TOP_K, t), jnp.int32),
                   jax.ShapeDtypeStruct((e, LANES), F32)),
        grid=(t // TR,),
        in_specs=[pl.BlockSpec((e, TR), lambda i: (0, i)),
                  pl.BlockSpec((e, 1), lambda i: (0, 0)),
                  pl.BlockSpec((TR, TR), lambda i: (0, 0))],
        out_specs=(tok, tok, tok, pl.BlockSpec((e, LANES), lambda i: (0, 0))),
        scratch_shapes=[pltpu.VMEM((e, LANES), F32)],
        compiler_params=pltpu.CompilerParams(vmem_limit_bytes=VMEM_LIMIT),
        name="route",
    )(lg_t, bias, tri)


def _plan_kernel(n_blocks, size_ref, start_ref, expert_ref, valid_ref, nact_ref):
    def per_expert(e, first_block):
        size = size_ref[e]
        n_blk = (size + TM_EXP - 1) // TM_EXP
        start_ref[e] = first_block * TM_EXP

        def per_block(j, carry):
            expert_ref[first_block + j] = e
            valid_ref[first_block + j] = jnp.minimum(size - j * TM_EXP, TM_EXP)
            return carry

        lax.fori_loop(0, n_blk, per_block, 0)
        return first_block + n_blk

    n_active = lax.fori_loop(0, N_EXPERTS, per_expert, 0)
    nact_ref[0] = n_active

    def unused(i, carry):
        expert_ref[i] = N_EXPERTS - 1
        valid_ref[i] = 0
        return carry

    lax.fori_loop(n_active, n_blocks, unused, 0)


def _plan_blocks(sizes, n_blocks):
    smem = pl.BlockSpec(memory_space=pltpu.SMEM)
    return pl.pallas_call(
        functools.partial(_plan_kernel, n_blocks),
        out_shape=(jax.ShapeDtypeStruct((N_EXPERTS,), jnp.int32),
                   jax.ShapeDtypeStruct((n_blocks,), jnp.int32),
                   jax.ShapeDtypeStruct((n_blocks,), jnp.int32),
                   jax.ShapeDtypeStruct((1,), jnp.int32)),
        in_specs=[smem],
        out_specs=(smem, smem, smem, smem),
        name="plan_blocks",
    )(sizes)


def _dest_kernel(start_ref, idx_ref, rank_ref, dest_ref):
    idx = idx_ref[...]
    dest = rank_ref[...]
    for e in range(N_EXPERTS):
        dest = dest + jnp.where(idx == e, start_ref[e], 0)
    dest_ref[...] = dest


def _dest_rows(pad_start, idx_kt, rank_kt):
    n_k, t = idx_kt.shape
    tile = 4096
    blk = pl.BlockSpec((n_k, tile), lambda i: (0, i))
    return pl.pallas_call(
        _dest_kernel,
        out_shape=jax.ShapeDtypeStruct((n_k, t), jnp.int32),
        grid=(t // tile,),
        in_specs=[pl.BlockSpec(memory_space=pltpu.SMEM), blk, blk],
        out_specs=blk,
        name="dest_rows",
    )(pad_start, idx_kt, rank_kt)


def _sc_mesh():
    return plsc.VectorSubcoreMesh(core_axis_name="c", subcore_axis_name="s")


def _sc_token_base(steps, j):
    worker = lax.axis_index("s") * SC_CORES + lax.axis_index("c")
    return (worker * steps + j) * SC_TOKENS


def _sc_scatter(dest_kt, h2p, n_rows):
    t, width = h2p.shape
    steps = t // (SC_WORKERS * SC_TOKENS)

    @functools.partial(
        pl.kernel, mesh=_sc_mesh(),
        out_type=jax.ShapeDtypeStruct((n_rows, width), U32),
        scratch_types=[pltpu.VMEM((TOP_K, SC_TOKENS), jnp.int32),
                       pltpu.VMEM((SC_TOKENS, width), U32),
                       pltpu.SemaphoreType.DMA],
        name="sc_scatter",
    )
    def body(dest_hbm, h_hbm, xs_hbm, idx_v, rows_v, sem):
        @pl.loop(0, steps)
        def _(j):
            base = _sc_token_base(steps, j)
            pltpu.sync_copy(dest_hbm.at[:, pl.ds(base, SC_TOKENS)], idx_v)
            pltpu.sync_copy(h_hbm.at[pl.ds(base, SC_TOKENS)], rows_v)
            for k0 in range(0, TOP_K, SC_SCATTER_BATCH):
                copies = [pltpu.async_copy(rows_v, xs_hbm.at[idx_v.at[k]], sem)
                          for k in range(k0, k0 + SC_SCATTER_BATCH)]
                for cp in copies:
                    cp.wait()

    return body(dest_kt, h2p)


def _sc_gather(dest_kt, ys, token0, n_tokens):
    n_k = dest_kt.shape[0]
    width = ys.shape[1]
    steps = n_tokens // (SC_WORKERS * SC_TOKENS)

    half = SC_TOKENS // 2
    units = [(k, h) for k in range(n_k) for h in range(2)]
    n_buf = 2

    @functools.partial(
        pl.kernel, mesh=_sc_mesh(),
        out_type=jax.ShapeDtypeStruct((n_k, n_tokens, width), U32),
        scratch_types=[pltpu.VMEM((n_k, SC_TOKENS), jnp.int32),
                       pltpu.VMEM((n_buf, half, width), U32),
                       pltpu.SemaphoreType.DMA((n_buf,))],
        name="sc_gather",
    )
    def body(dest_hbm, ys_hbm, yk_hbm, idx_v, rows_v, sems):
        @pl.loop(0, steps)
        def _(j):
            base = _sc_token_base(steps, j)
            pltpu.sync_copy(dest_hbm.at[:, pl.ds(token0 + base, SC_TOKENS)], idx_v)

            def gather(u):
                k, h = units[u]
                slot = u % n_buf
                return pltpu.make_async_copy(ys_hbm.at[idx_v.at[k, pl.ds(h * half, half)]], rows_v.at[slot],
                                             sems.at[slot])

            ahead = n_buf - 1
            for u in range(ahead):
                gather(u).start()
            for u, (k, h) in enumerate(units):
                gather(u).wait()
                pltpu.sync_copy(rows_v.at[u % n_buf], yk_hbm.at[k, pl.ds(base + h * half, half)])
                if u + ahead < len(units):
                    gather(u + ahead).start()

    return body(dest_kt, ys)


def _experts_kernel(be_ref, valid_ref, nact_ref, xs_hbm, wg_ref, wu_ref, wd_ref, ys_ref, wg_bf, wu_bf, wd_bf,
                    xbuf, sems):
    i = pl.program_id(0)
    n_active = nact_ref[0]
    slot = lax.rem(i, EXP_SLOTS)

    def fetch(j):
        src = xs_hbm.at[pl.ds(pl.multiple_of(j * TM_EXP, TM_EXP), TM_EXP), :]
        s = lax.rem(j, EXP_SLOTS)
        return pltpu.make_async_copy(src, xbuf.at[s], sems.at[s])

    @pl.when(i == 0)
    def _():
        for j in range(EXP_SLOTS - 1):
            pl.when(j < n_active)(lambda j=j: fetch(j).start())

    @pl.when(i + (EXP_SLOTS - 1) < n_active)
    def _():
        fetch(i + (EXP_SLOTS - 1)).start()

    prev = be_ref[jnp.maximum(i - 1, 0)]

    @pl.when((i == 0) | (be_ref[i] != prev))
    def _():
        wg_bf[...] = wg_ref[0].astype(BF16)
        wu_bf[...] = wu_ref[0].astype(BF16)
        wd_bf[...] = wd_ref[0].astype(BF16)

    @pl.when(i < n_active)
    def _():
        fetch(i).wait()

    valid = jnp.where(i < n_active, valid_ref[i], 0)

    def run(rows):
        row = lax.broadcasted_iota(jnp.int32, (rows, xbuf.shape[2]), 0)
        words = jnp.where(row < valid, xbuf[slot, 0:rows, :], jnp.uint32(0))
        xb = _unpack_words(words).astype(BF16)
        hid = _silu(_dot(xb, wg_bf[...])) * _dot(xb, wu_bf[...])
        ys_ref[0:rows, :] = _pack_words(_dot(hid.astype(BF16), wd_bf[...]))
        if rows < TM_EXP:
            ys_ref[rows:, :] = jnp.zeros((TM_EXP - rows, ys_ref.shape[1]), U32)

    for rows in range(EXP_MIN_ROWS, TM_EXP + 1, EXP_MIN_ROWS):
        pl.when((valid > rows - EXP_MIN_ROWS) & (valid <= rows))(functools.partial(run, rows))

    @pl.when(valid == 0)
    def _():
        ys_ref[...] = jnp.zeros_like(ys_ref)


def _experts(block_e, block_valid, n_active, xs, w_gate, w_up, w_down):
    n_rows, half = xs.shape
    d = w_gate.shape[1]
    nb = n_rows // TM_EXP
    grid_spec = pltpu.PrefetchScalarGridSpec(
        num_scalar_prefetch=3,
        grid=(nb,),
        in_specs=[pl.BlockSpec(memory_space=pl.ANY),
                  pl.BlockSpec((1, d, D_EXPERT), lambda i, be, bv, na: (be[i], 0, 0)),
                  pl.BlockSpec((1, d, D_EXPERT), lambda i, be, bv, na: (be[i], 0, 0)),
                  pl.BlockSpec((1, D_EXPERT, d), lambda i, be, bv, na: (be[i], 0, 0))],
        out_specs=pl.BlockSpec((TM_EXP, half), lambda i, be, bv, na: (i, 0)),
        scratch_shapes=[pltpu.VMEM((d, D_EXPERT), BF16), pltpu.VMEM((d, D_EXPERT), BF16),
                        pltpu.VMEM((D_EXPERT, d), BF16),
                        pltpu.VMEM((EXP_SLOTS, TM_EXP, half), U32),
                        pltpu.SemaphoreType.DMA((EXP_SLOTS,))],
    )
    return pl.pallas_call(
        _experts_kernel,
        out_shape=jax.ShapeDtypeStruct((n_rows, half), U32),
        grid_spec=grid_spec,
        compiler_params=pltpu.CompilerParams(vmem_limit_bytes=VMEM_LIMIT, dimension_semantics=("arbitrary",)),
        name="experts",
    )(block_e, block_valid, n_active, xs, w_gate, w_up, w_down)


def _combine_kernel(yk_ref, x1_ref, h2_ref, gate_ref, mod_ref, wsg_ref, wsu_ref, wsd_ref, fg_ref, *out_refs):
    out_ref = out_refs[-1]
    hb = _unpack_words(h2_ref[...]).astype(BF16)
    hid = _silu(_dot(hb, wsg_ref[...])) * _dot(hb, wsu_ref[...])
    ffn = _dot(hid.astype(BF16), wsd_ref[...])
    gate = gate_ref[...]
    for k in range(TOP_K):
        ffn = ffn + gate[:, k:k + 1] * _unpack_words(yk_ref[k])
    x2 = x1_ref[...] + mod_ref[0, 5:6, :] * ffn
    ms = jnp.mean(x2 * x2, axis=-1, keepdims=True)
    out_ref[...] = x2 * lax.rsqrt(ms + EPS) * fg_ref[...]


def _combine(yk, token0, out_token0, n_out_tokens, prev_out, x1, h2p, gate_tk, mod, wsg_bf, wsu_bf, wsd_bf, final_g,
             seq_len):
    t, d = n_out_tokens, x1.shape[1]
    tiles_per_seq = seq_len // TF
    tile0 = token0 // TF
    out_tile0 = out_token0 // TF
    in_tok = pl.BlockSpec((TF, d), lambda i: (tile0 + i, 0))
    tok = pl.BlockSpec((TF, d), lambda i: (out_tile0 + i, 0))
    in_specs = [pl.BlockSpec((TOP_K, TF, d // 2), lambda i: (0, i, 0)),
                in_tok, pl.BlockSpec((TF, d // 2), lambda i: (tile0 + i, 0)),
                pl.BlockSpec((TF, TOP_K), lambda i: (tile0 + i, 0)),
                pl.BlockSpec((1, 6, d), lambda i: ((out_tile0 + i) // tiles_per_seq, 0, 0)),
                pl.BlockSpec((d, D_SHARED), lambda i: (0, 0)),
                pl.BlockSpec((d, D_SHARED), lambda i: (0, 0)),
                pl.BlockSpec((D_SHARED, d), lambda i: (0, 0)),
                pl.BlockSpec((1, d), lambda i: (0, 0))]
    args = [yk, x1, h2p, gate_tk, mod, wsg_bf, wsu_bf, wsd_bf, final_g]
    aliases = {}
    if prev_out is not None:
        in_specs.append(pl.BlockSpec(memory_space=pl.ANY))
        args.append(prev_out)
        aliases = {len(args) - 1: 0}
    return pl.pallas_call(
        _combine_kernel,
        out_shape=jax.ShapeDtypeStruct((t, d), F32),
        grid=(yk.shape[1] // TF,),
        in_specs=in_specs,
        out_specs=tok,
        input_output_aliases=aliases,
        compiler_params=pltpu.CompilerParams(vmem_limit_bytes=VMEM_LIMIT),
        name="combine",
    )(*args)


def _rope_tables(n_tokens):
    n_rows = n_tokens // GRID_W
    n_freq = HEAD_DIM // 4
    inv_freq = ROPE_THETA ** (-jnp.arange(n_freq, dtype=F32) / n_freq)
    ang_r = jnp.arange(n_rows).astype(F32)[:, None] * inv_freq[None, :]
    ang_c = jnp.arange(GRID_W).astype(F32)[:, None] * inv_freq[None, :]

    def per_token(row_part, col_part):
        rows = jnp.broadcast_to(row_part[:, None, :], (n_rows, GRID_W, n_freq))
        cols = jnp.broadcast_to(col_part[None, :, :], (n_rows, GRID_W, n_freq))
        return rows.reshape(n_tokens, n_freq), cols.reshape(n_tokens, n_freq)

    cos_r, cos_c = per_token(jnp.cos(ang_r), jnp.cos(ang_c))
    sin_r, sin_c = per_token(jnp.sin(ang_r), jnp.sin(ang_c))
    cos = jnp.concatenate([cos_r, cos_r, cos_c, cos_c], axis=1)
    sin = jnp.concatenate([-sin_r, sin_r, -sin_c, sin_c], axis=1)
    reps = LANES // HEAD_DIM
    return jnp.tile(cos, (1, reps)), jnp.tile(sin, (1, reps))


def _pool_bands():
    i = jnp.arange(QB)[:, None]
    r = jnp.arange(POOL_SLAB)[None, :]
    return jnp.stack([((r >= i + POOL_OFF - w // 2) & (r < i + POOL_OFF + w // 2)).astype(BF16)
                      for w in POOL_WINDOWS])


def kernel(x, c, ctx, c_ctx, w_ada, b_ada, norm1_g, norm2_g, w_in, attn_sink, pool_w, pool_scale, w_out,
           w_router, router_bias, w_gate, w_up, w_down, ws_gate, ws_up, ws_down, final_g):
    b, s, d = x.shape
    t = b * s
    assert w_ada.shape[0] == 1 and d == D_MODEL and s % TQ == 0 and b + 1 <= 8

    c8 = jnp.zeros((8, d), F32).at[:b].set(c).at[b].set(c_ctx)
    mod = _ada(c8, w_ada[0], b_ada[0]).reshape(8, 6, d)
    g1 = norm1_g[0].reshape(1, d)
    g2 = norm2_g[0].reshape(1, d)
    w_in_bf = w_in[0].astype(BF16)
    cos_t, sin_t = _rope_tables(s)

    q, k4, v4, p = _inproj(x, mod, g1, w_in_bf, cos_t, sin_t)
    kc4, vc4 = _ctxproj(ctx, mod[b:b + 1], g1, w_in_bf[:, ATTN_WIDTH:ATTN_WIDTH + 2 * KV_WIDTH])

    wr_t = w_router[0].T
    wr_hi = wr_t.astype(BF16)
    wr_lo = (wr_t - wr_hi.astype(F32)).astype(BF16)
    attn_consts = (_pool_bands(), pool_w[0].astype(BF16), pool_scale[0].reshape(1, POOL_WIDTH),
                   w_out[0].astype(BF16), g2, wr_hi, wr_lo)
    tri = jnp.triu(jnp.ones((TR, TR), BF16), k=1)
    shared_w = (ws_gate[0].astype(BF16), ws_up[0].astype(BF16), ws_down[0].astype(BF16))

    nb = b // TOKEN_GROUPS
    tg = nb * s
    assert b % TOKEN_GROUPS == 0 and tg % (COMBINE_CHUNKS * SC_WORKERS * SC_TOKENS) == 0
    n_rows = -(-(tg * TOP_K + N_EXPERTS * (TM_EXP - 1)) // TM_EXP) * TM_EXP
    groups = []
    for g in range(TOKEN_GROUPS):
        x1, h2p, lg_t = _attn(g * nb, nb, attn_sink[0], q, k4, v4, kc4, vc4, p, x, mod, *attn_consts)
        idx_kt, gate_kt, rank_kt, counts = _route(lg_t, router_bias[0].reshape(N_EXPERTS, 1), tri)
        pad_start, block_e, block_valid, n_active = _plan_blocks(counts[:, 0].astype(jnp.int32), n_rows // TM_EXP)
        dest_kt = _dest_rows(pad_start, idx_kt, rank_kt)
        xs = _sc_scatter(dest_kt, h2p, n_rows)
        groups.append((x1.reshape(tg, d), h2p, gate_kt.T, dest_kt, xs, block_e, block_valid, n_active))

    out = None
    chunk = tg // COMBINE_CHUNKS
    for g, (x1, h2p, gate_tk, dest_kt, xs, block_e, block_valid, n_active) in enumerate(groups):
        ys = _experts(block_e, block_valid, n_active, xs, w_gate[0], w_up[0], w_down[0])
        for token0 in range(0, tg, chunk):
            out = _combine(_sc_gather(dest_kt, ys, token0, chunk), token0, g * tg + token0, t, out, x1, h2p,
                           gate_tk, mod, *shared_w, final_g.reshape(1, d), s)
    return out.reshape(b, s, d)
```

```python
import functools

import jax
import jax.numpy as jnp
from jax import lax
from jax.experimental import pallas as pl
from jax.experimental.pallas import tpu as pltpu
from jax.experimental.pallas import tpu_sc as plsc

F32 = jnp.float32
BF16 = jnp.bfloat16

D_MODEL = 1024
GRID_W = 64
N_HEADS = 8
N_KV_HEADS = 2
HEAD_DIM = 64
ATTN_WIDTH = N_HEADS * HEAD_DIM
KV_WIDTH = N_KV_HEADS * HEAD_DIM
WINDOW = 128
ROPE_THETA = 10000.0
POOL_WINDOWS = (2, 4, 8, 16)
POOL_WIDTH = D_MODEL - ATTN_WIDTH
POOL_GROUP_DIM = POOL_WIDTH // len(POOL_WINDOWS)
IN_COLS = ATTN_WIDTH + 2 * KV_WIDTH + POOL_WIDTH
N_EXPERTS = 64
TOP_K = 8
N_EXPERT_GROUPS = 8
EXPERTS_PER_GROUP = N_EXPERTS // N_EXPERT_GROUPS
TOPK_GROUPS = 4
D_EXPERT = 256
D_SHARED = 256
ROUTED_SCALE = 2.5
EPS = 1e-6
LOG2E = 1.4426950408889634

LANES = 128
U32 = jnp.uint32
VMEM_LIMIT = 48 * 1024 * 1024

TM_PROJ = 1024
TQ = 1024
QB = 128
POOL_SLAB = 256
POOL_OFF = 64
TR = 512
TM_EXP = 1024
EXP_MIN_ROWS = 256
EXP_SLOTS = 4
TF = 512
TOKEN_GROUPS = 2
COMBINE_CHUNKS = 2
SC_CORES = 2
SC_WORKERS = 32
SC_TOKENS = 128


def _silu(x):
    return x * (1.0 / (1.0 + jnp.exp(-x)))


def _split_bf16(x):
    hi = x.astype(BF16)
    lo = (x - hi.astype(F32)).astype(BF16)
    return hi, lo


def _dot(a, b):
    return jnp.dot(a, b, preferred_element_type=F32)


def _pack_words(val):
    half = val.shape[1] // 2
    lo = lax.bitcast_convert_type(val[:, :half].astype(BF16).astype(F32), U32)
    hi = lax.bitcast_convert_type(val[:, half:].astype(BF16).astype(F32), U32)
    return lax.shift_right_logical(lo, jnp.uint32(16)) | hi


def _unpack_words(words):
    lo = lax.bitcast_convert_type(lax.shift_left(words, jnp.uint32(16)), F32)
    hi = lax.bitcast_convert_type(words & jnp.uint32(0xFFFF0000), F32)
    return jnp.concatenate([lo, hi], axis=1)


def _dot_nt(a, b):
    return lax.dot_general(a, b, (((1,), (1,)), ((), ())), preferred_element_type=F32)


def _ada_kernel(c_ref, w_ref, b_ref, o_ref):
    a_hi, a_lo = _split_bf16(_silu(c_ref[...]))
    w_hi, w_lo = _split_bf16(w_ref[...])
    o_ref[...] = _dot(a_hi, w_hi) + _dot(a_lo, w_hi) + _dot(a_hi, w_lo) + b_ref[...]


def _ada(c8, w_ada, b_ada):
    d = c8.shape[1]
    n = w_ada.shape[1]
    tn = 512
    return pl.pallas_call(
        _ada_kernel,
        out_shape=jax.ShapeDtypeStruct((8, n), F32),
        grid=(n // tn,),
        in_specs=[pl.BlockSpec((8, d), lambda j: (0, 0)),
                  pl.BlockSpec((d, tn), lambda j: (0, j)),
                  pl.BlockSpec((1, tn), lambda j: (0, j))],
        out_specs=pl.BlockSpec((8, tn), lambda j: (0, j)),
        compiler_params=pltpu.CompilerParams(vmem_limit_bytes=VMEM_LIMIT),
        name="ada",
    )(c8, w_ada, b_ada.reshape(1, n))


def _norm_mod(x, g, shift, scale):
    ms = jnp.mean(x * x, axis=-1, keepdims=True)
    return (x * lax.rsqrt(ms + EPS) * g) * (1.0 + scale) + shift


def _lane_variants(t):
    lane = lax.broadcasted_iota(jnp.int32, t.shape, 1)
    lo = lane < HEAD_DIM
    tr = pltpu.roll(t, HEAD_DIM, 1)
    zero = jnp.zeros_like(t)
    return (jnp.where(lo, t, zero), jnp.where(lo, zero, tr),
            jnp.where(lo, tr, zero), jnp.where(lo, zero, t))


def _store_variants(ref, t):
    for i, var in enumerate(_lane_variants(t)):
        ref[0, :, i * LANES:(i + 1) * LANES] = var.astype(BF16)


def _inproj_kernel(x_ref, mod_ref, g_ref, w_ref, cos_ref, sin_ref, q_ref, k_ref, v_ref, p_ref):
    h = _norm_mod(x_ref[0], g_ref[...], mod_ref[0, 0:1, :], mod_ref[0, 1:2, :])
    z = _dot(h.astype(BF16), w_ref[...])
    cos = cos_ref[...]
    sin = sin_ref[...]
    lane = lax.broadcasted_iota(jnp.int32, cos.shape, 1)
    first_half = (lane & 16) == 0

    def rope(zc):
        partner = jnp.where(first_half, pltpu.roll(zc, LANES - 16, 1), pltpu.roll(zc, 16, 1))
        return zc * cos + partner * sin

    scale = HEAD_DIM ** -0.5 * LOG2E
    for c in range(ATTN_WIDTH // LANES):
        q_ref[0, :, c * LANES:(c + 1) * LANES] = (rope(z[:, c * LANES:(c + 1) * LANES]) * scale).astype(BF16)
    _store_variants(k_ref, rope(z[:, ATTN_WIDTH:ATTN_WIDTH + KV_WIDTH]))
    _store_variants(v_ref, z[:, ATTN_WIDTH + KV_WIDTH:ATTN_WIDTH + 2 * KV_WIDTH])
    p_ref[0] = z[:, ATTN_WIDTH + 2 * KV_WIDTH:]


def _inproj(x, mod, g1, w_in_bf, cos_t, sin_t):
    b, s, d = x.shape
    tm = TM_PROJ
    return pl.pallas_call(
        _inproj_kernel,
        out_shape=(jax.ShapeDtypeStruct((b, s, ATTN_WIDTH), BF16),
                   jax.ShapeDtypeStruct((b, s, 4 * LANES), BF16),
                   jax.ShapeDtypeStruct((b, s, 4 * LANES), BF16),
                   jax.ShapeDtypeStruct((b, s, POOL_WIDTH), F32)),
        grid=(s // tm, b),
        in_specs=[pl.BlockSpec((1, tm, d), lambda n, bi: (bi, n, 0)),
                  pl.BlockSpec((1, 6, d), lambda n, bi: (bi, 0, 0)),
                  pl.BlockSpec((1, d), lambda n, bi: (0, 0)),
                  pl.BlockSpec((d, IN_COLS), lambda n, bi: (0, 0)),
                  pl.BlockSpec((tm, LANES), lambda n, bi: (n, 0)),
                  pl.BlockSpec((tm, LANES), lambda n, bi: (n, 0))],
        out_specs=(pl.BlockSpec((1, tm, ATTN_WIDTH), lambda n, bi: (bi, n, 0)),
                   pl.BlockSpec((1, tm, 4 * LANES), lambda n, bi: (bi, n, 0)),
                   pl.BlockSpec((1, tm, 4 * LANES), lambda n, bi: (bi, n, 0)),
                   pl.BlockSpec((1, tm, POOL_WIDTH), lambda n, bi: (bi, n, 0))),
        compiler_params=pltpu.CompilerParams(vmem_limit_bytes=VMEM_LIMIT),
        name="inproj",
    )(x, mod, g1, w_in_bf, cos_t, sin_t)


def _ctxproj_kernel(x_ref, mod_ref, g_ref, w_ref, k_ref, v_ref):
    h = _norm_mod(x_ref[0], g_ref[...], mod_ref[0, 0:1, :], mod_ref[0, 1:2, :])
    z = _dot(h.astype(BF16), w_ref[...])
    _store_variants(k_ref, z[:, :KV_WIDTH])
    _store_variants(v_ref, z[:, KV_WIDTH:])


def _ctxproj(ctx, mod_c, g1, w_kv_bf):
    b, c, d = ctx.shape
    return pl.pallas_call(
        _ctxproj_kernel,
        out_shape=(jax.ShapeDtypeStruct((b, c, 4 * LANES), BF16),
                   jax.ShapeDtypeStruct((b, c, 4 * LANES), BF16)),
        grid=(b,),
        in_specs=[pl.BlockSpec((1, c, d), lambda bi: (bi, 0, 0)),
                  pl.BlockSpec((1, 6, d), lambda bi: (0, 0, 0)),
                  pl.BlockSpec((1, d), lambda bi: (0, 0)),
                  pl.BlockSpec((d, 2 * KV_WIDTH), lambda bi: (0, 0))],
        out_specs=(pl.BlockSpec((1, c, 4 * LANES), lambda bi: (bi, 0, 0)),
                   pl.BlockSpec((1, c, 4 * LANES), lambda bi: (bi, 0, 0))),
        compiler_params=pltpu.CompilerParams(vmem_limit_bytes=VMEM_LIMIT),
        name="ctxproj",
    )(ctx, mod_c, g1, w_kv_bf)


def _fold(op, tiles):
    while len(tiles) > 1:
        tiles = [op(tiles[i], tiles[i + 1]) if i + 1 < len(tiles) else tiles[i] for i in range(0, len(tiles), 2)]
    return tiles[0]


def _stack_variants(t4, kv):
    return jnp.concatenate([t4[:, (2 * kv) * LANES:(2 * kv + 1) * LANES],
                            t4[:, (2 * kv + 1) * LANES:(2 * kv + 2) * LANES]], axis=0)


def _attn_kernel(seq_len, sink_ref, q_ref, k_ref, kp_ref, kn_ref, v_ref, vp_ref, vn_ref, kc_ref, vc_ref,
                 p_ref, pp_ref, pn_ref, x_ref, mod_ref, band_ref, poolw_ref, pscale_ref, wout_ref,
                 g2_ref, wrh_ref, wrl_ref, x1_ref, h2_ref, lg_ref, kwin, vwin, pext, mix, s_scr, p_scr, m_scr):
    n = pl.program_id(1)
    n_last = pl.num_programs(1) - 1

    kwin[0:QB, :] = kp_ref[0]
    kwin[QB:QB + TQ, :] = k_ref[0]
    kwin[QB + TQ:, :] = kn_ref[0]
    vwin[0:QB, :] = vp_ref[0]
    vwin[QB:QB + TQ, :] = v_ref[0]
    vwin[QB + TQ:, :] = vn_ref[0]

    pext[0:QB - 8, :] = jnp.zeros((QB - 8, POOL_WIDTH), F32)
    pext[QB - 8:QB, :] = jnp.where(n > 0, pp_ref[0], 0.0)
    pext[QB:QB + TQ, :] = p_ref[0]
    pext[QB + TQ:QB + TQ + 8, :] = jnp.where(n < n_last, pn_ref[0], 0.0)
    pext[QB + TQ + 8:, :] = jnp.zeros((QB - 8, POOL_WIDTH), F32)

    row = lax.broadcasted_iota(jnp.int32, (QB, 3 * QB), 0)
    col = lax.broadcasted_iota(jnp.int32, (QB, 3 * QB), 1)
    in_band = (col >= row) & (col <= row + 2 * WINDOW)
    tok = lax.broadcasted_iota(jnp.int32, (QB, 1), 0)
    kc = kc_ref[0]
    kc_rows = [_stack_variants(kc, kv) for kv in range(N_KV_HEADS)]
    vc_rows = [_stack_variants(vc_ref[0], kv) for kv in range(N_KV_HEADS)]

    def sub_block(j, carry):
        r0 = pl.multiple_of(j * QB, QB)
        qj = q_ref[0, pl.ds(r0, QB), :]
        kw = kwin[pl.ds(r0, 3 * QB), :]
        vw = vwin[pl.ds(r0, 3 * QB), :]
        kpos = col + (n * TQ + j * QB - QB)
        ok = in_band & (kpos >= 0) & (kpos < seq_len)
        bias = jnp.where(ok, 0.0, -jnp.inf)
        n_loc, n_ctx = 3 * QB, kc.shape[0]
        bias2 = jnp.concatenate([bias, bias], axis=1)
        k_rows = [_stack_variants(kw, kv) for kv in range(N_KV_HEADS)]
        v_rows = [_stack_variants(vw, kv) for kv in range(N_KV_HEADS)]
        group = N_HEADS // N_KV_HEADS

        def head_tiles(head):
            loc0 = (head % 2) * n_loc
            ctx0 = 2 * n_loc + (head % 2) * n_ctx
            return ([loc0 + i * LANES for i in range(n_loc // LANES)]
                    + [ctx0 + i * LANES for i in range(n_ctx // LANES)])

        for c in range(N_HEADS // 2):
            qc = qj[:, c * LANES:(c + 1) * LANES]
            s_scr[c, :, 0:2 * n_loc] = _dot_nt(qc, k_rows[2 * c // group]) + bias2
            s_scr[c, :, 2 * n_loc:] = _dot_nt(qc, kc_rows[2 * c // group])
        for head in range(N_HEADS):
            tiles = [s_scr[head // 2, :, st:st + LANES] for st in head_tiles(head)]
            row_max = jnp.max(_fold(jnp.maximum, tiles), axis=1, keepdims=True)
            m_scr[head] = jnp.broadcast_to(jnp.maximum(row_max, sink_ref[head] * LOG2E), (QB, LANES))
        for head in range(N_HEADS):
            m = m_scr[head]
            acc = None
            for st in head_tiles(head):
                p = jnp.exp2(s_scr[head // 2, :, st:st + LANES] - m)
                p_scr[head // 2, :, st:st + LANES] = p.astype(BF16)
                acc = p if acc is None else acc + p
            denom = (jnp.broadcast_to(jnp.sum(acc, axis=1, keepdims=True), (QB, LANES))
                     + jnp.exp2(sink_ref[head] * LOG2E - m))
            m_scr[head] = 1.0 / denom
        lane = lax.broadcasted_iota(jnp.int32, (QB, LANES), 1)
        for c in range(N_HEADS // 2):
            o = (_dot(p_scr[c, :, 0:2 * n_loc], v_rows[2 * c // group])
                 + _dot(p_scr[c, :, 2 * n_loc:], vc_rows[2 * c // group]))
            o = o * jnp.where(lane < HEAD_DIM, m_scr[2 * c], m_scr[2 * c + 1])
            mix[pl.ds(r0, QB), c * LANES:(c + 1) * LANES] = o.astype(BF16)

        slab = pext[pl.ds(pl.multiple_of(r0 + POOL_OFF, 8), POOL_SLAB), :]
        tpos = tok + (n * TQ + j * QB)
        for g, w in enumerate(POOL_WINDOWS):
            sg = slab[:, g * LANES:(g + 1) * LANES]
            hi, lo = _split_bf16(sg)
            band = band_ref[g]
            wsum = _dot(band, hi) + _dot(band, lo)
            cnt = (jnp.minimum(tpos - w // 2 + w, seq_len) - jnp.maximum(tpos - w // 2, 0)).astype(F32)
            dlt = wsum / cnt - sg[POOL_OFF:POOL_OFF + QB, :]
            y = _dot(dlt.astype(BF16), poolw_ref[g]) * pscale_ref[:, g * LANES:(g + 1) * LANES]
            mix[pl.ds(r0, QB), ATTN_WIDTH + g * LANES:ATTN_WIDTH + (g + 1) * LANES] = y.astype(BF16)
        return carry

    lax.fori_loop(0, TQ // QB, sub_block, 0)

    proj = _dot(mix[...], wout_ref[...])
    x1 = x_ref[0] + mod_ref[0, 2:3, :] * proj
    x1_ref[0] = x1
    h2 = _norm_mod(x1, g2_ref[...], mod_ref[0, 3:4, :], mod_ref[0, 4:5, :])
    h2_ref[...] = _pack_words(h2)
    h_hi, h_lo = _split_bf16(h2)
    wrh = wrh_ref[...]
    lg_ref[...] = _dot_nt(wrh, h_hi) + _dot_nt(wrh, h_lo) + _dot_nt(wrl_ref[...], h_hi)


def _attn(b0, b, sink, q, k4, v4, kc4, vc4, p, x, mod, band, poolw_bf, pscale, wout_bf, g2, wr_hi, wr_lo):
    _, s, d = x.shape
    c = kc4.shape[1]
    nt = s // TQ
    hb = TQ // QB
    pb = TQ // 8
    kv_main = pl.BlockSpec((1, TQ, 4 * LANES), lambda bi, n: (b0 + bi, n, 0))
    kv_prev = pl.BlockSpec((1, QB, 4 * LANES), lambda bi, n: (b0 + bi, jnp.maximum(n * hb - 1, 0), 0))
    kv_next = pl.BlockSpec((1, QB, 4 * LANES), lambda bi, n: (b0 + bi, jnp.minimum((n + 1) * hb, s // QB - 1), 0))
    const2 = lambda bi, n: (0, 0)
    const3 = lambda bi, n: (0, 0, 0)
    return pl.pallas_call(
        functools.partial(_attn_kernel, s),
        out_shape=(jax.ShapeDtypeStruct((b, s, d), F32),
                   jax.ShapeDtypeStruct((b * s, d // 2), U32),
                   jax.ShapeDtypeStruct((N_EXPERTS, b * s), F32)),
        grid=(b, nt),
        in_specs=[pl.BlockSpec(memory_space=pltpu.SMEM),
                  pl.BlockSpec((1, TQ, ATTN_WIDTH), lambda bi, n: (b0 + bi, n, 0)),
                  kv_main, kv_prev, kv_next, kv_main, kv_prev, kv_next,
                  pl.BlockSpec((1, c, 4 * LANES), lambda bi, n: (b0 + bi, 0, 0)),
                  pl.BlockSpec((1, c, 4 * LANES), lambda bi, n: (b0 + bi, 0, 0)),
                  pl.BlockSpec((1, TQ, POOL_WIDTH), lambda bi, n: (b0 + bi, n, 0)),
                  pl.BlockSpec((1, 8, POOL_WIDTH), lambda bi, n: (b0 + bi, jnp.maximum(n * pb - 1, 0), 0)),
                  pl.BlockSpec((1, 8, POOL_WIDTH),
                               lambda bi, n: (b0 + bi, jnp.minimum((n + 1) * pb, s // 8 - 1), 0)),
                  pl.BlockSpec((1, TQ, d), lambda bi, n: (b0 + bi, n, 0)),
                  pl.BlockSpec((1, 6, d), lambda bi, n: (b0 + bi, 0, 0)),
                  pl.BlockSpec((len(POOL_WINDOWS), QB, POOL_SLAB), const3),
                  pl.BlockSpec((len(POOL_WINDOWS), POOL_GROUP_DIM, POOL_GROUP_DIM), const3),
                  pl.BlockSpec((1, POOL_WIDTH), const2),
                  pl.BlockSpec((d, d), const2),
                  pl.BlockSpec((1, d), const2),
                  pl.BlockSpec((N_EXPERTS, d), const2),
                  pl.BlockSpec((N_EXPERTS, d), const2)],
        out_specs=(pl.BlockSpec((1, TQ, d), lambda bi, n: (bi, n, 0)),
                   pl.BlockSpec((TQ, d // 2), lambda bi, n: (bi * nt + n, 0)),
                   pl.BlockSpec((N_EXPERTS, TQ), lambda bi, n: (0, bi * nt + n))),
        scratch_shapes=[pltpu.VMEM((TQ + 2 * QB, 4 * LANES), BF16),
                        pltpu.VMEM((TQ + 2 * QB, 4 * LANES), BF16),
                        pltpu.VMEM((TQ + 2 * QB, POOL_WIDTH), F32),
                        pltpu.VMEM((TQ, d), BF16),
                        pltpu.VMEM((N_HEADS // 2, QB, 2 * (3 * QB + c)), F32),
                        pltpu.VMEM((N_HEADS // 2, QB, 2 * (3 * QB + c)), BF16),
                        pltpu.VMEM((N_HEADS, QB, LANES), F32)],
        compiler_params=pltpu.CompilerParams(vmem_limit_bytes=VMEM_LIMIT),
        name="attn",
    )(sink, q, k4, k4, k4, v4, v4, v4, kc4, vc4, p, p, p, x, mod, band, poolw_bf, pscale, wout_bf,
      g2, wr_hi, wr_lo)


def _first_argmax_rows(v, row_iota, n_rows):
    m = jnp.max(v, axis=0, keepdims=True)
    idx = jnp.min(jnp.where(v == m, row_iota, n_rows), axis=0, keepdims=True)
    return m, idx


def _route_kernel(lg_ref, bias_ref, tri_ref, idx_ref, gate_ref, rank_ref, cnt_ref, carry):
    i = pl.program_id(0)

    @pl.when(i == 0)
    def _():
        carry[...] = jnp.zeros_like(carry)

    scores = 1.0 / (1.0 + jnp.exp(-lg_ref[...]))
    biased = scores + bias_ref[...]
    e_iota = lax.broadcasted_iota(jnp.int32, scores.shape, 0).astype(F32)
    g_iota = lax.broadcasted_iota(jnp.int32, (EXPERTS_PER_GROUP, TR), 0).astype(F32)
    neg = -jnp.inf

    grp = []
    for g in range(N_EXPERT_GROUPS):
        blk = biased[g * EXPERTS_PER_GROUP:(g + 1) * EXPERTS_PER_GROUP, :]
        m1, i1 = _first_argmax_rows(blk, g_iota, float(EXPERTS_PER_GROUP))
        m2 = jnp.max(jnp.where(g_iota == i1, neg, blk), axis=0, keepdims=True)
        grp.append(m1 + m2)
    grp = jnp.concatenate(grp, axis=0)
    gg_iota = lax.broadcasted_iota(jnp.int32, grp.shape, 0).astype(F32)
    grp_sel = jnp.zeros(grp.shape, F32)
    for _ in range(TOPK_GROUPS):
        _, gi = _first_argmax_rows(grp, gg_iota, float(N_EXPERT_GROUPS))
        hit = gg_iota == gi
        grp_sel = jnp.where(hit, 1.0, grp_sel)
        grp = jnp.where(hit, neg, grp)
    allowed = jnp.concatenate(
        [jnp.broadcast_to(grp_sel[g:g + 1, :], (EXPERTS_PER_GROUP, TR)) for g in range(N_EXPERT_GROUPS)], axis=0)
    masked = jnp.where(allowed > 0.5, biased, neg)

    idxs, gates = [], []
    onehot = jnp.zeros(scores.shape, F32)
    for _ in range(TOP_K):
        _, ei = _first_argmax_rows(masked, e_iota, float(N_EXPERTS))
        hit = e_iota == ei
        idxs.append(ei)
        gates.append(jnp.sum(jnp.where(hit, scores, 0.0), axis=0, keepdims=True))
        onehot = jnp.where(hit, 1.0, onehot)
        masked = jnp.where(hit, neg, masked)
    idx = jnp.concatenate(idxs, axis=0)
    gate = jnp.concatenate(gates, axis=0)
    gate = gate / jnp.sum(gate, axis=0, keepdims=True) * ROUTED_SCALE

    before = _dot(onehot.astype(BF16), tri_ref[...]) + carry[:, 0:1]
    ranks = [jnp.sum(jnp.where(e_iota == idxs[k], before, 0.0), axis=0, keepdims=True) for k in range(TOP_K)]
    idx_ref[...] = idx.astype(jnp.int32)
    gate_ref[...] = gate
    rank_ref[...] = jnp.concatenate(ranks, axis=0).astype(jnp.int32)
    total = carry[...] + jnp.sum(onehot, axis=1, keepdims=True)
    carry[...] = total
    cnt_ref[...] = total


def _route(lg_t, bias, tri):
    e, t = lg_t.shape
    tok = pl.BlockSpec((TOP_K, TR), lambda i: (0, i))
    return pl.pallas_call(
        _route_kernel,
        out_shape=(jax.ShapeDtypeStruct((TOP_K, t), jnp.int32),
                   jax.ShapeDtypeStruct((TOP_K, t), F32),
                   jax.ShapeDtypeStruct((TOP_K, t), jnp.int32),
                   jax.ShapeDtypeStruct((e, LANES), F32)),
        grid=(t // TR,),
        in_specs=[pl.BlockSpec((e, TR), lambda i: (0, i)),
                  pl.BlockSpec((e, 1), lambda i: (0, 0)),
                  pl.BlockSpec((TR, TR), lambda i: (0, 0))],
        out_specs=(tok, tok, tok, pl.BlockSpec((e, LANES), lambda i: (0, 0))),
        scratch_shapes=[pltpu.VMEM((e, LANES), F32)],
        compiler_params=pltpu.CompilerParams(vmem_limit_bytes=VMEM_LIMIT),
        name="route",
    )(lg_t, bias, tri)


def _plan_kernel(n_blocks, size_ref, start_ref, expert_ref, valid_ref, nact_ref):
    def per_expert(e, first_block):
        size = size_ref[e]
        n_blk = (size + TM_EXP - 1) // TM_EXP
        start_ref[e] = first_block * TM_EXP

        def per_block(j, carry):
            expert_ref[first_block + j] = e
            valid_ref[first_block + j] = jnp.minimum(size - j * TM_EXP, TM_EXP)
            return carry

        lax.fori_loop(0, n_blk, per_block, 0)
        return first_block + n_blk

    n_active = lax.fori_loop(0, N_EXPERTS, per_expert, 0)
    nact_ref[0] = n_active

    def unused(i, carry):
        expert_ref[i] = N_EXPERTS - 1
        valid_ref[i] = 0
        return carry

    lax.fori_loop(n_active, n_blocks, unused, 0)


def _plan_blocks(sizes, n_blocks):
    smem = pl.BlockSpec(memory_space=pltpu.SMEM)
    return pl.pallas_call(
        functools.partial(_plan_kernel, n_blocks),
        out_shape=(jax.ShapeDtypeStruct((N_EXPERTS,), jnp.int32),
                   jax.ShapeDtypeStruct((n_blocks,), jnp.int32),
                   jax.ShapeDtypeStruct((n_blocks,), jnp.int32),
                   jax.ShapeDtypeStruct((1,), jnp.int32)),
        in_specs=[smem],
        out_specs=(smem, smem, smem, smem),
        name="plan_blocks",
    )(sizes)


def _dest_kernel(start_ref, idx_ref, rank_ref, dest_ref):
    idx = idx_ref[...]
    dest = rank_ref[...]
    for e in range(N_EXPERTS):
        dest = dest + jnp.where(idx == e, start_ref[e], 0)
    dest_ref[...] = dest


def _dest_rows(pad_start, idx_kt, rank_kt):
    n_k, t = idx_kt.shape
    tile = 4096
    blk = pl.BlockSpec((n_k, tile), lambda i: (0, i))
    return pl.pallas_call(
        _dest_kernel,
        out_shape=jax.ShapeDtypeStruct((n_k, t), jnp.int32),
        grid=(t // tile,),
        in_specs=[pl.BlockSpec(memory_space=pltpu.SMEM), blk, blk],
        out_specs=blk,
        name="dest_rows",
    )(pad_start, idx_kt, rank_kt)


def _sc_mesh():
    return plsc.VectorSubcoreMesh(core_axis_name="c", subcore_axis_name="s")


def _sc_token_base(steps, j):
    worker = lax.axis_index("s") * SC_CORES + lax.axis_index("c")
    return (worker * steps + j) * SC_TOKENS


def _sc_scatter(dest_kt, h2p, n_rows):
    t, width = h2p.shape
    steps = t // (SC_WORKERS * SC_TOKENS)

    @functools.partial(
        pl.kernel, mesh=_sc_mesh(),
        out_type=jax.ShapeDtypeStruct((n_rows, width), U32),
        scratch_types=[pltpu.VMEM((TOP_K, SC_TOKENS), jnp.int32),
                       pltpu.VMEM((SC_TOKENS, width), U32),
                       pltpu.SemaphoreType.DMA],
        name="sc_scatter",
    )
    def body(dest_hbm, h_hbm, xs_hbm, idx_v, rows_v, sem):
        @pl.loop(0, steps)
        def _(j):
            base = _sc_token_base(steps, j)
            pltpu.sync_copy(dest_hbm.at[:, pl.ds(base, SC_TOKENS)], idx_v)
            pltpu.sync_copy(h_hbm.at[pl.ds(base, SC_TOKENS)], rows_v)
            copies = [pltpu.async_copy(rows_v, xs_hbm.at[idx_v.at[k]], sem) for k in range(TOP_K)]
            for cp in copies:
                cp.wait()

    return body(dest_kt, h2p)


def _sc_gather(dest_kt, ys, token0, n_tokens, n_buf):
    n_k = dest_kt.shape[0]
    width = ys.shape[1]
    steps = n_tokens // (SC_WORKERS * SC_TOKENS)

    half = SC_TOKENS // 2
    units = [(k, h) for k in range(n_k) for h in range(2)]

    @functools.partial(
        pl.kernel, mesh=_sc_mesh(),
        out_type=jax.ShapeDtypeStruct((n_k, n_tokens, width), U32),
        scratch_types=[pltpu.VMEM((n_k, SC_TOKENS), jnp.int32),
                       pltpu.VMEM((n_buf, half, width), U32),
                       pltpu.SemaphoreType.DMA((n_buf,))],
        name="sc_gather",
    )
    def body(dest_hbm, ys_hbm, yk_hbm, idx_v, rows_v, sems):
        @pl.loop(0, steps)
        def _(j):
            base = _sc_token_base(steps, j)
            pltpu.sync_copy(dest_hbm.at[:, pl.ds(token0 + base, SC_TOKENS)], idx_v)

            def gather(u):
                k, h = units[u]
                slot = u % n_buf
                return pltpu.make_async_copy(ys_hbm.at[idx_v.at[k, pl.ds(h * half, half)]], rows_v.at[slot],
                                             sems.at[slot])

            ahead = n_buf - 1
            for u in range(ahead):
                gather(u).start()
            for u, (k, h) in enumerate(units):
                gather(u).wait()
                pltpu.sync_copy(rows_v.at[u % n_buf], yk_hbm.at[k, pl.ds(base + h * half, half)])
                if u + ahead < len(units):
                    gather(u + ahead).start()

    return body(dest_kt, ys)


def _experts_kernel(be_ref, valid_ref, nact_ref, xs_hbm, wg_ref, wu_ref, wd_ref, ys_ref, wg_bf, wu_bf, wd_bf,
                    xbuf, sems):
    i = pl.program_id(0)
    n_active = nact_ref[0]
    slot = lax.rem(i, EXP_SLOTS)

    def fetch(j):
        src = xs_hbm.at[pl.ds(pl.multiple_of(j * TM_EXP, TM_EXP), TM_EXP), :]
        s = lax.rem(j, EXP_SLOTS)
        return pltpu.make_async_copy(src, xbuf.at[s], sems.at[s])

    @pl.when(i == 0)
    def _():
        for j in range(EXP_SLOTS - 1):
            pl.when(j < n_active)(lambda j=j: fetch(j).start())

    @pl.when(i + (EXP_SLOTS - 1) < n_active)
    def _():
        fetch(i + (EXP_SLOTS - 1)).start()

    prev = be_ref[jnp.maximum(i - 1, 0)]

    @pl.when((i == 0) | (be_ref[i] != prev))
    def _():
        wg_bf[...] = wg_ref[0].astype(BF16)
        wu_bf[...] = wu_ref[0].astype(BF16)
        wd_bf[...] = wd_ref[0].astype(BF16)

    @pl.when(i < n_active)
    def _():
        fetch(i).wait()

    valid = jnp.where(i < n_active, valid_ref[i], 0)

    def run(rows):
        row = lax.broadcasted_iota(jnp.int32, (rows, xbuf.shape[2]), 0)
        words = jnp.where(row < valid, xbuf[slot, 0:rows, :], jnp.uint32(0))
        xb = _unpack_words(words).astype(BF16)
        hid = _silu(_dot(xb, wg_bf[...])) * _dot(xb, wu_bf[...])
        ys_ref[0:rows, :] = _pack_words(_dot(hid.astype(BF16), wd_bf[...]))
        if rows < TM_EXP:
            ys_ref[rows:, :] = jnp.zeros((TM_EXP - rows, ys_ref.shape[1]), U32)

    for rows in range(EXP_MIN_ROWS, TM_EXP + 1, EXP_MIN_ROWS):
        pl.when((valid > rows - EXP_MIN_ROWS) & (valid <= rows))(functools.partial(run, rows))

    @pl.when(valid == 0)
    def _():
        ys_ref[...] = jnp.zeros_like(ys_ref)


def _experts(block_e, block_valid, n_active, xs, w_gate, w_up, w_down):
    n_rows, half = xs.shape
    d = w_gate.shape[1]
    nb = n_rows // TM_EXP
    grid_spec = pltpu.PrefetchScalarGridSpec(
        num_scalar_prefetch=3,
        grid=(nb,),
        in_specs=[pl.BlockSpec(memory_space=pl.ANY),
                  pl.BlockSpec((1, d, D_EXPERT), lambda i, be, bv, na: (be[i], 0, 0)),
                  pl.BlockSpec((1, d, D_EXPERT), lambda i, be, bv, na: (be[i], 0, 0)),
                  pl.BlockSpec((1, D_EXPERT, d), lambda i, be, bv, na: (be[i], 0, 0))],
        out_specs=pl.BlockSpec((TM_EXP, half), lambda i, be, bv, na: (i, 0)),
        scratch_shapes=[pltpu.VMEM((d, D_EXPERT), BF16), pltpu.VMEM((d, D_EXPERT), BF16),
                        pltpu.VMEM((D_EXPERT, d), BF16),
                        pltpu.VMEM((EXP_SLOTS, TM_EXP, half), U32),
                        pltpu.SemaphoreType.DMA((EXP_SLOTS,))],
    )
    return pl.pallas_call(
        _experts_kernel,
        out_shape=jax.ShapeDtypeStruct((n_rows, half), U32),
        grid_spec=grid_spec,
        compiler_params=pltpu.CompilerParams(vmem_limit_bytes=VMEM_LIMIT, dimension_semantics=("arbitrary",)),
        name="experts",
    )(block_e, block_valid, n_active, xs, w_gate, w_up, w_down)


def _combine_kernel(yk_ref, x1_ref, h2_ref, gate_ref, mod_ref, wsg_ref, wsu_ref, wsd_ref, fg_ref, *out_refs):
    out_ref = out_refs[-1]
    hb = _unpack_words(h2_ref[...]).astype(BF16)
    hid = _silu(_dot(hb, wsg_ref[...])) * _dot(hb, wsu_ref[...])
    ffn = _dot(hid.astype(BF16), wsd_ref[...])
    gate = gate_ref[...]
    for k in range(TOP_K):
        ffn = ffn + gate[:, k:k + 1] * _unpack_words(yk_ref[k])
    x2 = x1_ref[...] + mod_ref[0, 5:6, :] * ffn
    ms = jnp.mean(x2 * x2, axis=-1, keepdims=True)
    out_ref[...] = x2 * lax.rsqrt(ms + EPS) * fg_ref[...]


def _combine(yk, token0, out_token0, n_out_tokens, prev_out, x1, h2p, gate_tk, mod, wsg_bf, wsu_bf, wsd_bf, final_g,
             seq_len):
    t, d = n_out_tokens, x1.shape[1]
    tiles_per_seq = seq_len // TF
    tile0 = token0 // TF
    out_tile0 = out_token0 // TF
    in_tok = pl.BlockSpec((TF, d), lambda i: (tile0 + i, 0))
    tok = pl.BlockSpec((TF, d), lambda i: (out_tile0 + i, 0))
    in_specs = [pl.BlockSpec((TOP_K, TF, d // 2), lambda i: (0, i, 0)),
                in_tok, pl.BlockSpec((TF, d // 2), lambda i: (tile0 + i, 0)),
                pl.BlockSpec((TF, TOP_K), lambda i: (tile0 + i, 0)),
                pl.BlockSpec((1, 6, d), lambda i: ((out_tile0 + i) // tiles_per_seq, 0, 0)),
                pl.BlockSpec((d, D_SHARED), lambda i: (0, 0)),
                pl.BlockSpec((d, D_SHARED), lambda i: (0, 0)),
                pl.BlockSpec((D_SHARED, d), lambda i: (0, 0)),
                pl.BlockSpec((1, d), lambda i: (0, 0))]
    args = [yk, x1, h2p, gate_tk, mod, wsg_bf, wsu_bf, wsd_bf, final_g]
    aliases = {}
    if prev_out is not None:
        in_specs.append(pl.BlockSpec(memory_space=pl.ANY))
        args.append(prev_out)
        aliases = {len(args) - 1: 0}
    return pl.pallas_call(
        _combine_kernel,
        out_shape=jax.ShapeDtypeStruct((t, d), F32),
        grid=(yk.shape[1] // TF,),
        in_specs=in_specs,
        out_specs=tok,
        input_output_aliases=aliases,
        compiler_params=pltpu.CompilerParams(vmem_limit_bytes=VMEM_LIMIT),
        name="combine",
    )(*args)


def _rope_tables(n_tokens):
    n_rows = n_tokens // GRID_W
    n_freq = HEAD_DIM // 4
    inv_freq = ROPE_THETA ** (-jnp.arange(n_freq, dtype=F32) / n_freq)
    ang_r = jnp.arange(n_rows).astype(F32)[:, None] * inv_freq[None, :]
    ang_c = jnp.arange(GRID_W).astype(F32)[:, None] * inv_freq[None, :]

    def per_token(row_part, col_part):
        rows = jnp.broadcast_to(row_part[:, None, :], (n_rows, GRID_W, n_freq))
        cols = jnp.broadcast_to(col_part[None, :, :], (n_rows, GRID_W, n_freq))
        return rows.reshape(n_tokens, n_freq), cols.reshape(n_tokens, n_freq)

    cos_r, cos_c = per_token(jnp.cos(ang_r), jnp.cos(ang_c))
    sin_r, sin_c = per_token(jnp.sin(ang_r), jnp.sin(ang_c))
    cos = jnp.concatenate([cos_r, cos_r, cos_c, cos_c], axis=1)
    sin = jnp.concatenate([-sin_r, sin_r, -sin_c, sin_c], axis=1)
    reps = LANES // HEAD_DIM
    return jnp.tile(cos, (1, reps)), jnp.tile(sin, (1, reps))


def _pool_bands():
    i = jnp.arange(QB)[:, None]
    r = jnp.arange(POOL_SLAB)[None, :]
    return jnp.stack([((r >= i + POOL_OFF - w // 2) & (r < i + POOL_OFF + w // 2)).astype(BF16)
                      for w in POOL_WINDOWS])


def kernel(x, c, ctx, c_ctx, w_ada, b_ada, norm1_g, norm2_g, w_in, attn_sink, pool_w, pool_scale, w_out,
           w_router, router_bias, w_gate, w_up, w_down, ws_gate, ws_up, ws_down, final_g):
    b, s, d = x.shape
    t = b * s
    assert w_ada.shape[0] == 1 and d == D_MODEL and s % TQ == 0 and b + 1 <= 8

    c8 = jnp.zeros((8, d), F32).at[:b].set(c).at[b].set(c_ctx)
    mod = _ada(c8, w_ada[0], b_ada[0]).reshape(8, 6, d)
    g1 = norm1_g[0].reshape(1, d)
    g2 = norm2_g[0].reshape(1, d)
    w_in_bf = w_in[0].astype(BF16)
    cos_t, sin_t = _rope_tables(s)

    q, k4, v4, p = _inproj(x, mod, g1, w_in_bf, cos_t, sin_t)
    kc4, vc4 = _ctxproj(ctx, mod[b:b + 1], g1, w_in_bf[:, ATTN_WIDTH:ATTN_WIDTH + 2 * KV_WIDTH])

    wr_t = w_router[0].T
    wr_hi = wr_t.astype(BF16)
    wr_lo = (wr_t - wr_hi.astype(F32)).astype(BF16)
    attn_consts = (_pool_bands(), pool_w[0].astype(BF16), pool_scale[0].reshape(1, POOL_WIDTH),
                   w_out[0].astype(BF16), g2, wr_hi, wr_lo)
    tri = jnp.triu(jnp.ones((TR, TR), BF16), k=1)
    shared_w = (ws_gate[0].astype(BF16), ws_up[0].astype(BF16), ws_down[0].astype(BF16))

    nb = b // TOKEN_GROUPS
    tg = nb * s
    assert b % TOKEN_GROUPS == 0 and tg % (COMBINE_CHUNKS * SC_WORKERS * SC_TOKENS) == 0
    n_rows = -(-(tg * TOP_K + N_EXPERTS * (TM_EXP - 1)) // TM_EXP) * TM_EXP
    groups = []
    for g in range(TOKEN_GROUPS):
        x1, h2p, lg_t = _attn(g * nb, nb, attn_sink[0], q, k4, v4, kc4, vc4, p, x, mod, *attn_consts)
        idx_kt, gate_kt, rank_kt, counts = _route(lg_t, router_bias[0].reshape(N_EXPERTS, 1), tri)
        pad_start, block_e, block_valid, n_active = _plan_blocks(counts[:, 0].astype(jnp.int32), n_rows // TM_EXP)
        dest_kt = _dest_rows(pad_start, idx_kt, rank_kt)
        xs = _sc_scatter(dest_kt, h2p, n_rows)
        groups.append((x1.reshape(tg, d), h2p, gate_kt.T, dest_kt, xs, block_e, block_valid, n_active))

    out = None
    chunk = tg // COMBINE_CHUNKS
    for g, (x1, h2p, gate_tk, dest_kt, xs, block_e, block_valid, n_active) in enumerate(groups):
        ys = _experts(block_e, block_valid, n_active, xs, w_gate[0], w_up[0], w_down[0])
        gather_bufs = 3 if g == TOKEN_GROUPS - 1 else 2
        for token0 in range(0, tg, chunk):
            yk = _sc_gather(dest_kt, ys, token0, chunk, gather_bufs)
            out = _combine(yk, token0, g * tg + token0, t, out, x1, h2p, gate_tk, mod, *shared_w,
                           final_g.reshape(1, d), s)
    return out.reshape(b, s, d)
```

```python
import functools

import jax
import jax.numpy as jnp
from jax import lax
from jax.experimental import pallas as pl
from jax.experimental.pallas import tpu as pltpu
from jax.experimental.pallas import tpu_sc as plsc

F32 = jnp.float32
BF16 = jnp.bfloat16

D_MODEL = 1024
GRID_W = 64
N_HEADS = 8
N_KV_HEADS = 2
HEAD_DIM = 64
ATTN_WIDTH = N_HEADS * HEAD_DIM
KV_WIDTH = N_KV_HEADS * HEAD_DIM
WINDOW = 128
ROPE_THETA = 10000.0
POOL_WINDOWS = (2, 4, 8, 16)
POOL_WIDTH = D_MODEL - ATTN_WIDTH
POOL_GROUP_DIM = POOL_WIDTH // len(POOL_WINDOWS)
IN_COLS = ATTN_WIDTH + 2 * KV_WIDTH + POOL_WIDTH
N_EXPERTS = 64
TOP_K = 8
N_EXPERT_GROUPS = 8
EXPERTS_PER_GROUP = N_EXPERTS // N_EXPERT_GROUPS
TOPK_GROUPS = 4
D_EXPERT = 256
D_SHARED = 256
ROUTED_SCALE = 2.5
EPS = 1e-6
LOG2E = 1.4426950408889634

LANES = 128
U32 = jnp.uint32
VMEM_LIMIT = 48 * 1024 * 1024

TM_PROJ = 1024
TQ = 1024
QB = 128
POOL_SLAB = 256
POOL_OFF = 64
TR = 512
TM_EXP = 1024
EXP_MIN_ROWS = 256
EXP_SLOTS = 4
TF = 512
GROUP_SPLIT = (3, 1)
COMBINE_CHUNKS = 2
SC_CORES = 2
SC_WORKERS = 32
SC_TOKENS = 128


def _silu(x):
    return x * (1.0 / (1.0 + jnp.exp(-x)))


def _split_bf16(x):
    hi = x.astype(BF16)
    lo = (x - hi.astype(F32)).astype(BF16)
    return hi, lo


def _dot(a, b):
    return jnp.dot(a, b, preferred_element_type=F32)


def _pack_words(val):
    half = val.shape[1] // 2
    lo = lax.bitcast_convert_type(val[:, :half].astype(BF16).astype(F32), U32)
    hi = lax.bitcast_convert_type(val[:, half:].astype(BF16).astype(F32), U32)
    return lax.shift_right_logical(lo, jnp.uint32(16)) | hi


def _unpack_words(words):
    lo = lax.bitcast_convert_type(lax.shift_left(words, jnp.uint32(16)), F32)
    hi = lax.bitcast_convert_type(words & jnp.uint32(0xFFFF0000), F32)
    return jnp.concatenate([lo, hi], axis=1)


def _dot_nt(a, b):
    return lax.dot_general(a, b, (((1,), (1,)), ((), ())), preferred_element_type=F32)


def _ada_kernel(c_ref, w_ref, b_ref, o_ref):
    a_hi, a_lo = _split_bf16(_silu(c_ref[...]))
    w_hi, w_lo = _split_bf16(w_ref[...])
    o_ref[...] = _dot(a_hi, w_hi) + _dot(a_lo, w_hi) + _dot(a_hi, w_lo) + b_ref[...]


def _ada(c8, w_ada, b_ada):
    d = c8.shape[1]
    n = w_ada.shape[1]
    tn = 512
    return pl.pallas_call(
        _ada_kernel,
        out_shape=jax.ShapeDtypeStruct((8, n), F32),
        grid=(n // tn,),
        in_specs=[pl.BlockSpec((8, d), lambda j: (0, 0)),
                  pl.BlockSpec((d, tn), lambda j: (0, j)),
                  pl.BlockSpec((1, tn), lambda j: (0, j))],
        out_specs=pl.BlockSpec((8, tn), lambda j: (0, j)),
        compiler_params=pltpu.CompilerParams(vmem_limit_bytes=VMEM_LIMIT),
        name="ada",
    )(c8, w_ada, b_ada.reshape(1, n))


def _norm_mod(x, g, shift, scale):
    ms = jnp.mean(x * x, axis=-1, keepdims=True)
    return (x * lax.rsqrt(ms + EPS) * g) * (1.0 + scale) + shift


def _lane_variants(t):
    lane = lax.broadcasted_iota(jnp.int32, t.shape, 1)
    lo = lane < HEAD_DIM
    tr = pltpu.roll(t, HEAD_DIM, 1)
    zero = jnp.zeros_like(t)
    return (jnp.where(lo, t, zero), jnp.where(lo, zero, tr),
            jnp.where(lo, tr, zero), jnp.where(lo, zero, t))


def _store_variants(ref, t):
    for i, var in enumerate(_lane_variants(t)):
        ref[0, :, i * LANES:(i + 1) * LANES] = var.astype(BF16)


def _inproj_kernel(x_ref, mod_ref, g_ref, w_ref, cos_ref, sin_ref, q_ref, k_ref, v_ref, p_ref):
    h = _norm_mod(x_ref[0], g_ref[...], mod_ref[0, 0:1, :], mod_ref[0, 1:2, :])
    z = _dot(h.astype(BF16), w_ref[...])
    cos = cos_ref[...]
    sin = sin_ref[...]
    lane = lax.broadcasted_iota(jnp.int32, cos.shape, 1)
    first_half = (lane & 16) == 0

    def rope(zc):
        partner = jnp.where(first_half, pltpu.roll(zc, LANES - 16, 1), pltpu.roll(zc, 16, 1))
        return zc * cos + partner * sin

    scale = HEAD_DIM ** -0.5 * LOG2E
    for c in range(ATTN_WIDTH // LANES):
        q_ref[0, :, c * LANES:(c + 1) * LANES] = (rope(z[:, c * LANES:(c + 1) * LANES]) * scale).astype(BF16)
    _store_variants(k_ref, rope(z[:, ATTN_WIDTH:ATTN_WIDTH + KV_WIDTH]))
    _store_variants(v_ref, z[:, ATTN_WIDTH + KV_WIDTH:ATTN_WIDTH + 2 * KV_WIDTH])
    p_ref[0] = z[:, ATTN_WIDTH + 2 * KV_WIDTH:]


def _inproj(x, mod, g1, w_in_bf, cos_t, sin_t):
    b, s, d = x.shape
    tm = TM_PROJ
    return pl.pallas_call(
        _inproj_kernel,
        out_shape=(jax.ShapeDtypeStruct((b, s, ATTN_WIDTH), BF16),
                   jax.ShapeDtypeStruct((b, s, 4 * LANES), BF16),
                   jax.ShapeDtypeStruct((b, s, 4 * LANES), BF16),
                   jax.ShapeDtypeStruct((b, s, POOL_WIDTH), F32)),
        grid=(s // tm, b),
        in_specs=[pl.BlockSpec((1, tm, d), lambda n, bi: (bi, n, 0)),
                  pl.BlockSpec((1, 6, d), lambda n, bi: (bi, 0, 0)),
                  pl.BlockSpec((1, d), lambda n, bi: (0, 0)),
                  pl.BlockSpec((d, IN_COLS), lambda n, bi: (0, 0)),
                  pl.BlockSpec((tm, LANES), lambda n, bi: (n, 0)),
                  pl.BlockSpec((tm, LANES), lambda n, bi: (n, 0))],
        out_specs=(pl.BlockSpec((1, tm, ATTN_WIDTH), lambda n, bi: (bi, n, 0)),
                   pl.BlockSpec((1, tm, 4 * LANES), lambda n, bi: (bi, n, 0)),
                   pl.BlockSpec((1, tm, 4 * LANES), lambda n, bi: (bi, n, 0)),
                   pl.BlockSpec((1, tm, POOL_WIDTH), lambda n, bi: (bi, n, 0))),
        compiler_params=pltpu.CompilerParams(vmem_limit_bytes=VMEM_LIMIT),
        name="inproj",
    )(x, mod, g1, w_in_bf, cos_t, sin_t)


def _ctxproj_kernel(x_ref, mod_ref, g_ref, w_ref, k_ref, v_ref):
    h = _norm_mod(x_ref[0], g_ref[...], mod_ref[0, 0:1, :], mod_ref[0, 1:2, :])
    z = _dot(h.astype(BF16), w_ref[...])
    _store_variants(k_ref, z[:, :KV_WIDTH])
    _store_variants(v_ref, z[:, KV_WIDTH:])


def _ctxproj(ctx, mod_c, g1, w_kv_bf):
    b, c, d = ctx.shape
    return pl.pallas_call(
        _ctxproj_kernel,
        out_shape=(jax.ShapeDtypeStruct((b, c, 4 * LANES), BF16),
                   jax.ShapeDtypeStruct((b, c, 4 * LANES), BF16)),
        grid=(b,),
        in_specs=[pl.BlockSpec((1, c, d), lambda bi: (bi, 0, 0)),
                  pl.BlockSpec((1, 6, d), lambda bi: (0, 0, 0)),
                  pl.BlockSpec((1, d), lambda bi: (0, 0)),
                  pl.BlockSpec((d, 2 * KV_WIDTH), lambda bi: (0, 0))],
        out_specs=(pl.BlockSpec((1, c, 4 * LANES), lambda bi: (bi, 0, 0)),
                   pl.BlockSpec((1, c, 4 * LANES), lambda bi: (bi, 0, 0))),
        compiler_params=pltpu.CompilerParams(vmem_limit_bytes=VMEM_LIMIT),
        name="ctxproj",
    )(ctx, mod_c, g1, w_kv_bf)


def _fold(op, tiles):
    while len(tiles) > 1:
        tiles = [op(tiles[i], tiles[i + 1]) if i + 1 < len(tiles) else tiles[i] for i in range(0, len(tiles), 2)]
    return tiles[0]


def _stack_variants(t4, kv):
    return jnp.concatenate([t4[:, (2 * kv) * LANES:(2 * kv + 1) * LANES],
                            t4[:, (2 * kv + 1) * LANES:(2 * kv + 2) * LANES]], axis=0)


def _attn_kernel(seq_len, sink_ref, q_ref, k_ref, kp_ref, kn_ref, v_ref, vp_ref, vn_ref, kc_ref, vc_ref,
                 p_ref, pp_ref, pn_ref, x_ref, mod_ref, band_ref, poolw_ref, pscale_ref, wout_ref,
                 g2_ref, wrh_ref, wrl_ref, x1_ref, h2_ref, lg_ref, kwin, vwin, pext, mix, s_scr, p_scr, m_scr):
    n = pl.program_id(1)
    n_last = pl.num_programs(1) - 1

    kwin[0:QB, :] = kp_ref[0]
    kwin[QB:QB + TQ, :] = k_ref[0]
    kwin[QB + TQ:, :] = kn_ref[0]
    vwin[0:QB, :] = vp_ref[0]
    vwin[QB:QB + TQ, :] = v_ref[0]
    vwin[QB + TQ:, :] = vn_ref[0]

    pext[0:QB - 8, :] = jnp.zeros((QB - 8, POOL_WIDTH), F32)
    pext[QB - 8:QB, :] = jnp.where(n > 0, pp_ref[0], 0.0)
    pext[QB:QB + TQ, :] = p_ref[0]
    pext[QB + TQ:QB + TQ + 8, :] = jnp.where(n < n_last, pn_ref[0], 0.0)
    pext[QB + TQ + 8:, :] = jnp.zeros((QB - 8, POOL_WIDTH), F32)

    row = lax.broadcasted_iota(jnp.int32, (QB, 3 * QB), 0)
    col = lax.broadcasted_iota(jnp.int32, (QB, 3 * QB), 1)
    in_band = (col >= row) & (col <= row + 2 * WINDOW)
    tok = lax.broadcasted_iota(jnp.int32, (QB, 1), 0)
    kc = kc_ref[0]
    kc_rows = [_stack_variants(kc, kv) for kv in range(N_KV_HEADS)]
    vc_rows = [_stack_variants(vc_ref[0], kv) for kv in range(N_KV_HEADS)]

    def sub_block(j, carry):
        r0 = pl.multiple_of(j * QB, QB)
        qj = q_ref[0, pl.ds(r0, QB), :]
        kw = kwin[pl.ds(r0, 3 * QB), :]
        vw = vwin[pl.ds(r0, 3 * QB), :]
        kpos = col + (n * TQ + j * QB - QB)
        ok = in_band & (kpos >= 0) & (kpos < seq_len)
        bias = jnp.where(ok, 0.0, -jnp.inf)
        n_loc, n_ctx = 3 * QB, kc.shape[0]
        bias2 = jnp.concatenate([bias, bias], axis=1)
        k_rows = [_stack_variants(kw, kv) for kv in range(N_KV_HEADS)]
        v_rows = [_stack_variants(vw, kv) for kv in range(N_KV_HEADS)]
        group = N_HEADS // N_KV_HEADS

        def head_tiles(head):
            loc0 = (head % 2) * n_loc
            ctx0 = 2 * n_loc + (head % 2) * n_ctx
            return ([loc0 + i * LANES for i in range(n_loc // LANES)]
                    + [ctx0 + i * LANES for i in range(n_ctx // LANES)])

        for c in range(N_HEADS // 2):
            qc = qj[:, c * LANES:(c + 1) * LANES]
            s_scr[c, :, 0:2 * n_loc] = _dot_nt(qc, k_rows[2 * c // group]) + bias2
            s_scr[c, :, 2 * n_loc:] = _dot_nt(qc, kc_rows[2 * c // group])
        for head in range(N_HEADS):
            tiles = [s_scr[head // 2, :, st:st + LANES] for st in head_tiles(head)]
            row_max = jnp.max(_fold(jnp.maximum, tiles), axis=1, keepdims=True)
            m_scr[head] = jnp.broadcast_to(jnp.maximum(row_max, sink_ref[head] * LOG2E), (QB, LANES))
        for head in range(N_HEADS):
            m = m_scr[head]
            acc = None
            for st in head_tiles(head):
                p = jnp.exp2(s_scr[head // 2, :, st:st + LANES] - m)
                p_scr[head // 2, :, st:st + LANES] = p.astype(BF16)
                acc = p if acc is None else acc + p
            denom = (jnp.broadcast_to(jnp.sum(acc, axis=1, keepdims=True), (QB, LANES))
                     + jnp.exp2(sink_ref[head] * LOG2E - m))
            m_scr[head] = 1.0 / denom
        lane = lax.broadcasted_iota(jnp.int32, (QB, LANES), 1)
        for c in range(N_HEADS // 2):
            o = (_dot(p_scr[c, :, 0:2 * n_loc], v_rows[2 * c // group])
                 + _dot(p_scr[c, :, 2 * n_loc:], vc_rows[2 * c // group]))
            o = o * jnp.where(lane < HEAD_DIM, m_scr[2 * c], m_scr[2 * c + 1])
            mix[pl.ds(r0, QB), c * LANES:(c + 1) * LANES] = o.astype(BF16)

        slab = pext[pl.ds(pl.multiple_of(r0 + POOL_OFF, 8), POOL_SLAB), :]
        tpos = tok + (n * TQ + j * QB)
        for g, w in enumerate(POOL_WINDOWS):
            sg = slab[:, g * LANES:(g + 1) * LANES]
            hi, lo = _split_bf16(sg)
            band = band_ref[g]
            wsum = _dot(band, hi) + _dot(band, lo)
            cnt = (jnp.minimum(tpos - w // 2 + w, seq_len) - jnp.maximum(tpos - w // 2, 0)).astype(F32)
            dlt = wsum / cnt - sg[POOL_OFF:POOL_OFF + QB, :]
            y = _dot(dlt.astype(BF16), poolw_ref[g]) * pscale_ref[:, g * LANES:(g + 1) * LANES]
            mix[pl.ds(r0, QB), ATTN_WIDTH + g * LANES:ATTN_WIDTH + (g + 1) * LANES] = y.astype(BF16)
        return carry

    lax.fori_loop(0, TQ // QB, sub_block, 0)

    proj = _dot(mix[...], wout_ref[...])
    x1 = x_ref[0] + mod_ref[0, 2:3, :] * proj
    x1_ref[0] = x1
    h2 = _norm_mod(x1, g2_ref[...], mod_ref[0, 3:4, :], mod_ref[0, 4:5, :])
    h2_ref[...] = _pack_words(h2)
    h_hi, h_lo = _split_bf16(h2)
    wrh = wrh_ref[...]
    lg_ref[...] = _dot_nt(wrh, h_hi) + _dot_nt(wrh, h_lo) + _dot_nt(wrl_ref[...], h_hi)


def _attn(b0, b, sink, q, k4, v4, kc4, vc4, p, x, mod, band, poolw_bf, pscale, wout_bf, g2, wr_hi, wr_lo):
    _, s, d = x.shape
    c = kc4.shape[1]
    nt = s // TQ
    hb = TQ // QB
    pb = TQ // 8
    kv_main = pl.BlockSpec((1, TQ, 4 * LANES), lambda bi, n: (b0 + bi, n, 0))
    kv_prev = pl.BlockSpec((1, QB, 4 * LANES), lambda bi, n: (b0 + bi, jnp.maximum(n * hb - 1, 0), 0))
    kv_next = pl.BlockSpec((1, QB, 4 * LANES), lambda bi, n: (b0 + bi, jnp.minimum((n + 1) * hb, s // QB - 1), 0))
    const2 = lambda bi, n: (0, 0)
    const3 = lambda bi, n: (0, 0, 0)
    return pl.pallas_call(
        functools.partial(_attn_kernel, s),
        out_shape=(jax.ShapeDtypeStruct((b, s, d), F32),
                   jax.ShapeDtypeStruct((b * s, d // 2), U32),
                   jax.ShapeDtypeStruct((N_EXPERTS, b * s), F32)),
        grid=(b, nt),
        in_specs=[pl.BlockSpec(memory_space=pltpu.SMEM),
                  pl.BlockSpec((1, TQ, ATTN_WIDTH), lambda bi, n: (b0 + bi, n, 0)),
                  kv_main, kv_prev, kv_next, kv_main, kv_prev, kv_next,
                  pl.BlockSpec((1, c, 4 * LANES), lambda bi, n: (b0 + bi, 0, 0)),
                  pl.BlockSpec((1, c, 4 * LANES), lambda bi, n: (b0 + bi, 0, 0)),
                  pl.BlockSpec((1, TQ, POOL_WIDTH), lambda bi, n: (b0 + bi, n, 0)),
                  pl.BlockSpec((1, 8, POOL_WIDTH), lambda bi, n: (b0 + bi, jnp.maximum(n * pb - 1, 0), 0)),
                  pl.BlockSpec((1, 8, POOL_WIDTH),
                               lambda bi, n: (b0 + bi, jnp.minimum((n + 1) * pb, s // 8 - 1), 0)),
                  pl.BlockSpec((1, TQ, d), lambda bi, n: (b0 + bi, n, 0)),
                  pl.BlockSpec((1, 6, d), lambda bi, n: (b0 + bi, 0, 0)),
                  pl.BlockSpec((len(POOL_WINDOWS), QB, POOL_SLAB), const3),
                  pl.BlockSpec((len(POOL_WINDOWS), POOL_GROUP_DIM, POOL_GROUP_DIM), const3),
                  pl.BlockSpec((1, POOL_WIDTH), const2),
                  pl.BlockSpec((d, d), const2),
                  pl.BlockSpec((1, d), const2),
                  pl.BlockSpec((N_EXPERTS, d), const2),
                  pl.BlockSpec((N_EXPERTS, d), const2)],
        out_specs=(pl.BlockSpec((1, TQ, d), lambda bi, n: (bi, n, 0)),
                   pl.BlockSpec((TQ, d // 2), lambda bi, n: (bi * nt + n, 0)),
                   pl.BlockSpec((N_EXPERTS, TQ), lambda bi, n: (0, bi * nt + n))),
        scratch_shapes=[pltpu.VMEM((TQ + 2 * QB, 4 * LANES), BF16),
                        pltpu.VMEM((TQ + 2 * QB, 4 * LANES), BF16),
                        pltpu.VMEM((TQ + 2 * QB, POOL_WIDTH), F32),
                        pltpu.VMEM((TQ, d), BF16),
                        pltpu.VMEM((N_HEADS // 2, QB, 2 * (3 * QB + c)), F32),
                        pltpu.VMEM((N_HEADS // 2, QB, 2 * (3 * QB + c)), BF16),
                        pltpu.VMEM((N_HEADS, QB, LANES), F32)],
        compiler_params=pltpu.CompilerParams(vmem_limit_bytes=VMEM_LIMIT),
        name="attn",
    )(sink, q, k4, k4, k4, v4, v4, v4, kc4, vc4, p, p, p, x, mod, band, poolw_bf, pscale, wout_bf,
      g2, wr_hi, wr_lo)


def _first_argmax_rows(v, row_iota, n_rows):
    m = jnp.max(v, axis=0, keepdims=True)
    idx = jnp.min(jnp.where(v == m, row_iota, n_rows), axis=0, keepdims=True)
    return m, idx


def _route_kernel(lg_ref, bias_ref, tri_ref, idx_ref, gate_ref, rank_ref, cnt_ref, carry):
    i = pl.program_id(0)

    @pl.when(i == 0)
    def _():
        carry[...] = jnp.zeros_like(carry)

    scores = 1.0 / (1.0 + jnp.exp(-lg_ref[...]))
    biased = scores + bias_ref[...]
    e_iota = lax.broadcasted_iota(jnp.int32, scores.shape, 0).astype(F32)
    g_iota = lax.broadcasted_iota(jnp.int32, (EXPERTS_PER_GROUP, TR), 0).astype(F32)
    neg = -jnp.inf

    grp = []
    for g in range(N_EXPERT_GROUPS):
        blk = biased[g * EXPERTS_PER_GROUP:(g + 1) * EXPERTS_PER_GROUP, :]
        m1, i1 = _first_argmax_rows(blk, g_iota, float(EXPERTS_PER_GROUP))
        m2 = jnp.max(jnp.where(g_iota == i1, neg, blk), axis=0, keepdims=True)
        grp.append(m1 + m2)
    grp = jnp.concatenate(grp, axis=0)
    gg_iota = lax.broadcasted_iota(jnp.int32, grp.shape, 0).astype(F32)
    grp_sel = jnp.zeros(grp.shape, F32)
    for _ in range(TOPK_GROUPS):
        _, gi = _first_argmax_rows(grp, gg_iota, float(N_EXPERT_GROUPS))
        hit = gg_iota == gi
        grp_sel = jnp.where(hit, 1.0, grp_sel)
        grp = jnp.where(hit, neg, grp)
    allowed = jnp.concatenate(
        [jnp.broadcast_to(grp_sel[g:g + 1, :], (EXPERTS_PER_GROUP, TR)) for g in range(N_EXPERT_GROUPS)], axis=0)
    masked = jnp.where(allowed > 0.5, biased, neg)

    idxs, gates = [], []
    onehot = jnp.zeros(scores.shape, F32)
    for _ in range(TOP_K):
        _, ei = _first_argmax_rows(masked, e_iota, float(N_EXPERTS))
        hit = e_iota == ei
        idxs.append(ei)
        gates.append(jnp.sum(jnp.where(hit, scores, 0.0), axis=0, keepdims=True))
        onehot = jnp.where(hit, 1.0, onehot)
        masked = jnp.where(hit, neg, masked)
    idx = jnp.concatenate(idxs, axis=0)
    gate = jnp.concatenate(gates, axis=0)
    gate = gate / jnp.sum(gate, axis=0, keepdims=True) * ROUTED_SCALE

    before = _dot(onehot.astype(BF16), tri_ref[...]) + carry[:, 0:1]
    ranks = [jnp.sum(jnp.where(e_iota == idxs[k], before, 0.0), axis=0, keepdims=True) for k in range(TOP_K)]
    idx_ref[...] = idx.astype(jnp.int32)
    gate_ref[...] = gate
    rank_ref[...] = jnp.concatenate(ranks, axis=0).astype(jnp.int32)
    total = carry[...] + jnp.sum(onehot, axis=1, keepdims=True)
    carry[...] = total
    cnt_ref[...] = total


def _route(lg_t, bias, tri):
    e, t = lg_t.shape
    tok = pl.BlockSpec((TOP_K, TR), lambda i: (0, i))
    return pl.pallas_call(
        _route_kernel,
        out_shape=(jax.ShapeDtypeStruct((TOP_K, t), jnp.int32),
                   jax.ShapeDtypeStruct((TOP_K, t), F32),
                   jax.ShapeDtypeStruct((TOP_K, t), jnp.int32),
                   jax.ShapeDtypeStruct((e, LANES), F32)),
        grid=(t // TR,),
        in_specs=[pl.BlockSpec((e, TR), lambda i: (0, i)),
                  pl.BlockSpec((e, 1), lambda i: (0, 0)),
                  pl.BlockSpec((TR, TR), lambda i: (0, 0))],
        out_specs=(tok, tok, tok, pl.BlockSpec((e, LANES), lambda i: (0, 0))),
        scratch_shapes=[pltpu.VMEM((e, LANES), F32)],
        compiler_params=pltpu.CompilerParams(vmem_limit_bytes=VMEM_LIMIT),
        name="route",
    )(lg_t, bias, tri)


def _plan_kernel(n_blocks, size_ref, start_ref, expert_ref, valid_ref, nact_ref):
    def per_expert(e, first_block):
        size = size_ref[e]
        n_blk = (size + TM_EXP - 1) // TM_EXP
        start_ref[e] = first_block * TM_EXP

        def per_block(j, carry):
            expert_ref[first_block + j] = e
            valid_ref[first_block + j] = jnp.minimum(size - j * TM_EXP, TM_EXP)
            return carry

        lax.fori_loop(0, n_blk, per_block, 0)
        return first_block + n_blk

    n_active = lax.fori_loop(0, N_EXPERTS, per_expert, 0)
    nact_ref[0] = n_active

    def unused(i, carry):
        expert_ref[i] = N_EXPERTS - 1
        valid_ref[i] = 0
        return carry

    lax.fori_loop(n_active, n_blocks, unused, 0)


def _plan_blocks(sizes, n_blocks):
    smem = pl.BlockSpec(memory_space=pltpu.SMEM)
    return pl.pallas_call(
        functools.partial(_plan_kernel, n_blocks),
        out_shape=(jax.ShapeDtypeStruct((N_EXPERTS,), jnp.int32),
                   jax.ShapeDtypeStruct((n_blocks,), jnp.int32),
                   jax.ShapeDtypeStruct((n_blocks,), jnp.int32),
                   jax.ShapeDtypeStruct((1,), jnp.int32)),
        in_specs=[smem],
        out_specs=(smem, smem, smem, smem),
        name="plan_blocks",
    )(sizes)


def _dest_kernel(start_ref, idx_ref, rank_ref, dest_ref):
    idx = idx_ref[...]
    dest = rank_ref[...]
    for e in range(N_EXPERTS):
        dest = dest + jnp.where(idx == e, start_ref[e], 0)
    dest_ref[...] = dest


def _dest_rows(pad_start, idx_kt, rank_kt):
    n_k, t = idx_kt.shape
    tile = 4096
    blk = pl.BlockSpec((n_k, tile), lambda i: (0, i))
    return pl.pallas_call(
        _dest_kernel,
        out_shape=jax.ShapeDtypeStruct((n_k, t), jnp.int32),
        grid=(t // tile,),
        in_specs=[pl.BlockSpec(memory_space=pltpu.SMEM), blk, blk],
        out_specs=blk,
        name="dest_rows",
    )(pad_start, idx_kt, rank_kt)


def _sc_mesh():
    return plsc.VectorSubcoreMesh(core_axis_name="c", subcore_axis_name="s")


def _sc_token_base(steps, j):
    worker = lax.axis_index("s") * SC_CORES + lax.axis_index("c")
    return (worker * steps + j) * SC_TOKENS


def _sc_scatter(dest_kt, h2p, n_rows):
    t, width = h2p.shape
    steps = t // (SC_WORKERS * SC_TOKENS)

    @functools.partial(
        pl.kernel, mesh=_sc_mesh(),
        out_type=jax.ShapeDtypeStruct((n_rows, width), U32),
        scratch_types=[pltpu.VMEM((TOP_K, SC_TOKENS), jnp.int32),
                       pltpu.VMEM((SC_TOKENS, width), U32),
                       pltpu.SemaphoreType.DMA],
        name="sc_scatter",
    )
    def body(dest_hbm, h_hbm, xs_hbm, idx_v, rows_v, sem):
        @pl.loop(0, steps)
        def _(j):
            base = _sc_token_base(steps, j)
            pltpu.sync_copy(dest_hbm.at[:, pl.ds(base, SC_TOKENS)], idx_v)
            pltpu.sync_copy(h_hbm.at[pl.ds(base, SC_TOKENS)], rows_v)
            copies = [pltpu.async_copy(rows_v, xs_hbm.at[idx_v.at[k]], sem) for k in range(TOP_K)]
            for cp in copies:
                cp.wait()

    return body(dest_kt, h2p)


def _sc_gather(dest_kt, ys, token0, n_tokens):
    n_k = dest_kt.shape[0]
    width = ys.shape[1]
    steps = n_tokens // (SC_WORKERS * SC_TOKENS)

    half = SC_TOKENS // 2
    units = [(k, h) for k in range(n_k) for h in range(2)]
    n_buf = 2

    @functools.partial(
        pl.kernel, mesh=_sc_mesh(),
        out_type=jax.ShapeDtypeStruct((n_k, n_tokens, width), U32),
        scratch_types=[pltpu.VMEM((n_k, SC_TOKENS), jnp.int32),
                       pltpu.VMEM((n_buf, half, width), U32),
                       pltpu.SemaphoreType.DMA((n_buf,))],
        name="sc_gather",
    )
    def body(dest_hbm, ys_hbm, yk_hbm, idx_v, rows_v, sems):
        @pl.loop(0, steps)
        def _(j):
            base = _sc_token_base(steps, j)
            pltpu.sync_copy(dest_hbm.at[:, pl.ds(token0 + base, SC_TOKENS)], idx_v)

            def gather(u):
                k, h = units[u]
                slot = u % n_buf
                return pltpu.make_async_copy(ys_hbm.at[idx_v.at[k, pl.ds(h * half, half)]], rows_v.at[slot],
                                             sems.at[slot])

            ahead = n_buf - 1
            for u in range(ahead):
                gather(u).start()
            for u, (k, h) in enumerate(units):
                gather(u).wait()
                pltpu.sync_copy(rows_v.at[u % n_buf], yk_hbm.at[k, pl.ds(base + h * half, half)])
                if u + ahead < len(units):
                    gather(u + ahead).start()

    return body(dest_kt, ys)


def _experts_kernel(be_ref, valid_ref, nact_ref, xs_hbm, wg_ref, wu_ref, wd_ref, ys_ref, wg_bf, wu_bf, wd_bf,
                    xbuf, sems):
    i = pl.program_id(0)
    n_active = nact_ref[0]
    slot = lax.rem(i, EXP_SLOTS)

    def fetch(j):
        src = xs_hbm.at[pl.ds(pl.multiple_of(j * TM_EXP, TM_EXP), TM_EXP), :]
        s = lax.rem(j, EXP_SLOTS)
        return pltpu.make_async_copy(src, xbuf.at[s], sems.at[s])

    @pl.when(i == 0)
    def _():
        for j in range(EXP_SLOTS - 1):
            pl.when(j < n_active)(lambda j=j: fetch(j).start())

    @pl.when(i + (EXP_SLOTS - 1) < n_active)
    def _():
        fetch(i + (EXP_SLOTS - 1)).start()

    prev = be_ref[jnp.maximum(i - 1, 0)]

    @pl.when((i == 0) | (be_ref[i] != prev))
    def _():
        wg_bf[...] = wg_ref[0].astype(BF16)
        wu_bf[...] = wu_ref[0].astype(BF16)
        wd_bf[...] = wd_ref[0].astype(BF16)

    @pl.when(i < n_active)
    def _():
        fetch(i).wait()

    valid = jnp.where(i < n_active, valid_ref[i], 0)

    def run(rows):
        row = lax.broadcasted_iota(jnp.int32, (rows, xbuf.shape[2]), 0)
        words = jnp.where(row < valid, xbuf[slot, 0:rows, :], jnp.uint32(0))
        xb = _unpack_words(words).astype(BF16)
        hid = _silu(_dot(xb, wg_bf[...])) * _dot(xb, wu_bf[...])
        ys_ref[0:rows, :] = _pack_words(_dot(hid.astype(BF16), wd_bf[...]))
        if rows < TM_EXP:
            ys_ref[rows:, :] = jnp.zeros((TM_EXP - rows, ys_ref.shape[1]), U32)

    for rows in range(EXP_MIN_ROWS, TM_EXP + 1, EXP_MIN_ROWS):
        pl.when((valid > rows - EXP_MIN_ROWS) & (valid <= rows))(functools.partial(run, rows))

    @pl.when(valid == 0)
    def _():
        ys_ref[...] = jnp.zeros_like(ys_ref)


def _experts(block_e, block_valid, n_active, xs, w_gate, w_up, w_down):
    n_rows, half = xs.shape
    d = w_gate.shape[1]
    nb = n_rows // TM_EXP
    grid_spec = pltpu.PrefetchScalarGridSpec(
        num_scalar_prefetch=3,
        grid=(nb,),
        in_specs=[pl.BlockSpec(memory_space=pl.ANY),
                  pl.BlockSpec((1, d, D_EXPERT), lambda i, be, bv, na: (be[i], 0, 0)),
                  pl.BlockSpec((1, d, D_EXPERT), lambda i, be, bv, na: (be[i], 0, 0)),
                  pl.BlockSpec((1, D_EXPERT, d), lambda i, be, bv, na: (be[i], 0, 0))],
        out_specs=pl.BlockSpec((TM_EXP, half), lambda i, be, bv, na: (i, 0)),
        scratch_shapes=[pltpu.VMEM((d, D_EXPERT), BF16), pltpu.VMEM((d, D_EXPERT), BF16),
                        pltpu.VMEM((D_EXPERT, d), BF16),
                        pltpu.VMEM((EXP_SLOTS, TM_EXP, half), U32),
                        pltpu.SemaphoreType.DMA((EXP_SLOTS,))],
    )
    return pl.pallas_call(
        _experts_kernel,
        out_shape=jax.ShapeDtypeStruct((n_rows, half), U32),
        grid_spec=grid_spec,
        compiler_params=pltpu.CompilerParams(vmem_limit_bytes=VMEM_LIMIT, dimension_semantics=("arbitrary",)),
        name="experts",
    )(block_e, block_valid, n_active, xs, w_gate, w_up, w_down)


def _combine_kernel(yk_ref, x1_ref, h2_ref, gate_ref, mod_ref, wsg_ref, wsu_ref, wsd_ref, fg_ref, *out_refs):
    out_ref = out_refs[-1]
    hb = _unpack_words(h2_ref[...]).astype(BF16)
    hid = _silu(_dot(hb, wsg_ref[...])) * _dot(hb, wsu_ref[...])
    ffn = _dot(hid.astype(BF16), wsd_ref[...])
    gate = gate_ref[...]
    for k in range(TOP_K):
        ffn = ffn + gate[:, k:k + 1] * _unpack_words(yk_ref[k])
    x2 = x1_ref[...] + mod_ref[0, 5:6, :] * ffn
    ms = jnp.mean(x2 * x2, axis=-1, keepdims=True)
    out_ref[...] = x2 * lax.rsqrt(ms + EPS) * fg_ref[...]


def _combine(yk, token0, out_token0, n_out_tokens, prev_out, x1, h2p, gate_tk, mod, wsg_bf, wsu_bf, wsd_bf, final_g,
             seq_len):
    t, d = n_out_tokens, x1.shape[1]
    tiles_per_seq = seq_len // TF
    tile0 = token0 // TF
    out_tile0 = out_token0 // TF
    in_tok = pl.BlockSpec((TF, d), lambda i: (tile0 + i, 0))
    tok = pl.BlockSpec((TF, d), lambda i: (out_tile0 + i, 0))
    in_specs = [pl.BlockSpec((TOP_K, TF, d // 2), lambda i: (0, i, 0)),
                in_tok, pl.BlockSpec((TF, d // 2), lambda i: (tile0 + i, 0)),
                pl.BlockSpec((TF, TOP_K), lambda i: (tile0 + i, 0)),
                pl.BlockSpec((1, 6, d), lambda i: ((out_tile0 + i) // tiles_per_seq, 0, 0)),
                pl.BlockSpec((d, D_SHARED), lambda i: (0, 0)),
                pl.BlockSpec((d, D_SHARED), lambda i: (0, 0)),
                pl.BlockSpec((D_SHARED, d), lambda i: (0, 0)),
                pl.BlockSpec((1, d), lambda i: (0, 0))]
    args = [yk, x1, h2p, gate_tk, mod, wsg_bf, wsu_bf, wsd_bf, final_g]
    aliases = {}
    if prev_out is not None:
        in_specs.append(pl.BlockSpec(memory_space=pl.ANY))
        args.append(prev_out)
        aliases = {len(args) - 1: 0}
    return pl.pallas_call(
        _combine_kernel,
        out_shape=jax.ShapeDtypeStruct((t, d), F32),
        grid=(yk.shape[1] // TF,),
        in_specs=in_specs,
        out_specs=tok,
        input_output_aliases=aliases,
        compiler_params=pltpu.CompilerParams(vmem_limit_bytes=VMEM_LIMIT),
        name="combine",
    )(*args)


def _rope_tables(n_tokens):
    n_rows = n_tokens // GRID_W
    n_freq = HEAD_DIM // 4
    inv_freq = ROPE_THETA ** (-jnp.arange(n_freq, dtype=F32) / n_freq)
    ang_r = jnp.arange(n_rows).astype(F32)[:, None] * inv_freq[None, :]
    ang_c = jnp.arange(GRID_W).astype(F32)[:, None] * inv_freq[None, :]

    def per_token(row_part, col_part):
        rows = jnp.broadcast_to(row_part[:, None, :], (n_rows, GRID_W, n_freq))
        cols = jnp.broadcast_to(col_part[None, :, :], (n_rows, GRID_W, n_freq))
        return rows.reshape(n_tokens, n_freq), cols.reshape(n_tokens, n_freq)

    cos_r, cos_c = per_token(jnp.cos(ang_r), jnp.cos(ang_c))
    sin_r, sin_c = per_token(jnp.sin(ang_r), jnp.sin(ang_c))
    cos = jnp.concatenate([cos_r, cos_r, cos_c, cos_c], axis=1)
    sin = jnp.concatenate([-sin_r, sin_r, -sin_c, sin_c], axis=1)
    reps = LANES // HEAD_DIM
    return jnp.tile(cos, (1, reps)), jnp.tile(sin, (1, reps))


def _pool_bands():
    i = jnp.arange(QB)[:, None]
    r = jnp.arange(POOL_SLAB)[None, :]
    return jnp.stack([((r >= i + POOL_OFF - w // 2) & (r < i + POOL_OFF + w // 2)).astype(BF16)
                      for w in POOL_WINDOWS])


def kernel(x, c, ctx, c_ctx, w_ada, b_ada, norm1_g, norm2_g, w_in, attn_sink, pool_w, pool_scale, w_out,
           w_router, router_bias, w_gate, w_up, w_down, ws_gate, ws_up, ws_down, final_g):
    b, s, d = x.shape
    t = b * s
    assert w_ada.shape[0] == 1 and d == D_MODEL and s % TQ == 0 and b + 1 <= 8

    c8 = jnp.zeros((8, d), F32).at[:b].set(c).at[b].set(c_ctx)
    mod = _ada(c8, w_ada[0], b_ada[0]).reshape(8, 6, d)
    g1 = norm1_g[0].reshape(1, d)
    g2 = norm2_g[0].reshape(1, d)
    w_in_bf = w_in[0].astype(BF16)
    cos_t, sin_t = _rope_tables(s)

    q, k4, v4, p = _inproj(x, mod, g1, w_in_bf, cos_t, sin_t)
    kc4, vc4 = _ctxproj(ctx, mod[b:b + 1], g1, w_in_bf[:, ATTN_WIDTH:ATTN_WIDTH + 2 * KV_WIDTH])

    wr_t = w_router[0].T
    wr_hi = wr_t.astype(BF16)
    wr_lo = (wr_t - wr_hi.astype(F32)).astype(BF16)
    attn_consts = (_pool_bands(), pool_w[0].astype(BF16), pool_scale[0].reshape(1, POOL_WIDTH),
                   w_out[0].astype(BF16), g2, wr_hi, wr_lo)
    tri = jnp.triu(jnp.ones((TR, TR), BF16), k=1)
    shared_w = (ws_gate[0].astype(BF16), ws_up[0].astype(BF16), ws_down[0].astype(BF16))

    assert b % sum(GROUP_SPLIT) == 0
    group_batches = [b * part // sum(GROUP_SPLIT) for part in GROUP_SPLIT]
    groups = []
    b0 = 0
    for nb in group_batches:
        tg = nb * s
        assert tg % (COMBINE_CHUNKS * SC_WORKERS * SC_TOKENS) == 0
        n_rows = -(-(tg * TOP_K + N_EXPERTS * (TM_EXP - 1)) // TM_EXP) * TM_EXP
        x1, h2p, lg_t = _attn(b0, nb, attn_sink[0], q, k4, v4, kc4, vc4, p, x, mod, *attn_consts)
        idx_kt, gate_kt, rank_kt, counts = _route(lg_t, router_bias[0].reshape(N_EXPERTS, 1), tri)
        pad_start, block_e, block_valid, n_active = _plan_blocks(counts[:, 0].astype(jnp.int32), n_rows // TM_EXP)
        dest_kt = _dest_rows(pad_start, idx_kt, rank_kt)
        xs = _sc_scatter(dest_kt, h2p, n_rows)
        groups.append((b0 * s, tg, x1.reshape(tg, d), h2p, gate_kt.T, dest_kt, xs, block_e, block_valid, n_active))
        b0 += nb

    out = None
    for first_token, tg, x1, h2p, gate_tk, dest_kt, xs, block_e, block_valid, n_active in groups:
        ys = _experts(block_e, block_valid, n_active, xs, w_gate[0], w_up[0], w_down[0])
        chunk = tg // COMBINE_CHUNKS
        for token0 in range(0, tg, chunk):
            out = _combine(_sc_gather(dest_kt, ys, token0, chunk), token0, first_token + token0, t, out, x1, h2p,
                           gate_tk, mod, *shared_w, final_g.reshape(1, d), s)
    return out.reshape(b, s, d)
```

```python
import functools

import jax
import jax.numpy as jnp
from jax import lax
from jax.experimental import pallas as pl
from jax.experimental.pallas import tpu as pltpu
from jax.experimental.pallas import tpu_sc as plsc

F32 = jnp.float32
BF16 = jnp.bfloat16

D_MODEL = 1024
GRID_W = 64
N_HEADS = 8
N_KV_HEADS = 2
HEAD_DIM = 64
ATTN_WIDTH = N_HEADS * HEAD_DIM
KV_WIDTH = N_KV_HEADS * HEAD_DIM
WINDOW = 128
ROPE_THETA = 10000.0
POOL_WINDOWS = (2, 4, 8, 16)
POOL_WIDTH = D_MODEL - ATTN_WIDTH
POOL_GROUP_DIM = POOL_WIDTH // len(POOL_WINDOWS)
IN_COLS = ATTN_WIDTH + 2 * KV_WIDTH + POOL_WIDTH
N_EXPERTS = 64
TOP_K = 8
N_EXPERT_GROUPS = 8
EXPERTS_PER_GROUP = N_EXPERTS // N_EXPERT_GROUPS
TOPK_GROUPS = 4
D_EXPERT = 256
D_SHARED = 256
ROUTED_SCALE = 2.5
EPS = 1e-6
LOG2E = 1.4426950408889634

LANES = 128
U32 = jnp.uint32
VMEM_LIMIT = 48 * 1024 * 1024

TM_PROJ = 1024
TQ = 1024
QB = 128
POOL_SLAB = 256
POOL_OFF = 64
TR = 512
TM_EXP = 1024
EXP_MIN_ROWS = 256
EXP_SLOTS = 4
TF = 512
TOKEN_GROUPS = 2
COMBINE_CHUNKS = 2
SC_CORES = 2
SC_WORKERS = 32
SC_TOKENS = 128


def _silu(x):
    return x * (1.0 / (1.0 + jnp.exp(-x)))


def _split_bf16(x):
    hi = x.astype(BF16)
    lo = (x - hi.astype(F32)).astype(BF16)
    return hi, lo


def _dot(a, b):
    return jnp.dot(a, b, preferred_element_type=F32)


def _pack_words(val):
    half = val.shape[1] // 2
    lo = lax.bitcast_convert_type(val[:, :half].astype(BF16).astype(F32), U32)
    hi = lax.bitcast_convert_type(val[:, half:].astype(BF16).astype(F32), U32)
    return lax.shift_right_logical(lo, jnp.uint32(16)) | hi


def _unpack_words(words):
    lo = lax.bitcast_convert_type(lax.shift_left(words, jnp.uint32(16)), F32)
    hi = lax.bitcast_convert_type(words & jnp.uint32(0xFFFF0000), F32)
    return jnp.concatenate([lo, hi], axis=1)


def _dot_nt(a, b):
    return lax.dot_general(a, b, (((1,), (1,)), ((), ())), preferred_element_type=F32)


def _ada_kernel(c_ref, w_ref, b_ref, o_ref):
    a_hi, a_lo = _split_bf16(_silu(c_ref[...]))
    w_hi, w_lo = _split_bf16(w_ref[...])
    o_ref[...] = _dot(a_hi, w_hi) + _dot(a_lo, w_hi) + _dot(a_hi, w_lo) + b_ref[...]


def _ada(c8, w_ada, b_ada):
    d = c8.shape[1]
    n = w_ada.shape[1]
    tn = 512
    return pl.pallas_call(
        _ada_kernel,
        out_shape=jax.ShapeDtypeStruct((8, n), F32),
        grid=(n // tn,),
        in_specs=[pl.BlockSpec((8, d), lambda j: (0, 0)),
                  pl.BlockSpec((d, tn), lambda j: (0, j)),
                  pl.BlockSpec((1, tn), lambda j: (0, j))],
        out_specs=pl.BlockSpec((8, tn), lambda j: (0, j)),
        compiler_params=pltpu.CompilerParams(vmem_limit_bytes=VMEM_LIMIT),
        name="ada",
    )(c8, w_ada, b_ada.reshape(1, n))


def _norm_mod(x, g, shift, scale):
    ms = jnp.mean(x * x, axis=-1, keepdims=True)
    return (x * lax.rsqrt(ms + EPS) * g) * (1.0 + scale) + shift


def _lane_variants(t):
    lane = lax.broadcasted_iota(jnp.int32, t.shape, 1)
    lo = lane < HEAD_DIM
    tr = pltpu.roll(t, HEAD_DIM, 1)
    zero = jnp.zeros_like(t)
    return (jnp.where(lo, t, zero), jnp.where(lo, zero, tr),
            jnp.where(lo, tr, zero), jnp.where(lo, zero, t))


def _store_variants(ref, t):
    for i, var in enumerate(_lane_variants(t)):
        ref[0, :, i * LANES:(i + 1) * LANES] = var.astype(BF16)


def _inproj_kernel(x_ref, mod_ref, g_ref, w_ref, cos_ref, sin_ref, q_ref, k_ref, v_ref, p_ref):
    h = _norm_mod(x_ref[0], g_ref[...], mod_ref[0, 0:1, :], mod_ref[0, 1:2, :])
    z = _dot(h.astype(BF16), w_ref[...])
    cos = cos_ref[...]
    sin = sin_ref[...]
    lane = lax.broadcasted_iota(jnp.int32, cos.shape, 1)
    first_half = (lane & 16) == 0

    def rope(zc):
        partner = jnp.where(first_half, pltpu.roll(zc, LANES - 16, 1), pltpu.roll(zc, 16, 1))
        return zc * cos + partner * sin

    scale = HEAD_DIM ** -0.5 * LOG2E
    for c in range(ATTN_WIDTH // LANES):
        q_ref[0, :, c * LANES:(c + 1) * LANES] = (rope(z[:, c * LANES:(c + 1) * LANES]) * scale).astype(BF16)
    _store_variants(k_ref, rope(z[:, ATTN_WIDTH:ATTN_WIDTH + KV_WIDTH]))
    _store_variants(v_ref, z[:, ATTN_WIDTH + KV_WIDTH:ATTN_WIDTH + 2 * KV_WIDTH])
    p_ref[0] = z[:, ATTN_WIDTH + 2 * KV_WIDTH:]


def _inproj(x, mod, g1, w_in_bf, cos_t, sin_t):
    b, s, d = x.shape
    tm = TM_PROJ
    return pl.pallas_call(
        _inproj_kernel,
        out_shape=(jax.ShapeDtypeStruct((b, s, ATTN_WIDTH), BF16),
                   jax.ShapeDtypeStruct((b, s, 4 * LANES), BF16),
                   jax.ShapeDtypeStruct((b, s, 4 * LANES), BF16),
                   jax.ShapeDtypeStruct((b, s, POOL_WIDTH), F32)),
        grid=(s // tm, b),
        in_specs=[pl.BlockSpec((1, tm, d), lambda n, bi: (bi, n, 0)),
                  pl.BlockSpec((1, 6, d), lambda n, bi: (bi, 0, 0)),
                  pl.BlockSpec((1, d), lambda n, bi: (0, 0)),
                  pl.BlockSpec((d, IN_COLS), lambda n, bi: (0, 0)),
                  pl.BlockSpec((tm, LANES), lambda n, bi: (n, 0)),
                  pl.BlockSpec((tm, LANES), lambda n, bi: (n, 0))],
        out_specs=(pl.BlockSpec((1, tm, ATTN_WIDTH), lambda n, bi: (bi, n, 0)),
                   pl.BlockSpec((1, tm, 4 * LANES), lambda n, bi: (bi, n, 0)),
                   pl.BlockSpec((1, tm, 4 * LANES), lambda n, bi: (bi, n, 0)),
                   pl.BlockSpec((1, tm, POOL_WIDTH), lambda n, bi: (bi, n, 0))),
        compiler_params=pltpu.CompilerParams(vmem_limit_bytes=VMEM_LIMIT),
        name="inproj",
    )(x, mod, g1, w_in_bf, cos_t, sin_t)


def _ctxproj_kernel(x_ref, mod_ref, g_ref, w_ref, k_ref, v_ref):
    h = _norm_mod(x_ref[0], g_ref[...], mod_ref[0, 0:1, :], mod_ref[0, 1:2, :])
    z = _dot(h.astype(BF16), w_ref[...])
    _store_variants(k_ref, z[:, :KV_WIDTH])
    _store_variants(v_ref, z[:, KV_WIDTH:])


def _ctxproj(ctx, mod_c, g1, w_kv_bf):
    b, c, d = ctx.shape
    return pl.pallas_call(
        _ctxproj_kernel,
        out_shape=(jax.ShapeDtypeStruct((b, c, 4 * LANES), BF16),
                   jax.ShapeDtypeStruct((b, c, 4 * LANES), BF16)),
        grid=(b,),
        in_specs=[pl.BlockSpec((1, c, d), lambda bi: (bi, 0, 0)),
                  pl.BlockSpec((1, 6, d), lambda bi: (0, 0, 0)),
                  pl.BlockSpec((1, d), lambda bi: (0, 0)),
                  pl.BlockSpec((d, 2 * KV_WIDTH), lambda bi: (0, 0))],
        out_specs=(pl.BlockSpec((1, c, 4 * LANES), lambda bi: (bi, 0, 0)),
                   pl.BlockSpec((1, c, 4 * LANES), lambda bi: (bi, 0, 0))),
        compiler_params=pltpu.CompilerParams(vmem_limit_bytes=VMEM_LIMIT),
        name="ctxproj",
    )(ctx, mod_c, g1, w_kv_bf)


def _fold(op, tiles):
    while len(tiles) > 1:
        tiles = [op(tiles[i], tiles[i + 1]) if i + 1 < len(tiles) else tiles[i] for i in range(0, len(tiles), 2)]
    return tiles[0]


def _stack_variants(t4, kv):
    return jnp.concatenate([t4[:, (2 * kv) * LANES:(2 * kv + 1) * LANES],
                            t4[:, (2 * kv + 1) * LANES:(2 * kv + 2) * LANES]], axis=0)


def _attn_kernel(seq_len, sink_ref, q_ref, k_ref, kp_ref, kn_ref, v_ref, vp_ref, vn_ref, kc_ref, vc_ref,
                 p_ref, pp_ref, pn_ref, x_ref, mod_ref, band_ref, poolw_ref, pscale_ref, wout_ref,
                 g2_ref, wrh_ref, wrl_ref, x1_ref, h2_ref, lg_ref, kwin, vwin, pext, mix, s_scr, p_scr, m_scr):
    n = pl.program_id(1)
    n_last = pl.num_programs(1) - 1

    kwin[0:QB, :] = kp_ref[0]
    kwin[QB:QB + TQ, :] = k_ref[0]
    kwin[QB + TQ:, :] = kn_ref[0]
    vwin[0:QB, :] = vp_ref[0]
    vwin[QB:QB + TQ, :] = v_ref[0]
    vwin[QB + TQ:, :] = vn_ref[0]

    pext[0:QB - 8, :] = jnp.zeros((QB - 8, POOL_WIDTH), F32)
    pext[QB - 8:QB, :] = jnp.where(n > 0, pp_ref[0], 0.0)
    pext[QB:QB + TQ, :] = p_ref[0]
    pext[QB + TQ:QB + TQ + 8, :] = jnp.where(n < n_last, pn_ref[0], 0.0)
    pext[QB + TQ + 8:, :] = jnp.zeros((QB - 8, POOL_WIDTH), F32)

    row = lax.broadcasted_iota(jnp.int32, (QB, 3 * QB), 0)
    col = lax.broadcasted_iota(jnp.int32, (QB, 3 * QB), 1)
    in_band = (col >= row) & (col <= row + 2 * WINDOW)
    tok = lax.broadcasted_iota(jnp.int32, (QB, 1), 0)
    kc = kc_ref[0]
    kc_rows = [_stack_variants(kc, kv) for kv in range(N_KV_HEADS)]
    vc_rows = [_stack_variants(vc_ref[0], kv) for kv in range(N_KV_HEADS)]

    def sub_block(j, carry):
        r0 = pl.multiple_of(j * QB, QB)
        qj = q_ref[0, pl.ds(r0, QB), :]
        kw = kwin[pl.ds(r0, 3 * QB), :]
        vw = vwin[pl.ds(r0, 3 * QB), :]
        kpos = col + (n * TQ + j * QB - QB)
        ok = in_band & (kpos >= 0) & (kpos < seq_len)
        bias = jnp.where(ok, 0.0, -jnp.inf)
        n_loc, n_ctx = 3 * QB, kc.shape[0]
        bias2 = jnp.concatenate([bias, bias], axis=1)
        k_rows = [_stack_variants(kw, kv) for kv in range(N_KV_HEADS)]
        v_rows = [_stack_variants(vw, kv) for kv in range(N_KV_HEADS)]
        group = N_HEADS // N_KV_HEADS

        def head_tiles(head):
            loc0 = (head % 2) * n_loc
            ctx0 = 2 * n_loc + (head % 2) * n_ctx
            return ([loc0 + i * LANES for i in range(n_loc // LANES)]
                    + [ctx0 + i * LANES for i in range(n_ctx // LANES)])

        for c in range(N_HEADS // 2):
            qc = qj[:, c * LANES:(c + 1) * LANES]
            s_scr[c, :, 0:2 * n_loc] = _dot_nt(qc, k_rows[2 * c // group]) + bias2
            s_scr[c, :, 2 * n_loc:] = _dot_nt(qc, kc_rows[2 * c // group])
        for head in range(N_HEADS):
            tiles = [s_scr[head // 2, :, st:st + LANES] for st in head_tiles(head)]
            row_max = jnp.max(_fold(jnp.maximum, tiles), axis=1, keepdims=True)
            m_scr[head] = jnp.broadcast_to(jnp.maximum(row_max, sink_ref[head] * LOG2E), (QB, LANES))
        for head in range(N_HEADS):
            m = m_scr[head]
            acc = None
            for st in head_tiles(head):
                p = jnp.exp2(s_scr[head // 2, :, st:st + LANES] - m)
                p_scr[head // 2, :, st:st + LANES] = p.astype(BF16)
                acc = p if acc is None else acc + p
            denom = (jnp.broadcast_to(jnp.sum(acc, axis=1, keepdims=True), (QB, LANES))
                     + jnp.exp2(sink_ref[head] * LOG2E - m))
            m_scr[head] = 1.0 / denom
        lane = lax.broadcasted_iota(jnp.int32, (QB, LANES), 1)
        for c in range(N_HEADS // 2):
            o = (_dot(p_scr[c, :, 0:2 * n_loc], v_rows[2 * c // group])
                 + _dot(p_scr[c, :, 2 * n_loc:], vc_rows[2 * c // group]))
            o = o * jnp.where(lane < HEAD_DIM, m_scr[2 * c], m_scr[2 * c + 1])
            mix[pl.ds(r0, QB), c * LANES:(c + 1) * LANES] = o.astype(BF16)

        slab = pext[pl.ds(pl.multiple_of(r0 + POOL_OFF, 8), POOL_SLAB), :]
        tpos = tok + (n * TQ + j * QB)
        for g, w in enumerate(POOL_WINDOWS):
            sg = slab[:, g * LANES:(g + 1) * LANES]
            hi, lo = _split_bf16(sg)
            band = band_ref[g]
            wsum = _dot(band, hi) + _dot(band, lo)
            cnt = (jnp.minimum(tpos - w // 2 + w, seq_len) - jnp.maximum(tpos - w // 2, 0)).astype(F32)
            dlt = wsum / cnt - sg[POOL_OFF:POOL_OFF + QB, :]
            y = _dot(dlt.astype(BF16), poolw_ref[g]) * pscale_ref[:, g * LANES:(g + 1) * LANES]
            mix[pl.ds(r0, QB), ATTN_WIDTH + g * LANES:ATTN_WIDTH + (g + 1) * LANES] = y.astype(BF16)
        return carry

    lax.fori_loop(0, TQ // QB, sub_block, 0)

    proj = _dot(mix[...], wout_ref[...])
    x1 = x_ref[0] + mod_ref[0, 2:3, :] * proj
    x1_ref[0] = x1
    h2 = _norm_mod(x1, g2_ref[...], mod_ref[0, 3:4, :], mod_ref[0, 4:5, :])
    h2_ref[...] = _pack_words(h2)
    h_hi, h_lo = _split_bf16(h2)
    wrh = wrh_ref[...]
    lg_ref[...] = _dot_nt(wrh, h_hi) + _dot_nt(wrh, h_lo) + _dot_nt(wrl_ref[...], h_hi)


def _attn(b0, b, sink, q, k4, v4, kc4, vc4, p, x, mod, band, poolw_bf, pscale, wout_bf, g2, wr_hi, wr_lo):
    _, s, d = x.shape
    c = kc4.shape[1]
    nt = s // TQ
    hb = TQ // QB
    pb = TQ // 8
    kv_main = pl.BlockSpec((1, TQ, 4 * LANES), lambda bi, n: (b0 + bi, n, 0))
    kv_prev = pl.BlockSpec((1, QB, 4 * LANES), lambda bi, n: (b0 + bi, jnp.maximum(n * hb - 1, 0), 0))
    kv_next = pl.BlockSpec((1, QB, 4 * LANES), lambda bi, n: (b0 + bi, jnp.minimum((n + 1) * hb, s // QB - 1), 0))
    const2 = lambda bi, n: (0, 0)
    const3 = lambda bi, n: (0, 0, 0)
    return pl.pallas_call(
        functools.partial(_attn_kernel, s),
        out_shape=(jax.ShapeDtypeStruct((b, s, d), F32),
                   jax.ShapeDtypeStruct((b * s, d // 2), U32),
                   jax.ShapeDtypeStruct((N_EXPERTS, b * s), F32)),
        grid=(b, nt),
        in_specs=[pl.BlockSpec(memory_space=pltpu.SMEM),
                  pl.BlockSpec((1, TQ, ATTN_WIDTH), lambda bi, n: (b0 + bi, n, 0)),
                  kv_main, kv_prev, kv_next, kv_main, kv_prev, kv_next,
                  pl.BlockSpec((1, c, 4 * LANES), lambda bi, n: (b0 + bi, 0, 0)),
                  pl.BlockSpec((1, c, 4 * LANES), lambda bi, n: (b0 + bi, 0, 0)),
                  pl.BlockSpec((1, TQ, POOL_WIDTH), lambda bi, n: (b0 + bi, n, 0)),
                  pl.BlockSpec((1, 8, POOL_WIDTH), lambda bi, n: (b0 + bi, jnp.maximum(n * pb - 1, 0), 0)),
                  pl.BlockSpec((1, 8, POOL_WIDTH),
                               lambda bi, n: (b0 + bi, jnp.minimum((n + 1) * pb, s // 8 - 1), 0)),
                  pl.BlockSpec((1, TQ, d), lambda bi, n: (b0 + bi, n, 0)),
                  pl.BlockSpec((1, 6, d), lambda bi, n: (b0 + bi, 0, 0)),
                  pl.BlockSpec((len(POOL_WINDOWS), QB, POOL_SLAB), const3),
                  pl.BlockSpec((len(POOL_WINDOWS), POOL_GROUP_DIM, POOL_GROUP_DIM), const3),
                  pl.BlockSpec((1, POOL_WIDTH), const2),
                  pl.BlockSpec((d, d), const2),
                  pl.BlockSpec((1, d), const2),
                  pl.BlockSpec((N_EXPERTS, d), const2),
                  pl.BlockSpec((N_EXPERTS, d), const2)],
        out_specs=(pl.BlockSpec((1, TQ, d), lambda bi, n: (bi, n, 0)),
                   pl.BlockSpec((TQ, d // 2), lambda bi, n: (bi * nt + n, 0)),
                   pl.BlockSpec((N_EXPERTS, TQ), lambda bi, n: (0, bi * nt + n))),
        scratch_shapes=[pltpu.VMEM((TQ + 2 * QB, 4 * LANES), BF16),
                        pltpu.VMEM((TQ + 2 * QB, 4 * LANES), BF16),
                        pltpu.VMEM((TQ + 2 * QB, POOL_WIDTH), F32),
                        pltpu.VMEM((TQ, d), BF16),
                        pltpu.VMEM((N_HEADS // 2, QB, 2 * (3 * QB + c)), F32),
                        pltpu.VMEM((N_HEADS // 2, QB, 2 * (3 * QB + c)), BF16),
                        pltpu.VMEM((N_HEADS, QB, LANES), F32)],
        compiler_params=pltpu.CompilerParams(vmem_limit_bytes=VMEM_LIMIT),
        name="attn",
    )(sink, q, k4, k4, k4, v4, v4, v4, kc4, vc4, p, p, p, x, mod, band, poolw_bf, pscale, wout_bf,
      g2, wr_hi, wr_lo)


def _first_argmax_rows(v, row_iota, n_rows):
    m = jnp.max(v, axis=0, keepdims=True)
    idx = jnp.min(jnp.where(v == m, row_iota, n_rows), axis=0, keepdims=True)
    return m, idx


def _route_kernel(lg_ref, bias_ref, tri_ref, idx_ref, gate_ref, rank_ref, cnt_ref, carry):
    i = pl.program_id(0)

    @pl.when(i == 0)
    def _():
        carry[...] = jnp.zeros_like(carry)

    scores = 1.0 / (1.0 + jnp.exp(-lg_ref[...]))
    biased = scores + bias_ref[...]
    e_iota = lax.broadcasted_iota(jnp.int32, scores.shape, 0).astype(F32)
    g_iota = lax.broadcasted_iota(jnp.int32, (EXPERTS_PER_GROUP, TR), 0).astype(F32)
    neg = -jnp.inf

    grp = []
    for g in range(N_EXPERT_GROUPS):
        blk = biased[g * EXPERTS_PER_GROUP:(g + 1) * EXPERTS_PER_GROUP, :]
        m1, i1 = _first_argmax_rows(blk, g_iota, float(EXPERTS_PER_GROUP))
        m2 = jnp.max(jnp.where(g_iota == i1, neg, blk), axis=0, keepdims=True)
        grp.append(m1 + m2)
    grp = jnp.concatenate(grp, axis=0)
    gg_iota = lax.broadcasted_iota(jnp.int32, grp.shape, 0).astype(F32)
    grp_sel = jnp.zeros(grp.shape, F32)
    for _ in range(TOPK_GROUPS):
        _, gi = _first_argmax_rows(grp, gg_iota, float(N_EXPERT_GROUPS))
        hit = gg_iota == gi
        grp_sel = jnp.where(hit, 1.0, grp_sel)
        grp = jnp.where(hit, neg, grp)
    allowed = jnp.concatenate(
        [jnp.broadcast_to(grp_sel[g:g + 1, :], (EXPERTS_PER_GROUP, TR)) for g in range(N_EXPERT_GROUPS)], axis=0)
    masked = jnp.where(allowed > 0.5, biased, neg)

    idxs, gates = [], []
    onehot = jnp.zeros(scores.shape, F32)
    for _ in range(TOP_K):
        _, ei = _first_argmax_rows(masked, e_iota, float(N_EXPERTS))
        hit = e_iota == ei
        idxs.append(ei)
        gates.append(jnp.sum(jnp.where(hit, scores, 0.0), axis=0, keepdims=True))
        onehot = jnp.where(hit, 1.0, onehot)
        masked = jnp.where(hit, neg, masked)
    idx = jnp.concatenate(idxs, axis=0)
    gate = jnp.concatenate(gates, axis=0)
    gate = gate / jnp.sum(gate, axis=0, keepdims=True) * ROUTED_SCALE

    before = _dot(onehot.astype(BF16), tri_ref[...]) + carry[:, 0:1]
    ranks = [jnp.sum(jnp.where(e_iota == idxs[k], before, 0.0), axis=0, keepdims=True) for k in range(TOP_K)]
    idx_ref[...] = idx.astype(jnp.int32)
    gate_ref[...] = gate
    rank_ref[...] = jnp.concatenate(ranks, axis=0).astype(jnp.int32)
    total = carry[...] + jnp.sum(onehot, axis=1, keepdims=True)
    carry[...] = total
    cnt_ref[...] = total


def _route(lg_t, bias, tri):
    e, t = lg_t.shape
    tok = pl.BlockSpec((TOP_K, TR), lambda i: (0, i))
    return pl.pallas_call(
        _route_kernel,
        out_shape=(jax.ShapeDtypeStruct((TOP_K, t), jnp.int32),
                   jax.ShapeDtypeStruct((TOP_K, t), F32),
                   jax.ShapeDtypeStruct((TOP_K, t), jnp.int32),
                   jax.ShapeDtypeStruct((e, LANES), F32)),
        grid=(t // TR,),
        in_specs=[pl.BlockSpec((e, TR), lambda i: (0, i)),
                  pl.BlockSpec((e, 1), lambda i: (0, 0)),
                  pl.BlockSpec((TR, TR), lambda i: (0, 0))],
        out_specs=(tok, tok, tok, pl.BlockSpec((e, LANES), lambda i: (0, 0))),
        scratch_shapes=[pltpu.VMEM((e, LANES), F32)],
        compiler_params=pltpu.CompilerParams(vmem_limit_bytes=VMEM_LIMIT),
        name="route",
    )(lg_t, bias, tri)


def _plan_kernel(n_blocks, size_ref, start_ref, expert_ref, valid_ref, nact_ref):
    def per_expert(e, first_block):
        size = size_ref[e]
        n_blk = (size + TM_EXP - 1) // TM_EXP
        start_ref[e] = first_block * TM_EXP

        def per_block(j, carry):
            expert_ref[first_block + j] = e
            valid_ref[first_block + j] = jnp.minimum(size - j * TM_EXP, TM_EXP)
            return carry

        lax.fori_loop(0, n_blk, per_block, 0)
        return first_block + n_blk

    n_active = lax.fori_loop(0, N_EXPERTS, per_expert, 0)
    nact_ref[0] = n_active

    def unused(i, carry):
        expert_ref[i] = N_EXPERTS - 1
        valid_ref[i] = 0
        return carry

    lax.fori_loop(n_active, n_blocks, unused, 0)


def _plan_blocks(sizes, n_blocks):
    smem = pl.BlockSpec(memory_space=pltpu.SMEM)
    return pl.pallas_call(
        functools.partial(_plan_kernel, n_blocks),
        out_shape=(jax.ShapeDtypeStruct((N_EXPERTS,), jnp.int32),
                   jax.ShapeDtypeStruct((n_blocks,), jnp.int32),
                   jax.ShapeDtypeStruct((n_blocks,), jnp.int32),
                   jax.ShapeDtypeStruct((1,), jnp.int32)),
        in_specs=[smem],
        out_specs=(smem, smem, smem, smem),
        name="plan_blocks",
    )(sizes)


def _dest_kernel(start_ref, idx_ref, rank_ref, dest_ref):
    idx = idx_ref[...]
    dest = rank_ref[...]
    for e in range(N_EXPERTS):
        dest = dest + jnp.where(idx == e, start_ref[e], 0)
    dest_ref[...] = dest


def _dest_rows(pad_start, idx_kt, rank_kt):
    n_k, t = idx_kt.shape
    tile = 4096
    blk = pl.BlockSpec((n_k, tile), lambda i: (0, i))
    return pl.pallas_call(
        _dest_kernel,
        out_shape=jax.ShapeDtypeStruct((n_k, t), jnp.int32),
        grid=(t // tile,),
        in_specs=[pl.BlockSpec(memory_space=pltpu.SMEM), blk, blk],
        out_specs=blk,
        name="dest_rows",
    )(pad_start, idx_kt, rank_kt)


def _sc_mesh():
    return plsc.VectorSubcoreMesh(core_axis_name="c", subcore_axis_name="s")


def _sc_token_base(steps, j):
    worker = lax.axis_index("s") * SC_CORES + lax.axis_index("c")
    return (worker * steps + j) * SC_TOKENS


def _sc_scatter(dest_kt, h2p, n_rows):
    t, width = h2p.shape
    steps = t // (SC_WORKERS * SC_TOKENS)

    @functools.partial(
        pl.kernel, mesh=_sc_mesh(),
        out_type=jax.ShapeDtypeStruct((n_rows, width), U32),
        scratch_types=[pltpu.VMEM((TOP_K, SC_TOKENS), jnp.int32),
                       pltpu.VMEM((SC_TOKENS, width), U32),
                       pltpu.SemaphoreType.DMA],
        name="sc_scatter",
    )
    def body(dest_hbm, h_hbm, xs_hbm, idx_v, rows_v, sem):
        @pl.loop(0, steps)
        def _(j):
            base = _sc_token_base(steps, j)
            pltpu.sync_copy(dest_hbm.at[:, pl.ds(base, SC_TOKENS)], idx_v)
            pltpu.sync_copy(h_hbm.at[pl.ds(base, SC_TOKENS)], rows_v)
            copies = [pltpu.async_copy(rows_v, xs_hbm.at[idx_v.at[k]], sem) for k in range(TOP_K)]
            for cp in copies:
                cp.wait()

    return body(dest_kt, h2p)


def _sc_gather(dest_kt, ys, token0, n_tokens):
    n_k = dest_kt.shape[0]
    width = ys.shape[1]
    steps = n_tokens // (SC_WORKERS * SC_TOKENS)

    half = SC_TOKENS // 2
    units = [(k, h) for k in range(n_k) for h in range(2)]
    n_buf = 2

    @functools.partial(
        pl.kernel, mesh=_sc_mesh(),
        out_type=jax.ShapeDtypeStruct((n_k, n_tokens, width), U32),
        scratch_types=[pltpu.VMEM((n_k, SC_TOKENS), jnp.int32),
                       pltpu.VMEM((n_buf, half, width), U32),
                       pltpu.SemaphoreType.DMA((n_buf,))],
        name="sc_gather",
    )
    def body(dest_hbm, ys_hbm, yk_hbm, idx_v, rows_v, sems):
        @pl.loop(0, steps)
        def _(j):
            base = _sc_token_base(steps, j)
            pltpu.sync_copy(dest_hbm.at[:, pl.ds(token0 + base, SC_TOKENS)], idx_v)

            def gather(u):
                k, h = units[u]
                slot = u % n_buf
                return pltpu.make_async_copy(ys_hbm.at[idx_v.at[k, pl.ds(h * half, half)]], rows_v.at[slot],
                                             sems.at[slot])

            ahead = n_buf - 1
            for u in range(ahead):
                gather(u).start()
            for u, (k, h) in enumerate(units):
                gather(u).wait()
                pltpu.sync_copy(rows_v.at[u % n_buf], yk_hbm.at[k, pl.ds(base + h * half, half)])
                if u + ahead < len(units):
                    gather(u + ahead).start()

    return body(dest_kt, ys)


def _experts_kernel(be_ref, valid_ref, nact_ref, xs_hbm, wg_ref, wu_ref, wd_ref, ys_ref, wg_bf, wu_bf, wd_bf,
                    xbuf, sems):
    i = pl.program_id(0)
    n_active = nact_ref[0]
    slot = lax.rem(i, EXP_SLOTS)

    def fetch(j):
        src = xs_hbm.at[pl.ds(pl.multiple_of(j * TM_EXP, TM_EXP), TM_EXP), :]
        s = lax.rem(j, EXP_SLOTS)
        return pltpu.make_async_copy(src, xbuf.at[s], sems.at[s])

    @pl.when(i == 0)
    def _():
        for j in range(EXP_SLOTS - 1):
            pl.when(j < n_active)(lambda j=j: fetch(j).start())

    @pl.when(i + (EXP_SLOTS - 1) < n_active)
    def _():
        fetch(i + (EXP_SLOTS - 1)).start(priority=1)

    prev = be_ref[jnp.maximum(i - 1, 0)]

    @pl.when((i == 0) | (be_ref[i] != prev))
    def _():
        wg_bf[...] = wg_ref[0].astype(BF16)
        wu_bf[...] = wu_ref[0].astype(BF16)
        wd_bf[...] = wd_ref[0].astype(BF16)

    @pl.when(i < n_active)
    def _():
        fetch(i).wait()

    valid = jnp.where(i < n_active, valid_ref[i], 0)

    def run(rows):
        row = lax.broadcasted_iota(jnp.int32, (rows, xbuf.shape[2]), 0)
        words = jnp.where(row < valid, xbuf[slot, 0:rows, :], jnp.uint32(0))
        xb = _unpack_words(words).astype(BF16)
        hid = _silu(_dot(xb, wg_bf[...])) * _dot(xb, wu_bf[...])
        ys_ref[0:rows, :] = _pack_words(_dot(hid.astype(BF16), wd_bf[...]))
        if rows < TM_EXP:
            ys_ref[rows:, :] = jnp.zeros((TM_EXP - rows, ys_ref.shape[1]), U32)

    for rows in range(EXP_MIN_ROWS, TM_EXP + 1, EXP_MIN_ROWS):
        pl.when((valid > rows - EXP_MIN_ROWS) & (valid <= rows))(functools.partial(run, rows))

    @pl.when(valid == 0)
    def _():
        ys_ref[...] = jnp.zeros_like(ys_ref)


def _experts(block_e, block_valid, n_active, xs, w_gate, w_up, w_down):
    n_rows, half = xs.shape
    d = w_gate.shape[1]
    nb = n_rows // TM_EXP
    grid_spec = pltpu.PrefetchScalarGridSpec(
        num_scalar_prefetch=3,
        grid=(nb,),
        in_specs=[pl.BlockSpec(memory_space=pl.ANY),
                  pl.BlockSpec((1, d, D_EXPERT), lambda i, be, bv, na: (be[i], 0, 0)),
                  pl.BlockSpec((1, d, D_EXPERT), lambda i, be, bv, na: (be[i], 0, 0)),
                  pl.BlockSpec((1, D_EXPERT, d), lambda i, be, bv, na: (be[i], 0, 0))],
        out_specs=pl.BlockSpec((TM_EXP, half), lambda i, be, bv, na: (i, 0)),
        scratch_shapes=[pltpu.VMEM((d, D_EXPERT), BF16), pltpu.VMEM((d, D_EXPERT), BF16),
                        pltpu.VMEM((D_EXPERT, d), BF16),
                        pltpu.VMEM((EXP_SLOTS, TM_EXP, half), U32),
                        pltpu.SemaphoreType.DMA((EXP_SLOTS,))],
    )
    return pl.pallas_call(
        _experts_kernel,
        out_shape=jax.ShapeDtypeStruct((n_rows, half), U32),
        grid_spec=grid_spec,
        compiler_params=pltpu.CompilerParams(vmem_limit_bytes=VMEM_LIMIT, dimension_semantics=("arbitrary",)),
        name="experts",
    )(block_e, block_valid, n_active, xs, w_gate, w_up, w_down)


def _combine_kernel(yk_ref, x1_ref, h2_ref, gate_ref, mod_ref, wsg_ref, wsu_ref, wsd_ref, fg_ref, *out_refs):
    out_ref = out_refs[-1]
    hb = _unpack_words(h2_ref[...]).astype(BF16)
    hid = _silu(_dot(hb, wsg_ref[...])) * _dot(hb, wsu_ref[...])
    ffn = _dot(hid.astype(BF16), wsd_ref[...])
    gate = gate_ref[...]
    for k in range(TOP_K):
        ffn = ffn + gate[:, k:k + 1] * _unpack_words(yk_ref[k])
    x2 = x1_ref[...] + mod_ref[0, 5:6, :] * ffn
    ms = jnp.mean(x2 * x2, axis=-1, keepdims=True)
    out_ref[...] = x2 * lax.rsqrt(ms + EPS) * fg_ref[...]


def _combine(yk, token0, out_token0, n_out_tokens, prev_out, x1, h2p, gate_tk, mod, wsg_bf, wsu_bf, wsd_bf, final_g,
             seq_len):
    t, d = n_out_tokens, x1.shape[1]
    tiles_per_seq = seq_len // TF
    tile0 = token0 // TF
    out_tile0 = out_token0 // TF
    in_tok = pl.BlockSpec((TF, d), lambda i: (tile0 + i, 0))
    tok = pl.BlockSpec((TF, d), lambda i: (out_tile0 + i, 0))
    in_specs = [pl.BlockSpec((TOP_K, TF, d // 2), lambda i: (0, i, 0)),
                in_tok, pl.BlockSpec((TF, d // 2), lambda i: (tile0 + i, 0)),
                pl.BlockSpec((TF, TOP_K), lambda i: (tile0 + i, 0)),
                pl.BlockSpec((1, 6, d), lambda i: ((out_tile0 + i) // tiles_per_seq, 0, 0)),
                pl.BlockSpec((d, D_SHARED), lambda i: (0, 0)),
                pl.BlockSpec((d, D_SHARED), lambda i: (0, 0)),
                pl.BlockSpec((D_SHARED, d), lambda i: (0, 0)),
                pl.BlockSpec((1, d), lambda i: (0, 0))]
    args = [yk, x1, h2p, gate_tk, mod, wsg_bf, wsu_bf, wsd_bf, final_g]
    aliases = {}
    if prev_out is not None:
        in_specs.append(pl.BlockSpec(memory_space=pl.ANY))
        args.append(prev_out)
        aliases = {len(args) - 1: 0}
    return pl.pallas_call(
        _combine_kernel,
        out_shape=jax.ShapeDtypeStruct((t, d), F32),
        grid=(yk.shape[1] // TF,),
        in_specs=in_specs,
        out_specs=tok,
        input_output_aliases=aliases,
        compiler_params=pltpu.CompilerParams(vmem_limit_bytes=VMEM_LIMIT),
        name="combine",
    )(*args)


def _rope_tables(n_tokens):
    n_rows = n_tokens // GRID_W
    n_freq = HEAD_DIM // 4
    inv_freq = ROPE_THETA ** (-jnp.arange(n_freq, dtype=F32) / n_freq)
    ang_r = jnp.arange(n_rows).astype(F32)[:, None] * inv_freq[None, :]
    ang_c = jnp.arange(GRID_W).astype(F32)[:, None] * inv_freq[None, :]

    def per_token(row_part, col_part):
        rows = jnp.broadcast_to(row_part[:, None, :], (n_rows, GRID_W, n_freq))
        cols = jnp.broadcast_to(col_part[None, :, :], (n_rows, GRID_W, n_freq))
        return rows.reshape(n_tokens, n_freq), cols.reshape(n_tokens, n_freq)

    cos_r, cos_c = per_token(jnp.cos(ang_r), jnp.cos(ang_c))
    sin_r, sin_c = per_token(jnp.sin(ang_r), jnp.sin(ang_c))
    cos = jnp.concatenate([cos_r, cos_r, cos_c, cos_c], axis=1)
    sin = jnp.concatenate([-sin_r, sin_r, -sin_c, sin_c], axis=1)
    reps = LANES // HEAD_DIM
    return jnp.tile(cos, (1, reps)), jnp.tile(sin, (1, reps))


def _pool_bands():
    i = jnp.arange(QB)[:, None]
    r = jnp.arange(POOL_SLAB)[None, :]
    return jnp.stack([((r >= i + POOL_OFF - w // 2) & (r < i + POOL_OFF + w // 2)).astype(BF16)
                      for w in POOL_WINDOWS])


def kernel(x, c, ctx, c_ctx, w_ada, b_ada, norm1_g, norm2_g, w_in, attn_sink, pool_w, pool_scale, w_out,
           w_router, router_bias, w_gate, w_up, w_down, ws_gate, ws_up, ws_down, final_g):
    b, s, d = x.shape
    t = b * s
    assert w_ada.shape[0] == 1 and d == D_MODEL and s % TQ == 0 and b + 1 <= 8

    c8 = jnp.zeros((8, d), F32).at[:b].set(c).at[b].set(c_ctx)
    mod = _ada(c8, w_ada[0], b_ada[0]).reshape(8, 6, d)
    g1 = norm1_g[0].reshape(1, d)
    g2 = norm2_g[0].reshape(1, d)
    w_in_bf = w_in[0].astype(BF16)
    cos_t, sin_t = _rope_tables(s)

    q, k4, v4, p = _inproj(x, mod, g1, w_in_bf, cos_t, sin_t)
    kc4, vc4 = _ctxproj(ctx, mod[b:b + 1], g1, w_in_bf[:, ATTN_WIDTH:ATTN_WIDTH + 2 * KV_WIDTH])

    wr_t = w_router[0].T
    wr_hi = wr_t.astype(BF16)
    wr_lo = (wr_t - wr_hi.astype(F32)).astype(BF16)
    attn_consts = (_pool_bands(), pool_w[0].astype(BF16), pool_scale[0].reshape(1, POOL_WIDTH),
                   w_out[0].astype(BF16), g2, wr_hi, wr_lo)
    tri = jnp.triu(jnp.ones((TR, TR), BF16), k=1)
    shared_w = (ws_gate[0].astype(BF16), ws_up[0].astype(BF16), ws_down[0].astype(BF16))

    nb = b // TOKEN_GROUPS
    tg = nb * s
    assert b % TOKEN_GROUPS == 0 and tg % (COMBINE_CHUNKS * SC_WORKERS * SC_TOKENS) == 0
    n_rows = -(-(tg * TOP_K + N_EXPERTS * (TM_EXP - 1)) // TM_EXP) * TM_EXP
    groups = []
    for g in range(TOKEN_GROUPS):
        x1, h2p, lg_t = _attn(g * nb, nb, attn_sink[0], q, k4, v4, kc4, vc4, p, x, mod, *attn_consts)
        idx_kt, gate_kt, rank_kt, counts = _route(lg_t, router_bias[0].reshape(N_EXPERTS, 1), tri)
        pad_start, block_e, block_valid, n_active = _plan_blocks(counts[:, 0].astype(jnp.int32), n_rows // TM_EXP)
        dest_kt = _dest_rows(pad_start, idx_kt, rank_kt)
        xs = _sc_scatter(dest_kt, h2p, n_rows)
        groups.append((x1.reshape(tg, d), h2p, gate_kt.T, dest_kt, xs, block_e, block_valid, n_active))

    out = None
    chunk = tg // COMBINE_CHUNKS
    for g, (x1, h2p, gate_tk, dest_kt, xs, block_e, block_valid, n_active) in enumerate(groups):
        ys = _experts(block_e, block_valid, n_active, xs, w_gate[0], w_up[0], w_down[0])
        for token0 in range(0, tg, chunk):
            out = _combine(_sc_gather(dest_kt, ys, token0, chunk), token0, g * tg + token0, t, out, x1, h2p,
                           gate_tk, mod, *shared_w, final_g.reshape(1, d), s)
    return out.reshape(b, s, d)
```

```python
import functools

import jax
import jax.numpy as jnp
from jax import lax
from jax.experimental import pallas as pl
from jax.experimental.pallas import tpu as pltpu
from jax.experimental.pallas import tpu_sc as plsc

F32 = jnp.float32
BF16 = jnp.bfloat16

D_MODEL = 1024
GRID_W = 64
N_HEADS = 8
N_KV_HEADS = 2
HEAD_DIM = 64
ATTN_WIDTH = N_HEADS * HEAD_DIM
KV_WIDTH = N_KV_HEADS * HEAD_DIM
WINDOW = 128
ROPE_THETA = 10000.0
POOL_WINDOWS = (2, 4, 8, 16)
POOL_WIDTH = D_MODEL - ATTN_WIDTH
POOL_GROUP_DIM = POOL_WIDTH // len(POOL_WINDOWS)
IN_COLS = ATTN_WIDTH + 2 * KV_WIDTH + POOL_WIDTH
N_EXPERTS = 64
TOP_K = 8
N_EXPERT_GROUPS = 8
EXPERTS_PER_GROUP = N_EXPERTS // N_EXPERT_GROUPS
TOPK_GROUPS = 4
D_EXPERT = 256
D_SHARED = 256
ROUTED_SCALE = 2.5
EPS = 1e-6
LOG2E = 1.4426950408889634

LANES = 128
U32 = jnp.uint32
VMEM_LIMIT = 48 * 1024 * 1024

TM_PROJ = 1024
TQ = 1024
QB = 128
POOL_SLAB = 256
POOL_OFF = 64
TR = 512
TM_EXP = 1024
EXP_MIN_ROWS = 128
EXP_SLOTS = 4
TF = 512
TOKEN_GROUPS = 2
COMBINE_CHUNKS = 2
SC_CORES = 2
SC_WORKERS = 32
SC_TOKENS = 128


def _silu(x):
    return x * (1.0 / (1.0 + jnp.exp(-x)))


def _split_bf16(x):
    hi = x.astype(BF16)
    lo = (x - hi.astype(F32)).astype(BF16)
    return hi, lo


def _dot(a, b):
    return jnp.dot(a, b, preferred_element_type=F32)


def _pack_words(val):
    half = val.shape[1] // 2
    lo = lax.bitcast_convert_type(val[:, :half].astype(BF16).astype(F32), U32)
    hi = lax.bitcast_convert_type(val[:, half:].astype(BF16).astype(F32), U32)
    return lax.shift_right_logical(lo, jnp.uint32(16)) | hi


def _unpack_words(words):
    lo = lax.bitcast_convert_type(lax.shift_left(words, jnp.uint32(16)), F32)
    hi = lax.bitcast_convert_type(words & jnp.uint32(0xFFFF0000), F32)
    return jnp.concatenate([lo, hi], axis=1)


def _dot_nt(a, b):
    return lax.dot_general(a, b, (((1,), (1,)), ((), ())), preferred_element_type=F32)


def _ada_kernel(c_ref, w_ref, b_ref, o_ref):
    a_hi, a_lo = _split_bf16(_silu(c_ref[...]))
    w_hi, w_lo = _split_bf16(w_ref[...])
    o_ref[...] = _dot(a_hi, w_hi) + _dot(a_lo, w_hi) + _dot(a_hi, w_lo) + b_ref[...]


def _ada(c8, w_ada, b_ada):
    d = c8.shape[1]
    n = w_ada.shape[1]
    tn = 512
    return pl.pallas_call(
        _ada_kernel,
        out_shape=jax.ShapeDtypeStruct((8, n), F32),
        grid=(n // tn,),
        in_specs=[pl.BlockSpec((8, d), lambda j: (0, 0)),
                  pl.BlockSpec((d, tn), lambda j: (0, j)),
                  pl.BlockSpec((1, tn), lambda j: (0, j))],
        out_specs=pl.BlockSpec((8, tn), lambda j: (0, j)),
        compiler_params=pltpu.CompilerParams(vmem_limit_bytes=VMEM_LIMIT),
        name="ada",
    )(c8, w_ada, b_ada.reshape(1, n))


def _norm_mod(x, g, shift, scale):
    ms = jnp.mean(x * x, axis=-1, keepdims=True)
    return (x * lax.rsqrt(ms + EPS) * g) * (1.0 + scale) + shift


def _lane_variants(t):
    lane = lax.broadcasted_iota(jnp.int32, t.shape, 1)
    lo = lane < HEAD_DIM
    tr = pltpu.roll(t, HEAD_DIM, 1)
    zero = jnp.zeros_like(t)
    return (jnp.where(lo, t, zero), jnp.where(lo, zero, tr),
            jnp.where(lo, tr, zero), jnp.where(lo, zero, t))


def _store_variants(ref, t):
    for i, var in enumerate(_lane_variants(t)):
        ref[0, :, i * LANES:(i + 1) * LANES] = var.astype(BF16)


def _inproj_kernel(x_ref, mod_ref, g_ref, w_ref, cos_ref, sin_ref, q_ref, k_ref, v_ref, p_ref):
    h = _norm_mod(x_ref[0], g_ref[...], mod_ref[0, 0:1, :], mod_ref[0, 1:2, :])
    z = _dot(h.astype(BF16), w_ref[...])
    cos = cos_ref[...]
    sin = sin_ref[...]
    lane = lax.broadcasted_iota(jnp.int32, cos.shape, 1)
    first_half = (lane & 16) == 0

    def rope(zc):
        partner = jnp.where(first_half, pltpu.roll(zc, LANES - 16, 1), pltpu.roll(zc, 16, 1))
        return zc * cos + partner * sin

    scale = HEAD_DIM ** -0.5 * LOG2E
    for c in range(ATTN_WIDTH // LANES):
        q_ref[0, :, c * LANES:(c + 1) * LANES] = (rope(z[:, c * LANES:(c + 1) * LANES]) * scale).astype(BF16)
    _store_variants(k_ref, rope(z[:, ATTN_WIDTH:ATTN_WIDTH + KV_WIDTH]))
    _store_variants(v_ref, z[:, ATTN_WIDTH + KV_WIDTH:ATTN_WIDTH + 2 * KV_WIDTH])
    p_ref[0] = z[:, ATTN_WIDTH + 2 * KV_WIDTH:]


def _inproj(x, mod, g1, w_in_bf, cos_t, sin_t):
    b, s, d = x.shape
    tm = TM_PROJ
    return pl.pallas_call(
        _inproj_kernel,
        out_shape=(jax.ShapeDtypeStruct((b, s, ATTN_WIDTH), BF16),
                   jax.ShapeDtypeStruct((b, s, 4 * LANES), BF16),
                   jax.ShapeDtypeStruct((b, s, 4 * LANES), BF16),
                   jax.ShapeDtypeStruct((b, s, POOL_WIDTH), F32)),
        grid=(s // tm, b),
        in_specs=[pl.BlockSpec((1, tm, d), lambda n, bi: (bi, n, 0)),
                  pl.BlockSpec((1, 6, d), lambda n, bi: (bi, 0, 0)),
                  pl.BlockSpec((1, d), lambda n, bi: (0, 0)),
                  pl.BlockSpec((d, IN_COLS), lambda n, bi: (0, 0)),
                  pl.BlockSpec((tm, LANES), lambda n, bi: (n, 0)),
                  pl.BlockSpec((tm, LANES), lambda n, bi: (n, 0))],
        out_specs=(pl.BlockSpec((1, tm, ATTN_WIDTH), lambda n, bi: (bi, n, 0)),
                   pl.BlockSpec((1, tm, 4 * LANES), lambda n, bi: (bi, n, 0)),
                   pl.BlockSpec((1, tm, 4 * LANES), lambda n, bi: (bi, n, 0)),
                   pl.BlockSpec((1, tm, POOL_WIDTH), lambda n, bi: (bi, n, 0))),
        compiler_params=pltpu.CompilerParams(vmem_limit_bytes=VMEM_LIMIT),
        name="inproj",
    )(x, mod, g1, w_in_bf, cos_t, sin_t)


def _ctxproj_kernel(x_ref, mod_ref, g_ref, w_ref, k_ref, v_ref):
    h = _norm_mod(x_ref[0], g_ref[...], mod_ref[0, 0:1, :], mod_ref[0, 1:2, :])
    z = _dot(h.astype(BF16), w_ref[...])
    _store_variants(k_ref, z[:, :KV_WIDTH])
    _store_variants(v_ref, z[:, KV_WIDTH:])


def _ctxproj(ctx, mod_c, g1, w_kv_bf):
    b, c, d = ctx.shape
    return pl.pallas_call(
        _ctxproj_kernel,
        out_shape=(jax.ShapeDtypeStruct((b, c, 4 * LANES), BF16),
                   jax.ShapeDtypeStruct((b, c, 4 * LANES), BF16)),
        grid=(b,),
        in_specs=[pl.BlockSpec((1, c, d), lambda bi: (bi, 0, 0)),
                  pl.BlockSpec((1, 6, d), lambda bi: (0, 0, 0)),
                  pl.BlockSpec((1, d), lambda bi: (0, 0)),
                  pl.BlockSpec((d, 2 * KV_WIDTH), lambda bi: (0, 0))],
        out_specs=(pl.BlockSpec((1, c, 4 * LANES), lambda bi: (bi, 0, 0)),
                   pl.BlockSpec((1, c, 4 * LANES), lambda bi: (bi, 0, 0))),
        compiler_params=pltpu.CompilerParams(vmem_limit_bytes=VMEM_LIMIT),
        name="ctxproj",
    )(ctx, mod_c, g1, w_kv_bf)


def _fold(op, tiles):
    while len(tiles) > 1:
        tiles = [op(tiles[i], tiles[i + 1]) if i + 1 < len(tiles) else tiles[i] for i in range(0, len(tiles), 2)]
    return tiles[0]


def _stack_variants(t4, kv):
    return jnp.concatenate([t4[:, (2 * kv) * LANES:(2 * kv + 1) * LANES],
                            t4[:, (2 * kv + 1) * LANES:(2 * kv + 2) * LANES]], axis=0)


def _attn_kernel(seq_len, sink_ref, q_ref, k_ref, kp_ref, kn_ref, v_ref, vp_ref, vn_ref, kc_ref, vc_ref,
                 p_ref, pp_ref, pn_ref, x_ref, mod_ref, band_ref, poolw_ref, pscale_ref, wout_ref,
                 g2_ref, wrh_ref, wrl_ref, x1_ref, h2_ref, lg_ref, kwin, vwin, pext, mix, s_scr, p_scr, m_scr):
    n = pl.program_id(1)
    n_last = pl.num_programs(1) - 1

    kwin[0:QB, :] = kp_ref[0]
    kwin[QB:QB + TQ, :] = k_ref[0]
    kwin[QB + TQ:, :] = kn_ref[0]
    vwin[0:QB, :] = vp_ref[0]
    vwin[QB:QB + TQ, :] = v_ref[0]
    vwin[QB + TQ:, :] = vn_ref[0]

    pext[0:QB - 8, :] = jnp.zeros((QB - 8, POOL_WIDTH), F32)
    pext[QB - 8:QB, :] = jnp.where(n > 0, pp_ref[0], 0.0)
    pext[QB:QB + TQ, :] = p_ref[0]
    pext[QB + TQ:QB + TQ + 8, :] = jnp.where(n < n_last, pn_ref[0], 0.0)
    pext[QB + TQ + 8:, :] = jnp.zeros((QB - 8, POOL_WIDTH), F32)

    row = lax.broadcasted_iota(jnp.int32, (QB, 3 * QB), 0)
    col = lax.broadcasted_iota(jnp.int32, (QB, 3 * QB), 1)
    in_band = (col >= row) & (col <= row + 2 * WINDOW)
    tok = lax.broadcasted_iota(jnp.int32, (QB, 1), 0)
    kc = kc_ref[0]
    kc_rows = [_stack_variants(kc, kv) for kv in range(N_KV_HEADS)]
    vc_rows = [_stack_variants(vc_ref[0], kv) for kv in range(N_KV_HEADS)]

    def sub_block(j, carry):
        r0 = pl.multiple_of(j * QB, QB)
        qj = q_ref[0, pl.ds(r0, QB), :]
        kw = kwin[pl.ds(r0, 3 * QB), :]
        vw = vwin[pl.ds(r0, 3 * QB), :]
        kpos = col + (n * TQ + j * QB - QB)
        ok = in_band & (kpos >= 0) & (kpos < seq_len)
        bias = jnp.where(ok, 0.0, -jnp.inf)
        n_loc, n_ctx = 3 * QB, kc.shape[0]
        bias2 = jnp.concatenate([bias, bias], axis=1)
        k_rows = [_stack_variants(kw, kv) for kv in range(N_KV_HEADS)]
        v_rows = [_stack_variants(vw, kv) for kv in range(N_KV_HEADS)]
        group = N_HEADS // N_KV_HEADS

        def head_tiles(head):
            loc0 = (head % 2) * n_loc
            ctx0 = 2 * n_loc + (head % 2) * n_ctx
            return ([loc0 + i * LANES for i in range(n_loc // LANES)]
                    + [ctx0 + i * LANES for i in range(n_ctx // LANES)])

        for c in range(N_HEADS // 2):
            qc = qj[:, c * LANES:(c + 1) * LANES]
            s_scr[c, :, 0:2 * n_loc] = _dot_nt(qc, k_rows[2 * c // group]) + bias2
            s_scr[c, :, 2 * n_loc:] = _dot_nt(qc, kc_rows[2 * c // group])
        for head in range(N_HEADS):
            tiles = [s_scr[head // 2, :, st:st + LANES] for st in head_tiles(head)]
            row_max = jnp.max(_fold(jnp.maximum, tiles), axis=1, keepdims=True)
            m_scr[head] = jnp.broadcast_to(jnp.maximum(row_max, sink_ref[head] * LOG2E), (QB, LANES))
        for head in range(N_HEADS):
            m = m_scr[head]
            acc = None
            for st in head_tiles(head):
                p = jnp.exp2(s_scr[head // 2, :, st:st + LANES] - m)
                p_scr[head // 2, :, st:st + LANES] = p.astype(BF16)
                acc = p if acc is None else acc + p
            denom = (jnp.broadcast_to(jnp.sum(acc, axis=1, keepdims=True), (QB, LANES))
                     + jnp.exp2(sink_ref[head] * LOG2E - m))
            m_scr[head] = 1.0 / denom
        lane = lax.broadcasted_iota(jnp.int32, (QB, LANES), 1)
        for c in range(N_HEADS // 2):
            o = (_dot(p_scr[c, :, 0:2 * n_loc], v_rows[2 * c // group])
                 + _dot(p_scr[c, :, 2 * n_loc:], vc_rows[2 * c // group]))
            o = o * jnp.where(lane < HEAD_DIM, m_scr[2 * c], m_scr[2 * c + 1])
            mix[pl.ds(r0, QB), c * LANES:(c + 1) * LANES] = o.astype(BF16)

        slab = pext[pl.ds(pl.multiple_of(r0 + POOL_OFF, 8), POOL_SLAB), :]
        tpos = tok + (n * TQ + j * QB)
        for g, w in enumerate(POOL_WINDOWS):
            sg = slab[:, g * LANES:(g + 1) * LANES]
            hi, lo = _split_bf16(sg)
            band = band_ref[g]
            wsum = _dot(band, hi) + _dot(band, lo)
            cnt = (jnp.minimum(tpos - w // 2 + w, seq_len) - jnp.maximum(tpos - w // 2, 0)).astype(F32)
            dlt = wsum / cnt - sg[POOL_OFF:POOL_OFF + QB, :]
            y = _dot(dlt.astype(BF16), poolw_ref[g]) * pscale_ref[:, g * LANES:(g + 1) * LANES]
            mix[pl.ds(r0, QB), ATTN_WIDTH + g * LANES:ATTN_WIDTH + (g + 1) * LANES] = y.astype(BF16)
        return carry

    lax.fori_loop(0, TQ // QB, sub_block, 0)

    proj = _dot(mix[...], wout_ref[...])
    x1 = x_ref[0] + mod_ref[0, 2:3, :] * proj
    x1_ref[0] = x1
    h2 = _norm_mod(x1, g2_ref[...], mod_ref[0, 3:4, :], mod_ref[0, 4:5, :])
    h2_ref[...] = _pack_words(h2)
    h_hi, h_lo = _split_bf16(h2)
    wrh = wrh_ref[...]
    lg_ref[...] = _dot_nt(wrh, h_hi) + _dot_nt(wrh, h_lo) + _dot_nt(wrl_ref[...], h_hi)


def _attn(b0, b, sink, q, k4, v4, kc4, vc4, p, x, mod, band, poolw_bf, pscale, wout_bf, g2, wr_hi, wr_lo):
    _, s, d = x.shape
    c = kc4.shape[1]
    nt = s // TQ
    hb = TQ // QB
    pb = TQ // 8
    kv_main = pl.BlockSpec((1, TQ, 4 * LANES), lambda bi, n: (b0 + bi, n, 0))
    kv_prev = pl.BlockSpec((1, QB, 4 * LANES), lambda bi, n: (b0 + bi, jnp.maximum(n * hb - 1, 0), 0))
    kv_next = pl.BlockSpec((1, QB, 4 * LANES), lambda bi, n: (b0 + bi, jnp.minimum((n + 1) * hb, s // QB - 1), 0))
    const2 = lambda bi, n: (0, 0)
    const3 = lambda bi, n: (0, 0, 0)
    return pl.pallas_call(
        functools.partial(_attn_kernel, s),
        out_shape=(jax.ShapeDtypeStruct((b, s, d), F32),
                   jax.ShapeDtypeStruct((b * s, d // 2), U32),
                   jax.ShapeDtypeStruct((N_EXPERTS, b * s), F32)),
        grid=(b, nt),
        in_specs=[pl.BlockSpec(memory_space=pltpu.SMEM),
                  pl.BlockSpec((1, TQ, ATTN_WIDTH), lambda bi, n: (b0 + bi, n, 0)),
                  kv_main, kv_prev, kv_next, kv_main, kv_prev, kv_next,
                  pl.BlockSpec((1, c, 4 * LANES), lambda bi, n: (b0 + bi, 0, 0)),
                  pl.BlockSpec((1, c, 4 * LANES), lambda bi, n: (b0 + bi, 0, 0)),
                  pl.BlockSpec((1, TQ, POOL_WIDTH), lambda bi, n: (b0 + bi, n, 0)),
                  pl.BlockSpec((1, 8, POOL_WIDTH), lambda bi, n: (b0 + bi, jnp.maximum(n * pb - 1, 0), 0)),
                  pl.BlockSpec((1, 8, POOL_WIDTH),
                               lambda bi, n: (b0 + bi, jnp.minimum((n + 1) * pb, s // 8 - 1), 0)),
                  pl.BlockSpec((1, TQ, d), lambda bi, n: (b0 + bi, n, 0)),
                  pl.BlockSpec((1, 6, d), lambda bi, n: (b0 + bi, 0, 0)),
                  pl.BlockSpec((len(POOL_WINDOWS), QB, POOL_SLAB), const3),
                  pl.BlockSpec((len(POOL_WINDOWS), POOL_GROUP_DIM, POOL_GROUP_DIM), const3),
                  pl.BlockSpec((1, POOL_WIDTH), const2),
                  pl.BlockSpec((d, d), const2),
                  pl.BlockSpec((1, d), const2),
                  pl.BlockSpec((N_EXPERTS, d), const2),
                  pl.BlockSpec((N_EXPERTS, d), const2)],
        out_specs=(pl.BlockSpec((1, TQ, d), lambda bi, n: (bi, n, 0)),
                   pl.BlockSpec((TQ, d // 2), lambda bi, n: (bi * nt + n, 0)),
                   pl.BlockSpec((N_EXPERTS, TQ), lambda bi, n: (0, bi * nt + n))),
        scratch_shapes=[pltpu.VMEM((TQ + 2 * QB, 4 * LANES), BF16),
                        pltpu.VMEM((TQ + 2 * QB, 4 * LANES), BF16),
                        pltpu.VMEM((TQ + 2 * QB, POOL_WIDTH), F32),
                        pltpu.VMEM((TQ, d), BF16),
                        pltpu.VMEM((N_HEADS // 2, QB, 2 * (3 * QB + c)), F32),
                        pltpu.VMEM((N_HEADS // 2, QB, 2 * (3 * QB + c)), BF16),
                        pltpu.VMEM((N_HEADS, QB, LANES), F32)],
        compiler_params=pltpu.CompilerParams(vmem_limit_bytes=VMEM_LIMIT),
        name="attn",
    )(sink, q, k4, k4, k4, v4, v4, v4, kc4, vc4, p, p, p, x, mod, band, poolw_bf, pscale, wout_bf,
      g2, wr_hi, wr_lo)


def _first_argmax_rows(v, row_iota, n_rows):
    m = jnp.max(v, axis=0, keepdims=True)
    idx = jnp.min(jnp.where(v == m, row_iota, n_rows), axis=0, keepdims=True)
    return m, idx


def _route_kernel(lg_ref, bias_ref, tri_ref, idx_ref, gate_ref, rank_ref, cnt_ref, carry):
    i = pl.program_id(0)

    @pl.when(i == 0)
    def _():
        carry[...] = jnp.zeros_like(carry)

    scores = 1.0 / (1.0 + jnp.exp(-lg_ref[...]))
    biased = scores + bias_ref[...]
    e_iota = lax.broadcasted_iota(jnp.int32, scores.shape, 0).astype(F32)
    g_iota = lax.broadcasted_iota(jnp.int32, (EXPERTS_PER_GROUP, TR), 0).astype(F32)
    neg = -jnp.inf

    grp = []
    for g in range(N_EXPERT_GROUPS):
        blk = biased[g * EXPERTS_PER_GROUP:(g + 1) * EXPERTS_PER_GROUP, :]
        m1, i1 = _first_argmax_rows(blk, g_iota, float(EXPERTS_PER_GROUP))
        m2 = jnp.max(jnp.where(g_iota == i1, neg, blk), axis=0, keepdims=True)
        grp.append(m1 + m2)
    grp = jnp.concatenate(grp, axis=0)
    gg_iota = lax.broadcasted_iota(jnp.int32, grp.shape, 0).astype(F32)
    grp_sel = jnp.zeros(grp.shape, F32)
    for _ in range(TOPK_GROUPS):
        _, gi = _first_argmax_rows(grp, gg_iota, float(N_EXPERT_GROUPS))
        hit = gg_iota == gi
        grp_sel = jnp.where(hit, 1.0, grp_sel)
        grp = jnp.where(hit, neg, grp)
    allowed = jnp.concatenate(
        [jnp.broadcast_to(grp_sel[g:g + 1, :], (EXPERTS_PER_GROUP, TR)) for g in range(N_EXPERT_GROUPS)], axis=0)
    masked = jnp.where(allowed > 0.5, biased, neg)

    idxs, gates = [], []
    onehot = jnp.zeros(scores.shape, F32)
    for _ in range(TOP_K):
        _, ei = _first_argmax_rows(masked, e_iota, float(N_EXPERTS))
        hit = e_iota == ei
        idxs.append(ei)
        gates.append(jnp.sum(jnp.where(hit, scores, 0.0), axis=0, keepdims=True))
        onehot = jnp.where(hit, 1.0, onehot)
        masked = jnp.where(hit, neg, masked)
    idx = jnp.concatenate(idxs, axis=0)
    gate = jnp.concatenate(gates, axis=0)
    gate = gate / jnp.sum(gate, axis=0, keepdims=True) * ROUTED_SCALE

    before = _dot(onehot.astype(BF16), tri_ref[...]) + carry[:, 0:1]
    ranks = [jnp.sum(jnp.where(e_iota == idxs[k], before, 0.0), axis=0, keepdims=True) for k in range(TOP_K)]
    idx_ref[...] = idx.astype(jnp.int32)
    gate_ref[...] = gate
    rank_ref[...] = jnp.concatenate(ranks, axis=0).astype(jnp.int32)
    total = carry[...] + jnp.sum(onehot, axis=1, keepdims=True)
    carry[...] = total
    cnt_ref[...] = total


def _route(lg_t, bias, tri):
    e, t = lg_t.shape
    tok = pl.BlockSpec((TOP_K, TR), lambda i: (0, i))
    return pl.pallas_call(
        _route_kernel,
        out_shape=(jax.ShapeDtypeStruct((TOP_K, t), jnp.int32),
                   jax.ShapeDtypeStruct((TOP_K, t), F32),
                   jax.ShapeDtypeStruct((TOP_K, t), jnp.int32),
                   jax.ShapeDtypeStruct((e, LANES), F32)),
        grid=(t // TR,),
        in_specs=[pl.BlockSpec((e, TR), lambda i: (0, i)),
                  pl.BlockSpec((e, 1), lambda i: (0, 0)),
                  pl.BlockSpec((TR, TR), lambda i: (0, 0))],
        out_specs=(tok, tok, tok, pl.BlockSpec((e, LANES), lambda i: (0, 0))),
        scratch_shapes=[pltpu.VMEM((e, LANES), F32)],
        compiler_params=pltpu.CompilerParams(vmem_limit_bytes=VMEM_LIMIT),
        name="route",
    )(lg_t, bias, tri)


def _plan_kernel(n_blocks, size_ref, start_ref, expert_ref, valid_ref, nact_ref):
    def per_expert(e, first_block):
        size = size_ref[e]
        n_blk = (size + TM_EXP - 1) // TM_EXP
        start_ref[e] = first_block * TM_EXP

        def per_block(j, carry):
            expert_ref[first_block + j] = e
            valid_ref[first_block + j] = jnp.minimum(size - j * TM_EXP, TM_EXP)
            return carry

        lax.fori_loop(0, n_blk, per_block, 0)
        return first_block + n_blk

    n_active = lax.fori_loop(0, N_EXPERTS, per_expert, 0)
    nact_ref[0] = n_active

    def unused(i, carry):
        expert_ref[i] = N_EXPERTS - 1
        valid_ref[i] = 0
        return carry

    lax.fori_loop(n_active, n_blocks, unused, 0)


def _plan_blocks(sizes, n_blocks):
    smem = pl.BlockSpec(memory_space=pltpu.SMEM)
    return pl.pallas_call(
        functools.partial(_plan_kernel, n_blocks),
        out_shape=(jax.ShapeDtypeStruct((N_EXPERTS,), jnp.int32),
                   jax.ShapeDtypeStruct((n_blocks,), jnp.int32),
                   jax.ShapeDtypeStruct((n_blocks,), jnp.int32),
                   jax.ShapeDtypeStruct((1,), jnp.int32)),
        in_specs=[smem],
        out_specs=(smem, smem, smem, smem),
        name="plan_blocks",
    )(sizes)


def _dest_kernel(start_ref, idx_ref, rank_ref, dest_ref):
    idx = idx_ref[...]
    dest = rank_ref[...]
    for e in range(N_EXPERTS):
        dest = dest + jnp.where(idx == e, start_ref[e], 0)
    dest_ref[...] = dest


def _dest_rows(pad_start, idx_kt, rank_kt):
    n_k, t = idx_kt.shape
    tile = 4096
    blk = pl.BlockSpec((n_k, tile), lambda i: (0, i))
    return pl.pallas_call(
        _dest_kernel,
        out_shape=jax.ShapeDtypeStruct((n_k, t), jnp.int32),
        grid=(t // tile,),
        in_specs=[pl.BlockSpec(memory_space=pltpu.SMEM), blk, blk],
        out_specs=blk,
        name="dest_rows",
    )(pad_start, idx_kt, rank_kt)


def _sc_mesh():
    return plsc.VectorSubcoreMesh(core_axis_name="c", subcore_axis_name="s")


def _sc_token_base(steps, j):
    worker = lax.axis_index("s") * SC_CORES + lax.axis_index("c")
    return (worker * steps + j) * SC_TOKENS


def _sc_scatter(dest_kt, h2p, n_rows):
    t, width = h2p.shape
    steps = t // (SC_WORKERS * SC_TOKENS)

    @functools.partial(
        pl.kernel, mesh=_sc_mesh(),
        out_type=jax.ShapeDtypeStruct((n_rows, width), U32),
        scratch_types=[pltpu.VMEM((TOP_K, SC_TOKENS), jnp.int32),
                       pltpu.VMEM((SC_TOKENS, width), U32),
                       pltpu.SemaphoreType.DMA],
        name="sc_scatter",
    )
    def body(dest_hbm, h_hbm, xs_hbm, idx_v, rows_v, sem):
        @pl.loop(0, steps)
        def _(j):
            base = _sc_token_base(steps, j)
            pltpu.sync_copy(dest_hbm.at[:, pl.ds(base, SC_TOKENS)], idx_v)
            pltpu.sync_copy(h_hbm.at[pl.ds(base, SC_TOKENS)], rows_v)
            copies = [pltpu.async_copy(rows_v, xs_hbm.at[idx_v.at[k]], sem) for k in range(TOP_K)]
            for cp in copies:
                cp.wait()

    return body(dest_kt, h2p)


def _sc_gather(dest_kt, ys, token0, n_tokens):
    n_k = dest_kt.shape[0]
    width = ys.shape[1]
    steps = n_tokens // (SC_WORKERS * SC_TOKENS)

    half = SC_TOKENS // 2
    units = [(k, h) for k in range(n_k) for h in range(2)]
    n_buf = 2

    @functools.partial(
        pl.kernel, mesh=_sc_mesh(),
        out_type=jax.ShapeDtypeStruct((n_k, n_tokens, width), U32),
        scratch_types=[pltpu.VMEM((n_k, SC_TOKENS), jnp.int32),
                       pltpu.VMEM((n_buf, half, width), U32),
                       pltpu.SemaphoreType.DMA((n_buf,))],
        name="sc_gather",
    )
    def body(dest_hbm, ys_hbm, yk_hbm, idx_v, rows_v, sems):
        @pl.loop(0, steps)
        def _(j):
            base = _sc_token_base(steps, j)
            pltpu.sync_copy(dest_hbm.at[:, pl.ds(token0 + base, SC_TOKENS)], idx_v)

            def gather(u):
                k, h = units[u]
                slot = u % n_buf
                return pltpu.make_async_copy(ys_hbm.at[idx_v.at[k, pl.ds(h * half, half)]], rows_v.at[slot],
                                             sems.at[slot])

            ahead = n_buf - 1
            for u in range(ahead):
                gather(u).start()
            for u, (k, h) in enumerate(units):
                gather(u).wait()
                pltpu.sync_copy(rows_v.at[u % n_buf], yk_hbm.at[k, pl.ds(base + h * half, half)])
                if u + ahead < len(units):
                    gather(u + ahead).start()

    return body(dest_kt, ys)


def _experts_kernel(be_ref, valid_ref, nact_ref, xs_hbm, wg_ref, wu_ref, wd_ref, ys_ref, wg_bf, wu_bf, wd_bf,
                    xbuf, sems):
    i = pl.program_id(0)
    n_active = nact_ref[0]
    slot = lax.rem(i, EXP_SLOTS)

    def fetch(j):
        src = xs_hbm.at[pl.ds(pl.multiple_of(j * TM_EXP, TM_EXP), TM_EXP), :]
        s = lax.rem(j, EXP_SLOTS)
        return pltpu.make_async_copy(src, xbuf.at[s], sems.at[s])

    @pl.when(i == 0)
    def _():
        for j in range(EXP_SLOTS - 1):
            pl.when(j < n_active)(lambda j=j: fetch(j).start())

    @pl.when(i + (EXP_SLOTS - 1) < n_active)
    def _():
        fetch(i + (EXP_SLOTS - 1)).start()

    prev = be_ref[jnp.maximum(i - 1, 0)]

    @pl.when((i == 0) | (be_ref[i] != prev))
    def _():
        wg_bf[...] = wg_ref[0].astype(BF16)
        wu_bf[...] = wu_ref[0].astype(BF16)
        wd_bf[...] = wd_ref[0].astype(BF16)

    @pl.when(i < n_active)
    def _():
        fetch(i).wait()

    valid = jnp.where(i < n_active, valid_ref[i], 0)

    def run(rows):
        row = lax.broadcasted_iota(jnp.int32, (rows, xbuf.shape[2]), 0)
        words = jnp.where(row < valid, xbuf[slot, 0:rows, :], jnp.uint32(0))
        xb = _unpack_words(words).astype(BF16)
        hid = _silu(_dot(xb, wg_bf[...])) * _dot(xb, wu_bf[...])
        ys_ref[0:rows, :] = _pack_words(_dot(hid.astype(BF16), wd_bf[...]))
        if rows < TM_EXP:
            ys_ref[rows:, :] = jnp.zeros((TM_EXP - rows, ys_ref.shape[1]), U32)

    for rows in range(EXP_MIN_ROWS, TM_EXP + 1, EXP_MIN_ROWS):
        pl.when((valid > rows - EXP_MIN_ROWS) & (valid <= rows))(functools.partial(run, rows))

    @pl.when(valid == 0)
    def _():
        ys_ref[...] = jnp.zeros_like(ys_ref)


def _experts(block_e, block_valid, n_active, xs, w_gate, w_up, w_down):
    n_rows, half = xs.shape
    d = w_gate.shape[1]
    nb = n_rows // TM_EXP
    grid_spec = pltpu.PrefetchScalarGridSpec(
        num_scalar_prefetch=3,
        grid=(nb,),
        in_specs=[pl.BlockSpec(memory_space=pl.ANY),
                  pl.BlockSpec((1, d, D_EXPERT), lambda i, be, bv, na: (be[i], 0, 0)),
                  pl.BlockSpec((1, d, D_EXPERT), lambda i, be, bv, na: (be[i], 0, 0)),
                  pl.BlockSpec((1, D_EXPERT, d), lambda i, be, bv, na: (be[i], 0, 0))],
        out_specs=pl.BlockSpec((TM_EXP, half), lambda i, be, bv, na: (i, 0)),
        scratch_shapes=[pltpu.VMEM((d, D_EXPERT), BF16), pltpu.VMEM((d, D_EXPERT), BF16),
                        pltpu.VMEM((D_EXPERT, d), BF16),
                        pltpu.VMEM((EXP_SLOTS, TM_EXP, half), U32),
                        pltpu.SemaphoreType.DMA((EXP_SLOTS,))],
    )
    return pl.pallas_call(
        _experts_kernel,
        out_shape=jax.ShapeDtypeStruct((n_rows, half), U32),
        grid_spec=grid_spec,
        compiler_params=pltpu.CompilerParams(vmem_limit_bytes=VMEM_LIMIT, dimension_semantics=("arbitrary",)),
        name="experts",
    )(block_e, block_valid, n_active, xs, w_gate, w_up, w_down)


def _combine_kernel(yk_ref, x1_ref, h2_ref, gate_ref, mod_ref, wsg_ref, wsu_ref, wsd_ref, fg_ref, *out_refs):
    out_ref = out_refs[-1]
    hb = _unpack_words(h2_ref[...]).astype(BF16)
    hid = _silu(_dot(hb, wsg_ref[...])) * _dot(hb, wsu_ref[...])
    ffn = _dot(hid.astype(BF16), wsd_ref[...])
    gate = gate_ref[...]
    for k in range(TOP_K):
        ffn = ffn + gate[:, k:k + 1] * _unpack_words(yk_ref[k])
    x2 = x1_ref[...] + mod_ref[0, 5:6, :] * ffn
    ms = jnp.mean(x2 * x2, axis=-1, keepdims=True)
    out_ref[...] = x2 * lax.rsqrt(ms + EPS) * fg_ref[...]


def _combine(yk, token0, out_token0, n_out_tokens, prev_out, x1, h2p, gate_tk, mod, wsg_bf, wsu_bf, wsd_bf, final_g,
             seq_len):
    t, d = n_out_tokens, x1.shape[1]
    tiles_per_seq = seq_len // TF
    tile0 = token0 // TF
    out_tile0 = out_token0 // TF
    in_tok = pl.BlockSpec((TF, d), lambda i: (tile0 + i, 0))
    tok = pl.BlockSpec((TF, d), lambda i: (out_tile0 + i, 0))
    in_specs = [pl.BlockSpec((TOP_K, TF, d // 2), lambda i: (0, i, 0)),
                in_tok, pl.BlockSpec((TF, d // 2), lambda i: (tile0 + i, 0)),
                pl.BlockSpec((TF, TOP_K), lambda i: (tile0 + i, 0)),
                pl.BlockSpec((1, 6, d), lambda i: ((out_tile0 + i) // tiles_per_seq, 0, 0)),
                pl.BlockSpec((d, D_SHARED), lambda i: (0, 0)),
                pl.BlockSpec((d, D_SHARED), lambda i: (0, 0)),
                pl.BlockSpec((D_SHARED, d), lambda i: (0, 0)),
                pl.BlockSpec((1, d), lambda i: (0, 0))]
    args = [yk, x1, h2p, gate_tk, mod, wsg_bf, wsu_bf, wsd_bf, final_g]
    aliases = {}
    if prev_out is not None:
        in_specs.append(pl.BlockSpec(memory_space=pl.ANY))
        args.append(prev_out)
        aliases = {len(args) - 1: 0}
    return pl.pallas_call(
        _combine_kernel,
        out_shape=jax.ShapeDtypeStruct((t, d), F32),
        grid=(yk.shape[1] // TF,),
        in_specs=in_specs,
        out_specs=tok,
        input_output_aliases=aliases,
        compiler_params=pltpu.CompilerParams(vmem_limit_bytes=VMEM_LIMIT),
        name="combine",
    )(*args)


def _rope_tables(n_tokens):
    n_rows = n_tokens // GRID_W
    n_freq = HEAD_DIM // 4
    inv_freq = ROPE_THETA ** (-jnp.arange(n_freq, dtype=F32) / n_freq)
    ang_r = jnp.arange(n_rows).astype(F32)[:, None] * inv_freq[None, :]
    ang_c = jnp.arange(GRID_W).astype(F32)[:, None] * inv_freq[None, :]

    def per_token(row_part, col_part):
        rows = jnp.broadcast_to(row_part[:, None, :], (n_rows, GRID_W, n_freq))
        cols = jnp.broadcast_to(col_part[None, :, :], (n_rows, GRID_W, n_freq))
        return rows.reshape(n_tokens, n_freq), cols.reshape(n_tokens, n_freq)

    cos_r, cos_c = per_token(jnp.cos(ang_r), jnp.cos(ang_c))
    sin_r, sin_c = per_token(jnp.sin(ang_r), jnp.sin(ang_c))
    cos = jnp.concatenate([cos_r, cos_r, cos_c, cos_c], axis=1)
    sin = jnp.concatenate([-sin_r, sin_r, -sin_c, sin_c], axis=1)
    reps = LANES // HEAD_DIM
    return jnp.tile(cos, (1, reps)), jnp.tile(sin, (1, reps))


def _pool_bands():
    i = jnp.arange(QB)[:, None]
    r = jnp.arange(POOL_SLAB)[None, :]
    return jnp.stack([((r >= i + POOL_OFF - w // 2) & (r < i + POOL_OFF + w // 2)).astype(BF16)
                      for w in POOL_WINDOWS])


def kernel(x, c, ctx, c_ctx, w_ada, b_ada, norm1_g, norm2_g, w_in, attn_sink, pool_w, pool_scale, w_out,
           w_router, router_bias, w_gate, w_up, w_down, ws_gate, ws_up, ws_down, final_g):
    b, s, d = x.shape
    t = b * s
    assert w_ada.shape[0] == 1 and d == D_MODEL and s % TQ == 0 and b + 1 <= 8

    c8 = jnp.zeros((8, d), F32).at[:b].set(c).at[b].set(c_ctx)
    mod = _ada(c8, w_ada[0], b_ada[0]).reshape(8, 6, d)
    g1 = norm1_g[0].reshape(1, d)
    g2 = norm2_g[0].reshape(1, d)
    w_in_bf = w_in[0].astype(BF16)
    cos_t, sin_t = _rope_tables(s)

    q, k4, v4, p = _inproj(x, mod, g1, w_in_bf, cos_t, sin_t)
    kc4, vc4 = _ctxproj(ctx, mod[b:b + 1], g1, w_in_bf[:, ATTN_WIDTH:ATTN_WIDTH + 2 * KV_WIDTH])

    wr_t = w_router[0].T
    wr_hi = wr_t.astype(BF16)
    wr_lo = (wr_t - wr_hi.astype(F32)).astype(BF16)
    attn_consts = (_pool_bands(), pool_w[0].astype(BF16), pool_scale[0].reshape(1, POOL_WIDTH),
                   w_out[0].astype(BF16), g2, wr_hi, wr_lo)
    tri = jnp.triu(jnp.ones((TR, TR), BF16), k=1)
    shared_w = (ws_gate[0].astype(BF16), ws_up[0].astype(BF16), ws_down[0].astype(BF16))

    nb = b // TOKEN_GROUPS
    tg = nb * s
    assert b % TOKEN_GROUPS == 0 and tg % (COMBINE_CHUNKS * SC_WORKERS * SC_TOKENS) == 0
    n_rows = -(-(tg * TOP_K + N_EXPERTS * (TM_EXP - 1)) // TM_EXP) * TM_EXP
    groups = []
    for g in range(TOKEN_GROUPS):
        x1, h2p, lg_t = _attn(g * nb, nb, attn_sink[0], q, k4, v4, kc4, vc4, p, x, mod, *attn_consts)
        idx_kt, gate_kt, rank_kt, counts = _route(lg_t, router_bias[0].reshape(N_EXPERTS, 1), tri)
        pad_start, block_e, block_valid, n_active = _plan_blocks(counts[:, 0].astype(jnp.int32), n_rows // TM_EXP)
        dest_kt = _dest_rows(pad_start, idx_kt, rank_kt)
        xs = _sc_scatter(dest_kt, h2p, n_rows)
        groups.append((x1.reshape(tg, d), h2p, gate_kt.T, dest_kt, xs, block_e, block_valid, n_active))

    out = None
    chunk = tg // COMBINE_CHUNKS
    for g, (x1, h2p, gate_tk, dest_kt, xs, block_e, block_valid, n_active) in enumerate(groups):
        ys = _experts(block_e, block_valid, n_active, xs, w_gate[0], w_up[0], w_down[0])
        for token0 in range(0, tg, chunk):
            out = _combine(_sc_gather(dest_kt, ys, token0, chunk), token0, g * tg + token0, t, out, x1, h2p,
                           gate_tk, mod, *shared_w, final_g.reshape(1, d), s)
    return out.reshape(b, s, d)
```
